```python
import jax, jax.numpy as jnp
from jax import lax
import numpy as np

D_MODEL = 1024
BATCH = 16
SEQ = 4096
DEPTH = 4

N_A_LAYERS = DEPTH // 2
CONV_WIDTH = 3
PATTERNS = ((128, 1), (512, 4), (2048, 16))
N_GROUPS = len(PATTERNS)
H_G = 8
HEAD_DIM = 64
D_FF = 4 * D_MODEL
EPS = 1e-5
ALIBI_MAX_BIAS = 8.0
NEG_INF = -1e30

kernel_name = 'yoco_shortconv_dilated_attn_trunk'


def rmsnorm(x, g):
    xf = x.astype(jnp.float32)
    y = xf * lax.rsqrt(jnp.mean(xf * xf, axis=-1, keepdims=True) + EPS)
    return (y * g.astype(jnp.float32)).astype(x.dtype)


def short_conv_mixer(h, w_in, conv_w, w_out):
    S = h.shape[1]
    b, c, u = jnp.split(h @ w_in, 3, axis=-1)
    up = jnp.pad(c * u, ((0, 0), (CONV_WIDTH - 1, 0), (0, 0)))
    conv = sum(conv_w[k] * up[:, CONV_WIDTH - 1 - k: CONV_WIDTH - 1 - k + S] for k in range(CONV_WIDTH))
    return (b * conv) @ w_out


def padded_len(S, window):
    return -(-S // window) * window


def to_blocks(t, dilation, blk, seq_pad):
    B, S = t.shape[:2]
    t = jnp.pad(t, [(0, 0), (0, seq_pad - S)] + [(0, 0)] * (t.ndim - 2))
    return t.reshape(B, seq_pad // (dilation * blk), blk, dilation, *t.shape[2:])


def kv_context(tb):
    prev = jnp.pad(tb, [(0, 0), (1, 0)] + [(0, 0)] * 4)[:, :-1]
    return jnp.concatenate([prev, tb], axis=2)


def dilated_branch(q, k_ctx, v_ctx, window, dilation, slopes):
    B, S, H, dh = q.shape
    blk = window // dilation
    seq_pad = padded_len(S, window)
    qb = to_blocks(q, dilation, blk, seq_pad)
    nb = qb.shape[1]
    s = jnp.einsum('bnqrhd,bnkrhd->bnrhqk', qb, k_ctx).astype(jnp.float32) * (dh ** -0.5)
    a = jnp.arange(blk)[:, None]
    c = jnp.arange(2 * blk)[None, :]
    j = blk + a - c
    n = jnp.arange(nb)[:, None, None]
    valid = (j >= 0) & (j <= blk) & ((n > 0) | (c >= blk))
    bias = -slopes[:, None, None] * (dilation * j).astype(jnp.float32)
    s = jnp.where(valid[None, :, None, None], s + bias, NEG_INF)
    lse = jax.nn.logsumexp(s, axis=-1)
    p = jnp.exp(s - lse[..., None]).astype(v_ctx.dtype)
    o = jnp.einsum('bnrhqk,bnkrhd->bnqrhd', p, v_ctx).reshape(B, seq_pad, H, dh)[:, :S]
    lse = jnp.transpose(lse, (0, 1, 4, 2, 3)).reshape(B, seq_pad, H)[:, :S]
    return o, lse


def _fwd_setup_inputs(seed: int = 0) -> dict:
    key = jax.random.key(seed)
    ks = jax.random.split(key, 14)
    n_a = N_A_LAYERS
    n_b = DEPTH - N_A_LAYERS
    qw = N_GROUPS * H_G * HEAD_DIM
    f32 = jnp.float32
    nrm = lambda k, shape: jax.random.normal(k, shape, f32)
    return {
        'x': nrm(ks[0], (BATCH, SEQ, D_MODEL)),
        'norm_mix': 1.0 + 0.05 * nrm(ks[1], (DEPTH, D_MODEL)),
        'norm_mlp': 1.0 + 0.05 * nrm(ks[2], (DEPTH, D_MODEL)),
        'w_a_in': nrm(ks[3], (n_a, D_MODEL, 3 * D_MODEL)) * D_MODEL ** -0.5,
        'conv_w': nrm(ks[4], (n_a, CONV_WIDTH, D_MODEL)) * CONV_WIDTH ** -0.5,
        'w_a_out': nrm(ks[5], (n_a, D_MODEL, D_MODEL)) * D_MODEL ** -0.5,
        'norm_kv': 1.0 + 0.05 * nrm(ks[6], (D_MODEL,)),
        'w_kv': nrm(ks[7], (D_MODEL, 2 * qw)) * D_MODEL ** -0.5,
        'w_q': nrm(ks[8], (n_b, D_MODEL, qw)) * D_MODEL ** -0.5,
        'w_o': nrm(ks[9], (n_b, H_G * HEAD_DIM, D_MODEL)) * (H_G * HEAD_DIM) ** -0.5,
        'w_up': nrm(ks[10], (DEPTH, D_MODEL, D_FF)) * D_MODEL ** -0.5,
        'w_down': nrm(ks[11], (DEPTH, D_FF, D_MODEL)) * (0.5 * D_FF ** -0.5),
        'norm_final': 1.0 + 0.05 * nrm(ks[12], (D_MODEL,)),
    }


def _fwd_reference(x, norm_mix, norm_mlp, w_a_in, conv_w, w_a_out, norm_kv, w_kv, w_q, w_o, w_up, w_down, norm_final):
    B, S, _ = x.shape
    slopes = 2.0 ** (-ALIBI_MAX_BIAS * jnp.arange(1, H_G + 1, dtype=jnp.float32) / H_G)
    h = x
    shared = []
    for l in range(DEPTH):
        if l < N_A_LAYERS:
            h = h + short_conv_mixer(rmsnorm(h, norm_mix[l]), w_a_in[l], conv_w[l], w_a_out[l])
        else:
            if l == N_A_LAYERS:
                kv = (rmsnorm(h, norm_kv) @ w_kv).reshape(B, S, N_GROUPS, 2, H_G, HEAD_DIM)
                for g, (window, dil) in enumerate(PATTERNS):
                    blk = window // dil
                    sp = padded_len(S, window)
                    shared.append((kv_context(to_blocks(kv[:, :, g, 0], dil, blk, sp)),
                                   kv_context(to_blocks(kv[:, :, g, 1], dil, blk, sp))))
            i = l - N_A_LAYERS
            q = (rmsnorm(h, norm_mix[l]) @ w_q[i]).reshape(B, S, N_GROUPS, H_G, HEAD_DIM)
            outs, lses = [], []
            for g, (window, dil) in enumerate(PATTERNS):
                o_g, lse_g = dilated_branch(q[:, :, g], shared[g][0], shared[g][1], window, dil, slopes)
                outs.append(o_g.astype(jnp.float32))
                lses.append(lse_g)
            wts = jax.nn.softmax(jnp.stack(lses), axis=0)
            o = jnp.sum(wts[..., None] * jnp.stack(outs), axis=0).astype(h.dtype)
            h = h + o.reshape(B, S, H_G * HEAD_DIM) @ w_o[i]
        hn = rmsnorm(h, norm_mlp[l])
        h = h + jnp.square(jax.nn.relu(hn @ w_up[l])) @ w_down[l]
    return rmsnorm(h, norm_final)


import jax as _jax
import jax.numpy as _jnp

TWIN_FORMAT = 'train_step'
FWD_PARAMS = ['x', 'norm_mix', 'norm_mlp', 'w_a_in', 'conv_w', 'w_a_out', 'norm_kv', 'w_kv', 'w_q', 'w_o', 'w_up', 'w_down', 'norm_final']
TWIN_WEIGHTS = ['norm_mix', 'norm_mlp', 'w_a_in', 'conv_w', 'w_a_out', 'norm_kv', 'w_kv', 'w_q', 'w_o', 'w_up', 'w_down', 'norm_final']
TWIN_DIFF_INPUT = 'x'
TWIN_INPUTS = ['x', 'norm_mix', 'norm_mlp', 'w_a_in', 'conv_w', 'w_a_out', 'norm_kv', 'w_kv', 'w_q', 'w_o', 'w_up', 'w_down', 'norm_final', 'loss_target', 'm_norm_mix', 'm_norm_mlp', 'm_w_a_in', 'm_conv_w', 'm_w_a_out', 'm_norm_kv', 'm_w_kv', 'm_w_q', 'm_w_o', 'm_w_up', 'm_w_down', 'm_norm_final', 'v_norm_mix', 'v_norm_mlp', 'v_w_a_in', 'v_conv_w', 'v_w_a_out', 'v_norm_kv', 'v_w_kv', 'v_w_q', 'v_w_o', 'v_w_up', 'v_w_down', 'v_norm_final']
TWIN_OUTPUTS = ['loss', 'grad_x', 'grad_norm_mix', 'grad_norm_mlp', 'grad_w_a_in', 'grad_conv_w', 'grad_w_a_out', 'grad_norm_kv', 'grad_w_kv', 'grad_w_q', 'grad_w_o', 'grad_w_up', 'grad_w_down', 'grad_norm_final', 'delta_norm_mix', 'delta_norm_mlp', 'delta_w_a_in', 'delta_conv_w', 'delta_w_a_out', 'delta_norm_kv', 'delta_w_kv', 'delta_w_q', 'delta_w_o', 'delta_w_up', 'delta_w_down', 'delta_norm_final', 'new_m_norm_mix', 'new_m_norm_mlp', 'new_m_w_a_in', 'new_m_conv_w', 'new_m_w_a_out', 'new_m_norm_kv', 'new_m_w_kv', 'new_m_w_q', 'new_m_w_o', 'new_m_w_up', 'new_m_w_down', 'new_m_norm_final', 'new_v_norm_mix', 'new_v_norm_mlp', 'new_v_w_a_in', 'new_v_conv_w', 'new_v_w_a_out', 'new_v_norm_kv', 'new_v_w_kv', 'new_v_w_q', 'new_v_w_o', 'new_v_w_up', 'new_v_w_down', 'new_v_norm_final']
TWIN_LEAF_KINDS = {'loss': 'loss', 'grad_x': 'grad_x', 'grad_norm_mix': 'grad_w', 'grad_norm_mlp': 'grad_w', 'grad_w_a_in': 'grad_w', 'grad_conv_w': 'grad_w', 'grad_w_a_out': 'grad_w', 'grad_norm_kv': 'grad_w', 'grad_w_kv': 'grad_w', 'grad_w_q': 'grad_w', 'grad_w_o': 'grad_w', 'grad_w_up': 'grad_w', 'grad_w_down': 'grad_w', 'grad_norm_final': 'grad_w', 'delta_norm_mix': 'delta_w', 'delta_norm_mlp': 'delta_w', 'delta_w_a_in': 'delta_w', 'delta_conv_w': 'delta_w', 'delta_w_a_out': 'delta_w', 'delta_norm_kv': 'delta_w', 'delta_w_kv': 'delta_w', 'delta_w_q': 'delta_w', 'delta_w_o': 'delta_w', 'delta_w_up': 'delta_w', 'delta_w_down': 'delta_w', 'delta_norm_final': 'delta_w', 'new_m_norm_mix': 'new_m', 'new_m_norm_mlp': 'new_m', 'new_m_w_a_in': 'new_m', 'new_m_conv_w': 'new_m', 'new_m_w_a_out': 'new_m', 'new_m_norm_kv': 'new_m', 'new_m_w_kv': 'new_m', 'new_m_w_q': 'new_m', 'new_m_w_o': 'new_m', 'new_m_w_up': 'new_m', 'new_m_w_down': 'new_m', 'new_m_norm_final': 'new_m', 'new_v_norm_mix': 'new_v', 'new_v_norm_mlp': 'new_v', 'new_v_w_a_in': 'new_v', 'new_v_conv_w': 'new_v', 'new_v_w_a_out': 'new_v', 'new_v_norm_kv': 'new_v', 'new_v_w_kv': 'new_v', 'new_v_w_q': 'new_v', 'new_v_w_o': 'new_v', 'new_v_w_up': 'new_v', 'new_v_w_down': 'new_v', 'new_v_norm_final': 'new_v'}


def _forward(args):
    return _fwd_reference(*[args[k] for k in FWD_PARAMS])


def _output_shape():
    out = _jax.eval_shape(lambda: _forward(_fwd_setup_inputs(0)))
    return out.shape, out.dtype

N_MICROBATCH = 1
ADAM_LR = 0.001
ADAM_B1 = 0.9
ADAM_B2 = 0.999
ADAM_EPS = 1e-08
ADAM_WD = 0.01
ADAM_STEP = 10
PER_EXAMPLE_BATCH_AXIS = {'x': 0, 'loss_target': 0}
SHARED_INPUTS = []
_WEIGHT_DTYPES = {'norm_mix': _jnp.float32, 'norm_mlp': _jnp.float32, 'w_a_in': _jnp.float32, 'conv_w': _jnp.float32, 'w_a_out': _jnp.float32, 'norm_kv': _jnp.float32, 'w_kv': _jnp.float32, 'w_q': _jnp.float32, 'w_o': _jnp.float32, 'w_up': _jnp.float32, 'w_down': _jnp.float32, 'norm_final': _jnp.float32}
MOMENT_SCALE = {'norm_mix': 2.542327e-01, 'norm_mlp': 1.118036e-01, 'w_a_in': 2.054206e-01, 'conv_w': 2.041331e-01, 'w_a_out': 2.049969e-01, 'norm_kv': 8.781998e-02, 'w_kv': 4.789919e-02, 'w_q': 2.527574e-02, 'w_o': 5.295020e-02, 'w_up': 5.611158e-02, 'w_down': 2.532401e-01, 'norm_final': 6.449810e+01}


def _to_microbatches(a, axis):
    t = _jnp.moveaxis(a, axis, 0)
    t = t.reshape((N_MICROBATCH, t.shape[0] // N_MICROBATCH) + t.shape[1:])
    return _jnp.moveaxis(t, 1, axis + 1)


def setup_inputs(seed: int = 0) -> dict:
    inp = _fwd_setup_inputs(seed)
    key = _jax.random.fold_in(_jax.random.key(seed), 7919)
    shape, _ = _output_shape()
    out = dict(inp)
    out["loss_target"] = _jax.random.normal(_jax.random.fold_in(key, 0), shape, _jnp.float32)
    for i, name in enumerate(TWIN_WEIGHTS):
        w = inp[name].astype(_jnp.float32)
        if MOMENT_SCALE is None:
            s = _jnp.sqrt(_jnp.mean(_jnp.square(w)) + 1e-30)
        else:
            s = MOMENT_SCALE[name]
        km, kv = _jax.random.split(_jax.random.fold_in(key, i + 1))
        out[name] = w
        out["m_" + name] = s * _jax.random.normal(km, w.shape, _jnp.float32)
        out["v_" + name] = (s * s) * _jax.random.uniform(kv, w.shape, _jnp.float32, 0.5, 1.5)
    if N_MICROBATCH > 1:
        for name, axis in PER_EXAMPLE_BATCH_AXIS.items():
            out[name] = _to_microbatches(out[name], axis)
    return {'x': out['x'], 'norm_mix': out['norm_mix'], 'norm_mlp': out['norm_mlp'], 'w_a_in': out['w_a_in'], 'conv_w': out['conv_w'], 'w_a_out': out['w_a_out'], 'norm_kv': out['norm_kv'], 'w_kv': out['w_kv'], 'w_q': out['w_q'], 'w_o': out['w_o'], 'w_up': out['w_up'], 'w_down': out['w_down'], 'norm_final': out['norm_final'], 'loss_target': out['loss_target'], 'm_norm_mix': out['m_norm_mix'], 'm_norm_mlp': out['m_norm_mlp'], 'm_w_a_in': out['m_w_a_in'], 'm_conv_w': out['m_conv_w'], 'm_w_a_out': out['m_w_a_out'], 'm_norm_kv': out['m_norm_kv'], 'm_w_kv': out['m_w_kv'], 'm_w_q': out['m_w_q'], 'm_w_o': out['m_w_o'], 'm_w_up': out['m_w_up'], 'm_w_down': out['m_w_down'], 'm_norm_final': out['m_norm_final'], 'v_norm_mix': out['v_norm_mix'], 'v_norm_mlp': out['v_norm_mlp'], 'v_w_a_in': out['v_w_a_in'], 'v_conv_w': out['v_conv_w'], 'v_w_a_out': out['v_w_a_out'], 'v_norm_kv': out['v_norm_kv'], 'v_w_kv': out['v_w_kv'], 'v_w_q': out['v_w_q'], 'v_w_o': out['v_w_o'], 'v_w_up': out['v_w_up'], 'v_w_down': out['v_w_down'], 'v_norm_final': out['v_norm_final']}


def _loss(weights, diff, rest, loss_target):
    with _jax.named_scope("forward"):
        args = {**rest, TWIN_DIFF_INPUT: diff, **{k: w.astype(_WEIGHT_DTYPES[k]) for k, w in weights.items()}}
        y = _forward(args)
    with _jax.named_scope("loss_head"):
        err = _jnp.square(y.astype(_jnp.float32) - loss_target)
        return 0.5 * _jnp.sum(_jnp.mean(err, axis=-1)) if err.ndim else 0.5 * err


def _adamw(w, g, m, v):
    m = ADAM_B1 * m + (1.0 - ADAM_B1) * g
    v = ADAM_B2 * v + (1.0 - ADAM_B2) * _jnp.square(g)
    m_hat = m / (1.0 - ADAM_B1 ** ADAM_STEP)
    v_hat = v / (1.0 - ADAM_B2 ** ADAM_STEP)
    delta = -ADAM_LR * (m_hat / (_jnp.sqrt(v_hat) + ADAM_EPS) + ADAM_WD * w)
    return delta, m, v


def reference(x, norm_mix, norm_mlp, w_a_in, conv_w, w_a_out, norm_kv, w_kv, w_q, w_o, w_up, w_down, norm_final, loss_target, m_norm_mix, m_norm_mlp, m_w_a_in, m_conv_w, m_w_a_out, m_norm_kv, m_w_kv, m_w_q, m_w_o, m_w_up, m_w_down, m_norm_final, v_norm_mix, v_norm_mlp, v_w_a_in, v_conv_w, v_w_a_out, v_norm_kv, v_w_kv, v_w_q, v_w_o, v_w_up, v_w_down, v_norm_final):
    given = dict(x=x, norm_mix=norm_mix, norm_mlp=norm_mlp, w_a_in=w_a_in, conv_w=conv_w, w_a_out=w_a_out, norm_kv=norm_kv, w_kv=w_kv, w_q=w_q, w_o=w_o, w_up=w_up, w_down=w_down, norm_final=norm_final, loss_target=loss_target, m_norm_mix=m_norm_mix, m_norm_mlp=m_norm_mlp, m_w_a_in=m_w_a_in, m_conv_w=m_conv_w, m_w_a_out=m_w_a_out, m_norm_kv=m_norm_kv, m_w_kv=m_w_kv, m_w_q=m_w_q, m_w_o=m_w_o, m_w_up=m_w_up, m_w_down=m_w_down, m_norm_final=m_norm_final, v_norm_mix=v_norm_mix, v_norm_mlp=v_norm_mlp, v_w_a_in=v_w_a_in, v_conv_w=v_conv_w, v_w_a_out=v_w_a_out, v_norm_kv=v_norm_kv, v_w_kv=v_w_kv, v_w_q=v_w_q, v_w_o=v_w_o, v_w_up=v_w_up, v_w_down=v_w_down, v_norm_final=v_norm_final)
    weights = {n: given[n] for n in TWIN_WEIGHTS}
    shared = {n: given[n] for n in SHARED_INPUTS}
    per_example = {n: given[n] for n in ['x']}
    grad_fn = _jax.value_and_grad(_loss, argnums=(0, 1))

    def one_microbatch(ex, loss_target):
        ex = dict(ex)
        diff = ex.pop(TWIN_DIFF_INPUT)
        return grad_fn(weights, diff, {**shared, **ex}, loss_target)

    if N_MICROBATCH == 1:
        loss, (grad_w, grad_x) = one_microbatch(per_example, given["loss_target"])
    else:
        def body(carry, xs):
            loss_sum, grad_sum = carry
            l_k, (gw_k, gx_k) = one_microbatch(xs[0], xs[1])
            with _jax.named_scope("update"):
                return (loss_sum + l_k, _jax.tree.map(_jnp.add, grad_sum, gw_k)), gx_k

        init = (_jnp.zeros((), _jnp.float32), _jax.tree.map(_jnp.zeros_like, weights))
        (loss, grad_w), grad_x = _jax.lax.scan(body, init, (per_example, given["loss_target"]))
    with _jax.named_scope("update"):
        delta_w, new_m, new_v = {}, {}, {}
        for n in TWIN_WEIGHTS:
            delta_w[n], new_m[n], new_v[n] = _adamw(weights[n], grad_w[n], given["m_" + n], given["v_" + n])
    return (loss, grad_x, *[grad_w[n] for n in TWIN_WEIGHTS], *[delta_w[n] for n in TWIN_WEIGHTS],
            *[new_m[n] for n in TWIN_WEIGHTS], *[new_v[n] for n in TWIN_WEIGHTS])
```

```python
import functools

import jax
import jax.numpy as jnp
from jax import lax
from jax.experimental import pallas as pl
from jax.experimental.pallas import tpu as pltpu

F32 = jnp.float32
BF16 = jnp.bfloat16
MESH = pl.DeviceIdType.MESH

EPS = 1e-5
PATTERNS = ((128, 1), (512, 4), (2048, 16))
HEAD_DIM = 64
ALIBI_MAX_BIAS = 8.0
NEG_INF = -1e30
ATT_BLK = 128
N_CHIPS = 4
LANES = 128
VMEM_LIMIT = 56 * 1024 * 1024

ADAM_LR = 0.001
ADAM_B1 = 0.9
ADAM_B2 = 0.999
ADAM_EPS = 1e-08
ADAM_WD = 0.01
ADAM_STEP = 10


def _params(n_grid_axes):
    return pltpu.CompilerParams(dimension_semantics=("arbitrary",) * n_grid_axes, vmem_limit_bytes=VMEM_LIMIT)


def _dot(a, b):
    return jnp.dot(a, b, preferred_element_type=F32)


def _dot_nt(a, b):
    return lax.dot_general(a, b, (((1,), (1,)), ((), ())), preferred_element_type=F32)


def _dot_tn(a, b):
    return lax.dot_general(a, b, (((0,), (0,)), ((), ())), preferred_element_type=F32)


def _relu2(a):
    return jnp.square(jnp.maximum(a, 0.0))


def _identity(a):
    return a


def _rms(hf, g):
    y = hf * lax.rsqrt(jnp.mean(hf * hf, axis=-1, keepdims=True) + EPS)
    return y * g


def _rms_bwd(hf, g, dn):
    rstd = lax.rsqrt(jnp.mean(hf * hf, axis=-1, keepdims=True) + EPS)
    xhat = hf * rstd
    dg = jnp.sum(dn * xhat, axis=0, keepdims=True)
    dx = dn * g
    dh = rstd * (dx - xhat * jnp.mean(dx * xhat, axis=-1, keepdims=True))
    return dh, dg


def _pieces(seg_widths, chunk_width, max_width):
    total = sum(seg_widths)
    cuts = {0, total}
    acc = 0
    for w in seg_widths:
        cuts.add(acc)
        acc += w
    cuts.update(range(0, total, chunk_width))
    cuts = sorted(cuts)
    fine = []
    for lo, hi in zip(cuts[:-1], cuts[1:]):
        while hi - lo > max_width:
            fine.append((lo, lo + max_width))
            lo += max_width
        fine.append((lo, hi))
    out = []
    for lo, hi in fine:
        acc = 0
        for s, w in enumerate(seg_widths):
            if lo < acc + w:
                break
            acc += w
        out.append((s, lo - acc, lo // chunk_width, lo % chunk_width, hi - lo))
    return out


def _norm_mm(name, h, g, wg, layer, planes, out_dtype, tm, tn):
    T, D = h.shape
    cw = wg.shape[3]
    N = N_CHIPS * cw
    pw = N // planes
    per_chunk = cw // tn
    per_plane = pw // tn

    def body(h_ref, g_ref, w_ref, n_ref, o_ref):
        @pl.when(pl.program_id(1) == 0)
        def _():
            n_ref[...] = _rms(h_ref[...], g_ref[...]).astype(BF16)

        o_ref[...] = _dot(n_ref[...], w_ref[...]).astype(out_dtype)

    return pl.pallas_call(
        body, name=name, grid=(T // tm, N // tn),
        in_specs=[pl.BlockSpec((tm, D), lambda i, j: (i, 0)),
                  pl.BlockSpec((1, D), lambda i, j: (0, 0)),
                  pl.BlockSpec((None, None, D, tn), lambda i, j: (j // per_chunk, layer, 0, j % per_chunk))],
        out_specs=[pl.BlockSpec((tm, D), lambda i, j: (i, 0)),
                   pl.BlockSpec((None, tm, tn), lambda i, j: (j // per_plane, i, j % per_plane))],
        out_shape=[jax.ShapeDtypeStruct((T, D), BF16), jax.ShapeDtypeStruct((planes, T, pw), out_dtype)],
        compiler_params=_params(2))(h, g, wg)


def _mm_res_rows(name, a, wg, layer, h, act, tm):
    T = a.shape[0]
    rk, D = wg.shape[2], wg.shape[3]

    def body(a_ref, w_ref, h_ref, o_ref):
        d = _dot(act(a_ref[...]).astype(BF16), w_ref[...])
        k = pl.program_id(1)

        @pl.when(k == 0)
        def _():
            o_ref[...] = h_ref[...] + d

        @pl.when(k > 0)
        def _():
            o_ref[...] += d

    return pl.pallas_call(
        body, name=name, grid=(T // tm, N_CHIPS),
        in_specs=[pl.BlockSpec((tm, rk), lambda i, k: (i, k)),
                  pl.BlockSpec((None, None, rk, D), lambda i, k: (k, layer, 0, 0)),
                  pl.BlockSpec((tm, D), lambda i, k: (i, 0))],
        out_specs=pl.BlockSpec((tm, D), lambda i, k: (i, 0)),
        out_shape=jax.ShapeDtypeStruct((T, D), F32),
        compiler_params=_params(2))(a, wg, h)


def _mm_res_cols(name, a, wg, layer, h, tm):
    T, K = a.shape
    cw = wg.shape[3]
    D = N_CHIPS * cw

    def body(a_ref, w_ref, h_ref, o_ref):
        o_ref[...] = h_ref[...] + _dot(a_ref[...].astype(BF16), w_ref[...])

    return pl.pallas_call(
        body, name=name, grid=(T // tm, N_CHIPS),
        in_specs=[pl.BlockSpec((tm, K), lambda i, j: (i, 0)),
                  pl.BlockSpec((None, None, K, cw), lambda i, j: (j, layer, 0, 0)),
                  pl.BlockSpec((tm, cw), lambda i, j: (i, j))],
        out_specs=pl.BlockSpec((tm, cw), lambda i, j: (i, j)),
        out_shape=jax.ShapeDtypeStruct((T, D), F32),
        compiler_params=_params(2))(a, wg, h)


CONV_ROWS = 256
CONV_HALO = 8


def _conv_shifted(ext, k, r0, rows):
    rolled = pltpu.roll(ext, k, 0)[CONV_HALO:]
    t = r0 + lax.broadcasted_iota(jnp.int32, rolled.shape, 0)
    return jnp.where(t >= k, rolled, 0.0)


def _conv_ahead(ext, k, r0, rows, S):
    rolled = pltpu.roll(ext, rows + CONV_HALO - k, 0)[:rows]
    t = r0 + lax.broadcasted_iota(jnp.int32, rolled.shape, 0)
    return jnp.where(t + k < S, rolled, 0.0)


def _conv_fwd(name, bcu, cwg, layer, tc):
    _, B, S, D = bcu.shape
    cwc = cwg.shape[3]
    per_chunk = cwc // tc
    R = min(CONV_ROWS, S)

    def body(x_ref, w_ref, z_ref):
        w = [w_ref[k:k + 1, :] for k in range(3)]

        def step(i, carry):
            r0 = pl.multiple_of(i * R, R)
            h0 = pl.multiple_of(jnp.maximum(r0 - CONV_HALO, 0), CONV_HALO)
            cu = jnp.concatenate([x_ref[1, pl.ds(h0, CONV_HALO), :] * x_ref[2, pl.ds(h0, CONV_HALO), :],
                                  x_ref[1, pl.ds(r0, R), :] * x_ref[2, pl.ds(r0, R), :]], axis=0)
            conv = w[0] * cu[CONV_HALO:]
            conv = conv + w[1] * _conv_shifted(cu, 1, r0, R)
            conv = conv + w[2] * _conv_shifted(cu, 2, r0, R)
            z_ref[pl.ds(r0, R), :] = (x_ref[0, pl.ds(r0, R), :] * conv).astype(BF16)
            return carry

        lax.fori_loop(0, S // R, step, 0)

    return pl.pallas_call(
        body, name=name, grid=(B, D // tc),
        in_specs=[pl.BlockSpec((3, None, S, tc), lambda b, j: (0, b, 0, j)),
                  pl.BlockSpec((None, None, 3, tc), lambda b, j: (j // per_chunk, layer, 0, j % per_chunk))],
        out_specs=pl.BlockSpec((None, S, tc), lambda b, j: (b, 0, j)),
        out_shape=jax.ShapeDtypeStruct((B, S, D), BF16),
        compiler_params=_params(2))(bcu, cwg)


def _conv_bwd(name, bcu, dz, cwg, layer, tc):
    _, B, S, D = bcu.shape
    cwc = cwg.shape[3]
    per_chunk = cwc // tc
    R = min(CONV_ROWS, S)

    def body(x_ref, dz_ref, w_ref, d_ref, dw_ref):
        w = [w_ref[k:k + 1, :] for k in range(3)]

        @pl.when(pl.program_id(1) == 0)
        def _():
            dw_ref[...] = jnp.zeros_like(dw_ref)

        def step(i, carry):
            r0 = pl.multiple_of(i * R, R)
            h0 = pl.multiple_of(jnp.maximum(r0 - CONV_HALO, 0), CONV_HALO)
            a0 = pl.multiple_of(jnp.minimum(r0 + R, S - CONV_HALO), CONV_HALO)
            b = x_ref[0, pl.ds(r0, R), :]
            c = x_ref[1, pl.ds(r0, R), :]
            u = x_ref[2, pl.ds(r0, R), :]
            dz = dz_ref[pl.ds(r0, R), :]
            cu = jnp.concatenate([x_ref[1, pl.ds(h0, CONV_HALO), :] * x_ref[2, pl.ds(h0, CONV_HALO), :], c * u], axis=0)
            cu1 = _conv_shifted(cu, 1, r0, R)
            cu2 = _conv_shifted(cu, 2, r0, R)
            conv = w[0] * (c * u) + w[1] * cu1 + w[2] * cu2
            dconv = dz * b
            dca = jnp.concatenate([dconv, dz_ref[pl.ds(a0, CONV_HALO), :] * x_ref[0, pl.ds(a0, CONV_HALO), :]], axis=0)
            dcu = w[0] * dconv + w[1] * _conv_ahead(dca, 1, r0, R, S) + w[2] * _conv_ahead(dca, 2, r0, R, S)
            d_ref[0, pl.ds(r0, R), :] = (dz * conv).astype(BF16)
            d_ref[1, pl.ds(r0, R), :] = (dcu * u).astype(BF16)
            d_ref[2, pl.ds(r0, R), :] = (dcu * c).astype(BF16)
            return (carry[0] + jnp.sum(dconv * (c * u), axis=0, keepdims=True),
                    carry[1] + jnp.sum(dconv * cu1, axis=0, keepdims=True),
                    carry[2] + jnp.sum(dconv * cu2, axis=0, keepdims=True))

        zero = jnp.zeros((1, tc), F32)
        s0, s1, s2 = lax.fori_loop(0, S // R, step, (zero, zero, zero))
        for k, sk in enumerate((s0, s1, s2)):
            dw_ref[k:k + 1, :] += sk

    return pl.pallas_call(
        body, name=name, grid=(D // tc, B),
        in_specs=[pl.BlockSpec((3, None, S, tc), lambda j, b: (0, b, 0, j)),
                  pl.BlockSpec((None, S, tc), lambda j, b: (b, 0, j)),
                  pl.BlockSpec((None, None, 3, tc), lambda j, b: (j // per_chunk, layer, 0, j % per_chunk))],
        out_specs=[pl.BlockSpec((3, None, S, tc), lambda j, b: (0, b, 0, j)),
                   pl.BlockSpec((3, tc), lambda j, b: (0, j))],
        out_shape=[jax.ShapeDtypeStruct((3, B, S, D), BF16), jax.ShapeDtypeStruct((3, D), F32)],
        compiler_params=_params(2))(bcu, dz, cwg)


def _att_geometry(dil):
    qi = lax.broadcasted_iota(jnp.int32, (ATT_BLK, 2 * ATT_BLK), 0)
    ci = lax.broadcasted_iota(jnp.int32, (ATT_BLK, 2 * ATT_BLK), 1)
    j = ATT_BLK + qi - ci
    inwin = (j >= 0) & (j <= ATT_BLK)
    dist = (dil * j).astype(F32)
    return ci, inwin, dist


def _att_views(B, S, dil, n_heads):
    hp = n_heads * HEAD_DIM // LANES
    return S // dil, hp


def _attn_fwd(name, q, kv, slopes, g, dil, n_heads):
    B, S, CQ = q.shape
    Sd, HP = _att_views(B, S, dil, n_heads)
    NQ = CQ // LANES
    nb = Sd // ATT_BLK
    scale = HEAD_DIM ** -0.5
    qv = q.reshape(B, Sd, dil * CQ)
    kvv = kv.reshape(B, Sd, dil * 2 * CQ)

    def body(sl_ref, q_ref, k_ref, v_ref, o_ref, l_ref):
        hp = pl.program_id(2)
        lane = lax.broadcasted_iota(jnp.int32, (1, LANES), 1)
        ci, inwin, dist = _att_geometry(dil)

        def step(n, carry):
            q0 = pl.multiple_of(n * ATT_BLK, ATT_BLK)
            p0 = pl.multiple_of(jnp.maximum(n - 1, 0) * ATT_BLK, ATT_BLK)
            valid = inwin & (ci >= jnp.where(n > 0, 0, ATT_BLK))
            qb = q_ref[pl.ds(q0, ATT_BLK), :]
            kc = jnp.concatenate([k_ref[pl.ds(p0, ATT_BLK), :], k_ref[pl.ds(q0, ATT_BLK), :]], axis=0)
            vc = jnp.concatenate([v_ref[pl.ds(p0, ATT_BLK), :], v_ref[pl.ds(q0, ATT_BLK), :]], axis=0)
            o_acc = jnp.zeros((ATT_BLK, LANES), F32)
            l_acc = jnp.zeros((ATT_BLK, LANES), F32)
            for hh in range(2):
                hm = (lane >= hh * HEAD_DIM) & (lane < (hh + 1) * HEAD_DIM)
                s = _dot_nt(jnp.where(hm, qb, jnp.zeros_like(qb)), kc) * scale
                s = jnp.where(valid, s - sl_ref[2 * hp + hh] * dist, NEG_INF)
                m = jnp.max(s, axis=-1, keepdims=True)
                p = jnp.exp(s - m)
                l = jnp.sum(p, axis=-1, keepdims=True)
                o_acc = o_acc + _dot(p.astype(BF16), jnp.where(hm, vc, jnp.zeros_like(vc))) / l
                l_acc = jnp.where(hm, m + jnp.log(l), l_acc)
            o_ref[pl.ds(q0, ATT_BLK), :] = o_acc
            l_ref[pl.ds(q0, ATT_BLK), :] = l_acc
            return carry

        lax.fori_loop(0, nb, step, 0)

    blk = (None, Sd, LANES)
    og, lg = pl.pallas_call(
        body, name=name, grid=(B, dil, HP),
        in_specs=[pl.BlockSpec(memory_space=pltpu.SMEM),
                  pl.BlockSpec(blk, lambda b, r, hp: (b, 0, r * NQ + g * HP + hp)),
                  pl.BlockSpec(blk, lambda b, r, hp: (b, 0, r * 2 * NQ + g * 2 * HP + hp)),
                  pl.BlockSpec(blk, lambda b, r, hp: (b, 0, r * 2 * NQ + g * 2 * HP + HP + hp))],
        out_specs=[pl.BlockSpec(blk, lambda b, r, hp: (b, 0, r * HP + hp)),
                   pl.BlockSpec(blk, lambda b, r, hp: (b, 0, r * HP + hp))],
        out_shape=[jax.ShapeDtypeStruct((B, Sd, dil * HP * LANES), F32)] * 2,
        compiler_params=_params(3))(slopes, qv, kvv, kvv)
    return og.reshape(B, S, HP * LANES), lg.reshape(B, S, HP * LANES)


def _attn_combine(name, os, ls, tm):
    T, C = os[0].shape

    def body(o0, o1, o2, l0, l1, l2, o_ref, lse_ref):
        a, b, c = l0[...], l1[...], l2[...]
        m = jnp.maximum(jnp.maximum(a, b), c)
        ea, eb, ec = jnp.exp(a - m), jnp.exp(b - m), jnp.exp(c - m)
        z = ea + eb + ec
        o_ref[...] = (ea / z) * o0[...] + (eb / z) * o1[...] + (ec / z) * o2[...]
        lse_ref[...] = m + jnp.log(z)

    spec = pl.BlockSpec((tm, C), lambda i: (i, 0))
    return pl.pallas_call(
        body, name=name, grid=(T // tm,), in_specs=[spec] * 6, out_specs=[spec] * 2,
        out_shape=[jax.ShapeDtypeStruct((T, C), F32)] * 2, compiler_params=_params(1))(*os, *ls)


def _attn_bwd(name, q, kv, slopes, o, lse, do, g, dil, n_heads, dkv_prev):
    B, S, CQ = q.shape
    Sd, HP = _att_views(B, S, dil, n_heads)
    NQ = CQ // LANES
    nb = Sd // ATT_BLK
    scale = HEAD_DIM ** -0.5
    qv = q.reshape(B, Sd, dil * CQ)
    kvv = kv.reshape(B, Sd, dil * 2 * CQ)
    gshape = (B, Sd, dil * HP * LANES)
    n_prev = 0 if dkv_prev is None else 2

    def body(sl_ref, q_ref, k_ref, v_ref, o_ref, lse_ref, do_ref, *rest):
        dq_ref, dk_ref, dv_ref = rest[n_prev:]
        hp = pl.program_id(2)
        lane = lax.broadcasted_iota(jnp.int32, (1, LANES), 1)
        ci, inwin, dist = _att_geometry(dil)
        if n_prev:
            dk_ref[...] = rest[0][...]
            dv_ref[...] = rest[1][...]
        else:
            dk_ref[...] = jnp.zeros_like(dk_ref)
            dv_ref[...] = jnp.zeros_like(dv_ref)

        def step(n, carry):
            q0 = pl.multiple_of(n * ATT_BLK, ATT_BLK)
            p0 = pl.multiple_of(jnp.maximum(n - 1, 0) * ATT_BLK, ATT_BLK)
            valid = inwin & (ci >= jnp.where(n > 0, 0, ATT_BLK))
            qb = q_ref[pl.ds(q0, ATT_BLK), :]
            kc = jnp.concatenate([k_ref[pl.ds(p0, ATT_BLK), :], k_ref[pl.ds(q0, ATT_BLK), :]], axis=0)
            vc = jnp.concatenate([v_ref[pl.ds(p0, ATT_BLK), :], v_ref[pl.ds(q0, ATT_BLK), :]], axis=0)
            dob = do_ref[pl.ds(q0, ATT_BLK), :]
            prod = dob * o_ref[pl.ds(q0, ATT_BLK), :]
            lseb = lse_ref[pl.ds(q0, ATT_BLK), :]
            dob16 = dob.astype(BF16)
            dq = jnp.zeros((ATT_BLK, LANES), F32)
            dk = jnp.zeros((2 * ATT_BLK, LANES), F32)
            dv = jnp.zeros((2 * ATT_BLK, LANES), F32)
            for hh in range(2):
                hm = (lane >= hh * HEAD_DIM) & (lane < (hh + 1) * HEAD_DIM)
                qm = jnp.where(hm, qb, jnp.zeros_like(qb))
                dom = jnp.where(hm, dob16, jnp.zeros_like(dob16))
                delta = jnp.sum(jnp.where(hm, prod, 0.0), axis=-1, keepdims=True)
                lse_h = jnp.max(jnp.where(hm, lseb, -jnp.inf), axis=-1, keepdims=True)
                s = _dot_nt(qm, kc) * scale
                s = jnp.where(valid, s - sl_ref[2 * hp + hh] * dist, NEG_INF)
                p = jnp.exp(s - lse_h)
                dp = _dot_nt(dom, vc)
                ds = (p * (dp - delta) * scale)
                dq = dq + _dot(ds.astype(BF16), jnp.where(hm, kc, jnp.zeros_like(kc)))
                dk = dk + _dot(ds.T.astype(BF16), qm)
                dv = dv + _dot(p.T.astype(BF16), dom)
            dq_ref[pl.ds(q0, ATT_BLK), :] = dq.astype(BF16)
            dk_ref[pl.ds(p0, ATT_BLK), :] += dk[:ATT_BLK]
            dk_ref[pl.ds(q0, ATT_BLK), :] += dk[ATT_BLK:]
            dv_ref[pl.ds(p0, ATT_BLK), :] += dv[:ATT_BLK]
            dv_ref[pl.ds(q0, ATT_BLK), :] += dv[ATT_BLK:]
            return carry

        lax.fori_loop(0, nb, step, 0)

    blk = (None, Sd, LANES)
    gspec = pl.BlockSpec(blk, lambda b, r, hp: (b, 0, r * HP + hp))
    prev = [] if dkv_prev is None else [t.reshape(gshape) for t in dkv_prev]
    dq, dk, dv = pl.pallas_call(
        body, name=name, grid=(B, dil, HP),
        in_specs=[pl.BlockSpec(memory_space=pltpu.SMEM),
                  pl.BlockSpec(blk, lambda b, r, hp: (b, 0, r * NQ + g * HP + hp)),
                  pl.BlockSpec(blk, lambda b, r, hp: (b, 0, r * 2 * NQ + g * 2 * HP + hp)),
                  pl.BlockSpec(blk, lambda b, r, hp: (b, 0, r * 2 * NQ + g * 2 * HP + HP + hp)),
                  gspec, gspec, gspec] + [gspec] * n_prev,
        out_specs=[gspec] * 3,
        out_shape=[jax.ShapeDtypeStruct(gshape, BF16), jax.ShapeDtypeStruct(gshape, F32),
                   jax.ShapeDtypeStruct(gshape, F32)],
        compiler_params=_params(3))(slopes, qv, kvv, kvv, o.reshape(gshape), lse.reshape(gshape),
                                    do.reshape(gshape), *prev)
    C = HP * LANES
    return dq.reshape(B * S, C), dk.reshape(B * S, C), dv.reshape(B * S, C)


def _final_loss(name, h, g, target, tm):
    T, D = h.shape

    def body(h_ref, g_ref, t_ref, loss_ref, dh_ref, dg_ref):
        hf = h_ref[...]
        gv = g_ref[...]
        rstd = lax.rsqrt(jnp.mean(hf * hf, axis=-1, keepdims=True) + EPS)
        xhat = hf * rstd
        err = xhat * gv - t_ref[...]
        part = 0.5 * jnp.sum(jnp.mean(err * err, axis=-1, keepdims=True), axis=0, keepdims=True)
        dy = err * (1.0 / D)
        dg = jnp.sum(dy * xhat, axis=0, keepdims=True)
        dx = dy * gv
        dh_ref[...] = rstd * (dx - xhat * jnp.mean(dx * xhat, axis=-1, keepdims=True))

        @pl.when(pl.program_id(0) == 0)
        def _():
            loss_ref[...] = part
            dg_ref[...] = dg

        @pl.when(pl.program_id(0) > 0)
        def _():
            loss_ref[...] += part
            dg_ref[...] += dg

    return pl.pallas_call(
        body, name=name, grid=(T // tm,),
        in_specs=[pl.BlockSpec((tm, D), lambda i: (i, 0)), pl.BlockSpec((1, D), lambda i: (0, 0)),
                  pl.BlockSpec((tm, D), lambda i: (i, 0))],
        out_specs=[pl.BlockSpec((1, 1), lambda i: (0, 0)), pl.BlockSpec((tm, D), lambda i: (i, 0)),
                   pl.BlockSpec((1, D), lambda i: (0, 0))],
        out_shape=[jax.ShapeDtypeStruct((1, 1), F32), jax.ShapeDtypeStruct((T, D), F32),
                   jax.ShapeDtypeStruct((1, D), F32)],
        compiler_params=_params(1))(h, g, target)


def _nt_rows(name, dh, wg, layer, a_mul, out_dtype, tm, tn):
    T, D = dh.shape
    rk = wg.shape[2]
    N = N_CHIPS * rk
    per_chunk = rk // tn
    with_a = a_mul is not None

    def body(dh_ref, w_ref, *rest):
        if with_a:
            a_ref, o_ref, d16 = rest
        else:
            o_ref, d16 = rest

        @pl.when(pl.program_id(1) == 0)
        def _():
            d16[...] = dh_ref[...].astype(BF16)

        r = _dot_nt(d16[...], w_ref[...])
        if with_a:
            r = r * (2.0 * jnp.maximum(a_ref[...], 0.0))
        o_ref[...] = r.astype(out_dtype)

    in_specs = [pl.BlockSpec((tm, D), lambda i, j: (i, 0)),
                pl.BlockSpec((None, None, tn, D), lambda i, j: (j // per_chunk, layer, j % per_chunk, 0))]
    args = [dh, wg]
    if with_a:
        in_specs.append(pl.BlockSpec((tm, tn), lambda i, j: (i, j)))
        args.append(a_mul)
    return pl.pallas_call(
        body, name=name, grid=(T // tm, N // tn), in_specs=in_specs,
        out_specs=pl.BlockSpec((tm, tn), lambda i, j: (i, j)),
        out_shape=jax.ShapeDtypeStruct((T, N), out_dtype),
        scratch_shapes=[pltpu.VMEM((tm, D), BF16)],
        compiler_params=_params(2))(*args)


def _nt_cols(name, ysegs, wg, layer, tm, norm):
    Nw, cw = wg.shape[2], wg.shape[3]
    widths = [bs[-1] for _, bs, _ in ysegs]
    pieces = _pieces(widths, cw, 1024)
    ns = len(ysegs)
    T = norm[0].shape[0] if norm is not None else ysegs[0][0].shape[-2]

    def body(*refs):
        y_refs = refs[:ns]
        w_ref = refs[ns]
        acc = refs[-1]
        for n, (s, a0, ch, b0, wd) in enumerate(pieces):
            d = _dot_nt(y_refs[s][:, a0:a0 + wd].astype(BF16), w_ref[ch, :, b0:b0 + wd])
            if n == 0:
                acc[...] = d
            else:
                acc[...] += d
        if norm is None:
            refs[ns + 1][...] = acc[...]
        else:
            h_ref, g_ref, dhin_ref, out_ref, dg_ref = refs[ns + 1:ns + 6]
            dh_c, dg = _rms_bwd(h_ref[...], g_ref[...], acc[...])
            out_ref[...] = dhin_ref[...] + dh_c

            @pl.when(pl.program_id(0) == 0)
            def _():
                dg_ref[...] = dg

            @pl.when(pl.program_id(0) > 0)
            def _():
                dg_ref[...] += dg

    in_specs = [pl.BlockSpec(bs, im) for _, bs, im in ysegs]
    in_specs.append(pl.BlockSpec((N_CHIPS, None, Nw, cw), lambda i: (0, layer, 0, 0)))
    args = [a for a, _, _ in ysegs] + [wg]
    row = pl.BlockSpec((tm, Nw), lambda i: (i, 0))
    vec = pl.BlockSpec((1, Nw), lambda i: (0, 0))
    if norm is None:
        out_specs = row
        out_shape = jax.ShapeDtypeStruct((T, Nw), F32)
    else:
        in_specs += [row, vec, row]
        args += list(norm)
        out_specs = [row, vec]
        out_shape = [jax.ShapeDtypeStruct((T, Nw), F32), jax.ShapeDtypeStruct((1, Nw), F32)]
    return pl.pallas_call(
        body, name=name, grid=(T // tm,), in_specs=in_specs, out_specs=out_specs, out_shape=out_shape,
        scratch_shapes=[pltpu.VMEM((tm, Nw), F32)], compiler_params=_params(1))(*args)


def _tn(name, x, x_act, ysegs, cw, cols_layout, tmm, tt):
    T, M = x.shape
    widths = [bs[-1] for _, bs, _ in ysegs]
    N = sum(widths)
    pieces = _pieces(widths, cw if cols_layout else N, 1024)
    ns = len(ysegs)

    def body(x_ref, *refs):
        y_refs = refs[:ns]
        o_ref = refs[ns]

        @pl.when(pl.program_id(1) == 0)
        def _():
            o_ref[...] = jnp.zeros_like(o_ref)

        xt = x_act(x_ref[...]).astype(BF16)
        for s, a0, ch, b0, wd in pieces:
            d = _dot_tn(xt, y_refs[s][:, a0:a0 + wd].astype(BF16))
            if cols_layout:
                o_ref[ch, :, b0:b0 + wd] += d
            else:
                o_ref[:, b0:b0 + wd] += d

    in_specs = [pl.BlockSpec((tt, tmm), lambda m, t: (t, m))] + [pl.BlockSpec(bs, im) for _, bs, im in ysegs]
    if cols_layout:
        out_specs = pl.BlockSpec((N_CHIPS, tmm, cw), lambda m, t: (0, m, 0))
        out_shape = jax.ShapeDtypeStruct((N_CHIPS, M, cw), F32)
    else:
        out_specs = pl.BlockSpec((tmm, N), lambda m, t: (m, 0))
        out_shape = jax.ShapeDtypeStruct((M, N), F32)
    return pl.pallas_call(
        body, name=name, grid=(M // tmm, T // tt), in_specs=in_specs, out_specs=out_specs, out_shape=out_shape,
        compiler_params=_params(2))(x, *[a for a, _, _ in ysegs])


def _seg2d(a, t_rows, grid_rank):
    w = a.shape[1]
    if grid_rank == 1:
        return (a, (t_rows, w), lambda i: (i, 0))
    return (a, (t_rows, w), lambda m, t: (t, 0))


def _seg_plane(a, plane, t_rows, grid_rank):
    w = a.shape[2]
    if grid_rank == 1:
        return (a, (None, t_rows, w), lambda i: (plane, i, 0))
    return (a, (None, t_rows, w), lambda m, t: (plane, t, 0))


def _row_tile(rows, cols, budget_bytes=2 * 1024 * 1024):
    t = rows
    while t * cols * 4 > budget_bytes and t % 16 == 0:
        t //= 2
    return t


def _pair_add(name, gfull, recv, cidx):
    _, _, hr, c = gfull.shape
    tr = _row_tile(hr, c)

    def body(c_ref, g_ref, r_ref, o_ref):
        o_ref[...] = g_ref[...] + r_ref[...]

    grid_spec = pltpu.PrefetchScalarGridSpec(
        num_scalar_prefetch=1, grid=(N_CHIPS, hr // tr),
        in_specs=[pl.BlockSpec((None, None, tr, c), lambda q, i, c_ref: (q, c_ref[0], i, 0)),
                  pl.BlockSpec((None, tr, c), lambda q, i, c_ref: (q, i, 0))],
        out_specs=pl.BlockSpec((None, tr, c), lambda q, i, c_ref: (q, i, 0)))
    return pl.pallas_call(body, name=name, grid_spec=grid_spec,
                          out_shape=jax.ShapeDtypeStruct((N_CHIPS, hr, c), F32),
                          compiler_params=_params(2))(cidx, gfull, recv)


def _chip_add(name, slots):
    _, hr, c = slots.shape
    tr = _row_tile(hr, c)

    def body(s_ref, o_ref):
        o_ref[...] = ((s_ref[0] + s_ref[1]) + s_ref[2]) + s_ref[3]

    return pl.pallas_call(
        body, name=name, grid=(hr // tr,),
        in_specs=[pl.BlockSpec((N_CHIPS, tr, c), lambda i: (0, i, 0))],
        out_specs=pl.BlockSpec((tr, c), lambda i: (i, 0)),
        out_shape=jax.ShapeDtypeStruct((hr, c), F32), compiler_params=_params(1))(slots)


def _adamw(name, w, g, m, v):
    rows, cols = w.shape
    tr = _row_tile(rows, cols, 1024 * 1024)

    def body(w_ref, g_ref, m_ref, v_ref, d_ref, nm_ref, nv_ref):
        gv = g_ref[...]
        nm = ADAM_B1 * m_ref[...] + (1.0 - ADAM_B1) * gv
        nv = ADAM_B2 * v_ref[...] + (1.0 - ADAM_B2) * jnp.square(gv)
        m_hat = nm / (1.0 - ADAM_B1 ** ADAM_STEP)
        v_hat = nv / (1.0 - ADAM_B2 ** ADAM_STEP)
        d_ref[...] = -ADAM_LR * (m_hat / (jnp.sqrt(v_hat) + ADAM_EPS) + ADAM_WD * w_ref[...])
        nm_ref[...] = nm
        nv_ref[...] = nv

    spec = pl.BlockSpec((tr, cols), lambda i: (i, 0))
    return pl.pallas_call(
        body, name=name, grid=(rows // tr,), in_specs=[spec] * 4, out_specs=[spec] * 3,
        out_shape=[jax.ShapeDtypeStruct((rows, cols), F32)] * 3, compiler_params=_params(1))(w, g, m, v)


ANY = pl.BlockSpec(memory_space=pl.ANY)


def _place():
    x, y, c = lax.axis_index("x"), lax.axis_index("y"), lax.axis_index("c")
    chips = [(1 - x, y), (x, 1 - y), (1 - x, 1 - y)]
    return x, y, c, chips


def _all_gather(chunks):
    n = len(chunks)

    def body(*refs):
        ins, outs = refs[:n], refs[n:2 * n]
        ssem, rsem, fssem, frsem, lsem = refs[2 * n:]
        x, y, c, chips = _place()
        p = 2 * x + y
        locals_, sends = [], []
        for t in range(n):
            hr = ins[t].shape[0] // 2
            mine = pl.ds(pl.multiple_of(c * hr, 8), hr)
            local = pltpu.make_async_copy(ins[t], outs[t].at[p], lsem.at[t])
            local.start()
            locals_.append(local)
            for k, (qx, qy) in enumerate(chips):
                cp = pltpu.make_async_remote_copy(
                    src_ref=ins[t].at[mine], dst_ref=outs[t].at[p, mine], send_sem=ssem.at[t, k],
                    recv_sem=rsem.at[t, k], device_id=(qx, qy, c), device_id_type=MESH)
                cp.start()
                sends.append(cp)
        for t in range(n):
            hr = ins[t].shape[0] // 2
            mine = pl.ds(pl.multiple_of(c * hr, 8), hr)
            for k, (qx, qy) in enumerate(chips):
                q = 2 * qx + qy
                landed = outs[t].at[q, mine]
                pltpu.make_async_remote_copy(
                    src_ref=landed, dst_ref=landed, send_sem=ssem.at[t, k], recv_sem=rsem.at[t, k],
                    device_id=(qx, qy, c), device_id_type=MESH).wait_recv()
                fw = pltpu.make_async_remote_copy(
                    src_ref=landed, dst_ref=landed, send_sem=fssem.at[t, k], recv_sem=frsem.at[t, k],
                    device_id=(x, y, 1 - c), device_id_type=MESH)
                fw.start()
                sends.append(fw)
        for t in range(n):
            hr = ins[t].shape[0] // 2
            theirs = pl.ds(pl.multiple_of((1 - c) * hr, 8), hr)
            for k, (qx, qy) in enumerate(chips):
                q = 2 * qx + qy
                passed = outs[t].at[q, theirs]
                pltpu.make_async_remote_copy(
                    src_ref=passed, dst_ref=passed, send_sem=fssem.at[t, k], recv_sem=frsem.at[t, k],
                    device_id=(x, y, 1 - c), device_id_type=MESH).wait_recv()
        for cp in locals_:
            cp.wait()
        for cp in sends:
            cp.wait_send()

    return pl.pallas_call(
        body, name="gather_weights", in_specs=[ANY] * n, out_specs=[ANY] * n,
        out_shape=[jax.ShapeDtypeStruct((N_CHIPS,) + a.shape, a.dtype) for a in chunks],
        scratch_shapes=[pltpu.SemaphoreType.DMA((n, 3)), pltpu.SemaphoreType.DMA((n, 3)),
                        pltpu.SemaphoreType.DMA((n, 3)), pltpu.SemaphoreType.DMA((n, 3)),
                        pltpu.SemaphoreType.DMA((n,))])(*chunks)


def _pair_exchange(grads):
    n = len(grads)

    def body(*refs):
        ins, outs = refs[:n], refs[n:2 * n]
        ssem, rsem = refs[2 * n:]
        x, y, c, _ = _place()
        cps = []
        for t in range(n):
            cp = pltpu.make_async_remote_copy(
                src_ref=ins[t].at[:, 1 - c], dst_ref=outs[t], send_sem=ssem.at[t], recv_sem=rsem.at[t],
                device_id=(x, y, 1 - c), device_id_type=MESH)
            cp.start()
            cps.append(cp)
        for cp in cps:
            cp.wait()

    return pl.pallas_call(
        body, name="grad_pair_exchange", in_specs=[ANY] * n, out_specs=[ANY] * n,
        out_shape=[jax.ShapeDtypeStruct((a.shape[0],) + a.shape[2:], a.dtype) for a in grads],
        scratch_shapes=[pltpu.SemaphoreType.DMA((n,)), pltpu.SemaphoreType.DMA((n,))])(*grads)


def _chip_exchange(parts):
    n = len(parts)

    def body(*refs):
        ins, outs = refs[:n], refs[n:2 * n]
        ssem, rsem, lsem = refs[2 * n:]
        x, y, c, chips = _place()
        p = 2 * x + y
        locals_, sends = [], []
        for t in range(n):
            local = pltpu.make_async_copy(ins[t].at[p], outs[t].at[p], lsem.at[t])
            local.start()
            locals_.append(local)
            for k, (qx, qy) in enumerate(chips):
                cp = pltpu.make_async_remote_copy(
                    src_ref=ins[t].at[2 * qx + qy], dst_ref=outs[t].at[p], send_sem=ssem.at[t, k],
                    recv_sem=rsem.at[t, k], device_id=(qx, qy, c), device_id_type=MESH)
                cp.start()
                sends.append(cp)
        for t in range(n):
            for k, (qx, qy) in enumerate(chips):
                slot = outs[t].at[2 * qx + qy]
                pltpu.make_async_remote_copy(
                    src_ref=slot, dst_ref=slot, send_sem=ssem.at[t, k], recv_sem=rsem.at[t, k],
                    device_id=(qx, qy, c), device_id_type=MESH).wait_recv()
        for cp in locals_:
            cp.wait()
        for cp in sends:
            cp.wait_send()

    return pl.pallas_call(
        body, name="grad_chip_exchange", in_specs=[ANY] * n, out_specs=[ANY] * n,
        out_shape=[jax.ShapeDtypeStruct(a.shape, a.dtype) for a in parts],
        scratch_shapes=[pltpu.SemaphoreType.DMA((n, 3)), pltpu.SemaphoreType.DMA((n, 3)),
                        pltpu.SemaphoreType.DMA((n,))])(*parts)


def _pair_share(sums, layout):
    n = len(sums)
    n_out = 1 + max(o for o, _ in layout)
    shapes = {}
    for a, (o, l) in zip(sums, layout):
        L = 1 + max(ll for oo, ll in layout if oo == o)
        shapes[o] = (L, 2) + a.shape

    def body(*refs):
        ins, outs = refs[:n], refs[n:n + n_out]
        ssem, rsem, lsem = refs[n + n_out:]
        x, y, c, _ = _place()
        locals_, sends = [], []
        for t, (o, l) in enumerate(layout):
            local = pltpu.make_async_copy(ins[t], outs[o].at[l, c], lsem.at[t])
            local.start()
            locals_.append(local)
            cp = pltpu.make_async_remote_copy(
                src_ref=ins[t], dst_ref=outs[o].at[l, c], send_sem=ssem.at[t], recv_sem=rsem.at[t],
                device_id=(x, y, 1 - c), device_id_type=MESH)
            cp.start()
            sends.append(cp)
        for t, (o, l) in enumerate(layout):
            theirs = outs[o].at[l, 1 - c]
            pltpu.make_async_remote_copy(
                src_ref=theirs, dst_ref=theirs, send_sem=ssem.at[t], recv_sem=rsem.at[t],
                device_id=(x, y, 1 - c), device_id_type=MESH).wait_recv()
        for cp in locals_:
            cp.wait()
        for cp in sends:
            cp.wait_send()

    return pl.pallas_call(
        body, name="grad_pair_share", in_specs=[ANY] * n, out_specs=[ANY] * n_out,
        out_shape=[jax.ShapeDtypeStruct(shapes[o], F32) for o in range(n_out)],
        scratch_shapes=[pltpu.SemaphoreType.DMA((n,)), pltpu.SemaphoreType.DMA((n,)),
                        pltpu.SemaphoreType.DMA((n,))])(*sums)


def _small_allreduce(part):
    R, C = part.shape
    N_DEV = 8

    def body(in_ref, out_ref, slots, ssem, rsem):
        x, y, c, _ = _place()
        me = 4 * x + 2 * y + c
        sends = []
        for k in range(1, N_DEV):
            kx, ky, kc = (k >> 2) & 1, (k >> 1) & 1, k & 1
            peer = (1 - x if kx else x, 1 - y if ky else y, 1 - c if kc else c)
            cp = pltpu.make_async_remote_copy(
                src_ref=in_ref, dst_ref=slots.at[me], send_sem=ssem.at[k], recv_sem=rsem.at[k],
                device_id=peer, device_id_type=MESH)
            cp.start()
            sends.append(cp)
        slots[me] = in_ref[...]
        for k in range(1, N_DEV):
            kx, ky, kc = (k >> 2) & 1, (k >> 1) & 1, k & 1
            peer = (1 - x if kx else x, 1 - y if ky else y, 1 - c if kc else c)
            slot = slots.at[4 * peer[0] + 2 * peer[1] + peer[2]]
            pltpu.make_async_remote_copy(
                src_ref=slot, dst_ref=slot, send_sem=ssem.at[k], recv_sem=rsem.at[k],
                device_id=peer, device_id_type=MESH).wait_recv()
        acc = slots[0]
        for d in range(1, N_DEV):
            acc = acc + slots[d]
        out_ref[...] = acc
        for cp in sends:
            cp.wait_send()

    vm = pl.BlockSpec(memory_space=pltpu.VMEM)
    return pl.pallas_call(
        body, name="small_allreduce", in_specs=[vm], out_specs=vm,
        out_shape=jax.ShapeDtypeStruct((R, C), F32),
        scratch_shapes=[pltpu.VMEM((N_DEV, R, C), F32), pltpu.SemaphoreType.DMA((N_DEV,)),
                        pltpu.SemaphoreType.DMA((N_DEV,))])(part)


def _local_step(x, target, norm_mix, norm_mlp, norm_kv, norm_final, W, cwg):
    B, S, D = x.shape
    T = B * S
    n_heads = W["w_o"].shape[2] // HEAD_DIM
    C = n_heads * HEAD_DIM
    n_a = W["w_a_in"].shape[1]
    n_b = W["w_q"].shape[1]
    depth = n_a + n_b
    F = W["w_up"].shape[3] * N_CHIPS
    slopes = 2.0 ** (-ALIBI_MAX_BIAS * jnp.arange(1, n_heads + 1, dtype=F32) / n_heads)
    tm = min(512, T)
    row = lambda v: v.reshape(1, -1)

    h = x.reshape(T, D)
    saved = []
    kv = nkv = h_kv = None
    for l in range(depth):
        s = {"h_in": h}
        if l < n_a:
            s["n1"], bcu = _norm_mm(f"a_in_fwd{l}", h, row(norm_mix[l]), W["w_a_in"], l, 3, F32, tm, 256)
            s["bcu"] = bcu.reshape(3, B, S, D)
            s["z"] = _conv_fwd(f"conv_fwd{l}", s["bcu"], cwg, l, LANES).reshape(T, D)
            h = _mm_res_rows(f"a_out_fwd{l}", s["z"], W["w_a_out"], l, h, _identity, tm)
        else:
            i = l - n_a
            if i == 0:
                h_kv = h
                nkv, kv = _norm_mm("kv_fwd", h, row(norm_kv), W["w_kv"], 0, 1, BF16, tm, 256)
                kv = kv.reshape(B, S, 2 * 3 * C)
            s["n1"], q = _norm_mm(f"q_fwd{i}", h, row(norm_mix[l]), W["w_q"], i, 1, BF16, tm, W["w_q"].shape[3])
            s["q"] = q.reshape(B, S, 3 * C)
            os, ls = [], []
            for g, (window, dil) in enumerate(PATTERNS):
                og, lg = _attn_fwd(f"attn_fwd{i}_{g}", s["q"], kv, slopes, g, dil, n_heads)
                os.append(og.reshape(T, C))
                ls.append(lg.reshape(T, C))
            s["o"], s["lse"] = _attn_combine(f"attn_combine{i}", os, ls, tm)
            h = _mm_res_cols(f"o_fwd{i}", s["o"], W["w_o"], i, h, tm)
        s["h_mid"] = h
        s["n2"], a = _norm_mm(f"up_fwd{l}", h, row(norm_mlp[l]), W["w_up"], l, 1, F32, tm, 512)
        s["a"] = a.reshape(T, F)
        h = _mm_res_rows(f"down_fwd{l}", s["a"], W["w_down"], l, h, _relu2, tm)
        saved.append(s)

    loss, dh, dg_final = _final_loss("loss_head", h, row(norm_final), target.reshape(T, D), tm)

    gw = {k: [None] * W[k].shape[1] for k in W}
    g_mix, g_mlp = [None] * depth, [None] * depth
    g_conv = [None] * n_a
    dkv = None
    tt = min(512, T)
    for l in reversed(range(depth)):
        s = saved[l]
        da = _nt_rows(f"down_bwd{l}", dh, W["w_down"], l, s["a"], BF16, tm, 512)
        gw["w_down"][l] = _tn(f"down_wgrad{l}", s["a"], _relu2, [_seg2d(dh, tt, 2)], None, False,
                              min(1024, F), tt).reshape(N_CHIPS, F // N_CHIPS, D)
        gw["w_up"][l] = _tn(f"up_wgrad{l}", s["n2"], _identity, [_seg2d(da, tt, 2)], F // N_CHIPS, True,
                            min(512, D), tt)
        dh, g_mlp[l] = _nt_cols(f"up_bwd{l}", [_seg2d(da, tm, 1)], W["w_up"], l, tm,
                                (s["h_mid"], row(norm_mlp[l]), dh))
        if l < n_a:
            gw["w_a_out"][l] = _tn(f"a_out_wgrad{l}", s["z"], _identity, [_seg2d(dh, tt, 2)], None, False,
                                   D, tt).reshape(N_CHIPS, D // N_CHIPS, D)
            dz = _nt_rows(f"a_out_bwd{l}", dh, W["w_a_out"], l, None, F32, tm, W["w_a_out"].shape[2])
            dbcu, g_conv[l] = _conv_bwd(f"conv_bwd{l}", s["bcu"], dz.reshape(B, S, D), cwg, l, LANES)
            dbcu = dbcu.reshape(3, T, D)
            gw["w_a_in"][l] = _tn(f"a_in_wgrad{l}", s["n1"], _identity,
                                  [_seg_plane(dbcu, p, tt, 2) for p in range(3)], 3 * D // N_CHIPS, True,
                                  min(512, D), tt)
            dh, g_mix[l] = _nt_cols(f"a_in_bwd{l}", [_seg_plane(dbcu, p, tm, 1) for p in range(3)],
                                    W["w_a_in"], l, tm, (s["h_in"], row(norm_mix[l]), dh))
        else:
            i = l - n_a
            gw["w_o"][i] = _tn(f"o_wgrad{i}", s["o"], _identity, [_seg2d(dh, tt, 2)], D // N_CHIPS, True, C, tt)
            do = _nt_cols(f"o_bwd{i}", [_seg2d(dh, tm, 1)], W["w_o"], i, tm, None)
            dqs, new_dkv = [], []
            for g, (window, dil) in enumerate(PATTERNS):
                prev = None if dkv is None else (dkv[2 * g].reshape(B, S, C), dkv[2 * g + 1].reshape(B, S, C))
                dq, dk, dv = _attn_bwd(f"attn_bwd{i}_{g}", s["q"], kv, slopes, s["o"].reshape(B, S, C),
                                       s["lse"].reshape(B, S, C), do.reshape(B, S, C), g, dil, n_heads, prev)
                dqs.append(dq)
                new_dkv += [dk, dv]
            dkv = new_dkv
            gw["w_q"][i] = _tn(f"q_wgrad{i}", s["n1"], _identity, [_seg2d(t, tt, 2) for t in dqs],
                               3 * C // N_CHIPS, True, min(512, D), tt)
            dh, g_mix[l] = _nt_cols(f"q_bwd{i}", [_seg2d(t, tm, 1) for t in dqs], W["w_q"], i, tm,
                                    (s["h_in"], row(norm_mix[l]), dh))
            if i == 0:
                gw["w_kv"][0] = _tn("kv_wgrad", nkv, _identity, [_seg2d(t, tt, 2) for t in dkv],
                                    6 * C // N_CHIPS, True, min(512, D), tt)
                dh, g_kv = _nt_cols("kv_bwd", [_seg2d(t, tm, 1) for t in dkv], W["w_kv"], 0, tm,
                                    (h_kv, row(norm_kv), dh))
    small = dict(norm_mix=jnp.concatenate(g_mix, axis=0), norm_mlp=jnp.concatenate(g_mlp, axis=0),
                 norm_kv=g_kv, norm_final=dg_final, conv_w=jnp.stack(g_conv))
    return loss, dh.reshape(B, S, D), gw, small


BIG = ("w_a_in", "w_a_out", "w_kv", "w_q", "w_o", "w_up", "w_down")
CONV_PAD_ROWS = 16


def _reduce_scatter(gw):
    names, layout, flat = [], [], []
    for o, k in enumerate(BIG):
        for l, a in enumerate(gw[k]):
            flat.append(a.reshape(N_CHIPS, 2, a.shape[1] // 2, a.shape[2]))
            layout.append((o, l))
    cidx = lax.axis_index("c").astype(jnp.int32).reshape(1)
    recv = _pair_exchange(flat)
    parts = [_pair_add(f"grad_pair_add{t}", a, r, cidx) for t, (a, r) in enumerate(zip(flat, recv))]
    slots = _chip_exchange(parts)
    sums = [_chip_add(f"grad_chip_add{t}", a) for t, a in enumerate(slots)]
    outs = _pair_share(sums, layout)
    return {k: a.reshape(a.shape[0], a.shape[1] * a.shape[2], a.shape[3]) for k, a in zip(BIG, outs)}


def kernel(x, norm_mix, norm_mlp, w_a_in, conv_w, w_a_out, norm_kv, w_kv, w_q, w_o, w_up, w_down, norm_final, loss_target, m_norm_mix, m_norm_mlp, m_w_a_in, m_conv_w, m_w_a_out, m_norm_kv, m_w_kv, m_w_q, m_w_o, m_w_up, m_w_down, m_norm_final, v_norm_mix, v_norm_mlp, v_w_a_in, v_conv_w, v_w_a_out, v_norm_kv, v_w_kv, v_w_q, v_w_o, v_w_up, v_w_down, v_norm_final):
    D = x.shape[-1]
    w = dict(norm_mix=norm_mix, norm_mlp=norm_mlp, w_a_in=w_a_in, conv_w=conv_w, w_a_out=w_a_out, norm_kv=norm_kv,
             w_kv=w_kv[None], w_q=w_q, w_o=w_o, w_up=w_up, w_down=w_down, norm_final=norm_final)
    m = dict(norm_mix=m_norm_mix, norm_mlp=m_norm_mlp, w_a_in=m_w_a_in, conv_w=m_conv_w, w_a_out=m_w_a_out,
             norm_kv=m_norm_kv, w_kv=m_w_kv[None], w_q=m_w_q, w_o=m_w_o, w_up=m_w_up, w_down=m_w_down,
             norm_final=m_norm_final)
    v = dict(norm_mix=v_norm_mix, norm_mlp=v_norm_mlp, w_a_in=v_w_a_in, conv_w=v_conv_w, w_a_out=v_w_a_out,
             norm_kv=v_norm_kv, w_kv=v_w_kv[None], w_q=v_w_q, w_o=v_w_o, w_up=v_w_up, w_down=v_w_down,
             norm_final=v_norm_final)

    n_a, taps, cwc = conv_w.shape
    conv_rows = jnp.zeros((CONV_PAD_ROWS, cwc), F32).at[:n_a * taps].set(conv_w.reshape(n_a * taps, cwc))
    chunks = [w[k].astype(BF16).reshape(-1, w[k].shape[-1]) for k in BIG] + [conv_rows]
    gathered = _all_gather(chunks)
    W = {k: a.reshape((N_CHIPS,) + w[k].shape) for k, a in zip(BIG, gathered[:-1])}
    cwg = gathered[-1][:, :n_a * taps].reshape(N_CHIPS, n_a, taps, cwc)

    loss, grad_x, gw, small = _local_step(x, loss_target, norm_mix, norm_mlp, norm_kv, norm_final, W, cwg)
    loss = lax.psum(loss[0, 0], ("x", "y", "c"))

    grads = _reduce_scatter(gw)

    depth = norm_mix.shape[0]
    packed = jnp.concatenate([small["norm_mix"], small["norm_mlp"], small["norm_kv"], small["norm_final"],
                              small["conv_w"].reshape(n_a * taps, D)], axis=0)
    pad = (-packed.shape[0]) % 8
    packed = jnp.pad(packed, ((0, pad), (0, 0)))
    total = _small_allreduce(packed)
    grads["norm_mix"] = total[:depth]
    grads["norm_mlp"] = total[depth:2 * depth]
    grads["norm_kv"] = total[2 * depth]
    grads["norm_final"] = total[2 * depth + 1]
    chip = 2 * lax.axis_index("x") + lax.axis_index("y")
    conv_full = total[2 * depth + 2:2 * depth + 2 + n_a * taps].reshape(n_a, taps, N_CHIPS, cwc)
    grads["conv_w"] = lax.dynamic_index_in_dim(conv_full, chip, axis=2, keepdims=False)

    order = ("norm_mix", "norm_mlp", "w_a_in", "conv_w", "w_a_out", "norm_kv", "w_kv", "w_q", "w_o", "w_up",
             "w_down", "norm_final")
    delta, new_m, new_v = {}, {}, {}
    vec_names = ("norm_mix", "norm_mlp", "norm_kv", "norm_final")
    rows_of = lambda a: a.reshape(-1, D)
    vw, vg, vm_, vv = (jnp.concatenate([rows_of(t[k]) for k in vec_names], axis=0) for t in (w, grads, m, v))
    n_vec = vw.shape[0]
    vpad = (-n_vec) % 8
    padrows = lambda a: jnp.pad(a, ((0, vpad), (0, 0)))
    vd, vnm, vnv = _adamw("adamw_norms", padrows(vw), padrows(vg), padrows(vm_), padrows(vv))
    off = 0
    for k in vec_names:
        r = rows_of(w[k]).shape[0]
        delta[k] = vd[off:off + r].reshape(w[k].shape)
        new_m[k] = vnm[off:off + r].reshape(w[k].shape)
        new_v[k] = vnv[off:off + r].reshape(w[k].shape)
        off += r
    for k in BIG + ("conv_w",):
        shape = w[k].shape
        two_d = lambda a: a.reshape(-1, shape[-1])
        if k == "conv_w":
            cpad = (-n_a * taps) % 8
            two_d = lambda a: jnp.pad(a.reshape(-1, shape[-1]), ((0, cpad), (0, 0)))
        d, nm, nv = _adamw(f"adamw_{k}", two_d(w[k]), two_d(grads[k]), two_d(m[k]), two_d(v[k]))
        rows = shape[0] * shape[1] if len(shape) == 3 else shape[0]
        delta[k], new_m[k], new_v[k] = (t[:rows].reshape(shape) for t in (d, nm, nv))
    fix = lambda k, a: a[0] if k == "w_kv" else a
    return (loss, grad_x, *[fix(k, grads[k]).reshape(fix(k, w[k]).shape) for k in order],
            *[fix(k, delta[k]) for k in order], *[fix(k, new_m[k]) for k in order],
            *[fix(k, new_v[k]) for k in order])
```

```python
import functools

import jax
import jax.numpy as jnp
from jax import lax
from jax.experimental import pallas as pl
from jax.experimental.pallas import tpu as pltpu

F32 = jnp.float32
BF16 = jnp.bfloat16
MESH = pl.DeviceIdType.MESH

EPS = 1e-5
PATTERNS = ((128, 1), (512, 4), (2048, 16))
HEAD_DIM = 64
ALIBI_MAX_BIAS = 8.0
NEG_INF = -1e30
ATT_BLK = 128
N_CHIPS = 4
LANES = 128
VMEM_LIMIT = 56 * 1024 * 1024

ADAM_LR = 0.001
ADAM_B1 = 0.9
ADAM_B2 = 0.999
ADAM_EPS = 1e-08
ADAM_WD = 0.01
ADAM_STEP = 10


def _params(n_grid_axes):
    return pltpu.CompilerParams(dimension_semantics=("arbitrary",) * n_grid_axes, vmem_limit_bytes=VMEM_LIMIT)


def _dot(a, b):
    return jnp.dot(a, b, preferred_element_type=F32)


def _dot_nt(a, b):
    return lax.dot_general(a, b, (((1,), (1,)), ((), ())), preferred_element_type=F32)


def _dot_tn(a, b):
    return lax.dot_general(a, b, (((0,), (0,)), ((), ())), preferred_element_type=F32)


def _relu2(a):
    return jnp.square(jnp.maximum(a, 0.0))


def _rms(hf, g):
    y = hf * lax.rsqrt(jnp.mean(hf * hf, axis=-1, keepdims=True) + EPS)
    return y * g


def _rms_bwd(hf, g, dn):
    rstd = lax.rsqrt(jnp.mean(hf * hf, axis=-1, keepdims=True) + EPS)
    xhat = hf * rstd
    dg = jnp.sum(dn * xhat, axis=0, keepdims=True)
    dx = dn * g
    dh = rstd * (dx - xhat * jnp.mean(dx * xhat, axis=-1, keepdims=True))
    return dh, dg


def _pieces(seg_widths, chunk_width, max_width):
    total = sum(seg_widths)
    cuts = {0, total}
    acc = 0
    for w in seg_widths:
        cuts.add(acc)
        acc += w
    cuts.update(range(0, total, chunk_width))
    cuts = sorted(cuts)
    fine = []
    for lo, hi in zip(cuts[:-1], cuts[1:]):
        while hi - lo > max_width:
            fine.append((lo, lo + max_width))
            lo += max_width
        fine.append((lo, hi))
    out = []
    for lo, hi in fine:
        acc = 0
        for s, w in enumerate(seg_widths):
            if lo < acc + w:
                break
            acc += w
        out.append((s, lo - acc, lo // chunk_width, lo % chunk_width, hi - lo))
    return out


def _relu2_bf16(a):
    return _relu2(a.astype(F32)).astype(BF16)


def _to_bf16(a):
    return a.astype(BF16)


def _norm_mm(name, h, g, wg, layer, planes, out_dtype, tm):
    T, D = h.shape
    cw = wg.shape[3]
    N = N_CHIPS * cw
    pw = N // planes
    pieces = _pieces([pw] * planes, cw, 512)

    def body(h_ref, g_ref, w_ref, n_ref, o_ref):
        n = _rms(h_ref[...], g_ref[...]).astype(BF16)
        n_ref[...] = n
        for s, a0, ch, b0, wd in pieces:
            o_ref[s, :, a0:a0 + wd] = _dot(n, w_ref[ch, :, b0:b0 + wd]).astype(out_dtype)

    return pl.pallas_call(
        body, name=name, grid=(T // tm,),
        in_specs=[pl.BlockSpec((tm, D), lambda i: (i, 0)),
                  pl.BlockSpec((1, D), lambda i: (0, 0)),
                  pl.BlockSpec((N_CHIPS, None, D, cw), lambda i: (0, layer, 0, 0))],
        out_specs=[pl.BlockSpec((tm, D), lambda i: (i, 0)),
                   pl.BlockSpec((planes, tm, pw), lambda i: (0, i, 0))],
        out_shape=[jax.ShapeDtypeStruct((T, D), BF16), jax.ShapeDtypeStruct((planes, T, pw), out_dtype)],
        compiler_params=_params(1))(h, g, wg)


def _mm_res_rows(name, a, wg, layer, h, act, tm):
    T = a.shape[0]
    rk, D = wg.shape[2], wg.shape[3]

    def body(a_ref, w_ref, h_ref, o_ref):
        acc = h_ref[...]
        for k in range(N_CHIPS):
            acc = acc + _dot(act(a_ref[:, k * rk:(k + 1) * rk]), w_ref[k])
        o_ref[...] = acc

    return pl.pallas_call(
        body, name=name, grid=(T // tm,),
        in_specs=[pl.BlockSpec((tm, N_CHIPS * rk), lambda i: (i, 0)),
                  pl.BlockSpec((N_CHIPS, None, rk, D), lambda i: (0, layer, 0, 0)),
                  pl.BlockSpec((tm, D), lambda i: (i, 0))],
        out_specs=pl.BlockSpec((tm, D), lambda i: (i, 0)),
        out_shape=jax.ShapeDtypeStruct((T, D), F32),
        compiler_params=_params(1))(a, wg, h)


def _mm_res_cols(name, a, wg, layer, h, tm):
    T, K = a.shape
    cw = wg.shape[3]
    D = N_CHIPS * cw

    def body(a_ref, w_ref, h_ref, o_ref):
        a16 = a_ref[...].astype(BF16)
        for j in range(N_CHIPS):
            o_ref[:, j * cw:(j + 1) * cw] = h_ref[:, j * cw:(j + 1) * cw] + _dot(a16, w_ref[j])

    return pl.pallas_call(
        body, name=name, grid=(T // tm,),
        in_specs=[pl.BlockSpec((tm, K), lambda i: (i, 0)),
                  pl.BlockSpec((N_CHIPS, None, K, cw), lambda i: (0, layer, 0, 0)),
                  pl.BlockSpec((tm, D), lambda i: (i, 0))],
        out_specs=pl.BlockSpec((tm, D), lambda i: (i, 0)),
        out_shape=jax.ShapeDtypeStruct((T, D), F32),
        compiler_params=_params(1))(a, wg, h)


CONV_ROWS = 256
CONV_HALO = 16


def _conv_shifted(ext, k, r0, rows):
    rolled = pltpu.roll(ext, k, 0)[CONV_HALO:]
    t = r0 + lax.broadcasted_iota(jnp.int32, rolled.shape, 0)
    return jnp.where(t >= k, rolled, 0.0)


def _conv_ahead(ext, k, r0, rows, S):
    rolled = pltpu.roll(ext, rows + CONV_HALO - k, 0)[:rows]
    t = r0 + lax.broadcasted_iota(jnp.int32, rolled.shape, 0)
    return jnp.where(t + k < S, rolled, 0.0)


def _conv_fwd(name, bcu, cwg, layer, tc):
    _, B, S, D = bcu.shape
    cwc = cwg.shape[3]
    per_chunk = cwc // tc
    R = min(CONV_ROWS, S)

    def body(x_ref, w_ref, z_ref):
        w = [w_ref[k:k + 1, :] for k in range(3)]

        def step(i, carry):
            r0 = pl.multiple_of(i * R, R)
            h0 = pl.multiple_of(jnp.maximum(r0 - CONV_HALO, 0), CONV_HALO)
            ld = lambda p, start, rows: x_ref[p, pl.ds(start, rows), :].astype(F32)
            cu = jnp.concatenate([ld(1, h0, CONV_HALO) * ld(2, h0, CONV_HALO), ld(1, r0, R) * ld(2, r0, R)], axis=0)
            conv = w[0] * cu[CONV_HALO:]
            conv = conv + w[1] * _conv_shifted(cu, 1, r0, R)
            conv = conv + w[2] * _conv_shifted(cu, 2, r0, R)
            z_ref[pl.ds(r0, R), :] = (ld(0, r0, R) * conv).astype(BF16)
            return carry

        lax.fori_loop(0, S // R, step, 0)

    return pl.pallas_call(
        body, name=name, grid=(B, D // tc),
        in_specs=[pl.BlockSpec((3, None, S, tc), lambda b, j: (0, b, 0, j)),
                  pl.BlockSpec((None, None, 3, tc), lambda b, j: (j // per_chunk, layer, 0, j % per_chunk))],
        out_specs=pl.BlockSpec((None, S, tc), lambda b, j: (b, 0, j)),
        out_shape=jax.ShapeDtypeStruct((B, S, D), BF16),
        compiler_params=_params(2))(bcu, cwg)


def _conv_bwd(name, bcu, dz, cwg, layer, tc):
    _, B, S, D = bcu.shape
    cwc = cwg.shape[3]
    per_chunk = cwc // tc
    R = min(CONV_ROWS, S)

    def body(x_ref, dz_ref, w_ref, d_ref, dw_ref):
        w = [w_ref[k:k + 1, :] for k in range(3)]

        @pl.when(pl.program_id(1) == 0)
        def _():
            dw_ref[...] = jnp.zeros_like(dw_ref)

        def step(i, carry):
            r0 = pl.multiple_of(i * R, R)
            h0 = pl.multiple_of(jnp.maximum(r0 - CONV_HALO, 0), CONV_HALO)
            a0 = pl.multiple_of(jnp.minimum(r0 + R, S - CONV_HALO), CONV_HALO)
            ld = lambda p, start, rows: x_ref[p, pl.ds(start, rows), :].astype(F32)
            b, c, u = ld(0, r0, R), ld(1, r0, R), ld(2, r0, R)
            dz = dz_ref[pl.ds(r0, R), :]
            cu = jnp.concatenate([ld(1, h0, CONV_HALO) * ld(2, h0, CONV_HALO), c * u], axis=0)
            cu1 = _conv_shifted(cu, 1, r0, R)
            cu2 = _conv_shifted(cu, 2, r0, R)
            conv = w[0] * (c * u) + w[1] * cu1 + w[2] * cu2
            dconv = dz * b
            dca = jnp.concatenate([dconv, dz_ref[pl.ds(a0, CONV_HALO), :] * ld(0, a0, CONV_HALO)], axis=0)
            dcu = w[0] * dconv + w[1] * _conv_ahead(dca, 1, r0, R, S) + w[2] * _conv_ahead(dca, 2, r0, R, S)
            d_ref[0, pl.ds(r0, R), :] = (dz * conv).astype(BF16)
            d_ref[1, pl.ds(r0, R), :] = (dcu * u).astype(BF16)
            d_ref[2, pl.ds(r0, R), :] = (dcu * c).astype(BF16)
            return (carry[0] + jnp.sum(dconv * (c * u), axis=0, keepdims=True),
                    carry[1] + jnp.sum(dconv * cu1, axis=0, keepdims=True),
                    carry[2] + jnp.sum(dconv * cu2, axis=0, keepdims=True))

        zero = jnp.zeros((1, tc), F32)
        s0, s1, s2 = lax.fori_loop(0, S // R, step, (zero, zero, zero))
        for k, sk in enumerate((s0, s1, s2)):
            dw_ref[k:k + 1, :] += sk

    return pl.pallas_call(
        body, name=name, grid=(D // tc, B),
        in_specs=[pl.BlockSpec((3, None, S, tc), lambda j, b: (0, b, 0, j)),
                  pl.BlockSpec((None, S, tc), lambda j, b: (b, 0, j)),
                  pl.BlockSpec((None, None, 3, tc), lambda j, b: (j // per_chunk, layer, 0, j % per_chunk))],
        out_specs=[pl.BlockSpec((3, None, S, tc), lambda j, b: (0, b, 0, j)),
                   pl.BlockSpec((3, tc), lambda j, b: (0, j))],
        out_shape=[jax.ShapeDtypeStruct((3, B, S, D), BF16), jax.ShapeDtypeStruct((3, D), F32)],
        compiler_params=_params(2))(bcu, dz, cwg)


def _att_geometry(dil):
    qi = lax.broadcasted_iota(jnp.int32, (ATT_BLK, 2 * ATT_BLK), 0)
    ci = lax.broadcasted_iota(jnp.int32, (ATT_BLK, 2 * ATT_BLK), 1)
    j = ATT_BLK + qi - ci
    inwin = (j >= 0) & (j <= ATT_BLK)
    dist = (dil * j).astype(F32)
    return ci, inwin, dist


def _att_views(B, S, dil, n_heads):
    hp = n_heads * HEAD_DIM // LANES
    return S // dil, hp


def _attn_fwd(name, q, kv, slopes, g, dil, n_heads):
    B, S, CQ = q.shape
    Sd, HP = _att_views(B, S, dil, n_heads)
    NQ = CQ // LANES
    nb = Sd // ATT_BLK
    scale = HEAD_DIM ** -0.5
    qv = q.reshape(B, Sd, dil * CQ)
    kvv = kv.reshape(B, Sd, dil * 2 * CQ)

    def body(sl_ref, q_ref, k_ref, v_ref, o_ref, l_ref):
        hp = pl.program_id(2)
        lane = lax.broadcasted_iota(jnp.int32, (1, LANES), 1)
        ci, inwin, dist = _att_geometry(dil)

        def step(n, carry):
            q0 = pl.multiple_of(n * ATT_BLK, ATT_BLK)
            p0 = pl.multiple_of(jnp.maximum(n - 1, 0) * ATT_BLK, ATT_BLK)
            valid = inwin & (ci >= jnp.where(n > 0, 0, ATT_BLK))
            qb = q_ref[pl.ds(q0, ATT_BLK), :]
            kc = jnp.concatenate([k_ref[pl.ds(p0, ATT_BLK), :], k_ref[pl.ds(q0, ATT_BLK), :]], axis=0)
            vc = jnp.concatenate([v_ref[pl.ds(p0, ATT_BLK), :], v_ref[pl.ds(q0, ATT_BLK), :]], axis=0)
            o_acc = jnp.zeros((ATT_BLK, LANES), F32)
            l_acc = jnp.zeros((ATT_BLK, LANES), F32)
            for hh in range(2):
                hm = (lane >= hh * HEAD_DIM) & (lane < (hh + 1) * HEAD_DIM)
                s = _dot_nt(jnp.where(hm, qb, jnp.zeros_like(qb)), kc) * scale
                s = jnp.where(valid, s - sl_ref[2 * hp + hh] * dist, NEG_INF)
                m = jnp.max(s, axis=-1, keepdims=True)
                p = jnp.exp(s - m)
                l = jnp.sum(p, axis=-1, keepdims=True)
                o_acc = o_acc + _dot(p.astype(BF16), jnp.where(hm, vc, jnp.zeros_like(vc))) / l
                l_acc = jnp.where(hm, m + jnp.log(l), l_acc)
            o_ref[pl.ds(q0, ATT_BLK), :] = o_acc
            l_ref[pl.ds(q0, ATT_BLK), :] = l_acc
            return carry

        lax.fori_loop(0, nb, step, 0)

    blk = (None, Sd, LANES)
    og, lg = pl.pallas_call(
        body, name=name, grid=(B, dil, HP),
        in_specs=[pl.BlockSpec(memory_space=pltpu.SMEM),
                  pl.BlockSpec(blk, lambda b, r, hp: (b, 0, r * NQ + g * HP + hp)),
                  pl.BlockSpec(blk, lambda b, r, hp: (b, 0, r * 2 * NQ + g * 2 * HP + hp)),
                  pl.BlockSpec(blk, lambda b, r, hp: (b, 0, r * 2 * NQ + g * 2 * HP + HP + hp))],
        out_specs=[pl.BlockSpec(blk, lambda b, r, hp: (b, 0, r * HP + hp)),
                   pl.BlockSpec(blk, lambda b, r, hp: (b, 0, r * HP + hp))],
        out_shape=[jax.ShapeDtypeStruct((B, Sd, dil * HP * LANES), F32)] * 2,
        compiler_params=_params(3))(slopes, qv, kvv, kvv)
    return og.reshape(B, S, HP * LANES), lg.reshape(B, S, HP * LANES)


def _attn_combine(name, os, ls, tm):
    T, C = os[0].shape

    def body(o0, o1, o2, l0, l1, l2, o_ref, lse_ref):
        a, b, c = l0[...], l1[...], l2[...]
        m = jnp.maximum(jnp.maximum(a, b), c)
        ea, eb, ec = jnp.exp(a - m), jnp.exp(b - m), jnp.exp(c - m)
        z = ea + eb + ec
        o_ref[...] = (ea / z) * o0[...] + (eb / z) * o1[...] + (ec / z) * o2[...]
        lse_ref[...] = m + jnp.log(z)

    spec = pl.BlockSpec((tm, C), lambda i: (i, 0))
    return pl.pallas_call(
        body, name=name, grid=(T // tm,), in_specs=[spec] * 6, out_specs=[spec] * 2,
        out_shape=[jax.ShapeDtypeStruct((T, C), F32)] * 2, compiler_params=_params(1))(*os, *ls)


def _attn_bwd(name, q, kv, slopes, o, lse, do, g, dil, n_heads, dkv_prev):
    B, S, CQ = q.shape
    Sd, HP = _att_views(B, S, dil, n_heads)
    NQ = CQ // LANES
    nb = Sd // ATT_BLK
    scale = HEAD_DIM ** -0.5
    qv = q.reshape(B, Sd, dil * CQ)
    kvv = kv.reshape(B, Sd, dil * 2 * CQ)
    gshape = (B, Sd, dil * HP * LANES)
    n_prev = 0 if dkv_prev is None else 2

    def body(sl_ref, q_ref, k_ref, v_ref, o_ref, lse_ref, do_ref, *rest):
        dq_ref, dk_ref, dv_ref = rest[n_prev:]
        hp = pl.program_id(2)
        lane = lax.broadcasted_iota(jnp.int32, (1, LANES), 1)
        ci, inwin, dist = _att_geometry(dil)
        if n_prev:
            dk_ref[...] = rest[0][...]
            dv_ref[...] = rest[1][...]
        else:
            dk_ref[...] = jnp.zeros_like(dk_ref)
            dv_ref[...] = jnp.zeros_like(dv_ref)

        def step(n, carry):
            q0 = pl.multiple_of(n * ATT_BLK, ATT_BLK)
            p0 = pl.multiple_of(jnp.maximum(n - 1, 0) * ATT_BLK, ATT_BLK)
            valid = inwin & (ci >= jnp.where(n > 0, 0, ATT_BLK))
            qb = q_ref[pl.ds(q0, ATT_BLK), :]
            kc = jnp.concatenate([k_ref[pl.ds(p0, ATT_BLK), :], k_ref[pl.ds(q0, ATT_BLK), :]], axis=0)
            vc = jnp.concatenate([v_ref[pl.ds(p0, ATT_BLK), :], v_ref[pl.ds(q0, ATT_BLK), :]], axis=0)
            dob = do_ref[pl.ds(q0, ATT_BLK), :]
            prod = dob * o_ref[pl.ds(q0, ATT_BLK), :]
            lseb = lse_ref[pl.ds(q0, ATT_BLK), :]
            dob16 = dob.astype(BF16)
            dq = jnp.zeros((ATT_BLK, LANES), F32)
            dk = jnp.zeros((2 * ATT_BLK, LANES), F32)
            dv = jnp.zeros((2 * ATT_BLK, LANES), F32)
            for hh in range(2):
                hm = (lane >= hh * HEAD_DIM) & (lane < (hh + 1) * HEAD_DIM)
                qm = jnp.where(hm, qb, jnp.zeros_like(qb))
                dom = jnp.where(hm, dob16, jnp.zeros_like(dob16))
                delta = jnp.sum(jnp.where(hm, prod, 0.0), axis=-1, keepdims=True)
                lse_h = jnp.max(jnp.where(hm, lseb, -jnp.inf), axis=-1, keepdims=True)
                s = _dot_nt(qm, kc) * scale
                s = jnp.where(valid, s - sl_ref[2 * hp + hh] * dist, NEG_INF)
                p = jnp.exp(s - lse_h)
                dp = _dot_nt(dom, vc)
                ds = (p * (dp - delta) * scale)
                dq = dq + _dot(ds.astype(BF16), jnp.where(hm, kc, jnp.zeros_like(kc)))
                dk = dk + _dot(ds.T.astype(BF16), qm)
                dv = dv + _dot(p.T.astype(BF16), dom)
            dq_ref[pl.ds(q0, ATT_BLK), :] = dq.astype(BF16)
            dk_ref[pl.ds(p0, ATT_BLK), :] += dk[:ATT_BLK]
            dk_ref[pl.ds(q0, ATT_BLK), :] += dk[ATT_BLK:]
            dv_ref[pl.ds(p0, ATT_BLK), :] += dv[:ATT_BLK]
            dv_ref[pl.ds(q0, ATT_BLK), :] += dv[ATT_BLK:]
            return carry

        lax.fori_loop(0, nb, step, 0)

    blk = (None, Sd, LANES)
    gspec = pl.BlockSpec(blk, lambda b, r, hp: (b, 0, r * HP + hp))
    prev = [] if dkv_prev is None else [t.reshape(gshape) for t in dkv_prev]
    dq, dk, dv = pl.pallas_call(
        body, name=name, grid=(B, dil, HP),
        in_specs=[pl.BlockSpec(memory_space=pltpu.SMEM),
                  pl.BlockSpec(blk, lambda b, r, hp: (b, 0, r * NQ + g * HP + hp)),
                  pl.BlockSpec(blk, lambda b, r, hp: (b, 0, r * 2 * NQ + g * 2 * HP + hp)),
                  pl.BlockSpec(blk, lambda b, r, hp: (b, 0, r * 2 * NQ + g * 2 * HP + HP + hp)),
                  gspec, gspec, gspec] + [gspec] * n_prev,
        out_specs=[gspec] * 3,
        out_shape=[jax.ShapeDtypeStruct(gshape, BF16), jax.ShapeDtypeStruct(gshape, F32),
                   jax.ShapeDtypeStruct(gshape, F32)],
        compiler_params=_params(3))(slopes, qv, kvv, kvv, o.reshape(gshape), lse.reshape(gshape),
                                    do.reshape(gshape), *prev)
    C = HP * LANES
    return dq.reshape(B * S, C), dk.reshape(B * S, C), dv.reshape(B * S, C)


def _final_loss(name, h, g, target, tm):
    T, D = h.shape

    def body(h_ref, g_ref, t_ref, loss_ref, dh_ref, dg_ref):
        hf = h_ref[...]
        gv = g_ref[...]
        rstd = lax.rsqrt(jnp.mean(hf * hf, axis=-1, keepdims=True) + EPS)
        xhat = hf * rstd
        err = xhat * gv - t_ref[...]
        part = 0.5 * jnp.sum(jnp.mean(err * err, axis=-1, keepdims=True), axis=0, keepdims=True)
        dy = err * (1.0 / D)
        dg = jnp.sum(dy * xhat, axis=0, keepdims=True)
        dx = dy * gv
        dh_ref[...] = rstd * (dx - xhat * jnp.mean(dx * xhat, axis=-1, keepdims=True))

        @pl.when(pl.program_id(0) == 0)
        def _():
            loss_ref[...] = part
            dg_ref[...] = dg

        @pl.when(pl.program_id(0) > 0)
        def _():
            loss_ref[...] += part
            dg_ref[...] += dg

    return pl.pallas_call(
        body, name=name, grid=(T // tm,),
        in_specs=[pl.BlockSpec((tm, D), lambda i: (i, 0)), pl.BlockSpec((1, D), lambda i: (0, 0)),
                  pl.BlockSpec((tm, D), lambda i: (i, 0))],
        out_specs=[pl.BlockSpec((1, 1), lambda i: (0, 0)), pl.BlockSpec((tm, D), lambda i: (i, 0)),
                   pl.BlockSpec((1, D), lambda i: (0, 0))],
        out_shape=[jax.ShapeDtypeStruct((1, 1), F32), jax.ShapeDtypeStruct((T, D), F32),
                   jax.ShapeDtypeStruct((1, D), F32)],
        compiler_params=_params(1))(h, g, target)


def _nt_rows(name, dh, wg, layer, a_mul, out_dtype, tm):
    T, D = dh.shape
    rk = wg.shape[2]
    N = N_CHIPS * rk
    with_a = a_mul is not None

    def body(dh_ref, w_ref, *rest):
        o_ref = rest[-1]
        d16 = dh_ref[...].astype(BF16)
        for ch in range(N_CHIPS):
            r = _dot_nt(d16, w_ref[ch])
            if with_a:
                r = r * (2.0 * jnp.maximum(rest[0][:, ch * rk:(ch + 1) * rk].astype(F32), 0.0))
            o_ref[:, ch * rk:(ch + 1) * rk] = r.astype(out_dtype)

    in_specs = [pl.BlockSpec((tm, D), lambda i: (i, 0)),
                pl.BlockSpec((N_CHIPS, None, rk, D), lambda i: (0, layer, 0, 0))]
    args = [dh, wg]
    if with_a:
        in_specs.append(pl.BlockSpec((tm, N), lambda i: (i, 0)))
        args.append(a_mul)
    return pl.pallas_call(
        body, name=name, grid=(T // tm,), in_specs=in_specs,
        out_specs=pl.BlockSpec((tm, N), lambda i: (i, 0)),
        out_shape=jax.ShapeDtypeStruct((T, N), out_dtype),
        compiler_params=_params(1))(*args)


def _nt_cols(name, ysegs, wg, layer, tm, norm):
    Nw, cw = wg.shape[2], wg.shape[3]
    widths = [bs[-1] for _, bs, _ in ysegs]
    pieces = _pieces(widths, cw, 1024)
    ns = len(ysegs)
    T = norm[0].shape[0] if norm is not None else ysegs[0][0].shape[-2]

    def body(*refs):
        y_refs = refs[:ns]
        w_ref = refs[ns]
        acc = refs[-1]
        for n, (s, a0, ch, b0, wd) in enumerate(pieces):
            d = _dot_nt(y_refs[s][:, a0:a0 + wd].astype(BF16), w_ref[ch, :, b0:b0 + wd])
            if n == 0:
                acc[...] = d
            else:
                acc[...] += d
        if norm is None:
            refs[ns + 1][...] = acc[...]
        else:
            h_ref, g_ref, dhin_ref, out_ref, dg_ref = refs[ns + 1:ns + 6]
            dh_c, dg = _rms_bwd(h_ref[...], g_ref[...], acc[...])
            out_ref[...] = dhin_ref[...] + dh_c

            @pl.when(pl.program_id(0) == 0)
            def _():
                dg_ref[...] = dg

            @pl.when(pl.program_id(0) > 0)
            def _():
                dg_ref[...] += dg

    in_specs = [pl.BlockSpec(bs, im) for _, bs, im in ysegs]
    in_specs.append(pl.BlockSpec((N_CHIPS, None, Nw, cw), lambda i: (0, layer, 0, 0)))
    args = [a for a, _, _ in ysegs] + [wg]
    row = pl.BlockSpec((tm, Nw), lambda i: (i, 0))
    vec = pl.BlockSpec((1, Nw), lambda i: (0, 0))
    if norm is None:
        out_specs = row
        out_shape = jax.ShapeDtypeStruct((T, Nw), F32)
    else:
        in_specs += [row, vec, row]
        args += list(norm)
        out_specs = [row, vec]
        out_shape = [jax.ShapeDtypeStruct((T, Nw), F32), jax.ShapeDtypeStruct((1, Nw), F32)]
    return pl.pallas_call(
        body, name=name, grid=(T // tm,), in_specs=in_specs, out_specs=out_specs, out_shape=out_shape,
        scratch_shapes=[pltpu.VMEM((tm, Nw), F32)], compiler_params=_params(1))(*args)


def _tn(name, x, x_act, ysegs, cw, cols_layout, tmm, tt):
    T, M = x.shape
    widths = [bs[-1] for _, bs, _ in ysegs]
    N = sum(widths)
    pieces = _pieces(widths, cw if cols_layout else N, 1024)
    ns = len(ysegs)

    def body(x_ref, *refs):
        y_refs = refs[:ns]
        o_ref = refs[ns]

        @pl.when(pl.program_id(1) == 0)
        def _():
            o_ref[...] = jnp.zeros_like(o_ref)

        xt = x_act(x_ref[...])
        for s, a0, ch, b0, wd in pieces:
            d = _dot_tn(xt, y_refs[s][:, a0:a0 + wd].astype(BF16))
            if cols_layout:
                o_ref[ch, :, b0:b0 + wd] += d
            else:
                o_ref[:, b0:b0 + wd] += d

    in_specs = [pl.BlockSpec((tt, tmm), lambda m, t: (t, m))] + [pl.BlockSpec(bs, im) for _, bs, im in ysegs]
    if cols_layout:
        out_specs = pl.BlockSpec((N_CHIPS, tmm, cw), lambda m, t: (0, m, 0))
        out_shape = jax.ShapeDtypeStruct((N_CHIPS, M, cw), F32)
    else:
        out_specs = pl.BlockSpec((tmm, N), lambda m, t: (m, 0))
        out_shape = jax.ShapeDtypeStruct((M, N), F32)
    return pl.pallas_call(
        body, name=name, grid=(M // tmm, T // tt), in_specs=in_specs, out_specs=out_specs, out_shape=out_shape,
        compiler_params=_params(2))(x, *[a for a, _, _ in ysegs])


def _seg2d(a, t_rows, grid_rank):
    w = a.shape[1]
    if grid_rank == 1:
        return (a, (t_rows, w), lambda i: (i, 0))
    return (a, (t_rows, w), lambda m, t: (t, 0))


def _seg_plane(a, plane, t_rows, grid_rank):
    w = a.shape[2]
    if grid_rank == 1:
        return (a, (None, t_rows, w), lambda i: (plane, i, 0))
    return (a, (None, t_rows, w), lambda m, t: (plane, t, 0))


def _row_tile(rows, row_bytes, budget_bytes=2 * 1024 * 1024):
    t = rows
    while t * row_bytes > budget_bytes and t % 32 == 0:
        t //= 2
    return t


def _pair_add(name, layers, recv, place):
    L = len(layers)
    _, _, hr, c = layers[0].shape
    tr = _row_tile(hr, L * c * 4)

    def body(place_ref, *refs):
        r_ref, o_ref = refs[L], refs[L + 1]
        for l in range(L):
            o_ref[l] = (refs[l][...] + r_ref[l]).astype(BF16)

    stacked = pl.BlockSpec((None, L, tr, c), lambda q, i, pr: (q, 0, i, 0))
    grid_spec = pltpu.PrefetchScalarGridSpec(
        num_scalar_prefetch=1, grid=(N_CHIPS, hr // tr),
        in_specs=[pl.BlockSpec((None, None, tr, c), lambda q, i, pr: (q, pr[1], i, 0))] * L + [stacked],
        out_specs=stacked)
    return pl.pallas_call(body, name=name, grid_spec=grid_spec,
                          out_shape=jax.ShapeDtypeStruct((N_CHIPS, L, hr, c), BF16),
                          compiler_params=_params(2))(place, *layers, recv)


def _chip_add(name, part, slots, place):
    _, L, hr, c = part.shape
    tr = _row_tile(hr, L * c * 4)

    def body(place_ref, own, s1, s2, s3, o_ref):
        f = lambda r: r[...].astype(F32)
        o_ref[...] = ((f(own) + f(s1)) + f(s2)) + f(s3)

    def slot(k):
        return pl.BlockSpec((None, L, tr, c), lambda i, pr: ((pr[0] + k) % N_CHIPS, 0, i, 0))

    grid_spec = pltpu.PrefetchScalarGridSpec(
        num_scalar_prefetch=1, grid=(hr // tr,),
        in_specs=[slot(0), slot(1), slot(2), slot(3)],
        out_specs=pl.BlockSpec((L, None, tr, c), lambda i, pr: (0, pr[1], i, 0)))
    return pl.pallas_call(body, name=name, grid_spec=grid_spec,
                          out_shape=jax.ShapeDtypeStruct((L, 2, hr, c), F32),
                          compiler_params=_params(1))(place, part, slots, slots, slots)


def _adamw(name, w, g, m, v):
    rows, cols = w.shape
    tr = _row_tile(rows, cols * 4, 1024 * 1024)

    def body(w_ref, g_ref, m_ref, v_ref, d_ref, nm_ref, nv_ref):
        gv = g_ref[...]
        nm = ADAM_B1 * m_ref[...] + (1.0 - ADAM_B1) * gv
        nv = ADAM_B2 * v_ref[...] + (1.0 - ADAM_B2) * jnp.square(gv)
        m_hat = nm / (1.0 - ADAM_B1 ** ADAM_STEP)
        v_hat = nv / (1.0 - ADAM_B2 ** ADAM_STEP)
        d_ref[...] = -ADAM_LR * (m_hat / (jnp.sqrt(v_hat) + ADAM_EPS) + ADAM_WD * w_ref[...])
        nm_ref[...] = nm
        nv_ref[...] = nv

    spec = pl.BlockSpec((tr, cols), lambda i: (i, 0))
    return pl.pallas_call(
        body, name=name, grid=(rows // tr,), in_specs=[spec] * 4, out_specs=[spec] * 3,
        out_shape=[jax.ShapeDtypeStruct((rows, cols), F32)] * 3, compiler_params=_params(1))(w, g, m, v)


ANY = pl.BlockSpec(memory_space=pl.ANY)


def _place():
    x, y, c = lax.axis_index("x"), lax.axis_index("y"), lax.axis_index("c")
    chips = [(1 - x, y), (x, 1 - y), (1 - x, 1 - y)]
    return x, y, c, chips


def _all_gather(chunks):
    n = len(chunks)

    def body(*refs):
        ins, outs = refs[:n], refs[n:2 * n]
        ssem, rsem, fssem, frsem, osem_s, osem_r = refs[2 * n:]
        x, y, c, chips = _place()
        p = 2 * x + y
        sends = []
        for t in range(n):
            hr = ins[t].shape[0] // 2
            mine = pl.ds(pl.multiple_of(c * hr, 8), hr)
            own = pltpu.make_async_remote_copy(
                src_ref=ins[t], dst_ref=outs[t].at[p], send_sem=osem_s.at[t], recv_sem=osem_r.at[t],
                device_id=(x, y, 1 - c), device_id_type=MESH)
            own.start()
            sends.append(own)
            for k, (qx, qy) in enumerate(chips):
                cp = pltpu.make_async_remote_copy(
                    src_ref=ins[t].at[mine], dst_ref=outs[t].at[p, mine], send_sem=ssem.at[t, k],
                    recv_sem=rsem.at[t, k], device_id=(qx, qy, c), device_id_type=MESH)
                cp.start()
                sends.append(cp)
        for t in range(n):
            hr = ins[t].shape[0] // 2
            mine = pl.ds(pl.multiple_of(c * hr, 8), hr)
            for k, (qx, qy) in enumerate(chips):
                q = 2 * qx + qy
                landed = outs[t].at[q, mine]
                pltpu.make_async_remote_copy(
                    src_ref=landed, dst_ref=landed, send_sem=ssem.at[t, k], recv_sem=rsem.at[t, k],
                    device_id=(qx, qy, c), device_id_type=MESH).wait_recv()
                fw = pltpu.make_async_remote_copy(
                    src_ref=landed, dst_ref=landed, send_sem=fssem.at[t, k], recv_sem=frsem.at[t, k],
                    device_id=(x, y, 1 - c), device_id_type=MESH)
                fw.start()
                sends.append(fw)
        for t in range(n):
            hr = ins[t].shape[0] // 2
            theirs = pl.ds(pl.multiple_of((1 - c) * hr, 8), hr)
            for k, (qx, qy) in enumerate(chips):
                q = 2 * qx + qy
                passed = outs[t].at[q, theirs]
                pltpu.make_async_remote_copy(
                    src_ref=passed, dst_ref=passed, send_sem=fssem.at[t, k], recv_sem=frsem.at[t, k],
                    device_id=(x, y, 1 - c), device_id_type=MESH).wait_recv()
        for t in range(n):
            pltpu.make_async_remote_copy(
                src_ref=ins[t], dst_ref=outs[t].at[p], send_sem=osem_s.at[t], recv_sem=osem_r.at[t],
                device_id=(x, y, 1 - c), device_id_type=MESH).wait_recv()
        for cp in sends:
            cp.wait_send()

    return pl.pallas_call(
        body, name="gather_weights", in_specs=[ANY] * n, out_specs=[ANY] * n,
        out_shape=[jax.ShapeDtypeStruct((N_CHIPS,) + a.shape, a.dtype) for a in chunks],
        scratch_shapes=[pltpu.SemaphoreType.DMA((n, 3)), pltpu.SemaphoreType.DMA((n, 3)),
                        pltpu.SemaphoreType.DMA((n, 3)), pltpu.SemaphoreType.DMA((n, 3)),
                        pltpu.SemaphoreType.DMA((n,)), pltpu.SemaphoreType.DMA((n,))])(*chunks)


def _pair_exchange(layers_by_type):
    flat = [(o, l, a) for o, layers in enumerate(layers_by_type) for l, a in enumerate(layers)]
    n, n_out = len(flat), len(layers_by_type)

    def body(*refs):
        ins, outs = refs[:n], refs[n:n + n_out]
        ssem, rsem = refs[n + n_out:]
        x, y, c, _ = _place()
        cps = []
        for t, (o, l, _) in enumerate(flat):
            cp = pltpu.make_async_remote_copy(
                src_ref=ins[t].at[:, 1 - c], dst_ref=outs[o].at[:, l], send_sem=ssem.at[t], recv_sem=rsem.at[t],
                device_id=(x, y, 1 - c), device_id_type=MESH)
            cp.start()
            cps.append(cp)
        for cp in cps:
            cp.wait()

    return pl.pallas_call(
        body, name="grad_pair_exchange", in_specs=[ANY] * n, out_specs=[ANY] * n_out,
        out_shape=[jax.ShapeDtypeStruct((N_CHIPS, len(layers)) + layers[0].shape[2:], F32)
                   for layers in layers_by_type],
        scratch_shapes=[pltpu.SemaphoreType.DMA((n,)), pltpu.SemaphoreType.DMA((n,))])(*[a for _, _, a in flat])


def _chip_exchange(parts):
    n = len(parts)

    def body(*refs):
        ins, outs = refs[:n], refs[n:2 * n]
        ssem, rsem = refs[2 * n:]
        x, y, c, chips = _place()
        p = 2 * x + y
        sends = []
        for t in range(n):
            for k, (qx, qy) in enumerate(chips):
                cp = pltpu.make_async_remote_copy(
                    src_ref=ins[t].at[2 * qx + qy], dst_ref=outs[t].at[p], send_sem=ssem.at[t, k],
                    recv_sem=rsem.at[t, k], device_id=(qx, qy, c), device_id_type=MESH)
                cp.start()
                sends.append(cp)
        for t in range(n):
            for k, (qx, qy) in enumerate(chips):
                slot = outs[t].at[2 * qx + qy]
                pltpu.make_async_remote_copy(
                    src_ref=slot, dst_ref=slot, send_sem=ssem.at[t, k], recv_sem=rsem.at[t, k],
                    device_id=(qx, qy, c), device_id_type=MESH).wait_recv()
        for cp in sends:
            cp.wait_send()

    return pl.pallas_call(
        body, name="grad_chip_exchange", in_specs=[ANY] * n, out_specs=[ANY] * n,
        out_shape=[jax.ShapeDtypeStruct(a.shape, a.dtype) for a in parts],
        scratch_shapes=[pltpu.SemaphoreType.DMA((n, 3)), pltpu.SemaphoreType.DMA((n, 3))])(*parts)


def _pair_share(fulls):
    n = len(fulls)

    def body(*refs):
        outs = refs[n:2 * n]
        ssem, rsem = refs[2 * n:]
        x, y, c, _ = _place()
        sends = []
        for t in range(n):
            cp = pltpu.make_async_remote_copy(
                src_ref=outs[t].at[:, c], dst_ref=outs[t].at[:, c], send_sem=ssem.at[t], recv_sem=rsem.at[t],
                device_id=(x, y, 1 - c), device_id_type=MESH)
            cp.start()
            sends.append(cp)
        for t in range(n):
            theirs = outs[t].at[:, 1 - c]
            pltpu.make_async_remote_copy(
                src_ref=theirs, dst_ref=theirs, send_sem=ssem.at[t], recv_sem=rsem.at[t],
                device_id=(x, y, 1 - c), device_id_type=MESH).wait_recv()
        for cp in sends:
            cp.wait_send()

    return pl.pallas_call(
        body, name="grad_pair_share", in_specs=[ANY] * n, out_specs=[ANY] * n,
        out_shape=[jax.ShapeDtypeStruct(a.shape, a.dtype) for a in fulls],
        input_output_aliases={t: t for t in range(n)},
        scratch_shapes=[pltpu.SemaphoreType.DMA((n,)), pltpu.SemaphoreType.DMA((n,))])(*fulls)


def _small_allreduce(part):
    R, C = part.shape
    N_DEV = 8

    def body(in_ref, out_ref, slots, ssem, rsem):
        x, y, c, _ = _place()
        me = 4 * x + 2 * y + c
        sends = []
        for k in range(1, N_DEV):
            kx, ky, kc = (k >> 2) & 1, (k >> 1) & 1, k & 1
            peer = (1 - x if kx else x, 1 - y if ky else y, 1 - c if kc else c)
            cp = pltpu.make_async_remote_copy(
                src_ref=in_ref, dst_ref=slots.at[me], send_sem=ssem.at[k], recv_sem=rsem.at[k],
                device_id=peer, device_id_type=MESH)
            cp.start()
            sends.append(cp)
        slots[me] = in_ref[...]
        for k in range(1, N_DEV):
            kx, ky, kc = (k >> 2) & 1, (k >> 1) & 1, k & 1
            peer = (1 - x if kx else x, 1 - y if ky else y, 1 - c if kc else c)
            slot = slots.at[4 * peer[0] + 2 * peer[1] + peer[2]]
            pltpu.make_async_remote_copy(
                src_ref=slot, dst_ref=slot, send_sem=ssem.at[k], recv_sem=rsem.at[k],
                device_id=peer, device_id_type=MESH).wait_recv()
        acc = slots[0]
        for d in range(1, N_DEV):
            acc = acc + slots[d]
        out_ref[...] = acc
        for cp in sends:
            cp.wait_send()

    vm = pl.BlockSpec(memory_space=pltpu.VMEM)
    return pl.pallas_call(
        body, name="small_allreduce", in_specs=[vm], out_specs=vm,
        out_shape=jax.ShapeDtypeStruct((R, C), F32),
        scratch_shapes=[pltpu.VMEM((N_DEV, R, C), F32), pltpu.SemaphoreType.DMA((N_DEV,)),
                        pltpu.SemaphoreType.DMA((N_DEV,))])(part)


def _local_step(x, target, norm_mix, norm_mlp, norm_kv, norm_final, W, cwg):
    B, S, D = x.shape
    T = B * S
    n_heads = W["w_o"].shape[2] // HEAD_DIM
    C = n_heads * HEAD_DIM
    n_a = W["w_a_in"].shape[1]
    n_b = W["w_q"].shape[1]
    depth = n_a + n_b
    F = W["w_up"].shape[3] * N_CHIPS
    slopes = 2.0 ** (-ALIBI_MAX_BIAS * jnp.arange(1, n_heads + 1, dtype=F32) / n_heads)
    tm = min(512, T)
    row = lambda v: v.reshape(1, -1)

    h = x.reshape(T, D)
    saved = []
    kv = nkv = h_kv = None
    for l in range(depth):
        s = {"h_in": h}
        if l < n_a:
            s["n1"], bcu = _norm_mm(f"a_in_fwd{l}", h, row(norm_mix[l]), W["w_a_in"], l, 3, BF16, tm)
            s["bcu"] = bcu.reshape(3, B, S, D)
            s["z"] = _conv_fwd(f"conv_fwd{l}", s["bcu"], cwg, l, LANES).reshape(T, D)
            h = _mm_res_rows(f"a_out_fwd{l}", s["z"], W["w_a_out"], l, h, _to_bf16, tm)
        else:
            i = l - n_a
            if i == 0:
                h_kv = h
                nkv, kv = _norm_mm("kv_fwd", h, row(norm_kv), W["w_kv"], 0, 1, BF16, tm)
                kv = kv.reshape(B, S, 2 * 3 * C)
            s["n1"], q = _norm_mm(f"q_fwd{i}", h, row(norm_mix[l]), W["w_q"], i, 1, BF16, tm)
            s["q"] = q.reshape(B, S, 3 * C)
            os, ls = [], []
            for g, (window, dil) in enumerate(PATTERNS):
                og, lg = _attn_fwd(f"attn_fwd{i}_{g}", s["q"], kv, slopes, g, dil, n_heads)
                os.append(og.reshape(T, C))
                ls.append(lg.reshape(T, C))
            s["o"], s["lse"] = _attn_combine(f"attn_combine{i}", os, ls, tm)
            h = _mm_res_cols(f"o_fwd{i}", s["o"], W["w_o"], i, h, tm)
        s["h_mid"] = h
        s["n2"], a = _norm_mm(f"up_fwd{l}", h, row(norm_mlp[l]), W["w_up"], l, 1, BF16, tm)
        s["a"] = a.reshape(T, F)
        h = _mm_res_rows(f"down_fwd{l}", s["a"], W["w_down"], l, h, _relu2_bf16, tm)
        saved.append(s)

    loss, dh, dg_final = _final_loss("loss_head", h, row(norm_final), target.reshape(T, D), tm)

    gw = {k: [None] * W[k].shape[1] for k in W}
    g_mix, g_mlp = [None] * depth, [None] * depth
    g_conv = [None] * n_a
    dkv = None
    tt = min(512, T)
    for l in reversed(range(depth)):
        s = saved[l]
        da = _nt_rows(f"down_bwd{l}", dh, W["w_down"], l, s["a"], BF16, tm)
        gw["w_down"][l] = _tn(f"down_wgrad{l}", s["a"], _relu2_bf16, [_seg2d(dh, tt, 2)], None, False,
                              min(1024, F), tt).reshape(N_CHIPS, F // N_CHIPS, D)
        gw["w_up"][l] = _tn(f"up_wgrad{l}", s["n2"], _to_bf16, [_seg2d(da, tt, 2)], F // N_CHIPS, True,
                            min(512, D), tt)
        dh, g_mlp[l] = _nt_cols(f"up_bwd{l}", [_seg2d(da, tm, 1)], W["w_up"], l, tm,
                                (s["h_mid"], row(norm_mlp[l]), dh))
        if l < n_a:
            gw["w_a_out"][l] = _tn(f"a_out_wgrad{l}", s["z"], _to_bf16, [_seg2d(dh, tt, 2)], None, False,
                                   D, tt).reshape(N_CHIPS, D // N_CHIPS, D)
            dz = _nt_rows(f"a_out_bwd{l}", dh, W["w_a_out"], l, None, F32, tm)
            dbcu, g_conv[l] = _conv_bwd(f"conv_bwd{l}", s["bcu"], dz.reshape(B, S, D), cwg, l, LANES)
            dbcu = dbcu.reshape(3, T, D)
            gw["w_a_in"][l] = _tn(f"a_in_wgrad{l}", s["n1"], _to_bf16,
                                  [_seg_plane(dbcu, p, tt, 2) for p in range(3)], 3 * D // N_CHIPS, True,
                                  min(512, D), tt)
            dh, g_mix[l] = _nt_cols(f"a_in_bwd{l}", [_seg_plane(dbcu, p, tm, 1) for p in range(3)],
                                    W["w_a_in"], l, tm, (s["h_in"], row(norm_mix[l]), dh))
        else:
            i = l - n_a
            gw["w_o"][i] = _tn(f"o_wgrad{i}", s["o"], _to_bf16, [_seg2d(dh, tt, 2)], D // N_CHIPS, True, C, tt)
            do = _nt_cols(f"o_bwd{i}", [_seg2d(dh, tm, 1)], W["w_o"], i, tm, None)
            dqs, new_dkv = [], []
            for g, (window, dil) in enumerate(PATTERNS):
                prev = None if dkv is None else (dkv[2 * g].reshape(B, S, C), dkv[2 * g + 1].reshape(B, S, C))
                dq, dk, dv = _attn_bwd(f"attn_bwd{i}_{g}", s["q"], kv, slopes, s["o"].reshape(B, S, C),
                                       s["lse"].reshape(B, S, C), do.reshape(B, S, C), g, dil, n_heads, prev)
                dqs.append(dq)
                new_dkv += [dk, dv]
            dkv = new_dkv
            gw["w_q"][i] = _tn(f"q_wgrad{i}", s["n1"], _to_bf16, [_seg2d(t, tt, 2) for t in dqs],
                               3 * C // N_CHIPS, True, min(512, D), tt)
            dh, g_mix[l] = _nt_cols(f"q_bwd{i}", [_seg2d(t, tm, 1) for t in dqs], W["w_q"], i, tm,
                                    (s["h_in"], row(norm_mix[l]), dh))
            if i == 0:
                gw["w_kv"][0] = _tn("kv_wgrad", nkv, _to_bf16, [_seg2d(t, tt, 2) for t in dkv],
                                    6 * C // N_CHIPS, True, min(512, D), tt)
                dh, g_kv = _nt_cols("kv_bwd", [_seg2d(t, tm, 1) for t in dkv], W["w_kv"], 0, tm,
                                    (h_kv, row(norm_kv), dh))
    small = dict(norm_mix=jnp.concatenate(g_mix, axis=0), norm_mlp=jnp.concatenate(g_mlp, axis=0),
                 norm_kv=g_kv, norm_final=dg_final, conv_w=jnp.stack(g_conv))
    return loss, dh.reshape(B, S, D), gw, small


BIG = ("w_a_in", "w_a_out", "w_kv", "w_q", "w_o", "w_up", "w_down")
CONV_PAD_ROWS = 16


def _reduce_scatter(gw):
    layers_by_type = [[a.reshape(N_CHIPS, 2, a.shape[1] // 2, a.shape[2]) for a in gw[k]] for k in BIG]
    place = jnp.stack([2 * lax.axis_index("x") + lax.axis_index("y"), lax.axis_index("c")]).astype(jnp.int32)
    recv = _pair_exchange(layers_by_type)
    parts = [_pair_add(f"grad_pair_add_{k}", layers, r, place) for k, layers, r in zip(BIG, layers_by_type, recv)]
    slots = _chip_exchange(parts)
    fulls = [_chip_add(f"grad_chip_add_{k}", a, b, place) for k, a, b in zip(BIG, parts, slots)]
    outs = _pair_share(fulls)
    return {k: a.reshape(a.shape[0], a.shape[1] * a.shape[2], a.shape[3]) for k, a in zip(BIG, outs)}


def kernel(x, norm_mix, norm_mlp, w_a_in, conv_w, w_a_out, norm_kv, w_kv, w_q, w_o, w_up, w_down, norm_final, loss_target, m_norm_mix, m_norm_mlp, m_w_a_in, m_conv_w, m_w_a_out, m_norm_kv, m_w_kv, m_w_q, m_w_o, m_w_up, m_w_down, m_norm_final, v_norm_mix, v_norm_mlp, v_w_a_in, v_conv_w, v_w_a_out, v_norm_kv, v_w_kv, v_w_q, v_w_o, v_w_up, v_w_down, v_norm_final):
    D = x.shape[-1]
    w = dict(norm_mix=norm_mix, norm_mlp=norm_mlp, w_a_in=w_a_in, conv_w=conv_w, w_a_out=w_a_out, norm_kv=norm_kv,
             w_kv=w_kv[None], w_q=w_q, w_o=w_o, w_up=w_up, w_down=w_down, norm_final=norm_final)
    m = dict(norm_mix=m_norm_mix, norm_mlp=m_norm_mlp, w_a_in=m_w_a_in, conv_w=m_conv_w, w_a_out=m_w_a_out,
             norm_kv=m_norm_kv, w_kv=m_w_kv[None], w_q=m_w_q, w_o=m_w_o, w_up=m_w_up, w_down=m_w_down,
             norm_final=m_norm_final)
    v = dict(norm_mix=v_norm_mix, norm_mlp=v_norm_mlp, w_a_in=v_w_a_in, conv_w=v_conv_w, w_a_out=v_w_a_out,
             norm_kv=v_norm_kv, w_kv=v_w_kv[None], w_q=v_w_q, w_o=v_w_o, w_up=v_w_up, w_down=v_w_down,
             norm_final=v_norm_final)

    n_a, taps, cwc = conv_w.shape
    conv_rows = jnp.zeros((CONV_PAD_ROWS, cwc), F32).at[:n_a * taps].set(conv_w.reshape(n_a * taps, cwc))
    chunks = [w[k].astype(BF16).reshape(-1, w[k].shape[-1]) for k in BIG] + [conv_rows]
    gathered = _all_gather(chunks)
    W = {k: a.reshape((N_CHIPS,) + w[k].shape) for k, a in zip(BIG, gathered[:-1])}
    cwg = gathered[-1][:, :n_a * taps].reshape(N_CHIPS, n_a, taps, cwc)

    loss, grad_x, gw, small = _local_step(x, loss_target, norm_mix, norm_mlp, norm_kv, norm_final, W, cwg)
    loss = lax.psum(loss[0, 0], ("x", "y", "c"))

    grads = _reduce_scatter(gw)

    depth = norm_mix.shape[0]
    packed = jnp.concatenate([small["norm_mix"], small["norm_mlp"], small["norm_kv"], small["norm_final"],
                              small["conv_w"].reshape(n_a * taps, D)], axis=0)
    pad = (-packed.shape[0]) % 8
    packed = jnp.pad(packed, ((0, pad), (0, 0)))
    total = _small_allreduce(packed)
    grads["norm_mix"] = total[:depth]
    grads["norm_mlp"] = total[depth:2 * depth]
    grads["norm_kv"] = total[2 * depth]
    grads["norm_final"] = total[2 * depth + 1]
    chip = 2 * lax.axis_index("x") + lax.axis_index("y")
    conv_full = total[2 * depth + 2:2 * depth + 2 + n_a * taps].reshape(n_a, taps, N_CHIPS, cwc)
    grads["conv_w"] = lax.dynamic_index_in_dim(conv_full, chip, axis=2, keepdims=False)

    order = ("norm_mix", "norm_mlp", "w_a_in", "conv_w", "w_a_out", "norm_kv", "w_kv", "w_q", "w_o", "w_up",
             "w_down", "norm_final")
    delta, new_m, new_v = {}, {}, {}
    vec_names = ("norm_mix", "norm_mlp", "norm_kv", "norm_final")
    rows_of = lambda a: a.reshape(-1, D)
    vw, vg, vm_, vv = (jnp.concatenate([rows_of(t[k]) for k in vec_names], axis=0) for t in (w, grads, m, v))
    n_vec = vw.shape[0]
    vpad = (-n_vec) % 8
    padrows = lambda a: jnp.pad(a, ((0, vpad), (0, 0)))
    vd, vnm, vnv = _adamw("adamw_norms", padrows(vw), padrows(vg), padrows(vm_), padrows(vv))
    off = 0
    for k in vec_names:
        r = rows_of(w[k]).shape[0]
        delta[k] = vd[off:off + r].reshape(w[k].shape)
        new_m[k] = vnm[off:off + r].reshape(w[k].shape)
        new_v[k] = vnv[off:off + r].reshape(w[k].shape)
        off += r
    for k in BIG + ("conv_w",):
        shape = w[k].shape
        two_d = lambda a: a.reshape(-1, shape[-1])
        if k == "conv_w":
            cpad = (-n_a * taps) % 8
            two_d = lambda a: jnp.pad(a.reshape(-1, shape[-1]), ((0, cpad), (0, 0)))
        d, nm, nv = _adamw(f"adamw_{k}", two_d(w[k]), two_d(grads[k]), two_d(m[k]), two_d(v[k]))
        rows = shape[0] * shape[1] if len(shape) == 3 else shape[0]
        delta[k], new_m[k], new_v[k] = (t[:rows].reshape(shape) for t in (d, nm, nv))
    fix = lambda k, a: a[0] if k == "w_kv" else a
    return (loss, grad_x, *[fix(k, grads[k]).reshape(fix(k, w[k]).shape) for k in order],
            *[fix(k, delta[k]) for k in order], *[fix(k, new_m[k]) for k in order],
            *[fix(k, new_v[k]) for k in order])
```

```python
import functools

import jax
import jax.numpy as jnp
from jax import lax
from jax.experimental import pallas as pl
from jax.experimental.pallas import tpu as pltpu

F32 = jnp.float32
BF16 = jnp.bfloat16
MESH = pl.DeviceIdType.MESH

EPS = 1e-5
PATTERNS = ((128, 1), (512, 4), (2048, 16))
HEAD_DIM = 64
ALIBI_MAX_BIAS = 8.0
NEG_INF = -1e30
ATT_BLK = 128
N_CHIPS = 4
LANES = 128
VMEM_LIMIT = 56 * 1024 * 1024

ADAM_LR = 0.001
ADAM_B1 = 0.9
ADAM_B2 = 0.999
ADAM_EPS = 1e-08
ADAM_WD = 0.01
ADAM_STEP = 10


def _params(n_grid_axes):
    return pltpu.CompilerParams(dimension_semantics=("arbitrary",) * n_grid_axes, vmem_limit_bytes=VMEM_LIMIT)


def _dot(a, b):
    return jnp.dot(a, b, preferred_element_type=F32)


def _dot_nt(a, b):
    return lax.dot_general(a, b, (((1,), (1,)), ((), ())), preferred_element_type=F32)


def _dot_tn(a, b):
    return lax.dot_general(a, b, (((0,), (0,)), ((), ())), preferred_element_type=F32)


def _relu2(a):
    return jnp.square(jnp.maximum(a, 0.0))


def _rms(hf, g):
    y = hf * lax.rsqrt(jnp.mean(hf * hf, axis=-1, keepdims=True) + EPS)
    return y * g


def _rms_bwd(hf, g, dn):
    rstd = lax.rsqrt(jnp.mean(hf * hf, axis=-1, keepdims=True) + EPS)
    xhat = hf * rstd
    dg = jnp.sum(dn * xhat, axis=0, keepdims=True)
    dx = dn * g
    dh = rstd * (dx - xhat * jnp.mean(dx * xhat, axis=-1, keepdims=True))
    return dh, dg


def _pieces(seg_widths, chunk_width, max_width):
    total = sum(seg_widths)
    cuts = {0, total}
    acc = 0
    for w in seg_widths:
        cuts.add(acc)
        acc += w
    cuts.update(range(0, total, chunk_width))
    cuts = sorted(cuts)
    fine = []
    for lo, hi in zip(cuts[:-1], cuts[1:]):
        while hi - lo > max_width:
            fine.append((lo, lo + max_width))
            lo += max_width
        fine.append((lo, hi))
    out = []
    for lo, hi in fine:
        acc = 0
        for s, w in enumerate(seg_widths):
            if lo < acc + w:
                break
            acc += w
        out.append((s, lo - acc, lo // chunk_width, lo % chunk_width, hi - lo))
    return out


def _relu2_bf16(a):
    return _relu2(a.astype(F32)).astype(BF16)


def _to_bf16(a):
    return a.astype(BF16)


def _norm_mm(name, h, g, wg, layer, planes, out_dtype, tm):
    T, D = h.shape
    cw = wg.shape[3]
    N = N_CHIPS * cw
    pw = N // planes
    pieces = _pieces([pw] * planes, cw, 512)

    def body(h_ref, g_ref, w_ref, n_ref, o_ref):
        n = _rms(h_ref[...], g_ref[...]).astype(BF16)
        n_ref[...] = n
        for s, a0, ch, b0, wd in pieces:
            o_ref[s, :, a0:a0 + wd] = _dot(n, w_ref[ch, :, b0:b0 + wd]).astype(out_dtype)

    return pl.pallas_call(
        body, name=name, grid=(T // tm,),
        in_specs=[pl.BlockSpec((tm, D), lambda i: (i, 0)),
                  pl.BlockSpec((1, D), lambda i: (0, 0)),
                  pl.BlockSpec((N_CHIPS, None, D, cw), lambda i: (0, layer, 0, 0))],
        out_specs=[pl.BlockSpec((tm, D), lambda i: (i, 0)),
                   pl.BlockSpec((planes, tm, pw), lambda i: (0, i, 0))],
        out_shape=[jax.ShapeDtypeStruct((T, D), BF16), jax.ShapeDtypeStruct((planes, T, pw), out_dtype)],
        compiler_params=_params(1))(h, g, wg)


def _mm_res_rows(name, a, wg, layer, h, act, tm):
    T = a.shape[0]
    rk, D = wg.shape[2], wg.shape[3]

    def body(a_ref, w_ref, h_ref, o_ref):
        acc = h_ref[...]
        for k in range(N_CHIPS):
            acc = acc + _dot(act(a_ref[:, k * rk:(k + 1) * rk]), w_ref[k])
        o_ref[...] = acc

    return pl.pallas_call(
        body, name=name, grid=(T // tm,),
        in_specs=[pl.BlockSpec((tm, N_CHIPS * rk), lambda i: (i, 0)),
                  pl.BlockSpec((N_CHIPS, None, rk, D), lambda i: (0, layer, 0, 0)),
                  pl.BlockSpec((tm, D), lambda i: (i, 0))],
        out_specs=pl.BlockSpec((tm, D), lambda i: (i, 0)),
        out_shape=jax.ShapeDtypeStruct((T, D), F32),
        compiler_params=_params(1))(a, wg, h)


def _mm_res_cols(name, a, wg, layer, h, tm):
    T, K = a.shape
    cw = wg.shape[3]
    D = N_CHIPS * cw

    def body(a_ref, w_ref, h_ref, o_ref):
        a16 = a_ref[...].astype(BF16)
        for j in range(N_CHIPS):
            o_ref[:, j * cw:(j + 1) * cw] = h_ref[:, j * cw:(j + 1) * cw] + _dot(a16, w_ref[j])

    return pl.pallas_call(
        body, name=name, grid=(T // tm,),
        in_specs=[pl.BlockSpec((tm, K), lambda i: (i, 0)),
                  pl.BlockSpec((N_CHIPS, None, K, cw), lambda i: (0, layer, 0, 0)),
                  pl.BlockSpec((tm, D), lambda i: (i, 0))],
        out_specs=pl.BlockSpec((tm, D), lambda i: (i, 0)),
        out_shape=jax.ShapeDtypeStruct((T, D), F32),
        compiler_params=_params(1))(a, wg, h)


CONV_ROWS = 256
CONV_HALO = 16


def _conv_shifted(ext, k, r0, rows):
    rolled = pltpu.roll(ext, k, 0)[CONV_HALO:]
    t = r0 + lax.broadcasted_iota(jnp.int32, rolled.shape, 0)
    return jnp.where(t >= k, rolled, 0.0)


def _conv_ahead(ext, k, r0, rows, S):
    rolled = pltpu.roll(ext, rows + CONV_HALO - k, 0)[:rows]
    t = r0 + lax.broadcasted_iota(jnp.int32, rolled.shape, 0)
    return jnp.where(t + k < S, rolled, 0.0)


def _conv_fwd(name, bcu, cwg, layer, tc):
    _, B, S, D = bcu.shape
    cwc = cwg.shape[3]
    per_chunk = cwc // tc
    R = min(CONV_ROWS, S)

    def body(x_ref, w_ref, z_ref):
        w = [w_ref[k:k + 1, :] for k in range(3)]

        def step(i, carry):
            r0 = pl.multiple_of(i * R, R)
            h0 = pl.multiple_of(jnp.maximum(r0 - CONV_HALO, 0), CONV_HALO)
            ld = lambda p, start, rows: x_ref[p, pl.ds(start, rows), :].astype(F32)
            cu = jnp.concatenate([ld(1, h0, CONV_HALO) * ld(2, h0, CONV_HALO), ld(1, r0, R) * ld(2, r0, R)], axis=0)
            conv = w[0] * cu[CONV_HALO:]
            conv = conv + w[1] * _conv_shifted(cu, 1, r0, R)
            conv = conv + w[2] * _conv_shifted(cu, 2, r0, R)
            z_ref[pl.ds(r0, R), :] = (ld(0, r0, R) * conv).astype(BF16)
            return carry

        lax.fori_loop(0, S // R, step, 0)

    return pl.pallas_call(
        body, name=name, grid=(B, D // tc),
        in_specs=[pl.BlockSpec((3, None, S, tc), lambda b, j: (0, b, 0, j)),
                  pl.BlockSpec((None, None, 3, tc), lambda b, j: (j // per_chunk, layer, 0, j % per_chunk))],
        out_specs=pl.BlockSpec((None, S, tc), lambda b, j: (b, 0, j)),
        out_shape=jax.ShapeDtypeStruct((B, S, D), BF16),
        compiler_params=_params(2))(bcu, cwg)


def _conv_bwd(name, bcu, dz, cwg, layer, tc):
    _, B, S, D = bcu.shape
    cwc = cwg.shape[3]
    per_chunk = cwc // tc
    R = min(CONV_ROWS, S)

    def body(x_ref, dz_ref, w_ref, d_ref, dw_ref):
        w = [w_ref[k:k + 1, :] for k in range(3)]

        @pl.when(pl.program_id(1) == 0)
        def _():
            dw_ref[...] = jnp.zeros_like(dw_ref)

        def step(i, carry):
            r0 = pl.multiple_of(i * R, R)
            h0 = pl.multiple_of(jnp.maximum(r0 - CONV_HALO, 0), CONV_HALO)
            a0 = pl.multiple_of(jnp.minimum(r0 + R, S - CONV_HALO), CONV_HALO)
            ld = lambda p, start, rows: x_ref[p, pl.ds(start, rows), :].astype(F32)
            b, c, u = ld(0, r0, R), ld(1, r0, R), ld(2, r0, R)
            dz = dz_ref[pl.ds(r0, R), :]
            cu = jnp.concatenate([ld(1, h0, CONV_HALO) * ld(2, h0, CONV_HALO), c * u], axis=0)
            cu1 = _conv_shifted(cu, 1, r0, R)
            cu2 = _conv_shifted(cu, 2, r0, R)
            conv = w[0] * (c * u) + w[1] * cu1 + w[2] * cu2
            dconv = dz * b
            dca = jnp.concatenate([dconv, dz_ref[pl.ds(a0, CONV_HALO), :] * ld(0, a0, CONV_HALO)], axis=0)
            dcu = w[0] * dconv + w[1] * _conv_ahead(dca, 1, r0, R, S) + w[2] * _conv_ahead(dca, 2, r0, R, S)
            d_ref[0, pl.ds(r0, R), :] = (dz * conv).astype(BF16)
            d_ref[1, pl.ds(r0, R), :] = (dcu * u).astype(BF16)
            d_ref[2, pl.ds(r0, R), :] = (dcu * c).astype(BF16)
            return (carry[0] + jnp.sum(dconv * (c * u), axis=0, keepdims=True),
                    carry[1] + jnp.sum(dconv * cu1, axis=0, keepdims=True),
                    carry[2] + jnp.sum(dconv * cu2, axis=0, keepdims=True))

        zero = jnp.zeros((1, tc), F32)
        s0, s1, s2 = lax.fori_loop(0, S // R, step, (zero, zero, zero))
        for k, sk in enumerate((s0, s1, s2)):
            dw_ref[k:k + 1, :] += sk

    return pl.pallas_call(
        body, name=name, grid=(D // tc, B),
        in_specs=[pl.BlockSpec((3, None, S, tc), lambda j, b: (0, b, 0, j)),
                  pl.BlockSpec((None, S, tc), lambda j, b: (b, 0, j)),
                  pl.BlockSpec((None, None, 3, tc), lambda j, b: (j // per_chunk, layer, 0, j % per_chunk))],
        out_specs=[pl.BlockSpec((3, None, S, tc), lambda j, b: (0, b, 0, j)),
                   pl.BlockSpec((3, tc), lambda j, b: (0, j))],
        out_shape=[jax.ShapeDtypeStruct((3, B, S, D), BF16), jax.ShapeDtypeStruct((3, D), F32)],
        compiler_params=_params(2))(bcu, dz, cwg)


def _att_rows(dil, idx, nb):
    r, n = idx // nb, idx % nb
    if dil == 1:
        cur = pl.ds(pl.multiple_of(n * ATT_BLK, ATT_BLK), ATT_BLK)
        prev = pl.ds(pl.multiple_of(jnp.maximum(n - 1, 0) * ATT_BLK, ATT_BLK), ATT_BLK)
    else:
        cur = pl.ds(n * (ATT_BLK * dil) + r, ATT_BLK, stride=dil)
        prev = pl.ds(jnp.maximum(n - 1, 0) * (ATT_BLK * dil) + r, ATT_BLK, stride=dil)
    return n, cur, prev


def _att_tiles(dil, n, sl_ref, hp):
    row = lax.broadcasted_iota(jnp.int32, (2 * ATT_BLK, 2 * ATT_BLK), 0)
    ci = lax.broadcasted_iota(jnp.int32, (2 * ATT_BLK, 2 * ATT_BLK), 1)
    j = ATT_BLK + (row & (ATT_BLK - 1)) - ci
    valid = (j >= 0) & (j <= ATT_BLK) & (ci >= jnp.where(n > 0, 0, ATT_BLK))
    slope = jnp.where(row < ATT_BLK, sl_ref[2 * hp], sl_ref[2 * hp + 1])
    return valid, slope * (dil * j).astype(F32)


def _stack_heads(x16, lane):
    first = lane < HEAD_DIM
    return jnp.concatenate([jnp.where(first, x16, jnp.zeros_like(x16)),
                            jnp.where(first, jnp.zeros_like(x16), x16)], axis=0)


def _per_head(col, lane):
    return jnp.where(lane < HEAD_DIM, col[:ATT_BLK], col[ATT_BLK:])


def _attn_fwd(name, q, kv, slopes, n_heads):
    B, S, CQ = q.shape
    HP = n_heads * HEAD_DIM // LANES
    scale = HEAD_DIM ** -0.5
    n_groups = len(PATTERNS)
    CH = 256

    def body(sl_ref, q_ref, k_ref, v_ref, o_ref, lse_ref, *parts):
        og, lg = parts[:n_groups], parts[n_groups:]
        hp, g = pl.program_id(1), pl.program_id(2)
        lane = lax.broadcasted_iota(jnp.int32, (1, LANES), 1)

        for gi, (window, dil) in enumerate(PATTERNS):
            nb = S // dil // ATT_BLK

            @pl.when(g == gi)
            def _(gi=gi, dil=dil, nb=nb):
                def step(idx, carry):
                    n, cur, prev = _att_rows(dil, idx, nb)
                    valid, bias = _att_tiles(dil, n, sl_ref, hp)
                    qs = _stack_heads((q_ref[cur, :] * scale).astype(BF16), lane)
                    kc = jnp.concatenate([k_ref[prev, :], k_ref[cur, :]], axis=0).astype(BF16)
                    vc = jnp.concatenate([v_ref[prev, :], v_ref[cur, :]], axis=0).astype(BF16)
                    s = jnp.where(valid, _dot_nt(qs, kc) - bias, NEG_INF)
                    m = jnp.max(s, axis=-1, keepdims=True)
                    p = jnp.exp(s - m)
                    l = jnp.sum(p, axis=-1, keepdims=True)
                    p16 = p.astype(BF16)
                    o_un = _dot(jnp.concatenate([p16[:ATT_BLK], p16[ATT_BLK:]], axis=1), _stack_heads_rows(vc, lane))
                    og[gi][cur, :] = o_un / _per_head(l, lane)
                    lg[gi][cur, :] = _per_head(m + jnp.log(l), lane)
                    return carry

                lax.fori_loop(0, S // ATT_BLK, step, 0, unroll=2)

        @pl.when(g == n_groups - 1)
        def _():
            def comb(i, carry):
                rows = pl.ds(pl.multiple_of(i * CH, CH), CH)
                a, b, c = lg[0][rows, :], lg[1][rows, :], lg[2][rows, :]
                m = jnp.maximum(jnp.maximum(a, b), c)
                ea, eb, ec = jnp.exp(a - m), jnp.exp(b - m), jnp.exp(c - m)
                z = ea + eb + ec
                o_ref[rows, :] = (ea / z) * og[0][rows, :] + (eb / z) * og[1][rows, :] + (ec / z) * og[2][rows, :]
                lse_ref[rows, :] = m + jnp.log(z)
                return carry

            lax.fori_loop(0, S // CH, comb, 0)

    blk = (None, S, LANES)
    out = pl.BlockSpec(blk, lambda b, hp, g: (b, 0, hp))
    return pl.pallas_call(
        body, name=name, grid=(B, HP, n_groups),
        in_specs=[pl.BlockSpec(memory_space=pltpu.SMEM),
                  pl.BlockSpec(blk, lambda b, hp, g: (b, 0, g * HP + hp)),
                  pl.BlockSpec(blk, lambda b, hp, g: (b, 0, g * 2 * HP + hp)),
                  pl.BlockSpec(blk, lambda b, hp, g: (b, 0, g * 2 * HP + HP + hp))],
        out_specs=[out, out],
        out_shape=[jax.ShapeDtypeStruct((B, S, HP * LANES), F32)] * 2,
        scratch_shapes=[pltpu.VMEM((S, LANES), F32)] * (2 * n_groups),
        compiler_params=_params(3))(slopes, q, kv, kv)


def _stack_heads_rows(x16, lane):
    first = lane < HEAD_DIM
    return jnp.concatenate([jnp.where(first, x16, jnp.zeros_like(x16)),
                            jnp.where(first, jnp.zeros_like(x16), x16)], axis=0)


def _attn_bwd(name, q, kv, slopes, o, lse, do, n_heads, dkv_prev):
    B, S, CQ = q.shape
    HP = n_heads * HEAD_DIM // LANES
    scale = HEAD_DIM ** -0.5
    n_groups = len(PATTERNS)
    n_prev = 0 if dkv_prev is None else 2

    def body(sl_ref, q_ref, k_ref, v_ref, o_ref, lse_ref, do_ref, *rest):
        dq_ref, dk_ref, dv_ref = rest[n_prev:]
        hp, g = pl.program_id(1), pl.program_id(2)
        lane = lax.broadcasted_iota(jnp.int32, (1, LANES), 1)
        first = lane < HEAD_DIM

        def flush(rows, dk, dv):
            if n_prev:
                dk = dk + rest[0][rows, :]
                dv = dv + rest[1][rows, :]
            dk_ref[rows, :] = dk
            dv_ref[rows, :] = dv

        for gi, (window, dil) in enumerate(PATTERNS):
            nb = S // dil // ATT_BLK
            n_blocks = S // ATT_BLK

            @pl.when(g == gi)
            def _(dil=dil, nb=nb, n_blocks=n_blocks):
                def step(idx, carry):
                    n, cur, prev = _att_rows(dil, idx, nb)
                    valid, bias = _att_tiles(dil, n, sl_ref, hp)
                    qs = _stack_heads((q_ref[cur, :] * scale).astype(BF16), lane)
                    kc = jnp.concatenate([k_ref[prev, :], k_ref[cur, :]], axis=0).astype(BF16)
                    vc = jnp.concatenate([v_ref[prev, :], v_ref[cur, :]], axis=0).astype(BF16)
                    dob = do_ref[cur, :]
                    prod = dob * o_ref[cur, :]
                    lseb = lse_ref[cur, :]
                    dos = _stack_heads(dob.astype(BF16), lane)
                    delta = jnp.concatenate(
                        [jnp.sum(jnp.where(first, prod, 0.0), axis=-1, keepdims=True),
                         jnp.sum(jnp.where(first, 0.0, prod), axis=-1, keepdims=True)], axis=0)
                    lse_col = jnp.concatenate(
                        [jnp.max(jnp.where(first, lseb, -jnp.inf), axis=-1, keepdims=True),
                         jnp.max(jnp.where(first, -jnp.inf, lseb), axis=-1, keepdims=True)], axis=0)
                    s = jnp.where(valid, _dot_nt(qs, kc) - bias, NEG_INF)
                    p = jnp.exp(s - lse_col)
                    ds = p * (_dot_nt(dos, vc) - delta)
                    ds16 = ds.astype(BF16)
                    dq = _dot(jnp.concatenate([ds16[:ATT_BLK], ds16[ATT_BLK:]], axis=1), _stack_heads_rows(kc, lane))
                    dq_ref[cur, :] = dq * scale
                    dk = _dot(ds.T.astype(BF16), qs)
                    dv = _dot(p.T.astype(BF16), dos)

                    @pl.when(idx > 0)
                    def _():
                        _, before, _ = _att_rows(dil, idx - 1, nb)
                        flush(before, carry[0] + dk[:ATT_BLK], carry[1] + dv[:ATT_BLK])

                    return dk[ATT_BLK:], dv[ATT_BLK:]

                zero = jnp.zeros((ATT_BLK, LANES), F32)
                dk_last, dv_last = lax.fori_loop(0, n_blocks, step, (zero, zero), unroll=2)
                _, last, _ = _att_rows(dil, n_blocks - 1, nb)
                flush(last, dk_last, dv_last)

    blk = (None, S, LANES)
    shared = pl.BlockSpec(blk, lambda b, hp, g: (b, 0, hp))
    grouped = pl.BlockSpec(blk, lambda b, hp, g: (b, 0, g * HP + hp))
    prev = [] if dkv_prev is None else list(dkv_prev)
    gshape = jax.ShapeDtypeStruct((B, S, n_groups * HP * LANES), F32)
    return pl.pallas_call(
        body, name=name, grid=(B, HP, n_groups),
        in_specs=[pl.BlockSpec(memory_space=pltpu.SMEM), grouped,
                  pl.BlockSpec(blk, lambda b, hp, g: (b, 0, g * 2 * HP + hp)),
                  pl.BlockSpec(blk, lambda b, hp, g: (b, 0, g * 2 * HP + HP + hp)),
                  shared, shared, shared] + [grouped] * n_prev,
        out_specs=[grouped] * 3, out_shape=[gshape] * 3,
        compiler_params=_params(3))(slopes, q, kv, kv, o, lse, do, *prev)


def _final_loss(name, h, g, target, tm):
    T, D = h.shape

    def body(h_ref, g_ref, t_ref, loss_ref, dh_ref, dg_ref):
        hf = h_ref[...]
        gv = g_ref[...]
        rstd = lax.rsqrt(jnp.mean(hf * hf, axis=-1, keepdims=True) + EPS)
        xhat = hf * rstd
        err = xhat * gv - t_ref[...]
        part = 0.5 * jnp.sum(jnp.mean(err * err, axis=-1, keepdims=True), axis=0, keepdims=True)
        dy = err * (1.0 / D)
        dg = jnp.sum(dy * xhat, axis=0, keepdims=True)
        dx = dy * gv
        dh_ref[...] = rstd * (dx - xhat * jnp.mean(dx * xhat, axis=-1, keepdims=True))

        @pl.when(pl.program_id(0) == 0)
        def _():
            loss_ref[...] = part
            dg_ref[...] = dg

        @pl.when(pl.program_id(0) > 0)
        def _():
            loss_ref[...] += part
            dg_ref[...] += dg

    return pl.pallas_call(
        body, name=name, grid=(T // tm,),
        in_specs=[pl.BlockSpec((tm, D), lambda i: (i, 0)), pl.BlockSpec((1, D), lambda i: (0, 0)),
                  pl.BlockSpec((tm, D), lambda i: (i, 0))],
        out_specs=[pl.BlockSpec((1, 1), lambda i: (0, 0)), pl.BlockSpec((tm, D), lambda i: (i, 0)),
                   pl.BlockSpec((1, D), lambda i: (0, 0))],
        out_shape=[jax.ShapeDtypeStruct((1, 1), F32), jax.ShapeDtypeStruct((T, D), F32),
                   jax.ShapeDtypeStruct((1, D), F32)],
        compiler_params=_params(1))(h, g, target)


def _nt_rows(name, dh, wg, layer, a_mul, out_dtype, tm):
    T, D = dh.shape
    rk = wg.shape[2]
    N = N_CHIPS * rk
    with_a = a_mul is not None

    def body(dh_ref, w_ref, *rest):
        o_ref = rest[-1]
        d16 = dh_ref[...].astype(BF16)
        for ch in range(N_CHIPS):
            r = _dot_nt(d16, w_ref[ch])
            if with_a:
                r = r * (2.0 * jnp.maximum(rest[0][:, ch * rk:(ch + 1) * rk].astype(F32), 0.0))
            o_ref[:, ch * rk:(ch + 1) * rk] = r.astype(out_dtype)

    in_specs = [pl.BlockSpec((tm, D), lambda i: (i, 0)),
                pl.BlockSpec((N_CHIPS, None, rk, D), lambda i: (0, layer, 0, 0))]
    args = [dh, wg]
    if with_a:
        in_specs.append(pl.BlockSpec((tm, N), lambda i: (i, 0)))
        args.append(a_mul)
    return pl.pallas_call(
        body, name=name, grid=(T // tm,), in_specs=in_specs,
        out_specs=pl.BlockSpec((tm, N), lambda i: (i, 0)),
        out_shape=jax.ShapeDtypeStruct((T, N), out_dtype),
        compiler_params=_params(1))(*args)


def _nt_cols(name, ysegs, wg, layer, tm, norm):
    Nw, cw = wg.shape[2], wg.shape[3]
    widths = [bs[-1] for _, bs, _ in ysegs]
    pieces = _pieces(widths, cw, 1024)
    ns = len(ysegs)
    T = norm[0].shape[0] if norm is not None else ysegs[0][0].shape[-2]

    def body(*refs):
        y_refs = refs[:ns]
        w_ref = refs[ns]
        acc = refs[-1]
        for n, (s, a0, ch, b0, wd) in enumerate(pieces):
            d = _dot_nt(y_refs[s][:, a0:a0 + wd].astype(BF16), w_ref[ch, :, b0:b0 + wd])
            if n == 0:
                acc[...] = d
            else:
                acc[...] += d
        if norm is None:
            refs[ns + 1][...] = acc[...]
        else:
            h_ref, g_ref, dhin_ref, out_ref, dg_ref = refs[ns + 1:ns + 6]
            dh_c, dg = _rms_bwd(h_ref[...], g_ref[...], acc[...])
            out_ref[...] = dhin_ref[...] + dh_c

            @pl.when(pl.program_id(0) == 0)
            def _():
                dg_ref[...] = dg

            @pl.when(pl.program_id(0) > 0)
            def _():
                dg_ref[...] += dg

    in_specs = [pl.BlockSpec(bs, im) for _, bs, im in ysegs]
    in_specs.append(pl.BlockSpec((N_CHIPS, None, Nw, cw), lambda i: (0, layer, 0, 0)))
    args = [a for a, _, _ in ysegs] + [wg]
    row = pl.BlockSpec((tm, Nw), lambda i: (i, 0))
    vec = pl.BlockSpec((1, Nw), lambda i: (0, 0))
    if norm is None:
        out_specs = row
        out_shape = jax.ShapeDtypeStruct((T, Nw), F32)
    else:
        in_specs += [row, vec, row]
        args += list(norm)
        out_specs = [row, vec]
        out_shape = [jax.ShapeDtypeStruct((T, Nw), F32), jax.ShapeDtypeStruct((1, Nw), F32)]
    return pl.pallas_call(
        body, name=name, grid=(T // tm,), in_specs=in_specs, out_specs=out_specs, out_shape=out_shape,
        scratch_shapes=[pltpu.VMEM((tm, Nw), F32)], compiler_params=_params(1))(*args)


def _tn(name, x, x_act, ysegs, cw, cols_layout, tmm, tt):
    T, M = x.shape
    widths = [bs[-1] for _, bs, _ in ysegs]
    N = sum(widths)
    pieces = _pieces(widths, cw if cols_layout else N, 1024)
    ns = len(ysegs)

    def body(x_ref, *refs):
        y_refs = refs[:ns]
        o_ref = refs[ns]

        @pl.when(pl.program_id(1) == 0)
        def _():
            o_ref[...] = jnp.zeros_like(o_ref)

        xt = x_act(x_ref[...])
        for s, a0, ch, b0, wd in pieces:
            d = _dot_tn(xt, y_refs[s][:, a0:a0 + wd].astype(BF16))
            if cols_layout:
                o_ref[ch, :, b0:b0 + wd] += d
            else:
                o_ref[:, b0:b0 + wd] += d

    in_specs = [pl.BlockSpec((tt, tmm), lambda m, t: (t, m))] + [pl.BlockSpec(bs, im) for _, bs, im in ysegs]
    if cols_layout:
        out_specs = pl.BlockSpec((N_CHIPS, tmm, cw), lambda m, t: (0, m, 0))
        out_shape = jax.ShapeDtypeStruct((N_CHIPS, M, cw), F32)
    else:
        out_specs = pl.BlockSpec((tmm, N), lambda m, t: (m, 0))
        out_shape = jax.ShapeDtypeStruct((M, N), F32)
    return pl.pallas_call(
        body, name=name, grid=(M // tmm, T // tt), in_specs=in_specs, out_specs=out_specs, out_shape=out_shape,
        compiler_params=_params(2))(x, *[a for a, _, _ in ysegs])


def _seg2d(a, t_rows, grid_rank):
    w = a.shape[1]
    if grid_rank == 1:
        return (a, (t_rows, w), lambda i: (i, 0))
    return (a, (t_rows, w), lambda m, t: (t, 0))


def _kv_segments(dk, dv, C, t_rows, grid_rank):
    segs = []
    for g in range(len(PATTERNS)):
        for a in (dk, dv):
            if grid_rank == 1:
                segs.append((a, (t_rows, C), lambda i, g=g: (i, g)))
            else:
                segs.append((a, (t_rows, C), lambda m, t, g=g: (t, g)))
    return segs


def _seg_plane(a, plane, t_rows, grid_rank):
    w = a.shape[2]
    if grid_rank == 1:
        return (a, (None, t_rows, w), lambda i: (plane, i, 0))
    return (a, (None, t_rows, w), lambda m, t: (plane, t, 0))


def _row_tile(rows, row_bytes, budget_bytes=2 * 1024 * 1024):
    t = rows
    while t * row_bytes > budget_bytes and t % 32 == 0:
        t //= 2
    return t


def _pair_add(name, layers, recv, place):
    L = len(layers)
    _, _, hr, c = layers[0].shape
    tr = _row_tile(hr, L * c * 4)

    def body(place_ref, *refs):
        r_ref, o_ref = refs[L], refs[L + 1]
        for l in range(L):
            o_ref[l] = (refs[l][...] + r_ref[l]).astype(BF16)

    stacked = pl.BlockSpec((None, L, tr, c), lambda q, i, pr: (q, 0, i, 0))
    grid_spec = pltpu.PrefetchScalarGridSpec(
        num_scalar_prefetch=1, grid=(N_CHIPS, hr // tr),
        in_specs=[pl.BlockSpec((None, None, tr, c), lambda q, i, pr: (q, pr[1], i, 0))] * L + [stacked],
        out_specs=stacked)
    return pl.pallas_call(body, name=name, grid_spec=grid_spec,
                          out_shape=jax.ShapeDtypeStruct((N_CHIPS, L, hr, c), BF16),
                          compiler_params=_params(2))(place, *layers, recv)


def _chip_add(name, part, slots, place):
    _, L, hr, c = part.shape
    tr = _row_tile(hr, L * c * 4)

    def body(place_ref, own, s1, s2, s3, o_ref):
        f = lambda r: r[...].astype(F32)
        o_ref[...] = ((f(own) + f(s1)) + f(s2)) + f(s3)

    def slot(k):
        return pl.BlockSpec((None, L, tr, c), lambda i, pr: ((pr[0] + k) % N_CHIPS, 0, i, 0))

    grid_spec = pltpu.PrefetchScalarGridSpec(
        num_scalar_prefetch=1, grid=(hr // tr,),
        in_specs=[slot(0), slot(1), slot(2), slot(3)],
        out_specs=pl.BlockSpec((L, None, tr, c), lambda i, pr: (0, pr[1], i, 0)))
    return pl.pallas_call(body, name=name, grid_spec=grid_spec,
                          out_shape=jax.ShapeDtypeStruct((L, 2, hr, c), F32),
                          compiler_params=_params(1))(place, part, slots, slots, slots)


def _adamw(name, w, g, m, v):
    rows, cols = w.shape
    tr = _row_tile(rows, cols * 4, 1024 * 1024)

    def body(w_ref, g_ref, m_ref, v_ref, d_ref, nm_ref, nv_ref):
        gv = g_ref[...]
        nm = ADAM_B1 * m_ref[...] + (1.0 - ADAM_B1) * gv
        nv = ADAM_B2 * v_ref[...] + (1.0 - ADAM_B2) * jnp.square(gv)
        m_hat = nm / (1.0 - ADAM_B1 ** ADAM_STEP)
        v_hat = nv / (1.0 - ADAM_B2 ** ADAM_STEP)
        d_ref[...] = -ADAM_LR * (m_hat / (jnp.sqrt(v_hat) + ADAM_EPS) + ADAM_WD * w_ref[...])
        nm_ref[...] = nm
        nv_ref[...] = nv

    spec = pl.BlockSpec((tr, cols), lambda i: (i, 0))
    return pl.pallas_call(
        body, name=name, grid=(rows // tr,), in_specs=[spec] * 4, out_specs=[spec] * 3,
        out_shape=[jax.ShapeDtypeStruct((rows, cols), F32)] * 3, compiler_params=_params(1))(w, g, m, v)


ANY = pl.BlockSpec(memory_space=pl.ANY)


def _place():
    x, y, c = lax.axis_index("x"), lax.axis_index("y"), lax.axis_index("c")
    chips = [(1 - x, y), (x, 1 - y), (1 - x, 1 - y)]
    return x, y, c, chips


def _all_gather(chunks):
    n = len(chunks)

    def body(*refs):
        ins, outs = refs[:n], refs[n:2 * n]
        ssem, rsem, fssem, frsem, osem_s, osem_r = refs[2 * n:]
        x, y, c, chips = _place()
        p = 2 * x + y
        sends = []
        for t in range(n):
            hr = ins[t].shape[0] // 2
            mine = pl.ds(pl.multiple_of(c * hr, 8), hr)
            own = pltpu.make_async_remote_copy(
                src_ref=ins[t], dst_ref=outs[t].at[p], send_sem=osem_s.at[t], recv_sem=osem_r.at[t],
                device_id=(x, y, 1 - c), device_id_type=MESH)
            own.start()
            sends.append(own)
            for k, (qx, qy) in enumerate(chips):
                cp = pltpu.make_async_remote_copy(
                    src_ref=ins[t].at[mine], dst_ref=outs[t].at[p, mine], send_sem=ssem.at[t, k],
                    recv_sem=rsem.at[t, k], device_id=(qx, qy, c), device_id_type=MESH)
                cp.start()
                sends.append(cp)
        for t in range(n):
            hr = ins[t].shape[0] // 2
            mine = pl.ds(pl.multiple_of(c * hr, 8), hr)
            for k, (qx, qy) in enumerate(chips):
                q = 2 * qx + qy
                landed = outs[t].at[q, mine]
                pltpu.make_async_remote_copy(
                    src_ref=landed, dst_ref=landed, send_sem=ssem.at[t, k], recv_sem=rsem.at[t, k],
                    device_id=(qx, qy, c), device_id_type=MESH).wait_recv()
                fw = pltpu.make_async_remote_copy(
                    src_ref=landed, dst_ref=landed, send_sem=fssem.at[t, k], recv_sem=frsem.at[t, k],
                    device_id=(x, y, 1 - c), device_id_type=MESH)
                fw.start()
                sends.append(fw)
        for t in range(n):
            hr = ins[t].shape[0] // 2
            theirs = pl.ds(pl.multiple_of((1 - c) * hr, 8), hr)
            for k, (qx, qy) in enumerate(chips):
                q = 2 * qx + qy
                passed = outs[t].at[q, theirs]
                pltpu.make_async_remote_copy(
                    src_ref=passed, dst_ref=passed, send_sem=fssem.at[t, k], recv_sem=frsem.at[t, k],
                    device_id=(x, y, 1 - c), device_id_type=MESH).wait_recv()
        for t in range(n):
            pltpu.make_async_remote_copy(
                src_ref=ins[t], dst_ref=outs[t].at[p], send_sem=osem_s.at[t], recv_sem=osem_r.at[t],
                device_id=(x, y, 1 - c), device_id_type=MESH).wait_recv()
        for cp in sends:
            cp.wait_send()

    return pl.pallas_call(
        body, name="gather_weights", in_specs=[ANY] * n, out_specs=[ANY] * n,
        out_shape=[jax.ShapeDtypeStruct((N_CHIPS,) + a.shape, a.dtype) for a in chunks],
        scratch_shapes=[pltpu.SemaphoreType.DMA((n, 3)), pltpu.SemaphoreType.DMA((n, 3)),
                        pltpu.SemaphoreType.DMA((n, 3)), pltpu.SemaphoreType.DMA((n, 3)),
                        pltpu.SemaphoreType.DMA((n,)), pltpu.SemaphoreType.DMA((n,))])(*chunks)


def _pair_exchange(layers_by_type):
    flat = [(o, l, a) for o, layers in enumerate(layers_by_type) for l, a in enumerate(layers)]
    n, n_out = len(flat), len(layers_by_type)

    def body(*refs):
        ins, outs = refs[:n], refs[n:n + n_out]
        ssem, rsem = refs[n + n_out:]
        x, y, c, _ = _place()
        cps = []
        for t, (o, l, _) in enumerate(flat):
            cp = pltpu.make_async_remote_copy(
                src_ref=ins[t].at[:, 1 - c], dst_ref=outs[o].at[:, l], send_sem=ssem.at[t], recv_sem=rsem.at[t],
                device_id=(x, y, 1 - c), device_id_type=MESH)
            cp.start()
            cps.append(cp)
        for cp in cps:
            cp.wait()

    return pl.pallas_call(
        body, name="grad_pair_exchange", in_specs=[ANY] * n, out_specs=[ANY] * n_out,
        out_shape=[jax.ShapeDtypeStruct((N_CHIPS, len(layers)) + layers[0].shape[2:], F32)
                   for layers in layers_by_type],
        scratch_shapes=[pltpu.SemaphoreType.DMA((n,)), pltpu.SemaphoreType.DMA((n,))])(*[a for _, _, a in flat])


def _chip_exchange(parts):
    n = len(parts)

    def body(*refs):
        ins, outs = refs[:n], refs[n:2 * n]
        ssem, rsem = refs[2 * n:]
        x, y, c, chips = _place()
        p = 2 * x + y
        sends = []
        for t in range(n):
            for k, (qx, qy) in enumerate(chips):
                cp = pltpu.make_async_remote_copy(
                    src_ref=ins[t].at[2 * qx + qy], dst_ref=outs[t].at[p], send_sem=ssem.at[t, k],
                    recv_sem=rsem.at[t, k], device_id=(qx, qy, c), device_id_type=MESH)
                cp.start()
                sends.append(cp)
        for t in range(n):
            for k, (qx, qy) in enumerate(chips):
                slot = outs[t].at[2 * qx + qy]
                pltpu.make_async_remote_copy(
                    src_ref=slot, dst_ref=slot, send_sem=ssem.at[t, k], recv_sem=rsem.at[t, k],
                    device_id=(qx, qy, c), device_id_type=MESH).wait_recv()
        for cp in sends:
            cp.wait_send()

    return pl.pallas_call(
        body, name="grad_chip_exchange", in_specs=[ANY] * n, out_specs=[ANY] * n,
        out_shape=[jax.ShapeDtypeStruct(a.shape, a.dtype) for a in parts],
        scratch_shapes=[pltpu.SemaphoreType.DMA((n, 3)), pltpu.SemaphoreType.DMA((n, 3))])(*parts)


def _pair_share(fulls):
    n = len(fulls)

    def body(*refs):
        outs = refs[n:2 * n]
        ssem, rsem = refs[2 * n:]
        x, y, c, _ = _place()
        sends = []
        for t in range(n):
            cp = pltpu.make_async_remote_copy(
                src_ref=outs[t].at[:, c], dst_ref=outs[t].at[:, c], send_sem=ssem.at[t], recv_sem=rsem.at[t],
                device_id=(x, y, 1 - c), device_id_type=MESH)
            cp.start()
            sends.append(cp)
        for t in range(n):
            theirs = outs[t].at[:, 1 - c]
            pltpu.make_async_remote_copy(
                src_ref=theirs, dst_ref=theirs, send_sem=ssem.at[t], recv_sem=rsem.at[t],
                device_id=(x, y, 1 - c), device_id_type=MESH).wait_recv()
        for cp in sends:
            cp.wait_send()

    return pl.pallas_call(
        body, name="grad_pair_share", in_specs=[ANY] * n, out_specs=[ANY] * n,
        out_shape=[jax.ShapeDtypeStruct(a.shape, a.dtype) for a in fulls],
        input_output_aliases={t: t for t in range(n)},
        scratch_shapes=[pltpu.SemaphoreType.DMA((n,)), pltpu.SemaphoreType.DMA((n,))])(*fulls)


def _small_allreduce(part):
    R, C = part.shape
    N_DEV = 8

    def body(in_ref, out_ref, slots, ssem, rsem):
        x, y, c, _ = _place()
        me = 4 * x + 2 * y + c
        sends = []
        for k in range(1, N_DEV):
            kx, ky, kc = (k >> 2) & 1, (k >> 1) & 1, k & 1
            peer = (1 - x if kx else x, 1 - y if ky else y, 1 - c if kc else c)
            cp = pltpu.make_async_remote_copy(
                src_ref=in_ref, dst_ref=slots.at[me], send_sem=ssem.at[k], recv_sem=rsem.at[k],
                device_id=peer, device_id_type=MESH)
            cp.start()
            sends.append(cp)
        slots[me] = in_ref[...]
        for k in range(1, N_DEV):
            kx, ky, kc = (k >> 2) & 1, (k >> 1) & 1, k & 1
            peer = (1 - x if kx else x, 1 - y if ky else y, 1 - c if kc else c)
            slot = slots.at[4 * peer[0] + 2 * peer[1] + peer[2]]
            pltpu.make_async_remote_copy(
                src_ref=slot, dst_ref=slot, send_sem=ssem.at[k], recv_sem=rsem.at[k],
                device_id=peer, device_id_type=MESH).wait_recv()
        acc = slots[0]
        for d in range(1, N_DEV):
            acc = acc + slots[d]
        out_ref[...] = acc
        for cp in sends:
            cp.wait_send()

    vm = pl.BlockSpec(memory_space=pltpu.VMEM)
    return pl.pallas_call(
        body, name="small_allreduce", in_specs=[vm], out_specs=vm,
        out_shape=jax.ShapeDtypeStruct((R, C), F32),
        scratch_shapes=[pltpu.VMEM((N_DEV, R, C), F32), pltpu.SemaphoreType.DMA((N_DEV,)),
                        pltpu.SemaphoreType.DMA((N_DEV,))])(part)


def _local_step(x, target, norm_mix, norm_mlp, norm_kv, norm_final, W, cwg):
    B, S, D = x.shape
    T = B * S
    n_heads = W["w_o"].shape[2] // HEAD_DIM
    C = n_heads * HEAD_DIM
    n_a = W["w_a_in"].shape[1]
    n_b = W["w_q"].shape[1]
    depth = n_a + n_b
    F = W["w_up"].shape[3] * N_CHIPS
    slopes = 2.0 ** (-ALIBI_MAX_BIAS * jnp.arange(1, n_heads + 1, dtype=F32) / n_heads)
    tm = min(512, T)
    row = lambda v: v.reshape(1, -1)

    h = x.reshape(T, D)
    saved = []
    kv = nkv = h_kv = None
    for l in range(depth):
        s = {"h_in": h}
        if l < n_a:
            s["n1"], bcu = _norm_mm(f"a_in_fwd{l}", h, row(norm_mix[l]), W["w_a_in"], l, 3, BF16, tm)
            s["bcu"] = bcu.reshape(3, B, S, D)
            s["z"] = _conv_fwd(f"conv_fwd{l}", s["bcu"], cwg, l, LANES).reshape(T, D)
            h = _mm_res_rows(f"a_out_fwd{l}", s["z"], W["w_a_out"], l, h, _to_bf16, tm)
        else:
            i = l - n_a
            if i == 0:
                h_kv = h
                nkv, kv = _norm_mm("kv_fwd", h, row(norm_kv), W["w_kv"], 0, 1, F32, tm)
                kv = kv.reshape(B, S, 2 * 3 * C)
            s["n1"], q = _norm_mm(f"q_fwd{i}", h, row(norm_mix[l]), W["w_q"], i, 1, F32, tm)
            s["q"] = q.reshape(B, S, 3 * C)
            o, lse = _attn_fwd(f"attn_fwd{i}", s["q"], kv, slopes, n_heads)
            s["o"], s["lse"] = o.reshape(T, C), lse.reshape(T, C)
            h = _mm_res_cols(f"o_fwd{i}", s["o"], W["w_o"], i, h, tm)
        s["h_mid"] = h
        s["n2"], a = _norm_mm(f"up_fwd{l}", h, row(norm_mlp[l]), W["w_up"], l, 1, BF16, tm)
        s["a"] = a.reshape(T, F)
        h = _mm_res_rows(f"down_fwd{l}", s["a"], W["w_down"], l, h, _relu2_bf16, tm)
        saved.append(s)

    loss, dh, dg_final = _final_loss("loss_head", h, row(norm_final), target.reshape(T, D), tm)

    gw = {k: [None] * W[k].shape[1] for k in W}
    g_mix, g_mlp = [None] * depth, [None] * depth
    g_conv = [None] * n_a
    dkv = None
    tt = min(512, T)
    for l in reversed(range(depth)):
        s = saved[l]
        da = _nt_rows(f"down_bwd{l}", dh, W["w_down"], l, s["a"], BF16, tm)
        gw["w_down"][l] = _tn(f"down_wgrad{l}", s["a"], _relu2_bf16, [_seg2d(dh, tt, 2)], None, False,
                              min(1024, F), tt).reshape(N_CHIPS, F // N_CHIPS, D)
        gw["w_up"][l] = _tn(f"up_wgrad{l}", s["n2"], _to_bf16, [_seg2d(da, tt, 2)], F // N_CHIPS, True,
                            min(512, D), tt)
        dh, g_mlp[l] = _nt_cols(f"up_bwd{l}", [_seg2d(da, tm, 1)], W["w_up"], l, tm,
                                (s["h_mid"], row(norm_mlp[l]), dh))
        if l < n_a:
            gw["w_a_out"][l] = _tn(f"a_out_wgrad{l}", s["z"], _to_bf16, [_seg2d(dh, tt, 2)], None, False,
                                   D, tt).reshape(N_CHIPS, D // N_CHIPS, D)
            dz = _nt_rows(f"a_out_bwd{l}", dh, W["w_a_out"], l, None, F32, tm)
            dbcu, g_conv[l] = _conv_bwd(f"conv_bwd{l}", s["bcu"], dz.reshape(B, S, D), cwg, l, LANES)
            dbcu = dbcu.reshape(3, T, D)
            gw["w_a_in"][l] = _tn(f"a_in_wgrad{l}", s["n1"], _to_bf16,
                                  [_seg_plane(dbcu, p, tt, 2) for p in range(3)], 3 * D // N_CHIPS, True,
                                  min(512, D), tt)
            dh, g_mix[l] = _nt_cols(f"a_in_bwd{l}", [_seg_plane(dbcu, p, tm, 1) for p in range(3)],
                                    W["w_a_in"], l, tm, (s["h_in"], row(norm_mix[l]), dh))
        else:
            i = l - n_a
            gw["w_o"][i] = _tn(f"o_wgrad{i}", s["o"], _to_bf16, [_seg2d(dh, tt, 2)], D // N_CHIPS, True, C, tt)
            do = _nt_cols(f"o_bwd{i}", [_seg2d(dh, tm, 1)], W["w_o"], i, tm, None)
            dq, dk, dv = _attn_bwd(f"attn_bwd{i}", s["q"], kv, slopes, s["o"].reshape(B, S, C),
                                   s["lse"].reshape(B, S, C), do.reshape(B, S, C), n_heads, dkv)
            dkv = (dk, dv)
            dq = dq.reshape(T, 3 * C)
            gw["w_q"][i] = _tn(f"q_wgrad{i}", s["n1"], _to_bf16, [_seg2d(dq, tt, 2)],
                               3 * C // N_CHIPS, True, min(512, D), tt)
            dh, g_mix[l] = _nt_cols(f"q_bwd{i}", [_seg2d(dq, tm, 1)], W["w_q"], i, tm,
                                    (s["h_in"], row(norm_mix[l]), dh))
            if i == 0:
                dk2, dv2 = (t.reshape(T, 3 * C) for t in dkv)
                gw["w_kv"][0] = _tn("kv_wgrad", nkv, _to_bf16, _kv_segments(dk2, dv2, C, tt, 2),
                                    6 * C // N_CHIPS, True, min(512, D), tt)
                dh, g_kv = _nt_cols("kv_bwd", _kv_segments(dk2, dv2, C, tm, 1), W["w_kv"], 0, tm,
                                    (h_kv, row(norm_kv), dh))
    small = dict(norm_mix=jnp.concatenate(g_mix, axis=0), norm_mlp=jnp.concatenate(g_mlp, axis=0),
                 norm_kv=g_kv, norm_final=dg_final, conv_w=jnp.stack(g_conv))
    return loss, dh.reshape(B, S, D), gw, small


BIG = ("w_a_in", "w_a_out", "w_kv", "w_q", "w_o", "w_up", "w_down")
CONV_PAD_ROWS = 16


def _reduce_scatter(gw):
    layers_by_type = [[a.reshape(N_CHIPS, 2, a.shape[1] // 2, a.shape[2]) for a in gw[k]] for k in BIG]
    place = jnp.stack([2 * lax.axis_index("x") + lax.axis_index("y"), lax.axis_index("c")]).astype(jnp.int32)
    recv = _pair_exchange(layers_by_type)
    parts = [_pair_add(f"grad_pair_add_{k}", layers, r, place) for k, layers, r in zip(BIG, layers_by_type, recv)]
    slots = _chip_exchange(parts)
    fulls = [_chip_add(f"grad_chip_add_{k}", a, b, place) for k, a, b in zip(BIG, parts, slots)]
    outs = _pair_share(fulls)
    return {k: a.reshape(a.shape[0], a.shape[1] * a.shape[2], a.shape[3]) for k, a in zip(BIG, outs)}


def kernel(x, norm_mix, norm_mlp, w_a_in, conv_w, w_a_out, norm_kv, w_kv, w_q, w_o, w_up, w_down, norm_final, loss_target, m_norm_mix, m_norm_mlp, m_w_a_in, m_conv_w, m_w_a_out, m_norm_kv, m_w_kv, m_w_q, m_w_o, m_w_up, m_w_down, m_norm_final, v_norm_mix, v_norm_mlp, v_w_a_in, v_conv_w, v_w_a_out, v_norm_kv, v_w_kv, v_w_q, v_w_o, v_w_up, v_w_down, v_norm_final):
    D = x.shape[-1]
    w = dict(norm_mix=norm_mix, norm_mlp=norm_mlp, w_a_in=w_a_in, conv_w=conv_w, w_a_out=w_a_out, norm_kv=norm_kv,
             w_kv=w_kv[None], w_q=w_q, w_o=w_o, w_up=w_up, w_down=w_down, norm_final=norm_final)
    m = dict(norm_mix=m_norm_mix, norm_mlp=m_norm_mlp, w_a_in=m_w_a_in, conv_w=m_conv_w, w_a_out=m_w_a_out,
             norm_kv=m_norm_kv, w_kv=m_w_kv[None], w_q=m_w_q, w_o=m_w_o, w_up=m_w_up, w_down=m_w_down,
             norm_final=m_norm_final)
    v = dict(norm_mix=v_norm_mix, norm_mlp=v_norm_mlp, w_a_in=v_w_a_in, conv_w=v_conv_w, w_a_out=v_w_a_out,
             norm_kv=v_norm_kv, w_kv=v_w_kv[None], w_q=v_w_q, w_o=v_w_o, w_up=v_w_up, w_down=v_w_down,
             norm_final=v_norm_final)

    n_a, taps, cwc = conv_w.shape
    conv_rows = jnp.zeros((CONV_PAD_ROWS, cwc), F32).at[:n_a * taps].set(conv_w.reshape(n_a * taps, cwc))
    chunks = [w[k].astype(BF16).reshape(-1, w[k].shape[-1]) for k in BIG] + [conv_rows]
    gathered = _all_gather(chunks)
    W = {k: a.reshape((N_CHIPS,) + w[k].shape) for k, a in zip(BIG, gathered[:-1])}
    cwg = gathered[-1][:, :n_a * taps].reshape(N_CHIPS, n_a, taps, cwc)

    loss, grad_x, gw, small = _local_step(x, loss_target, norm_mix, norm_mlp, norm_kv, norm_final, W, cwg)
    loss = lax.psum(loss[0, 0], ("x", "y", "c"))

    grads = _reduce_scatter(gw)

    depth = norm_mix.shape[0]
    packed = jnp.concatenate([small["norm_mix"], small["norm_mlp"], small["norm_kv"], small["norm_final"],
                              small["conv_w"].reshape(n_a * taps, D)], axis=0)
    pad = (-packed.shape[0]) % 8
    packed = jnp.pad(packed, ((0, pad), (0, 0)))
    total = _small_allreduce(packed)
    grads["norm_mix"] = total[:depth]
    grads["norm_mlp"] = total[depth:2 * depth]
    grads["norm_kv"] = total[2 * depth]
    grads["norm_final"] = total[2 * depth + 1]
    chip = 2 * lax.axis_index("x") + lax.axis_index("y")
    conv_full = total[2 * depth + 2:2 * depth + 2 + n_a * taps].reshape(n_a, taps, N_CHIPS, cwc)
    grads["conv_w"] = lax.dynamic_index_in_dim(conv_full, chip, axis=2, keepdims=False)

    order = ("norm_mix", "norm_mlp", "w_a_in", "conv_w", "w_a_out", "norm_kv", "w_kv", "w_q", "w_o", "w_up",
             "w_down", "norm_final")
    delta, new_m, new_v = {}, {}, {}
    vec_names = ("norm_mix", "norm_mlp", "norm_kv", "norm_final")
    rows_of = lambda a: a.reshape(-1, D)
    vw, vg, vm_, vv = (jnp.concatenate([rows_of(t[k]) for k in vec_names], axis=0) for t in (w, grads, m, v))
    n_vec = vw.shape[0]
    vpad = (-n_vec) % 8
    padrows = lambda a: jnp.pad(a, ((0, vpad), (0, 0)))
    vd, vnm, vnv = _adamw("adamw_norms", padrows(vw), padrows(vg), padrows(vm_), padrows(vv))
    off = 0
    for k in vec_names:
        r = rows_of(w[k]).shape[0]
        delta[k] = vd[off:off + r].reshape(w[k].shape)
        new_m[k] = vnm[off:off + r].reshape(w[k].shape)
        new_v[k] = vnv[off:off + r].reshape(w[k].shape)
        off += r
    for k in BIG + ("conv_w",):
        shape = w[k].shape
        two_d = lambda a: a.reshape(-1, shape[-1])
        if k == "conv_w":
            cpad = (-n_a * taps) % 8
            two_d = lambda a: jnp.pad(a.reshape(-1, shape[-1]), ((0, cpad), (0, 0)))
        d, nm, nv = _adamw(f"adamw_{k}", two_d(w[k]), two_d(grads[k]), two_d(m[k]), two_d(v[k]))
        rows = shape[0] * shape[1] if len(shape) == 3 else shape[0]
        delta[k], new_m[k], new_v[k] = (t[:rows].reshape(shape) for t in (d, nm, nv))
    fix = lambda k, a: a[0] if k == "w_kv" else a
    return (loss, grad_x, *[fix(k, grads[k]).reshape(fix(k, w[k]).shape) for k in order],
            *[fix(k, delta[k]) for k in order], *[fix(k, new_m[k]) for k in order],
            *[fix(k, new_v[k]) for k in order])
```

```python
import functools

import jax
import jax.numpy as jnp
from jax import lax
from jax.experimental import pallas as pl
from jax.experimental.pallas import tpu as pltpu

F32 = jnp.float32
BF16 = jnp.bfloat16
MESH = pl.DeviceIdType.MESH

EPS = 1e-5
PATTERNS = ((128, 1), (512, 4), (2048, 16))
HEAD_DIM = 64
ALIBI_MAX_BIAS = 8.0
NEG_INF = -1e30
ATT_BLK = 128
N_CHIPS = 4
LANES = 128
VMEM_LIMIT = 56 * 1024 * 1024

ADAM_LR = 0.001
ADAM_B1 = 0.9
ADAM_B2 = 0.999
ADAM_EPS = 1e-08
ADAM_WD = 0.01
ADAM_STEP = 10


ANY = pl.BlockSpec(memory_space=pl.ANY)


def _params(n_grid_axes):
    return pltpu.CompilerParams(dimension_semantics=("arbitrary",) * n_grid_axes, vmem_limit_bytes=VMEM_LIMIT)


def _dot(a, b):
    return jnp.dot(a, b, preferred_element_type=F32)


def _dot_nt(a, b):
    return lax.dot_general(a, b, (((1,), (1,)), ((), ())), preferred_element_type=F32)


def _dot_tn(a, b):
    return lax.dot_general(a, b, (((0,), (0,)), ((), ())), preferred_element_type=F32)


def _relu2(a):
    return jnp.square(jnp.maximum(a, 0.0))


def _rms(hf, g):
    y = hf * lax.rsqrt(jnp.mean(hf * hf, axis=-1, keepdims=True) + EPS)
    return y * g


def _rms_bwd(hf, g, dn):
    rstd = lax.rsqrt(jnp.mean(hf * hf, axis=-1, keepdims=True) + EPS)
    xhat = hf * rstd
    dg = jnp.sum(dn * xhat, axis=0, keepdims=True)
    dx = dn * g
    dh = rstd * (dx - xhat * jnp.mean(dx * xhat, axis=-1, keepdims=True))
    return dh, dg


def _pieces(seg_widths, chunk_width, max_width):
    total = sum(seg_widths)
    cuts = {0, total}
    acc = 0
    for w in seg_widths:
        cuts.add(acc)
        acc += w
    cuts.update(range(0, total, chunk_width))
    cuts = sorted(cuts)
    fine = []
    for lo, hi in zip(cuts[:-1], cuts[1:]):
        while hi - lo > max_width:
            fine.append((lo, lo + max_width))
            lo += max_width
        fine.append((lo, hi))
    out = []
    for lo, hi in fine:
        acc = 0
        for s, w in enumerate(seg_widths):
            if lo < acc + w:
                break
            acc += w
        out.append((s, lo - acc, lo // chunk_width, lo % chunk_width, hi - lo))
    return out


def _relu2_bf16(a):
    return _relu2(a.astype(F32)).astype(BF16)


def _to_bf16(a):
    return a.astype(BF16)


def _norm_mm(name, h, g, wg, layer, planes, out_dtype, tm):
    T, D = h.shape
    cw = wg.shape[3]
    N = N_CHIPS * cw
    pw = N // planes
    pieces = _pieces([pw] * planes, cw, 512)

    def body(h_ref, g_ref, w_ref, n_ref, o_ref):
        n = _rms(h_ref[...], g_ref[...]).astype(BF16)
        n_ref[...] = n
        for s, a0, ch, b0, wd in pieces:
            o_ref[s, :, a0:a0 + wd] = _dot(n, w_ref[ch, :, b0:b0 + wd]).astype(out_dtype)

    return pl.pallas_call(
        body, name=name, grid=(T // tm,),
        in_specs=[pl.BlockSpec((tm, D), lambda i: (i, 0)),
                  pl.BlockSpec((1, D), lambda i: (0, 0)),
                  pl.BlockSpec((N_CHIPS, None, D, cw), lambda i: (0, layer, 0, 0))],
        out_specs=[pl.BlockSpec((tm, D), lambda i: (i, 0)),
                   pl.BlockSpec((planes, tm, pw), lambda i: (0, i, 0))],
        out_shape=[jax.ShapeDtypeStruct((T, D), BF16), jax.ShapeDtypeStruct((planes, T, pw), out_dtype)],
        compiler_params=_params(1))(h, g, wg)


def _mm_res_rows(name, a, wg, layer, h, act, tm):
    T = a.shape[0]
    rk, D = wg.shape[2], wg.shape[3]

    def body(a_ref, w_ref, h_ref, o_ref):
        acc = h_ref[...]
        for k in range(N_CHIPS):
            acc = acc + _dot(act(a_ref[:, k * rk:(k + 1) * rk]), w_ref[k])
        o_ref[...] = acc

    return pl.pallas_call(
        body, name=name, grid=(T // tm,),
        in_specs=[pl.BlockSpec((tm, N_CHIPS * rk), lambda i: (i, 0)),
                  pl.BlockSpec((N_CHIPS, None, rk, D), lambda i: (0, layer, 0, 0)),
                  pl.BlockSpec((tm, D), lambda i: (i, 0))],
        out_specs=pl.BlockSpec((tm, D), lambda i: (i, 0)),
        out_shape=jax.ShapeDtypeStruct((T, D), F32),
        compiler_params=_params(1))(a, wg, h)


def _mm_res_cols(name, a, wg, layer, h, tm):
    T, K = a.shape
    cw = wg.shape[3]
    D = N_CHIPS * cw

    def body(a_ref, w_ref, h_ref, o_ref):
        a16 = a_ref[...].astype(BF16)
        for j in range(N_CHIPS):
            o_ref[:, j * cw:(j + 1) * cw] = h_ref[:, j * cw:(j + 1) * cw] + _dot(a16, w_ref[j])

    return pl.pallas_call(
        body, name=name, grid=(T // tm,),
        in_specs=[pl.BlockSpec((tm, K), lambda i: (i, 0)),
                  pl.BlockSpec((N_CHIPS, None, K, cw), lambda i: (0, layer, 0, 0)),
                  pl.BlockSpec((tm, D), lambda i: (i, 0))],
        out_specs=pl.BlockSpec((tm, D), lambda i: (i, 0)),
        out_shape=jax.ShapeDtypeStruct((T, D), F32),
        compiler_params=_params(1))(a, wg, h)


CONV_ROWS = 256
CONV_HALO = 16


def _conv_shifted(ext, k, r0, rows):
    rolled = pltpu.roll(ext, k, 0)[CONV_HALO:]
    t = r0 + lax.broadcasted_iota(jnp.int32, rolled.shape, 0)
    return jnp.where(t >= k, rolled, 0.0)


def _conv_ahead(ext, k, r0, rows, S):
    rolled = pltpu.roll(ext, rows + CONV_HALO - k, 0)[:rows]
    t = r0 + lax.broadcasted_iota(jnp.int32, rolled.shape, 0)
    return jnp.where(t + k < S, rolled, 0.0)


def _conv_fwd(name, bcu, cwg, layer, tc):
    _, B, S, D = bcu.shape
    cwc = cwg.shape[3]
    per_chunk = cwc // tc
    R = min(CONV_ROWS, S)

    def body(x_ref, w_ref, z_ref):
        w = [w_ref[k:k + 1, :] for k in range(3)]

        def step(i, carry):
            r0 = pl.multiple_of(i * R, R)
            h0 = pl.multiple_of(jnp.maximum(r0 - CONV_HALO, 0), CONV_HALO)
            ld = lambda p, start, rows: x_ref[p, pl.ds(start, rows), :].astype(F32)
            cu = jnp.concatenate([ld(1, h0, CONV_HALO) * ld(2, h0, CONV_HALO), ld(1, r0, R) * ld(2, r0, R)], axis=0)
            conv = w[0] * cu[CONV_HALO:]
            conv = conv + w[1] * _conv_shifted(cu, 1, r0, R)
            conv = conv + w[2] * _conv_shifted(cu, 2, r0, R)
            z_ref[pl.ds(r0, R), :] = (ld(0, r0, R) * conv).astype(BF16)
            return carry

        lax.fori_loop(0, S // R, step, 0)

    return pl.pallas_call(
        body, name=name, grid=(B, D // tc),
        in_specs=[pl.BlockSpec((3, None, S, tc), lambda b, j: (0, b, 0, j)),
                  pl.BlockSpec((None, None, 3, tc), lambda b, j: (j // per_chunk, layer, 0, j % per_chunk))],
        out_specs=pl.BlockSpec((None, S, tc), lambda b, j: (b, 0, j)),
        out_shape=jax.ShapeDtypeStruct((B, S, D), BF16),
        compiler_params=_params(2))(bcu, cwg)


def _conv_bwd(name, bcu, dz, cwg, layer, tc):
    _, B, S, D = bcu.shape
    cwc = cwg.shape[3]
    per_chunk = cwc // tc
    R = min(CONV_ROWS, S)

    def body(x_ref, dz_ref, w_ref, d_ref, dw_ref):
        w = [w_ref[k:k + 1, :] for k in range(3)]

        @pl.when(pl.program_id(1) == 0)
        def _():
            dw_ref[...] = jnp.zeros_like(dw_ref)

        def step(i, carry):
            r0 = pl.multiple_of(i * R, R)
            h0 = pl.multiple_of(jnp.maximum(r0 - CONV_HALO, 0), CONV_HALO)
            a0 = pl.multiple_of(jnp.minimum(r0 + R, S - CONV_HALO), CONV_HALO)
            ld = lambda p, start, rows: x_ref[p, pl.ds(start, rows), :].astype(F32)
            b, c, u = ld(0, r0, R), ld(1, r0, R), ld(2, r0, R)
            dz = dz_ref[pl.ds(r0, R), :]
            cu = jnp.concatenate([ld(1, h0, CONV_HALO) * ld(2, h0, CONV_HALO), c * u], axis=0)
            cu1 = _conv_shifted(cu, 1, r0, R)
            cu2 = _conv_shifted(cu, 2, r0, R)
            conv = w[0] * (c * u) + w[1] * cu1 + w[2] * cu2
            dconv = dz * b
            dca = jnp.concatenate([dconv, dz_ref[pl.ds(a0, CONV_HALO), :] * ld(0, a0, CONV_HALO)], axis=0)
            dcu = w[0] * dconv + w[1] * _conv_ahead(dca, 1, r0, R, S) + w[2] * _conv_ahead(dca, 2, r0, R, S)
            d_ref[0, pl.ds(r0, R), :] = (dz * conv).astype(BF16)
            d_ref[1, pl.ds(r0, R), :] = (dcu * u).astype(BF16)
            d_ref[2, pl.ds(r0, R), :] = (dcu * c).astype(BF16)
            return (carry[0] + jnp.sum(dconv * (c * u), axis=0, keepdims=True),
                    carry[1] + jnp.sum(dconv * cu1, axis=0, keepdims=True),
                    carry[2] + jnp.sum(dconv * cu2, axis=0, keepdims=True))

        zero = jnp.zeros((1, tc), F32)
        s0, s1, s2 = lax.fori_loop(0, S // R, step, (zero, zero, zero))
        for k, sk in enumerate((s0, s1, s2)):
            dw_ref[k:k + 1, :] += sk

    return pl.pallas_call(
        body, name=name, grid=(D // tc, B),
        in_specs=[pl.BlockSpec((3, None, S, tc), lambda j, b: (0, b, 0, j)),
                  pl.BlockSpec((None, S, tc), lambda j, b: (b, 0, j)),
                  pl.BlockSpec((None, None, 3, tc), lambda j, b: (j // per_chunk, layer, 0, j % per_chunk))],
        out_specs=[pl.BlockSpec((3, None, S, tc), lambda j, b: (0, b, 0, j)),
                   pl.BlockSpec((3, tc), lambda j, b: (0, j))],
        out_shape=[jax.ShapeDtypeStruct((3, B, S, D), BF16), jax.ShapeDtypeStruct((3, D), F32)],
        compiler_params=_params(2))(bcu, dz, cwg)


def _att_rows(dil, idx, nb):
    r, n = idx // nb, idx % nb
    if dil == 1:
        cur = pl.ds(pl.multiple_of(n * ATT_BLK, ATT_BLK), ATT_BLK)
        prev = pl.ds(pl.multiple_of(jnp.maximum(n - 1, 0) * ATT_BLK, ATT_BLK), ATT_BLK)
    else:
        cur = pl.ds(n * (ATT_BLK * dil) + r, ATT_BLK, stride=dil)
        prev = pl.ds(jnp.maximum(n - 1, 0) * (ATT_BLK * dil) + r, ATT_BLK, stride=dil)
    return n, cur, prev


def _att_tiles(dil, n, sl_ref, hp):
    row = lax.broadcasted_iota(jnp.int32, (2 * ATT_BLK, 2 * ATT_BLK), 0)
    ci = lax.broadcasted_iota(jnp.int32, (2 * ATT_BLK, 2 * ATT_BLK), 1)
    j = ATT_BLK + (row & (ATT_BLK - 1)) - ci
    valid = (j >= 0) & (j <= ATT_BLK) & (ci >= jnp.where(n > 0, 0, ATT_BLK))
    slope = jnp.where(row < ATT_BLK, sl_ref[2 * hp], sl_ref[2 * hp + 1])
    return valid, slope * (dil * j).astype(F32)


def _stack_heads(x16, lane):
    first = lane < HEAD_DIM
    return jnp.concatenate([jnp.where(first, x16, jnp.zeros_like(x16)),
                            jnp.where(first, jnp.zeros_like(x16), x16)], axis=0)


def _per_head(col, lane):
    return jnp.where(lane < HEAD_DIM, col[:ATT_BLK], col[ATT_BLK:])


def _attn_fwd(name, q, kv, slopes, n_heads):
    B, S, CQ = q.shape
    HP = n_heads * HEAD_DIM // LANES
    scale = HEAD_DIM ** -0.5
    n_groups = len(PATTERNS)
    CH = 256

    def body(sl_ref, q_ref, k_ref, v_ref, o_ref, lse_ref, *parts):
        og, lg = parts[:n_groups], parts[n_groups:]
        hp, g = pl.program_id(1), pl.program_id(2)
        lane = lax.broadcasted_iota(jnp.int32, (1, LANES), 1)

        for gi, (window, dil) in enumerate(PATTERNS):
            nb = S // dil // ATT_BLK

            @pl.when(g == gi)
            def _(gi=gi, dil=dil, nb=nb):
                def step(idx, carry):
                    n, cur, prev = _att_rows(dil, idx, nb)
                    valid, bias = _att_tiles(dil, n, sl_ref, hp)
                    qs = _stack_heads((q_ref[cur, :] * scale).astype(BF16), lane)
                    kc = jnp.concatenate([k_ref[prev, :], k_ref[cur, :]], axis=0).astype(BF16)
                    vc = jnp.concatenate([v_ref[prev, :], v_ref[cur, :]], axis=0).astype(BF16)
                    s = jnp.where(valid, _dot_nt(qs, kc) - bias, NEG_INF)
                    m = jnp.max(s, axis=-1, keepdims=True)
                    p = jnp.exp(s - m)
                    l = jnp.sum(p, axis=-1, keepdims=True)
                    p16 = p.astype(BF16)
                    o_un = _dot(jnp.concatenate([p16[:ATT_BLK], p16[ATT_BLK:]], axis=1), _stack_heads_rows(vc, lane))
                    og[gi][cur, :] = o_un / _per_head(l, lane)
                    lg[gi][cur, :] = _per_head(m + jnp.log(l), lane)
                    return carry

                lax.fori_loop(0, S // ATT_BLK, step, 0, unroll=2)

        @pl.when(g == n_groups - 1)
        def _():
            def comb(i, carry):
                rows = pl.ds(pl.multiple_of(i * CH, CH), CH)
                a, b, c = lg[0][rows, :], lg[1][rows, :], lg[2][rows, :]
                m = jnp.maximum(jnp.maximum(a, b), c)
                ea, eb, ec = jnp.exp(a - m), jnp.exp(b - m), jnp.exp(c - m)
                z = ea + eb + ec
                o_ref[rows, :] = (ea / z) * og[0][rows, :] + (eb / z) * og[1][rows, :] + (ec / z) * og[2][rows, :]
                lse_ref[rows, :] = m + jnp.log(z)
                return carry

            lax.fori_loop(0, S // CH, comb, 0)

    blk = (None, S, LANES)
    out = pl.BlockSpec(blk, lambda b, hp, g: (b, 0, hp))
    return pl.pallas_call(
        body, name=name, grid=(B, HP, n_groups),
        in_specs=[pl.BlockSpec(memory_space=pltpu.SMEM),
                  pl.BlockSpec(blk, lambda b, hp, g: (b, 0, g * HP + hp)),
                  pl.BlockSpec(blk, lambda b, hp, g: (b, 0, g * 2 * HP + hp)),
                  pl.BlockSpec(blk, lambda b, hp, g: (b, 0, g * 2 * HP + HP + hp))],
        out_specs=[out, out],
        out_shape=[jax.ShapeDtypeStruct((B, S, HP * LANES), F32)] * 2,
        scratch_shapes=[pltpu.VMEM((S, LANES), F32)] * (2 * n_groups),
        compiler_params=_params(3))(slopes, q, kv, kv)


def _stack_heads_rows(x16, lane):
    first = lane < HEAD_DIM
    return jnp.concatenate([jnp.where(first, x16, jnp.zeros_like(x16)),
                            jnp.where(first, jnp.zeros_like(x16), x16)], axis=0)


def _attn_bwd(name, q, kv, slopes, o, lse, do, n_heads, dkv_prev):
    B, S, CQ = q.shape
    HP = n_heads * HEAD_DIM // LANES
    scale = HEAD_DIM ** -0.5
    n_groups = len(PATTERNS)
    n_prev = 0 if dkv_prev is None else 2

    def body(sl_ref, q_ref, k_ref, v_ref, o_ref, lse_ref, do_ref, *rest):
        dq_ref, dk_ref, dv_ref = rest[n_prev:]
        hp, g = pl.program_id(1), pl.program_id(2)
        lane = lax.broadcasted_iota(jnp.int32, (1, LANES), 1)
        first = lane < HEAD_DIM

        def flush(rows, dk, dv):
            if n_prev:
                dk = dk + rest[0][rows, :]
                dv = dv + rest[1][rows, :]
            dk_ref[rows, :] = dk
            dv_ref[rows, :] = dv

        for gi, (window, dil) in enumerate(PATTERNS):
            nb = S // dil // ATT_BLK
            n_blocks = S // ATT_BLK

            @pl.when(g == gi)
            def _(dil=dil, nb=nb, n_blocks=n_blocks):
                def step(idx, carry):
                    n, cur, prev = _att_rows(dil, idx, nb)
                    valid, bias = _att_tiles(dil, n, sl_ref, hp)
                    qs = _stack_heads((q_ref[cur, :] * scale).astype(BF16), lane)
                    kc = jnp.concatenate([k_ref[prev, :], k_ref[cur, :]], axis=0).astype(BF16)
                    vc = jnp.concatenate([v_ref[prev, :], v_ref[cur, :]], axis=0).astype(BF16)
                    dob = do_ref[cur, :]
                    prod = dob * o_ref[cur, :]
                    lseb = lse_ref[cur, :]
                    dos = _stack_heads(dob.astype(BF16), lane)
                    delta = jnp.concatenate(
                        [jnp.sum(jnp.where(first, prod, 0.0), axis=-1, keepdims=True),
                         jnp.sum(jnp.where(first, 0.0, prod), axis=-1, keepdims=True)], axis=0)
                    lse_col = jnp.concatenate(
                        [jnp.max(jnp.where(first, lseb, -jnp.inf), axis=-1, keepdims=True),
                         jnp.max(jnp.where(first, -jnp.inf, lseb), axis=-1, keepdims=True)], axis=0)
                    s = jnp.where(valid, _dot_nt(qs, kc) - bias, NEG_INF)
                    p = jnp.exp(s - lse_col)
                    ds = p * (_dot_nt(dos, vc) - delta)
                    ds16 = ds.astype(BF16)
                    dq = _dot(jnp.concatenate([ds16[:ATT_BLK], ds16[ATT_BLK:]], axis=1), _stack_heads_rows(kc, lane))
                    dq_ref[cur, :] = dq * scale
                    dk = _dot(ds.T.astype(BF16), qs)
                    dv = _dot(p.T.astype(BF16), dos)

                    @pl.when(idx > 0)
                    def _():
                        _, before, _ = _att_rows(dil, idx - 1, nb)
                        flush(before, carry[0] + dk[:ATT_BLK], carry[1] + dv[:ATT_BLK])

                    return dk[ATT_BLK:], dv[ATT_BLK:]

                zero = jnp.zeros((ATT_BLK, LANES), F32)
                dk_last, dv_last = lax.fori_loop(0, n_blocks, step, (zero, zero), unroll=2)
                _, last, _ = _att_rows(dil, n_blocks - 1, nb)
                flush(last, dk_last, dv_last)

    blk = (None, S, LANES)
    shared = pl.BlockSpec(blk, lambda b, hp, g: (b, 0, hp))
    grouped = pl.BlockSpec(blk, lambda b, hp, g: (b, 0, g * HP + hp))
    prev = [] if dkv_prev is None else list(dkv_prev)
    gshape = jax.ShapeDtypeStruct((B, S, n_groups * HP * LANES), F32)
    return pl.pallas_call(
        body, name=name, grid=(B, HP, n_groups),
        in_specs=[pl.BlockSpec(memory_space=pltpu.SMEM), grouped,
                  pl.BlockSpec(blk, lambda b, hp, g: (b, 0, g * 2 * HP + hp)),
                  pl.BlockSpec(blk, lambda b, hp, g: (b, 0, g * 2 * HP + HP + hp)),
                  shared, shared, shared] + [grouped] * n_prev,
        out_specs=[grouped] * 3, out_shape=[gshape] * 3,
        compiler_params=_params(3))(slopes, q, kv, kv, o, lse, do, *prev)


def _final_loss(name, h, g, target, tm):
    T, D = h.shape

    def body(h_ref, g_ref, t_ref, loss_ref, dh_ref, dg_ref):
        hf = h_ref[...]
        gv = g_ref[...]
        rstd = lax.rsqrt(jnp.mean(hf * hf, axis=-1, keepdims=True) + EPS)
        xhat = hf * rstd
        err = xhat * gv - t_ref[...]
        part = 0.5 * jnp.sum(jnp.mean(err * err, axis=-1, keepdims=True), axis=0, keepdims=True)
        dy = err * (1.0 / D)
        dg = jnp.sum(dy * xhat, axis=0, keepdims=True)
        dx = dy * gv
        dh_ref[...] = rstd * (dx - xhat * jnp.mean(dx * xhat, axis=-1, keepdims=True))

        @pl.when(pl.program_id(0) == 0)
        def _():
            loss_ref[...] = part
            dg_ref[...] = dg

        @pl.when(pl.program_id(0) > 0)
        def _():
            loss_ref[...] += part
            dg_ref[...] += dg

    return pl.pallas_call(
        body, name=name, grid=(T // tm,),
        in_specs=[pl.BlockSpec((tm, D), lambda i: (i, 0)), pl.BlockSpec((1, D), lambda i: (0, 0)),
                  pl.BlockSpec((tm, D), lambda i: (i, 0))],
        out_specs=[pl.BlockSpec((1, 1), lambda i: (0, 0)), pl.BlockSpec((tm, D), lambda i: (i, 0)),
                   pl.BlockSpec((1, D), lambda i: (0, 0))],
        out_shape=[jax.ShapeDtypeStruct((1, 1), F32), jax.ShapeDtypeStruct((T, D), F32),
                   jax.ShapeDtypeStruct((1, D), F32)],
        compiler_params=_params(1))(h, g, target)


def _nt_rows(name, dh, wg, layer, a_mul, out_dtype, tm, deps=()):
    T, D = dh.shape
    rk = wg.shape[2]
    N = N_CHIPS * rk
    with_a = a_mul is not None

    def body(dh_ref, w_ref, *rest):
        o_ref = rest[-1]
        d16 = dh_ref[...].astype(BF16)
        for ch in range(N_CHIPS):
            r = _dot_nt(d16, w_ref[ch])
            if with_a:
                r = r * (2.0 * jnp.maximum(rest[0][:, ch * rk:(ch + 1) * rk].astype(F32), 0.0))
            o_ref[:, ch * rk:(ch + 1) * rk] = r.astype(out_dtype)

    in_specs = [pl.BlockSpec((tm, D), lambda i: (i, 0)),
                pl.BlockSpec((N_CHIPS, None, rk, D), lambda i: (0, layer, 0, 0))]
    args = [dh, wg]
    if with_a:
        in_specs.append(pl.BlockSpec((tm, N), lambda i: (i, 0)))
        args.append(a_mul)
    in_specs += [ANY] * len(deps)
    args += list(deps)
    return pl.pallas_call(
        body, name=name, grid=(T // tm,), in_specs=in_specs,
        out_specs=pl.BlockSpec((tm, N), lambda i: (i, 0)),
        out_shape=jax.ShapeDtypeStruct((T, N), out_dtype),
        compiler_params=_params(1))(*args)


def _nt_cols(name, ysegs, wg, layer, tm, norm):
    Nw, cw = wg.shape[2], wg.shape[3]
    widths = [bs[-1] for _, bs, _ in ysegs]
    pieces = _pieces(widths, cw, 1024)
    ns = len(ysegs)
    T = norm[0].shape[0] if norm is not None else ysegs[0][0].shape[-2]

    def body(*refs):
        y_refs = refs[:ns]
        w_ref = refs[ns]
        acc = refs[-1]
        for n, (s, a0, ch, b0, wd) in enumerate(pieces):
            d = _dot_nt(y_refs[s][:, a0:a0 + wd].astype(BF16), w_ref[ch, :, b0:b0 + wd])
            if n == 0:
                acc[...] = d
            else:
                acc[...] += d
        if norm is None:
            refs[ns + 1][...] = acc[...]
        else:
            h_ref, g_ref, dhin_ref, out_ref, dg_ref = refs[ns + 1:ns + 6]
            dh_c, dg = _rms_bwd(h_ref[...], g_ref[...], acc[...])
            out_ref[...] = dhin_ref[...] + dh_c

            @pl.when(pl.program_id(0) == 0)
            def _():
                dg_ref[...] = dg

            @pl.when(pl.program_id(0) > 0)
            def _():
                dg_ref[...] += dg

    in_specs = [pl.BlockSpec(bs, im) for _, bs, im in ysegs]
    in_specs.append(pl.BlockSpec((N_CHIPS, None, Nw, cw), lambda i: (0, layer, 0, 0)))
    args = [a for a, _, _ in ysegs] + [wg]
    row = pl.BlockSpec((tm, Nw), lambda i: (i, 0))
    vec = pl.BlockSpec((1, Nw), lambda i: (0, 0))
    if norm is None:
        out_specs = row
        out_shape = jax.ShapeDtypeStruct((T, Nw), F32)
    else:
        in_specs += [row, vec, row]
        args += list(norm)
        out_specs = [row, vec]
        out_shape = [jax.ShapeDtypeStruct((T, Nw), F32), jax.ShapeDtypeStruct((1, Nw), F32)]
    return pl.pallas_call(
        body, name=name, grid=(T // tm,), in_specs=in_specs, out_specs=out_specs, out_shape=out_shape,
        scratch_shapes=[pltpu.VMEM((tm, Nw), F32)], compiler_params=_params(1))(*args)


def _tn(name, x, x_act, ysegs, cw, cols_layout, tmm, tt, deps=()):
    T, M = x.shape
    widths = [bs[-1] for _, bs, _ in ysegs]
    N = sum(widths)
    pieces = _pieces(widths, cw if cols_layout else N, 1024)
    ns = len(ysegs)

    def body(x_ref, *refs):
        y_refs = refs[:ns]
        o_ref = refs[-1]

        @pl.when(pl.program_id(1) == 0)
        def _():
            o_ref[...] = jnp.zeros_like(o_ref)

        xt = x_act(x_ref[...])
        for s, a0, ch, b0, wd in pieces:
            d = _dot_tn(xt, y_refs[s][:, a0:a0 + wd].astype(BF16))
            if cols_layout:
                o_ref[ch, :, b0:b0 + wd] += d
            else:
                o_ref[:, b0:b0 + wd] += d

    in_specs = [pl.BlockSpec((tt, tmm), lambda m, t: (t, m))] + [pl.BlockSpec(bs, im) for _, bs, im in ysegs]
    in_specs += [ANY] * len(deps)
    if cols_layout:
        out_specs = pl.BlockSpec((N_CHIPS, tmm, cw), lambda m, t: (0, m, 0))
        out_shape = jax.ShapeDtypeStruct((N_CHIPS, M, cw), F32)
    else:
        out_specs = pl.BlockSpec((tmm, N), lambda m, t: (m, 0))
        out_shape = jax.ShapeDtypeStruct((M, N), F32)
    return pl.pallas_call(
        body, name=name, grid=(M // tmm, T // tt), in_specs=in_specs, out_specs=out_specs, out_shape=out_shape,
        compiler_params=_params(2))(x, *[a for a, _, _ in ysegs], *deps)


def _seg2d(a, t_rows, grid_rank):
    w = a.shape[1]
    if grid_rank == 1:
        return (a, (t_rows, w), lambda i: (i, 0))
    return (a, (t_rows, w), lambda m, t: (t, 0))


def _kv_segments(dk, dv, C, t_rows, grid_rank):
    segs = []
    for g in range(len(PATTERNS)):
        for a in (dk, dv):
            if grid_rank == 1:
                segs.append((a, (t_rows, C), lambda i, g=g: (i, g)))
            else:
                segs.append((a, (t_rows, C), lambda m, t, g=g: (t, g)))
    return segs


def _seg_plane(a, plane, t_rows, grid_rank):
    w = a.shape[2]
    if grid_rank == 1:
        return (a, (None, t_rows, w), lambda i: (plane, i, 0))
    return (a, (None, t_rows, w), lambda m, t: (plane, t, 0))


def _row_tile(rows, row_bytes, budget_bytes=2 * 1024 * 1024):
    t = rows
    while t * row_bytes > budget_bytes and t % 32 == 0:
        t //= 2
    return t


def _pair_add(name, layers, recv, place):
    L = len(layers)
    _, _, hr, c = layers[0].shape
    tr = _row_tile(hr, L * c * 4)

    def body(place_ref, *refs):
        r_ref, o_ref = refs[L], refs[L + 1]
        for l in range(L):
            o_ref[l] = (refs[l][...] + r_ref[l]).astype(BF16)

    stacked = pl.BlockSpec((None, L, tr, c), lambda q, i, pr: (q, 0, i, 0))
    grid_spec = pltpu.PrefetchScalarGridSpec(
        num_scalar_prefetch=1, grid=(N_CHIPS, hr // tr),
        in_specs=[pl.BlockSpec((None, None, tr, c), lambda q, i, pr: (q, pr[1], i, 0))] * L + [stacked],
        out_specs=stacked)
    return pl.pallas_call(body, name=name, grid_spec=grid_spec,
                          out_shape=jax.ShapeDtypeStruct((N_CHIPS, L, hr, c), BF16),
                          compiler_params=_params(2))(place, *layers, recv)


def _chip_add(name, part, slots, place):
    _, L, hr, c = part.shape
    tr = _row_tile(hr, L * c * 4)

    def body(place_ref, own, s1, s2, s3, o_ref):
        f = lambda r: r[...].astype(F32)
        o_ref[...] = ((f(own) + f(s1)) + f(s2)) + f(s3)

    def slot(k):
        return pl.BlockSpec((None, L, tr, c), lambda i, pr: ((pr[0] + k) % N_CHIPS, 0, i, 0))

    grid_spec = pltpu.PrefetchScalarGridSpec(
        num_scalar_prefetch=1, grid=(hr // tr,),
        in_specs=[slot(0), slot(1), slot(2), slot(3)],
        out_specs=pl.BlockSpec((L, None, tr, c), lambda i, pr: (0, pr[1], i, 0)))
    return pl.pallas_call(body, name=name, grid_spec=grid_spec,
                          out_shape=jax.ShapeDtypeStruct((L, 2, hr, c), F32),
                          compiler_params=_params(1))(place, part, slots, slots, slots)


def _adamw(name, w, g, m, v):
    rows, cols = w.shape
    tr = _row_tile(rows, cols * 4, 1024 * 1024)

    def body(w_ref, g_ref, m_ref, v_ref, d_ref, nm_ref, nv_ref):
        d_ref[...], nm_ref[...], nv_ref[...] = _adamw_math(w_ref[...], g_ref[...], m_ref[...], v_ref[...])

    spec = pl.BlockSpec((tr, cols), lambda i: (i, 0))
    return pl.pallas_call(
        body, name=name, grid=(rows // tr,), in_specs=[spec] * 4, out_specs=[spec] * 3,
        out_shape=[jax.ShapeDtypeStruct((rows, cols), F32)] * 3, compiler_params=_params(1))(w, g, m, v)


def _adamw_math(w, g, m, v):
    nm = ADAM_B1 * m + (1.0 - ADAM_B1) * g
    nv = ADAM_B2 * v + (1.0 - ADAM_B2) * jnp.square(g)
    m_hat = nm / (1.0 - ADAM_B1 ** ADAM_STEP)
    v_hat = nv / (1.0 - ADAM_B2 ** ADAM_STEP)
    return -ADAM_LR * (m_hat / (jnp.sqrt(v_hat) + ADAM_EPS) + ADAM_WD * w), nm, nv


def _adamw_layers(name, w, grads, m, v):
    L, r, c = w.shape
    tr = _row_tile(r, L * c * 4, 1024 * 1024)

    def body(*refs):
        w_ref, m_ref, v_ref = refs[:3]
        g_refs = refs[3:3 + L]
        go_ref, d_ref, nm_ref, nv_ref = refs[3 + L:]
        for l in range(L):
            g = g_refs[l][...]
            go_ref[l] = g
            d_ref[l], nm_ref[l], nv_ref[l] = _adamw_math(w_ref[l], g, m_ref[l], v_ref[l])

    stacked = pl.BlockSpec((L, tr, c), lambda i: (0, i, 0))
    return pl.pallas_call(
        body, name=name, grid=(r // tr,),
        in_specs=[stacked] * 3 + [pl.BlockSpec((tr, c), lambda i: (i, 0))] * L, out_specs=[stacked] * 4,
        out_shape=[jax.ShapeDtypeStruct((L, r, c), F32)] * 4, compiler_params=_params(1))(w, m, v, *grads)


def _place():
    x, y, c = lax.axis_index("x"), lax.axis_index("y"), lax.axis_index("c")
    chips = [(1 - x, y), (x, 1 - y), (1 - x, 1 - y)]
    return x, y, c, chips


HBM = pl.BlockSpec(memory_space=pltpu.HBM)
SEM = pl.BlockSpec(memory_space=pltpu.SEMAPHORE)
EFFECT = pltpu.SideEffectType.DATAFLOW_SIDE_EFFECTING


class _Copy:
    def __init__(self, src, src_view, land, dst_view, recv_view, target):
        self.src, self.src_view, self.land, self.dst_view, self.recv_view, self.target = (
            src, src_view, land, dst_view, recv_view, target)


def _whole(ref, place):
    return ref


def _split_start(name, srcs, land_shapes, plans):
    skeys, lkeys = list(srcs), list(land_shapes)
    ns, nl, ng = len(skeys), len(lkeys), len(plans)

    def body(*refs):
        src = dict(zip(skeys, refs[:ns]))
        land = dict(zip(lkeys, refs[ns:ns + nl]))
        sems = refs[ns + nl:ns + nl + 2 * ng]
        token = refs[-1]
        place = _place()
        for gi, plan in enumerate(plans):
            for k, cp in enumerate(plan):
                pltpu.make_async_remote_copy(
                    src_ref=cp.src_view(src[cp.src], place), dst_ref=cp.dst_view(land[cp.land], place),
                    send_sem=sems[2 * gi].at[k], recv_sem=sems[2 * gi + 1].at[k],
                    device_id=cp.target(place), device_id_type=MESH).start()
        token[...] = jnp.zeros_like(token)

    sem_shapes = []
    for plan in plans:
        sem_shapes += [pltpu.SemaphoreType.DMA((len(plan),))] * 2
    buffers = [srcs[k] for k in skeys] + [lax.empty(land_shapes[k].shape, land_shapes[k].dtype) for k in lkeys]
    outs = pl.pallas_call(
        body, name=name,
        out_shape=(*sem_shapes, *[pltpu.HBM(a.shape, a.dtype) for a in buffers], jax.ShapeDtypeStruct((8, LANES), F32)),
        in_specs=[HBM] * (ns + nl),
        out_specs=(*[SEM] * (2 * ng), *[HBM] * (ns + nl), pl.BlockSpec(memory_space=pltpu.VMEM)),
        input_output_aliases={i: 2 * ng + i for i in range(ns + nl)},
        compiler_params=pltpu.CompilerParams(has_side_effects=EFFECT),
    )(*[pltpu.with_memory_space_constraint(a, pltpu.HBM) for a in buffers])
    sems = [(outs[2 * gi], outs[2 * gi + 1]) for gi in range(ng)]
    thru = outs[2 * ng:2 * ng + ns + nl]
    return sems, dict(zip(skeys, thru[:ns])), dict(zip(lkeys, thru[ns:])), outs[-1]


def _split_wait(name, sems, srcs, lands, plan, after):
    skeys, lkeys = list(srcs), list(lands)
    ns, nl = len(skeys), len(lkeys)

    def body(*refs):
        src = dict(zip(skeys, refs[:ns]))
        land = dict(zip(lkeys, refs[ns:ns + nl]))
        ssem, rsem = refs[ns + nl], refs[ns + nl + 1]
        place = _place()
        for k, cp in enumerate(plan):
            pltpu.make_async_remote_copy(
                src_ref=cp.src_view(src[cp.src], place), dst_ref=cp.dst_view(land[cp.land], place),
                send_sem=ssem.at[k], recv_sem=rsem.at[k],
                device_id=cp.target(place), device_id_type=MESH).wait_send()
            got = cp.recv_view(land[cp.land], place)
            pltpu.make_async_remote_copy(
                src_ref=got, dst_ref=got, send_sem=ssem.at[k], recv_sem=rsem.at[k],
                device_id=cp.target(place), device_id_type=MESH).wait_recv()

    buffers = [srcs[k] for k in skeys] + [lands[k] for k in lkeys]
    outs = pl.pallas_call(
        body, name=name, out_shape=tuple(pltpu.HBM(a.shape, a.dtype) for a in buffers),
        in_specs=(*[HBM] * (ns + nl), SEM, SEM, ANY), out_specs=tuple([HBM] * (ns + nl)),
        input_output_aliases={i: i for i in range(ns + nl)},
        compiler_params=pltpu.CompilerParams(has_side_effects=EFFECT),
    )(*buffers, sems[0], sems[1], after)
    return dict(zip(skeys, outs[:ns])), dict(zip(lkeys, outs[ns:]))


def _chip_of(place):
    x, y, c, chips = place
    return 2 * x + y


class _WeightGather:
    def __init__(self, layers):
        self.plans, srcs, shapes = [], {}, {}
        for l, blocks in enumerate(layers):
            plan = []
            for name, a in blocks.items():
                key = (l, name)
                srcs[key] = a
                shapes[key] = jax.ShapeDtypeStruct((N_CHIPS,) + a.shape, a.dtype)
                slot = lambda ref, place: ref.at[_chip_of(place)]
                for k in range(3):
                    plan.append(_Copy(
                        key, _whole, key, slot,
                        lambda ref, place, k=k: ref.at[2 * place[3][k][0] + place[3][k][1]],
                        lambda place, k=k: (place[3][k][0], place[3][k][1], place[2])))
                plan.append(_Copy(key, _whole, key, slot, slot, lambda place: (place[0], place[1], 1 - place[2])))
            self.plans.append(plan)
        self.sems, self.srcs, self.lands, self.token = _split_start("gather_start", srcs, shapes, self.plans)

    def layer(self, l, after):
        keys = [k for k in self.srcs if k[0] == l]
        _, lands = _split_wait(f"gather_wait{l}", self.sems[l], {k: self.srcs[k] for k in keys},
                               {k: self.lands[k] for k in keys}, self.plans[l], after)
        return {k[1]: a for k, a in lands.items()}


class _GradReduce:
    def __init__(self, place):
        self.place = place
        self.jobs = []
        self.done = {}
        self.n = 0

    def submit(self, grads):
        views = {k: a.reshape(N_CHIPS, 2, a.shape[1] // 2, a.shape[2]) for k, a in grads.items()}
        shapes = {k: jax.ShapeDtypeStruct((N_CHIPS,) + a.shape[2:], F32) for k, a in views.items()}
        sibling = lambda place: (place[0], place[1], 1 - place[2])
        plan = [_Copy(k, lambda ref, place: ref.at[:, 1 - place[2]], k, _whole, _whole, sibling) for k in views]
        sems, srcs, lands, token = _split_start(f"grad_pair_start{self.n}", views, shapes, [plan])
        self.jobs.append(dict(id=self.n, stage=1, sems=sems[0], srcs=srcs, lands=lands, plan=plan))
        self.n += 1
        return token

    def pump(self, after):
        tokens = []
        for job in list(self.jobs):
            srcs, lands = _split_wait(f"grad_wait{job['id']}_{job['stage']}", job["sems"], job["srcs"], job["lands"],
                                      job["plan"], after)
            if job["stage"] == 1:
                parts = {k: _pair_add(f"grad_pair_add{job['id']}_{i}", [srcs[k]], lands[k][:, None], self.place)
                         for i, k in enumerate(srcs)}
                shapes = {k: jax.ShapeDtypeStruct(a.shape, a.dtype) for k, a in parts.items()}
                plan = []
                for k in parts:
                    for j in range(3):
                        there = lambda ref, place, j=j: ref.at[2 * place[3][j][0] + place[3][j][1]]
                        plan.append(_Copy(k, there, k, lambda ref, place: ref.at[_chip_of(place)], there,
                                          lambda place, j=j: (place[3][j][0], place[3][j][1], place[2])))
                sems, srcs2, lands2, token = _split_start(f"grad_chip_start{job['id']}", parts, shapes, [plan])
                job.update(stage=2, sems=sems[0], srcs=srcs2, lands=lands2, plan=plan)
                tokens.append(token)
            else:
                for i, k in enumerate(srcs):
                    self.done[k] = _chip_add(f"grad_chip_add{job['id']}_{i}", srcs[k], lands[k], self.place)[0]
                self.jobs.remove(job)
        return tokens

    def finish(self, after):
        while self.jobs:
            self.pump(after)
        return self.done


def _pair_share(halves):
    n = len(halves)

    def body(*refs):
        outs = refs[n:2 * n]
        ssem, rsem = refs[2 * n:]
        x, y, c, _ = _place()
        sends = []
        for t in range(n):
            cp = pltpu.make_async_remote_copy(
                src_ref=outs[t].at[c], dst_ref=outs[t].at[c], send_sem=ssem.at[t], recv_sem=rsem.at[t],
                device_id=(x, y, 1 - c), device_id_type=MESH)
            cp.start()
            sends.append(cp)
        for t in range(n):
            theirs = outs[t].at[1 - c]
            pltpu.make_async_remote_copy(
                src_ref=theirs, dst_ref=theirs, send_sem=ssem.at[t], recv_sem=rsem.at[t],
                device_id=(x, y, 1 - c), device_id_type=MESH).wait_recv()
        for cp in sends:
            cp.wait_send()

    return pl.pallas_call(
        body, name="grad_pair_share", in_specs=[ANY] * n, out_specs=[ANY] * n,
        out_shape=[jax.ShapeDtypeStruct(a.shape, a.dtype) for a in halves],
        input_output_aliases={t: t for t in range(n)},
        scratch_shapes=[pltpu.SemaphoreType.DMA((n,)), pltpu.SemaphoreType.DMA((n,))])(*halves)


def _small_allreduce(part):
    R, C = part.shape
    N_DEV = 8

    def body(in_ref, out_ref, slots, ssem, rsem):
        x, y, c, _ = _place()
        me = 4 * x + 2 * y + c
        sends = []
        for k in range(1, N_DEV):
            kx, ky, kc = (k >> 2) & 1, (k >> 1) & 1, k & 1
            peer = (1 - x if kx else x, 1 - y if ky else y, 1 - c if kc else c)
            cp = pltpu.make_async_remote_copy(
                src_ref=in_ref, dst_ref=slots.at[me], send_sem=ssem.at[k], recv_sem=rsem.at[k],
                device_id=peer, device_id_type=MESH)
            cp.start()
            sends.append(cp)
        slots[me] = in_ref[...]
        for k in range(1, N_DEV):
            kx, ky, kc = (k >> 2) & 1, (k >> 1) & 1, k & 1
            peer = (1 - x if kx else x, 1 - y if ky else y, 1 - c if kc else c)
            slot = slots.at[4 * peer[0] + 2 * peer[1] + peer[2]]
            pltpu.make_async_remote_copy(
                src_ref=slot, dst_ref=slot, send_sem=ssem.at[k], recv_sem=rsem.at[k],
                device_id=peer, device_id_type=MESH).wait_recv()
        acc = slots[0]
        for d in range(1, N_DEV):
            acc = acc + slots[d]
        out_ref[...] = acc
        for cp in sends:
            cp.wait_send()

    vm = pl.BlockSpec(memory_space=pltpu.VMEM)
    return pl.pallas_call(
        body, name="small_allreduce", in_specs=[vm], out_specs=vm,
        out_shape=jax.ShapeDtypeStruct((R, C), F32),
        scratch_shapes=[pltpu.VMEM((N_DEV, R, C), F32), pltpu.SemaphoreType.DMA((N_DEV,)),
                        pltpu.SemaphoreType.DMA((N_DEV,))])(part)


def _local_step(x, target, norm_mix, norm_mlp, norm_kv, norm_final, weights, sink, n_a, n_heads):
    B, S, D = x.shape
    T = B * S
    C = n_heads * HEAD_DIM
    depth = norm_mix.shape[0]
    slopes = 2.0 ** (-ALIBI_MAX_BIAS * jnp.arange(1, n_heads + 1, dtype=F32) / n_heads)
    tm = min(512, T)
    row = lambda v: v.reshape(1, -1)

    h = x.reshape(T, D)
    saved, Wl = [], []
    kv = nkv = h_kv = cwg = None
    for l in range(depth):
        s = {"h_in": h}
        w = {k: a[:, None] for k, a in weights.layer(l, h).items()}
        Wl.append(w)
        if l == 0:
            cwg = w["conv"][:, 0, :n_a * 3].reshape(N_CHIPS, n_a, 3, -1)
        if l < n_a:
            s["n1"], bcu = _norm_mm(f"a_in_fwd{l}", h, row(norm_mix[l]), w["w_a_in"], 0, 3, BF16, tm)
            s["bcu"] = bcu.reshape(3, B, S, D)
            s["z"] = _conv_fwd(f"conv_fwd{l}", s["bcu"], cwg, l, LANES).reshape(T, D)
            h = _mm_res_rows(f"a_out_fwd{l}", s["z"], w["w_a_out"], 0, h, _to_bf16, tm)
        else:
            i = l - n_a
            if i == 0:
                h_kv = h
                nkv, kv = _norm_mm("kv_fwd", h, row(norm_kv), w["w_kv"], 0, 1, F32, tm)
                kv = kv.reshape(B, S, 2 * 3 * C)
            s["n1"], q = _norm_mm(f"q_fwd{i}", h, row(norm_mix[l]), w["w_q"], 0, 1, F32, tm)
            s["q"] = q.reshape(B, S, 3 * C)
            o, lse = _attn_fwd(f"attn_fwd{i}", s["q"], kv, slopes, n_heads)
            s["o"], s["lse"] = o.reshape(T, C), lse.reshape(T, C)
            h = _mm_res_cols(f"o_fwd{i}", s["o"], w["w_o"], 0, h, tm)
        s["h_mid"] = h
        s["n2"], a = _norm_mm(f"up_fwd{l}", h, row(norm_mlp[l]), w["w_up"], 0, 1, BF16, tm)
        F = a.shape[2]
        s["a"] = a.reshape(T, F)
        h = _mm_res_rows(f"down_fwd{l}", s["a"], w["w_down"], 0, h, _relu2_bf16, tm)
        saved.append(s)

    loss, dh, dg_final = _final_loss("loss_head", h, row(norm_final), target.reshape(T, D), tm)

    g_mix, g_mlp = [None] * depth, [None] * depth
    g_conv = [None] * n_a
    dkv = None
    tt = min(512, T)
    deps = []
    for l in reversed(range(depth)):
        s, w = saved[l], Wl[l]
        da = _nt_rows(f"down_bwd{l}", dh, w["w_down"], 0, s["a"], BF16, tm, deps)
        g_down = _tn(f"down_wgrad{l}", s["a"], _relu2_bf16, [_seg2d(dh, tt, 2)], None, False,
                     min(1024, F), tt).reshape(N_CHIPS, F // N_CHIPS, D)
        g_up = _tn(f"up_wgrad{l}", s["n2"], _to_bf16, [_seg2d(da, tt, 2)], F // N_CHIPS, True, min(512, D), tt)
        dh, g_mlp[l] = _nt_cols(f"up_bwd{l}", [_seg2d(da, tm, 1)], w["w_up"], 0, tm,
                                (s["h_mid"], row(norm_mlp[l]), dh))
        deps = sink.pump(dh) + [sink.submit({("w_up", l): g_up, ("w_down", l): g_down})]
        if l < n_a:
            g_out = _tn(f"a_out_wgrad{l}", s["z"], _to_bf16, [_seg2d(dh, tt, 2)], None, False,
                        D, tt, deps).reshape(N_CHIPS, D // N_CHIPS, D)
            dz = _nt_rows(f"a_out_bwd{l}", dh, w["w_a_out"], 0, None, F32, tm)
            dbcu, g_conv[l] = _conv_bwd(f"conv_bwd{l}", s["bcu"], dz.reshape(B, S, D), cwg, l, LANES)
            dbcu = dbcu.reshape(3, T, D)
            g_in = _tn(f"a_in_wgrad{l}", s["n1"], _to_bf16, [_seg_plane(dbcu, p, tt, 2) for p in range(3)],
                       3 * D // N_CHIPS, True, min(512, D), tt)
            dh, g_mix[l] = _nt_cols(f"a_in_bwd{l}", [_seg_plane(dbcu, p, tm, 1) for p in range(3)],
                                    w["w_a_in"], 0, tm, (s["h_in"], row(norm_mix[l]), dh))
            mixer = {("w_a_in", l): g_in, ("w_a_out", l): g_out}
        else:
            i = l - n_a
            g_o = _tn(f"o_wgrad{i}", s["o"], _to_bf16, [_seg2d(dh, tt, 2)], D // N_CHIPS, True, C, tt, deps)
            do = _nt_cols(f"o_bwd{i}", [_seg2d(dh, tm, 1)], w["w_o"], 0, tm, None)
            dq, dk, dv = _attn_bwd(f"attn_bwd{i}", s["q"], kv, slopes, s["o"].reshape(B, S, C),
                                   s["lse"].reshape(B, S, C), do.reshape(B, S, C), n_heads, dkv)
            dkv = (dk, dv)
            dq = dq.reshape(T, 3 * C)
            g_q = _tn(f"q_wgrad{i}", s["n1"], _to_bf16, [_seg2d(dq, tt, 2)], 3 * C // N_CHIPS, True, min(512, D), tt)
            dh, g_mix[l] = _nt_cols(f"q_bwd{i}", [_seg2d(dq, tm, 1)], w["w_q"], 0, tm,
                                    (s["h_in"], row(norm_mix[l]), dh))
            mixer = {("w_q", i): g_q, ("w_o", i): g_o}
            if i == 0:
                dk2, dv2 = (t.reshape(T, 3 * C) for t in dkv)
                mixer[("w_kv", 0)] = _tn("kv_wgrad", nkv, _to_bf16, _kv_segments(dk2, dv2, C, tt, 2),
                                         6 * C // N_CHIPS, True, min(512, D), tt)
                dh, g_kv = _nt_cols("kv_bwd", _kv_segments(dk2, dv2, C, tm, 1), w["w_kv"], 0, tm,
                                    (h_kv, row(norm_kv), dh))
        deps = sink.pump(dh) + [sink.submit(mixer)]
    small = dict(norm_mix=jnp.concatenate(g_mix, axis=0), norm_mlp=jnp.concatenate(g_mlp, axis=0),
                 norm_kv=g_kv, norm_final=dg_final, conv_w=jnp.stack(g_conv))
    return loss, dh.reshape(B, S, D), small


BIG = ("w_a_in", "w_a_out", "w_kv", "w_q", "w_o", "w_up", "w_down")
CONV_PAD_ROWS = 16


def kernel(x, norm_mix, norm_mlp, w_a_in, conv_w, w_a_out, norm_kv, w_kv, w_q, w_o, w_up, w_down, norm_final, loss_target, m_norm_mix, m_norm_mlp, m_w_a_in, m_conv_w, m_w_a_out, m_norm_kv, m_w_kv, m_w_q, m_w_o, m_w_up, m_w_down, m_norm_final, v_norm_mix, v_norm_mlp, v_w_a_in, v_conv_w, v_w_a_out, v_norm_kv, v_w_kv, v_w_q, v_w_o, v_w_up, v_w_down, v_norm_final):
    D = x.shape[-1]
    w = dict(norm_mix=norm_mix, norm_mlp=norm_mlp, w_a_in=w_a_in, conv_w=conv_w, w_a_out=w_a_out, norm_kv=norm_kv,
             w_kv=w_kv[None], w_q=w_q, w_o=w_o, w_up=w_up, w_down=w_down, norm_final=norm_final)
    m = dict(norm_mix=m_norm_mix, norm_mlp=m_norm_mlp, w_a_in=m_w_a_in, conv_w=m_conv_w, w_a_out=m_w_a_out,
             norm_kv=m_norm_kv, w_kv=m_w_kv[None], w_q=m_w_q, w_o=m_w_o, w_up=m_w_up, w_down=m_w_down,
             norm_final=m_norm_final)
    v = dict(norm_mix=v_norm_mix, norm_mlp=v_norm_mlp, w_a_in=v_w_a_in, conv_w=v_conv_w, w_a_out=v_w_a_out,
             norm_kv=v_norm_kv, w_kv=v_w_kv[None], w_q=v_w_q, w_o=v_w_o, w_up=v_w_up, w_down=v_w_down,
             norm_final=v_norm_final)
    depth = norm_mix.shape[0]
    n_a, taps, cwc = conv_w.shape
    n_heads = w_o.shape[1] // HEAD_DIM

    conv_rows = jnp.zeros((CONV_PAD_ROWS, cwc), F32).at[:n_a * taps].set(conv_w.reshape(n_a * taps, cwc))
    layers = []
    for l in range(depth):
        names = [("w_a_in", l), ("w_a_out", l)] if l < n_a else [("w_q", l - n_a), ("w_o", l - n_a)]
        if l == n_a:
            names.append(("w_kv", 0))
        names += [("w_up", l), ("w_down", l)]
        blocks = {k: w[k][i].astype(BF16) for k, i in names}
        if l == 0:
            blocks["conv"] = conv_rows
        layers.append(blocks)
    weights = _WeightGather(layers)
    place = jnp.stack([2 * lax.axis_index("x") + lax.axis_index("y"), lax.axis_index("c")]).astype(jnp.int32)
    sink = _GradReduce(place)

    loss, grad_x, small = _local_step(x, loss_target, norm_mix, norm_mlp, norm_kv, norm_final, weights, sink,
                                      n_a, n_heads)
    loss = lax.psum(loss[0, 0], ("x", "y", "c"))

    done = sink.finish(grad_x)
    keys = list(done)
    shared = dict(zip(keys, _pair_share([done[k] for k in keys])))
    grads = {}

    packed = jnp.concatenate([small["norm_mix"], small["norm_mlp"], small["norm_kv"], small["norm_final"],
                              small["conv_w"].reshape(n_a * taps, D)], axis=0)
    pad = (-packed.shape[0]) % 8
    packed = jnp.pad(packed, ((0, pad), (0, 0)))
    total = _small_allreduce(packed)
    grads["norm_mix"] = total[:depth]
    grads["norm_mlp"] = total[depth:2 * depth]
    grads["norm_kv"] = total[2 * depth]
    grads["norm_final"] = total[2 * depth + 1]
    chip = 2 * lax.axis_index("x") + lax.axis_index("y")
    conv_full = total[2 * depth + 2:2 * depth + 2 + n_a * taps].reshape(n_a, taps, N_CHIPS, cwc)
    grads["conv_w"] = lax.dynamic_index_in_dim(conv_full, chip, axis=2, keepdims=False)

    order = ("norm_mix", "norm_mlp", "w_a_in", "conv_w", "w_a_out", "norm_kv", "w_kv", "w_q", "w_o", "w_up",
             "w_down", "norm_final")
    delta, new_m, new_v = {}, {}, {}
    vec_names = ("norm_mix", "norm_mlp", "norm_kv", "norm_final")
    rows_of = lambda a: a.reshape(-1, D)
    vw, vg, vm_, vv = (jnp.concatenate([rows_of(t[k]) for k in vec_names], axis=0) for t in (w, grads, m, v))
    vpad = (-vw.shape[0]) % 8
    padrows = lambda a: jnp.pad(a, ((0, vpad), (0, 0)))
    vd, vnm, vnv = _adamw("adamw_norms", padrows(vw), padrows(vg), padrows(vm_), padrows(vv))
    off = 0
    for k in vec_names:
        r = rows_of(w[k]).shape[0]
        delta[k] = vd[off:off + r].reshape(w[k].shape)
        new_m[k] = vnm[off:off + r].reshape(w[k].shape)
        new_v[k] = vnv[off:off + r].reshape(w[k].shape)
        off += r
    cpad = (-n_a * taps) % 8
    two_d = lambda a: jnp.pad(a.reshape(-1, cwc), ((0, cpad), (0, 0)))
    cd, cnm, cnv = _adamw("adamw_conv_w", two_d(w["conv_w"]), two_d(grads["conv_w"]), two_d(m["conv_w"]),
                          two_d(v["conv_w"]))
    delta["conv_w"], new_m["conv_w"], new_v["conv_w"] = (t[:n_a * taps].reshape(conv_w.shape) for t in (cd, cnm, cnv))
    for k in BIG:
        per_layer = [shared[(k, l)].reshape(w[k].shape[1:]) for l in range(w[k].shape[0])]
        grads[k], delta[k], new_m[k], new_v[k] = _adamw_layers(f"adamw_{k}", w[k], per_layer, m[k], v[k])
    fix = lambda k, a: a[0] if k == "w_kv" else a
    return (loss, grad_x, *[fix(k, grads[k]) for k in order], *[fix(k, delta[k]) for k in order],
            *[fix(k, new_m[k]) for k in order], *[fix(k, new_v[k]) for k in order])
```

```python
import functools

import jax
import jax.numpy as jnp
from jax import lax
from jax.experimental import pallas as pl
from jax.experimental.pallas import tpu as pltpu

F32 = jnp.float32
BF16 = jnp.bfloat16
MESH = pl.DeviceIdType.MESH

EPS = 1e-5
PATTERNS = ((128, 1), (512, 4), (2048, 16))
HEAD_DIM = 64
ALIBI_MAX_BIAS = 8.0
NEG_INF = -1e30
ATT_BLK = 128
BWD_UNROLL = 4
N_CHIPS = 4
LANES = 128
VMEM_LIMIT = 56 * 1024 * 1024

ADAM_LR = 0.001
ADAM_B1 = 0.9
ADAM_B2 = 0.999
ADAM_EPS = 1e-08
ADAM_WD = 0.01
ADAM_STEP = 10


ANY = pl.BlockSpec(memory_space=pl.ANY)


def _params(n_grid_axes):
    return pltpu.CompilerParams(dimension_semantics=("arbitrary",) * n_grid_axes, vmem_limit_bytes=VMEM_LIMIT)


def _dot(a, b):
    return jnp.dot(a, b, preferred_element_type=F32)


def _dot_nt(a, b):
    return lax.dot_general(a, b, (((1,), (1,)), ((), ())), preferred_element_type=F32)


def _dot_tn(a, b):
    return lax.dot_general(a, b, (((0,), (0,)), ((), ())), preferred_element_type=F32)


def _relu2(a):
    return jnp.square(jnp.maximum(a, 0.0))


def _rms(hf, g):
    y = hf * lax.rsqrt(jnp.mean(hf * hf, axis=-1, keepdims=True) + EPS)
    return y * g


def _rms_bwd(hf, g, dn):
    rstd = lax.rsqrt(jnp.mean(hf * hf, axis=-1, keepdims=True) + EPS)
    xhat = hf * rstd
    dg = jnp.sum(dn * xhat, axis=0, keepdims=True)
    dx = dn * g
    dh = rstd * (dx - xhat * jnp.mean(dx * xhat, axis=-1, keepdims=True))
    return dh, dg


def _pieces(seg_widths, chunk_width, max_width):
    total = sum(seg_widths)
    cuts = {0, total}
    acc = 0
    for w in seg_widths:
        cuts.add(acc)
        acc += w
    cuts.update(range(0, total, chunk_width))
    cuts = sorted(cuts)
    fine = []
    for lo, hi in zip(cuts[:-1], cuts[1:]):
        while hi - lo > max_width:
            fine.append((lo, lo + max_width))
            lo += max_width
        fine.append((lo, hi))
    out = []
    for lo, hi in fine:
        acc = 0
        for s, w in enumerate(seg_widths):
            if lo < acc + w:
                break
            acc += w
        out.append((s, lo - acc, lo // chunk_width, lo % chunk_width, hi - lo))
    return out


def _relu2_bf16(a):
    return _relu2(a.astype(F32)).astype(BF16)


def _to_bf16(a):
    return a.astype(BF16)


def _norm_mm(name, h, g, wg, layer, planes, out_dtype, tm):
    T, D = h.shape
    cw = wg.shape[3]
    N = N_CHIPS * cw
    pw = N // planes
    pieces = _pieces([pw] * planes, cw, 512)

    def body(h_ref, g_ref, w_ref, n_ref, o_ref):
        n = _rms(h_ref[...], g_ref[...]).astype(BF16)
        n_ref[...] = n
        for s, a0, ch, b0, wd in pieces:
            o_ref[s, :, a0:a0 + wd] = _dot(n, w_ref[ch, :, b0:b0 + wd]).astype(out_dtype)

    return pl.pallas_call(
        body, name=name, grid=(T // tm,),
        in_specs=[pl.BlockSpec((tm, D), lambda i: (i, 0)),
                  pl.BlockSpec((1, D), lambda i: (0, 0)),
                  pl.BlockSpec((N_CHIPS, None, D, cw), lambda i: (0, layer, 0, 0))],
        out_specs=[pl.BlockSpec((tm, D), lambda i: (i, 0)),
                   pl.BlockSpec((planes, tm, pw), lambda i: (0, i, 0))],
        out_shape=[jax.ShapeDtypeStruct((T, D), BF16), jax.ShapeDtypeStruct((planes, T, pw), out_dtype)],
        compiler_params=_params(1))(h, g, wg)


def _mm_res_rows(name, a, wg, layer, h, act, tm):
    T = a.shape[0]
    rk, D = wg.shape[2], wg.shape[3]

    def body(a_ref, w_ref, h_ref, o_ref):
        acc = h_ref[...]
        for k in range(N_CHIPS):
            acc = acc + _dot(act(a_ref[:, k * rk:(k + 1) * rk]), w_ref[k])
        o_ref[...] = acc

    return pl.pallas_call(
        body, name=name, grid=(T // tm,),
        in_specs=[pl.BlockSpec((tm, N_CHIPS * rk), lambda i: (i, 0)),
                  pl.BlockSpec((N_CHIPS, None, rk, D), lambda i: (0, layer, 0, 0)),
                  pl.BlockSpec((tm, D), lambda i: (i, 0))],
        out_specs=pl.BlockSpec((tm, D), lambda i: (i, 0)),
        out_shape=jax.ShapeDtypeStruct((T, D), F32),
        compiler_params=_params(1))(a, wg, h)


def _mm_res_cols(name, a, wg, layer, h, tm):
    T, K = a.shape
    cw = wg.shape[3]
    D = N_CHIPS * cw

    def body(a_ref, w_ref, h_ref, o_ref):
        a16 = a_ref[...].astype(BF16)
        for j in range(N_CHIPS):
            o_ref[:, j * cw:(j + 1) * cw] = h_ref[:, j * cw:(j + 1) * cw] + _dot(a16, w_ref[j])

    return pl.pallas_call(
        body, name=name, grid=(T // tm,),
        in_specs=[pl.BlockSpec((tm, K), lambda i: (i, 0)),
                  pl.BlockSpec((N_CHIPS, None, K, cw), lambda i: (0, layer, 0, 0)),
                  pl.BlockSpec((tm, D), lambda i: (i, 0))],
        out_specs=pl.BlockSpec((tm, D), lambda i: (i, 0)),
        out_shape=jax.ShapeDtypeStruct((T, D), F32),
        compiler_params=_params(1))(a, wg, h)


CONV_ROWS = 256
CONV_HALO = 16


def _conv_shifted(ext, k, r0, rows):
    rolled = pltpu.roll(ext, k, 0)[CONV_HALO:]
    t = r0 + lax.broadcasted_iota(jnp.int32, rolled.shape, 0)
    return jnp.where(t >= k, rolled, 0.0)


def _conv_ahead(ext, k, r0, rows, S):
    rolled = pltpu.roll(ext, rows + CONV_HALO - k, 0)[:rows]
    t = r0 + lax.broadcasted_iota(jnp.int32, rolled.shape, 0)
    return jnp.where(t + k < S, rolled, 0.0)


def _conv_fwd(name, bcu, cwg, layer, tc):
    _, B, S, D = bcu.shape
    cwc = cwg.shape[3]
    per_chunk = cwc // tc
    R = min(CONV_ROWS, S)

    def body(x_ref, w_ref, z_ref):
        w = [w_ref[k:k + 1, :] for k in range(3)]

        def step(i, carry):
            r0 = pl.multiple_of(i * R, R)
            h0 = pl.multiple_of(jnp.maximum(r0 - CONV_HALO, 0), CONV_HALO)
            ld = lambda p, start, rows: x_ref[p, pl.ds(start, rows), :].astype(F32)
            cu = jnp.concatenate([ld(1, h0, CONV_HALO) * ld(2, h0, CONV_HALO), ld(1, r0, R) * ld(2, r0, R)], axis=0)
            conv = w[0] * cu[CONV_HALO:]
            conv = conv + w[1] * _conv_shifted(cu, 1, r0, R)
            conv = conv + w[2] * _conv_shifted(cu, 2, r0, R)
            z_ref[pl.ds(r0, R), :] = (ld(0, r0, R) * conv).astype(BF16)
            return carry

        lax.fori_loop(0, S // R, step, 0)

    return pl.pallas_call(
        body, name=name, grid=(B, D // tc),
        in_specs=[pl.BlockSpec((3, None, S, tc), lambda b, j: (0, b, 0, j)),
                  pl.BlockSpec((None, None, 3, tc), lambda b, j: (j // per_chunk, layer, 0, j % per_chunk))],
        out_specs=pl.BlockSpec((None, S, tc), lambda b, j: (b, 0, j)),
        out_shape=jax.ShapeDtypeStruct((B, S, D), BF16),
        compiler_params=_params(2))(bcu, cwg)


def _conv_bwd(name, bcu, dz, cwg, layer, tc):
    _, B, S, D = bcu.shape
    cwc = cwg.shape[3]
    per_chunk = cwc // tc
    R = min(CONV_ROWS, S)

    def body(x_ref, dz_ref, w_ref, d_ref, dw_ref):
        w = [w_ref[k:k + 1, :] for k in range(3)]

        @pl.when(pl.program_id(1) == 0)
        def _():
            dw_ref[...] = jnp.zeros_like(dw_ref)

        def step(i, carry):
            r0 = pl.multiple_of(i * R, R)
            h0 = pl.multiple_of(jnp.maximum(r0 - CONV_HALO, 0), CONV_HALO)
            a0 = pl.multiple_of(jnp.minimum(r0 + R, S - CONV_HALO), CONV_HALO)
            ld = lambda p, start, rows: x_ref[p, pl.ds(start, rows), :].astype(F32)
            b, c, u = ld(0, r0, R), ld(1, r0, R), ld(2, r0, R)
            dz = dz_ref[pl.ds(r0, R), :]
            cu = jnp.concatenate([ld(1, h0, CONV_HALO) * ld(2, h0, CONV_HALO), c * u], axis=0)
            cu1 = _conv_shifted(cu, 1, r0, R)
            cu2 = _conv_shifted(cu, 2, r0, R)
            conv = w[0] * (c * u) + w[1] * cu1 + w[2] * cu2
            dconv = dz * b
            dca = jnp.concatenate([dconv, dz_ref[pl.ds(a0, CONV_HALO), :] * ld(0, a0, CONV_HALO)], axis=0)
            dcu = w[0] * dconv + w[1] * _conv_ahead(dca, 1, r0, R, S) + w[2] * _conv_ahead(dca, 2, r0, R, S)
            d_ref[0, pl.ds(r0, R), :] = (dz * conv).astype(BF16)
            d_ref[1, pl.ds(r0, R), :] = (dcu * u).astype(BF16)
            d_ref[2, pl.ds(r0, R), :] = (dcu * c).astype(BF16)
            return (carry[0] + jnp.sum(dconv * (c * u), axis=0, keepdims=True),
                    carry[1] + jnp.sum(dconv * cu1, axis=0, keepdims=True),
                    carry[2] + jnp.sum(dconv * cu2, axis=0, keepdims=True))

        zero = jnp.zeros((1, tc), F32)
        s0, s1, s2 = lax.fori_loop(0, S // R, step, (zero, zero, zero))
        for k, sk in enumerate((s0, s1, s2)):
            dw_ref[k:k + 1, :] += sk

    return pl.pallas_call(
        body, name=name, grid=(D // tc, B),
        in_specs=[pl.BlockSpec((3, None, S, tc), lambda j, b: (0, b, 0, j)),
                  pl.BlockSpec((None, S, tc), lambda j, b: (b, 0, j)),
                  pl.BlockSpec((None, None, 3, tc), lambda j, b: (j // per_chunk, layer, 0, j % per_chunk))],
        out_specs=[pl.BlockSpec((3, None, S, tc), lambda j, b: (0, b, 0, j)),
                   pl.BlockSpec((3, tc), lambda j, b: (0, j))],
        out_shape=[jax.ShapeDtypeStruct((3, B, S, D), BF16), jax.ShapeDtypeStruct((3, D), F32)],
        compiler_params=_params(2))(bcu, dz, cwg)


def _att_rows(dil, idx, nb):
    r, n = idx // nb, idx % nb
    if dil == 1:
        cur = pl.ds(pl.multiple_of(n * ATT_BLK, ATT_BLK), ATT_BLK)
        prev = pl.ds(pl.multiple_of(jnp.maximum(n - 1, 0) * ATT_BLK, ATT_BLK), ATT_BLK)
    else:
        cur = pl.ds(n * (ATT_BLK * dil) + r, ATT_BLK, stride=dil)
        prev = pl.ds(jnp.maximum(n - 1, 0) * (ATT_BLK * dil) + r, ATT_BLK, stride=dil)
    return n, cur, prev


def _att_bias(bias_ref, dil, sl_ref, hp):
    row = lax.broadcasted_iota(jnp.int32, (2 * ATT_BLK, 2 * ATT_BLK), 0)
    ci = lax.broadcasted_iota(jnp.int32, (2 * ATT_BLK, 2 * ATT_BLK), 1)
    j = ATT_BLK + (row & (ATT_BLK - 1)) - ci
    slope = jnp.where(row < ATT_BLK, sl_ref[2 * hp], sl_ref[2 * hp + 1])
    rest = jnp.where((j >= 0) & (j <= ATT_BLK), -slope * (dil * j).astype(F32), NEG_INF)
    bias_ref[1] = rest
    bias_ref[0] = jnp.where(ci >= ATT_BLK, rest, NEG_INF)


def _stack_heads(x16, lane):
    first = lane < HEAD_DIM
    return jnp.concatenate([jnp.where(first, x16, jnp.zeros_like(x16)),
                            jnp.where(first, jnp.zeros_like(x16), x16)], axis=0)


def _per_head(col, lane):
    return jnp.where(lane < HEAD_DIM, col[:ATT_BLK], col[ATT_BLK:])


def _attn_fwd(name, q, kv, slopes, n_heads):
    B, S, CQ = q.shape
    HP = n_heads * HEAD_DIM // LANES
    scale = HEAD_DIM ** -0.5
    n_groups = len(PATTERNS)
    CH = 256

    def body(sl_ref, q_ref, k_ref, v_ref, o_ref, lse_ref, bias_ref, *parts):
        og, lg = parts[:n_groups], parts[n_groups:]
        hp, g = pl.program_id(1), pl.program_id(2)
        lane = lax.broadcasted_iota(jnp.int32, (1, LANES), 1)

        for gi, (window, dil) in enumerate(PATTERNS):
            nb = S // dil // ATT_BLK

            @pl.when(g == gi)
            def _(gi=gi, dil=dil, nb=nb):
                _att_bias(bias_ref, dil, sl_ref, hp)

                def step(idx, carry):
                    n, cur, prev = _att_rows(dil, idx, nb)
                    qs = _stack_heads((q_ref[cur, :] * scale).astype(BF16), lane)
                    kc = jnp.concatenate([k_ref[prev, :], k_ref[cur, :]], axis=0).astype(BF16)
                    vc = jnp.concatenate([v_ref[prev, :], v_ref[cur, :]], axis=0).astype(BF16)
                    s = _dot_nt(qs, kc) + bias_ref[jnp.minimum(n, 1)]
                    m = jnp.max(s, axis=-1, keepdims=True)
                    p = jnp.exp(s - m)
                    l = jnp.sum(p, axis=-1, keepdims=True)
                    p16 = p.astype(BF16)
                    o_un = _dot(jnp.concatenate([p16[:ATT_BLK], p16[ATT_BLK:]], axis=1), _stack_heads_rows(vc, lane))
                    og[gi][cur, :] = o_un / _per_head(l, lane)
                    lg[gi][cur, :] = _per_head(m + jnp.log(l), lane)
                    return carry

                lax.fori_loop(0, S // ATT_BLK, step, 0, unroll=8)

        @pl.when(g == n_groups - 1)
        def _():
            def comb(i, carry):
                rows = pl.ds(pl.multiple_of(i * CH, CH), CH)
                a, b, c = lg[0][rows, :], lg[1][rows, :], lg[2][rows, :]
                m = jnp.maximum(jnp.maximum(a, b), c)
                ea, eb, ec = jnp.exp(a - m), jnp.exp(b - m), jnp.exp(c - m)
                z = ea + eb + ec
                o_ref[rows, :] = (ea / z) * og[0][rows, :] + (eb / z) * og[1][rows, :] + (ec / z) * og[2][rows, :]
                lse_ref[rows, :] = m + jnp.log(z)
                return carry

            lax.fori_loop(0, S // CH, comb, 0)

    blk = (None, S, LANES)
    out = pl.BlockSpec(blk, lambda b, hp, g: (b, 0, hp))
    return pl.pallas_call(
        body, name=name, grid=(B, HP, n_groups),
        in_specs=[pl.BlockSpec(memory_space=pltpu.SMEM),
                  pl.BlockSpec(blk, lambda b, hp, g: (b, 0, g * HP + hp)),
                  pl.BlockSpec(blk, lambda b, hp, g: (b, 0, g * 2 * HP + hp)),
                  pl.BlockSpec(blk, lambda b, hp, g: (b, 0, g * 2 * HP + HP + hp))],
        out_specs=[out, out],
        out_shape=[jax.ShapeDtypeStruct((B, S, HP * LANES), F32)] * 2,
        scratch_shapes=[pltpu.VMEM((2, 2 * ATT_BLK, 2 * ATT_BLK), F32)] + [pltpu.VMEM((S, LANES), F32)] * (2 * n_groups),
        compiler_params=_params(3))(slopes, q, kv, kv)


def _stack_heads_rows(x16, lane):
    first = lane < HEAD_DIM
    return jnp.concatenate([jnp.where(first, x16, jnp.zeros_like(x16)),
                            jnp.where(first, jnp.zeros_like(x16), x16)], axis=0)


def _attn_bwd(name, q, kv, slopes, o, lse, do, n_heads, dkv_prev):
    B, S, CQ = q.shape
    HP = n_heads * HEAD_DIM // LANES
    scale = HEAD_DIM ** -0.5
    n_groups = len(PATTERNS)
    n_prev = 0 if dkv_prev is None else 2

    def body(sl_ref, q_ref, k_ref, v_ref, o_ref, lse_ref, do_ref, *rest):
        dq_ref, dk_ref, dv_ref, bias_ref = rest[n_prev:]
        hp, g = pl.program_id(1), pl.program_id(2)
        lane = lax.broadcasted_iota(jnp.int32, (1, LANES), 1)
        first = lane < HEAD_DIM

        def flush(rows, dk, dv):
            if n_prev:
                dk = dk + rest[0][rows, :]
                dv = dv + rest[1][rows, :]
            dk_ref[rows, :] = dk
            dv_ref[rows, :] = dv

        for gi, (window, dil) in enumerate(PATTERNS):
            nb = S // dil // ATT_BLK
            n_blocks = S // ATT_BLK

            @pl.when(g == gi)
            def _(dil=dil, nb=nb, n_blocks=n_blocks):
                _att_bias(bias_ref, dil, sl_ref, hp)

                def block(idx, carry, first_of_all):
                    n, cur, prev = _att_rows(dil, idx, nb)
                    qs = _stack_heads((q_ref[cur, :] * scale).astype(BF16), lane)
                    kc = jnp.concatenate([k_ref[prev, :], k_ref[cur, :]], axis=0).astype(BF16)
                    vc = jnp.concatenate([v_ref[prev, :], v_ref[cur, :]], axis=0).astype(BF16)
                    dob = do_ref[cur, :]
                    prod = dob * o_ref[cur, :]
                    lseb = lse_ref[cur, :]
                    dos = _stack_heads(dob.astype(BF16), lane)
                    delta = jnp.concatenate(
                        [jnp.sum(jnp.where(first, prod, 0.0), axis=-1, keepdims=True),
                         jnp.sum(jnp.where(first, 0.0, prod), axis=-1, keepdims=True)], axis=0)
                    lse_col = jnp.concatenate(
                        [jnp.max(jnp.where(first, lseb, -jnp.inf), axis=-1, keepdims=True),
                         jnp.max(jnp.where(first, -jnp.inf, lseb), axis=-1, keepdims=True)], axis=0)
                    s = _dot_nt(qs, kc) + bias_ref[jnp.minimum(n, 1)]
                    p = jnp.exp(s - lse_col)
                    ds = p * (_dot_nt(dos, vc) - delta)
                    ds16 = ds.astype(BF16)
                    dq = _dot(jnp.concatenate([ds16[:ATT_BLK], ds16[ATT_BLK:]], axis=1), _stack_heads_rows(kc, lane))
                    dq_ref[cur, :] = dq * scale
                    dk = _dot_tn(ds16, qs)
                    dv = _dot_tn(p.astype(BF16), dos)

                    def flush_before():
                        _, before, _ = _att_rows(dil, idx - 1, nb)
                        flush(before, carry[0] + dk[:ATT_BLK], carry[1] + dv[:ATT_BLK])

                    if first_of_all:
                        pl.when(idx > 0)(flush_before)
                    else:
                        flush_before()
                    return dk[ATT_BLK:], dv[ATT_BLK:]

                def step(i, carry):
                    for u in range(BWD_UNROLL):
                        carry = block(i * BWD_UNROLL + u, carry, u == 0)
                    return carry

                zero = jnp.zeros((ATT_BLK, LANES), F32)
                dk_last, dv_last = lax.fori_loop(0, n_blocks // BWD_UNROLL, step, (zero, zero))
                _, last, _ = _att_rows(dil, n_blocks - 1, nb)
                flush(last, dk_last, dv_last)

    blk = (None, S, LANES)
    shared = pl.BlockSpec(blk, lambda b, hp, g: (b, 0, hp))
    grouped = pl.BlockSpec(blk, lambda b, hp, g: (b, 0, g * HP + hp))
    prev = [] if dkv_prev is None else list(dkv_prev)
    gshape = jax.ShapeDtypeStruct((B, S, n_groups * HP * LANES), F32)
    return pl.pallas_call(
        body, name=name, grid=(B, HP, n_groups),
        in_specs=[pl.BlockSpec(memory_space=pltpu.SMEM), grouped,
                  pl.BlockSpec(blk, lambda b, hp, g: (b, 0, g * 2 * HP + hp)),
                  pl.BlockSpec(blk, lambda b, hp, g: (b, 0, g * 2 * HP + HP + hp)),
                  shared, shared, shared] + [grouped] * n_prev,
        out_specs=[grouped] * 3, out_shape=[gshape] * 3,
        scratch_shapes=[pltpu.VMEM((2, 2 * ATT_BLK, 2 * ATT_BLK), F32)],
        compiler_params=_params(3))(slopes, q, kv, kv, o, lse, do, *prev)


def _final_loss(name, h, g, target, tm):
    T, D = h.shape

    def body(h_ref, g_ref, t_ref, loss_ref, dh_ref, dg_ref):
        hf = h_ref[...]
        gv = g_ref[...]
        rstd = lax.rsqrt(jnp.mean(hf * hf, axis=-1, keepdims=True) + EPS)
        xhat = hf * rstd
        err = xhat * gv - t_ref[...]
        part = 0.5 * jnp.sum(jnp.mean(err * err, axis=-1, keepdims=True), axis=0, keepdims=True)
        dy = err * (1.0 / D)
        dg = jnp.sum(dy * xhat, axis=0, keepdims=True)
        dx = dy * gv
        dh_ref[...] = rstd * (dx - xhat * jnp.mean(dx * xhat, axis=-1, keepdims=True))

        @pl.when(pl.program_id(0) == 0)
        def _():
            loss_ref[...] = part
            dg_ref[...] = dg

        @pl.when(pl.program_id(0) > 0)
        def _():
            loss_ref[...] += part
            dg_ref[...] += dg

    return pl.pallas_call(
        body, name=name, grid=(T // tm,),
        in_specs=[pl.BlockSpec((tm, D), lambda i: (i, 0)), pl.BlockSpec((1, D), lambda i: (0, 0)),
                  pl.BlockSpec((tm, D), lambda i: (i, 0))],
        out_specs=[pl.BlockSpec((1, 1), lambda i: (0, 0)), pl.BlockSpec((tm, D), lambda i: (i, 0)),
                   pl.BlockSpec((1, D), lambda i: (0, 0))],
        out_shape=[jax.ShapeDtypeStruct((1, 1), F32), jax.ShapeDtypeStruct((T, D), F32),
                   jax.ShapeDtypeStruct((1, D), F32)],
        compiler_params=_params(1))(h, g, target)


def _nt_rows(name, dh, wg, layer, a_mul, out_dtype, tm, deps=()):
    T, D = dh.shape
    rk = wg.shape[2]
    N = N_CHIPS * rk
    with_a = a_mul is not None

    def body(dh_ref, w_ref, *rest):
        o_ref = rest[-1]
        d16 = dh_ref[...].astype(BF16)
        for ch in range(N_CHIPS):
            r = _dot_nt(d16, w_ref[ch])
            if with_a:
                r = r * (2.0 * jnp.maximum(rest[0][:, ch * rk:(ch + 1) * rk].astype(F32), 0.0))
            o_ref[:, ch * rk:(ch + 1) * rk] = r.astype(out_dtype)

    in_specs = [pl.BlockSpec((tm, D), lambda i: (i, 0)),
                pl.BlockSpec((N_CHIPS, None, rk, D), lambda i: (0, layer, 0, 0))]
    args = [dh, wg]
    if with_a:
        in_specs.append(pl.BlockSpec((tm, N), lambda i: (i, 0)))
        args.append(a_mul)
    in_specs += [ANY] * len(deps)
    args += list(deps)
    return pl.pallas_call(
        body, name=name, grid=(T // tm,), in_specs=in_specs,
        out_specs=pl.BlockSpec((tm, N), lambda i: (i, 0)),
        out_shape=jax.ShapeDtypeStruct((T, N), out_dtype),
        compiler_params=_params(1))(*args)


def _nt_cols(name, ysegs, wg, layer, tm, norm):
    Nw, cw = wg.shape[2], wg.shape[3]
    widths = [bs[-1] for _, bs, _ in ysegs]
    pieces = _pieces(widths, cw, 1024)
    ns = len(ysegs)
    T = norm[0].shape[0] if norm is not None else ysegs[0][0].shape[-2]

    def body(*refs):
        y_refs = refs[:ns]
        w_ref = refs[ns]
        acc = refs[-1]
        for n, (s, a0, ch, b0, wd) in enumerate(pieces):
            d = _dot_nt(y_refs[s][:, a0:a0 + wd].astype(BF16), w_ref[ch, :, b0:b0 + wd])
            if n == 0:
                acc[...] = d
            else:
                acc[...] += d
        if norm is None:
            refs[ns + 1][...] = acc[...]
        else:
            h_ref, g_ref, dhin_ref, out_ref, dg_ref = refs[ns + 1:ns + 6]
            dh_c, dg = _rms_bwd(h_ref[...], g_ref[...], acc[...])
            out_ref[...] = dhin_ref[...] + dh_c

            @pl.when(pl.program_id(0) == 0)
            def _():
                dg_ref[...] = dg

            @pl.when(pl.program_id(0) > 0)
            def _():
                dg_ref[...] += dg

    in_specs = [pl.BlockSpec(bs, im) for _, bs, im in ysegs]
    in_specs.append(pl.BlockSpec((N_CHIPS, None, Nw, cw), lambda i: (0, layer, 0, 0)))
    args = [a for a, _, _ in ysegs] + [wg]
    row = pl.BlockSpec((tm, Nw), lambda i: (i, 0))
    vec = pl.BlockSpec((1, Nw), lambda i: (0, 0))
    if norm is None:
        out_specs = row
        out_shape = jax.ShapeDtypeStruct((T, Nw), F32)
    else:
        in_specs += [row, vec, row]
        args += list(norm)
        out_specs = [row, vec]
        out_shape = [jax.ShapeDtypeStruct((T, Nw), F32), jax.ShapeDtypeStruct((1, Nw), F32)]
    return pl.pallas_call(
        body, name=name, grid=(T // tm,), in_specs=in_specs, out_specs=out_specs, out_shape=out_shape,
        scratch_shapes=[pltpu.VMEM((tm, Nw), F32)], compiler_params=_params(1))(*args)


def _tn(name, x, x_act, ysegs, cw, cols_layout, tmm, tt, deps=()):
    T, M = x.shape
    widths = [bs[-1] for _, bs, _ in ysegs]
    N = sum(widths)
    pieces = _pieces(widths, cw if cols_layout else N, 1024)
    ns = len(ysegs)

    def body(x_ref, *refs):
        y_refs = refs[:ns]
        o_ref = refs[-1]

        @pl.when(pl.program_id(1) == 0)
        def _():
            o_ref[...] = jnp.zeros_like(o_ref)

        xt = x_act(x_ref[...])
        for s, a0, ch, b0, wd in pieces:
            d = _dot_tn(xt, y_refs[s][:, a0:a0 + wd].astype(BF16))
            if cols_layout:
                o_ref[ch, :, b0:b0 + wd] += d
            else:
                o_ref[:, b0:b0 + wd] += d

    in_specs = [pl.BlockSpec((tt, tmm), lambda m, t: (t, m))] + [pl.BlockSpec(bs, im) for _, bs, im in ysegs]
    in_specs += [ANY] * len(deps)
    if cols_layout:
        out_specs = pl.BlockSpec((N_CHIPS, tmm, cw), lambda m, t: (0, m, 0))
        out_shape = jax.ShapeDtypeStruct((N_CHIPS, M, cw), F32)
    else:
        out_specs = pl.BlockSpec((tmm, N), lambda m, t: (m, 0))
        out_shape = jax.ShapeDtypeStruct((M, N), F32)
    return pl.pallas_call(
        body, name=name, grid=(M // tmm, T // tt), in_specs=in_specs, out_specs=out_specs, out_shape=out_shape,
        compiler_params=_params(2))(x, *[a for a, _, _ in ysegs], *deps)


def _seg2d(a, t_rows, grid_rank):
    w = a.shape[1]
    if grid_rank == 1:
        return (a, (t_rows, w), lambda i: (i, 0))
    return (a, (t_rows, w), lambda m, t: (t, 0))


def _kv_segments(dk, dv, C, t_rows, grid_rank):
    segs = []
    for g in range(len(PATTERNS)):
        for a in (dk, dv):
            if grid_rank == 1:
                segs.append((a, (t_rows, C), lambda i, g=g: (i, g)))
            else:
                segs.append((a, (t_rows, C), lambda m, t, g=g: (t, g)))
    return segs


def _seg_plane(a, plane, t_rows, grid_rank):
    w = a.shape[2]
    if grid_rank == 1:
        return (a, (None, t_rows, w), lambda i: (plane, i, 0))
    return (a, (None, t_rows, w), lambda m, t: (plane, t, 0))


def _row_tile(rows, row_bytes, budget_bytes=2 * 1024 * 1024):
    t = rows
    while t * row_bytes > budget_bytes and t % 32 == 0:
        t //= 2
    return t


def _pair_add(name, layers, recv, place):
    L = len(layers)
    _, _, hr, c = layers[0].shape
    tr = _row_tile(hr, L * c * 4)

    def body(place_ref, *refs):
        r_ref, o_ref = refs[L], refs[L + 1]
        for l in range(L):
            o_ref[l] = (refs[l][...] + r_ref[l]).astype(BF16)

    stacked = pl.BlockSpec((None, L, tr, c), lambda q, i, pr: (q, 0, i, 0))
    grid_spec = pltpu.PrefetchScalarGridSpec(
        num_scalar_prefetch=1, grid=(N_CHIPS, hr // tr),
        in_specs=[pl.BlockSpec((None, None, tr, c), lambda q, i, pr: (q, pr[1], i, 0))] * L + [stacked],
        out_specs=stacked)
    return pl.pallas_call(body, name=name, grid_spec=grid_spec,
                          out_shape=jax.ShapeDtypeStruct((N_CHIPS, L, hr, c), BF16),
                          compiler_params=_params(2))(place, *layers, recv)


def _chip_add(name, part, slots, place):
    _, L, hr, c = part.shape
    tr = _row_tile(hr, L * c * 4)

    def body(place_ref, own, s1, s2, s3, o_ref):
        f = lambda r: r[...].astype(F32)
        o_ref[...] = ((f(own) + f(s1)) + f(s2)) + f(s3)

    def slot(k):
        return pl.BlockSpec((None, L, tr, c), lambda i, pr: ((pr[0] + k) % N_CHIPS, 0, i, 0))

    grid_spec = pltpu.PrefetchScalarGridSpec(
        num_scalar_prefetch=1, grid=(hr // tr,),
        in_specs=[slot(0), slot(1), slot(2), slot(3)],
        out_specs=pl.BlockSpec((L, None, tr, c), lambda i, pr: (0, pr[1], i, 0)))
    return pl.pallas_call(body, name=name, grid_spec=grid_spec,
                          out_shape=jax.ShapeDtypeStruct((L, 2, hr, c), F32),
                          compiler_params=_params(1))(place, part, slots, slots, slots)


def _adamw(name, w, g, m, v):
    rows, cols = w.shape
    tr = _row_tile(rows, cols * 4, 1024 * 1024)

    def body(w_ref, g_ref, m_ref, v_ref, d_ref, nm_ref, nv_ref):
        d_ref[...], nm_ref[...], nv_ref[...] = _adamw_math(w_ref[...], g_ref[...], m_ref[...], v_ref[...])

    spec = pl.BlockSpec((tr, cols), lambda i: (i, 0))
    return pl.pallas_call(
        body, name=name, grid=(rows // tr,), in_specs=[spec] * 4, out_specs=[spec] * 3,
        out_shape=[jax.ShapeDtypeStruct((rows, cols), F32)] * 3, compiler_params=_params(1))(w, g, m, v)


def _adamw_math(w, g, m, v):
    nm = ADAM_B1 * m + (1.0 - ADAM_B1) * g
    nv = ADAM_B2 * v + (1.0 - ADAM_B2) * jnp.square(g)
    m_hat = nm / (1.0 - ADAM_B1 ** ADAM_STEP)
    v_hat = nv / (1.0 - ADAM_B2 ** ADAM_STEP)
    return -ADAM_LR * (m_hat / (jnp.sqrt(v_hat) + ADAM_EPS) + ADAM_WD * w), nm, nv


def _adamw_layers(name, w, grads, m, v):
    L, r, c = w.shape
    tr = _row_tile(r, L * c * 4, 1024 * 1024)

    def body(*refs):
        w_ref, m_ref, v_ref = refs[:3]
        g_refs = refs[3:3 + L]
        go_ref, d_ref, nm_ref, nv_ref = refs[3 + L:]
        for l in range(L):
            g = g_refs[l][...]
            go_ref[l] = g
            d_ref[l], nm_ref[l], nv_ref[l] = _adamw_math(w_ref[l], g, m_ref[l], v_ref[l])

    stacked = pl.BlockSpec((L, tr, c), lambda i: (0, i, 0))
    return pl.pallas_call(
        body, name=name, grid=(r // tr,),
        in_specs=[stacked] * 3 + [pl.BlockSpec((tr, c), lambda i: (i, 0))] * L, out_specs=[stacked] * 4,
        out_shape=[jax.ShapeDtypeStruct((L, r, c), F32)] * 4, compiler_params=_params(1))(w, m, v, *grads)


def _place():
    x, y, c = lax.axis_index("x"), lax.axis_index("y"), lax.axis_index("c")
    chips = [(1 - x, y), (x, 1 - y), (1 - x, 1 - y)]
    return x, y, c, chips


HBM = pl.BlockSpec(memory_space=pltpu.HBM)
SEM = pl.BlockSpec(memory_space=pltpu.SEMAPHORE)
EFFECT = pltpu.SideEffectType.DATAFLOW_SIDE_EFFECTING


class _Copy:
    def __init__(self, src, src_view, land, dst_view, recv_view, target):
        self.src, self.src_view, self.land, self.dst_view, self.recv_view, self.target = (
            src, src_view, land, dst_view, recv_view, target)


def _whole(ref, place):
    return ref


def _split_start(name, srcs, land_shapes, plans):
    skeys, lkeys = list(srcs), list(land_shapes)
    ns, nl, ng = len(skeys), len(lkeys), len(plans)

    def body(*refs):
        src = dict(zip(skeys, refs[:ns]))
        land = dict(zip(lkeys, refs[ns:ns + nl]))
        sems = refs[ns + nl:ns + nl + 2 * ng]
        token = refs[-1]
        place = _place()
        for gi, plan in enumerate(plans):
            for k, cp in enumerate(plan):
                pltpu.make_async_remote_copy(
                    src_ref=cp.src_view(src[cp.src], place), dst_ref=cp.dst_view(land[cp.land], place),
                    send_sem=sems[2 * gi].at[k], recv_sem=sems[2 * gi + 1].at[k],
                    device_id=cp.target(place), device_id_type=MESH).start()
        token[...] = jnp.zeros_like(token)

    sem_shapes = []
    for plan in plans:
        sem_shapes += [pltpu.SemaphoreType.DMA((len(plan),))] * 2
    buffers = [srcs[k] for k in skeys] + [lax.empty(land_shapes[k].shape, land_shapes[k].dtype) for k in lkeys]
    outs = pl.pallas_call(
        body, name=name,
        out_shape=(*sem_shapes, *[pltpu.HBM(a.shape, a.dtype) for a in buffers], jax.ShapeDtypeStruct((8, LANES), F32)),
        in_specs=[HBM] * (ns + nl),
        out_specs=(*[SEM] * (2 * ng), *[HBM] * (ns + nl), pl.BlockSpec(memory_space=pltpu.VMEM)),
        input_output_aliases={i: 2 * ng + i for i in range(ns + nl)},
        compiler_params=pltpu.CompilerParams(has_side_effects=EFFECT),
    )(*[pltpu.with_memory_space_constraint(a, pltpu.HBM) for a in buffers])
    sems = [(outs[2 * gi], outs[2 * gi + 1]) for gi in range(ng)]
    thru = outs[2 * ng:2 * ng + ns + nl]
    return sems, dict(zip(skeys, thru[:ns])), dict(zip(lkeys, thru[ns:])), outs[-1]


def _split_wait(name, sems, srcs, lands, plan, after):
    skeys, lkeys = list(srcs), list(lands)
    ns, nl = len(skeys), len(lkeys)

    def body(*refs):
        src = dict(zip(skeys, refs[:ns]))
        land = dict(zip(lkeys, refs[ns:ns + nl]))
        ssem, rsem = refs[ns + nl], refs[ns + nl + 1]
        place = _place()
        for k, cp in enumerate(plan):
            pltpu.make_async_remote_copy(
                src_ref=cp.src_view(src[cp.src], place), dst_ref=cp.dst_view(land[cp.land], place),
                send_sem=ssem.at[k], recv_sem=rsem.at[k],
                device_id=cp.target(place), device_id_type=MESH).wait_send()
            got = cp.recv_view(land[cp.land], place)
            pltpu.make_async_remote_copy(
                src_ref=got, dst_ref=got, send_sem=ssem.at[k], recv_sem=rsem.at[k],
                device_id=cp.target(place), device_id_type=MESH).wait_recv()

    buffers = [srcs[k] for k in skeys] + [lands[k] for k in lkeys]
    outs = pl.pallas_call(
        body, name=name, out_shape=tuple(pltpu.HBM(a.shape, a.dtype) for a in buffers),
        in_specs=(*[HBM] * (ns + nl), SEM, SEM, ANY), out_specs=tuple([HBM] * (ns + nl)),
        input_output_aliases={i: i for i in range(ns + nl)},
        compiler_params=pltpu.CompilerParams(has_side_effects=EFFECT),
    )(*buffers, sems[0], sems[1], after)
    return dict(zip(skeys, outs[:ns])), dict(zip(lkeys, outs[ns:]))


def _chip_of(place):
    x, y, c, chips = place
    return 2 * x + y


class _WeightGather:
    def __init__(self, blocks):
        self.plans, shapes = {}, {}
        for key, a in blocks.items():
            shapes[key] = jax.ShapeDtypeStruct((N_CHIPS,) + a.shape, a.dtype)
            slot = lambda ref, place: ref.at[_chip_of(place)]
            plan = [_Copy(key, _whole, key, slot,
                          lambda ref, place, k=k: ref.at[2 * place[3][k][0] + place[3][k][1]],
                          lambda place, k=k: (place[3][k][0], place[3][k][1], place[2])) for k in range(3)]
            plan.append(_Copy(key, _whole, key, slot, slot, lambda place: (place[0], place[1], 1 - place[2])))
            self.plans[key] = plan
        sems, self.srcs, self.lands, self.token = _split_start("gather_start", blocks, shapes, list(self.plans.values()))
        self.sems = dict(zip(self.plans, sems))

    def get(self, l, name, after):
        key = (l, name)
        _, lands = _split_wait(f"gather_wait_{name}{l}", self.sems[key], {key: self.srcs[key]},
                               {key: self.lands[key]}, self.plans[key], after)
        return lands[key][:, None]


class _GradReduce:
    def __init__(self, place):
        self.place = place
        self.jobs = []
        self.done = {}
        self.n = 0

    def submit(self, grads):
        views = {k: a.reshape(N_CHIPS, 2, a.shape[1] // 2, a.shape[2]) for k, a in grads.items()}
        shapes = {k: jax.ShapeDtypeStruct((N_CHIPS,) + a.shape[2:], F32) for k, a in views.items()}
        sibling = lambda place: (place[0], place[1], 1 - place[2])
        plan = [_Copy(k, lambda ref, place: ref.at[:, 1 - place[2]], k, _whole, _whole, sibling) for k in views]
        sems, srcs, lands, token = _split_start(f"grad_pair_start{self.n}", views, shapes, [plan])
        self.jobs.append(dict(id=self.n, stage=1, sems=sems[0], srcs=srcs, lands=lands, plan=plan))
        self.n += 1
        return token

    def pump(self, after):
        tokens = []
        for job in list(self.jobs):
            srcs, lands = _split_wait(f"grad_wait{job['id']}_{job['stage']}", job["sems"], job["srcs"], job["lands"],
                                      job["plan"], after)
            if job["stage"] == 1:
                parts = {k: _pair_add(f"grad_pair_add{job['id']}_{i}", [srcs[k]], lands[k][:, None], self.place)
                         for i, k in enumerate(srcs)}
                shapes = {k: jax.ShapeDtypeStruct(a.shape, a.dtype) for k, a in parts.items()}
                plan = []
                for k in parts:
                    for j in range(3):
                        there = lambda ref, place, j=j: ref.at[2 * place[3][j][0] + place[3][j][1]]
                        plan.append(_Copy(k, there, k, lambda ref, place: ref.at[_chip_of(place)], there,
                                          lambda place, j=j: (place[3][j][0], place[3][j][1], place[2])))
                sems, srcs2, lands2, token = _split_start(f"grad_chip_start{job['id']}", parts, shapes, [plan])
                job.update(stage=2, sems=sems[0], srcs=srcs2, lands=lands2, plan=plan)
                tokens.append(token)
            else:
                for i, k in enumerate(srcs):
                    self.done[k] = _chip_add(f"grad_chip_add{job['id']}_{i}", srcs[k], lands[k], self.place)[0]
                self.jobs.remove(job)
        return tokens

    def finish(self, after):
        while self.jobs:
            self.pump(after)
        return self.done


def _pair_share(halves):
    n = len(halves)

    def body(*refs):
        outs = refs[n:2 * n]
        ssem, rsem = refs[2 * n:]
        x, y, c, _ = _place()
        sends = []
        for t in range(n):
            cp = pltpu.make_async_remote_copy(
                src_ref=outs[t].at[c], dst_ref=outs[t].at[c], send_sem=ssem.at[t], recv_sem=rsem.at[t],
                device_id=(x, y, 1 - c), device_id_type=MESH)
            cp.start()
            sends.append(cp)
        for t in range(n):
            theirs = outs[t].at[1 - c]
            pltpu.make_async_remote_copy(
                src_ref=theirs, dst_ref=theirs, send_sem=ssem.at[t], recv_sem=rsem.at[t],
                device_id=(x, y, 1 - c), device_id_type=MESH).wait_recv()
        for cp in sends:
            cp.wait_send()

    return pl.pallas_call(
        body, name="grad_pair_share", in_specs=[ANY] * n, out_specs=[ANY] * n,
        out_shape=[jax.ShapeDtypeStruct(a.shape, a.dtype) for a in halves],
        input_output_aliases={t: t for t in range(n)},
        scratch_shapes=[pltpu.SemaphoreType.DMA((n,)), pltpu.SemaphoreType.DMA((n,))])(*halves)


def _small_allreduce(part):
    R, C = part.shape
    N_DEV = 8

    def body(in_ref, out_ref, slots, ssem, rsem):
        x, y, c, _ = _place()
        me = 4 * x + 2 * y + c
        sends = []
        for k in range(1, N_DEV):
            kx, ky, kc = (k >> 2) & 1, (k >> 1) & 1, k & 1
            peer = (1 - x if kx else x, 1 - y if ky else y, 1 - c if kc else c)
            cp = pltpu.make_async_remote_copy(
                src_ref=in_ref, dst_ref=slots.at[me], send_sem=ssem.at[k], recv_sem=rsem.at[k],
                device_id=peer, device_id_type=MESH)
            cp.start()
            sends.append(cp)
        slots[me] = in_ref[...]
        for k in range(1, N_DEV):
            kx, ky, kc = (k >> 2) & 1, (k >> 1) & 1, k & 1
            peer = (1 - x if kx else x, 1 - y if ky else y, 1 - c if kc else c)
            slot = slots.at[4 * peer[0] + 2 * peer[1] + peer[2]]
            pltpu.make_async_remote_copy(
                src_ref=slot, dst_ref=slot, send_sem=ssem.at[k], recv_sem=rsem.at[k],
                device_id=peer, device_id_type=MESH).wait_recv()
        acc = slots[0]
        for d in range(1, N_DEV):
            acc = acc + slots[d]
        out_ref[...] = acc
        for cp in sends:
            cp.wait_send()

    vm = pl.BlockSpec(memory_space=pltpu.VMEM)
    return pl.pallas_call(
        body, name="small_allreduce", in_specs=[vm], out_specs=vm,
        out_shape=jax.ShapeDtypeStruct((R, C), F32),
        scratch_shapes=[pltpu.VMEM((N_DEV, R, C), F32), pltpu.SemaphoreType.DMA((N_DEV,)),
                        pltpu.SemaphoreType.DMA((N_DEV,))])(part)


def _local_step(x, target, norm_mix, norm_mlp, norm_kv, norm_final, weights, sink, n_a, n_heads):
    B, S, D = x.shape
    T = B * S
    C = n_heads * HEAD_DIM
    depth = norm_mix.shape[0]
    slopes = 2.0 ** (-ALIBI_MAX_BIAS * jnp.arange(1, n_heads + 1, dtype=F32) / n_heads)
    tm = min(512, T)
    row = lambda v: v.reshape(1, -1)

    h = x.reshape(T, D)
    saved, Wl = [], []
    kv = nkv = h_kv = cwg = None
    for l in range(depth):
        s = {"h_in": h}
        w = {}
        Wl.append(w)
        if l < n_a:
            w["w_a_in"] = weights.get(l, "w_a_in", h)
            s["n1"], bcu = _norm_mm(f"a_in_fwd{l}", h, row(norm_mix[l]), w["w_a_in"], 0, 3, BF16, tm)
            s["bcu"] = bcu.reshape(3, B, S, D)
            if l == 0:
                cwg = weights.get(0, "conv", bcu)[:, 0, :n_a * 3].reshape(N_CHIPS, n_a, 3, -1)
            s["z"] = _conv_fwd(f"conv_fwd{l}", s["bcu"], cwg, l, LANES).reshape(T, D)
            w["w_a_out"] = weights.get(l, "w_a_out", s["z"])
            h = _mm_res_rows(f"a_out_fwd{l}", s["z"], w["w_a_out"], 0, h, _to_bf16, tm)
        else:
            i = l - n_a
            if i == 0:
                h_kv = h
                w["w_kv"] = weights.get(l, "w_kv", h)
                nkv, kv = _norm_mm("kv_fwd", h, row(norm_kv), w["w_kv"], 0, 1, F32, tm)
                kv = kv.reshape(B, S, 2 * 3 * C)
            w["w_q"] = weights.get(l, "w_q", h)
            s["n1"], q = _norm_mm(f"q_fwd{i}", h, row(norm_mix[l]), w["w_q"], 0, 1, F32, tm)
            s["q"] = q.reshape(B, S, 3 * C)
            o, lse = _attn_fwd(f"attn_fwd{i}", s["q"], kv, slopes, n_heads)
            s["o"], s["lse"] = o.reshape(T, C), lse.reshape(T, C)
            w["w_o"] = weights.get(l, "w_o", o)
            h = _mm_res_cols(f"o_fwd{i}", s["o"], w["w_o"], 0, h, tm)
        s["h_mid"] = h
        w["w_up"] = weights.get(l, "w_up", h)
        s["n2"], a = _norm_mm(f"up_fwd{l}", h, row(norm_mlp[l]), w["w_up"], 0, 1, BF16, tm)
        F = a.shape[2]
        s["a"] = a.reshape(T, F)
        w["w_down"] = weights.get(l, "w_down", a)
        h = _mm_res_rows(f"down_fwd{l}", s["a"], w["w_down"], 0, h, _relu2_bf16, tm)
        saved.append(s)

    loss, dh, dg_final = _final_loss("loss_head", h, row(norm_final), target.reshape(T, D), tm)

    g_mix, g_mlp = [None] * depth, [None] * depth
    g_conv = [None] * n_a
    dkv = None
    tt = min(512, T)
    deps = []
    for l in reversed(range(depth)):
        s, w = saved[l], Wl[l]
        da = _nt_rows(f"down_bwd{l}", dh, w["w_down"], 0, s["a"], BF16, tm, deps)
        g_down = _tn(f"down_wgrad{l}", s["a"], _relu2_bf16, [_seg2d(dh, tt, 2)], None, False,
                     min(1024, F), tt).reshape(N_CHIPS, F // N_CHIPS, D)
        g_up = _tn(f"up_wgrad{l}", s["n2"], _to_bf16, [_seg2d(da, tt, 2)], F // N_CHIPS, True, min(512, D), tt)
        dh, g_mlp[l] = _nt_cols(f"up_bwd{l}", [_seg2d(da, tm, 1)], w["w_up"], 0, tm,
                                (s["h_mid"], row(norm_mlp[l]), dh))
        deps = sink.pump(dh) + [sink.submit({("w_up", l): g_up, ("w_down", l): g_down})]
        if l < n_a:
            g_out = _tn(f"a_out_wgrad{l}", s["z"], _to_bf16, [_seg2d(dh, tt, 2)], None, False,
                        D, tt, deps).reshape(N_CHIPS, D // N_CHIPS, D)
            dz = _nt_rows(f"a_out_bwd{l}", dh, w["w_a_out"], 0, None, F32, tm)
            deps = sink.pump(dz) + [sink.submit({("w_a_out", l): g_out})]
            dbcu, g_conv[l] = _conv_bwd(f"conv_bwd{l}", s["bcu"], dz.reshape(B, S, D), cwg, l, LANES)
            dbcu = dbcu.reshape(3, T, D)
            g_in = _tn(f"a_in_wgrad{l}", s["n1"], _to_bf16, [_seg_plane(dbcu, p, tt, 2) for p in range(3)],
                       3 * D // N_CHIPS, True, min(512, D), tt, deps)
            dh, g_mix[l] = _nt_cols(f"a_in_bwd{l}", [_seg_plane(dbcu, p, tm, 1) for p in range(3)],
                                    w["w_a_in"], 0, tm, (s["h_in"], row(norm_mix[l]), dh))
            mixer = {("w_a_in", l): g_in}
        else:
            i = l - n_a
            g_o = _tn(f"o_wgrad{i}", s["o"], _to_bf16, [_seg2d(dh, tt, 2)], D // N_CHIPS, True, C, tt, deps)
            do = _nt_cols(f"o_bwd{i}", [_seg2d(dh, tm, 1)], w["w_o"], 0, tm, None)
            deps = sink.pump(do) + [sink.submit({("w_o", i): g_o})]
            dq, dk, dv = _attn_bwd(f"attn_bwd{i}", s["q"], kv, slopes, s["o"].reshape(B, S, C),
                                   s["lse"].reshape(B, S, C), do.reshape(B, S, C), n_heads, dkv)
            dkv = (dk, dv)
            dq = dq.reshape(T, 3 * C)
            g_q = _tn(f"q_wgrad{i}", s["n1"], _to_bf16, [_seg2d(dq, tt, 2)], 3 * C // N_CHIPS, True, min(512, D), tt,
                      deps)
            dh, g_mix[l] = _nt_cols(f"q_bwd{i}", [_seg2d(dq, tm, 1)], w["w_q"], 0, tm,
                                    (s["h_in"], row(norm_mix[l]), dh))
            mixer = {("w_q", i): g_q}
            if i == 0:
                dk2, dv2 = (t.reshape(T, 3 * C) for t in dkv)
                mixer[("w_kv", 0)] = _tn("kv_wgrad", nkv, _to_bf16, _kv_segments(dk2, dv2, C, tt, 2),
                                         6 * C // N_CHIPS, True, min(512, D), tt)
                dh, g_kv = _nt_cols("kv_bwd", _kv_segments(dk2, dv2, C, tm, 1), w["w_kv"], 0, tm,
                                    (h_kv, row(norm_kv), dh))
        deps = sink.pump(dh) + [sink.submit(mixer)]
    small = dict(norm_mix=jnp.concatenate(g_mix, axis=0), norm_mlp=jnp.concatenate(g_mlp, axis=0),
                 norm_kv=g_kv, norm_final=dg_final, conv_w=jnp.stack(g_conv))
    return loss, dh.reshape(B, S, D), small


BIG = ("w_a_in", "w_a_out", "w_kv", "w_q", "w_o", "w_up", "w_down")
CONV_PAD_ROWS = 16


def kernel(x, norm_mix, norm_mlp, w_a_in, conv_w, w_a_out, norm_kv, w_kv, w_q, w_o, w_up, w_down, norm_final, loss_target, m_norm_mix, m_norm_mlp, m_w_a_in, m_conv_w, m_w_a_out, m_norm_kv, m_w_kv, m_w_q, m_w_o, m_w_up, m_w_down, m_norm_final, v_norm_mix, v_norm_mlp, v_w_a_in, v_conv_w, v_w_a_out, v_norm_kv, v_w_kv, v_w_q, v_w_o, v_w_up, v_w_down, v_norm_final):
    D = x.shape[-1]
    w = dict(norm_mix=norm_mix, norm_mlp=norm_mlp, w_a_in=w_a_in, conv_w=conv_w, w_a_out=w_a_out, norm_kv=norm_kv,
             w_kv=w_kv[None], w_q=w_q, w_o=w_o, w_up=w_up, w_down=w_down, norm_final=norm_final)
    m = dict(norm_mix=m_norm_mix, norm_mlp=m_norm_mlp, w_a_in=m_w_a_in, conv_w=m_conv_w, w_a_out=m_w_a_out,
             norm_kv=m_norm_kv, w_kv=m_w_kv[None], w_q=m_w_q, w_o=m_w_o, w_up=m_w_up, w_down=m_w_down,
             norm_final=m_norm_final)
    v = dict(norm_mix=v_norm_mix, norm_mlp=v_norm_mlp, w_a_in=v_w_a_in, conv_w=v_conv_w, w_a_out=v_w_a_out,
             norm_kv=v_norm_kv, w_kv=v_w_kv[None], w_q=v_w_q, w_o=v_w_o, w_up=v_w_up, w_down=v_w_down,
             norm_final=v_norm_final)
    depth = norm_mix.shape[0]
    n_a, taps, cwc = conv_w.shape
    n_heads = w_o.shape[1] // HEAD_DIM

    conv_rows = jnp.zeros((CONV_PAD_ROWS, cwc), F32).at[:n_a * taps].set(conv_w.reshape(n_a * taps, cwc))
    blocks = {}
    for l in range(depth):
        if l < n_a:
            blocks[(l, "w_a_in")] = w_a_in[l].astype(BF16)
            if l == 0:
                blocks[(0, "conv")] = conv_rows
            blocks[(l, "w_a_out")] = w_a_out[l].astype(BF16)
        else:
            if l == n_a:
                blocks[(l, "w_kv")] = w_kv.astype(BF16)
            blocks[(l, "w_q")] = w_q[l - n_a].astype(BF16)
            blocks[(l, "w_o")] = w_o[l - n_a].astype(BF16)
        blocks[(l, "w_up")] = w_up[l].astype(BF16)
        blocks[(l, "w_down")] = w_down[l].astype(BF16)
    weights = _WeightGather(blocks)
    place = jnp.stack([2 * lax.axis_index("x") + lax.axis_index("y"), lax.axis_index("c")]).astype(jnp.int32)
    sink = _GradReduce(place)

    loss, grad_x, small = _local_step(x, loss_target, norm_mix, norm_mlp, norm_kv, norm_final, weights, sink,
                                      n_a, n_heads)
    loss = lax.psum(loss[0, 0], ("x", "y", "c"))

    done = sink.finish(grad_x)
    keys = list(done)
    shared = dict(zip(keys, _pair_share([done[k] for k in keys])))
    grads = {}

    packed = jnp.concatenate([small["norm_mix"], small["norm_mlp"], small["norm_kv"], small["norm_final"],
                              small["conv_w"].reshape(n_a * taps, D)], axis=0)
    pad = (-packed.shape[0]) % 8
    packed = jnp.pad(packed, ((0, pad), (0, 0)))
    total = _small_allreduce(packed)
    grads["norm_mix"] = total[:depth]
    grads["norm_mlp"] = total[depth:2 * depth]
    grads["norm_kv"] = total[2 * depth]
    grads["norm_final"] = total[2 * depth + 1]
    chip = 2 * lax.axis_index("x") + lax.axis_index("y")
    conv_full = total[2 * depth + 2:2 * depth + 2 + n_a * taps].reshape(n_a, taps, N_CHIPS, cwc)
    grads["conv_w"] = lax.dynamic_index_in_dim(conv_full, chip, axis=2, keepdims=False)

    order = ("norm_mix", "norm_mlp", "w_a_in", "conv_w", "w_a_out", "norm_kv", "w_kv", "w_q", "w_o", "w_up",
             "w_down", "norm_final")
    delta, new_m, new_v = {}, {}, {}
    vec_names = ("norm_mix", "norm_mlp", "norm_kv", "norm_final")
    rows_of = lambda a: a.reshape(-1, D)
    vw, vg, vm_, vv = (jnp.concatenate([rows_of(t[k]) for k in vec_names], axis=0) for t in (w, grads, m, v))
    vpad = (-vw.shape[0]) % 8
    padrows = lambda a: jnp.pad(a, ((0, vpad), (0, 0)))
    vd, vnm, vnv = _adamw("adamw_norms", padrows(vw), padrows(vg), padrows(vm_), padrows(vv))
    off = 0
    for k in vec_names:
        r = rows_of(w[k]).shape[0]
        delta[k] = vd[off:off + r].reshape(w[k].shape)
        new_m[k] = vnm[off:off + r].reshape(w[k].shape)
        new_v[k] = vnv[off:off + r].reshape(w[k].shape)
        off += r
    cpad = (-n_a * taps) % 8
    two_d = lambda a: jnp.pad(a.reshape(-1, cwc), ((0, cpad), (0, 0)))
    cd, cnm, cnv = _adamw("adamw_conv_w", two_d(w["conv_w"]), two_d(grads["conv_w"]), two_d(m["conv_w"]),
                          two_d(v["conv_w"]))
    delta["conv_w"], new_m["conv_w"], new_v["conv_w"] = (t[:n_a * taps].reshape(conv_w.shape) for t in (cd, cnm, cnv))
    for k in BIG:
        per_layer = [shared[(k, l)].reshape(w[k].shape[1:]) for l in range(w[k].shape[0])]
        grads[k], delta[k], new_m[k], new_v[k] = _adamw_layers(f"adamw_{k}", w[k], per_layer, m[k], v[k])
    fix = lambda k, a: a[0] if k == "w_kv" else a
    return (loss, grad_x, *[fix(k, grads[k]) for k in order], *[fix(k, delta[k]) for k in order],
            *[fix(k, new_m[k]) for k in order], *[fix(k, new_v[k]) for k in order])
```

```python
import functools

import jax
import jax.numpy as jnp
from jax import lax
from jax.experimental import pallas as pl
from jax.experimental.pallas import tpu as pltpu

F32 = jnp.float32
BF16 = jnp.bfloat16
MESH = pl.DeviceIdType.MESH

EPS = 1e-5
PATTERNS = ((128, 1), (512, 4), (2048, 16))
HEAD_DIM = 64
ALIBI_MAX_BIAS = 8.0
NEG_INF = -1e30
ATT_BLK = 128
BWD_UNROLL = 8
N_CHIPS = 4
LANES = 128
VMEM_LIMIT = 56 * 1024 * 1024

ADAM_LR = 0.001
ADAM_B1 = 0.9
ADAM_B2 = 0.999
ADAM_EPS = 1e-08
ADAM_WD = 0.01
ADAM_STEP = 10


ANY = pl.BlockSpec(memory_space=pl.ANY)


def _params(n_grid_axes):
    return pltpu.CompilerParams(dimension_semantics=("arbitrary",) * n_grid_axes, vmem_limit_bytes=VMEM_LIMIT)


def _dot(a, b):
    return jnp.dot(a, b, preferred_element_type=F32)


def _dot_nt(a, b):
    return lax.dot_general(a, b, (((1,), (1,)), ((), ())), preferred_element_type=F32)


def _dot_tn(a, b):
    return lax.dot_general(a, b, (((0,), (0,)), ((), ())), preferred_element_type=F32)


def _relu2(a):
    return jnp.square(jnp.maximum(a, 0.0))


def _rms(hf, g):
    y = hf * lax.rsqrt(jnp.mean(hf * hf, axis=-1, keepdims=True) + EPS)
    return y * g


def _rms_bwd(hf, g, dn):
    rstd = lax.rsqrt(jnp.mean(hf * hf, axis=-1, keepdims=True) + EPS)
    xhat = hf * rstd
    dg = jnp.sum(dn * xhat, axis=0, keepdims=True)
    dx = dn * g
    dh = rstd * (dx - xhat * jnp.mean(dx * xhat, axis=-1, keepdims=True))
    return dh, dg


def _pieces(seg_widths, chunk_width, max_width):
    total = sum(seg_widths)
    cuts = {0, total}
    acc = 0
    for w in seg_widths:
        cuts.add(acc)
        acc += w
    cuts.update(range(0, total, chunk_width))
    cuts = sorted(cuts)
    fine = []
    for lo, hi in zip(cuts[:-1], cuts[1:]):
        while hi - lo > max_width:
            fine.append((lo, lo + max_width))
            lo += max_width
        fine.append((lo, hi))
    out = []
    for lo, hi in fine:
        acc = 0
        for s, w in enumerate(seg_widths):
            if lo < acc + w:
                break
            acc += w
        out.append((s, lo - acc, lo // chunk_width, lo % chunk_width, hi - lo))
    return out


def _relu2_bf16(a):
    return _relu2(a.astype(F32)).astype(BF16)


def _to_bf16(a):
    return a.astype(BF16)


def _norm_mm(name, h, g, wg, layer, planes, out_dtype, tm):
    T, D = h.shape
    cw = wg.shape[3]
    N = N_CHIPS * cw
    pw = N // planes
    pieces = _pieces([pw] * planes, cw, 512)

    def body(h_ref, g_ref, w_ref, n_ref, o_ref):
        n = _rms(h_ref[...], g_ref[...]).astype(BF16)
        n_ref[...] = n
        for s, a0, ch, b0, wd in pieces:
            o_ref[s, :, a0:a0 + wd] = _dot(n, w_ref[ch, :, b0:b0 + wd]).astype(out_dtype)

    return pl.pallas_call(
        body, name=name, grid=(T // tm,),
        in_specs=[pl.BlockSpec((tm, D), lambda i: (i, 0)),
                  pl.BlockSpec((1, D), lambda i: (0, 0)),
                  pl.BlockSpec((N_CHIPS, None, D, cw), lambda i: (0, layer, 0, 0))],
        out_specs=[pl.BlockSpec((tm, D), lambda i: (i, 0)),
                   pl.BlockSpec((planes, tm, pw), lambda i: (0, i, 0))],
        out_shape=[jax.ShapeDtypeStruct((T, D), BF16), jax.ShapeDtypeStruct((planes, T, pw), out_dtype)],
        compiler_params=_params(1))(h, g, wg)


def _mm_res_rows(name, a, wg, layer, h, act, tm):
    T = a.shape[0]
    rk, D = wg.shape[2], wg.shape[3]

    def body(a_ref, w_ref, h_ref, o_ref):
        acc = h_ref[...]
        for k in range(N_CHIPS):
            acc = acc + _dot(act(a_ref[:, k * rk:(k + 1) * rk]), w_ref[k])
        o_ref[...] = acc

    return pl.pallas_call(
        body, name=name, grid=(T // tm,),
        in_specs=[pl.BlockSpec((tm, N_CHIPS * rk), lambda i: (i, 0)),
                  pl.BlockSpec((N_CHIPS, None, rk, D), lambda i: (0, layer, 0, 0)),
                  pl.BlockSpec((tm, D), lambda i: (i, 0))],
        out_specs=pl.BlockSpec((tm, D), lambda i: (i, 0)),
        out_shape=jax.ShapeDtypeStruct((T, D), F32),
        compiler_params=_params(1))(a, wg, h)


def _mm_res_cols(name, a, wg, layer, h, tm):
    T, K = a.shape
    cw = wg.shape[3]
    D = N_CHIPS * cw

    def body(a_ref, w_ref, h_ref, o_ref):
        a16 = a_ref[...].astype(BF16)
        for j in range(N_CHIPS):
            o_ref[:, j * cw:(j + 1) * cw] = h_ref[:, j * cw:(j + 1) * cw] + _dot(a16, w_ref[j])

    return pl.pallas_call(
        body, name=name, grid=(T // tm,),
        in_specs=[pl.BlockSpec((tm, K), lambda i: (i, 0)),
                  pl.BlockSpec((N_CHIPS, None, K, cw), lambda i: (0, layer, 0, 0)),
                  pl.BlockSpec((tm, D), lambda i: (i, 0))],
        out_specs=pl.BlockSpec((tm, D), lambda i: (i, 0)),
        out_shape=jax.ShapeDtypeStruct((T, D), F32),
        compiler_params=_params(1))(a, wg, h)


CONV_ROWS = 256
CONV_HALO = 16


def _conv_shifted(ext, k, r0, rows):
    rolled = pltpu.roll(ext, k, 0)[CONV_HALO:]
    t = r0 + lax.broadcasted_iota(jnp.int32, rolled.shape, 0)
    return jnp.where(t >= k, rolled, 0.0)


def _conv_ahead(ext, k, r0, rows, S):
    rolled = pltpu.roll(ext, rows + CONV_HALO - k, 0)[:rows]
    t = r0 + lax.broadcasted_iota(jnp.int32, rolled.shape, 0)
    return jnp.where(t + k < S, rolled, 0.0)


def _conv_fwd(name, bcu, cwg, layer, tc):
    _, B, S, D = bcu.shape
    cwc = cwg.shape[3]
    per_chunk = cwc // tc
    R = min(CONV_ROWS, S)

    def body(x_ref, w_ref, z_ref):
        w = [w_ref[k:k + 1, :] for k in range(3)]

        def step(i, carry):
            r0 = pl.multiple_of(i * R, R)
            h0 = pl.multiple_of(jnp.maximum(r0 - CONV_HALO, 0), CONV_HALO)
            ld = lambda p, start, rows: x_ref[p, pl.ds(start, rows), :].astype(F32)
            cu = jnp.concatenate([ld(1, h0, CONV_HALO) * ld(2, h0, CONV_HALO), ld(1, r0, R) * ld(2, r0, R)], axis=0)
            conv = w[0] * cu[CONV_HALO:]
            conv = conv + w[1] * _conv_shifted(cu, 1, r0, R)
            conv = conv + w[2] * _conv_shifted(cu, 2, r0, R)
            z_ref[pl.ds(r0, R), :] = (ld(0, r0, R) * conv).astype(BF16)
            return carry

        lax.fori_loop(0, S // R, step, 0)

    return pl.pallas_call(
        body, name=name, grid=(B, D // tc),
        in_specs=[pl.BlockSpec((3, None, S, tc), lambda b, j: (0, b, 0, j)),
                  pl.BlockSpec((None, None, 3, tc), lambda b, j: (j // per_chunk, layer, 0, j % per_chunk))],
        out_specs=pl.BlockSpec((None, S, tc), lambda b, j: (b, 0, j)),
        out_shape=jax.ShapeDtypeStruct((B, S, D), BF16),
        compiler_params=_params(2))(bcu, cwg)


def _conv_bwd(name, bcu, dz, cwg, layer, tc):
    _, B, S, D = bcu.shape
    cwc = cwg.shape[3]
    per_chunk = cwc // tc
    R = min(CONV_ROWS, S)

    def body(x_ref, dz_ref, w_ref, d_ref, dw_ref):
        w = [w_ref[k:k + 1, :] for k in range(3)]

        @pl.when(pl.program_id(1) == 0)
        def _():
            dw_ref[...] = jnp.zeros_like(dw_ref)

        def step(i, carry):
            r0 = pl.multiple_of(i * R, R)
            h0 = pl.multiple_of(jnp.maximum(r0 - CONV_HALO, 0), CONV_HALO)
            a0 = pl.multiple_of(jnp.minimum(r0 + R, S - CONV_HALO), CONV_HALO)
            ld = lambda p, start, rows: x_ref[p, pl.ds(start, rows), :].astype(F32)
            b, c, u = ld(0, r0, R), ld(1, r0, R), ld(2, r0, R)
            dz = dz_ref[pl.ds(r0, R), :]
            cu = jnp.concatenate([ld(1, h0, CONV_HALO) * ld(2, h0, CONV_HALO), c * u], axis=0)
            cu1 = _conv_shifted(cu, 1, r0, R)
            cu2 = _conv_shifted(cu, 2, r0, R)
            conv = w[0] * (c * u) + w[1] * cu1 + w[2] * cu2
            dconv = dz * b
            dca = jnp.concatenate([dconv, dz_ref[pl.ds(a0, CONV_HALO), :] * ld(0, a0, CONV_HALO)], axis=0)
            dcu = w[0] * dconv + w[1] * _conv_ahead(dca, 1, r0, R, S) + w[2] * _conv_ahead(dca, 2, r0, R, S)
            d_ref[0, pl.ds(r0, R), :] = (dz * conv).astype(BF16)
            d_ref[1, pl.ds(r0, R), :] = (dcu * u).astype(BF16)
            d_ref[2, pl.ds(r0, R), :] = (dcu * c).astype(BF16)
            return (carry[0] + jnp.sum(dconv * (c * u), axis=0, keepdims=True),
                    carry[1] + jnp.sum(dconv * cu1, axis=0, keepdims=True),
                    carry[2] + jnp.sum(dconv * cu2, axis=0, keepdims=True))

        zero = jnp.zeros((1, tc), F32)
        s0, s1, s2 = lax.fori_loop(0, S // R, step, (zero, zero, zero))
        for k, sk in enumerate((s0, s1, s2)):
            dw_ref[k:k + 1, :] += sk

    return pl.pallas_call(
        body, name=name, grid=(D // tc, B),
        in_specs=[pl.BlockSpec((3, None, S, tc), lambda j, b: (0, b, 0, j)),
                  pl.BlockSpec((None, S, tc), lambda j, b: (b, 0, j)),
                  pl.BlockSpec((None, None, 3, tc), lambda j, b: (j // per_chunk, layer, 0, j % per_chunk))],
        out_specs=[pl.BlockSpec((3, None, S, tc), lambda j, b: (0, b, 0, j)),
                   pl.BlockSpec((3, tc), lambda j, b: (0, j))],
        out_shape=[jax.ShapeDtypeStruct((3, B, S, D), BF16), jax.ShapeDtypeStruct((3, D), F32)],
        compiler_params=_params(2))(bcu, dz, cwg)


def _att_rows(dil, idx, nb):
    r, n = idx // nb, idx % nb
    if dil == 1:
        cur = pl.ds(pl.multiple_of(n * ATT_BLK, ATT_BLK), ATT_BLK)
        prev = pl.ds(pl.multiple_of(jnp.maximum(n - 1, 0) * ATT_BLK, ATT_BLK), ATT_BLK)
    else:
        cur = pl.ds(n * (ATT_BLK * dil) + r, ATT_BLK, stride=dil)
        prev = pl.ds(jnp.maximum(n - 1, 0) * (ATT_BLK * dil) + r, ATT_BLK, stride=dil)
    return n, cur, prev


def _att_bias(bias_ref, dil, sl_ref, hp):
    row = lax.broadcasted_iota(jnp.int32, (2 * ATT_BLK, 2 * ATT_BLK), 0)
    ci = lax.broadcasted_iota(jnp.int32, (2 * ATT_BLK, 2 * ATT_BLK), 1)
    j = ATT_BLK + (row & (ATT_BLK - 1)) - ci
    slope = jnp.where(row < ATT_BLK, sl_ref[2 * hp], sl_ref[2 * hp + 1])
    rest = jnp.where((j >= 0) & (j <= ATT_BLK), -slope * (dil * j).astype(F32), NEG_INF)
    bias_ref[1] = rest
    bias_ref[0] = jnp.where(ci >= ATT_BLK, rest, NEG_INF)


def _stack_heads(x16, lane):
    first = lane < HEAD_DIM
    return jnp.concatenate([jnp.where(first, x16, jnp.zeros_like(x16)),
                            jnp.where(first, jnp.zeros_like(x16), x16)], axis=0)


def _per_head(col, lane):
    return jnp.where(lane < HEAD_DIM, col[:ATT_BLK], col[ATT_BLK:])


def _attn_fwd(name, q, kv, slopes, n_heads):
    B, S, CQ = q.shape
    HP = n_heads * HEAD_DIM // LANES
    scale = HEAD_DIM ** -0.5
    n_groups = len(PATTERNS)
    CH = 256

    def body(sl_ref, q_ref, k_ref, v_ref, o_ref, lse_ref, bias_ref, *parts):
        og, lg = parts[:n_groups], parts[n_groups:]
        hp, g = pl.program_id(1), pl.program_id(2)
        lane = lax.broadcasted_iota(jnp.int32, (1, LANES), 1)

        for gi, (window, dil) in enumerate(PATTERNS):
            nb = S // dil // ATT_BLK

            @pl.when(g == gi)
            def _(gi=gi, dil=dil, nb=nb):
                _att_bias(bias_ref, dil, sl_ref, hp)

                def step(idx, carry):
                    n, cur, prev = _att_rows(dil, idx, nb)
                    qs = _stack_heads((q_ref[cur, :] * scale).astype(BF16), lane)
                    kc = jnp.concatenate([k_ref[prev, :], k_ref[cur, :]], axis=0).astype(BF16)
                    vc = jnp.concatenate([v_ref[prev, :], v_ref[cur, :]], axis=0).astype(BF16)
                    s = _dot_nt(qs, kc) + bias_ref[jnp.minimum(n, 1)]
                    m = jnp.max(s, axis=-1, keepdims=True)
                    p = jnp.exp(s - m)
                    l = jnp.sum(p, axis=-1, keepdims=True)
                    p16 = p.astype(BF16)
                    o_un = _dot(jnp.concatenate([p16[:ATT_BLK], p16[ATT_BLK:]], axis=1), _stack_heads_rows(vc, lane))
                    og[gi][cur, :] = o_un / _per_head(l, lane)
                    lg[gi][cur, :] = _per_head(m + jnp.log(l), lane)
                    return carry

                lax.fori_loop(0, S // ATT_BLK, step, 0, unroll=8)

        @pl.when(g == n_groups - 1)
        def _():
            def comb(i, carry):
                rows = pl.ds(pl.multiple_of(i * CH, CH), CH)
                a, b, c = lg[0][rows, :], lg[1][rows, :], lg[2][rows, :]
                m = jnp.maximum(jnp.maximum(a, b), c)
                ea, eb, ec = jnp.exp(a - m), jnp.exp(b - m), jnp.exp(c - m)
                z = ea + eb + ec
                o_ref[rows, :] = (ea / z) * og[0][rows, :] + (eb / z) * og[1][rows, :] + (ec / z) * og[2][rows, :]
                lse_ref[rows, :] = m + jnp.log(z)
                return carry

            lax.fori_loop(0, S // CH, comb, 0)

    blk = (None, S, LANES)
    out = pl.BlockSpec(blk, lambda b, hp, g: (b, 0, hp))
    return pl.pallas_call(
        body, name=name, grid=(B, HP, n_groups),
        in_specs=[pl.BlockSpec(memory_space=pltpu.SMEM),
                  pl.BlockSpec(blk, lambda b, hp, g: (b, 0, g * HP + hp)),
                  pl.BlockSpec(blk, lambda b, hp, g: (b, 0, g * 2 * HP + hp)),
                  pl.BlockSpec(blk, lambda b, hp, g: (b, 0, g * 2 * HP + HP + hp))],
        out_specs=[out, out],
        out_shape=[jax.ShapeDtypeStruct((B, S, HP * LANES), F32)] * 2,
        scratch_shapes=[pltpu.VMEM((2, 2 * ATT_BLK, 2 * ATT_BLK), F32)] + [pltpu.VMEM((S, LANES), F32)] * (2 * n_groups),
        compiler_params=_params(3))(slopes, q, kv, kv)


def _stack_heads_rows(x16, lane):
    first = lane < HEAD_DIM
    return jnp.concatenate([jnp.where(first, x16, jnp.zeros_like(x16)),
                            jnp.where(first, jnp.zeros_like(x16), x16)], axis=0)


def _attn_bwd(name, q, kv, slopes, o, lse, do, n_heads, dkv_prev):
    B, S, CQ = q.shape
    HP = n_heads * HEAD_DIM // LANES
    scale = HEAD_DIM ** -0.5
    n_groups = len(PATTERNS)
    n_prev = 0 if dkv_prev is None else 2

    def body(sl_ref, q_ref, k_ref, v_ref, o_ref, lse_ref, do_ref, *rest):
        dq_ref, dk_ref, dv_ref, bias_ref = rest[n_prev:]
        hp, g = pl.program_id(1), pl.program_id(2)
        lane = lax.broadcasted_iota(jnp.int32, (1, LANES), 1)
        first = lane < HEAD_DIM

        def flush(rows, dk, dv):
            if n_prev:
                dk = dk + rest[0][rows, :]
                dv = dv + rest[1][rows, :]
            dk_ref[rows, :] = dk
            dv_ref[rows, :] = dv

        for gi, (window, dil) in enumerate(PATTERNS):
            nb = S // dil // ATT_BLK
            n_blocks = S // ATT_BLK

            @pl.when(g == gi)
            def _(dil=dil, nb=nb, n_blocks=n_blocks):
                _att_bias(bias_ref, dil, sl_ref, hp)

                def block(idx, carry, first_of_all):
                    n, cur, prev = _att_rows(dil, idx, nb)
                    qs = _stack_heads((q_ref[cur, :] * scale).astype(BF16), lane)
                    kc = jnp.concatenate([k_ref[prev, :], k_ref[cur, :]], axis=0).astype(BF16)
                    vc = jnp.concatenate([v_ref[prev, :], v_ref[cur, :]], axis=0).astype(BF16)
                    dob = do_ref[cur, :]
                    prod = dob * o_ref[cur, :]
                    lseb = lse_ref[cur, :]
                    dos = _stack_heads(dob.astype(BF16), lane)
                    delta = jnp.concatenate(
                        [jnp.sum(jnp.where(first, prod, 0.0), axis=-1, keepdims=True),
                         jnp.sum(jnp.where(first, 0.0, prod), axis=-1, keepdims=True)], axis=0)
                    lse_col = jnp.concatenate(
                        [jnp.max(jnp.where(first, lseb, -jnp.inf), axis=-1, keepdims=True),
                         jnp.max(jnp.where(first, -jnp.inf, lseb), axis=-1, keepdims=True)], axis=0)
                    s = _dot_nt(qs, kc) + bias_ref[jnp.minimum(n, 1)]
                    p = jnp.exp(s - lse_col)
                    ds = p * (_dot_nt(dos, vc) - delta)
                    ds16 = ds.astype(BF16)
                    dq = _dot(jnp.concatenate([ds16[:ATT_BLK], ds16[ATT_BLK:]], axis=1), _stack_heads_rows(kc, lane))
                    dq_ref[cur, :] = dq * scale
                    dk = _dot_tn(ds16, qs)
                    dv = _dot_tn(p.astype(BF16), dos)

                    def flush_before():
                        _, before, _ = _att_rows(dil, idx - 1, nb)
                        flush(before, carry[0] + dk[:ATT_BLK], carry[1] + dv[:ATT_BLK])

                    if first_of_all:
                        pl.when(idx > 0)(flush_before)
                    else:
                        flush_before()
                    return dk[ATT_BLK:], dv[ATT_BLK:]

                def step(i, carry):
                    for u in range(BWD_UNROLL):
                        carry = block(i * BWD_UNROLL + u, carry, u == 0)
                    return carry

                zero = jnp.zeros((ATT_BLK, LANES), F32)
                dk_last, dv_last = lax.fori_loop(0, n_blocks // BWD_UNROLL, step, (zero, zero))
                _, last, _ = _att_rows(dil, n_blocks - 1, nb)
                flush(last, dk_last, dv_last)

    blk = (None, S, LANES)
    shared = pl.BlockSpec(blk, lambda b, hp, g: (b, 0, hp))
    grouped = pl.BlockSpec(blk, lambda b, hp, g: (b, 0, g * HP + hp))
    prev = [] if dkv_prev is None else list(dkv_prev)
    gshape = jax.ShapeDtypeStruct((B, S, n_groups * HP * LANES), F32)
    return pl.pallas_call(
        body, name=name, grid=(B, HP, n_groups),
        in_specs=[pl.BlockSpec(memory_space=pltpu.SMEM), grouped,
                  pl.BlockSpec(blk, lambda b, hp, g: (b, 0, g * 2 * HP + hp)),
                  pl.BlockSpec(blk, lambda b, hp, g: (b, 0, g * 2 * HP + HP + hp)),
                  shared, shared, shared] + [grouped] * n_prev,
        out_specs=[grouped] * 3, out_shape=[gshape] * 3,
        scratch_shapes=[pltpu.VMEM((2, 2 * ATT_BLK, 2 * ATT_BLK), F32)],
        compiler_params=_params(3))(slopes, q, kv, kv, o, lse, do, *prev)


def _final_loss(name, h, g, target, tm):
    T, D = h.shape

    def body(h_ref, g_ref, t_ref, loss_ref, dh_ref, dh16_ref, dg_ref):
        hf = h_ref[...]
        gv = g_ref[...]
        rstd = lax.rsqrt(jnp.mean(hf * hf, axis=-1, keepdims=True) + EPS)
        xhat = hf * rstd
        err = xhat * gv - t_ref[...]
        part = 0.5 * jnp.sum(jnp.mean(err * err, axis=-1, keepdims=True), axis=0, keepdims=True)
        dy = err * (1.0 / D)
        dg = jnp.sum(dy * xhat, axis=0, keepdims=True)
        dx = dy * gv
        dh = rstd * (dx - xhat * jnp.mean(dx * xhat, axis=-1, keepdims=True))
        dh_ref[...] = dh
        dh16_ref[...] = dh.astype(BF16)

        @pl.when(pl.program_id(0) == 0)
        def _():
            loss_ref[...] = part
            dg_ref[...] = dg

        @pl.when(pl.program_id(0) > 0)
        def _():
            loss_ref[...] += part
            dg_ref[...] += dg

    return pl.pallas_call(
        body, name=name, grid=(T // tm,),
        in_specs=[pl.BlockSpec((tm, D), lambda i: (i, 0)), pl.BlockSpec((1, D), lambda i: (0, 0)),
                  pl.BlockSpec((tm, D), lambda i: (i, 0))],
        out_specs=[pl.BlockSpec((1, 1), lambda i: (0, 0)), pl.BlockSpec((tm, D), lambda i: (i, 0)),
                   pl.BlockSpec((tm, D), lambda i: (i, 0)), pl.BlockSpec((1, D), lambda i: (0, 0))],
        out_shape=[jax.ShapeDtypeStruct((1, 1), F32), jax.ShapeDtypeStruct((T, D), F32),
                   jax.ShapeDtypeStruct((T, D), BF16), jax.ShapeDtypeStruct((1, D), F32)],
        compiler_params=_params(1))(h, g, target)


def _nt_rows(name, dh, wg, layer, a_mul, out_dtype, tm, deps=()):
    T, D = dh.shape
    rk = wg.shape[2]
    N = N_CHIPS * rk
    with_a = a_mul is not None

    def body(dh_ref, w_ref, *rest):
        o_ref = rest[-1]
        d16 = dh_ref[...]
        for ch in range(N_CHIPS):
            r = _dot_nt(d16, w_ref[ch])
            if with_a:
                r = r * (2.0 * jnp.maximum(rest[0][:, ch * rk:(ch + 1) * rk].astype(F32), 0.0))
            o_ref[:, ch * rk:(ch + 1) * rk] = r.astype(out_dtype)

    in_specs = [pl.BlockSpec((tm, D), lambda i: (i, 0)),
                pl.BlockSpec((N_CHIPS, None, rk, D), lambda i: (0, layer, 0, 0))]
    args = [dh, wg]
    if with_a:
        in_specs.append(pl.BlockSpec((tm, N), lambda i: (i, 0)))
        args.append(a_mul)
    in_specs += [ANY] * len(deps)
    args += list(deps)
    return pl.pallas_call(
        body, name=name, grid=(T // tm,), in_specs=in_specs,
        out_specs=pl.BlockSpec((tm, N), lambda i: (i, 0)),
        out_shape=jax.ShapeDtypeStruct((T, N), out_dtype),
        compiler_params=_params(1))(*args)


def _nt_cols(name, ysegs, wg, layer, tm, norm):
    Nw, cw = wg.shape[2], wg.shape[3]
    widths = [bs[-1] for _, bs, _ in ysegs]
    pieces = _pieces(widths, cw, 1024)
    ns = len(ysegs)
    T = norm[0].shape[0] if norm is not None else ysegs[0][0].shape[-2]

    def body(*refs):
        y_refs = refs[:ns]
        w_ref = refs[ns]
        acc = refs[-1]
        for n, (s, a0, ch, b0, wd) in enumerate(pieces):
            d = _dot_nt(y_refs[s][:, a0:a0 + wd].astype(BF16), w_ref[ch, :, b0:b0 + wd])
            if n == 0:
                acc[...] = d
            else:
                acc[...] += d
        if norm is None:
            refs[ns + 1][...] = acc[...]
        else:
            h_ref, g_ref, dhin_ref, out_ref, out16_ref, dg_ref = refs[ns + 1:ns + 7]
            dh_c, dg = _rms_bwd(h_ref[...], g_ref[...], acc[...])
            dh = dhin_ref[...] + dh_c
            out_ref[...] = dh
            out16_ref[...] = dh.astype(BF16)

            @pl.when(pl.program_id(0) == 0)
            def _():
                dg_ref[...] = dg

            @pl.when(pl.program_id(0) > 0)
            def _():
                dg_ref[...] += dg

    in_specs = [pl.BlockSpec(bs, im) for _, bs, im in ysegs]
    in_specs.append(pl.BlockSpec((N_CHIPS, None, Nw, cw), lambda i: (0, layer, 0, 0)))
    args = [a for a, _, _ in ysegs] + [wg]
    row = pl.BlockSpec((tm, Nw), lambda i: (i, 0))
    vec = pl.BlockSpec((1, Nw), lambda i: (0, 0))
    if norm is None:
        out_specs = row
        out_shape = jax.ShapeDtypeStruct((T, Nw), F32)
    else:
        in_specs += [row, vec, row]
        args += list(norm)
        out_specs = [row, row, vec]
        out_shape = [jax.ShapeDtypeStruct((T, Nw), F32), jax.ShapeDtypeStruct((T, Nw), BF16),
                     jax.ShapeDtypeStruct((1, Nw), F32)]
    return pl.pallas_call(
        body, name=name, grid=(T // tm,), in_specs=in_specs, out_specs=out_specs, out_shape=out_shape,
        scratch_shapes=[pltpu.VMEM((tm, Nw), F32)], compiler_params=_params(1))(*args)


def _tn(name, x, x_act, ysegs, cw, cols_layout, tmm, tt, deps=()):
    T, M = x.shape
    widths = [bs[-1] for _, bs, _ in ysegs]
    N = sum(widths)
    pieces = _pieces(widths, cw if cols_layout else N, 1024)
    ns = len(ysegs)

    def body(x_ref, *refs):
        y_refs = refs[:ns]
        o_ref = refs[-1]

        @pl.when(pl.program_id(1) == 0)
        def _():
            o_ref[...] = jnp.zeros_like(o_ref)

        xt = x_act(x_ref[...])
        for s, a0, ch, b0, wd in pieces:
            d = _dot_tn(xt, y_refs[s][:, a0:a0 + wd].astype(BF16))
            if cols_layout:
                o_ref[ch, :, b0:b0 + wd] += d
            else:
                o_ref[:, b0:b0 + wd] += d

    in_specs = [pl.BlockSpec((tt, tmm), lambda m, t: (t, m))] + [pl.BlockSpec(bs, im) for _, bs, im in ysegs]
    in_specs += [ANY] * len(deps)
    if cols_layout:
        out_specs = pl.BlockSpec((N_CHIPS, tmm, cw), lambda m, t: (0, m, 0))
        out_shape = jax.ShapeDtypeStruct((N_CHIPS, M, cw), F32)
    else:
        out_specs = pl.BlockSpec((tmm, N), lambda m, t: (m, 0))
        out_shape = jax.ShapeDtypeStruct((M, N), F32)
    return pl.pallas_call(
        body, name=name, grid=(M // tmm, T // tt), in_specs=in_specs, out_specs=out_specs, out_shape=out_shape,
        compiler_params=_params(2))(x, *[a for a, _, _ in ysegs], *deps)


def _seg2d(a, t_rows, grid_rank):
    w = a.shape[1]
    if grid_rank == 1:
        return (a, (t_rows, w), lambda i: (i, 0))
    return (a, (t_rows, w), lambda m, t: (t, 0))


def _kv_segments(dk, dv, C, t_rows, grid_rank):
    segs = []
    for g in range(len(PATTERNS)):
        for a in (dk, dv):
            if grid_rank == 1:
                segs.append((a, (t_rows, C), lambda i, g=g: (i, g)))
            else:
                segs.append((a, (t_rows, C), lambda m, t, g=g: (t, g)))
    return segs


def _seg_plane(a, plane, t_rows, grid_rank):
    w = a.shape[2]
    if grid_rank == 1:
        return (a, (None, t_rows, w), lambda i: (plane, i, 0))
    return (a, (None, t_rows, w), lambda m, t: (plane, t, 0))


def _row_tile(rows, row_bytes, budget_bytes=2 * 1024 * 1024):
    t = rows
    while t * row_bytes > budget_bytes and t % 32 == 0:
        t //= 2
    return t


def _pair_add(name, layers, recv, place):
    L = len(layers)
    _, _, hr, c = layers[0].shape
    tr = _row_tile(hr, L * c * 4)

    def body(place_ref, *refs):
        r_ref, o_ref = refs[L], refs[L + 1]
        for l in range(L):
            o_ref[l] = (refs[l][...] + r_ref[l]).astype(BF16)

    stacked = pl.BlockSpec((None, L, tr, c), lambda q, i, pr: (q, 0, i, 0))
    grid_spec = pltpu.PrefetchScalarGridSpec(
        num_scalar_prefetch=1, grid=(N_CHIPS, hr // tr),
        in_specs=[pl.BlockSpec((None, None, tr, c), lambda q, i, pr: (q, pr[1], i, 0))] * L + [stacked],
        out_specs=stacked)
    return pl.pallas_call(body, name=name, grid_spec=grid_spec,
                          out_shape=jax.ShapeDtypeStruct((N_CHIPS, L, hr, c), BF16),
                          compiler_params=_params(2))(place, *layers, recv)


def _chip_add(name, part, slots, place):
    _, L, hr, c = part.shape
    tr = _row_tile(hr, L * c * 4)

    def body(place_ref, own, s1, s2, s3, o_ref):
        f = lambda r: r[...].astype(F32)
        o_ref[...] = ((f(own) + f(s1)) + f(s2)) + f(s3)

    def slot(k):
        return pl.BlockSpec((None, L, tr, c), lambda i, pr: ((pr[0] + k) % N_CHIPS, 0, i, 0))

    grid_spec = pltpu.PrefetchScalarGridSpec(
        num_scalar_prefetch=1, grid=(hr // tr,),
        in_specs=[slot(0), slot(1), slot(2), slot(3)],
        out_specs=pl.BlockSpec((L, None, tr, c), lambda i, pr: (0, pr[1], i, 0)))
    return pl.pallas_call(body, name=name, grid_spec=grid_spec,
                          out_shape=jax.ShapeDtypeStruct((L, 2, hr, c), F32),
                          compiler_params=_params(1))(place, part, slots, slots, slots)


def _adamw(name, w, g, m, v):
    rows, cols = w.shape
    tr = _row_tile(rows, cols * 4, 1024 * 1024)

    def body(w_ref, g_ref, m_ref, v_ref, d_ref, nm_ref, nv_ref):
        d_ref[...], nm_ref[...], nv_ref[...] = _adamw_math(w_ref[...], g_ref[...], m_ref[...], v_ref[...])

    spec = pl.BlockSpec((tr, cols), lambda i: (i, 0))
    return pl.pallas_call(
        body, name=name, grid=(rows // tr,), in_specs=[spec] * 4, out_specs=[spec] * 3,
        out_shape=[jax.ShapeDtypeStruct((rows, cols), F32)] * 3, compiler_params=_params(1))(w, g, m, v)


def _adamw_math(w, g, m, v):
    nm = ADAM_B1 * m + (1.0 - ADAM_B1) * g
    nv = ADAM_B2 * v + (1.0 - ADAM_B2) * jnp.square(g)
    m_hat = nm / (1.0 - ADAM_B1 ** ADAM_STEP)
    v_hat = nv / (1.0 - ADAM_B2 ** ADAM_STEP)
    return -ADAM_LR * (m_hat / (jnp.sqrt(v_hat) + ADAM_EPS) + ADAM_WD * w), nm, nv


def _adamw_layers(name, w, grads, m, v):
    L, r, c = w.shape
    tr = _row_tile(r, L * c * 4, 1024 * 1024)

    def body(*refs):
        w_ref, m_ref, v_ref = refs[:3]
        g_refs = refs[3:3 + L]
        go_ref, d_ref, nm_ref, nv_ref = refs[3 + L:]
        for l in range(L):
            g = g_refs[l][...]
            go_ref[l] = g
            d_ref[l], nm_ref[l], nv_ref[l] = _adamw_math(w_ref[l], g, m_ref[l], v_ref[l])

    stacked = pl.BlockSpec((L, tr, c), lambda i: (0, i, 0))
    return pl.pallas_call(
        body, name=name, grid=(r // tr,),
        in_specs=[stacked] * 3 + [pl.BlockSpec((tr, c), lambda i: (i, 0))] * L, out_specs=[stacked] * 4,
        out_shape=[jax.ShapeDtypeStruct((L, r, c), F32)] * 4, compiler_params=_params(1))(w, m, v, *grads)


def _place():
    x, y, c = lax.axis_index("x"), lax.axis_index("y"), lax.axis_index("c")
    chips = [(1 - x, y), (x, 1 - y), (1 - x, 1 - y)]
    return x, y, c, chips


HBM = pl.BlockSpec(memory_space=pltpu.HBM)
SEM = pl.BlockSpec(memory_space=pltpu.SEMAPHORE)
EFFECT = pltpu.SideEffectType.DATAFLOW_SIDE_EFFECTING


class _Copy:
    def __init__(self, src, src_view, land, dst_view, recv_view, target):
        self.src, self.src_view, self.land, self.dst_view, self.recv_view, self.target = (
            src, src_view, land, dst_view, recv_view, target)


def _whole(ref, place):
    return ref


def _split_start(name, srcs, land_shapes, plans):
    skeys, lkeys = list(srcs), list(land_shapes)
    ns, nl, ng = len(skeys), len(lkeys), len(plans)

    def body(*refs):
        src = dict(zip(skeys, refs[:ns]))
        land = dict(zip(lkeys, refs[ns:ns + nl]))
        sems = refs[ns + nl:ns + nl + 2 * ng]
        token = refs[-1]
        place = _place()
        for gi, plan in enumerate(plans):
            for k, cp in enumerate(plan):
                pltpu.make_async_remote_copy(
                    src_ref=cp.src_view(src[cp.src], place), dst_ref=cp.dst_view(land[cp.land], place),
                    send_sem=sems[2 * gi].at[k], recv_sem=sems[2 * gi + 1].at[k],
                    device_id=cp.target(place), device_id_type=MESH).start()
        token[...] = jnp.zeros_like(token)

    sem_shapes = []
    for plan in plans:
        sem_shapes += [pltpu.SemaphoreType.DMA((len(plan),))] * 2
    buffers = [srcs[k] for k in skeys] + [lax.empty(land_shapes[k].shape, land_shapes[k].dtype) for k in lkeys]
    outs = pl.pallas_call(
        body, name=name,
        out_shape=(*sem_shapes, *[pltpu.HBM(a.shape, a.dtype) for a in buffers], jax.ShapeDtypeStruct((8, LANES), F32)),
        in_specs=[HBM] * (ns + nl),
        out_specs=(*[SEM] * (2 * ng), *[HBM] * (ns + nl), pl.BlockSpec(memory_space=pltpu.VMEM)),
        input_output_aliases={i: 2 * ng + i for i in range(ns + nl)},
        compiler_params=pltpu.CompilerParams(has_side_effects=EFFECT),
    )(*[pltpu.with_memory_space_constraint(a, pltpu.HBM) for a in buffers])
    sems = [(outs[2 * gi], outs[2 * gi + 1]) for gi in range(ng)]
    thru = outs[2 * ng:2 * ng + ns + nl]
    return sems, dict(zip(skeys, thru[:ns])), dict(zip(lkeys, thru[ns:])), outs[-1]


def _split_wait(name, sems, srcs, lands, plan, after):
    skeys, lkeys = list(srcs), list(lands)
    ns, nl = len(skeys), len(lkeys)

    def body(*refs):
        src = dict(zip(skeys, refs[:ns]))
        land = dict(zip(lkeys, refs[ns:ns + nl]))
        ssem, rsem = refs[ns + nl], refs[ns + nl + 1]
        place = _place()
        for k, cp in enumerate(plan):
            pltpu.make_async_remote_copy(
                src_ref=cp.src_view(src[cp.src], place), dst_ref=cp.dst_view(land[cp.land], place),
                send_sem=ssem.at[k], recv_sem=rsem.at[k],
                device_id=cp.target(place), device_id_type=MESH).wait_send()
            got = cp.recv_view(land[cp.land], place)
            pltpu.make_async_remote_copy(
                src_ref=got, dst_ref=got, send_sem=ssem.at[k], recv_sem=rsem.at[k],
                device_id=cp.target(place), device_id_type=MESH).wait_recv()

    buffers = [srcs[k] for k in skeys] + [lands[k] for k in lkeys]
    outs = pl.pallas_call(
        body, name=name, out_shape=tuple(pltpu.HBM(a.shape, a.dtype) for a in buffers),
        in_specs=(*[HBM] * (ns + nl), SEM, SEM, ANY), out_specs=tuple([HBM] * (ns + nl)),
        input_output_aliases={i: i for i in range(ns + nl)},
        compiler_params=pltpu.CompilerParams(has_side_effects=EFFECT),
    )(*buffers, sems[0], sems[1], after)
    return dict(zip(skeys, outs[:ns])), dict(zip(lkeys, outs[ns:]))


def _chip_of(place):
    x, y, c, chips = place
    return 2 * x + y


class _WeightGather:
    def __init__(self, blocks):
        self.plans, shapes = {}, {}
        for key, a in blocks.items():
            shapes[key] = jax.ShapeDtypeStruct((N_CHIPS,) + a.shape, a.dtype)
            slot = lambda ref, place: ref.at[_chip_of(place)]
            plan = [_Copy(key, _whole, key, slot,
                          lambda ref, place, k=k: ref.at[2 * place[3][k][0] + place[3][k][1]],
                          lambda place, k=k: (place[3][k][0], place[3][k][1], place[2])) for k in range(3)]
            plan.append(_Copy(key, _whole, key, slot, slot, lambda place: (place[0], place[1], 1 - place[2])))
            self.plans[key] = plan
        sems, self.srcs, self.lands, self.token = _split_start("gather_start", blocks, shapes, list(self.plans.values()))
        self.sems = dict(zip(self.plans, sems))

    def get(self, l, name, after):
        key = (l, name)
        _, lands = _split_wait(f"gather_wait_{name}{l}", self.sems[key], {key: self.srcs[key]},
                               {key: self.lands[key]}, self.plans[key], after)
        return lands[key][:, None]


class _GradReduce:
    def __init__(self, place):
        self.place = place
        self.jobs = []
        self.done = {}
        self.n = 0

    def submit(self, grads):
        views = {k: a.reshape(N_CHIPS, 2, a.shape[1] // 2, a.shape[2]) for k, a in grads.items()}
        shapes = {k: jax.ShapeDtypeStruct((N_CHIPS,) + a.shape[2:], F32) for k, a in views.items()}
        sibling = lambda place: (place[0], place[1], 1 - place[2])
        plan = [_Copy(k, lambda ref, place: ref.at[:, 1 - place[2]], k, _whole, _whole, sibling) for k in views]
        sems, srcs, lands, token = _split_start(f"grad_pair_start{self.n}", views, shapes, [plan])
        self.jobs.append(dict(id=self.n, stage=1, sems=sems[0], srcs=srcs, lands=lands, plan=plan))
        self.n += 1
        return token

    def pump(self, after):
        tokens = []
        for job in list(self.jobs):
            srcs, lands = _split_wait(f"grad_wait{job['id']}_{job['stage']}", job["sems"], job["srcs"], job["lands"],
                                      job["plan"], after)
            if job["stage"] == 1:
                parts = {k: _pair_add(f"grad_pair_add{job['id']}_{i}", [srcs[k]], lands[k][:, None], self.place)
                         for i, k in enumerate(srcs)}
                shapes = {k: jax.ShapeDtypeStruct(a.shape, a.dtype) for k, a in parts.items()}
                plan = []
                for k in parts:
                    for j in range(3):
                        there = lambda ref, place, j=j: ref.at[2 * place[3][j][0] + place[3][j][1]]
                        plan.append(_Copy(k, there, k, lambda ref, place: ref.at[_chip_of(place)], there,
                                          lambda place, j=j: (place[3][j][0], place[3][j][1], place[2])))
                sems, srcs2, lands2, token = _split_start(f"grad_chip_start{job['id']}", parts, shapes, [plan])
                job.update(stage=2, sems=sems[0], srcs=srcs2, lands=lands2, plan=plan)
                tokens.append(token)
            else:
                for i, k in enumerate(srcs):
                    self.done[k] = _chip_add(f"grad_chip_add{job['id']}_{i}", srcs[k], lands[k], self.place)[0]
                self.jobs.remove(job)
        return tokens

    def finish(self, after):
        while self.jobs:
            self.pump(after)
        return self.done


def _pair_share(halves):
    n = len(halves)

    def body(*refs):
        outs = refs[n:2 * n]
        ssem, rsem = refs[2 * n:]
        x, y, c, _ = _place()
        sends = []
        for t in range(n):
            cp = pltpu.make_async_remote_copy(
                src_ref=outs[t].at[c], dst_ref=outs[t].at[c], send_sem=ssem.at[t], recv_sem=rsem.at[t],
                device_id=(x, y, 1 - c), device_id_type=MESH)
            cp.start()
            sends.append(cp)
        for t in range(n):
            theirs = outs[t].at[1 - c]
            pltpu.make_async_remote_copy(
                src_ref=theirs, dst_ref=theirs, send_sem=ssem.at[t], recv_sem=rsem.at[t],
                device_id=(x, y, 1 - c), device_id_type=MESH).wait_recv()
        for cp in sends:
            cp.wait_send()

    return pl.pallas_call(
        body, name="grad_pair_share", in_specs=[ANY] * n, out_specs=[ANY] * n,
        out_shape=[jax.ShapeDtypeStruct(a.shape, a.dtype) for a in halves],
        input_output_aliases={t: t for t in range(n)},
        scratch_shapes=[pltpu.SemaphoreType.DMA((n,)), pltpu.SemaphoreType.DMA((n,))])(*halves)


def _small_allreduce(part):
    R, C = part.shape
    N_DEV = 8

    def body(in_ref, out_ref, slots, ssem, rsem):
        x, y, c, _ = _place()
        me = 4 * x + 2 * y + c
        sends = []
        for k in range(1, N_DEV):
            kx, ky, kc = (k >> 2) & 1, (k >> 1) & 1, k & 1
            peer = (1 - x if kx else x, 1 - y if ky else y, 1 - c if kc else c)
            cp = pltpu.make_async_remote_copy(
                src_ref=in_ref, dst_ref=slots.at[me], send_sem=ssem.at[k], recv_sem=rsem.at[k],
                device_id=peer, device_id_type=MESH)
            cp.start()
            sends.append(cp)
        slots[me] = in_ref[...]
        for k in range(1, N_DEV):
            kx, ky, kc = (k >> 2) & 1, (k >> 1) & 1, k & 1
            peer = (1 - x if kx else x, 1 - y if ky else y, 1 - c if kc else c)
            slot = slots.at[4 * peer[0] + 2 * peer[1] + peer[2]]
            pltpu.make_async_remote_copy(
                src_ref=slot, dst_ref=slot, send_sem=ssem.at[k], recv_sem=rsem.at[k],
                device_id=peer, device_id_type=MESH).wait_recv()
        acc = slots[0]
        for d in range(1, N_DEV):
            acc = acc + slots[d]
        out_ref[...] = acc
        for cp in sends:
            cp.wait_send()

    vm = pl.BlockSpec(memory_space=pltpu.VMEM)
    return pl.pallas_call(
        body, name="small_allreduce", in_specs=[vm], out_specs=vm,
        out_shape=jax.ShapeDtypeStruct((R, C), F32),
        scratch_shapes=[pltpu.VMEM((N_DEV, R, C), F32), pltpu.SemaphoreType.DMA((N_DEV,)),
                        pltpu.SemaphoreType.DMA((N_DEV,))])(part)


def _local_step(x, target, norm_mix, norm_mlp, norm_kv, norm_final, weights, sink, n_a, n_heads):
    B, S, D = x.shape
    T = B * S
    C = n_heads * HEAD_DIM
    depth = norm_mix.shape[0]
    slopes = 2.0 ** (-ALIBI_MAX_BIAS * jnp.arange(1, n_heads + 1, dtype=F32) / n_heads)
    tm = min(512, T)
    row = lambda v: v.reshape(1, -1)

    h = x.reshape(T, D)
    saved, Wl = [], []
    kv = nkv = h_kv = cwg = None
    for l in range(depth):
        s = {"h_in": h}
        w = {}
        Wl.append(w)
        if l < n_a:
            w["w_a_in"] = weights.get(l, "w_a_in", h)
            s["n1"], bcu = _norm_mm(f"a_in_fwd{l}", h, row(norm_mix[l]), w["w_a_in"], 0, 3, BF16, tm)
            s["bcu"] = bcu.reshape(3, B, S, D)
            if l == 0:
                cwg = weights.get(0, "conv", bcu)[:, 0, :n_a * 3].reshape(N_CHIPS, n_a, 3, -1)
            s["z"] = _conv_fwd(f"conv_fwd{l}", s["bcu"], cwg, l, LANES).reshape(T, D)
            w["w_a_out"] = weights.get(l, "w_a_out", s["z"])
            h = _mm_res_rows(f"a_out_fwd{l}", s["z"], w["w_a_out"], 0, h, _to_bf16, tm)
        else:
            i = l - n_a
            if i == 0:
                h_kv = h
                w["w_kv"] = weights.get(l, "w_kv", h)
                nkv, kv = _norm_mm("kv_fwd", h, row(norm_kv), w["w_kv"], 0, 1, F32, tm)
                kv = kv.reshape(B, S, 2 * 3 * C)
            w["w_q"] = weights.get(l, "w_q", h)
            s["n1"], q = _norm_mm(f"q_fwd{i}", h, row(norm_mix[l]), w["w_q"], 0, 1, F32, tm)
            s["q"] = q.reshape(B, S, 3 * C)
            o, lse = _attn_fwd(f"attn_fwd{i}", s["q"], kv, slopes, n_heads)
            s["o"], s["lse"] = o.reshape(T, C), lse.reshape(T, C)
            w["w_o"] = weights.get(l, "w_o", o)
            h = _mm_res_cols(f"o_fwd{i}", s["o"], w["w_o"], 0, h, tm)
        s["h_mid"] = h
        w["w_up"] = weights.get(l, "w_up", h)
        s["n2"], a = _norm_mm(f"up_fwd{l}", h, row(norm_mlp[l]), w["w_up"], 0, 1, BF16, tm)
        F = a.shape[2]
        s["a"] = a.reshape(T, F)
        w["w_down"] = weights.get(l, "w_down", a)
        h = _mm_res_rows(f"down_fwd{l}", s["a"], w["w_down"], 0, h, _relu2_bf16, tm)
        saved.append(s)

    loss, dh, dh16, dg_final = _final_loss("loss_head", h, row(norm_final), target.reshape(T, D), tm)

    g_mix, g_mlp = [None] * depth, [None] * depth
    g_conv = [None] * n_a
    dkv = None
    tt = min(512, T)
    deps = []
    for l in reversed(range(depth)):
        s, w = saved[l], Wl[l]
        da = _nt_rows(f"down_bwd{l}", dh16, w["w_down"], 0, s["a"], BF16, tm, deps)
        g_down = _tn(f"down_wgrad{l}", s["a"], _relu2_bf16, [_seg2d(dh16, tt, 2)], None, False,
                     min(2048, F), tt).reshape(N_CHIPS, F // N_CHIPS, D)
        g_up = _tn(f"up_wgrad{l}", s["n2"], _to_bf16, [_seg2d(da, tt, 2)], F // N_CHIPS, True, D, tt)
        dh, dh16, g_mlp[l] = _nt_cols(f"up_bwd{l}", [_seg2d(da, tm, 1)], w["w_up"], 0, tm,
                                      (s["h_mid"], row(norm_mlp[l]), dh))
        deps = sink.pump(dh) + [sink.submit({("w_up", l): g_up, ("w_down", l): g_down})]
        if l < n_a:
            g_out = _tn(f"a_out_wgrad{l}", s["z"], _to_bf16, [_seg2d(dh16, tt, 2)], None, False,
                        D, tt, deps).reshape(N_CHIPS, D // N_CHIPS, D)
            dz = _nt_rows(f"a_out_bwd{l}", dh16, w["w_a_out"], 0, None, F32, tm)
            deps = sink.pump(dz) + [sink.submit({("w_a_out", l): g_out})]
            dbcu, g_conv[l] = _conv_bwd(f"conv_bwd{l}", s["bcu"], dz.reshape(B, S, D), cwg, l, LANES)
            dbcu = dbcu.reshape(3, T, D)
            g_in = _tn(f"a_in_wgrad{l}", s["n1"], _to_bf16, [_seg_plane(dbcu, p, tt, 2) for p in range(3)],
                       3 * D // N_CHIPS, True, min(512, D), tt, deps)
            dh, dh16, g_mix[l] = _nt_cols(f"a_in_bwd{l}", [_seg_plane(dbcu, p, tm, 1) for p in range(3)],
                                          w["w_a_in"], 0, tm, (s["h_in"], row(norm_mix[l]), dh))
            mixer = {("w_a_in", l): g_in}
        else:
            i = l - n_a
            g_o = _tn(f"o_wgrad{i}", s["o"], _to_bf16, [_seg2d(dh16, tt, 2)], D // N_CHIPS, True, C, tt, deps)
            do = _nt_cols(f"o_bwd{i}", [_seg2d(dh16, tm, 1)], w["w_o"], 0, tm, None)
            deps = sink.pump(do) + [sink.submit({("w_o", i): g_o})]
            dq, dk, dv = _attn_bwd(f"attn_bwd{i}", s["q"], kv, slopes, s["o"].reshape(B, S, C),
                                   s["lse"].reshape(B, S, C), do.reshape(B, S, C), n_heads, dkv)
            dkv = (dk, dv)
            dq = dq.reshape(T, 3 * C)
            g_q = _tn(f"q_wgrad{i}", s["n1"], _to_bf16, [_seg2d(dq, tt, 2)], 3 * C // N_CHIPS, True, min(512, D), tt,
                      deps)
            dh, dh16, g_mix[l] = _nt_cols(f"q_bwd{i}", [_seg2d(dq, tm, 1)], w["w_q"], 0, tm,
                                          (s["h_in"], row(norm_mix[l]), dh))
            mixer = {("w_q", i): g_q}
            if i == 0:
                dk2, dv2 = (t.reshape(T, 3 * C) for t in dkv)
                mixer[("w_kv", 0)] = _tn("kv_wgrad", nkv, _to_bf16, _kv_segments(dk2, dv2, C, tt, 2),
                                         6 * C // N_CHIPS, True, min(512, D), tt)
                dh, dh16, g_kv = _nt_cols("kv_bwd", _kv_segments(dk2, dv2, C, tm, 1), w["w_kv"], 0, tm,
                                          (h_kv, row(norm_kv), dh))
        deps = sink.pump(dh) + [sink.submit(mixer)]
    small = dict(norm_mix=jnp.concatenate(g_mix, axis=0), norm_mlp=jnp.concatenate(g_mlp, axis=0),
                 norm_kv=g_kv, norm_final=dg_final, conv_w=jnp.stack(g_conv))
    return loss, dh.reshape(B, S, D), small


BIG = ("w_a_in", "w_a_out", "w_kv", "w_q", "w_o", "w_up", "w_down")
CONV_PAD_ROWS = 16


def kernel(x, norm_mix, norm_mlp, w_a_in, conv_w, w_a_out, norm_kv, w_kv, w_q, w_o, w_up, w_down, norm_final, loss_target, m_norm_mix, m_norm_mlp, m_w_a_in, m_conv_w, m_w_a_out, m_norm_kv, m_w_kv, m_w_q, m_w_o, m_w_up, m_w_down, m_norm_final, v_norm_mix, v_norm_mlp, v_w_a_in, v_conv_w, v_w_a_out, v_norm_kv, v_w_kv, v_w_q, v_w_o, v_w_up, v_w_down, v_norm_final):
    D = x.shape[-1]
    w = dict(norm_mix=norm_mix, norm_mlp=norm_mlp, w_a_in=w_a_in, conv_w=conv_w, w_a_out=w_a_out, norm_kv=norm_kv,
             w_kv=w_kv[None], w_q=w_q, w_o=w_o, w_up=w_up, w_down=w_down, norm_final=norm_final)
    m = dict(norm_mix=m_norm_mix, norm_mlp=m_norm_mlp, w_a_in=m_w_a_in, conv_w=m_conv_w, w_a_out=m_w_a_out,
             norm_kv=m_norm_kv, w_kv=m_w_kv[None], w_q=m_w_q, w_o=m_w_o, w_up=m_w_up, w_down=m_w_down,
             norm_final=m_norm_final)
    v = dict(norm_mix=v_norm_mix, norm_mlp=v_norm_mlp, w_a_in=v_w_a_in, conv_w=v_conv_w, w_a_out=v_w_a_out,
             norm_kv=v_norm_kv, w_kv=v_w_kv[None], w_q=v_w_q, w_o=v_w_o, w_up=v_w_up, w_down=v_w_down,
             norm_final=v_norm_final)
    depth = norm_mix.shape[0]
    n_a, taps, cwc = conv_w.shape
    n_heads = w_o.shape[1] // HEAD_DIM

    conv_rows = jnp.zeros((CONV_PAD_ROWS, cwc), F32).at[:n_a * taps].set(conv_w.reshape(n_a * taps, cwc))
    blocks = {}
    for l in range(depth):
        if l < n_a:
            blocks[(l, "w_a_in")] = w_a_in[l].astype(BF16)
            if l == 0:
                blocks[(0, "conv")] = conv_rows
            blocks[(l, "w_a_out")] = w_a_out[l].astype(BF16)
        else:
            if l == n_a:
                blocks[(l, "w_kv")] = w_kv.astype(BF16)
            blocks[(l, "w_q")] = w_q[l - n_a].astype(BF16)
            blocks[(l, "w_o")] = w_o[l - n_a].astype(BF16)
        blocks[(l, "w_up")] = w_up[l].astype(BF16)
        blocks[(l, "w_down")] = w_down[l].astype(BF16)
    weights = _WeightGather(blocks)
    place = jnp.stack([2 * lax.axis_index("x") + lax.axis_index("y"), lax.axis_index("c")]).astype(jnp.int32)
    sink = _GradReduce(place)

    loss, grad_x, small = _local_step(x, loss_target, norm_mix, norm_mlp, norm_kv, norm_final, weights, sink,
                                      n_a, n_heads)
    loss = lax.psum(loss[0, 0], ("x", "y", "c"))

    done = sink.finish(grad_x)
    keys = list(done)
    shared = dict(zip(keys, _pair_share([done[k] for k in keys])))
    grads = {}

    packed = jnp.concatenate([small["norm_mix"], small["norm_mlp"], small["norm_kv"], small["norm_final"],
                              small["conv_w"].reshape(n_a * taps, D)], axis=0)
    pad = (-packed.shape[0]) % 8
    packed = jnp.pad(packed, ((0, pad), (0, 0)))
    total = _small_allreduce(packed)
    grads["norm_mix"] = total[:depth]
    grads["norm_mlp"] = total[depth:2 * depth]
    grads["norm_kv"] = total[2 * depth]
    grads["norm_final"] = total[2 * depth + 1]
    chip = 2 * lax.axis_index("x") + lax.axis_index("y")
    conv_full = total[2 * depth + 2:2 * depth + 2 + n_a * taps].reshape(n_a, taps, N_CHIPS, cwc)
    grads["conv_w"] = lax.dynamic_index_in_dim(conv_full, chip, axis=2, keepdims=False)

    order = ("norm_mix", "norm_mlp", "w_a_in", "conv_w", "w_a_out", "norm_kv", "w_kv", "w_q", "w_o", "w_up",
             "w_down", "norm_final")
    delta, new_m, new_v = {}, {}, {}
    vec_names = ("norm_mix", "norm_mlp", "norm_kv", "norm_final")
    rows_of = lambda a: a.reshape(-1, D)
    vw, vg, vm_, vv = (jnp.concatenate([rows_of(t[k]) for k in vec_names], axis=0) for t in (w, grads, m, v))
    vpad = (-vw.shape[0]) % 8
    padrows = lambda a: jnp.pad(a, ((0, vpad), (0, 0)))
    vd, vnm, vnv = _adamw("adamw_norms", padrows(vw), padrows(vg), padrows(vm_), padrows(vv))
    off = 0
    for k in vec_names:
        r = rows_of(w[k]).shape[0]
        delta[k] = vd[off:off + r].reshape(w[k].shape)
        new_m[k] = vnm[off:off + r].reshape(w[k].shape)
        new_v[k] = vnv[off:off + r].reshape(w[k].shape)
        off += r
    cpad = (-n_a * taps) % 8
    two_d = lambda a: jnp.pad(a.reshape(-1, cwc), ((0, cpad), (0, 0)))
    cd, cnm, cnv = _adamw("adamw_conv_w", two_d(w["conv_w"]), two_d(grads["conv_w"]), two_d(m["conv_w"]),
                          two_d(v["conv_w"]))
    delta["conv_w"], new_m["conv_w"], new_v["conv_w"] = (t[:n_a * taps].reshape(conv_w.shape) for t in (cd, cnm, cnv))
    for k in BIG:
        per_layer = [shared[(k, l)].reshape(w[k].shape[1:]) for l in range(w[k].shape[0])]
        grads[k], delta[k], new_m[k], new_v[k] = _adamw_layers(f"adamw_{k}", w[k], per_layer, m[k], v[k])
    fix = lambda k, a: a[0] if k == "w_kv" else a
    return (loss, grad_x, *[fix(k, grads[k]) for k in order], *[fix(k, delta[k]) for k in order],
            *[fix(k, new_m[k]) for k in order], *[fix(k, new_v[k]) for k in order])
```

```python
import functools

import jax
import jax.numpy as jnp
from jax import lax
from jax.experimental import pallas as pl
from jax.experimental.pallas import tpu as pltpu

F32 = jnp.float32
BF16 = jnp.bfloat16
MESH = pl.DeviceIdType.MESH

EPS = 1e-5
PATTERNS = ((128, 1), (512, 4), (2048, 16))
HEAD_DIM = 64
ALIBI_MAX_BIAS = 8.0
NEG_INF = -1e30
ATT_BLK = 128
BWD_UNROLL = 8
N_CHIPS = 4
LANES = 128
VMEM_LIMIT = 56 * 1024 * 1024

ADAM_LR = 0.001
ADAM_B1 = 0.9
ADAM_B2 = 0.999
ADAM_EPS = 1e-08
ADAM_WD = 0.01
ADAM_STEP = 10


ANY = pl.BlockSpec(memory_space=pl.ANY)


def _params(n_grid_axes):
    return pltpu.CompilerParams(dimension_semantics=("arbitrary",) * n_grid_axes, vmem_limit_bytes=VMEM_LIMIT)


def _dot(a, b):
    return jnp.dot(a, b, preferred_element_type=F32)


def _dot_nt(a, b):
    return lax.dot_general(a, b, (((1,), (1,)), ((), ())), preferred_element_type=F32)


def _dot_tn(a, b):
    return lax.dot_general(a, b, (((0,), (0,)), ((), ())), preferred_element_type=F32)


def _relu2(a):
    return jnp.square(jnp.maximum(a, 0.0))


def _rms(hf, g):
    y = hf * lax.rsqrt(jnp.mean(hf * hf, axis=-1, keepdims=True) + EPS)
    return y * g


def _rms_bwd(hf, g, dn):
    rstd = lax.rsqrt(jnp.mean(hf * hf, axis=-1, keepdims=True) + EPS)
    xhat = hf * rstd
    dg = jnp.sum(dn * xhat, axis=0, keepdims=True)
    dx = dn * g
    dh = rstd * (dx - xhat * jnp.mean(dx * xhat, axis=-1, keepdims=True))
    return dh, dg


def _pieces(seg_widths, chunk_width, max_width):
    total = sum(seg_widths)
    cuts = {0, total}
    acc = 0
    for w in seg_widths:
        cuts.add(acc)
        acc += w
    cuts.update(range(0, total, chunk_width))
    cuts = sorted(cuts)
    fine = []
    for lo, hi in zip(cuts[:-1], cuts[1:]):
        while hi - lo > max_width:
            fine.append((lo, lo + max_width))
            lo += max_width
        fine.append((lo, hi))
    out = []
    for lo, hi in fine:
        acc = 0
        for s, w in enumerate(seg_widths):
            if lo < acc + w:
                break
            acc += w
        out.append((s, lo - acc, lo // chunk_width, lo % chunk_width, hi - lo))
    return out


def _relu2_bf16(a):
    return _relu2(a.astype(F32)).astype(BF16)


def _to_bf16(a):
    return a.astype(BF16)


def _norm_mm(name, h, g, wg, layer, planes, out_dtype, tm):
    T, D = h.shape
    cw = wg.shape[3]
    N = N_CHIPS * cw
    pw = N // planes
    pieces = _pieces([pw] * planes, cw, 512)

    def body(h_ref, g_ref, w_ref, n_ref, o_ref):
        n = _rms(h_ref[...], g_ref[...]).astype(BF16)
        n_ref[...] = n
        for s, a0, ch, b0, wd in pieces:
            o_ref[s, :, a0:a0 + wd] = _dot(n, w_ref[ch, :, b0:b0 + wd]).astype(out_dtype)

    return pl.pallas_call(
        body, name=name, grid=(T // tm,),
        in_specs=[pl.BlockSpec((tm, D), lambda i: (i, 0)),
                  pl.BlockSpec((1, D), lambda i: (0, 0)),
                  pl.BlockSpec((N_CHIPS, None, D, cw), lambda i: (0, layer, 0, 0))],
        out_specs=[pl.BlockSpec((tm, D), lambda i: (i, 0)),
                   pl.BlockSpec((planes, tm, pw), lambda i: (0, i, 0))],
        out_shape=[jax.ShapeDtypeStruct((T, D), BF16), jax.ShapeDtypeStruct((planes, T, pw), out_dtype)],
        compiler_params=_params(1))(h, g, wg)


def _mm_res_rows(name, a, wg, layer, h, act, tm):
    T = a.shape[0]
    rk, D = wg.shape[2], wg.shape[3]

    def body(a_ref, w_ref, h_ref, o_ref):
        acc = h_ref[...]
        for k in range(N_CHIPS):
            acc = acc + _dot(act(a_ref[:, k * rk:(k + 1) * rk]), w_ref[k])
        o_ref[...] = acc

    return pl.pallas_call(
        body, name=name, grid=(T // tm,),
        in_specs=[pl.BlockSpec((tm, N_CHIPS * rk), lambda i: (i, 0)),
                  pl.BlockSpec((N_CHIPS, None, rk, D), lambda i: (0, layer, 0, 0)),
                  pl.BlockSpec((tm, D), lambda i: (i, 0))],
        out_specs=pl.BlockSpec((tm, D), lambda i: (i, 0)),
        out_shape=jax.ShapeDtypeStruct((T, D), F32),
        compiler_params=_params(1))(a, wg, h)


def _mm_res_cols(name, a, wg, layer, h, tm):
    T, K = a.shape
    cw = wg.shape[3]
    D = N_CHIPS * cw

    def body(a_ref, w_ref, h_ref, o_ref):
        a16 = a_ref[...].astype(BF16)
        for j in range(N_CHIPS):
            o_ref[:, j * cw:(j + 1) * cw] = h_ref[:, j * cw:(j + 1) * cw] + _dot(a16, w_ref[j])

    return pl.pallas_call(
        body, name=name, grid=(T // tm,),
        in_specs=[pl.BlockSpec((tm, K), lambda i: (i, 0)),
                  pl.BlockSpec((N_CHIPS, None, K, cw), lambda i: (0, layer, 0, 0)),
                  pl.BlockSpec((tm, D), lambda i: (i, 0))],
        out_specs=pl.BlockSpec((tm, D), lambda i: (i, 0)),
        out_shape=jax.ShapeDtypeStruct((T, D), F32),
        compiler_params=_params(1))(a, wg, h)


def _resident(shape, index_map):
    return pl.BlockSpec(shape, index_map, pipeline_mode=pl.Buffered(1))


def _mlp_fwd(name, h, g, wup, wdown, tm):
    T, D = h.shape
    cw = wup.shape[3]

    def body(h_ref, g_ref, wu_ref, wd_ref, n_ref, a_ref, o_ref):
        hf = h_ref[...]
        n = _rms(hf, g_ref[...]).astype(BF16)
        n_ref[...] = n
        acc = hf
        for ch in range(N_CHIPS):
            a16 = _dot(n, wu_ref[ch]).astype(BF16)
            a_ref[:, ch * cw:(ch + 1) * cw] = a16
            acc = acc + _dot(_relu2_bf16(a16), wd_ref[ch])
        o_ref[...] = acc

    row = pl.BlockSpec((tm, D), lambda i: (i, 0))
    return pl.pallas_call(
        body, name=name, grid=(T // tm,),
        in_specs=[row, pl.BlockSpec((1, D), lambda i: (0, 0)),
                  _resident((N_CHIPS, None, D, cw), lambda i: (0, 0, 0, 0)),
                  _resident((N_CHIPS, None, cw, D), lambda i: (0, 0, 0, 0))],
        out_specs=[row, pl.BlockSpec((tm, N_CHIPS * cw), lambda i: (i, 0)), row],
        out_shape=[jax.ShapeDtypeStruct((T, D), BF16), jax.ShapeDtypeStruct((T, N_CHIPS * cw), BF16),
                   jax.ShapeDtypeStruct((T, D), F32)],
        compiler_params=_params(1))(h, g, wup, wdown)


def _mlp_bwd(name, dh, dh16, a, wdown, wup, h_mid, g, tm, deps=()):
    T, D = dh.shape
    cw = wup.shape[3]
    F = N_CHIPS * cw

    def body(dh_ref, dh16_ref, a_ref, wd_ref, wu_ref, h_ref, g_ref, *rest):
        da_ref, out_ref, out16_ref, dg_ref = rest[len(deps):]
        d16 = dh16_ref[...]
        acc = None
        for ch in range(N_CHIPS):
            cols = slice(ch * cw, (ch + 1) * cw)
            da = (_dot_nt(d16, wd_ref[ch]) * (2.0 * jnp.maximum(a_ref[:, cols].astype(F32), 0.0))).astype(BF16)
            da_ref[:, cols] = da
            d = _dot_nt(da, wu_ref[ch])
            acc = d if acc is None else acc + d
        dh_c, dg = _rms_bwd(h_ref[...], g_ref[...], acc)
        out = dh_ref[...] + dh_c
        out_ref[...] = out
        out16_ref[...] = out.astype(BF16)

        @pl.when(pl.program_id(0) == 0)
        def _():
            dg_ref[...] = dg

        @pl.when(pl.program_id(0) > 0)
        def _():
            dg_ref[...] += dg

    row = pl.BlockSpec((tm, D), lambda i: (i, 0))
    wide = pl.BlockSpec((tm, F), lambda i: (i, 0))
    vec = pl.BlockSpec((1, D), lambda i: (0, 0))
    return pl.pallas_call(
        body, name=name, grid=(T // tm,),
        in_specs=[row, row, wide, _resident((N_CHIPS, None, cw, D), lambda i: (0, 0, 0, 0)),
                  _resident((N_CHIPS, None, D, cw), lambda i: (0, 0, 0, 0)), row, vec] + [ANY] * len(deps),
        out_specs=[wide, row, row, vec],
        out_shape=[jax.ShapeDtypeStruct((T, F), BF16), jax.ShapeDtypeStruct((T, D), F32),
                   jax.ShapeDtypeStruct((T, D), BF16), jax.ShapeDtypeStruct((1, D), F32)],
        compiler_params=_params(1))(dh, dh16, a, wdown, wup, h_mid, g, *deps)


CONV_ROWS = 256
CONV_HALO = 16


def _conv_shifted(ext, k, r0, rows):
    rolled = pltpu.roll(ext, k, 0)[CONV_HALO:]
    t = r0 + lax.broadcasted_iota(jnp.int32, rolled.shape, 0)
    return jnp.where(t >= k, rolled, 0.0)


def _conv_ahead(ext, k, r0, rows, S):
    rolled = pltpu.roll(ext, rows + CONV_HALO - k, 0)[:rows]
    t = r0 + lax.broadcasted_iota(jnp.int32, rolled.shape, 0)
    return jnp.where(t + k < S, rolled, 0.0)


def _conv_fwd(name, bcu, cwg, layer, tc):
    _, B, S, D = bcu.shape
    cwc = cwg.shape[3]
    per_chunk = cwc // tc
    R = min(CONV_ROWS, S)

    def body(x_ref, w_ref, z_ref):
        w = [w_ref[k:k + 1, :] for k in range(3)]

        def step(i, carry):
            r0 = pl.multiple_of(i * R, R)
            h0 = pl.multiple_of(jnp.maximum(r0 - CONV_HALO, 0), CONV_HALO)
            ld = lambda p, start, rows: x_ref[p, pl.ds(start, rows), :].astype(F32)
            cu = jnp.concatenate([ld(1, h0, CONV_HALO) * ld(2, h0, CONV_HALO), ld(1, r0, R) * ld(2, r0, R)], axis=0)
            conv = w[0] * cu[CONV_HALO:]
            conv = conv + w[1] * _conv_shifted(cu, 1, r0, R)
            conv = conv + w[2] * _conv_shifted(cu, 2, r0, R)
            z_ref[pl.ds(r0, R), :] = (ld(0, r0, R) * conv).astype(BF16)
            return carry

        lax.fori_loop(0, S // R, step, 0)

    return pl.pallas_call(
        body, name=name, grid=(B, D // tc),
        in_specs=[pl.BlockSpec((3, None, S, tc), lambda b, j: (0, b, 0, j)),
                  pl.BlockSpec((None, None, 3, tc), lambda b, j: (j // per_chunk, layer, 0, j % per_chunk))],
        out_specs=pl.BlockSpec((None, S, tc), lambda b, j: (b, 0, j)),
        out_shape=jax.ShapeDtypeStruct((B, S, D), BF16),
        compiler_params=_params(2))(bcu, cwg)


def _conv_bwd(name, bcu, dz, cwg, layer, tc):
    _, B, S, D = bcu.shape
    cwc = cwg.shape[3]
    per_chunk = cwc // tc
    R = min(CONV_ROWS, S)

    def body(x_ref, dz_ref, w_ref, d_ref, dw_ref):
        w = [w_ref[k:k + 1, :] for k in range(3)]

        @pl.when(pl.program_id(1) == 0)
        def _():
            dw_ref[...] = jnp.zeros_like(dw_ref)

        def step(i, carry):
            r0 = pl.multiple_of(i * R, R)
            h0 = pl.multiple_of(jnp.maximum(r0 - CONV_HALO, 0), CONV_HALO)
            a0 = pl.multiple_of(jnp.minimum(r0 + R, S - CONV_HALO), CONV_HALO)
            ld = lambda p, start, rows: x_ref[p, pl.ds(start, rows), :].astype(F32)
            b, c, u = ld(0, r0, R), ld(1, r0, R), ld(2, r0, R)
            dz = dz_ref[pl.ds(r0, R), :]
            cu = jnp.concatenate([ld(1, h0, CONV_HALO) * ld(2, h0, CONV_HALO), c * u], axis=0)
            cu1 = _conv_shifted(cu, 1, r0, R)
            cu2 = _conv_shifted(cu, 2, r0, R)
            conv = w[0] * (c * u) + w[1] * cu1 + w[2] * cu2
            dconv = dz * b
            dca = jnp.concatenate([dconv, dz_ref[pl.ds(a0, CONV_HALO), :] * ld(0, a0, CONV_HALO)], axis=0)
            dcu = w[0] * dconv + w[1] * _conv_ahead(dca, 1, r0, R, S) + w[2] * _conv_ahead(dca, 2, r0, R, S)
            d_ref[0, pl.ds(r0, R), :] = (dz * conv).astype(BF16)
            d_ref[1, pl.ds(r0, R), :] = (dcu * u).astype(BF16)
            d_ref[2, pl.ds(r0, R), :] = (dcu * c).astype(BF16)
            return (carry[0] + jnp.sum(dconv * (c * u), axis=0, keepdims=True),
                    carry[1] + jnp.sum(dconv * cu1, axis=0, keepdims=True),
                    carry[2] + jnp.sum(dconv * cu2, axis=0, keepdims=True))

        zero = jnp.zeros((1, tc), F32)
        s0, s1, s2 = lax.fori_loop(0, S // R, step, (zero, zero, zero))
        for k, sk in enumerate((s0, s1, s2)):
            dw_ref[k:k + 1, :] += sk

    return pl.pallas_call(
        body, name=name, grid=(D // tc, B),
        in_specs=[pl.BlockSpec((3, None, S, tc), lambda j, b: (0, b, 0, j)),
                  pl.BlockSpec((None, S, tc), lambda j, b: (b, 0, j)),
                  pl.BlockSpec((None, None, 3, tc), lambda j, b: (j // per_chunk, layer, 0, j % per_chunk))],
        out_specs=[pl.BlockSpec((3, None, S, tc), lambda j, b: (0, b, 0, j)),
                   pl.BlockSpec((3, tc), lambda j, b: (0, j))],
        out_shape=[jax.ShapeDtypeStruct((3, B, S, D), BF16), jax.ShapeDtypeStruct((3, D), F32)],
        compiler_params=_params(2))(bcu, dz, cwg)


def _att_rows(dil, idx, nb):
    r, n = idx // nb, idx % nb
    if dil == 1:
        cur = pl.ds(pl.multiple_of(n * ATT_BLK, ATT_BLK), ATT_BLK)
        prev = pl.ds(pl.multiple_of(jnp.maximum(n - 1, 0) * ATT_BLK, ATT_BLK), ATT_BLK)
    else:
        cur = pl.ds(n * (ATT_BLK * dil) + r, ATT_BLK, stride=dil)
        prev = pl.ds(jnp.maximum(n - 1, 0) * (ATT_BLK * dil) + r, ATT_BLK, stride=dil)
    return n, cur, prev


def _att_bias(bias_ref, dil, sl_ref, hp):
    row = lax.broadcasted_iota(jnp.int32, (2 * ATT_BLK, 2 * ATT_BLK), 0)
    ci = lax.broadcasted_iota(jnp.int32, (2 * ATT_BLK, 2 * ATT_BLK), 1)
    j = ATT_BLK + (row & (ATT_BLK - 1)) - ci
    slope = jnp.where(row < ATT_BLK, sl_ref[2 * hp], sl_ref[2 * hp + 1])
    rest = jnp.where((j >= 0) & (j <= ATT_BLK), -slope * (dil * j).astype(F32), NEG_INF)
    bias_ref[1] = rest
    bias_ref[0] = jnp.where(ci >= ATT_BLK, rest, NEG_INF)


def _stack_heads(x16, lane):
    first = lane < HEAD_DIM
    return jnp.concatenate([jnp.where(first, x16, jnp.zeros_like(x16)),
                            jnp.where(first, jnp.zeros_like(x16), x16)], axis=0)


def _per_head(col, lane):
    return jnp.where(lane < HEAD_DIM, col[:ATT_BLK], col[ATT_BLK:])


def _attn_fwd(name, q, kv, slopes, n_heads):
    B, S, CQ = q.shape
    HP = n_heads * HEAD_DIM // LANES
    scale = HEAD_DIM ** -0.5
    n_groups = len(PATTERNS)
    CH = 256

    def body(sl_ref, q_ref, k_ref, v_ref, o_ref, lse_ref, bias_ref, *parts):
        og, lg = parts[:n_groups], parts[n_groups:]
        hp, g = pl.program_id(1), pl.program_id(2)
        lane = lax.broadcasted_iota(jnp.int32, (1, LANES), 1)

        for gi, (window, dil) in enumerate(PATTERNS):
            nb = S // dil // ATT_BLK

            @pl.when(g == gi)
            def _(gi=gi, dil=dil, nb=nb):
                _att_bias(bias_ref, dil, sl_ref, hp)

                def step(idx, carry):
                    n, cur, prev = _att_rows(dil, idx, nb)
                    qs = _stack_heads((q_ref[cur, :] * scale).astype(BF16), lane)
                    kc = jnp.concatenate([k_ref[prev, :], k_ref[cur, :]], axis=0).astype(BF16)
                    vc = jnp.concatenate([v_ref[prev, :], v_ref[cur, :]], axis=0).astype(BF16)
                    s = _dot_nt(qs, kc) + bias_ref[jnp.minimum(n, 1)]
                    m = jnp.max(s, axis=-1, keepdims=True)
                    p = jnp.exp(s - m)
                    l = jnp.sum(p, axis=-1, keepdims=True)
                    p16 = p.astype(BF16)
                    o_un = _dot(jnp.concatenate([p16[:ATT_BLK], p16[ATT_BLK:]], axis=1), _stack_heads_rows(vc, lane))
                    og[gi][cur, :] = o_un / _per_head(l, lane)
                    lg[gi][cur, :] = _per_head(m + jnp.log(l), lane)
                    return carry

                lax.fori_loop(0, S // ATT_BLK, step, 0, unroll=8)

        @pl.when(g == n_groups - 1)
        def _():
            def comb(i, carry):
                rows = pl.ds(pl.multiple_of(i * CH, CH), CH)
                a, b, c = lg[0][rows, :], lg[1][rows, :], lg[2][rows, :]
                m = jnp.maximum(jnp.maximum(a, b), c)
                ea, eb, ec = jnp.exp(a - m), jnp.exp(b - m), jnp.exp(c - m)
                z = ea + eb + ec
                o_ref[rows, :] = (ea / z) * og[0][rows, :] + (eb / z) * og[1][rows, :] + (ec / z) * og[2][rows, :]
                lse_ref[rows, :] = m + jnp.log(z)
                return carry

            lax.fori_loop(0, S // CH, comb, 0)

    blk = (None, S, LANES)
    out = pl.BlockSpec(blk, lambda b, hp, g: (b, 0, hp))
    return pl.pallas_call(
        body, name=name, grid=(B, HP, n_groups),
        in_specs=[pl.BlockSpec(memory_space=pltpu.SMEM),
                  pl.BlockSpec(blk, lambda b, hp, g: (b, 0, g * HP + hp)),
                  pl.BlockSpec(blk, lambda b, hp, g: (b, 0, g * 2 * HP + hp)),
                  pl.BlockSpec(blk, lambda b, hp, g: (b, 0, g * 2 * HP + HP + hp))],
        out_specs=[out, out],
        out_shape=[jax.ShapeDtypeStruct((B, S, HP * LANES), F32)] * 2,
        scratch_shapes=[pltpu.VMEM((2, 2 * ATT_BLK, 2 * ATT_BLK), F32)] + [pltpu.VMEM((S, LANES), F32)] * (2 * n_groups),
        compiler_params=_params(3))(slopes, q, kv, kv)


def _stack_heads_rows(x16, lane):
    first = lane < HEAD_DIM
    return jnp.concatenate([jnp.where(first, x16, jnp.zeros_like(x16)),
                            jnp.where(first, jnp.zeros_like(x16), x16)], axis=0)


def _attn_bwd(name, q, kv, slopes, o, lse, do, n_heads, dkv_prev):
    B, S, CQ = q.shape
    HP = n_heads * HEAD_DIM // LANES
    scale = HEAD_DIM ** -0.5
    n_groups = len(PATTERNS)
    n_prev = 0 if dkv_prev is None else 2

    def body(sl_ref, q_ref, k_ref, v_ref, o_ref, lse_ref, do_ref, *rest):
        dq_ref, dk_ref, dv_ref, bias_ref = rest[n_prev:]
        hp, g = pl.program_id(1), pl.program_id(2)
        lane = lax.broadcasted_iota(jnp.int32, (1, LANES), 1)
        first = lane < HEAD_DIM

        def flush(rows, dk, dv):
            if n_prev:
                dk = dk + rest[0][rows, :]
                dv = dv + rest[1][rows, :]
            dk_ref[rows, :] = dk
            dv_ref[rows, :] = dv

        for gi, (window, dil) in enumerate(PATTERNS):
            nb = S // dil // ATT_BLK
            n_blocks = S // ATT_BLK

            @pl.when(g == gi)
            def _(dil=dil, nb=nb, n_blocks=n_blocks):
                _att_bias(bias_ref, dil, sl_ref, hp)

                def block(idx, carry, first_of_all):
                    n, cur, prev = _att_rows(dil, idx, nb)
                    qs = _stack_heads((q_ref[cur, :] * scale).astype(BF16), lane)
                    kc = jnp.concatenate([k_ref[prev, :], k_ref[cur, :]], axis=0).astype(BF16)
                    vc = jnp.concatenate([v_ref[prev, :], v_ref[cur, :]], axis=0).astype(BF16)
                    dob = do_ref[cur, :]
                    prod = dob * o_ref[cur, :]
                    lseb = lse_ref[cur, :]
                    dos = _stack_heads(dob.astype(BF16), lane)
                    delta = jnp.concatenate(
                        [jnp.sum(jnp.where(first, prod, 0.0), axis=-1, keepdims=True),
                         jnp.sum(jnp.where(first, 0.0, prod), axis=-1, keepdims=True)], axis=0)
                    lse_col = jnp.concatenate(
                        [jnp.max(jnp.where(first, lseb, -jnp.inf), axis=-1, keepdims=True),
                         jnp.max(jnp.where(first, -jnp.inf, lseb), axis=-1, keepdims=True)], axis=0)
                    s = _dot_nt(qs, kc) + bias_ref[jnp.minimum(n, 1)]
                    p = jnp.exp(s - lse_col)
                    ds = p * (_dot_nt(dos, vc) - delta)
                    ds16 = ds.astype(BF16)
                    dq = _dot(jnp.concatenate([ds16[:ATT_BLK], ds16[ATT_BLK:]], axis=1), _stack_heads_rows(kc, lane))
                    dq_ref[cur, :] = dq * scale
                    dk = _dot_tn(ds16, qs)
                    dv = _dot_tn(p.astype(BF16), dos)

                    def flush_before():
                        _, before, _ = _att_rows(dil, idx - 1, nb)
                        flush(before, carry[0] + dk[:ATT_BLK], carry[1] + dv[:ATT_BLK])

                    if first_of_all:
                        pl.when(idx > 0)(flush_before)
                    else:
                        flush_before()
                    return dk[ATT_BLK:], dv[ATT_BLK:]

                def step(i, carry):
                    for u in range(BWD_UNROLL):
                        carry = block(i * BWD_UNROLL + u, carry, u == 0)
                    return carry

                zero = jnp.zeros((ATT_BLK, LANES), F32)
                dk_last, dv_last = lax.fori_loop(0, n_blocks // BWD_UNROLL, step, (zero, zero))
                _, last, _ = _att_rows(dil, n_blocks - 1, nb)
                flush(last, dk_last, dv_last)

    blk = (None, S, LANES)
    shared = pl.BlockSpec(blk, lambda b, hp, g: (b, 0, hp))
    grouped = pl.BlockSpec(blk, lambda b, hp, g: (b, 0, g * HP + hp))
    prev = [] if dkv_prev is None else list(dkv_prev)
    gshape = jax.ShapeDtypeStruct((B, S, n_groups * HP * LANES), F32)
    return pl.pallas_call(
        body, name=name, grid=(B, HP, n_groups),
        in_specs=[pl.BlockSpec(memory_space=pltpu.SMEM), grouped,
                  pl.BlockSpec(blk, lambda b, hp, g: (b, 0, g * 2 * HP + hp)),
                  pl.BlockSpec(blk, lambda b, hp, g: (b, 0, g * 2 * HP + HP + hp)),
                  shared, shared, shared] + [grouped] * n_prev,
        out_specs=[grouped] * 3, out_shape=[gshape] * 3,
        scratch_shapes=[pltpu.VMEM((2, 2 * ATT_BLK, 2 * ATT_BLK), F32)],
        compiler_params=_params(3))(slopes, q, kv, kv, o, lse, do, *prev)


def _final_loss(name, h, g, target, tm):
    T, D = h.shape

    def body(h_ref, g_ref, t_ref, loss_ref, dh_ref, dh16_ref, dg_ref):
        hf = h_ref[...]
        gv = g_ref[...]
        rstd = lax.rsqrt(jnp.mean(hf * hf, axis=-1, keepdims=True) + EPS)
        xhat = hf * rstd
        err = xhat * gv - t_ref[...]
        part = 0.5 * jnp.sum(jnp.mean(err * err, axis=-1, keepdims=True), axis=0, keepdims=True)
        dy = err * (1.0 / D)
        dg = jnp.sum(dy * xhat, axis=0, keepdims=True)
        dx = dy * gv
        dh = rstd * (dx - xhat * jnp.mean(dx * xhat, axis=-1, keepdims=True))
        dh_ref[...] = dh
        dh16_ref[...] = dh.astype(BF16)

        @pl.when(pl.program_id(0) == 0)
        def _():
            loss_ref[...] = part
            dg_ref[...] = dg

        @pl.when(pl.program_id(0) > 0)
        def _():
            loss_ref[...] += part
            dg_ref[...] += dg

    return pl.pallas_call(
        body, name=name, grid=(T // tm,),
        in_specs=[pl.BlockSpec((tm, D), lambda i: (i, 0)), pl.BlockSpec((1, D), lambda i: (0, 0)),
                  pl.BlockSpec((tm, D), lambda i: (i, 0))],
        out_specs=[pl.BlockSpec((1, 1), lambda i: (0, 0)), pl.BlockSpec((tm, D), lambda i: (i, 0)),
                   pl.BlockSpec((tm, D), lambda i: (i, 0)), pl.BlockSpec((1, D), lambda i: (0, 0))],
        out_shape=[jax.ShapeDtypeStruct((1, 1), F32), jax.ShapeDtypeStruct((T, D), F32),
                   jax.ShapeDtypeStruct((T, D), BF16), jax.ShapeDtypeStruct((1, D), F32)],
        compiler_params=_params(1))(h, g, target)


def _nt_rows(name, dh, wg, layer, a_mul, out_dtype, tm, deps=()):
    T, D = dh.shape
    rk = wg.shape[2]
    N = N_CHIPS * rk
    with_a = a_mul is not None

    def body(dh_ref, w_ref, *rest):
        o_ref = rest[-1]
        d16 = dh_ref[...]
        for ch in range(N_CHIPS):
            r = _dot_nt(d16, w_ref[ch])
            if with_a:
                r = r * (2.0 * jnp.maximum(rest[0][:, ch * rk:(ch + 1) * rk].astype(F32), 0.0))
            o_ref[:, ch * rk:(ch + 1) * rk] = r.astype(out_dtype)

    in_specs = [pl.BlockSpec((tm, D), lambda i: (i, 0)),
                pl.BlockSpec((N_CHIPS, None, rk, D), lambda i: (0, layer, 0, 0))]
    args = [dh, wg]
    if with_a:
        in_specs.append(pl.BlockSpec((tm, N), lambda i: (i, 0)))
        args.append(a_mul)
    in_specs += [ANY] * len(deps)
    args += list(deps)
    return pl.pallas_call(
        body, name=name, grid=(T // tm,), in_specs=in_specs,
        out_specs=pl.BlockSpec((tm, N), lambda i: (i, 0)),
        out_shape=jax.ShapeDtypeStruct((T, N), out_dtype),
        compiler_params=_params(1))(*args)


def _nt_cols(name, ysegs, wg, layer, tm, norm):
    Nw, cw = wg.shape[2], wg.shape[3]
    widths = [bs[-1] for _, bs, _ in ysegs]
    pieces = _pieces(widths, cw, 1024)
    ns = len(ysegs)
    T = norm[0].shape[0] if norm is not None else ysegs[0][0].shape[-2]

    def body(*refs):
        y_refs = refs[:ns]
        w_ref = refs[ns]
        acc = refs[-1]
        for n, (s, a0, ch, b0, wd) in enumerate(pieces):
            d = _dot_nt(y_refs[s][:, a0:a0 + wd].astype(BF16), w_ref[ch, :, b0:b0 + wd])
            if n == 0:
                acc[...] = d
            else:
                acc[...] += d
        if norm is None:
            refs[ns + 1][...] = acc[...]
        else:
            h_ref, g_ref, dhin_ref, out_ref, out16_ref, dg_ref = refs[ns + 1:ns + 7]
            dh_c, dg = _rms_bwd(h_ref[...], g_ref[...], acc[...])
            dh = dhin_ref[...] + dh_c
            out_ref[...] = dh
            out16_ref[...] = dh.astype(BF16)

            @pl.when(pl.program_id(0) == 0)
            def _():
                dg_ref[...] = dg

            @pl.when(pl.program_id(0) > 0)
            def _():
                dg_ref[...] += dg

    in_specs = [pl.BlockSpec(bs, im) for _, bs, im in ysegs]
    in_specs.append(pl.BlockSpec((N_CHIPS, None, Nw, cw), lambda i: (0, layer, 0, 0)))
    args = [a for a, _, _ in ysegs] + [wg]
    row = pl.BlockSpec((tm, Nw), lambda i: (i, 0))
    vec = pl.BlockSpec((1, Nw), lambda i: (0, 0))
    if norm is None:
        out_specs = row
        out_shape = jax.ShapeDtypeStruct((T, Nw), F32)
    else:
        in_specs += [row, vec, row]
        args += list(norm)
        out_specs = [row, row, vec]
        out_shape = [jax.ShapeDtypeStruct((T, Nw), F32), jax.ShapeDtypeStruct((T, Nw), BF16),
                     jax.ShapeDtypeStruct((1, Nw), F32)]
    return pl.pallas_call(
        body, name=name, grid=(T // tm,), in_specs=in_specs, out_specs=out_specs, out_shape=out_shape,
        scratch_shapes=[pltpu.VMEM((tm, Nw), F32)], compiler_params=_params(1))(*args)


def _tn(name, x, x_act, ysegs, cw, cols_layout, tmm, tt, deps=()):
    T, M = x.shape
    widths = [bs[-1] for _, bs, _ in ysegs]
    N = sum(widths)
    pieces = _pieces(widths, cw if cols_layout else N, 1024)
    ns = len(ysegs)

    def body(x_ref, *refs):
        y_refs = refs[:ns]
        o_ref = refs[-1]

        @pl.when(pl.program_id(1) == 0)
        def _():
            o_ref[...] = jnp.zeros_like(o_ref)

        xt = x_act(x_ref[...])
        for s, a0, ch, b0, wd in pieces:
            d = _dot_tn(xt, y_refs[s][:, a0:a0 + wd].astype(BF16))
            if cols_layout:
                o_ref[ch, :, b0:b0 + wd] += d
            else:
                o_ref[:, b0:b0 + wd] += d

    in_specs = [pl.BlockSpec((tt, tmm), lambda m, t: (t, m))] + [pl.BlockSpec(bs, im) for _, bs, im in ysegs]
    in_specs += [ANY] * len(deps)
    if cols_layout:
        out_specs = pl.BlockSpec((N_CHIPS, tmm, cw), lambda m, t: (0, m, 0))
        out_shape = jax.ShapeDtypeStruct((N_CHIPS, M, cw), F32)
    else:
        out_specs = pl.BlockSpec((tmm, N), lambda m, t: (m, 0))
        out_shape = jax.ShapeDtypeStruct((M, N), F32)
    return pl.pallas_call(
        body, name=name, grid=(M // tmm, T // tt), in_specs=in_specs, out_specs=out_specs, out_shape=out_shape,
        compiler_params=_params(2))(x, *[a for a, _, _ in ysegs], *deps)


def _seg2d(a, t_rows, grid_rank):
    w = a.shape[1]
    if grid_rank == 1:
        return (a, (t_rows, w), lambda i: (i, 0))
    return (a, (t_rows, w), lambda m, t: (t, 0))


def _kv_segments(dk, dv, C, t_rows, grid_rank):
    segs = []
    for g in range(len(PATTERNS)):
        for a in (dk, dv):
            if grid_rank == 1:
                segs.append((a, (t_rows, C), lambda i, g=g: (i, g)))
            else:
                segs.append((a, (t_rows, C), lambda m, t, g=g: (t, g)))
    return segs


def _seg_plane(a, plane, t_rows, grid_rank):
    w = a.shape[2]
    if grid_rank == 1:
        return (a, (None, t_rows, w), lambda i: (plane, i, 0))
    return (a, (None, t_rows, w), lambda m, t: (plane, t, 0))


def _row_tile(rows, row_bytes, budget_bytes=2 * 1024 * 1024):
    t = rows
    while t * row_bytes > budget_bytes and t % 32 == 0:
        t //= 2
    return t


def _pair_add(name, layers, recv, place):
    L = len(layers)
    _, _, hr, c = layers[0].shape
    tr = _row_tile(hr, L * c * 4)

    def body(place_ref, *refs):
        r_ref, o_ref = refs[L], refs[L + 1]
        for l in range(L):
            o_ref[l] = (refs[l][...] + r_ref[l]).astype(BF16)

    stacked = pl.BlockSpec((None, L, tr, c), lambda q, i, pr: (q, 0, i, 0))
    grid_spec = pltpu.PrefetchScalarGridSpec(
        num_scalar_prefetch=1, grid=(N_CHIPS, hr // tr),
        in_specs=[pl.BlockSpec((None, None, tr, c), lambda q, i, pr: (q, pr[1], i, 0))] * L + [stacked],
        out_specs=stacked)
    return pl.pallas_call(body, name=name, grid_spec=grid_spec,
                          out_shape=jax.ShapeDtypeStruct((N_CHIPS, L, hr, c), BF16),
                          compiler_params=_params(2))(place, *layers, recv)


def _chip_add(name, part, slots, place):
    _, L, hr, c = part.shape
    tr = _row_tile(hr, L * c * 4)

    def body(place_ref, own, s1, s2, s3, o_ref):
        f = lambda r: r[...].astype(F32)
        o_ref[...] = ((f(own) + f(s1)) + f(s2)) + f(s3)

    def slot(k):
        return pl.BlockSpec((None, L, tr, c), lambda i, pr: ((pr[0] + k) % N_CHIPS, 0, i, 0))

    grid_spec = pltpu.PrefetchScalarGridSpec(
        num_scalar_prefetch=1, grid=(hr // tr,),
        in_specs=[slot(0), slot(1), slot(2), slot(3)],
        out_specs=pl.BlockSpec((L, None, tr, c), lambda i, pr: (0, pr[1], i, 0)))
    return pl.pallas_call(body, name=name, grid_spec=grid_spec,
                          out_shape=jax.ShapeDtypeStruct((L, 2, hr, c), F32),
                          compiler_params=_params(1))(place, part, slots, slots, slots)


def _adamw(name, w, g, m, v):
    rows, cols = w.shape
    tr = _row_tile(rows, cols * 4, 1024 * 1024)

    def body(w_ref, g_ref, m_ref, v_ref, d_ref, nm_ref, nv_ref):
        d_ref[...], nm_ref[...], nv_ref[...] = _adamw_math(w_ref[...], g_ref[...], m_ref[...], v_ref[...])

    spec = pl.BlockSpec((tr, cols), lambda i: (i, 0))
    return pl.pallas_call(
        body, name=name, grid=(rows // tr,), in_specs=[spec] * 4, out_specs=[spec] * 3,
        out_shape=[jax.ShapeDtypeStruct((rows, cols), F32)] * 3, compiler_params=_params(1))(w, g, m, v)


def _adamw_math(w, g, m, v):
    nm = ADAM_B1 * m + (1.0 - ADAM_B1) * g
    nv = ADAM_B2 * v + (1.0 - ADAM_B2) * jnp.square(g)
    m_hat = nm / (1.0 - ADAM_B1 ** ADAM_STEP)
    v_hat = nv / (1.0 - ADAM_B2 ** ADAM_STEP)
    return -ADAM_LR * (m_hat / (jnp.sqrt(v_hat) + ADAM_EPS) + ADAM_WD * w), nm, nv


def _adamw_layers(name, w, grads, m, v):
    L, r, c = w.shape
    tr = _row_tile(r, L * c * 4, 1024 * 1024)

    def body(*refs):
        w_ref, m_ref, v_ref = refs[:3]
        g_refs = refs[3:3 + L]
        go_ref, d_ref, nm_ref, nv_ref = refs[3 + L:]
        for l in range(L):
            g = g_refs[l][...]
            go_ref[l] = g
            d_ref[l], nm_ref[l], nv_ref[l] = _adamw_math(w_ref[l], g, m_ref[l], v_ref[l])

    stacked = pl.BlockSpec((L, tr, c), lambda i: (0, i, 0))
    return pl.pallas_call(
        body, name=name, grid=(r // tr,),
        in_specs=[stacked] * 3 + [pl.BlockSpec((tr, c), lambda i: (i, 0))] * L, out_specs=[stacked] * 4,
        out_shape=[jax.ShapeDtypeStruct((L, r, c), F32)] * 4, compiler_params=_params(1))(w, m, v, *grads)


def _place():
    x, y, c = lax.axis_index("x"), lax.axis_index("y"), lax.axis_index("c")
    chips = [(1 - x, y), (x, 1 - y), (1 - x, 1 - y)]
    return x, y, c, chips


HBM = pl.BlockSpec(memory_space=pltpu.HBM)
SEM = pl.BlockSpec(memory_space=pltpu.SEMAPHORE)
EFFECT = pltpu.SideEffectType.DATAFLOW_SIDE_EFFECTING


class _Copy:
    def __init__(self, src, src_view, land, dst_view, recv_view, target):
        self.src, self.src_view, self.land, self.dst_view, self.recv_view, self.target = (
            src, src_view, land, dst_view, recv_view, target)


def _whole(ref, place):
    return ref


def _split_start(name, srcs, land_shapes, plans):
    skeys, lkeys = list(srcs), list(land_shapes)
    ns, nl, ng = len(skeys), len(lkeys), len(plans)

    def body(*refs):
        src = dict(zip(skeys, refs[:ns]))
        land = dict(zip(lkeys, refs[ns:ns + nl]))
        sems = refs[ns + nl:ns + nl + 2 * ng]
        token = refs[-1]
        place = _place()
        for gi, plan in enumerate(plans):
            for k, cp in enumerate(plan):
                pltpu.make_async_remote_copy(
                    src_ref=cp.src_view(src[cp.src], place), dst_ref=cp.dst_view(land[cp.land], place),
                    send_sem=sems[2 * gi].at[k], recv_sem=sems[2 * gi + 1].at[k],
                    device_id=cp.target(place), device_id_type=MESH).start()
        token[...] = jnp.zeros_like(token)

    sem_shapes = []
    for plan in plans:
        sem_shapes += [pltpu.SemaphoreType.DMA((len(plan),))] * 2
    buffers = [srcs[k] for k in skeys] + [lax.empty(land_shapes[k].shape, land_shapes[k].dtype) for k in lkeys]
    outs = pl.pallas_call(
        body, name=name,
        out_shape=(*sem_shapes, *[pltpu.HBM(a.shape, a.dtype) for a in buffers], jax.ShapeDtypeStruct((8, LANES), F32)),
        in_specs=[HBM] * (ns + nl),
        out_specs=(*[SEM] * (2 * ng), *[HBM] * (ns + nl), pl.BlockSpec(memory_space=pltpu.VMEM)),
        input_output_aliases={i: 2 * ng + i for i in range(ns + nl)},
        compiler_params=pltpu.CompilerParams(has_side_effects=EFFECT),
    )(*[pltpu.with_memory_space_constraint(a, pltpu.HBM) for a in buffers])
    sems = [(outs[2 * gi], outs[2 * gi + 1]) for gi in range(ng)]
    thru = outs[2 * ng:2 * ng + ns + nl]
    return sems, dict(zip(skeys, thru[:ns])), dict(zip(lkeys, thru[ns:])), outs[-1]


def _split_wait(name, sems, srcs, lands, plan, after):
    skeys, lkeys = list(srcs), list(lands)
    ns, nl = len(skeys), len(lkeys)

    def body(*refs):
        src = dict(zip(skeys, refs[:ns]))
        land = dict(zip(lkeys, refs[ns:ns + nl]))
        ssem, rsem = refs[ns + nl], refs[ns + nl + 1]
        place = _place()
        for k, cp in enumerate(plan):
            pltpu.make_async_remote_copy(
                src_ref=cp.src_view(src[cp.src], place), dst_ref=cp.dst_view(land[cp.land], place),
                send_sem=ssem.at[k], recv_sem=rsem.at[k],
                device_id=cp.target(place), device_id_type=MESH).wait_send()
            got = cp.recv_view(land[cp.land], place)
            pltpu.make_async_remote_copy(
                src_ref=got, dst_ref=got, send_sem=ssem.at[k], recv_sem=rsem.at[k],
                device_id=cp.target(place), device_id_type=MESH).wait_recv()

    buffers = [srcs[k] for k in skeys] + [lands[k] for k in lkeys]
    outs = pl.pallas_call(
        body, name=name, out_shape=tuple(pltpu.HBM(a.shape, a.dtype) for a in buffers),
        in_specs=(*[HBM] * (ns + nl), SEM, SEM, ANY), out_specs=tuple([HBM] * (ns + nl)),
        input_output_aliases={i: i for i in range(ns + nl)},
        compiler_params=pltpu.CompilerParams(has_side_effects=EFFECT),
    )(*buffers, sems[0], sems[1], after)
    return dict(zip(skeys, outs[:ns])), dict(zip(lkeys, outs[ns:]))


def _chip_of(place):
    x, y, c, chips = place
    return 2 * x + y


class _WeightGather:
    def __init__(self, blocks):
        self.plans, shapes = {}, {}
        for key, a in blocks.items():
            shapes[key] = jax.ShapeDtypeStruct((N_CHIPS,) + a.shape, a.dtype)
            slot = lambda ref, place: ref.at[_chip_of(place)]
            plan = [_Copy(key, _whole, key, slot,
                          lambda ref, place, k=k: ref.at[2 * place[3][k][0] + place[3][k][1]],
                          lambda place, k=k: (place[3][k][0], place[3][k][1], place[2])) for k in range(3)]
            plan.append(_Copy(key, _whole, key, slot, slot, lambda place: (place[0], place[1], 1 - place[2])))
            self.plans[key] = plan
        sems, self.srcs, self.lands, self.token = _split_start("gather_start", blocks, shapes, list(self.plans.values()))
        self.sems = dict(zip(self.plans, sems))

    def get(self, l, name, after):
        key = (l, name)
        _, lands = _split_wait(f"gather_wait_{name}{l}", self.sems[key], {key: self.srcs[key]},
                               {key: self.lands[key]}, self.plans[key], after)
        return lands[key][:, None]


class _GradReduce:
    def __init__(self, place):
        self.place = place
        self.jobs = []
        self.done = {}
        self.n = 0

    def submit(self, grads):
        views = {k: a.reshape(N_CHIPS, 2, a.shape[1] // 2, a.shape[2]) for k, a in grads.items()}
        shapes = {k: jax.ShapeDtypeStruct((N_CHIPS,) + a.shape[2:], F32) for k, a in views.items()}
        sibling = lambda place: (place[0], place[1], 1 - place[2])
        plan = [_Copy(k, lambda ref, place: ref.at[:, 1 - place[2]], k, _whole, _whole, sibling) for k in views]
        sems, srcs, lands, token = _split_start(f"grad_pair_start{self.n}", views, shapes, [plan])
        self.jobs.append(dict(id=self.n, stage=1, sems=sems[0], srcs=srcs, lands=lands, plan=plan))
        self.n += 1
        return token

    def pump(self, after):
        tokens = []
        for job in list(self.jobs):
            srcs, lands = _split_wait(f"grad_wait{job['id']}_{job['stage']}", job["sems"], job["srcs"], job["lands"],
                                      job["plan"], after)
            if job["stage"] == 1:
                parts = {k: _pair_add(f"grad_pair_add{job['id']}_{i}", [srcs[k]], lands[k][:, None], self.place)
                         for i, k in enumerate(srcs)}
                shapes = {k: jax.ShapeDtypeStruct(a.shape, a.dtype) for k, a in parts.items()}
                plan = []
                for k in parts:
                    for j in range(3):
                        there = lambda ref, place, j=j: ref.at[2 * place[3][j][0] + place[3][j][1]]
                        plan.append(_Copy(k, there, k, lambda ref, place: ref.at[_chip_of(place)], there,
                                          lambda place, j=j: (place[3][j][0], place[3][j][1], place[2])))
                sems, srcs2, lands2, token = _split_start(f"grad_chip_start{job['id']}", parts, shapes, [plan])
                job.update(stage=2, sems=sems[0], srcs=srcs2, lands=lands2, plan=plan)
                tokens.append(token)
            else:
                for i, k in enumerate(srcs):
                    self.done[k] = _chip_add(f"grad_chip_add{job['id']}_{i}", srcs[k], lands[k], self.place)[0]
                self.jobs.remove(job)
        return tokens

    def finish(self, after):
        while self.jobs:
            self.pump(after)
        return self.done


def _pair_share(halves):
    n = len(halves)

    def body(*refs):
        outs = refs[n:2 * n]
        ssem, rsem = refs[2 * n:]
        x, y, c, _ = _place()
        sends = []
        for t in range(n):
            cp = pltpu.make_async_remote_copy(
                src_ref=outs[t].at[c], dst_ref=outs[t].at[c], send_sem=ssem.at[t], recv_sem=rsem.at[t],
                device_id=(x, y, 1 - c), device_id_type=MESH)
            cp.start()
            sends.append(cp)
        for t in range(n):
            theirs = outs[t].at[1 - c]
            pltpu.make_async_remote_copy(
                src_ref=theirs, dst_ref=theirs, send_sem=ssem.at[t], recv_sem=rsem.at[t],
                device_id=(x, y, 1 - c), device_id_type=MESH).wait_recv()
        for cp in sends:
            cp.wait_send()

    return pl.pallas_call(
        body, name="grad_pair_share", in_specs=[ANY] * n, out_specs=[ANY] * n,
        out_shape=[jax.ShapeDtypeStruct(a.shape, a.dtype) for a in halves],
        input_output_aliases={t: t for t in range(n)},
        scratch_shapes=[pltpu.SemaphoreType.DMA((n,)), pltpu.SemaphoreType.DMA((n,))])(*halves)


def _small_allreduce(part):
    R, C = part.shape
    N_DEV = 8

    def body(in_ref, out_ref, slots, ssem, rsem):
        x, y, c, _ = _place()
        me = 4 * x + 2 * y + c
        sends = []
        for k in range(1, N_DEV):
            kx, ky, kc = (k >> 2) & 1, (k >> 1) & 1, k & 1
            peer = (1 - x if kx else x, 1 - y if ky else y, 1 - c if kc else c)
            cp = pltpu.make_async_remote_copy(
                src_ref=in_ref, dst_ref=slots.at[me], send_sem=ssem.at[k], recv_sem=rsem.at[k],
                device_id=peer, device_id_type=MESH)
            cp.start()
            sends.append(cp)
        slots[me] = in_ref[...]
        for k in range(1, N_DEV):
            kx, ky, kc = (k >> 2) & 1, (k >> 1) & 1, k & 1
            peer = (1 - x if kx else x, 1 - y if ky else y, 1 - c if kc else c)
            slot = slots.at[4 * peer[0] + 2 * peer[1] + peer[2]]
            pltpu.make_async_remote_copy(
                src_ref=slot, dst_ref=slot, send_sem=ssem.at[k], recv_sem=rsem.at[k],
                device_id=peer, device_id_type=MESH).wait_recv()
        acc = slots[0]
        for d in range(1, N_DEV):
            acc = acc + slots[d]
        out_ref[...] = acc
        for cp in sends:
            cp.wait_send()

    vm = pl.BlockSpec(memory_space=pltpu.VMEM)
    return pl.pallas_call(
        body, name="small_allreduce", in_specs=[vm], out_specs=vm,
        out_shape=jax.ShapeDtypeStruct((R, C), F32),
        scratch_shapes=[pltpu.VMEM((N_DEV, R, C), F32), pltpu.SemaphoreType.DMA((N_DEV,)),
                        pltpu.SemaphoreType.DMA((N_DEV,))])(part)


def _local_step(x, target, norm_mix, norm_mlp, norm_kv, norm_final, weights, sink, n_a, n_heads):
    B, S, D = x.shape
    T = B * S
    C = n_heads * HEAD_DIM
    depth = norm_mix.shape[0]
    slopes = 2.0 ** (-ALIBI_MAX_BIAS * jnp.arange(1, n_heads + 1, dtype=F32) / n_heads)
    tm = min(512, T)
    row = lambda v: v.reshape(1, -1)

    h = x.reshape(T, D)
    saved, Wl = [], []
    kv = nkv = h_kv = cwg = None
    for l in range(depth):
        s = {"h_in": h}
        w = {}
        Wl.append(w)
        if l < n_a:
            w["w_a_in"] = weights.get(l, "w_a_in", h)
            s["n1"], bcu = _norm_mm(f"a_in_fwd{l}", h, row(norm_mix[l]), w["w_a_in"], 0, 3, BF16, tm)
            s["bcu"] = bcu.reshape(3, B, S, D)
            if l == 0:
                cwg = weights.get(0, "conv", bcu)[:, 0, :n_a * 3].reshape(N_CHIPS, n_a, 3, -1)
            s["z"] = _conv_fwd(f"conv_fwd{l}", s["bcu"], cwg, l, LANES).reshape(T, D)
            w["w_a_out"] = weights.get(l, "w_a_out", s["z"])
            h = _mm_res_rows(f"a_out_fwd{l}", s["z"], w["w_a_out"], 0, h, _to_bf16, tm)
        else:
            i = l - n_a
            if i == 0:
                h_kv = h
                w["w_kv"] = weights.get(l, "w_kv", h)
                nkv, kv = _norm_mm("kv_fwd", h, row(norm_kv), w["w_kv"], 0, 1, F32, tm)
                kv = kv.reshape(B, S, 2 * 3 * C)
            w["w_q"] = weights.get(l, "w_q", h)
            s["n1"], q = _norm_mm(f"q_fwd{i}", h, row(norm_mix[l]), w["w_q"], 0, 1, F32, tm)
            s["q"] = q.reshape(B, S, 3 * C)
            o, lse = _attn_fwd(f"attn_fwd{i}", s["q"], kv, slopes, n_heads)
            s["o"], s["lse"] = o.reshape(T, C), lse.reshape(T, C)
            w["w_o"] = weights.get(l, "w_o", o)
            h = _mm_res_cols(f"o_fwd{i}", s["o"], w["w_o"], 0, h, tm)
        s["h_mid"] = h
        w["w_up"] = weights.get(l, "w_up", h)
        w["w_down"] = weights.get(l, "w_down", h)
        s["n2"], s["a"], h = _mlp_fwd(f"mlp_fwd{l}", h, row(norm_mlp[l]), w["w_up"], w["w_down"], tm)
        F = s["a"].shape[1]
        saved.append(s)

    loss, dh, dh16, dg_final = _final_loss("loss_head", h, row(norm_final), target.reshape(T, D), tm)

    g_mix, g_mlp = [None] * depth, [None] * depth
    g_conv = [None] * n_a
    dkv = None
    tt = min(512, T)
    deps = []
    for l in reversed(range(depth)):
        s, w = saved[l], Wl[l]
        g_down = _tn(f"down_wgrad{l}", s["a"], _relu2_bf16, [_seg2d(dh16, tt, 2)], None, False,
                     min(2048, F), tt, deps).reshape(N_CHIPS, F // N_CHIPS, D)
        da, dh, dh16, g_mlp[l] = _mlp_bwd(f"mlp_bwd{l}", dh, dh16, s["a"], w["w_down"], w["w_up"], s["h_mid"],
                                          row(norm_mlp[l]), tm)
        g_up = _tn(f"up_wgrad{l}", s["n2"], _to_bf16, [_seg2d(da, tt, 2)], F // N_CHIPS, True, D, tt)
        deps = sink.pump(dh) + [sink.submit({("w_up", l): g_up, ("w_down", l): g_down})]
        if l < n_a:
            g_out = _tn(f"a_out_wgrad{l}", s["z"], _to_bf16, [_seg2d(dh16, tt, 2)], None, False,
                        D, tt, deps).reshape(N_CHIPS, D // N_CHIPS, D)
            dz = _nt_rows(f"a_out_bwd{l}", dh16, w["w_a_out"], 0, None, F32, tm)
            deps = sink.pump(dz) + [sink.submit({("w_a_out", l): g_out})]
            dbcu, g_conv[l] = _conv_bwd(f"conv_bwd{l}", s["bcu"], dz.reshape(B, S, D), cwg, l, LANES)
            dbcu = dbcu.reshape(3, T, D)
            g_in = _tn(f"a_in_wgrad{l}", s["n1"], _to_bf16, [_seg_plane(dbcu, p, tt, 2) for p in range(3)],
                       3 * D // N_CHIPS, True, D, tt, deps)
            dh, dh16, g_mix[l] = _nt_cols(f"a_in_bwd{l}", [_seg_plane(dbcu, p, tm, 1) for p in range(3)],
                                          w["w_a_in"], 0, tm, (s["h_in"], row(norm_mix[l]), dh))
            mixer = {("w_a_in", l): g_in}
        else:
            i = l - n_a
            g_o = _tn(f"o_wgrad{i}", s["o"], _to_bf16, [_seg2d(dh16, tt, 2)], D // N_CHIPS, True, C, tt, deps)
            do = _nt_cols(f"o_bwd{i}", [_seg2d(dh16, tm, 1)], w["w_o"], 0, tm, None)
            deps = sink.pump(do) + [sink.submit({("w_o", i): g_o})]
            dq, dk, dv = _attn_bwd(f"attn_bwd{i}", s["q"], kv, slopes, s["o"].reshape(B, S, C),
                                   s["lse"].reshape(B, S, C), do.reshape(B, S, C), n_heads, dkv)
            dkv = (dk, dv)
            dq = dq.reshape(T, 3 * C)
            g_q = _tn(f"q_wgrad{i}", s["n1"], _to_bf16, [_seg2d(dq, tt, 2)], 3 * C // N_CHIPS, True, D, tt, deps)
            dh, dh16, g_mix[l] = _nt_cols(f"q_bwd{i}", [_seg2d(dq, tm, 1)], w["w_q"], 0, tm,
                                          (s["h_in"], row(norm_mix[l]), dh))
            mixer = {("w_q", i): g_q}
            if i == 0:
                dk2, dv2 = (t.reshape(T, 3 * C) for t in dkv)
                mixer[("w_kv", 0)] = _tn("kv_wgrad", nkv, _to_bf16, _kv_segments(dk2, dv2, C, tt, 2),
                                         6 * C // N_CHIPS, True, D, tt)
                dh, dh16, g_kv = _nt_cols("kv_bwd", _kv_segments(dk2, dv2, C, tm, 1), w["w_kv"], 0, tm,
                                          (h_kv, row(norm_kv), dh))
        deps = sink.pump(dh) + [sink.submit(mixer)]
    small = dict(norm_mix=jnp.concatenate(g_mix, axis=0), norm_mlp=jnp.concatenate(g_mlp, axis=0),
                 norm_kv=g_kv, norm_final=dg_final, conv_w=jnp.stack(g_conv))
    return loss, dh.reshape(B, S, D), small


BIG = ("w_a_in", "w_a_out", "w_kv", "w_q", "w_o", "w_up", "w_down")
CONV_PAD_ROWS = 16


def kernel(x, norm_mix, norm_mlp, w_a_in, conv_w, w_a_out, norm_kv, w_kv, w_q, w_o, w_up, w_down, norm_final, loss_target, m_norm_mix, m_norm_mlp, m_w_a_in, m_conv_w, m_w_a_out, m_norm_kv, m_w_kv, m_w_q, m_w_o, m_w_up, m_w_down, m_norm_final, v_norm_mix, v_norm_mlp, v_w_a_in, v_conv_w, v_w_a_out, v_norm_kv, v_w_kv, v_w_q, v_w_o, v_w_up, v_w_down, v_norm_final):
    D = x.shape[-1]
    w = dict(norm_mix=norm_mix, norm_mlp=norm_mlp, w_a_in=w_a_in, conv_w=conv_w, w_a_out=w_a_out, norm_kv=norm_kv,
             w_kv=w_kv[None], w_q=w_q, w_o=w_o, w_up=w_up, w_down=w_down, norm_final=norm_final)
    m = dict(norm_mix=m_norm_mix, norm_mlp=m_norm_mlp, w_a_in=m_w_a_in, conv_w=m_conv_w, w_a_out=m_w_a_out,
             norm_kv=m_norm_kv, w_kv=m_w_kv[None], w_q=m_w_q, w_o=m_w_o, w_up=m_w_up, w_down=m_w_down,
             norm_final=m_norm_final)
    v = dict(norm_mix=v_norm_mix, norm_mlp=v_norm_mlp, w_a_in=v_w_a_in, conv_w=v_conv_w, w_a_out=v_w_a_out,
             norm_kv=v_norm_kv, w_kv=v_w_kv[None], w_q=v_w_q, w_o=v_w_o, w_up=v_w_up, w_down=v_w_down,
             norm_final=v_norm_final)
    depth = norm_mix.shape[0]
    n_a, taps, cwc = conv_w.shape
    n_heads = w_o.shape[1] // HEAD_DIM

    conv_rows = jnp.zeros((CONV_PAD_ROWS, cwc), F32).at[:n_a * taps].set(conv_w.reshape(n_a * taps, cwc))
    blocks = {}
    for l in range(depth):
        if l < n_a:
            blocks[(l, "w_a_in")] = w_a_in[l].astype(BF16)
            if l == 0:
                blocks[(0, "conv")] = conv_rows
            blocks[(l, "w_a_out")] = w_a_out[l].astype(BF16)
        else:
            if l == n_a:
                blocks[(l, "w_kv")] = w_kv.astype(BF16)
            blocks[(l, "w_q")] = w_q[l - n_a].astype(BF16)
            blocks[(l, "w_o")] = w_o[l - n_a].astype(BF16)
        blocks[(l, "w_up")] = w_up[l].astype(BF16)
        blocks[(l, "w_down")] = w_down[l].astype(BF16)
    weights = _WeightGather(blocks)
    place = jnp.stack([2 * lax.axis_index("x") + lax.axis_index("y"), lax.axis_index("c")]).astype(jnp.int32)
    sink = _GradReduce(place)

    loss, grad_x, small = _local_step(x, loss_target, norm_mix, norm_mlp, norm_kv, norm_final, weights, sink,
                                      n_a, n_heads)
    loss = lax.psum(loss[0, 0], ("x", "y", "c"))

    done = sink.finish(grad_x)
    keys = list(done)
    shared = dict(zip(keys, _pair_share([done[k] for k in keys])))
    grads = {}

    packed = jnp.concatenate([small["norm_mix"], small["norm_mlp"], small["norm_kv"], small["norm_final"],
                              small["conv_w"].reshape(n_a * taps, D)], axis=0)
    pad = (-packed.shape[0]) % 8
    packed = jnp.pad(packed, ((0, pad), (0, 0)))
    total = _small_allreduce(packed)
    grads["norm_mix"] = total[:depth]
    grads["norm_mlp"] = total[depth:2 * depth]
    grads["norm_kv"] = total[2 * depth]
    grads["norm_final"] = total[2 * depth + 1]
    chip = 2 * lax.axis_index("x") + lax.axis_index("y")
    conv_full = total[2 * depth + 2:2 * depth + 2 + n_a * taps].reshape(n_a, taps, N_CHIPS, cwc)
    grads["conv_w"] = lax.dynamic_index_in_dim(conv_full, chip, axis=2, keepdims=False)

    order = ("norm_mix", "norm_mlp", "w_a_in", "conv_w", "w_a_out", "norm_kv", "w_kv", "w_q", "w_o", "w_up",
             "w_down", "norm_final")
    delta, new_m, new_v = {}, {}, {}
    vec_names = ("norm_mix", "norm_mlp", "norm_kv", "norm_final")
    rows_of = lambda a: a.reshape(-1, D)
    vw, vg, vm_, vv = (jnp.concatenate([rows_of(t[k]) for k in vec_names], axis=0) for t in (w, grads, m, v))
    vpad = (-vw.shape[0]) % 8
    padrows = lambda a: jnp.pad(a, ((0, vpad), (0, 0)))
    vd, vnm, vnv = _adamw("adamw_norms", padrows(vw), padrows(vg), padrows(vm_), padrows(vv))
    off = 0
    for k in vec_names:
        r = rows_of(w[k]).shape[0]
        delta[k] = vd[off:off + r].reshape(w[k].shape)
        new_m[k] = vnm[off:off + r].reshape(w[k].shape)
        new_v[k] = vnv[off:off + r].reshape(w[k].shape)
        off += r
    cpad = (-n_a * taps) % 8
    two_d = lambda a: jnp.pad(a.reshape(-1, cwc), ((0, cpad), (0, 0)))
    cd, cnm, cnv = _adamw("adamw_conv_w", two_d(w["conv_w"]), two_d(grads["conv_w"]), two_d(m["conv_w"]),
                          two_d(v["conv_w"]))
    delta["conv_w"], new_m["conv_w"], new_v["conv_w"] = (t[:n_a * taps].reshape(conv_w.shape) for t in (cd, cnm, cnv))
    for k in BIG:
        per_layer = [shared[(k, l)].reshape(w[k].shape[1:]) for l in range(w[k].shape[0])]
        grads[k], delta[k], new_m[k], new_v[k] = _adamw_layers(f"adamw_{k}", w[k], per_layer, m[k], v[k])
    fix = lambda k, a: a[0] if k == "w_kv" else a
    return (loss, grad_x, *[fix(k, grads[k]) for k in order], *[fix(k, delta[k]) for k in order],
            *[fix(k, new_m[k]) for k in order], *[fix(k, new_v[k]) for k in order])
```

```python
import functools

import jax
import jax.numpy as jnp
from jax import lax
from jax.experimental import pallas as pl
from jax.experimental.pallas import tpu as pltpu

F32 = jnp.float32
BF16 = jnp.bfloat16
MESH = pl.DeviceIdType.MESH

EPS = 1e-5
PATTERNS = ((128, 1), (512, 4), (2048, 16))
HEAD_DIM = 64
ALIBI_MAX_BIAS = 8.0
NEG_INF = -1e30
ATT_BLK = 128
BWD_UNROLL = 8
N_CHIPS = 4
LANES = 128
VMEM_LIMIT = 56 * 1024 * 1024

ADAM_LR = 0.001
ADAM_B1 = 0.9
ADAM_B2 = 0.999
ADAM_EPS = 1e-08
ADAM_WD = 0.01
ADAM_STEP = 10


ANY = pl.BlockSpec(memory_space=pl.ANY)


def _params(n_grid_axes):
    return pltpu.CompilerParams(dimension_semantics=("arbitrary",) * n_grid_axes, vmem_limit_bytes=VMEM_LIMIT)


def _dot(a, b):
    return jnp.dot(a, b, preferred_element_type=F32)


def _dot_nt(a, b):
    return lax.dot_general(a, b, (((1,), (1,)), ((), ())), preferred_element_type=F32)


def _dot_tn(a, b):
    return lax.dot_general(a, b, (((0,), (0,)), ((), ())), preferred_element_type=F32)


def _relu2(a):
    return jnp.square(jnp.maximum(a, 0.0))


def _rms(hf, g):
    y = hf * lax.rsqrt(jnp.mean(hf * hf, axis=-1, keepdims=True) + EPS)
    return y * g


def _rms_bwd(hf, g, dn):
    rstd = lax.rsqrt(jnp.mean(hf * hf, axis=-1, keepdims=True) + EPS)
    xhat = hf * rstd
    dg = jnp.sum(dn * xhat, axis=0, keepdims=True)
    dx = dn * g
    dh = rstd * (dx - xhat * jnp.mean(dx * xhat, axis=-1, keepdims=True))
    return dh, dg


def _pieces(seg_widths, chunk_width, max_width):
    total = sum(seg_widths)
    cuts = {0, total}
    acc = 0
    for w in seg_widths:
        cuts.add(acc)
        acc += w
    cuts.update(range(0, total, chunk_width))
    cuts = sorted(cuts)
    fine = []
    for lo, hi in zip(cuts[:-1], cuts[1:]):
        while hi - lo > max_width:
            fine.append((lo, lo + max_width))
            lo += max_width
        fine.append((lo, hi))
    out = []
    for lo, hi in fine:
        acc = 0
        for s, w in enumerate(seg_widths):
            if lo < acc + w:
                break
            acc += w
        out.append((s, lo - acc, lo // chunk_width, lo % chunk_width, hi - lo))
    return out


def _relu2_bf16(a):
    return _relu2(a.astype(F32)).astype(BF16)


def _to_bf16(a):
    return a.astype(BF16)


def _norm_mm(name, h, g, wg, layer, planes, out_dtype, tm):
    T, D = h.shape
    cw = wg.shape[3]
    N = N_CHIPS * cw
    pw = N // planes
    pieces = _pieces([pw] * planes, cw, 512)

    def body(h_ref, g_ref, w_ref, n_ref, o_ref):
        n = _rms(h_ref[...], g_ref[...]).astype(BF16)
        n_ref[...] = n
        for s, a0, ch, b0, wd in pieces:
            o_ref[s, :, a0:a0 + wd] = _dot(n, w_ref[ch, :, b0:b0 + wd]).astype(out_dtype)

    return pl.pallas_call(
        body, name=name, grid=(T // tm,),
        in_specs=[pl.BlockSpec((tm, D), lambda i: (i, 0)),
                  pl.BlockSpec((1, D), lambda i: (0, 0)),
                  pl.BlockSpec((N_CHIPS, None, D, cw), lambda i: (0, layer, 0, 0))],
        out_specs=[pl.BlockSpec((tm, D), lambda i: (i, 0)),
                   pl.BlockSpec((planes, tm, pw), lambda i: (0, i, 0))],
        out_shape=[jax.ShapeDtypeStruct((T, D), BF16), jax.ShapeDtypeStruct((planes, T, pw), out_dtype)],
        compiler_params=_params(1))(h, g, wg)


def _mm_res_rows(name, a, wg, layer, h, act, tm):
    T = a.shape[0]
    rk, D = wg.shape[2], wg.shape[3]

    def body(a_ref, w_ref, h_ref, o_ref):
        acc = h_ref[...]
        for k in range(N_CHIPS):
            acc = acc + _dot(act(a_ref[:, k * rk:(k + 1) * rk]), w_ref[k])
        o_ref[...] = acc

    return pl.pallas_call(
        body, name=name, grid=(T // tm,),
        in_specs=[pl.BlockSpec((tm, N_CHIPS * rk), lambda i: (i, 0)),
                  pl.BlockSpec((N_CHIPS, None, rk, D), lambda i: (0, layer, 0, 0)),
                  pl.BlockSpec((tm, D), lambda i: (i, 0))],
        out_specs=pl.BlockSpec((tm, D), lambda i: (i, 0)),
        out_shape=jax.ShapeDtypeStruct((T, D), F32),
        compiler_params=_params(1))(a, wg, h)


def _mm_res_cols(name, a, wg, layer, h, tm):
    T, K = a.shape
    cw = wg.shape[3]
    D = N_CHIPS * cw

    def body(a_ref, w_ref, h_ref, o_ref):
        a16 = a_ref[...].astype(BF16)
        for j in range(N_CHIPS):
            o_ref[:, j * cw:(j + 1) * cw] = h_ref[:, j * cw:(j + 1) * cw] + _dot(a16, w_ref[j])

    return pl.pallas_call(
        body, name=name, grid=(T // tm,),
        in_specs=[pl.BlockSpec((tm, K), lambda i: (i, 0)),
                  pl.BlockSpec((N_CHIPS, None, K, cw), lambda i: (0, layer, 0, 0)),
                  pl.BlockSpec((tm, D), lambda i: (i, 0))],
        out_specs=pl.BlockSpec((tm, D), lambda i: (i, 0)),
        out_shape=jax.ShapeDtypeStruct((T, D), F32),
        compiler_params=_params(1))(a, wg, h)


def _resident(shape, index_map):
    return pl.BlockSpec(shape, index_map, pipeline_mode=pl.Buffered(1))


def _mlp_fwd(name, h, g, wup, wdown, tm):
    T, D = h.shape
    cw = wup.shape[3]

    def body(h_ref, g_ref, wu_ref, wd_ref, n_ref, a_ref, o_ref):
        hf = h_ref[...]
        n = _rms(hf, g_ref[...]).astype(BF16)
        n_ref[...] = n
        acc = hf
        for ch in range(N_CHIPS):
            a16 = _dot(n, wu_ref[ch]).astype(BF16)
            a_ref[:, ch * cw:(ch + 1) * cw] = a16
            acc = acc + _dot(_relu2_bf16(a16), wd_ref[ch])
        o_ref[...] = acc

    row = pl.BlockSpec((tm, D), lambda i: (i, 0))
    return pl.pallas_call(
        body, name=name, grid=(T // tm,),
        in_specs=[row, pl.BlockSpec((1, D), lambda i: (0, 0)),
                  _resident((N_CHIPS, None, D, cw), lambda i: (0, 0, 0, 0)),
                  _resident((N_CHIPS, None, cw, D), lambda i: (0, 0, 0, 0))],
        out_specs=[row, pl.BlockSpec((tm, N_CHIPS * cw), lambda i: (i, 0)), row],
        out_shape=[jax.ShapeDtypeStruct((T, D), BF16), jax.ShapeDtypeStruct((T, N_CHIPS * cw), BF16),
                   jax.ShapeDtypeStruct((T, D), F32)],
        compiler_params=_params(1))(h, g, wup, wdown)


def _mlp_bwd(name, dh, dh16, a, wdown, wup, h_mid, g, tm, deps=()):
    T, D = dh.shape
    cw = wup.shape[3]
    F = N_CHIPS * cw

    def body(dh_ref, dh16_ref, a_ref, wd_ref, wu_ref, h_ref, g_ref, *rest):
        da_ref, out_ref, out16_ref, dg_ref = rest[len(deps):]
        d16 = dh16_ref[...]
        acc = None
        for ch in range(N_CHIPS):
            cols = slice(ch * cw, (ch + 1) * cw)
            da = (_dot_nt(d16, wd_ref[ch]) * (2.0 * jnp.maximum(a_ref[:, cols].astype(F32), 0.0))).astype(BF16)
            da_ref[:, cols] = da
            d = _dot_nt(da, wu_ref[ch])
            acc = d if acc is None else acc + d
        dh_c, dg = _rms_bwd(h_ref[...], g_ref[...], acc)
        out = dh_ref[...] + dh_c
        out_ref[...] = out
        out16_ref[...] = out.astype(BF16)

        @pl.when(pl.program_id(0) == 0)
        def _():
            dg_ref[...] = dg

        @pl.when(pl.program_id(0) > 0)
        def _():
            dg_ref[...] += dg

    row = pl.BlockSpec((tm, D), lambda i: (i, 0))
    wide = pl.BlockSpec((tm, F), lambda i: (i, 0))
    vec = pl.BlockSpec((1, D), lambda i: (0, 0))
    return pl.pallas_call(
        body, name=name, grid=(T // tm,),
        in_specs=[row, row, wide, _resident((N_CHIPS, None, cw, D), lambda i: (0, 0, 0, 0)),
                  _resident((N_CHIPS, None, D, cw), lambda i: (0, 0, 0, 0)), row, vec] + [ANY] * len(deps),
        out_specs=[wide, row, row, vec],
        out_shape=[jax.ShapeDtypeStruct((T, F), BF16), jax.ShapeDtypeStruct((T, D), F32),
                   jax.ShapeDtypeStruct((T, D), BF16), jax.ShapeDtypeStruct((1, D), F32)],
        compiler_params=_params(1))(dh, dh16, a, wdown, wup, h_mid, g, *deps)


CONV_ROWS = 256
CONV_HALO = 16


def _conv_shifted(ext, k, r0, rows):
    rolled = pltpu.roll(ext, k, 0)[CONV_HALO:]
    t = r0 + lax.broadcasted_iota(jnp.int32, rolled.shape, 0)
    return jnp.where(t >= k, rolled, 0.0)


def _conv_ahead(ext, k, r0, rows, S):
    rolled = pltpu.roll(ext, rows + CONV_HALO - k, 0)[:rows]
    t = r0 + lax.broadcasted_iota(jnp.int32, rolled.shape, 0)
    return jnp.where(t + k < S, rolled, 0.0)


def _conv_fwd(name, bcu, cwg, layer, tc):
    _, B, S, D = bcu.shape
    cwc = cwg.shape[3]
    per_chunk = cwc // tc
    R = min(CONV_ROWS, S)

    def body(x_ref, w_ref, z_ref):
        w = [w_ref[k:k + 1, :] for k in range(3)]

        def step(i, carry):
            r0 = pl.multiple_of(i * R, R)
            h0 = pl.multiple_of(jnp.maximum(r0 - CONV_HALO, 0), CONV_HALO)
            ld = lambda p, start, rows: x_ref[p, pl.ds(start, rows), :].astype(F32)
            cu = jnp.concatenate([ld(1, h0, CONV_HALO) * ld(2, h0, CONV_HALO), ld(1, r0, R) * ld(2, r0, R)], axis=0)
            conv = w[0] * cu[CONV_HALO:]
            conv = conv + w[1] * _conv_shifted(cu, 1, r0, R)
            conv = conv + w[2] * _conv_shifted(cu, 2, r0, R)
            z_ref[pl.ds(r0, R), :] = (ld(0, r0, R) * conv).astype(BF16)
            return carry

        lax.fori_loop(0, S // R, step, 0)

    return pl.pallas_call(
        body, name=name, grid=(B, D // tc),
        in_specs=[pl.BlockSpec((3, None, S, tc), lambda b, j: (0, b, 0, j)),
                  pl.BlockSpec((None, None, 3, tc), lambda b, j: (j // per_chunk, layer, 0, j % per_chunk))],
        out_specs=pl.BlockSpec((None, S, tc), lambda b, j: (b, 0, j)),
        out_shape=jax.ShapeDtypeStruct((B, S, D), BF16),
        compiler_params=_params(2))(bcu, cwg)


def _conv_bwd(name, bcu, dz, cwg, layer, tc):
    _, B, S, D = bcu.shape
    cwc = cwg.shape[3]
    per_chunk = cwc // tc
    R = min(CONV_ROWS, S)

    def body(x_ref, dz_ref, w_ref, d_ref, dw_ref):
        w = [w_ref[k:k + 1, :] for k in range(3)]

        @pl.when(pl.program_id(1) == 0)
        def _():
            dw_ref[...] = jnp.zeros_like(dw_ref)

        def step(i, carry):
            r0 = pl.multiple_of(i * R, R)
            h0 = pl.multiple_of(jnp.maximum(r0 - CONV_HALO, 0), CONV_HALO)
            a0 = pl.multiple_of(jnp.minimum(r0 + R, S - CONV_HALO), CONV_HALO)
            ld = lambda p, start, rows: x_ref[p, pl.ds(start, rows), :].astype(F32)
            b, c, u = ld(0, r0, R), ld(1, r0, R), ld(2, r0, R)
            dz = dz_ref[pl.ds(r0, R), :]
            cu = jnp.concatenate([ld(1, h0, CONV_HALO) * ld(2, h0, CONV_HALO), c * u], axis=0)
            cu1 = _conv_shifted(cu, 1, r0, R)
            cu2 = _conv_shifted(cu, 2, r0, R)
            conv = w[0] * (c * u) + w[1] * cu1 + w[2] * cu2
            dconv = dz * b
            dca = jnp.concatenate([dconv, dz_ref[pl.ds(a0, CONV_HALO), :] * ld(0, a0, CONV_HALO)], axis=0)
            dcu = w[0] * dconv + w[1] * _conv_ahead(dca, 1, r0, R, S) + w[2] * _conv_ahead(dca, 2, r0, R, S)
            d_ref[0, pl.ds(r0, R), :] = (dz * conv).astype(BF16)
            d_ref[1, pl.ds(r0, R), :] = (dcu * u).astype(BF16)
            d_ref[2, pl.ds(r0, R), :] = (dcu * c).astype(BF16)
            return (carry[0] + jnp.sum(dconv * (c * u), axis=0, keepdims=True),
                    carry[1] + jnp.sum(dconv * cu1, axis=0, keepdims=True),
                    carry[2] + jnp.sum(dconv * cu2, axis=0, keepdims=True))

        zero = jnp.zeros((1, tc), F32)
        s0, s1, s2 = lax.fori_loop(0, S // R, step, (zero, zero, zero))
        for k, sk in enumerate((s0, s1, s2)):
            dw_ref[k:k + 1, :] += sk

    return pl.pallas_call(
        body, name=name, grid=(D // tc, B),
        in_specs=[pl.BlockSpec((3, None, S, tc), lambda j, b: (0, b, 0, j)),
                  pl.BlockSpec((None, S, tc), lambda j, b: (b, 0, j)),
                  pl.BlockSpec((None, None, 3, tc), lambda j, b: (j // per_chunk, layer, 0, j % per_chunk))],
        out_specs=[pl.BlockSpec((3, None, S, tc), lambda j, b: (0, b, 0, j)),
                   pl.BlockSpec((3, tc), lambda j, b: (0, j))],
        out_shape=[jax.ShapeDtypeStruct((3, B, S, D), BF16), jax.ShapeDtypeStruct((3, D), F32)],
        compiler_params=_params(2))(bcu, dz, cwg)


def _att_rows(dil, idx, nb):
    r, n = idx // nb, idx % nb
    if dil == 1:
        cur = pl.ds(pl.multiple_of(n * ATT_BLK, ATT_BLK), ATT_BLK)
        prev = pl.ds(pl.multiple_of(jnp.maximum(n - 1, 0) * ATT_BLK, ATT_BLK), ATT_BLK)
    else:
        cur = pl.ds(n * (ATT_BLK * dil) + r, ATT_BLK, stride=dil)
        prev = pl.ds(jnp.maximum(n - 1, 0) * (ATT_BLK * dil) + r, ATT_BLK, stride=dil)
    return n, cur, prev


def _att_bias(bias_ref, dil, sl_ref, hp):
    row = lax.broadcasted_iota(jnp.int32, (2 * ATT_BLK, 2 * ATT_BLK), 0)
    ci = lax.broadcasted_iota(jnp.int32, (2 * ATT_BLK, 2 * ATT_BLK), 1)
    j = ATT_BLK + (row & (ATT_BLK - 1)) - ci
    slope = jnp.where(row < ATT_BLK, sl_ref[2 * hp], sl_ref[2 * hp + 1])
    rest = jnp.where((j >= 0) & (j <= ATT_BLK), -slope * (dil * j).astype(F32), NEG_INF)
    bias_ref[1] = rest
    bias_ref[0] = jnp.where(ci >= ATT_BLK, rest, NEG_INF)


def _stack_heads(x16, lane):
    first = lane < HEAD_DIM
    return jnp.concatenate([jnp.where(first, x16, jnp.zeros_like(x16)),
                            jnp.where(first, jnp.zeros_like(x16), x16)], axis=0)


def _per_head(col, lane):
    return jnp.where(lane < HEAD_DIM, col[:ATT_BLK], col[ATT_BLK:])


def _attn_fwd(name, q, kv, slopes, n_heads):
    B, S, CQ = q.shape
    HP = n_heads * HEAD_DIM // LANES
    scale = HEAD_DIM ** -0.5
    n_groups = len(PATTERNS)
    CH = 256

    def body(sl_ref, q_ref, k_ref, v_ref, o_ref, lse_ref, bias_ref, *parts):
        og, lg = parts[:n_groups], parts[n_groups:]
        hp, g = pl.program_id(1), pl.program_id(2)
        lane = lax.broadcasted_iota(jnp.int32, (1, LANES), 1)

        for gi, (window, dil) in enumerate(PATTERNS):
            nb = S // dil // ATT_BLK

            @pl.when(g == gi)
            def _(gi=gi, dil=dil, nb=nb):
                _att_bias(bias_ref, dil, sl_ref, hp)

                def step(idx, carry):
                    n, cur, prev = _att_rows(dil, idx, nb)
                    qs = _stack_heads((q_ref[cur, :] * scale).astype(BF16), lane)
                    kc = jnp.concatenate([k_ref[prev, :], k_ref[cur, :]], axis=0).astype(BF16)
                    vc = jnp.concatenate([v_ref[prev, :], v_ref[cur, :]], axis=0).astype(BF16)
                    s = _dot_nt(qs, kc) + bias_ref[jnp.minimum(n, 1)]
                    m = jnp.max(s, axis=-1, keepdims=True)
                    p = jnp.exp(s - m)
                    l = jnp.sum(p, axis=-1, keepdims=True)
                    p16 = p.astype(BF16)
                    o_un = _dot(jnp.concatenate([p16[:ATT_BLK], p16[ATT_BLK:]], axis=1), _stack_heads_rows(vc, lane))
                    og[gi][cur, :] = o_un / _per_head(l, lane)
                    lg[gi][cur, :] = _per_head(m + jnp.log(l), lane)
                    return carry

                lax.fori_loop(0, S // ATT_BLK, step, 0, unroll=8)

        @pl.when(g == n_groups - 1)
        def _():
            def comb(i, carry):
                rows = pl.ds(pl.multiple_of(i * CH, CH), CH)
                a, b, c = lg[0][rows, :], lg[1][rows, :], lg[2][rows, :]
                m = jnp.maximum(jnp.maximum(a, b), c)
                ea, eb, ec = jnp.exp(a - m), jnp.exp(b - m), jnp.exp(c - m)
                z = ea + eb + ec
                o_ref[rows, :] = (ea / z) * og[0][rows, :] + (eb / z) * og[1][rows, :] + (ec / z) * og[2][rows, :]
                lse_ref[rows, :] = m + jnp.log(z)
                return carry

            lax.fori_loop(0, S // CH, comb, 0)

    blk = (None, S, LANES)
    out = pl.BlockSpec(blk, lambda b, hp, g: (b, 0, hp))
    return pl.pallas_call(
        body, name=name, grid=(B, HP, n_groups),
        in_specs=[pl.BlockSpec(memory_space=pltpu.SMEM),
                  pl.BlockSpec(blk, lambda b, hp, g: (b, 0, g * HP + hp)),
                  pl.BlockSpec(blk, lambda b, hp, g: (b, 0, g * 2 * HP + hp)),
                  pl.BlockSpec(blk, lambda b, hp, g: (b, 0, g * 2 * HP + HP + hp))],
        out_specs=[out, out],
        out_shape=[jax.ShapeDtypeStruct((B, S, HP * LANES), F32)] * 2,
        scratch_shapes=[pltpu.VMEM((2, 2 * ATT_BLK, 2 * ATT_BLK), F32)] + [pltpu.VMEM((S, LANES), F32)] * (2 * n_groups),
        compiler_params=_params(3))(slopes, q, kv, kv)


def _stack_heads_rows(x16, lane):
    first = lane < HEAD_DIM
    return jnp.concatenate([jnp.where(first, x16, jnp.zeros_like(x16)),
                            jnp.where(first, jnp.zeros_like(x16), x16)], axis=0)


def _attn_bwd(name, q, kv, slopes, o, lse, do, n_heads, dkv_prev):
    B, S, CQ = q.shape
    HP = n_heads * HEAD_DIM // LANES
    scale = HEAD_DIM ** -0.5
    n_groups = len(PATTERNS)
    n_prev = 0 if dkv_prev is None else 2

    def body(sl_ref, q_ref, k_ref, v_ref, o_ref, lse_ref, do_ref, *rest):
        dq_ref, dk_ref, dv_ref, bias_ref = rest[n_prev:]
        hp, g = pl.program_id(1), pl.program_id(2)
        lane = lax.broadcasted_iota(jnp.int32, (1, LANES), 1)
        first = lane < HEAD_DIM

        def flush(rows, dk, dv):
            if n_prev:
                dk = dk + rest[0][rows, :]
                dv = dv + rest[1][rows, :]
            dk_ref[rows, :] = dk
            dv_ref[rows, :] = dv

        for gi, (window, dil) in enumerate(PATTERNS):
            nb = S // dil // ATT_BLK
            n_blocks = S // ATT_BLK

            @pl.when(g == gi)
            def _(dil=dil, nb=nb, n_blocks=n_blocks):
                _att_bias(bias_ref, dil, sl_ref, hp)

                def block(idx, carry, first_of_all):
                    n, cur, prev = _att_rows(dil, idx, nb)
                    qs = _stack_heads((q_ref[cur, :] * scale).astype(BF16), lane)
                    kc = jnp.concatenate([k_ref[prev, :], k_ref[cur, :]], axis=0).astype(BF16)
                    vc = jnp.concatenate([v_ref[prev, :], v_ref[cur, :]], axis=0).astype(BF16)
                    dob = do_ref[cur, :]
                    prod = dob * o_ref[cur, :]
                    lseb = lse_ref[cur, :]
                    dos = _stack_heads(dob.astype(BF16), lane)
                    delta = jnp.concatenate(
                        [jnp.sum(jnp.where(first, prod, 0.0), axis=-1, keepdims=True),
                         jnp.sum(jnp.where(first, 0.0, prod), axis=-1, keepdims=True)], axis=0)
                    lse_col = jnp.concatenate(
                        [jnp.max(jnp.where(first, lseb, -jnp.inf), axis=-1, keepdims=True),
                         jnp.max(jnp.where(first, -jnp.inf, lseb), axis=-1, keepdims=True)], axis=0)
                    s = _dot_nt(qs, kc) + bias_ref[jnp.minimum(n, 1)]
                    p = jnp.exp(s - lse_col)
                    ds = p * (_dot_nt(dos, vc) - delta)
                    ds16 = ds.astype(BF16)
                    dq = _dot(jnp.concatenate([ds16[:ATT_BLK], ds16[ATT_BLK:]], axis=1), _stack_heads_rows(kc, lane))
                    dq_ref[cur, :] = dq * scale
                    dk = _dot_tn(ds16, qs)
                    dv = _dot_tn(p.astype(BF16), dos)

                    def flush_before():
                        _, before, _ = _att_rows(dil, idx - 1, nb)
                        flush(before, carry[0] + dk[:ATT_BLK], carry[1] + dv[:ATT_BLK])

                    if first_of_all:
                        pl.when(idx > 0)(flush_before)
                    else:
                        flush_before()
                    return dk[ATT_BLK:], dv[ATT_BLK:]

                def step(i, carry):
                    for u in range(BWD_UNROLL):
                        carry = block(i * BWD_UNROLL + u, carry, u == 0)
                    return carry

                zero = jnp.zeros((ATT_BLK, LANES), F32)
                dk_last, dv_last = lax.fori_loop(0, n_blocks // BWD_UNROLL, step, (zero, zero))
                _, last, _ = _att_rows(dil, n_blocks - 1, nb)
                flush(last, dk_last, dv_last)

    blk = (None, S, LANES)
    shared = pl.BlockSpec(blk, lambda b, hp, g: (b, 0, hp))
    grouped = pl.BlockSpec(blk, lambda b, hp, g: (b, 0, g * HP + hp))
    prev = [] if dkv_prev is None else list(dkv_prev)
    gshape = jax.ShapeDtypeStruct((B, S, n_groups * HP * LANES), F32)
    return pl.pallas_call(
        body, name=name, grid=(B, HP, n_groups),
        in_specs=[pl.BlockSpec(memory_space=pltpu.SMEM), grouped,
                  pl.BlockSpec(blk, lambda b, hp, g: (b, 0, g * 2 * HP + hp)),
                  pl.BlockSpec(blk, lambda b, hp, g: (b, 0, g * 2 * HP + HP + hp)),
                  shared, shared, shared] + [grouped] * n_prev,
        out_specs=[grouped] * 3, out_shape=[gshape] * 3,
        scratch_shapes=[pltpu.VMEM((2, 2 * ATT_BLK, 2 * ATT_BLK), F32)],
        compiler_params=_params(3))(slopes, q, kv, kv, o, lse, do, *prev)


def _final_loss(name, h, g, target, tm):
    T, D = h.shape

    def body(h_ref, g_ref, t_ref, loss_ref, dh_ref, dh16_ref, dg_ref):
        hf = h_ref[...]
        gv = g_ref[...]
        rstd = lax.rsqrt(jnp.mean(hf * hf, axis=-1, keepdims=True) + EPS)
        xhat = hf * rstd
        err = xhat * gv - t_ref[...]
        part = 0.5 * jnp.sum(jnp.mean(err * err, axis=-1, keepdims=True), axis=0, keepdims=True)
        dy = err * (1.0 / D)
        dg = jnp.sum(dy * xhat, axis=0, keepdims=True)
        dx = dy * gv
        dh = rstd * (dx - xhat * jnp.mean(dx * xhat, axis=-1, keepdims=True))
        dh_ref[...] = dh
        dh16_ref[...] = dh.astype(BF16)

        @pl.when(pl.program_id(0) == 0)
        def _():
            loss_ref[...] = part
            dg_ref[...] = dg

        @pl.when(pl.program_id(0) > 0)
        def _():
            loss_ref[...] += part
            dg_ref[...] += dg

    return pl.pallas_call(
        body, name=name, grid=(T // tm,),
        in_specs=[pl.BlockSpec((tm, D), lambda i: (i, 0)), pl.BlockSpec((1, D), lambda i: (0, 0)),
                  pl.BlockSpec((tm, D), lambda i: (i, 0))],
        out_specs=[pl.BlockSpec((1, 1), lambda i: (0, 0)), pl.BlockSpec((tm, D), lambda i: (i, 0)),
                   pl.BlockSpec((tm, D), lambda i: (i, 0)), pl.BlockSpec((1, D), lambda i: (0, 0))],
        out_shape=[jax.ShapeDtypeStruct((1, 1), F32), jax.ShapeDtypeStruct((T, D), F32),
                   jax.ShapeDtypeStruct((T, D), BF16), jax.ShapeDtypeStruct((1, D), F32)],
        compiler_params=_params(1))(h, g, target)


def _nt_rows(name, dh, wg, layer, a_mul, out_dtype, tm, deps=()):
    T, D = dh.shape
    rk = wg.shape[2]
    N = N_CHIPS * rk
    with_a = a_mul is not None

    def body(dh_ref, w_ref, *rest):
        o_ref = rest[-1]
        d16 = dh_ref[...]
        for ch in range(N_CHIPS):
            r = _dot_nt(d16, w_ref[ch])
            if with_a:
                r = r * (2.0 * jnp.maximum(rest[0][:, ch * rk:(ch + 1) * rk].astype(F32), 0.0))
            o_ref[:, ch * rk:(ch + 1) * rk] = r.astype(out_dtype)

    in_specs = [pl.BlockSpec((tm, D), lambda i: (i, 0)),
                pl.BlockSpec((N_CHIPS, None, rk, D), lambda i: (0, layer, 0, 0))]
    args = [dh, wg]
    if with_a:
        in_specs.append(pl.BlockSpec((tm, N), lambda i: (i, 0)))
        args.append(a_mul)
    in_specs += [ANY] * len(deps)
    args += list(deps)
    return pl.pallas_call(
        body, name=name, grid=(T // tm,), in_specs=in_specs,
        out_specs=pl.BlockSpec((tm, N), lambda i: (i, 0)),
        out_shape=jax.ShapeDtypeStruct((T, N), out_dtype),
        compiler_params=_params(1))(*args)


def _nt_cols(name, ysegs, wg, layer, tm, norm):
    Nw, cw = wg.shape[2], wg.shape[3]
    widths = [bs[-1] for _, bs, _ in ysegs]
    pieces = _pieces(widths, cw, 1024)
    ns = len(ysegs)
    T = norm[0].shape[0] if norm is not None else ysegs[0][0].shape[-2]

    def body(*refs):
        y_refs = refs[:ns]
        w_ref = refs[ns]
        acc = refs[-1]
        for n, (s, a0, ch, b0, wd) in enumerate(pieces):
            d = _dot_nt(y_refs[s][:, a0:a0 + wd].astype(BF16), w_ref[ch, :, b0:b0 + wd])
            if n == 0:
                acc[...] = d
            else:
                acc[...] += d
        if norm is None:
            refs[ns + 1][...] = acc[...]
        else:
            h_ref, g_ref, dhin_ref, out_ref, out16_ref, dg_ref = refs[ns + 1:ns + 7]
            dh_c, dg = _rms_bwd(h_ref[...], g_ref[...], acc[...])
            dh = dhin_ref[...] + dh_c
            out_ref[...] = dh
            out16_ref[...] = dh.astype(BF16)

            @pl.when(pl.program_id(0) == 0)
            def _():
                dg_ref[...] = dg

            @pl.when(pl.program_id(0) > 0)
            def _():
                dg_ref[...] += dg

    in_specs = [pl.BlockSpec(bs, im) for _, bs, im in ysegs]
    in_specs.append(pl.BlockSpec((N_CHIPS, None, Nw, cw), lambda i: (0, layer, 0, 0)))
    args = [a for a, _, _ in ysegs] + [wg]
    row = pl.BlockSpec((tm, Nw), lambda i: (i, 0))
    vec = pl.BlockSpec((1, Nw), lambda i: (0, 0))
    if norm is None:
        out_specs = row
        out_shape = jax.ShapeDtypeStruct((T, Nw), F32)
    else:
        in_specs += [row, vec, row]
        args += list(norm)
        out_specs = [row, row, vec]
        out_shape = [jax.ShapeDtypeStruct((T, Nw), F32), jax.ShapeDtypeStruct((T, Nw), BF16),
                     jax.ShapeDtypeStruct((1, Nw), F32)]
    return pl.pallas_call(
        body, name=name, grid=(T // tm,), in_specs=in_specs, out_specs=out_specs, out_shape=out_shape,
        scratch_shapes=[pltpu.VMEM((tm, Nw), F32)], compiler_params=_params(1))(*args)


def _tn(name, x, x_act, ysegs, cw, cols_layout, tmm, tt, deps=(), out_dtype=F32):
    T, M = x.shape
    widths = [bs[-1] for _, bs, _ in ysegs]
    N = sum(widths)
    pieces = _pieces(widths, cw if cols_layout else N, 1024)
    ns = len(ysegs)
    n_t = T // tt
    block = (N_CHIPS, tmm, cw) if cols_layout else (tmm, N)
    narrow = out_dtype != F32

    def body(x_ref, *refs):
        y_refs = refs[:ns]
        o_ref = refs[ns + len(deps)]
        acc = refs[-1] if narrow else o_ref

        @pl.when(pl.program_id(1) == 0)
        def _():
            acc[...] = jnp.zeros_like(acc)

        xt = x_act(x_ref[...])
        for s, a0, ch, b0, wd in pieces:
            d = _dot_tn(xt, y_refs[s][:, a0:a0 + wd].astype(BF16))
            if cols_layout:
                acc[ch, :, b0:b0 + wd] += d
            else:
                acc[:, b0:b0 + wd] += d
        if narrow:
            @pl.when(pl.program_id(1) == n_t - 1)
            def _():
                o_ref[...] = acc[...].astype(out_dtype)

    in_specs = [pl.BlockSpec((tt, tmm), lambda m, t: (t, m))] + [pl.BlockSpec(bs, im) for _, bs, im in ysegs]
    in_specs += [ANY] * len(deps)
    if cols_layout:
        out_specs = pl.BlockSpec(block, lambda m, t: (0, m, 0))
        out_shape = jax.ShapeDtypeStruct((N_CHIPS, M, cw), out_dtype)
    else:
        out_specs = pl.BlockSpec(block, lambda m, t: (m, 0))
        out_shape = jax.ShapeDtypeStruct((M, N), out_dtype)
    return pl.pallas_call(
        body, name=name, grid=(M // tmm, n_t), in_specs=in_specs, out_specs=out_specs, out_shape=out_shape,
        scratch_shapes=[pltpu.VMEM(block, F32)] if narrow else [],
        compiler_params=_params(2))(x, *[a for a, _, _ in ysegs], *deps)


def _seg2d(a, t_rows, grid_rank):
    w = a.shape[1]
    if grid_rank == 1:
        return (a, (t_rows, w), lambda i: (i, 0))
    return (a, (t_rows, w), lambda m, t: (t, 0))


def _kv_segments(dk, dv, C, t_rows, grid_rank):
    segs = []
    for g in range(len(PATTERNS)):
        for a in (dk, dv):
            if grid_rank == 1:
                segs.append((a, (t_rows, C), lambda i, g=g: (i, g)))
            else:
                segs.append((a, (t_rows, C), lambda m, t, g=g: (t, g)))
    return segs


def _seg_plane(a, plane, t_rows, grid_rank):
    w = a.shape[2]
    if grid_rank == 1:
        return (a, (None, t_rows, w), lambda i: (plane, i, 0))
    return (a, (None, t_rows, w), lambda m, t: (plane, t, 0))


def _row_tile(rows, row_bytes, budget_bytes=2 * 1024 * 1024):
    t = rows
    while t * row_bytes > budget_bytes and t % 32 == 0:
        t //= 2
    return t


N_DEVICES = 8


def _device_add(name, own, slots, place):
    _, _, hr, c = own.shape
    tr = _row_tile(hr, c * 4, 1024 * 1024)

    def body(place_ref, own_ref, *refs):
        o_ref = refs[-1]
        acc = own_ref[...].astype(F32)
        for r in refs[:-1]:
            acc = acc + r[...].astype(F32)
        o_ref[...] = acc

    def slot(k):
        return pl.BlockSpec((None, tr, c), lambda i, pr: ((2 * pr[0] + pr[1] + k) % N_DEVICES, i, 0))

    grid_spec = pltpu.PrefetchScalarGridSpec(
        num_scalar_prefetch=1, grid=(hr // tr,),
        in_specs=[pl.BlockSpec((None, None, tr, c), lambda i, pr: (pr[0], pr[1], i, 0))]
        + [slot(k) for k in range(1, N_DEVICES)],
        out_specs=pl.BlockSpec((None, tr, c), lambda i, pr: (pr[1], i, 0)))
    return pl.pallas_call(body, name=name, grid_spec=grid_spec,
                          out_shape=jax.ShapeDtypeStruct((2, hr, c), F32),
                          compiler_params=_params(1))(place, own, *[slots] * (N_DEVICES - 1))


def _adamw(name, w, g, m, v):
    rows, cols = w.shape
    tr = _row_tile(rows, cols * 4, 1024 * 1024)

    def body(w_ref, g_ref, m_ref, v_ref, d_ref, nm_ref, nv_ref):
        d_ref[...], nm_ref[...], nv_ref[...] = _adamw_math(w_ref[...], g_ref[...], m_ref[...], v_ref[...])

    spec = pl.BlockSpec((tr, cols), lambda i: (i, 0))
    return pl.pallas_call(
        body, name=name, grid=(rows // tr,), in_specs=[spec] * 4, out_specs=[spec] * 3,
        out_shape=[jax.ShapeDtypeStruct((rows, cols), F32)] * 3, compiler_params=_params(1))(w, g, m, v)


def _adamw_math(w, g, m, v):
    nm = ADAM_B1 * m + (1.0 - ADAM_B1) * g
    nv = ADAM_B2 * v + (1.0 - ADAM_B2) * jnp.square(g)
    m_hat = nm / (1.0 - ADAM_B1 ** ADAM_STEP)
    v_hat = nv / (1.0 - ADAM_B2 ** ADAM_STEP)
    return -ADAM_LR * (m_hat / (jnp.sqrt(v_hat) + ADAM_EPS) + ADAM_WD * w), nm, nv


def _adamw_layers(name, w, grads, m, v):
    L, r, c = w.shape
    tr = _row_tile(r, L * c * 4, 1024 * 1024)

    def body(*refs):
        w_ref, m_ref, v_ref = refs[:3]
        g_refs = refs[3:3 + L]
        go_ref, d_ref, nm_ref, nv_ref = refs[3 + L:]
        for l in range(L):
            g = g_refs[l][...]
            go_ref[l] = g
            d_ref[l], nm_ref[l], nv_ref[l] = _adamw_math(w_ref[l], g, m_ref[l], v_ref[l])

    stacked = pl.BlockSpec((L, tr, c), lambda i: (0, i, 0))
    return pl.pallas_call(
        body, name=name, grid=(r // tr,),
        in_specs=[stacked] * 3 + [pl.BlockSpec((tr, c), lambda i: (i, 0))] * L, out_specs=[stacked] * 4,
        out_shape=[jax.ShapeDtypeStruct((L, r, c), F32)] * 4, compiler_params=_params(1))(w, m, v, *grads)


def _place():
    x, y, c = lax.axis_index("x"), lax.axis_index("y"), lax.axis_index("c")
    chips = [(1 - x, y), (x, 1 - y), (1 - x, 1 - y)]
    return x, y, c, chips


HBM = pl.BlockSpec(memory_space=pltpu.HBM)
SEM = pl.BlockSpec(memory_space=pltpu.SEMAPHORE)
EFFECT = pltpu.SideEffectType.DATAFLOW_SIDE_EFFECTING


class _Copy:
    def __init__(self, src, src_view, land, dst_view, recv_view, target):
        self.src, self.src_view, self.land, self.dst_view, self.recv_view, self.target = (
            src, src_view, land, dst_view, recv_view, target)


def _whole(ref, place):
    return ref


def _split_start(name, srcs, land_shapes, plans):
    skeys, lkeys = list(srcs), list(land_shapes)
    ns, nl, ng = len(skeys), len(lkeys), len(plans)

    def body(*refs):
        src = dict(zip(skeys, refs[:ns]))
        land = dict(zip(lkeys, refs[ns:ns + nl]))
        sems = refs[ns + nl:ns + nl + 2 * ng]
        token = refs[-1]
        place = _place()
        for gi, plan in enumerate(plans):
            for k, cp in enumerate(plan):
                pltpu.make_async_remote_copy(
                    src_ref=cp.src_view(src[cp.src], place), dst_ref=cp.dst_view(land[cp.land], place),
                    send_sem=sems[2 * gi].at[k], recv_sem=sems[2 * gi + 1].at[k],
                    device_id=cp.target(place), device_id_type=MESH).start()
        token[...] = jnp.zeros_like(token)

    sem_shapes = []
    for plan in plans:
        sem_shapes += [pltpu.SemaphoreType.DMA((len(plan),))] * 2
    buffers = [srcs[k] for k in skeys] + [lax.empty(land_shapes[k].shape, land_shapes[k].dtype) for k in lkeys]
    outs = pl.pallas_call(
        body, name=name,
        out_shape=(*sem_shapes, *[pltpu.HBM(a.shape, a.dtype) for a in buffers], jax.ShapeDtypeStruct((8, LANES), F32)),
        in_specs=[HBM] * (ns + nl),
        out_specs=(*[SEM] * (2 * ng), *[HBM] * (ns + nl), pl.BlockSpec(memory_space=pltpu.VMEM)),
        input_output_aliases={i: 2 * ng + i for i in range(ns + nl)},
        compiler_params=pltpu.CompilerParams(has_side_effects=EFFECT),
    )(*[pltpu.with_memory_space_constraint(a, pltpu.HBM) for a in buffers])
    sems = [(outs[2 * gi], outs[2 * gi + 1]) for gi in range(ng)]
    thru = outs[2 * ng:2 * ng + ns + nl]
    return sems, dict(zip(skeys, thru[:ns])), dict(zip(lkeys, thru[ns:])), outs[-1]


def _split_wait(name, sems, srcs, lands, plan, after):
    skeys, lkeys = list(srcs), list(lands)
    ns, nl = len(skeys), len(lkeys)

    def body(*refs):
        src = dict(zip(skeys, refs[:ns]))
        land = dict(zip(lkeys, refs[ns:ns + nl]))
        ssem, rsem = refs[ns + nl], refs[ns + nl + 1]
        place = _place()
        for k, cp in enumerate(plan):
            pltpu.make_async_remote_copy(
                src_ref=cp.src_view(src[cp.src], place), dst_ref=cp.dst_view(land[cp.land], place),
                send_sem=ssem.at[k], recv_sem=rsem.at[k],
                device_id=cp.target(place), device_id_type=MESH).wait_send()
            got = cp.recv_view(land[cp.land], place)
            pltpu.make_async_remote_copy(
                src_ref=got, dst_ref=got, send_sem=ssem.at[k], recv_sem=rsem.at[k],
                device_id=cp.target(place), device_id_type=MESH).wait_recv()

    buffers = [srcs[k] for k in skeys] + [lands[k] for k in lkeys]
    outs = pl.pallas_call(
        body, name=name, out_shape=tuple(pltpu.HBM(a.shape, a.dtype) for a in buffers),
        in_specs=(*[HBM] * (ns + nl), SEM, SEM, ANY), out_specs=tuple([HBM] * (ns + nl)),
        input_output_aliases={i: i for i in range(ns + nl)},
        compiler_params=pltpu.CompilerParams(has_side_effects=EFFECT),
    )(*buffers, sems[0], sems[1], after)
    return dict(zip(skeys, outs[:ns])), dict(zip(lkeys, outs[ns:]))


def _chip_of(place):
    x, y, c, chips = place
    return 2 * x + y


class _WeightGather:
    def __init__(self, blocks):
        self.plans, shapes = {}, {}
        for key, a in blocks.items():
            shapes[key] = jax.ShapeDtypeStruct((N_CHIPS,) + a.shape, a.dtype)
            slot = lambda ref, place: ref.at[_chip_of(place)]
            plan = [_Copy(key, _whole, key, slot,
                          lambda ref, place, k=k: ref.at[2 * place[3][k][0] + place[3][k][1]],
                          lambda place, k=k: (place[3][k][0], place[3][k][1], place[2])) for k in range(3)]
            plan.append(_Copy(key, _whole, key, slot, slot, lambda place: (place[0], place[1], 1 - place[2])))
            self.plans[key] = plan
        sems, self.srcs, self.lands, self.token = _split_start("gather_start", blocks, shapes, list(self.plans.values()))
        self.sems = dict(zip(self.plans, sems))

    def get(self, l, name, after):
        key = (l, name)
        _, lands = _split_wait(f"gather_wait_{name}{l}", self.sems[key], {key: self.srcs[key]},
                               {key: self.lands[key]}, self.plans[key], after)
        return lands[key][:, None]


class _GradReduce:
    def __init__(self, place):
        self.place = place
        self.jobs = []
        self.done = {}
        self.n = 0

    def submit(self, grads):
        views = {k: a.reshape(N_CHIPS, 2, a.shape[1] // 2, a.shape[2]) for k, a in grads.items()}
        shapes = {k: jax.ShapeDtypeStruct((N_DEVICES,) + a.shape[2:], a.dtype) for k, a in views.items()}

        def peer(place, k):
            x, y, c, _ = place
            return (1 - x if k & 4 else x, 1 - y if k & 2 else y, 1 - c if k & 1 else c)

        def index(dev):
            return 4 * dev[0] + 2 * dev[1] + dev[2]

        plan = []
        for key in views:
            for k in range(1, N_DEVICES):
                plan.append(_Copy(
                    key, lambda ref, place, k=k: ref.at[2 * peer(place, k)[0] + peer(place, k)[1], peer(place, k)[2]],
                    key, lambda ref, place: ref.at[index(place[:3])],
                    lambda ref, place, k=k: ref.at[index(peer(place, k))],
                    lambda place, k=k: peer(place, k)))
        sems, srcs, lands, token = _split_start(f"grad_start{self.n}", views, shapes, [plan])
        self.jobs.append(dict(id=self.n, sems=sems[0], srcs=srcs, lands=lands, plan=plan))
        self.n += 1
        return token

    def pump(self, after):
        for job in self.jobs:
            srcs, lands = _split_wait(f"grad_wait{job['id']}", job["sems"], job["srcs"], job["lands"], job["plan"],
                                      after)
            for i, k in enumerate(srcs):
                self.done[k] = _device_add(f"grad_add{job['id']}_{i}", srcs[k], lands[k], self.place)
        self.jobs = []
        return []

    def finish(self, after):
        self.pump(after)
        return self.done


def _pair_share(halves):
    n = len(halves)

    def body(*refs):
        outs = refs[n:2 * n]
        ssem, rsem = refs[2 * n:]
        x, y, c, _ = _place()
        sends = []
        for t in range(n):
            cp = pltpu.make_async_remote_copy(
                src_ref=outs[t].at[c], dst_ref=outs[t].at[c], send_sem=ssem.at[t], recv_sem=rsem.at[t],
                device_id=(x, y, 1 - c), device_id_type=MESH)
            cp.start()
            sends.append(cp)
        for t in range(n):
            theirs = outs[t].at[1 - c]
            pltpu.make_async_remote_copy(
                src_ref=theirs, dst_ref=theirs, send_sem=ssem.at[t], recv_sem=rsem.at[t],
                device_id=(x, y, 1 - c), device_id_type=MESH).wait_recv()
        for cp in sends:
            cp.wait_send()

    return pl.pallas_call(
        body, name="grad_pair_share", in_specs=[ANY] * n, out_specs=[ANY] * n,
        out_shape=[jax.ShapeDtypeStruct(a.shape, a.dtype) for a in halves],
        input_output_aliases={t: t for t in range(n)},
        scratch_shapes=[pltpu.SemaphoreType.DMA((n,)), pltpu.SemaphoreType.DMA((n,))])(*halves)


def _small_allreduce(part):
    R, C = part.shape
    N_DEV = 8

    def body(in_ref, out_ref, slots, ssem, rsem):
        x, y, c, _ = _place()
        me = 4 * x + 2 * y + c
        sends = []
        for k in range(1, N_DEV):
            kx, ky, kc = (k >> 2) & 1, (k >> 1) & 1, k & 1
            peer = (1 - x if kx else x, 1 - y if ky else y, 1 - c if kc else c)
            cp = pltpu.make_async_remote_copy(
                src_ref=in_ref, dst_ref=slots.at[me], send_sem=ssem.at[k], recv_sem=rsem.at[k],
                device_id=peer, device_id_type=MESH)
            cp.start()
            sends.append(cp)
        slots[me] = in_ref[...]
        for k in range(1, N_DEV):
            kx, ky, kc = (k >> 2) & 1, (k >> 1) & 1, k & 1
            peer = (1 - x if kx else x, 1 - y if ky else y, 1 - c if kc else c)
            slot = slots.at[4 * peer[0] + 2 * peer[1] + peer[2]]
            pltpu.make_async_remote_copy(
                src_ref=slot, dst_ref=slot, send_sem=ssem.at[k], recv_sem=rsem.at[k],
                device_id=peer, device_id_type=MESH).wait_recv()
        acc = slots[0]
        for d in range(1, N_DEV):
            acc = acc + slots[d]
        out_ref[...] = acc
        for cp in sends:
            cp.wait_send()

    vm = pl.BlockSpec(memory_space=pltpu.VMEM)
    return pl.pallas_call(
        body, name="small_allreduce", in_specs=[vm], out_specs=vm,
        out_shape=jax.ShapeDtypeStruct((R, C), F32),
        scratch_shapes=[pltpu.VMEM((N_DEV, R, C), F32), pltpu.SemaphoreType.DMA((N_DEV,)),
                        pltpu.SemaphoreType.DMA((N_DEV,))])(part)


def _local_step(x, target, norm_mix, norm_mlp, norm_kv, norm_final, weights, sink, n_a, n_heads):
    B, S, D = x.shape
    T = B * S
    C = n_heads * HEAD_DIM
    depth = norm_mix.shape[0]
    slopes = 2.0 ** (-ALIBI_MAX_BIAS * jnp.arange(1, n_heads + 1, dtype=F32) / n_heads)
    tm = min(512, T)
    row = lambda v: v.reshape(1, -1)

    h = x.reshape(T, D)
    saved, Wl = [], []
    kv = nkv = h_kv = cwg = None
    for l in range(depth):
        s = {"h_in": h}
        w = {}
        Wl.append(w)
        if l < n_a:
            w["w_a_in"] = weights.get(l, "w_a_in", h)
            s["n1"], bcu = _norm_mm(f"a_in_fwd{l}", h, row(norm_mix[l]), w["w_a_in"], 0, 3, BF16, tm)
            s["bcu"] = bcu.reshape(3, B, S, D)
            if l == 0:
                cwg = weights.get(0, "conv", bcu)[:, 0, :n_a * 3].reshape(N_CHIPS, n_a, 3, -1)
            s["z"] = _conv_fwd(f"conv_fwd{l}", s["bcu"], cwg, l, LANES).reshape(T, D)
            w["w_a_out"] = weights.get(l, "w_a_out", s["z"])
            h = _mm_res_rows(f"a_out_fwd{l}", s["z"], w["w_a_out"], 0, h, _to_bf16, tm)
        else:
            i = l - n_a
            if i == 0:
                h_kv = h
                w["w_kv"] = weights.get(l, "w_kv", h)
                nkv, kv = _norm_mm("kv_fwd", h, row(norm_kv), w["w_kv"], 0, 1, F32, tm)
                kv = kv.reshape(B, S, 2 * 3 * C)
            w["w_q"] = weights.get(l, "w_q", h)
            s["n1"], q = _norm_mm(f"q_fwd{i}", h, row(norm_mix[l]), w["w_q"], 0, 1, F32, tm)
            s["q"] = q.reshape(B, S, 3 * C)
            o, lse = _attn_fwd(f"attn_fwd{i}", s["q"], kv, slopes, n_heads)
            s["o"], s["lse"] = o.reshape(T, C), lse.reshape(T, C)
            w["w_o"] = weights.get(l, "w_o", o)
            h = _mm_res_cols(f"o_fwd{i}", s["o"], w["w_o"], 0, h, tm)
        s["h_mid"] = h
        w["w_up"] = weights.get(l, "w_up", h)
        w["w_down"] = weights.get(l, "w_down", h)
        s["n2"], s["a"], h = _mlp_fwd(f"mlp_fwd{l}", h, row(norm_mlp[l]), w["w_up"], w["w_down"], tm)
        F = s["a"].shape[1]
        saved.append(s)

    loss, dh, dh16, dg_final = _final_loss("loss_head", h, row(norm_final), target.reshape(T, D), tm)

    g_mix, g_mlp = [None] * depth, [None] * depth
    g_conv = [None] * n_a
    dkv = None
    tt = min(512, T)
    deps = []
    for l in reversed(range(depth)):
        s, w = saved[l], Wl[l]
        g_down = _tn(f"down_wgrad{l}", s["a"], _relu2_bf16, [_seg2d(dh16, tt, 2)], None, False,
                     min(2048, F), tt, deps, BF16).reshape(N_CHIPS, F // N_CHIPS, D)
        da, dh, dh16, g_mlp[l] = _mlp_bwd(f"mlp_bwd{l}", dh, dh16, s["a"], w["w_down"], w["w_up"], s["h_mid"],
                                          row(norm_mlp[l]), tm)
        g_up = _tn(f"up_wgrad{l}", s["n2"], _to_bf16, [_seg2d(da, tt, 2)], F // N_CHIPS, True, D, tt, (), BF16)
        deps = sink.pump(dh) + [sink.submit({("w_up", l): g_up, ("w_down", l): g_down})]
        if l < n_a:
            g_out = _tn(f"a_out_wgrad{l}", s["z"], _to_bf16, [_seg2d(dh16, tt, 2)], None, False,
                        D, tt, deps, BF16).reshape(N_CHIPS, D // N_CHIPS, D)
            dz = _nt_rows(f"a_out_bwd{l}", dh16, w["w_a_out"], 0, None, F32, tm)
            deps = sink.pump(dz) + [sink.submit({("w_a_out", l): g_out})]
            dbcu, g_conv[l] = _conv_bwd(f"conv_bwd{l}", s["bcu"], dz.reshape(B, S, D), cwg, l, LANES)
            dbcu = dbcu.reshape(3, T, D)
            g_in = _tn(f"a_in_wgrad{l}", s["n1"], _to_bf16, [_seg_plane(dbcu, p, tt, 2) for p in range(3)],
                       3 * D // N_CHIPS, True, D, tt, deps, BF16)
            dh, dh16, g_mix[l] = _nt_cols(f"a_in_bwd{l}", [_seg_plane(dbcu, p, tm, 1) for p in range(3)],
                                          w["w_a_in"], 0, tm, (s["h_in"], row(norm_mix[l]), dh))
            mixer = {("w_a_in", l): g_in}
        else:
            i = l - n_a
            g_o = _tn(f"o_wgrad{i}", s["o"], _to_bf16, [_seg2d(dh16, tt, 2)], D // N_CHIPS, True, C, tt, deps,
                      BF16)
            do = _nt_cols(f"o_bwd{i}", [_seg2d(dh16, tm, 1)], w["w_o"], 0, tm, None)
            deps = sink.pump(do) + [sink.submit({("w_o", i): g_o})]
            dq, dk, dv = _attn_bwd(f"attn_bwd{i}", s["q"], kv, slopes, s["o"].reshape(B, S, C),
                                   s["lse"].reshape(B, S, C), do.reshape(B, S, C), n_heads, dkv)
            dkv = (dk, dv)
            dq = dq.reshape(T, 3 * C)
            g_q = _tn(f"q_wgrad{i}", s["n1"], _to_bf16, [_seg2d(dq, tt, 2)], 3 * C // N_CHIPS, True, D, tt, deps,
                      BF16)
            dh, dh16, g_mix[l] = _nt_cols(f"q_bwd{i}", [_seg2d(dq, tm, 1)], w["w_q"], 0, tm,
                                          (s["h_in"], row(norm_mix[l]), dh))
            mixer = {("w_q", i): g_q}
            if i == 0:
                dk2, dv2 = (t.reshape(T, 3 * C) for t in dkv)
                mixer[("w_kv", 0)] = _tn("kv_wgrad", nkv, _to_bf16, _kv_segments(dk2, dv2, C, tt, 2),
                                         6 * C // N_CHIPS, True, D, tt, (), BF16)
                dh, dh16, g_kv = _nt_cols("kv_bwd", _kv_segments(dk2, dv2, C, tm, 1), w["w_kv"], 0, tm,
                                          (h_kv, row(norm_kv), dh))
        deps = sink.pump(dh) + [sink.submit(mixer)]
    small = dict(norm_mix=jnp.concatenate(g_mix, axis=0), norm_mlp=jnp.concatenate(g_mlp, axis=0),
                 norm_kv=g_kv, norm_final=dg_final, conv_w=jnp.stack(g_conv))
    return loss, dh.reshape(B, S, D), small


BIG = ("w_a_in", "w_a_out", "w_kv", "w_q", "w_o", "w_up", "w_down")
CONV_PAD_ROWS = 16


def kernel(x, norm_mix, norm_mlp, w_a_in, conv_w, w_a_out, norm_kv, w_kv, w_q, w_o, w_up, w_down, norm_final, loss_target, m_norm_mix, m_norm_mlp, m_w_a_in, m_conv_w, m_w_a_out, m_norm_kv, m_w_kv, m_w_q, m_w_o, m_w_up, m_w_down, m_norm_final, v_norm_mix, v_norm_mlp, v_w_a_in, v_conv_w, v_w_a_out, v_norm_kv, v_w_kv, v_w_q, v_w_o, v_w_up, v_w_down, v_norm_final):
    D = x.shape[-1]
    w = dict(norm_mix=norm_mix, norm_mlp=norm_mlp, w_a_in=w_a_in, conv_w=conv_w, w_a_out=w_a_out, norm_kv=norm_kv,
             w_kv=w_kv[None], w_q=w_q, w_o=w_o, w_up=w_up, w_down=w_down, norm_final=norm_final)
    m = dict(norm_mix=m_norm_mix, norm_mlp=m_norm_mlp, w_a_in=m_w_a_in, conv_w=m_conv_w, w_a_out=m_w_a_out,
             norm_kv=m_norm_kv, w_kv=m_w_kv[None], w_q=m_w_q, w_o=m_w_o, w_up=m_w_up, w_down=m_w_down,
             norm_final=m_norm_final)
    v = dict(norm_mix=v_norm_mix, norm_mlp=v_norm_mlp, w_a_in=v_w_a_in, conv_w=v_conv_w, w_a_out=v_w_a_out,
             norm_kv=v_norm_kv, w_kv=v_w_kv[None], w_q=v_w_q, w_o=v_w_o, w_up=v_w_up, w_down=v_w_down,
             norm_final=v_norm_final)
    depth = norm_mix.shape[0]
    n_a, taps, cwc = conv_w.shape
    n_heads = w_o.shape[1] // HEAD_DIM

    conv_rows = jnp.zeros((CONV_PAD_ROWS, cwc), F32).at[:n_a * taps].set(conv_w.reshape(n_a * taps, cwc))
    blocks = {}
    for l in range(depth):
        if l < n_a:
            blocks[(l, "w_a_in")] = w_a_in[l].astype(BF16)
            if l == 0:
                blocks[(0, "conv")] = conv_rows
            blocks[(l, "w_a_out")] = w_a_out[l].astype(BF16)
        else:
            if l == n_a:
                blocks[(l, "w_kv")] = w_kv.astype(BF16)
            blocks[(l, "w_q")] = w_q[l - n_a].astype(BF16)
            blocks[(l, "w_o")] = w_o[l - n_a].astype(BF16)
        blocks[(l, "w_up")] = w_up[l].astype(BF16)
        blocks[(l, "w_down")] = w_down[l].astype(BF16)
    weights = _WeightGather(blocks)
    place = jnp.stack([2 * lax.axis_index("x") + lax.axis_index("y"), lax.axis_index("c")]).astype(jnp.int32)
    sink = _GradReduce(place)

    loss, grad_x, small = _local_step(x, loss_target, norm_mix, norm_mlp, norm_kv, norm_final, weights, sink,
                                      n_a, n_heads)
    loss = lax.psum(loss[0, 0], ("x", "y", "c"))

    done = sink.finish(grad_x)
    keys = list(done)
    shared = dict(zip(keys, _pair_share([done[k] for k in keys])))
    grads = {}

    packed = jnp.concatenate([small["norm_mix"], small["norm_mlp"], small["norm_kv"], small["norm_final"],
                              small["conv_w"].reshape(n_a * taps, D)], axis=0)
    pad = (-packed.shape[0]) % 8
    packed = jnp.pad(packed, ((0, pad), (0, 0)))
    total = _small_allreduce(packed)
    grads["norm_mix"] = total[:depth]
    grads["norm_mlp"] = total[depth:2 * depth]
    grads["norm_kv"] = total[2 * depth]
    grads["norm_final"] = total[2 * depth + 1]
    chip = 2 * lax.axis_index("x") + lax.axis_index("y")
    conv_full = total[2 * depth + 2:2 * depth + 2 + n_a * taps].reshape(n_a, taps, N_CHIPS, cwc)
    grads["conv_w"] = lax.dynamic_index_in_dim(conv_full, chip, axis=2, keepdims=False)

    order = ("norm_mix", "norm_mlp", "w_a_in", "conv_w", "w_a_out", "norm_kv", "w_kv", "w_q", "w_o", "w_up",
             "w_down", "norm_final")
    delta, new_m, new_v = {}, {}, {}
    vec_names = ("norm_mix", "norm_mlp", "norm_kv", "norm_final")
    rows_of = lambda a: a.reshape(-1, D)
    vw, vg, vm_, vv = (jnp.concatenate([rows_of(t[k]) for k in vec_names], axis=0) for t in (w, grads, m, v))
    vpad = (-vw.shape[0]) % 8
    padrows = lambda a: jnp.pad(a, ((0, vpad), (0, 0)))
    vd, vnm, vnv = _adamw("adamw_norms", padrows(vw), padrows(vg), padrows(vm_), padrows(vv))
    off = 0
    for k in vec_names:
        r = rows_of(w[k]).shape[0]
        delta[k] = vd[off:off + r].reshape(w[k].shape)
        new_m[k] = vnm[off:off + r].reshape(w[k].shape)
        new_v[k] = vnv[off:off + r].reshape(w[k].shape)
        off += r
    cpad = (-n_a * taps) % 8
    two_d = lambda a: jnp.pad(a.reshape(-1, cwc), ((0, cpad), (0, 0)))
    cd, cnm, cnv = _adamw("adamw_conv_w", two_d(w["conv_w"]), two_d(grads["conv_w"]), two_d(m["conv_w"]),
                          two_d(v["conv_w"]))
    delta["conv_w"], new_m["conv_w"], new_v["conv_w"] = (t[:n_a * taps].reshape(conv_w.shape) for t in (cd, cnm, cnv))
    for k in BIG:
        per_layer = [shared[(k, l)].reshape(w[k].shape[1:]) for l in range(w[k].shape[0])]
        grads[k], delta[k], new_m[k], new_v[k] = _adamw_layers(f"adamw_{k}", w[k], per_layer, m[k], v[k])
    fix = lambda k, a: a[0] if k == "w_kv" else a
    return (loss, grad_x, *[fix(k, grads[k]) for k in order], *[fix(k, delta[k]) for k in order],
            *[fix(k, new_m[k]) for k in order], *[fix(k, new_v[k]) for k in order])
```

```python
import functools

import jax
import jax.numpy as jnp
from jax import lax
from jax.experimental import pallas as pl
from jax.experimental.pallas import tpu as pltpu

F32 = jnp.float32
BF16 = jnp.bfloat16
MESH = pl.DeviceIdType.MESH

EPS = 1e-5
PATTERNS = ((128, 1), (512, 4), (2048, 16))
HEAD_DIM = 64
ALIBI_MAX_BIAS = 8.0
NEG_INF = -1e30
ATT_BLK = 128
BWD_UNROLL = 8
N_CHIPS = 4
LANES = 128
VMEM_LIMIT = 56 * 1024 * 1024

ADAM_LR = 0.001
ADAM_B1 = 0.9
ADAM_B2 = 0.999
ADAM_EPS = 1e-08
ADAM_WD = 0.01
ADAM_STEP = 10


ANY = pl.BlockSpec(memory_space=pl.ANY)


def _params(n_grid_axes):
    return pltpu.CompilerParams(dimension_semantics=("arbitrary",) * n_grid_axes, vmem_limit_bytes=VMEM_LIMIT)


def _dot(a, b):
    return jnp.dot(a, b, preferred_element_type=F32)


def _dot_nt(a, b):
    return lax.dot_general(a, b, (((1,), (1,)), ((), ())), preferred_element_type=F32)


def _dot_tn(a, b):
    return lax.dot_general(a, b, (((0,), (0,)), ((), ())), preferred_element_type=F32)


def _relu2(a):
    return jnp.square(jnp.maximum(a, 0.0))


def _rms(hf, g):
    y = hf * lax.rsqrt(jnp.mean(hf * hf, axis=-1, keepdims=True) + EPS)
    return y * g


def _rms_bwd(hf, g, dn):
    rstd = lax.rsqrt(jnp.mean(hf * hf, axis=-1, keepdims=True) + EPS)
    xhat = hf * rstd
    dg = jnp.sum(dn * xhat, axis=0, keepdims=True)
    dx = dn * g
    dh = rstd * (dx - xhat * jnp.mean(dx * xhat, axis=-1, keepdims=True))
    return dh, dg


def _pieces(seg_widths, chunk_width, max_width):
    total = sum(seg_widths)
    cuts = {0, total}
    acc = 0
    for w in seg_widths:
        cuts.add(acc)
        acc += w
    cuts.update(range(0, total, chunk_width))
    cuts = sorted(cuts)
    fine = []
    for lo, hi in zip(cuts[:-1], cuts[1:]):
        while hi - lo > max_width:
            fine.append((lo, lo + max_width))
            lo += max_width
        fine.append((lo, hi))
    out = []
    for lo, hi in fine:
        acc = 0
        for s, w in enumerate(seg_widths):
            if lo < acc + w:
                break
            acc += w
        out.append((s, lo - acc, lo // chunk_width, lo % chunk_width, hi - lo))
    return out


def _relu2_bf16(a):
    return _relu2(a.astype(F32)).astype(BF16)


def _to_bf16(a):
    return a.astype(BF16)


def _norm_mm(name, h, g, wg, layer, planes, out_dtype, tm):
    T, D = h.shape
    cw = wg.shape[3]
    N = N_CHIPS * cw
    pw = N // planes
    pieces = _pieces([pw] * planes, cw, 512)

    def body(h_ref, g_ref, w_ref, n_ref, o_ref):
        n = _rms(h_ref[...], g_ref[...]).astype(BF16)
        n_ref[...] = n
        for s, a0, ch, b0, wd in pieces:
            o_ref[s, :, a0:a0 + wd] = _dot(n, w_ref[ch, :, b0:b0 + wd]).astype(out_dtype)

    return pl.pallas_call(
        body, name=name, grid=(T // tm,),
        in_specs=[pl.BlockSpec((tm, D), lambda i: (i, 0)),
                  pl.BlockSpec((1, D), lambda i: (0, 0)),
                  pl.BlockSpec((N_CHIPS, None, D, cw), lambda i: (0, layer, 0, 0))],
        out_specs=[pl.BlockSpec((tm, D), lambda i: (i, 0)),
                   pl.BlockSpec((planes, tm, pw), lambda i: (0, i, 0))],
        out_shape=[jax.ShapeDtypeStruct((T, D), BF16), jax.ShapeDtypeStruct((planes, T, pw), out_dtype)],
        compiler_params=_params(1))(h, g, wg)


def _mm_res_rows(name, a, wg, layer, h, act, tm):
    T = a.shape[0]
    rk, D = wg.shape[2], wg.shape[3]

    def body(a_ref, w_ref, h_ref, o_ref):
        acc = h_ref[...]
        for k in range(N_CHIPS):
            acc = acc + _dot(act(a_ref[:, k * rk:(k + 1) * rk]), w_ref[k])
        o_ref[...] = acc

    return pl.pallas_call(
        body, name=name, grid=(T // tm,),
        in_specs=[pl.BlockSpec((tm, N_CHIPS * rk), lambda i: (i, 0)),
                  pl.BlockSpec((N_CHIPS, None, rk, D), lambda i: (0, layer, 0, 0)),
                  pl.BlockSpec((tm, D), lambda i: (i, 0))],
        out_specs=pl.BlockSpec((tm, D), lambda i: (i, 0)),
        out_shape=jax.ShapeDtypeStruct((T, D), F32),
        compiler_params=_params(1))(a, wg, h)


def _mm_res_cols(name, a, wg, layer, h, tm):
    T, K = a.shape
    cw = wg.shape[3]
    D = N_CHIPS * cw

    def body(a_ref, w_ref, h_ref, o_ref):
        a16 = a_ref[...].astype(BF16)
        for j in range(N_CHIPS):
            o_ref[:, j * cw:(j + 1) * cw] = h_ref[:, j * cw:(j + 1) * cw] + _dot(a16, w_ref[j])

    return pl.pallas_call(
        body, name=name, grid=(T // tm,),
        in_specs=[pl.BlockSpec((tm, K), lambda i: (i, 0)),
                  pl.BlockSpec((N_CHIPS, None, K, cw), lambda i: (0, layer, 0, 0)),
                  pl.BlockSpec((tm, D), lambda i: (i, 0))],
        out_specs=pl.BlockSpec((tm, D), lambda i: (i, 0)),
        out_shape=jax.ShapeDtypeStruct((T, D), F32),
        compiler_params=_params(1))(a, wg, h)


def _resident(shape, index_map):
    return pl.BlockSpec(shape, index_map, pipeline_mode=pl.Buffered(1))


def _mlp_fwd(name, h, g, wup, wdown, tm):
    T, D = h.shape
    cw = wup.shape[3]

    def body(h_ref, g_ref, wu_ref, wd_ref, n_ref, a_ref, o_ref):
        hf = h_ref[...]
        n = _rms(hf, g_ref[...]).astype(BF16)
        n_ref[...] = n
        acc = hf
        for ch in range(N_CHIPS):
            a16 = _dot(n, wu_ref[ch]).astype(BF16)
            a_ref[:, ch * cw:(ch + 1) * cw] = a16
            acc = acc + _dot(_relu2_bf16(a16), wd_ref[ch])
        o_ref[...] = acc

    row = pl.BlockSpec((tm, D), lambda i: (i, 0))
    return pl.pallas_call(
        body, name=name, grid=(T // tm,),
        in_specs=[row, pl.BlockSpec((1, D), lambda i: (0, 0)),
                  _resident((N_CHIPS, None, D, cw), lambda i: (0, 0, 0, 0)),
                  _resident((N_CHIPS, None, cw, D), lambda i: (0, 0, 0, 0))],
        out_specs=[row, pl.BlockSpec((tm, N_CHIPS * cw), lambda i: (i, 0)), row],
        out_shape=[jax.ShapeDtypeStruct((T, D), BF16), jax.ShapeDtypeStruct((T, N_CHIPS * cw), BF16),
                   jax.ShapeDtypeStruct((T, D), F32)],
        compiler_params=_params(1))(h, g, wup, wdown)


def _mlp_bwd(name, dh, dh16, a, wdown, wup, h_mid, g, tm, deps=()):
    T, D = dh.shape
    cw = wup.shape[3]
    F = N_CHIPS * cw

    def body(dh_ref, dh16_ref, a_ref, wd_ref, wu_ref, h_ref, g_ref, *rest):
        da_ref, out_ref, out16_ref, dg_ref = rest[len(deps):]
        d16 = dh16_ref[...]
        acc = None
        for ch in range(N_CHIPS):
            cols = slice(ch * cw, (ch + 1) * cw)
            da = (_dot_nt(d16, wd_ref[ch]) * (2.0 * jnp.maximum(a_ref[:, cols].astype(F32), 0.0))).astype(BF16)
            da_ref[:, cols] = da
            d = _dot_nt(da, wu_ref[ch])
            acc = d if acc is None else acc + d
        dh_c, dg = _rms_bwd(h_ref[...], g_ref[...], acc)
        out = dh_ref[...] + dh_c
        out_ref[...] = out
        out16_ref[...] = out.astype(BF16)

        @pl.when(pl.program_id(0) == 0)
        def _():
            dg_ref[...] = dg

        @pl.when(pl.program_id(0) > 0)
        def _():
            dg_ref[...] += dg

    row = pl.BlockSpec((tm, D), lambda i: (i, 0))
    wide = pl.BlockSpec((tm, F), lambda i: (i, 0))
    vec = pl.BlockSpec((1, D), lambda i: (0, 0))
    return pl.pallas_call(
        body, name=name, grid=(T // tm,),
        in_specs=[row, row, wide, _resident((N_CHIPS, None, cw, D), lambda i: (0, 0, 0, 0)),
                  _resident((N_CHIPS, None, D, cw), lambda i: (0, 0, 0, 0)), row, vec] + [ANY] * len(deps),
        out_specs=[wide, row, row, vec],
        out_shape=[jax.ShapeDtypeStruct((T, F), BF16), jax.ShapeDtypeStruct((T, D), F32),
                   jax.ShapeDtypeStruct((T, D), BF16), jax.ShapeDtypeStruct((1, D), F32)],
        compiler_params=_params(1))(dh, dh16, a, wdown, wup, h_mid, g, *deps)


CONV_ROWS = 256
CONV_HALO = 16


def _conv_shifted(ext, k, r0, rows):
    rolled = pltpu.roll(ext, k, 0)[CONV_HALO:]
    t = r0 + lax.broadcasted_iota(jnp.int32, rolled.shape, 0)
    return jnp.where(t >= k, rolled, 0.0)


def _conv_ahead(ext, k, r0, rows, S):
    rolled = pltpu.roll(ext, rows + CONV_HALO - k, 0)[:rows]
    t = r0 + lax.broadcasted_iota(jnp.int32, rolled.shape, 0)
    return jnp.where(t + k < S, rolled, 0.0)


def _conv_fwd(name, bcu, cwg, layer, tc):
    _, B, S, D = bcu.shape
    cwc = cwg.shape[3]
    per_chunk = cwc // tc
    R = min(CONV_ROWS, S)

    def body(x_ref, w_ref, z_ref):
        w = [w_ref[k:k + 1, :] for k in range(3)]

        def step(i, carry):
            r0 = pl.multiple_of(i * R, R)
            h0 = pl.multiple_of(jnp.maximum(r0 - CONV_HALO, 0), CONV_HALO)
            ld = lambda p, start, rows: x_ref[p, pl.ds(start, rows), :].astype(F32)
            cu = jnp.concatenate([ld(1, h0, CONV_HALO) * ld(2, h0, CONV_HALO), ld(1, r0, R) * ld(2, r0, R)], axis=0)
            conv = w[0] * cu[CONV_HALO:]
            conv = conv + w[1] * _conv_shifted(cu, 1, r0, R)
            conv = conv + w[2] * _conv_shifted(cu, 2, r0, R)
            z_ref[pl.ds(r0, R), :] = (ld(0, r0, R) * conv).astype(BF16)
            return carry

        lax.fori_loop(0, S // R, step, 0)

    return pl.pallas_call(
        body, name=name, grid=(B, D // tc),
        in_specs=[pl.BlockSpec((3, None, S, tc), lambda b, j: (0, b, 0, j)),
                  pl.BlockSpec((None, None, 3, tc), lambda b, j: (j // per_chunk, layer, 0, j % per_chunk))],
        out_specs=pl.BlockSpec((None, S, tc), lambda b, j: (b, 0, j)),
        out_shape=jax.ShapeDtypeStruct((B, S, D), BF16),
        compiler_params=_params(2))(bcu, cwg)


def _conv_bwd(name, bcu, dz, cwg, layer, tc):
    _, B, S, D = bcu.shape
    cwc = cwg.shape[3]
    per_chunk = cwc // tc
    R = min(CONV_ROWS, S)

    def body(x_ref, dz_ref, w_ref, d_ref, dw_ref):
        w = [w_ref[k:k + 1, :] for k in range(3)]

        @pl.when(pl.program_id(1) == 0)
        def _():
            dw_ref[...] = jnp.zeros_like(dw_ref)

        def step(i, carry):
            r0 = pl.multiple_of(i * R, R)
            h0 = pl.multiple_of(jnp.maximum(r0 - CONV_HALO, 0), CONV_HALO)
            a0 = pl.multiple_of(jnp.minimum(r0 + R, S - CONV_HALO), CONV_HALO)
            ld = lambda p, start, rows: x_ref[p, pl.ds(start, rows), :].astype(F32)
            b, c, u = ld(0, r0, R), ld(1, r0, R), ld(2, r0, R)
            dz = dz_ref[pl.ds(r0, R), :]
            cu = jnp.concatenate([ld(1, h0, CONV_HALO) * ld(2, h0, CONV_HALO), c * u], axis=0)
            cu1 = _conv_shifted(cu, 1, r0, R)
            cu2 = _conv_shifted(cu, 2, r0, R)
            conv = w[0] * (c * u) + w[1] * cu1 + w[2] * cu2
            dconv = dz * b
            dca = jnp.concatenate([dconv, dz_ref[pl.ds(a0, CONV_HALO), :] * ld(0, a0, CONV_HALO)], axis=0)
            dcu = w[0] * dconv + w[1] * _conv_ahead(dca, 1, r0, R, S) + w[2] * _conv_ahead(dca, 2, r0, R, S)
            d_ref[0, pl.ds(r0, R), :] = (dz * conv).astype(BF16)
            d_ref[1, pl.ds(r0, R), :] = (dcu * u).astype(BF16)
            d_ref[2, pl.ds(r0, R), :] = (dcu * c).astype(BF16)
            return (carry[0] + jnp.sum(dconv * (c * u), axis=0, keepdims=True),
                    carry[1] + jnp.sum(dconv * cu1, axis=0, keepdims=True),
                    carry[2] + jnp.sum(dconv * cu2, axis=0, keepdims=True))

        zero = jnp.zeros((1, tc), F32)
        s0, s1, s2 = lax.fori_loop(0, S // R, step, (zero, zero, zero))
        for k, sk in enumerate((s0, s1, s2)):
            dw_ref[k:k + 1, :] += sk

    return pl.pallas_call(
        body, name=name, grid=(D // tc, B),
        in_specs=[pl.BlockSpec((3, None, S, tc), lambda j, b: (0, b, 0, j)),
                  pl.BlockSpec((None, S, tc), lambda j, b: (b, 0, j)),
                  pl.BlockSpec((None, None, 3, tc), lambda j, b: (j // per_chunk, layer, 0, j % per_chunk))],
        out_specs=[pl.BlockSpec((3, None, S, tc), lambda j, b: (0, b, 0, j)),
                   pl.BlockSpec((3, tc), lambda j, b: (0, j))],
        out_shape=[jax.ShapeDtypeStruct((3, B, S, D), BF16), jax.ShapeDtypeStruct((3, D), F32)],
        compiler_params=_params(2))(bcu, dz, cwg)


def _att_rows(dil, idx, nb):
    r, n = idx // nb, idx % nb
    if dil == 1:
        cur = pl.ds(pl.multiple_of(n * ATT_BLK, ATT_BLK), ATT_BLK)
        prev = pl.ds(pl.multiple_of(jnp.maximum(n - 1, 0) * ATT_BLK, ATT_BLK), ATT_BLK)
    else:
        cur = pl.ds(n * (ATT_BLK * dil) + r, ATT_BLK, stride=dil)
        prev = pl.ds(jnp.maximum(n - 1, 0) * (ATT_BLK * dil) + r, ATT_BLK, stride=dil)
    return n, cur, prev


def _att_bias(bias_ref, dil, sl_ref, hp):
    row = lax.broadcasted_iota(jnp.int32, (2 * ATT_BLK, 2 * ATT_BLK), 0)
    ci = lax.broadcasted_iota(jnp.int32, (2 * ATT_BLK, 2 * ATT_BLK), 1)
    j = ATT_BLK + (row & (ATT_BLK - 1)) - ci
    slope = jnp.where(row < ATT_BLK, sl_ref[2 * hp], sl_ref[2 * hp + 1])
    rest = jnp.where((j >= 0) & (j <= ATT_BLK), -slope * (dil * j).astype(F32), NEG_INF)
    bias_ref[1] = rest
    bias_ref[0] = jnp.where(ci >= ATT_BLK, rest, NEG_INF)


def _stack_heads(x16, lane):
    first = lane < HEAD_DIM
    return jnp.concatenate([jnp.where(first, x16, jnp.zeros_like(x16)),
                            jnp.where(first, jnp.zeros_like(x16), x16)], axis=0)


def _per_head(col, lane):
    return jnp.where(lane < HEAD_DIM, col[:ATT_BLK], col[ATT_BLK:])


def _attn_fwd(name, q, kv, slopes, n_heads):
    B, S, CQ = q.shape
    HP = n_heads * HEAD_DIM // LANES
    scale = HEAD_DIM ** -0.5
    n_groups = len(PATTERNS)
    CH = 256

    def body(sl_ref, q_ref, k_ref, v_ref, o_ref, lse_ref, bias_ref, *parts):
        og, lg = parts[:n_groups], parts[n_groups:]
        hp, g = pl.program_id(1), pl.program_id(2)
        lane = lax.broadcasted_iota(jnp.int32, (1, LANES), 1)

        for gi, (window, dil) in enumerate(PATTERNS):
            nb = S // dil // ATT_BLK

            @pl.when(g == gi)
            def _(gi=gi, dil=dil, nb=nb):
                _att_bias(bias_ref, dil, sl_ref, hp)

                def step(idx, carry):
                    n, cur, prev = _att_rows(dil, idx, nb)
                    qs = _stack_heads((q_ref[cur, :] * scale).astype(BF16), lane)
                    kc = jnp.concatenate([k_ref[prev, :], k_ref[cur, :]], axis=0).astype(BF16)
                    vc = jnp.concatenate([v_ref[prev, :], v_ref[cur, :]], axis=0).astype(BF16)
                    s = _dot_nt(qs, kc) + bias_ref[jnp.minimum(n, 1)]
                    m = jnp.max(s, axis=-1, keepdims=True)
                    p = jnp.exp(s - m)
                    l = jnp.sum(p, axis=-1, keepdims=True)
                    p16 = p.astype(BF16)
                    o_un = _dot(jnp.concatenate([p16[:ATT_BLK], p16[ATT_BLK:]], axis=1), _stack_heads_rows(vc, lane))
                    og[gi][cur, :] = o_un / _per_head(l, lane)
                    lg[gi][cur, :] = _per_head(m + jnp.log(l), lane)
                    return carry

                lax.fori_loop(0, S // ATT_BLK, step, 0, unroll=8)

        @pl.when(g == n_groups - 1)
        def _():
            def comb(i, carry):
                rows = pl.ds(pl.multiple_of(i * CH, CH), CH)
                a, b, c = lg[0][rows, :], lg[1][rows, :], lg[2][rows, :]
                m = jnp.maximum(jnp.maximum(a, b), c)
                ea, eb, ec = jnp.exp(a - m), jnp.exp(b - m), jnp.exp(c - m)
                z = ea + eb + ec
                o_ref[rows, :] = (ea / z) * og[0][rows, :] + (eb / z) * og[1][rows, :] + (ec / z) * og[2][rows, :]
                lse_ref[rows, :] = m + jnp.log(z)
                return carry

            lax.fori_loop(0, S // CH, comb, 0)

    blk = (None, S, LANES)
    out = pl.BlockSpec(blk, lambda b, hp, g: (b, 0, hp))
    return pl.pallas_call(
        body, name=name, grid=(B, HP, n_groups),
        in_specs=[pl.BlockSpec(memory_space=pltpu.SMEM),
                  pl.BlockSpec(blk, lambda b, hp, g: (b, 0, g * HP + hp)),
                  pl.BlockSpec(blk, lambda b, hp, g: (b, 0, g * 2 * HP + hp)),
                  pl.BlockSpec(blk, lambda b, hp, g: (b, 0, g * 2 * HP + HP + hp))],
        out_specs=[out, out],
        out_shape=[jax.ShapeDtypeStruct((B, S, HP * LANES), F32)] * 2,
        scratch_shapes=[pltpu.VMEM((2, 2 * ATT_BLK, 2 * ATT_BLK), F32)] + [pltpu.VMEM((S, LANES), F32)] * (2 * n_groups),
        compiler_params=_params(3))(slopes, q, kv, kv)


def _stack_heads_rows(x16, lane):
    first = lane < HEAD_DIM
    return jnp.concatenate([jnp.where(first, x16, jnp.zeros_like(x16)),
                            jnp.where(first, jnp.zeros_like(x16), x16)], axis=0)


def _attn_bwd(name, q, kv, slopes, o, lse, do, n_heads, dkv_prev):
    B, S, CQ = q.shape
    HP = n_heads * HEAD_DIM // LANES
    scale = HEAD_DIM ** -0.5
    n_groups = len(PATTERNS)
    n_prev = 0 if dkv_prev is None else 2

    def body(sl_ref, q_ref, k_ref, v_ref, o_ref, lse_ref, do_ref, *rest):
        dq_ref, dk_ref, dv_ref, bias_ref = rest[n_prev:]
        hp, g = pl.program_id(1), pl.program_id(2)
        lane = lax.broadcasted_iota(jnp.int32, (1, LANES), 1)
        first = lane < HEAD_DIM

        def flush(rows, dk, dv):
            if n_prev:
                dk = dk + rest[0][rows, :]
                dv = dv + rest[1][rows, :]
            dk_ref[rows, :] = dk
            dv_ref[rows, :] = dv

        for gi, (window, dil) in enumerate(PATTERNS):
            nb = S // dil // ATT_BLK
            n_blocks = S // ATT_BLK

            @pl.when(g == gi)
            def _(dil=dil, nb=nb, n_blocks=n_blocks):
                _att_bias(bias_ref, dil, sl_ref, hp)

                def block(idx, carry, first_of_all):
                    n, cur, prev = _att_rows(dil, idx, nb)
                    qs = _stack_heads((q_ref[cur, :] * scale).astype(BF16), lane)
                    kc = jnp.concatenate([k_ref[prev, :], k_ref[cur, :]], axis=0).astype(BF16)
                    vc = jnp.concatenate([v_ref[prev, :], v_ref[cur, :]], axis=0).astype(BF16)
                    dob = do_ref[cur, :]
                    prod = dob * o_ref[cur, :]
                    lseb = lse_ref[cur, :]
                    dos = _stack_heads(dob.astype(BF16), lane)
                    delta = jnp.concatenate(
                        [jnp.sum(jnp.where(first, prod, 0.0), axis=-1, keepdims=True),
                         jnp.sum(jnp.where(first, 0.0, prod), axis=-1, keepdims=True)], axis=0)
                    lse_col = jnp.concatenate(
                        [jnp.max(jnp.where(first, lseb, -jnp.inf), axis=-1, keepdims=True),
                         jnp.max(jnp.where(first, -jnp.inf, lseb), axis=-1, keepdims=True)], axis=0)
                    s = _dot_nt(qs, kc) + bias_ref[jnp.minimum(n, 1)]
                    p = jnp.exp(s - lse_col)
                    ds = p * (_dot_nt(dos, vc) - delta)
                    ds16 = ds.astype(BF16)
                    dq = _dot(jnp.concatenate([ds16[:ATT_BLK], ds16[ATT_BLK:]], axis=1), _stack_heads_rows(kc, lane))
                    dq_ref[cur, :] = dq * scale
                    dk = _dot_tn(ds16, qs)
                    dv = _dot_tn(p.astype(BF16), dos)

                    def flush_before():
                        _, before, _ = _att_rows(dil, idx - 1, nb)
                        flush(before, carry[0] + dk[:ATT_BLK], carry[1] + dv[:ATT_BLK])

                    if first_of_all:
                        pl.when(idx > 0)(flush_before)
                    else:
                        flush_before()
                    return dk[ATT_BLK:], dv[ATT_BLK:]

                def step(i, carry):
                    for u in range(BWD_UNROLL):
                        carry = block(i * BWD_UNROLL + u, carry, u == 0)
                    return carry

                zero = jnp.zeros((ATT_BLK, LANES), F32)
                dk_last, dv_last = lax.fori_loop(0, n_blocks // BWD_UNROLL, step, (zero, zero))
                _, last, _ = _att_rows(dil, n_blocks - 1, nb)
                flush(last, dk_last, dv_last)

    blk = (None, S, LANES)
    shared = pl.BlockSpec(blk, lambda b, hp, g: (b, 0, hp))
    grouped = pl.BlockSpec(blk, lambda b, hp, g: (b, 0, g * HP + hp))
    prev = [] if dkv_prev is None else list(dkv_prev)
    gshape = jax.ShapeDtypeStruct((B, S, n_groups * HP * LANES), F32)
    return pl.pallas_call(
        body, name=name, grid=(B, HP, n_groups),
        in_specs=[pl.BlockSpec(memory_space=pltpu.SMEM), grouped,
                  pl.BlockSpec(blk, lambda b, hp, g: (b, 0, g * 2 * HP + hp)),
                  pl.BlockSpec(blk, lambda b, hp, g: (b, 0, g * 2 * HP + HP + hp)),
                  shared, shared, shared] + [grouped] * n_prev,
        out_specs=[grouped] * 3, out_shape=[gshape] * 3,
        scratch_shapes=[pltpu.VMEM((2, 2 * ATT_BLK, 2 * ATT_BLK), F32)],
        compiler_params=_params(3))(slopes, q, kv, kv, o, lse, do, *prev)


def _final_loss(name, h, g, target, tm):
    T, D = h.shape

    def body(h_ref, g_ref, t_ref, loss_ref, dh_ref, dh16_ref, dg_ref):
        hf = h_ref[...]
        gv = g_ref[...]
        rstd = lax.rsqrt(jnp.mean(hf * hf, axis=-1, keepdims=True) + EPS)
        xhat = hf * rstd
        err = xhat * gv - t_ref[...]
        part = 0.5 * jnp.sum(jnp.mean(err * err, axis=-1, keepdims=True), axis=0, keepdims=True)
        dy = err * (1.0 / D)
        dg = jnp.sum(dy * xhat, axis=0, keepdims=True)
        dx = dy * gv
        dh = rstd * (dx - xhat * jnp.mean(dx * xhat, axis=-1, keepdims=True))
        dh_ref[...] = dh
        dh16_ref[...] = dh.astype(BF16)

        @pl.when(pl.program_id(0) == 0)
        def _():
            loss_ref[...] = part
            dg_ref[...] = dg

        @pl.when(pl.program_id(0) > 0)
        def _():
            loss_ref[...] += part
            dg_ref[...] += dg

    return pl.pallas_call(
        body, name=name, grid=(T // tm,),
        in_specs=[pl.BlockSpec((tm, D), lambda i: (i, 0)), pl.BlockSpec((1, D), lambda i: (0, 0)),
                  pl.BlockSpec((tm, D), lambda i: (i, 0))],
        out_specs=[pl.BlockSpec((1, 1), lambda i: (0, 0)), pl.BlockSpec((tm, D), lambda i: (i, 0)),
                   pl.BlockSpec((tm, D), lambda i: (i, 0)), pl.BlockSpec((1, D), lambda i: (0, 0))],
        out_shape=[jax.ShapeDtypeStruct((1, 1), F32), jax.ShapeDtypeStruct((T, D), F32),
                   jax.ShapeDtypeStruct((T, D), BF16), jax.ShapeDtypeStruct((1, D), F32)],
        compiler_params=_params(1))(h, g, target)


def _nt_rows(name, dh, wg, layer, a_mul, out_dtype, tm, deps=()):
    T, D = dh.shape
    rk = wg.shape[2]
    N = N_CHIPS * rk
    with_a = a_mul is not None

    def body(dh_ref, w_ref, *rest):
        o_ref = rest[-1]
        d16 = dh_ref[...]
        for ch in range(N_CHIPS):
            r = _dot_nt(d16, w_ref[ch])
            if with_a:
                r = r * (2.0 * jnp.maximum(rest[0][:, ch * rk:(ch + 1) * rk].astype(F32), 0.0))
            o_ref[:, ch * rk:(ch + 1) * rk] = r.astype(out_dtype)

    in_specs = [pl.BlockSpec((tm, D), lambda i: (i, 0)),
                pl.BlockSpec((N_CHIPS, None, rk, D), lambda i: (0, layer, 0, 0))]
    args = [dh, wg]
    if with_a:
        in_specs.append(pl.BlockSpec((tm, N), lambda i: (i, 0)))
        args.append(a_mul)
    in_specs += [ANY] * len(deps)
    args += list(deps)
    return pl.pallas_call(
        body, name=name, grid=(T // tm,), in_specs=in_specs,
        out_specs=pl.BlockSpec((tm, N), lambda i: (i, 0)),
        out_shape=jax.ShapeDtypeStruct((T, N), out_dtype),
        compiler_params=_params(1))(*args)


def _nt_cols(name, ysegs, wg, layer, tm, norm):
    Nw, cw = wg.shape[2], wg.shape[3]
    widths = [bs[-1] for _, bs, _ in ysegs]
    pieces = _pieces(widths, cw, 1024)
    ns = len(ysegs)
    T = norm[0].shape[0] if norm is not None else ysegs[0][0].shape[-2]

    def body(*refs):
        y_refs = refs[:ns]
        w_ref = refs[ns]
        acc = refs[-1]
        for n, (s, a0, ch, b0, wd) in enumerate(pieces):
            d = _dot_nt(y_refs[s][:, a0:a0 + wd].astype(BF16), w_ref[ch, :, b0:b0 + wd])
            if n == 0:
                acc[...] = d
            else:
                acc[...] += d
        if norm is None:
            refs[ns + 1][...] = acc[...]
        else:
            h_ref, g_ref, dhin_ref, out_ref, out16_ref, dg_ref = refs[ns + 1:ns + 7]
            dh_c, dg = _rms_bwd(h_ref[...], g_ref[...], acc[...])
            dh = dhin_ref[...] + dh_c
            out_ref[...] = dh
            out16_ref[...] = dh.astype(BF16)

            @pl.when(pl.program_id(0) == 0)
            def _():
                dg_ref[...] = dg

            @pl.when(pl.program_id(0) > 0)
            def _():
                dg_ref[...] += dg

    in_specs = [pl.BlockSpec(bs, im) for _, bs, im in ysegs]
    in_specs.append(pl.BlockSpec((N_CHIPS, None, Nw, cw), lambda i: (0, layer, 0, 0)))
    args = [a for a, _, _ in ysegs] + [wg]
    row = pl.BlockSpec((tm, Nw), lambda i: (i, 0))
    vec = pl.BlockSpec((1, Nw), lambda i: (0, 0))
    if norm is None:
        out_specs = row
        out_shape = jax.ShapeDtypeStruct((T, Nw), F32)
    else:
        in_specs += [row, vec, row]
        args += list(norm)
        out_specs = [row, row, vec]
        out_shape = [jax.ShapeDtypeStruct((T, Nw), F32), jax.ShapeDtypeStruct((T, Nw), BF16),
                     jax.ShapeDtypeStruct((1, Nw), F32)]
    return pl.pallas_call(
        body, name=name, grid=(T // tm,), in_specs=in_specs, out_specs=out_specs, out_shape=out_shape,
        scratch_shapes=[pltpu.VMEM((tm, Nw), F32)], compiler_params=_params(1))(*args)


def _tn(name, x, x_act, ysegs, cw, cols_layout, tmm, tt, deps=(), out_dtype=F32):
    T, M = x.shape
    widths = [bs[-1] for _, bs, _ in ysegs]
    N = sum(widths)
    pieces = _pieces(widths, cw if cols_layout else N, 1024)
    ns = len(ysegs)
    n_t = T // tt
    block = (N_CHIPS, tmm, cw) if cols_layout else (tmm, N)
    narrow = out_dtype != F32

    def body(x_ref, *refs):
        y_refs = refs[:ns]
        o_ref = refs[ns + len(deps)]
        acc = refs[-1] if narrow else o_ref

        @pl.when(pl.program_id(1) == 0)
        def _():
            acc[...] = jnp.zeros_like(acc)

        xt = x_act(x_ref[...])
        for s, a0, ch, b0, wd in pieces:
            d = _dot_tn(xt, y_refs[s][:, a0:a0 + wd].astype(BF16))
            if cols_layout:
                acc[ch, :, b0:b0 + wd] += d
            else:
                acc[:, b0:b0 + wd] += d
        if narrow:
            @pl.when(pl.program_id(1) == n_t - 1)
            def _():
                o_ref[...] = acc[...].astype(out_dtype)

    in_specs = [pl.BlockSpec((tt, tmm), lambda m, t: (t, m))] + [pl.BlockSpec(bs, im) for _, bs, im in ysegs]
    in_specs += [ANY] * len(deps)
    if cols_layout:
        out_specs = pl.BlockSpec(block, lambda m, t: (0, m, 0))
        out_shape = jax.ShapeDtypeStruct((N_CHIPS, M, cw), out_dtype)
    else:
        out_specs = pl.BlockSpec(block, lambda m, t: (m, 0))
        out_shape = jax.ShapeDtypeStruct((M, N), out_dtype)
    return pl.pallas_call(
        body, name=name, grid=(M // tmm, n_t), in_specs=in_specs, out_specs=out_specs, out_shape=out_shape,
        scratch_shapes=[pltpu.VMEM(block, F32)] if narrow else [],
        compiler_params=_params(2))(x, *[a for a, _, _ in ysegs], *deps)


def _seg2d(a, t_rows, grid_rank):
    w = a.shape[1]
    if grid_rank == 1:
        return (a, (t_rows, w), lambda i: (i, 0))
    return (a, (t_rows, w), lambda m, t: (t, 0))


def _kv_segments(dk, dv, C, t_rows, grid_rank):
    segs = []
    for g in range(len(PATTERNS)):
        for a in (dk, dv):
            if grid_rank == 1:
                segs.append((a, (t_rows, C), lambda i, g=g: (i, g)))
            else:
                segs.append((a, (t_rows, C), lambda m, t, g=g: (t, g)))
    return segs


def _seg_plane(a, plane, t_rows, grid_rank):
    w = a.shape[2]
    if grid_rank == 1:
        return (a, (None, t_rows, w), lambda i: (plane, i, 0))
    return (a, (None, t_rows, w), lambda m, t: (plane, t, 0))


def _row_tile(rows, row_bytes, budget_bytes=2 * 1024 * 1024):
    t = rows
    while t * row_bytes > budget_bytes and t % 32 == 0:
        t //= 2
    return t


N_DEVICES = 8


def _device_add(name, own, slots, place):
    _, _, hr, c = own.shape
    tr = _row_tile(hr, c * 4, 1024 * 1024)

    def body(place_ref, own_ref, *refs):
        o_ref = refs[-1]
        acc = own_ref[...].astype(F32)
        for r in refs[:-1]:
            acc = acc + r[...].astype(F32)
        o_ref[...] = acc

    def slot(k):
        return pl.BlockSpec((None, tr, c), lambda i, pr: ((2 * pr[0] + pr[1] + k) % N_DEVICES, i, 0))

    grid_spec = pltpu.PrefetchScalarGridSpec(
        num_scalar_prefetch=1, grid=(hr // tr,),
        in_specs=[pl.BlockSpec((None, None, tr, c), lambda i, pr: (pr[0], pr[1], i, 0))]
        + [slot(k) for k in range(1, N_DEVICES)],
        out_specs=pl.BlockSpec((None, tr, c), lambda i, pr: (pr[1], i, 0)))
    return pl.pallas_call(body, name=name, grid_spec=grid_spec,
                          out_shape=jax.ShapeDtypeStruct((2, hr, c), F32),
                          compiler_params=_params(1))(place, own, *[slots] * (N_DEVICES - 1))


def _adamw(name, w, g, m, v):
    rows, cols = w.shape
    tr = _row_tile(rows, cols * 4, 1024 * 1024)

    def body(w_ref, g_ref, m_ref, v_ref, d_ref, nm_ref, nv_ref):
        d_ref[...], nm_ref[...], nv_ref[...] = _adamw_math(w_ref[...], g_ref[...], m_ref[...], v_ref[...])

    spec = pl.BlockSpec((tr, cols), lambda i: (i, 0))
    return pl.pallas_call(
        body, name=name, grid=(rows // tr,), in_specs=[spec] * 4, out_specs=[spec] * 3,
        out_shape=[jax.ShapeDtypeStruct((rows, cols), F32)] * 3, compiler_params=_params(1))(w, g, m, v)


def _adamw_math(w, g, m, v):
    nm = ADAM_B1 * m + (1.0 - ADAM_B1) * g
    nv = ADAM_B2 * v + (1.0 - ADAM_B2) * jnp.square(g)
    m_hat = nm / (1.0 - ADAM_B1 ** ADAM_STEP)
    v_hat = nv / (1.0 - ADAM_B2 ** ADAM_STEP)
    return -ADAM_LR * (m_hat / (jnp.sqrt(v_hat) + ADAM_EPS) + ADAM_WD * w), nm, nv


def _adamw_layers(name, w, grads, m, v):
    L, r, c = w.shape
    tr = _row_tile(r, L * c * 4, 1024 * 1024)

    def body(*refs):
        w_ref, m_ref, v_ref = refs[:3]
        g_refs = refs[3:3 + L]
        go_ref, d_ref, nm_ref, nv_ref = refs[3 + L:]
        for l in range(L):
            g = g_refs[l][...]
            go_ref[l] = g
            d_ref[l], nm_ref[l], nv_ref[l] = _adamw_math(w_ref[l], g, m_ref[l], v_ref[l])

    stacked = pl.BlockSpec((L, tr, c), lambda i: (0, i, 0))
    return pl.pallas_call(
        body, name=name, grid=(r // tr,),
        in_specs=[stacked] * 3 + [pl.BlockSpec((tr, c), lambda i: (i, 0))] * L, out_specs=[stacked] * 4,
        out_shape=[jax.ShapeDtypeStruct((L, r, c), F32)] * 4, compiler_params=_params(1))(w, m, v, *grads)


def _place():
    x, y, c = lax.axis_index("x"), lax.axis_index("y"), lax.axis_index("c")
    chips = [(1 - x, y), (x, 1 - y), (1 - x, 1 - y)]
    return x, y, c, chips


HBM = pl.BlockSpec(memory_space=pltpu.HBM)
SEM = pl.BlockSpec(memory_space=pltpu.SEMAPHORE)
EFFECT = pltpu.SideEffectType.DATAFLOW_SIDE_EFFECTING


class _Copy:
    def __init__(self, src, src_view, land, dst_view, recv_view, target):
        self.src, self.src_view, self.land, self.dst_view, self.recv_view, self.target = (
            src, src_view, land, dst_view, recv_view, target)


def _whole(ref, place):
    return ref


def _split_start(name, srcs, land_shapes, plans):
    skeys, lkeys = list(srcs), list(land_shapes)
    ns, nl, ng = len(skeys), len(lkeys), len(plans)

    def body(*refs):
        src = dict(zip(skeys, refs[:ns]))
        land = dict(zip(lkeys, refs[ns:ns + nl]))
        sems = refs[ns + nl:ns + nl + 2 * ng]
        token = refs[-1]
        place = _place()
        for gi, plan in enumerate(plans):
            for k, cp in enumerate(plan):
                pltpu.make_async_remote_copy(
                    src_ref=cp.src_view(src[cp.src], place), dst_ref=cp.dst_view(land[cp.land], place),
                    send_sem=sems[2 * gi].at[k], recv_sem=sems[2 * gi + 1].at[k],
                    device_id=cp.target(place), device_id_type=MESH).start()
        token[...] = jnp.zeros_like(token)

    sem_shapes = []
    for plan in plans:
        sem_shapes += [pltpu.SemaphoreType.DMA((len(plan),))] * 2
    buffers = [srcs[k] for k in skeys] + [lax.empty(land_shapes[k].shape, land_shapes[k].dtype) for k in lkeys]
    outs = pl.pallas_call(
        body, name=name,
        out_shape=(*sem_shapes, *[pltpu.HBM(a.shape, a.dtype) for a in buffers], jax.ShapeDtypeStruct((8, LANES), F32)),
        in_specs=[HBM] * (ns + nl),
        out_specs=(*[SEM] * (2 * ng), *[HBM] * (ns + nl), pl.BlockSpec(memory_space=pltpu.VMEM)),
        input_output_aliases={i: 2 * ng + i for i in range(ns + nl)},
        compiler_params=pltpu.CompilerParams(has_side_effects=EFFECT),
    )(*[pltpu.with_memory_space_constraint(a, pltpu.HBM) for a in buffers])
    sems = [(outs[2 * gi], outs[2 * gi + 1]) for gi in range(ng)]
    thru = outs[2 * ng:2 * ng + ns + nl]
    return sems, dict(zip(skeys, thru[:ns])), dict(zip(lkeys, thru[ns:])), outs[-1]


def _split_wait(name, sems, srcs, lands, plan, after):
    skeys, lkeys = list(srcs), list(lands)
    ns, nl = len(skeys), len(lkeys)

    def body(*refs):
        src = dict(zip(skeys, refs[:ns]))
        land = dict(zip(lkeys, refs[ns:ns + nl]))
        ssem, rsem = refs[ns + nl], refs[ns + nl + 1]
        place = _place()
        for k, cp in enumerate(plan):
            pltpu.make_async_remote_copy(
                src_ref=cp.src_view(src[cp.src], place), dst_ref=cp.dst_view(land[cp.land], place),
                send_sem=ssem.at[k], recv_sem=rsem.at[k],
                device_id=cp.target(place), device_id_type=MESH).wait_send()
            got = cp.recv_view(land[cp.land], place)
            pltpu.make_async_remote_copy(
                src_ref=got, dst_ref=got, send_sem=ssem.at[k], recv_sem=rsem.at[k],
                device_id=cp.target(place), device_id_type=MESH).wait_recv()

    buffers = [srcs[k] for k in skeys] + [lands[k] for k in lkeys]
    outs = pl.pallas_call(
        body, name=name, out_shape=tuple(pltpu.HBM(a.shape, a.dtype) for a in buffers),
        in_specs=(*[HBM] * (ns + nl), SEM, SEM, ANY), out_specs=tuple([HBM] * (ns + nl)),
        input_output_aliases={i: i for i in range(ns + nl)},
        compiler_params=pltpu.CompilerParams(has_side_effects=EFFECT),
    )(*buffers, sems[0], sems[1], after)
    return dict(zip(skeys, outs[:ns])), dict(zip(lkeys, outs[ns:]))


def _chip_of(place):
    x, y, c, chips = place
    return 2 * x + y


class _WeightGather:
    def __init__(self, blocks):
        self.plans, shapes = {}, {}
        for key, a in blocks.items():
            shapes[key] = jax.ShapeDtypeStruct((N_CHIPS,) + a.shape, a.dtype)
            slot = lambda ref, place: ref.at[_chip_of(place)]
            plan = [_Copy(key, _whole, key, slot,
                          lambda ref, place, k=k: ref.at[2 * place[3][k][0] + place[3][k][1]],
                          lambda place, k=k: (place[3][k][0], place[3][k][1], place[2])) for k in range(3)]
            plan.append(_Copy(key, _whole, key, slot, slot, lambda place: (place[0], place[1], 1 - place[2])))
            self.plans[key] = plan
        sems, self.srcs, self.lands, self.token = _split_start("gather_start", blocks, shapes, list(self.plans.values()))
        self.sems = dict(zip(self.plans, sems))

    def get(self, l, name, after):
        key = (l, name)
        _, lands = _split_wait(f"gather_wait_{name}{l}", self.sems[key], {key: self.srcs[key]},
                               {key: self.lands[key]}, self.plans[key], after)
        return lands[key][:, None]


class _GradReduce:
    def __init__(self, place):
        self.place = place
        self.jobs = []
        self.done = {}
        self.n = 0

    def submit(self, grads):
        views = {k: a.reshape(N_CHIPS, 2, a.shape[1] // 2, a.shape[2]) for k, a in grads.items()}
        shapes = {k: jax.ShapeDtypeStruct((N_DEVICES,) + a.shape[2:], a.dtype) for k, a in views.items()}

        def peer(place, k):
            x, y, c, _ = place
            return (1 - x if k & 4 else x, 1 - y if k & 2 else y, 1 - c if k & 1 else c)

        def index(dev):
            return 4 * dev[0] + 2 * dev[1] + dev[2]

        plan = []
        for key in views:
            for k in range(1, N_DEVICES):
                plan.append(_Copy(
                    key, lambda ref, place, k=k: ref.at[2 * peer(place, k)[0] + peer(place, k)[1], peer(place, k)[2]],
                    key, lambda ref, place: ref.at[index(place[:3])],
                    lambda ref, place, k=k: ref.at[index(peer(place, k))],
                    lambda place, k=k: peer(place, k)))
        sems, srcs, lands, token = _split_start(f"grad_start{self.n}", views, shapes, [plan])
        self.jobs.append(dict(id=self.n, sems=sems[0], srcs=srcs, lands=lands, plan=plan))
        self.n += 1
        return token

    def pump(self, after):
        return []

    def finish(self, after):
        for job in self.jobs:
            srcs, lands = _split_wait(f"grad_wait{job['id']}", job["sems"], job["srcs"], job["lands"], job["plan"],
                                      after)
            for i, k in enumerate(srcs):
                self.done[k] = _device_add(f"grad_add{job['id']}_{i}", srcs[k], lands[k], self.place)
        self.jobs = []
        return self.done


def _pair_share(halves):
    n = len(halves)

    def body(*refs):
        outs = refs[n:2 * n]
        ssem, rsem = refs[2 * n:]
        x, y, c, _ = _place()
        sends = []
        for t in range(n):
            cp = pltpu.make_async_remote_copy(
                src_ref=outs[t].at[c], dst_ref=outs[t].at[c], send_sem=ssem.at[t], recv_sem=rsem.at[t],
                device_id=(x, y, 1 - c), device_id_type=MESH)
            cp.start()
            sends.append(cp)
        for t in range(n):
            theirs = outs[t].at[1 - c]
            pltpu.make_async_remote_copy(
                src_ref=theirs, dst_ref=theirs, send_sem=ssem.at[t], recv_sem=rsem.at[t],
                device_id=(x, y, 1 - c), device_id_type=MESH).wait_recv()
        for cp in sends:
            cp.wait_send()

    return pl.pallas_call(
        body, name="grad_pair_share", in_specs=[ANY] * n, out_specs=[ANY] * n,
        out_shape=[jax.ShapeDtypeStruct(a.shape, a.dtype) for a in halves],
        input_output_aliases={t: t for t in range(n)},
        scratch_shapes=[pltpu.SemaphoreType.DMA((n,)), pltpu.SemaphoreType.DMA((n,))])(*halves)


def _small_allreduce(part):
    R, C = part.shape
    N_DEV = 8

    def body(in_ref, out_ref, slots, ssem, rsem):
        x, y, c, _ = _place()
        me = 4 * x + 2 * y + c
        sends = []
        for k in range(1, N_DEV):
            kx, ky, kc = (k >> 2) & 1, (k >> 1) & 1, k & 1
            peer = (1 - x if kx else x, 1 - y if ky else y, 1 - c if kc else c)
            cp = pltpu.make_async_remote_copy(
                src_ref=in_ref, dst_ref=slots.at[me], send_sem=ssem.at[k], recv_sem=rsem.at[k],
                device_id=peer, device_id_type=MESH)
            cp.start()
            sends.append(cp)
        slots[me] = in_ref[...]
        for k in range(1, N_DEV):
            kx, ky, kc = (k >> 2) & 1, (k >> 1) & 1, k & 1
            peer = (1 - x if kx else x, 1 - y if ky else y, 1 - c if kc else c)
            slot = slots.at[4 * peer[0] + 2 * peer[1] + peer[2]]
            pltpu.make_async_remote_copy(
                src_ref=slot, dst_ref=slot, send_sem=ssem.at[k], recv_sem=rsem.at[k],
                device_id=peer, device_id_type=MESH).wait_recv()
        acc = slots[0]
        for d in range(1, N_DEV):
            acc = acc + slots[d]
        out_ref[...] = acc
        for cp in sends:
            cp.wait_send()

    vm = pl.BlockSpec(memory_space=pltpu.VMEM)
    return pl.pallas_call(
        body, name="small_allreduce", in_specs=[vm], out_specs=vm,
        out_shape=jax.ShapeDtypeStruct((R, C), F32),
        scratch_shapes=[pltpu.VMEM((N_DEV, R, C), F32), pltpu.SemaphoreType.DMA((N_DEV,)),
                        pltpu.SemaphoreType.DMA((N_DEV,))])(part)


def _local_step(x, target, norm_mix, norm_mlp, norm_kv, norm_final, weights, sink, n_a, n_heads):
    B, S, D = x.shape
    T = B * S
    C = n_heads * HEAD_DIM
    depth = norm_mix.shape[0]
    slopes = 2.0 ** (-ALIBI_MAX_BIAS * jnp.arange(1, n_heads + 1, dtype=F32) / n_heads)
    tm = min(512, T)
    row = lambda v: v.reshape(1, -1)

    h = x.reshape(T, D)
    saved, Wl = [], []
    kv = nkv = h_kv = cwg = None
    for l in range(depth):
        s = {"h_in": h}
        w = {}
        Wl.append(w)
        if l < n_a:
            w["w_a_in"] = weights.get(l, "w_a_in", h)
            s["n1"], bcu = _norm_mm(f"a_in_fwd{l}", h, row(norm_mix[l]), w["w_a_in"], 0, 3, BF16, tm)
            s["bcu"] = bcu.reshape(3, B, S, D)
            if l == 0:
                cwg = weights.get(0, "conv", bcu)[:, 0, :n_a * 3].reshape(N_CHIPS, n_a, 3, -1)
            s["z"] = _conv_fwd(f"conv_fwd{l}", s["bcu"], cwg, l, LANES).reshape(T, D)
            w["w_a_out"] = weights.get(l, "w_a_out", s["z"])
            h = _mm_res_rows(f"a_out_fwd{l}", s["z"], w["w_a_out"], 0, h, _to_bf16, tm)
        else:
            i = l - n_a
            if i == 0:
                h_kv = h
                w["w_kv"] = weights.get(l, "w_kv", h)
                nkv, kv = _norm_mm("kv_fwd", h, row(norm_kv), w["w_kv"], 0, 1, F32, tm)
                kv = kv.reshape(B, S, 2 * 3 * C)
            w["w_q"] = weights.get(l, "w_q", h)
            s["n1"], q = _norm_mm(f"q_fwd{i}", h, row(norm_mix[l]), w["w_q"], 0, 1, F32, tm)
            s["q"] = q.reshape(B, S, 3 * C)
            o, lse = _attn_fwd(f"attn_fwd{i}", s["q"], kv, slopes, n_heads)
            s["o"], s["lse"] = o.reshape(T, C), lse.reshape(T, C)
            w["w_o"] = weights.get(l, "w_o", o)
            h = _mm_res_cols(f"o_fwd{i}", s["o"], w["w_o"], 0, h, tm)
        s["h_mid"] = h
        w["w_up"] = weights.get(l, "w_up", h)
        w["w_down"] = weights.get(l, "w_down", h)
        s["n2"], s["a"], h = _mlp_fwd(f"mlp_fwd{l}", h, row(norm_mlp[l]), w["w_up"], w["w_down"], tm)
        F = s["a"].shape[1]
        saved.append(s)

    loss, dh, dh16, dg_final = _final_loss("loss_head", h, row(norm_final), target.reshape(T, D), tm)

    g_mix, g_mlp = [None] * depth, [None] * depth
    g_conv = [None] * n_a
    dkv = None
    tt = min(512, T)
    deps = []
    for l in reversed(range(depth)):
        s, w = saved[l], Wl[l]
        g_down = _tn(f"down_wgrad{l}", s["a"], _relu2_bf16, [_seg2d(dh16, tt, 2)], None, False,
                     min(2048, F), tt, deps, BF16).reshape(N_CHIPS, F // N_CHIPS, D)
        da, dh, dh16, g_mlp[l] = _mlp_bwd(f"mlp_bwd{l}", dh, dh16, s["a"], w["w_down"], w["w_up"], s["h_mid"],
                                          row(norm_mlp[l]), tm)
        g_up = _tn(f"up_wgrad{l}", s["n2"], _to_bf16, [_seg2d(da, tt, 2)], F // N_CHIPS, True, D, tt, (), BF16)
        deps = sink.pump(dh) + [sink.submit({("w_up", l): g_up, ("w_down", l): g_down})]
        if l < n_a:
            g_out = _tn(f"a_out_wgrad{l}", s["z"], _to_bf16, [_seg2d(dh16, tt, 2)], None, False,
                        D, tt, deps, BF16).reshape(N_CHIPS, D // N_CHIPS, D)
            dz = _nt_rows(f"a_out_bwd{l}", dh16, w["w_a_out"], 0, None, F32, tm)
            deps = sink.pump(dz) + [sink.submit({("w_a_out", l): g_out})]
            dbcu, g_conv[l] = _conv_bwd(f"conv_bwd{l}", s["bcu"], dz.reshape(B, S, D), cwg, l, LANES)
            dbcu = dbcu.reshape(3, T, D)
            g_in = _tn(f"a_in_wgrad{l}", s["n1"], _to_bf16, [_seg_plane(dbcu, p, tt, 2) for p in range(3)],
                       3 * D // N_CHIPS, True, D, tt, deps, BF16)
            dh, dh16, g_mix[l] = _nt_cols(f"a_in_bwd{l}", [_seg_plane(dbcu, p, tm, 1) for p in range(3)],
                                          w["w_a_in"], 0, tm, (s["h_in"], row(norm_mix[l]), dh))
            mixer = {("w_a_in", l): g_in}
        else:
            i = l - n_a
            g_o = _tn(f"o_wgrad{i}", s["o"], _to_bf16, [_seg2d(dh16, tt, 2)], D // N_CHIPS, True, C, tt, deps,
                      BF16)
            do = _nt_cols(f"o_bwd{i}", [_seg2d(dh16, tm, 1)], w["w_o"], 0, tm, None)
            deps = sink.pump(do) + [sink.submit({("w_o", i): g_o})]
            dq, dk, dv = _attn_bwd(f"attn_bwd{i}", s["q"], kv, slopes, s["o"].reshape(B, S, C),
                                   s["lse"].reshape(B, S, C), do.reshape(B, S, C), n_heads, dkv)
            dkv = (dk, dv)
            dq = dq.reshape(T, 3 * C)
            g_q = _tn(f"q_wgrad{i}", s["n1"], _to_bf16, [_seg2d(dq, tt, 2)], 3 * C // N_CHIPS, True, D, tt, deps,
                      BF16)
            dh, dh16, g_mix[l] = _nt_cols(f"q_bwd{i}", [_seg2d(dq, tm, 1)], w["w_q"], 0, tm,
                                          (s["h_in"], row(norm_mix[l]), dh))
            mixer = {("w_q", i): g_q}
            if i == 0:
                dk2, dv2 = (t.reshape(T, 3 * C) for t in dkv)
                mixer[("w_kv", 0)] = _tn("kv_wgrad", nkv, _to_bf16, _kv_segments(dk2, dv2, C, tt, 2),
                                         6 * C // N_CHIPS, True, D, tt, (), BF16)
                dh, dh16, g_kv = _nt_cols("kv_bwd", _kv_segments(dk2, dv2, C, tm, 1), w["w_kv"], 0, tm,
                                          (h_kv, row(norm_kv), dh))
        deps = sink.pump(dh) + [sink.submit(mixer)]
    small = dict(norm_mix=jnp.concatenate(g_mix, axis=0), norm_mlp=jnp.concatenate(g_mlp, axis=0),
                 norm_kv=g_kv, norm_final=dg_final, conv_w=jnp.stack(g_conv))
    return loss, dh.reshape(B, S, D), small


BIG = ("w_a_in", "w_a_out", "w_kv", "w_q", "w_o", "w_up", "w_down")
CONV_PAD_ROWS = 16


def kernel(x, norm_mix, norm_mlp, w_a_in, conv_w, w_a_out, norm_kv, w_kv, w_q, w_o, w_up, w_down, norm_final, loss_target, m_norm_mix, m_norm_mlp, m_w_a_in, m_conv_w, m_w_a_out, m_norm_kv, m_w_kv, m_w_q, m_w_o, m_w_up, m_w_down, m_norm_final, v_norm_mix, v_norm_mlp, v_w_a_in, v_conv_w, v_w_a_out, v_norm_kv, v_w_kv, v_w_q, v_w_o, v_w_up, v_w_down, v_norm_final):
    D = x.shape[-1]
    w = dict(norm_mix=norm_mix, norm_mlp=norm_mlp, w_a_in=w_a_in, conv_w=conv_w, w_a_out=w_a_out, norm_kv=norm_kv,
             w_kv=w_kv[None], w_q=w_q, w_o=w_o, w_up=w_up, w_down=w_down, norm_final=norm_final)
    m = dict(norm_mix=m_norm_mix, norm_mlp=m_norm_mlp, w_a_in=m_w_a_in, conv_w=m_conv_w, w_a_out=m_w_a_out,
             norm_kv=m_norm_kv, w_kv=m_w_kv[None], w_q=m_w_q, w_o=m_w_o, w_up=m_w_up, w_down=m_w_down,
             norm_final=m_norm_final)
    v = dict(norm_mix=v_norm_mix, norm_mlp=v_norm_mlp, w_a_in=v_w_a_in, conv_w=v_conv_w, w_a_out=v_w_a_out,
             norm_kv=v_norm_kv, w_kv=v_w_kv[None], w_q=v_w_q, w_o=v_w_o, w_up=v_w_up, w_down=v_w_down,
             norm_final=v_norm_final)
    depth = norm_mix.shape[0]
    n_a, taps, cwc = conv_w.shape
    n_heads = w_o.shape[1] // HEAD_DIM

    conv_rows = jnp.zeros((CONV_PAD_ROWS, cwc), F32).at[:n_a * taps].set(conv_w.reshape(n_a * taps, cwc))
    blocks = {}
    for l in range(depth):
        if l < n_a:
            blocks[(l, "w_a_in")] = w_a_in[l].astype(BF16)
            if l == 0:
                blocks[(0, "conv")] = conv_rows
            blocks[(l, "w_a_out")] = w_a_out[l].astype(BF16)
        else:
            if l == n_a:
                blocks[(l, "w_kv")] = w_kv.astype(BF16)
            blocks[(l, "w_q")] = w_q[l - n_a].astype(BF16)
            blocks[(l, "w_o")] = w_o[l - n_a].astype(BF16)
        blocks[(l, "w_up")] = w_up[l].astype(BF16)
        blocks[(l, "w_down")] = w_down[l].astype(BF16)
    weights = _WeightGather(blocks)
    place = jnp.stack([2 * lax.axis_index("x") + lax.axis_index("y"), lax.axis_index("c")]).astype(jnp.int32)
    sink = _GradReduce(place)

    loss, grad_x, small = _local_step(x, loss_target, norm_mix, norm_mlp, norm_kv, norm_final, weights, sink,
                                      n_a, n_heads)
    loss = lax.psum(loss[0, 0], ("x", "y", "c"))

    done = sink.finish(grad_x)
    keys = list(done)
    shared = dict(zip(keys, _pair_share([done[k] for k in keys])))
    grads = {}

    packed = jnp.concatenate([small["norm_mix"], small["norm_mlp"], small["norm_kv"], small["norm_final"],
                              small["conv_w"].reshape(n_a * taps, D)], axis=0)
    pad = (-packed.shape[0]) % 8
    packed = jnp.pad(packed, ((0, pad), (0, 0)))
    total = _small_allreduce(packed)
    grads["norm_mix"] = total[:depth]
    grads["norm_mlp"] = total[depth:2 * depth]
    grads["norm_kv"] = total[2 * depth]
    grads["norm_final"] = total[2 * depth + 1]
    chip = 2 * lax.axis_index("x") + lax.axis_index("y")
    conv_full = total[2 * depth + 2:2 * depth + 2 + n_a * taps].reshape(n_a, taps, N_CHIPS, cwc)
    grads["conv_w"] = lax.dynamic_index_in_dim(conv_full, chip, axis=2, keepdims=False)

    order = ("norm_mix", "norm_mlp", "w_a_in", "conv_w", "w_a_out", "norm_kv", "w_kv", "w_q", "w_o", "w_up",
             "w_down", "norm_final")
    delta, new_m, new_v = {}, {}, {}
    vec_names = ("norm_mix", "norm_mlp", "norm_kv", "norm_final")
    rows_of = lambda a: a.reshape(-1, D)
    vw, vg, vm_, vv = (jnp.concatenate([rows_of(t[k]) for k in vec_names], axis=0) for t in (w, grads, m, v))
    vpad = (-vw.shape[0]) % 8
    padrows = lambda a: jnp.pad(a, ((0, vpad), (0, 0)))
    vd, vnm, vnv = _adamw("adamw_norms", padrows(vw), padrows(vg), padrows(vm_), padrows(vv))
    off = 0
    for k in vec_names:
        r = rows_of(w[k]).shape[0]
        delta[k] = vd[off:off + r].reshape(w[k].shape)
        new_m[k] = vnm[off:off + r].reshape(w[k].shape)
        new_v[k] = vnv[off:off + r].reshape(w[k].shape)
        off += r
    cpad = (-n_a * taps) % 8
    two_d = lambda a: jnp.pad(a.reshape(-1, cwc), ((0, cpad), (0, 0)))
    cd, cnm, cnv = _adamw("adamw_conv_w", two_d(w["conv_w"]), two_d(grads["conv_w"]), two_d(m["conv_w"]),
                          two_d(v["conv_w"]))
    delta["conv_w"], new_m["conv_w"], new_v["conv_w"] = (t[:n_a * taps].reshape(conv_w.shape) for t in (cd, cnm, cnv))
    for k in BIG:
        per_layer = [shared[(k, l)].reshape(w[k].shape[1:]) for l in range(w[k].shape[0])]
        grads[k], delta[k], new_m[k], new_v[k] = _adamw_layers(f"adamw_{k}", w[k], per_layer, m[k], v[k])
    fix = lambda k, a: a[0] if k == "w_kv" else a
    return (loss, grad_x, *[fix(k, grads[k]) for k in order], *[fix(k, delta[k]) for k in order],
            *[fix(k, new_m[k]) for k in order], *[fix(k, new_v[k]) for k in order])
```

```python
import functools

import jax
import jax.numpy as jnp
from jax import lax
from jax.experimental import pallas as pl
from jax.experimental.pallas import tpu as pltpu

F32 = jnp.float32
BF16 = jnp.bfloat16
MESH = pl.DeviceIdType.MESH

EPS = 1e-5
PATTERNS = ((128, 1), (512, 4), (2048, 16))
HEAD_DIM = 64
ALIBI_MAX_BIAS = 8.0
NEG_INF = -1e30
ATT_BLK = 128
BWD_UNROLL = 8
N_CHIPS = 4
LANES = 128
VMEM_LIMIT = 56 * 1024 * 1024

ADAM_LR = 0.001
ADAM_B1 = 0.9
ADAM_B2 = 0.999
ADAM_EPS = 1e-08
ADAM_WD = 0.01
ADAM_STEP = 10


ANY = pl.BlockSpec(memory_space=pl.ANY)


def _params(n_grid_axes):
    return pltpu.CompilerParams(dimension_semantics=("arbitrary",) * n_grid_axes, vmem_limit_bytes=VMEM_LIMIT)


def _dot(a, b):
    return jnp.dot(a, b, preferred_element_type=F32)


def _dot_nt(a, b):
    return lax.dot_general(a, b, (((1,), (1,)), ((), ())), preferred_element_type=F32)


def _dot_tn(a, b):
    return lax.dot_general(a, b, (((0,), (0,)), ((), ())), preferred_element_type=F32)


def _relu2(a):
    return jnp.square(jnp.maximum(a, 0.0))


def _rms(hf, g):
    y = hf * lax.rsqrt(jnp.mean(hf * hf, axis=-1, keepdims=True) + EPS)
    return y * g


def _rms_bwd(hf, g, dn):
    rstd = lax.rsqrt(jnp.mean(hf * hf, axis=-1, keepdims=True) + EPS)
    xhat = hf * rstd
    dg = jnp.sum(dn * xhat, axis=0, keepdims=True)
    dx = dn * g
    dh = rstd * (dx - xhat * jnp.mean(dx * xhat, axis=-1, keepdims=True))
    return dh, dg


def _pieces(seg_widths, chunk_width, max_width):
    total = sum(seg_widths)
    cuts = {0, total}
    acc = 0
    for w in seg_widths:
        cuts.add(acc)
        acc += w
    cuts.update(range(0, total, chunk_width))
    cuts = sorted(cuts)
    fine = []
    for lo, hi in zip(cuts[:-1], cuts[1:]):
        while hi - lo > max_width:
            fine.append((lo, lo + max_width))
            lo += max_width
        fine.append((lo, hi))
    out = []
    for lo, hi in fine:
        acc = 0
        for s, w in enumerate(seg_widths):
            if lo < acc + w:
                break
            acc += w
        out.append((s, lo - acc, lo // chunk_width, lo % chunk_width, hi - lo))
    return out


def _relu2_bf16(a):
    return _relu2(a.astype(F32)).astype(BF16)


def _to_bf16(a):
    return a.astype(BF16)


def _norm_mm(name, h, g, wg, layer, planes, out_dtype, tm, deps=()):
    T, D = h.shape
    cw = wg.shape[3]
    N = N_CHIPS * cw
    pw = N // planes
    pieces = _pieces([pw] * planes, cw, 512)

    def body(h_ref, g_ref, w_ref, *rest):
        n_ref, o_ref = rest[len(deps):]
        n = _rms(h_ref[...], g_ref[...]).astype(BF16)
        n_ref[...] = n
        for s, a0, ch, b0, wd in pieces:
            o_ref[s, :, a0:a0 + wd] = _dot(n, w_ref[ch, :, b0:b0 + wd]).astype(out_dtype)

    return pl.pallas_call(
        body, name=name, grid=(T // tm,),
        in_specs=[pl.BlockSpec((tm, D), lambda i: (i, 0)),
                  pl.BlockSpec((1, D), lambda i: (0, 0)),
                  pl.BlockSpec((N_CHIPS, None, D, cw), lambda i: (0, layer, 0, 0))] + [ANY] * len(deps),
        out_specs=[pl.BlockSpec((tm, D), lambda i: (i, 0)),
                   pl.BlockSpec((planes, tm, pw), lambda i: (0, i, 0))],
        out_shape=[jax.ShapeDtypeStruct((T, D), BF16), jax.ShapeDtypeStruct((planes, T, pw), out_dtype)],
        compiler_params=_params(1))(h, g, wg, *deps)


def _resident(shape, index_map):
    return pl.BlockSpec(shape, index_map, pipeline_mode=pl.Buffered(1))


def _mm_res_rows(name, a, wg, layer, h, act, tm):
    T = a.shape[0]
    rk, D = wg.shape[2], wg.shape[3]

    def body(a_ref, w_ref, h_ref, o_ref):
        acc = h_ref[...]
        for k in range(N_CHIPS):
            acc = acc + _dot(act(a_ref[:, k * rk:(k + 1) * rk]), w_ref[k])
        o_ref[...] = acc

    return pl.pallas_call(
        body, name=name, grid=(T // tm,),
        in_specs=[pl.BlockSpec((tm, N_CHIPS * rk), lambda i: (i, 0)),
                  pl.BlockSpec((N_CHIPS, None, rk, D), lambda i: (0, layer, 0, 0)),
                  pl.BlockSpec((tm, D), lambda i: (i, 0))],
        out_specs=pl.BlockSpec((tm, D), lambda i: (i, 0)),
        out_shape=jax.ShapeDtypeStruct((T, D), F32),
        compiler_params=_params(1))(a, wg, h)


def _mm_res_cols(name, a, wg, layer, h, tm):
    T, K = a.shape
    cw = wg.shape[3]
    D = N_CHIPS * cw

    def body(a_ref, w_ref, h_ref, o_ref):
        a16 = a_ref[...].astype(BF16)
        for j in range(N_CHIPS):
            o_ref[:, j * cw:(j + 1) * cw] = h_ref[:, j * cw:(j + 1) * cw] + _dot(a16, w_ref[j])

    return pl.pallas_call(
        body, name=name, grid=(T // tm,),
        in_specs=[pl.BlockSpec((tm, K), lambda i: (i, 0)),
                  pl.BlockSpec((N_CHIPS, None, K, cw), lambda i: (0, layer, 0, 0)),
                  pl.BlockSpec((tm, D), lambda i: (i, 0))],
        out_specs=pl.BlockSpec((tm, D), lambda i: (i, 0)),
        out_shape=jax.ShapeDtypeStruct((T, D), F32),
        compiler_params=_params(1))(a, wg, h)


def _mlp_fwd(name, h, g, wup, wdown, tm):
    T, D = h.shape
    cw = wup.shape[3]

    def body(h_ref, g_ref, wu_ref, wd_ref, n_ref, a_ref, o_ref):
        hf = h_ref[...]
        n = _rms(hf, g_ref[...]).astype(BF16)
        n_ref[...] = n
        acc = hf
        for ch in range(N_CHIPS):
            a16 = _dot(n, wu_ref[ch]).astype(BF16)
            a_ref[:, ch * cw:(ch + 1) * cw] = a16
            acc = acc + _dot(_relu2_bf16(a16), wd_ref[ch])
        o_ref[...] = acc

    row = pl.BlockSpec((tm, D), lambda i: (i, 0))
    return pl.pallas_call(
        body, name=name, grid=(T // tm,),
        in_specs=[row, pl.BlockSpec((1, D), lambda i: (0, 0)),
                  _resident((N_CHIPS, None, D, cw), lambda i: (0, 0, 0, 0)),
                  _resident((N_CHIPS, None, cw, D), lambda i: (0, 0, 0, 0))],
        out_specs=[row, pl.BlockSpec((tm, N_CHIPS * cw), lambda i: (i, 0)), row],
        out_shape=[jax.ShapeDtypeStruct((T, D), BF16), jax.ShapeDtypeStruct((T, N_CHIPS * cw), BF16),
                   jax.ShapeDtypeStruct((T, D), F32)],
        compiler_params=_params(1))(h, g, wup, wdown)


def _mlp_bwd(name, dh, dh16, a, wdown, wup, h_mid, g, tm, deps=()):
    T, D = dh.shape
    cw = wup.shape[3]
    F = N_CHIPS * cw

    def body(dh_ref, dh16_ref, a_ref, wd_ref, wu_ref, h_ref, g_ref, *rest):
        da_ref, out_ref, out16_ref, dg_ref = rest[len(deps):]
        d16 = dh16_ref[...]
        acc = None
        for ch in range(N_CHIPS):
            cols = slice(ch * cw, (ch + 1) * cw)
            da = (_dot_nt(d16, wd_ref[ch]) * (2.0 * jnp.maximum(a_ref[:, cols].astype(F32), 0.0))).astype(BF16)
            da_ref[:, cols] = da
            d = _dot_nt(da, wu_ref[ch])
            acc = d if acc is None else acc + d
        dh_c, dg = _rms_bwd(h_ref[...], g_ref[...], acc)
        out = dh_ref[...] + dh_c
        out_ref[...] = out
        out16_ref[...] = out.astype(BF16)

        @pl.when(pl.program_id(0) == 0)
        def _():
            dg_ref[...] = dg

        @pl.when(pl.program_id(0) > 0)
        def _():
            dg_ref[...] += dg

    row = pl.BlockSpec((tm, D), lambda i: (i, 0))
    wide = pl.BlockSpec((tm, F), lambda i: (i, 0))
    vec = pl.BlockSpec((1, D), lambda i: (0, 0))
    return pl.pallas_call(
        body, name=name, grid=(T // tm,),
        in_specs=[row, row, wide, _resident((N_CHIPS, None, cw, D), lambda i: (0, 0, 0, 0)),
                  _resident((N_CHIPS, None, D, cw), lambda i: (0, 0, 0, 0)), row, vec] + [ANY] * len(deps),
        out_specs=[wide, row, row, vec],
        out_shape=[jax.ShapeDtypeStruct((T, F), BF16), jax.ShapeDtypeStruct((T, D), F32),
                   jax.ShapeDtypeStruct((T, D), BF16), jax.ShapeDtypeStruct((1, D), F32)],
        compiler_params=_params(1))(dh, dh16, a, wdown, wup, h_mid, g, *deps)


CONV_ROWS = 256
CONV_HALO = 16


def _conv_shifted(ext, k, r0, rows):
    rolled = pltpu.roll(ext, k, 0)[CONV_HALO:]
    t = r0 + lax.broadcasted_iota(jnp.int32, rolled.shape, 0)
    return jnp.where(t >= k, rolled, 0.0)


def _conv_ahead(ext, k, r0, rows, S):
    rolled = pltpu.roll(ext, rows + CONV_HALO - k, 0)[:rows]
    t = r0 + lax.broadcasted_iota(jnp.int32, rolled.shape, 0)
    return jnp.where(t + k < S, rolled, 0.0)


def _conv_fwd(name, bcu, cwg, layer, tc):
    _, B, S, D = bcu.shape
    cwc = cwg.shape[3]
    per_chunk = cwc // tc
    R = min(CONV_ROWS, S)

    def body(x_ref, w_ref, z_ref):
        w = [w_ref[k:k + 1, :] for k in range(3)]

        def step(i, carry):
            r0 = pl.multiple_of(i * R, R)
            h0 = pl.multiple_of(jnp.maximum(r0 - CONV_HALO, 0), CONV_HALO)
            ld = lambda p, start, rows: x_ref[p, pl.ds(start, rows), :].astype(F32)
            cu = jnp.concatenate([ld(1, h0, CONV_HALO) * ld(2, h0, CONV_HALO), ld(1, r0, R) * ld(2, r0, R)], axis=0)
            conv = w[0] * cu[CONV_HALO:]
            conv = conv + w[1] * _conv_shifted(cu, 1, r0, R)
            conv = conv + w[2] * _conv_shifted(cu, 2, r0, R)
            z_ref[pl.ds(r0, R), :] = (ld(0, r0, R) * conv).astype(BF16)
            return carry

        lax.fori_loop(0, S // R, step, 0)

    return pl.pallas_call(
        body, name=name, grid=(B, D // tc),
        in_specs=[pl.BlockSpec((3, None, S, tc), lambda b, j: (0, b, 0, j)),
                  pl.BlockSpec((None, None, 3, tc), lambda b, j: (j // per_chunk, layer, 0, j % per_chunk))],
        out_specs=pl.BlockSpec((None, S, tc), lambda b, j: (b, 0, j)),
        out_shape=jax.ShapeDtypeStruct((B, S, D), BF16),
        compiler_params=_params(2))(bcu, cwg)


def _conv_bwd(name, bcu, dz, cwg, layer, tc):
    _, B, S, D = bcu.shape
    cwc = cwg.shape[3]
    per_chunk = cwc // tc
    R = min(CONV_ROWS, S)

    def body(x_ref, dz_ref, w_ref, d_ref, dw_ref):
        w = [w_ref[k:k + 1, :] for k in range(3)]

        @pl.when(pl.program_id(1) == 0)
        def _():
            dw_ref[...] = jnp.zeros_like(dw_ref)

        def step(i, carry):
            r0 = pl.multiple_of(i * R, R)
            h0 = pl.multiple_of(jnp.maximum(r0 - CONV_HALO, 0), CONV_HALO)
            a0 = pl.multiple_of(jnp.minimum(r0 + R, S - CONV_HALO), CONV_HALO)
            ld = lambda p, start, rows: x_ref[p, pl.ds(start, rows), :].astype(F32)
            b, c, u = ld(0, r0, R), ld(1, r0, R), ld(2, r0, R)
            dz = dz_ref[pl.ds(r0, R), :]
            cu = jnp.concatenate([ld(1, h0, CONV_HALO) * ld(2, h0, CONV_HALO), c * u], axis=0)
            cu1 = _conv_shifted(cu, 1, r0, R)
            cu2 = _conv_shifted(cu, 2, r0, R)
            conv = w[0] * (c * u) + w[1] * cu1 + w[2] * cu2
            dconv = dz * b
            dca = jnp.concatenate([dconv, dz_ref[pl.ds(a0, CONV_HALO), :] * ld(0, a0, CONV_HALO)], axis=0)
            dcu = w[0] * dconv + w[1] * _conv_ahead(dca, 1, r0, R, S) + w[2] * _conv_ahead(dca, 2, r0, R, S)
            d_ref[0, pl.ds(r0, R), :] = (dz * conv).astype(BF16)
            d_ref[1, pl.ds(r0, R), :] = (dcu * u).astype(BF16)
            d_ref[2, pl.ds(r0, R), :] = (dcu * c).astype(BF16)
            return (carry[0] + jnp.sum(dconv * (c * u), axis=0, keepdims=True),
                    carry[1] + jnp.sum(dconv * cu1, axis=0, keepdims=True),
                    carry[2] + jnp.sum(dconv * cu2, axis=0, keepdims=True))

        zero = jnp.zeros((1, tc), F32)
        s0, s1, s2 = lax.fori_loop(0, S // R, step, (zero, zero, zero))
        for k, sk in enumerate((s0, s1, s2)):
            dw_ref[k:k + 1, :] += sk

    return pl.pallas_call(
        body, name=name, grid=(D // tc, B),
        in_specs=[pl.BlockSpec((3, None, S, tc), lambda j, b: (0, b, 0, j)),
                  pl.BlockSpec((None, S, tc), lambda j, b: (b, 0, j)),
                  pl.BlockSpec((None, None, 3, tc), lambda j, b: (j // per_chunk, layer, 0, j % per_chunk))],
        out_specs=[pl.BlockSpec((3, None, S, tc), lambda j, b: (0, b, 0, j)),
                   pl.BlockSpec((3, tc), lambda j, b: (0, j))],
        out_shape=[jax.ShapeDtypeStruct((3, B, S, D), BF16), jax.ShapeDtypeStruct((3, D), F32)],
        compiler_params=_params(2))(bcu, dz, cwg)


def _att_rows(dil, idx, nb):
    r, n = idx // nb, idx % nb
    if dil == 1:
        cur = pl.ds(pl.multiple_of(n * ATT_BLK, ATT_BLK), ATT_BLK)
        prev = pl.ds(pl.multiple_of(jnp.maximum(n - 1, 0) * ATT_BLK, ATT_BLK), ATT_BLK)
    else:
        cur = pl.ds(n * (ATT_BLK * dil) + r, ATT_BLK, stride=dil)
        prev = pl.ds(jnp.maximum(n - 1, 0) * (ATT_BLK * dil) + r, ATT_BLK, stride=dil)
    return n, cur, prev


def _att_bias(bias_ref, dil, sl_ref, hp):
    row = lax.broadcasted_iota(jnp.int32, (2 * ATT_BLK, 2 * ATT_BLK), 0)
    ci = lax.broadcasted_iota(jnp.int32, (2 * ATT_BLK, 2 * ATT_BLK), 1)
    j = ATT_BLK + (row & (ATT_BLK - 1)) - ci
    slope = jnp.where(row < ATT_BLK, sl_ref[2 * hp], sl_ref[2 * hp + 1])
    rest = jnp.where((j >= 0) & (j <= ATT_BLK), -slope * (dil * j).astype(F32), NEG_INF)
    bias_ref[1] = rest
    bias_ref[0] = jnp.where(ci >= ATT_BLK, rest, NEG_INF)


def _stack_heads(x16, lane):
    first = lane < HEAD_DIM
    return jnp.concatenate([jnp.where(first, x16, jnp.zeros_like(x16)),
                            jnp.where(first, jnp.zeros_like(x16), x16)], axis=0)


def _per_head(col, lane):
    return jnp.where(lane < HEAD_DIM, col[:ATT_BLK], col[ATT_BLK:])


def _attn_fwd(name, q, kv, slopes, n_heads):
    B, S, CQ = q.shape
    HP = n_heads * HEAD_DIM // LANES
    scale = HEAD_DIM ** -0.5
    n_groups = len(PATTERNS)
    CH = 256

    def body(sl_ref, q_ref, k_ref, v_ref, o_ref, lse_ref, bias_ref, *parts):
        og, lg = parts[:n_groups], parts[n_groups:]
        hp, g = pl.program_id(1), pl.program_id(2)
        lane = lax.broadcasted_iota(jnp.int32, (1, LANES), 1)

        for gi, (window, dil) in enumerate(PATTERNS):
            nb = S // dil // ATT_BLK

            @pl.when(g == gi)
            def _(gi=gi, dil=dil, nb=nb):
                _att_bias(bias_ref, dil, sl_ref, hp)

                def step(idx, carry):
                    n, cur, prev = _att_rows(dil, idx, nb)
                    qs = _stack_heads((q_ref[cur, :] * scale).astype(BF16), lane)
                    kc = jnp.concatenate([k_ref[prev, :], k_ref[cur, :]], axis=0).astype(BF16)
                    vc = jnp.concatenate([v_ref[prev, :], v_ref[cur, :]], axis=0).astype(BF16)
                    s = _dot_nt(qs, kc) + bias_ref[jnp.minimum(n, 1)]
                    m = jnp.max(s, axis=-1, keepdims=True)
                    p = jnp.exp(s - m)
                    l = jnp.sum(p, axis=-1, keepdims=True)
                    p16 = p.astype(BF16)
                    o_un = _dot(jnp.concatenate([p16[:ATT_BLK], p16[ATT_BLK:]], axis=1), _stack_heads_rows(vc, lane))
                    og[gi][cur, :] = o_un / _per_head(l, lane)
                    lg[gi][cur, :] = _per_head(m + jnp.log(l), lane)
                    return carry

                lax.fori_loop(0, S // ATT_BLK, step, 0, unroll=8)

        @pl.when(g == n_groups - 1)
        def _():
            def comb(i, carry):
                rows = pl.ds(pl.multiple_of(i * CH, CH), CH)
                a, b, c = lg[0][rows, :], lg[1][rows, :], lg[2][rows, :]
                m = jnp.maximum(jnp.maximum(a, b), c)
                ea, eb, ec = jnp.exp(a - m), jnp.exp(b - m), jnp.exp(c - m)
                z = ea + eb + ec
                o_ref[rows, :] = (ea / z) * og[0][rows, :] + (eb / z) * og[1][rows, :] + (ec / z) * og[2][rows, :]
                lse_ref[rows, :] = m + jnp.log(z)
                return carry

            lax.fori_loop(0, S // CH, comb, 0)

    blk = (None, S, LANES)
    out = pl.BlockSpec(blk, lambda b, hp, g: (b, 0, hp))
    return pl.pallas_call(
        body, name=name, grid=(B, HP, n_groups),
        in_specs=[pl.BlockSpec(memory_space=pltpu.SMEM),
                  pl.BlockSpec(blk, lambda b, hp, g: (b, 0, g * HP + hp)),
                  pl.BlockSpec(blk, lambda b, hp, g: (b, 0, g * 2 * HP + hp)),
                  pl.BlockSpec(blk, lambda b, hp, g: (b, 0, g * 2 * HP + HP + hp))],
        out_specs=[out, out],
        out_shape=[jax.ShapeDtypeStruct((B, S, HP * LANES), F32)] * 2,
        scratch_shapes=[pltpu.VMEM((2, 2 * ATT_BLK, 2 * ATT_BLK), F32)] + [pltpu.VMEM((S, LANES), F32)] * (2 * n_groups),
        compiler_params=_params(3))(slopes, q, kv, kv)


def _stack_heads_rows(x16, lane):
    first = lane < HEAD_DIM
    return jnp.concatenate([jnp.where(first, x16, jnp.zeros_like(x16)),
                            jnp.where(first, jnp.zeros_like(x16), x16)], axis=0)


def _attn_bwd(name, q, kv, slopes, o, lse, do, n_heads, dkv_prev):
    B, S, CQ = q.shape
    HP = n_heads * HEAD_DIM // LANES
    scale = HEAD_DIM ** -0.5
    n_groups = len(PATTERNS)
    n_prev = 0 if dkv_prev is None else 2

    def body(sl_ref, q_ref, k_ref, v_ref, o_ref, lse_ref, do_ref, *rest):
        dq_ref, dk_ref, dv_ref, bias_ref = rest[n_prev:]
        hp, g = pl.program_id(1), pl.program_id(2)
        lane = lax.broadcasted_iota(jnp.int32, (1, LANES), 1)
        first = lane < HEAD_DIM

        def flush(rows, dk, dv):
            if n_prev:
                dk = dk + rest[0][rows, :]
                dv = dv + rest[1][rows, :]
            dk_ref[rows, :] = dk
            dv_ref[rows, :] = dv

        for gi, (window, dil) in enumerate(PATTERNS):
            nb = S // dil // ATT_BLK
            n_blocks = S // ATT_BLK

            @pl.when(g == gi)
            def _(dil=dil, nb=nb, n_blocks=n_blocks):
                _att_bias(bias_ref, dil, sl_ref, hp)

                def block(idx, carry, first_of_all):
                    n, cur, prev = _att_rows(dil, idx, nb)
                    qs = _stack_heads((q_ref[cur, :] * scale).astype(BF16), lane)
                    kc = jnp.concatenate([k_ref[prev, :], k_ref[cur, :]], axis=0).astype(BF16)
                    vc = jnp.concatenate([v_ref[prev, :], v_ref[cur, :]], axis=0).astype(BF16)
                    dob = do_ref[cur, :]
                    prod = dob * o_ref[cur, :]
                    lseb = lse_ref[cur, :]
                    dos = _stack_heads(dob.astype(BF16), lane)
                    delta = jnp.concatenate(
                        [jnp.sum(jnp.where(first, prod, 0.0), axis=-1, keepdims=True),
                         jnp.sum(jnp.where(first, 0.0, prod), axis=-1, keepdims=True)], axis=0)
                    lse_col = jnp.concatenate(
                        [jnp.max(jnp.where(first, lseb, -jnp.inf), axis=-1, keepdims=True),
                         jnp.max(jnp.where(first, -jnp.inf, lseb), axis=-1, keepdims=True)], axis=0)
                    s = _dot_nt(qs, kc) + bias_ref[jnp.minimum(n, 1)]
                    p = jnp.exp(s - lse_col)
                    ds = p * (_dot_nt(dos, vc) - delta)
                    ds16 = ds.astype(BF16)
                    dq = _dot(jnp.concatenate([ds16[:ATT_BLK], ds16[ATT_BLK:]], axis=1), _stack_heads_rows(kc, lane))
                    dq_ref[cur, :] = dq * scale
                    dk = _dot_tn(ds16, qs)
                    dv = _dot_tn(p.astype(BF16), dos)

                    def flush_before():
                        _, before, _ = _att_rows(dil, idx - 1, nb)
                        flush(before, carry[0] + dk[:ATT_BLK], carry[1] + dv[:ATT_BLK])

                    if first_of_all:
                        pl.when(idx > 0)(flush_before)
                    else:
                        flush_before()
                    return dk[ATT_BLK:], dv[ATT_BLK:]

                def step(i, carry):
                    for u in range(BWD_UNROLL):
                        carry = block(i * BWD_UNROLL + u, carry, u == 0)
                    return carry

                zero = jnp.zeros((ATT_BLK, LANES), F32)
                dk_last, dv_last = lax.fori_loop(0, n_blocks // BWD_UNROLL, step, (zero, zero))
                _, last, _ = _att_rows(dil, n_blocks - 1, nb)
                flush(last, dk_last, dv_last)

    blk = (None, S, LANES)
    shared = pl.BlockSpec(blk, lambda b, hp, g: (b, 0, hp))
    grouped = pl.BlockSpec(blk, lambda b, hp, g: (b, 0, g * HP + hp))
    prev = [] if dkv_prev is None else list(dkv_prev)
    gshape = jax.ShapeDtypeStruct((B, S, n_groups * HP * LANES), F32)
    return pl.pallas_call(
        body, name=name, grid=(B, HP, n_groups),
        in_specs=[pl.BlockSpec(memory_space=pltpu.SMEM), grouped,
                  pl.BlockSpec(blk, lambda b, hp, g: (b, 0, g * 2 * HP + hp)),
                  pl.BlockSpec(blk, lambda b, hp, g: (b, 0, g * 2 * HP + HP + hp)),
                  shared, shared, shared] + [grouped] * n_prev,
        out_specs=[grouped] * 3, out_shape=[gshape] * 3,
        scratch_shapes=[pltpu.VMEM((2, 2 * ATT_BLK, 2 * ATT_BLK), F32)],
        compiler_params=_params(3))(slopes, q, kv, kv, o, lse, do, *prev)


def _final_loss(name, h, g, target, tm):
    T, D = h.shape

    def body(h_ref, g_ref, t_ref, loss_ref, dh_ref, dh16_ref, dg_ref):
        hf = h_ref[...]
        gv = g_ref[...]
        rstd = lax.rsqrt(jnp.mean(hf * hf, axis=-1, keepdims=True) + EPS)
        xhat = hf * rstd
        err = xhat * gv - t_ref[...]
        part = 0.5 * jnp.sum(jnp.mean(err * err, axis=-1, keepdims=True), axis=0, keepdims=True)
        dy = err * (1.0 / D)
        dg = jnp.sum(dy * xhat, axis=0, keepdims=True)
        dx = dy * gv
        dh = rstd * (dx - xhat * jnp.mean(dx * xhat, axis=-1, keepdims=True))
        dh_ref[...] = dh
        dh16_ref[...] = dh.astype(BF16)

        @pl.when(pl.program_id(0) == 0)
        def _():
            loss_ref[...] = part
            dg_ref[...] = dg

        @pl.when(pl.program_id(0) > 0)
        def _():
            loss_ref[...] += part
            dg_ref[...] += dg

    return pl.pallas_call(
        body, name=name, grid=(T // tm,),
        in_specs=[pl.BlockSpec((tm, D), lambda i: (i, 0)), pl.BlockSpec((1, D), lambda i: (0, 0)),
                  pl.BlockSpec((tm, D), lambda i: (i, 0))],
        out_specs=[pl.BlockSpec((1, 1), lambda i: (0, 0)), pl.BlockSpec((tm, D), lambda i: (i, 0)),
                   pl.BlockSpec((tm, D), lambda i: (i, 0)), pl.BlockSpec((1, D), lambda i: (0, 0))],
        out_shape=[jax.ShapeDtypeStruct((1, 1), F32), jax.ShapeDtypeStruct((T, D), F32),
                   jax.ShapeDtypeStruct((T, D), BF16), jax.ShapeDtypeStruct((1, D), F32)],
        compiler_params=_params(1))(h, g, target)


def _nt_rows(name, dh, wg, layer, a_mul, out_dtype, tm, deps=()):
    T, D = dh.shape
    rk = wg.shape[2]
    N = N_CHIPS * rk
    with_a = a_mul is not None

    def body(dh_ref, w_ref, *rest):
        o_ref = rest[-1]
        d16 = dh_ref[...]
        for ch in range(N_CHIPS):
            r = _dot_nt(d16, w_ref[ch])
            if with_a:
                r = r * (2.0 * jnp.maximum(rest[0][:, ch * rk:(ch + 1) * rk].astype(F32), 0.0))
            o_ref[:, ch * rk:(ch + 1) * rk] = r.astype(out_dtype)

    in_specs = [pl.BlockSpec((tm, D), lambda i: (i, 0)),
                pl.BlockSpec((N_CHIPS, None, rk, D), lambda i: (0, layer, 0, 0))]
    args = [dh, wg]
    if with_a:
        in_specs.append(pl.BlockSpec((tm, N), lambda i: (i, 0)))
        args.append(a_mul)
    in_specs += [ANY] * len(deps)
    args += list(deps)
    return pl.pallas_call(
        body, name=name, grid=(T // tm,), in_specs=in_specs,
        out_specs=pl.BlockSpec((tm, N), lambda i: (i, 0)),
        out_shape=jax.ShapeDtypeStruct((T, N), out_dtype),
        compiler_params=_params(1))(*args)


def _nt_cols(name, ysegs, wg, layer, tm, norm):
    Nw, cw = wg.shape[2], wg.shape[3]
    widths = [bs[-1] for _, bs, _ in ysegs]
    pieces = _pieces(widths, cw, 1024)
    ns = len(ysegs)
    T = norm[0].shape[0] if norm is not None else ysegs[0][0].shape[-2]

    def body(*refs):
        y_refs = refs[:ns]
        w_ref = refs[ns]
        acc = refs[-1]
        for n, (s, a0, ch, b0, wd) in enumerate(pieces):
            d = _dot_nt(y_refs[s][:, a0:a0 + wd].astype(BF16), w_ref[ch, :, b0:b0 + wd])
            if n == 0:
                acc[...] = d
            else:
                acc[...] += d
        if norm is None:
            refs[ns + 1][...] = acc[...]
        else:
            h_ref, g_ref, dhin_ref, out_ref, out16_ref, dg_ref = refs[ns + 1:ns + 7]
            dh_c, dg = _rms_bwd(h_ref[...], g_ref[...], acc[...])
            dh = dhin_ref[...] + dh_c
            out_ref[...] = dh
            out16_ref[...] = dh.astype(BF16)

            @pl.when(pl.program_id(0) == 0)
            def _():
                dg_ref[...] = dg

            @pl.when(pl.program_id(0) > 0)
            def _():
                dg_ref[...] += dg

    in_specs = [pl.BlockSpec(bs, im) for _, bs, im in ysegs]
    in_specs.append(pl.BlockSpec((N_CHIPS, None, Nw, cw), lambda i: (0, layer, 0, 0)))
    args = [a for a, _, _ in ysegs] + [wg]
    row = pl.BlockSpec((tm, Nw), lambda i: (i, 0))
    vec = pl.BlockSpec((1, Nw), lambda i: (0, 0))
    if norm is None:
        out_specs = row
        out_shape = jax.ShapeDtypeStruct((T, Nw), F32)
    else:
        in_specs += [row, vec, row]
        args += list(norm)
        out_specs = [row, row, vec]
        out_shape = [jax.ShapeDtypeStruct((T, Nw), F32), jax.ShapeDtypeStruct((T, Nw), BF16),
                     jax.ShapeDtypeStruct((1, Nw), F32)]
    return pl.pallas_call(
        body, name=name, grid=(T // tm,), in_specs=in_specs, out_specs=out_specs, out_shape=out_shape,
        scratch_shapes=[pltpu.VMEM((tm, Nw), F32)], compiler_params=_params(1))(*args)


def _tn(name, x, x_act, ysegs, cw, cols_layout, tmm, tt, deps=(), out_dtype=F32):
    T, M = x.shape
    widths = [bs[-1] for _, bs, _ in ysegs]
    N = sum(widths)
    pieces = _pieces(widths, cw if cols_layout else N, 1024)
    ns = len(ysegs)
    n_t = T // tt
    block = (N_CHIPS, tmm, cw) if cols_layout else (tmm, N)
    narrow = out_dtype != F32

    def body(x_ref, *refs):
        y_refs = refs[:ns]
        o_ref = refs[ns + len(deps)]
        acc = refs[-1] if narrow else o_ref

        @pl.when(pl.program_id(1) == 0)
        def _():
            acc[...] = jnp.zeros_like(acc)

        xt = x_act(x_ref[...])
        for s, a0, ch, b0, wd in pieces:
            d = _dot_tn(xt, y_refs[s][:, a0:a0 + wd].astype(BF16))
            if cols_layout:
                acc[ch, :, b0:b0 + wd] += d
            else:
                acc[:, b0:b0 + wd] += d
        if narrow:
            @pl.when(pl.program_id(1) == n_t - 1)
            def _():
                o_ref[...] = acc[...].astype(out_dtype)

    in_specs = [pl.BlockSpec((tt, tmm), lambda m, t: (t, m))] + [pl.BlockSpec(bs, im) for _, bs, im in ysegs]
    in_specs += [ANY] * len(deps)
    if cols_layout:
        out_specs = pl.BlockSpec(block, lambda m, t: (0, m, 0))
        out_shape = jax.ShapeDtypeStruct((N_CHIPS, M, cw), out_dtype)
    else:
        out_specs = pl.BlockSpec(block, lambda m, t: (m, 0))
        out_shape = jax.ShapeDtypeStruct((M, N), out_dtype)
    return pl.pallas_call(
        body, name=name, grid=(M // tmm, n_t), in_specs=in_specs, out_specs=out_specs, out_shape=out_shape,
        scratch_shapes=[pltpu.VMEM(block, F32)] if narrow else [],
        compiler_params=_params(2))(x, *[a for a, _, _ in ysegs], *deps)


def _seg2d(a, t_rows, grid_rank):
    w = a.shape[1]
    if grid_rank == 1:
        return (a, (t_rows, w), lambda i: (i, 0))
    return (a, (t_rows, w), lambda m, t: (t, 0))


def _kv_segments(dk, dv, C, t_rows, grid_rank):
    segs = []
    for g in range(len(PATTERNS)):
        for a in (dk, dv):
            if grid_rank == 1:
                segs.append((a, (t_rows, C), lambda i, g=g: (i, g)))
            else:
                segs.append((a, (t_rows, C), lambda m, t, g=g: (t, g)))
    return segs


def _seg_plane(a, plane, t_rows, grid_rank):
    w = a.shape[2]
    if grid_rank == 1:
        return (a, (None, t_rows, w), lambda i: (plane, i, 0))
    return (a, (None, t_rows, w), lambda m, t: (plane, t, 0))


def _row_tile(rows, row_bytes, budget_bytes=2 * 1024 * 1024):
    t = rows
    while t * row_bytes > budget_bytes and t % 32 == 0:
        t //= 2
    return t


N_DEVICES = 8


def _device_add(name, own, slots, place):
    _, _, hr, c = own.shape
    tr = _row_tile(hr, c * 4, 1024 * 1024)

    def body(place_ref, own_ref, *refs):
        o_ref = refs[-1]
        acc = own_ref[...].astype(F32)
        for r in refs[:-1]:
            acc = acc + r[...].astype(F32)
        o_ref[...] = acc

    def slot(k):
        return pl.BlockSpec((None, tr, c), lambda i, pr: ((2 * pr[0] + pr[1] + k) % N_DEVICES, i, 0))

    grid_spec = pltpu.PrefetchScalarGridSpec(
        num_scalar_prefetch=1, grid=(hr // tr,),
        in_specs=[pl.BlockSpec((None, None, tr, c), lambda i, pr: (pr[0], pr[1], i, 0))]
        + [slot(k) for k in range(1, N_DEVICES)],
        out_specs=pl.BlockSpec((None, tr, c), lambda i, pr: (pr[1], i, 0)))
    return pl.pallas_call(body, name=name, grid_spec=grid_spec,
                          out_shape=jax.ShapeDtypeStruct((2, hr, c), F32),
                          compiler_params=_params(1))(place, own, *[slots] * (N_DEVICES - 1))


def _adamw(name, w, g, m, v):
    rows, cols = w.shape
    tr = _row_tile(rows, cols * 4, 1024 * 1024)

    def body(w_ref, g_ref, m_ref, v_ref, d_ref, nm_ref, nv_ref):
        d_ref[...], nm_ref[...], nv_ref[...] = _adamw_math(w_ref[...], g_ref[...], m_ref[...], v_ref[...])

    spec = pl.BlockSpec((tr, cols), lambda i: (i, 0))
    return pl.pallas_call(
        body, name=name, grid=(rows // tr,), in_specs=[spec] * 4, out_specs=[spec] * 3,
        out_shape=[jax.ShapeDtypeStruct((rows, cols), F32)] * 3, compiler_params=_params(1))(w, g, m, v)


def _adamw_math(w, g, m, v):
    nm = ADAM_B1 * m + (1.0 - ADAM_B1) * g
    nv = ADAM_B2 * v + (1.0 - ADAM_B2) * jnp.square(g)
    m_hat = nm / (1.0 - ADAM_B1 ** ADAM_STEP)
    v_hat = nv / (1.0 - ADAM_B2 ** ADAM_STEP)
    return -ADAM_LR * (m_hat / (jnp.sqrt(v_hat) + ADAM_EPS) + ADAM_WD * w), nm, nv


def _adamw_layers(name, w, grads, m, v):
    L, r, c = w.shape
    tr = _row_tile(r, L * c * 4, 1024 * 1024)

    def body(*refs):
        w_ref, m_ref, v_ref = refs[:3]
        g_refs = refs[3:3 + L]
        go_ref, d_ref, nm_ref, nv_ref = refs[3 + L:]
        for l in range(L):
            g = g_refs[l][...]
            go_ref[l] = g
            d_ref[l], nm_ref[l], nv_ref[l] = _adamw_math(w_ref[l], g, m_ref[l], v_ref[l])

    stacked = pl.BlockSpec((L, tr, c), lambda i: (0, i, 0))
    return pl.pallas_call(
        body, name=name, grid=(r // tr,),
        in_specs=[stacked] * 3 + [pl.BlockSpec((tr, c), lambda i: (i, 0))] * L, out_specs=[stacked] * 4,
        out_shape=[jax.ShapeDtypeStruct((L, r, c), F32)] * 4, compiler_params=_params(1))(w, m, v, *grads)


def _place():
    x, y, c = lax.axis_index("x"), lax.axis_index("y"), lax.axis_index("c")
    chips = [(1 - x, y), (x, 1 - y), (1 - x, 1 - y)]
    return x, y, c, chips


HBM = pl.BlockSpec(memory_space=pltpu.HBM)
SEM = pl.BlockSpec(memory_space=pltpu.SEMAPHORE)
EFFECT = pltpu.SideEffectType.DATAFLOW_SIDE_EFFECTING


class _Copy:
    def __init__(self, src, src_view, land, dst_view, recv_view, target):
        self.src, self.src_view, self.land, self.dst_view, self.recv_view, self.target = (
            src, src_view, land, dst_view, recv_view, target)


def _whole(ref, place):
    return ref


def _split_start(name, srcs, land_shapes, plans):
    skeys, lkeys = list(srcs), list(land_shapes)
    ns, nl, ng = len(skeys), len(lkeys), len(plans)

    def body(*refs):
        src = dict(zip(skeys, refs[:ns]))
        land = dict(zip(lkeys, refs[ns:ns + nl]))
        sems = refs[ns + nl:ns + nl + 2 * ng]
        token = refs[-1]
        place = _place()
        for gi, plan in enumerate(plans):
            for k, cp in enumerate(plan):
                dst = land[cp.land] if cp.land in land else src[cp.land]
                pltpu.make_async_remote_copy(
                    src_ref=cp.src_view(src[cp.src], place), dst_ref=cp.dst_view(dst, place),
                    send_sem=sems[2 * gi].at[k], recv_sem=sems[2 * gi + 1].at[k],
                    device_id=cp.target(place), device_id_type=MESH).start()
        token[...] = jnp.zeros_like(token)

    sem_shapes = []
    for plan in plans:
        sem_shapes += [pltpu.SemaphoreType.DMA((len(plan),))] * 2
    buffers = [srcs[k] for k in skeys] + [lax.empty(land_shapes[k].shape, land_shapes[k].dtype) for k in lkeys]
    outs = pl.pallas_call(
        body, name=name,
        out_shape=(*sem_shapes, *[pltpu.HBM(a.shape, a.dtype) for a in buffers], jax.ShapeDtypeStruct((8, LANES), F32)),
        in_specs=[HBM] * (ns + nl),
        out_specs=(*[SEM] * (2 * ng), *[HBM] * (ns + nl), pl.BlockSpec(memory_space=pltpu.VMEM)),
        input_output_aliases={i: 2 * ng + i for i in range(ns + nl)},
        compiler_params=pltpu.CompilerParams(has_side_effects=EFFECT),
    )(*[pltpu.with_memory_space_constraint(a, pltpu.HBM) for a in buffers])
    sems = [(outs[2 * gi], outs[2 * gi + 1]) for gi in range(ng)]
    thru = outs[2 * ng:2 * ng + ns + nl]
    return sems, dict(zip(skeys, thru[:ns])), dict(zip(lkeys, thru[ns:])), outs[-1]


def _split_wait(name, sems, srcs, lands, plan, after):
    skeys, lkeys = list(srcs), list(lands)
    ns, nl = len(skeys), len(lkeys)

    def body(*refs):
        src = dict(zip(skeys, refs[:ns]))
        land = dict(zip(lkeys, refs[ns:ns + nl]))
        ssem, rsem = refs[ns + nl], refs[ns + nl + 1]
        place = _place()
        for k, cp in enumerate(plan):
            dst = land[cp.land] if cp.land in land else src[cp.land]
            pltpu.make_async_remote_copy(
                src_ref=cp.src_view(src[cp.src], place), dst_ref=cp.dst_view(dst, place),
                send_sem=ssem.at[k], recv_sem=rsem.at[k],
                device_id=cp.target(place), device_id_type=MESH).wait_send()
            got = cp.recv_view(dst, place)
            pltpu.make_async_remote_copy(
                src_ref=got, dst_ref=got, send_sem=ssem.at[k], recv_sem=rsem.at[k],
                device_id=cp.target(place), device_id_type=MESH).wait_recv()

    buffers = [srcs[k] for k in skeys] + [lands[k] for k in lkeys]
    outs = pl.pallas_call(
        body, name=name, out_shape=tuple(pltpu.HBM(a.shape, a.dtype) for a in buffers),
        in_specs=(*[HBM] * (ns + nl), SEM, SEM, ANY), out_specs=tuple([HBM] * (ns + nl)),
        input_output_aliases={i: i for i in range(ns + nl)},
        compiler_params=pltpu.CompilerParams(has_side_effects=EFFECT),
    )(*buffers, sems[0], sems[1], after)
    return dict(zip(skeys, outs[:ns])), dict(zip(lkeys, outs[ns:]))


def _chip_of(place):
    x, y, c, chips = place
    return 2 * x + y


GATHER_FIRST = 2


class _WeightGather:
    def __init__(self, blocks):
        self.plans, shapes = {}, {}
        for key, a in blocks.items():
            shapes[key] = jax.ShapeDtypeStruct((N_CHIPS,) + a.shape, a.dtype)
            slot = lambda ref, place: ref.at[_chip_of(place)]
            plan = [_Copy(key, _whole, key, slot,
                          lambda ref, place, k=k: ref.at[2 * place[3][k][0] + place[3][k][1]],
                          lambda place, k=k: (place[3][k][0], place[3][k][1], place[2])) for k in range(3)]
            plan.append(_Copy(key, _whole, key, slot, slot, lambda place: (place[0], place[1], 1 - place[2])))
            self.plans[key] = plan
        keys = list(blocks)
        self.sems, self.srcs, self.lands = {}, {}, {}
        for name, part in (("gather_start_first", keys[:GATHER_FIRST]), ("gather_start", keys[GATHER_FIRST:])):
            sems, srcs, lands, self.token = _split_start(name, {k: blocks[k] for k in part}, {k: shapes[k] for k in part},
                                                         [self.plans[k] for k in part])
            self.sems.update(zip(part, sems))
            self.srcs.update(srcs)
            self.lands.update(lands)

    def get(self, l, name, after):
        key = (l, name)
        _, lands = _split_wait(f"gather_wait_{name}{l}", self.sems[key], {key: self.srcs[key]},
                               {key: self.lands[key]}, self.plans[key], after)
        return lands[key][:, None]


class _GradReduce:
    def __init__(self, place):
        self.place = place
        self.jobs = []
        self.done = {}
        self.n = 0

    def submit(self, grads):
        views = {k: a.reshape(N_CHIPS, 2, a.shape[1] // 2, a.shape[2]) for k, a in grads.items()}
        shapes = {k: jax.ShapeDtypeStruct((N_DEVICES,) + a.shape[2:], a.dtype) for k, a in views.items()}

        def peer(place, k):
            x, y, c, _ = place
            return (1 - x if k & 4 else x, 1 - y if k & 2 else y, 1 - c if k & 1 else c)

        def index(dev):
            return 4 * dev[0] + 2 * dev[1] + dev[2]

        plan = []
        for key in views:
            for k in range(1, N_DEVICES):
                plan.append(_Copy(
                    key, lambda ref, place, k=k: ref.at[2 * peer(place, k)[0] + peer(place, k)[1], peer(place, k)[2]],
                    key, lambda ref, place: ref.at[index(place[:3])],
                    lambda ref, place, k=k: ref.at[index(peer(place, k))],
                    lambda place, k=k: peer(place, k)))
        sems, srcs, lands, token = _split_start(f"grad_start{self.n}", views, shapes, [plan])
        self.jobs.append(dict(id=self.n, sems=sems[0], srcs=srcs, lands=lands, plan=plan))
        self.n += 1
        return token

    def pump(self, after):
        return []

    def finish(self, after):
        for job in self.jobs:
            srcs, lands = _split_wait(f"grad_wait{job['id']}", job["sems"], job["srcs"], job["lands"], job["plan"],
                                      after)
            for i, k in enumerate(srcs):
                self.done[k] = _device_add(f"grad_add{job['id']}_{i}", srcs[k], lands[k], self.place)
        self.jobs = []
        return self.done


class _PairShare:
    def __init__(self, halves, types):
        sibling = lambda place: (place[0], place[1], 1 - place[2])
        mine = lambda ref, place: ref.at[place[2]]
        theirs = lambda ref, place: ref.at[1 - place[2]]
        self.plans = {t: [_Copy(k, mine, k, mine, theirs, sibling) for k in halves if k[0] == t] for t in types}
        sems, self.bufs, _, self.token = _split_start("share_start", halves, {}, list(self.plans.values()))
        self.sems = dict(zip(self.plans, sems))

    def get(self, t, after):
        keys = [cp.src for cp in self.plans[t]]
        bufs, _ = _split_wait(f"share_wait_{t}", self.sems[t], {k: self.bufs[k] for k in keys}, {}, self.plans[t], after)
        return bufs


def _small_allreduce(part):
    R, C = part.shape
    N_DEV = 8

    def body(in_ref, out_ref, slots, ssem, rsem):
        x, y, c, _ = _place()
        me = 4 * x + 2 * y + c
        sends = []
        for k in range(1, N_DEV):
            kx, ky, kc = (k >> 2) & 1, (k >> 1) & 1, k & 1
            peer = (1 - x if kx else x, 1 - y if ky else y, 1 - c if kc else c)
            cp = pltpu.make_async_remote_copy(
                src_ref=in_ref, dst_ref=slots.at[me], send_sem=ssem.at[k], recv_sem=rsem.at[k],
                device_id=peer, device_id_type=MESH)
            cp.start()
            sends.append(cp)
        slots[me] = in_ref[...]
        for k in range(1, N_DEV):
            kx, ky, kc = (k >> 2) & 1, (k >> 1) & 1, k & 1
            peer = (1 - x if kx else x, 1 - y if ky else y, 1 - c if kc else c)
            slot = slots.at[4 * peer[0] + 2 * peer[1] + peer[2]]
            pltpu.make_async_remote_copy(
                src_ref=slot, dst_ref=slot, send_sem=ssem.at[k], recv_sem=rsem.at[k],
                device_id=peer, device_id_type=MESH).wait_recv()
        acc = slots[0]
        for d in range(1, N_DEV):
            acc = acc + slots[d]
        out_ref[...] = acc
        for cp in sends:
            cp.wait_send()

    vm = pl.BlockSpec(memory_space=pltpu.VMEM)
    return pl.pallas_call(
        body, name="small_allreduce", in_specs=[vm], out_specs=vm,
        out_shape=jax.ShapeDtypeStruct((R, C), F32),
        scratch_shapes=[pltpu.VMEM((N_DEV, R, C), F32), pltpu.SemaphoreType.DMA((N_DEV,)),
                        pltpu.SemaphoreType.DMA((N_DEV,))])(part)


def _local_step(x, target, norm_mix, norm_mlp, norm_kv, norm_final, weights, sink, n_a, n_heads):
    B, S, D = x.shape
    T = B * S
    C = n_heads * HEAD_DIM
    depth = norm_mix.shape[0]
    slopes = 2.0 ** (-ALIBI_MAX_BIAS * jnp.arange(1, n_heads + 1, dtype=F32) / n_heads)
    tm = min(512, T)
    row = lambda v: v.reshape(1, -1)

    h = x.reshape(T, D)
    saved, Wl = [], []
    kv = nkv = h_kv = cwg = None
    for l in range(depth):
        s = {"h_in": h}
        w = {}
        Wl.append(w)
        if l < n_a:
            w["w_a_in"] = weights.get(l, "w_a_in", h)
            first = [weights.token] if l == 0 and hasattr(weights, "token") else []
            s["n1"], bcu = _norm_mm(f"a_in_fwd{l}", h, row(norm_mix[l]), w["w_a_in"], 0, 3, BF16, tm, first)
            s["bcu"] = bcu.reshape(3, B, S, D)
            if l == 0:
                cwg = weights.get(0, "conv", bcu)[:, 0, :n_a * 3].reshape(N_CHIPS, n_a, 3, -1)
            s["z"] = _conv_fwd(f"conv_fwd{l}", s["bcu"], cwg, l, LANES).reshape(T, D)
            w["w_a_out"] = weights.get(l, "w_a_out", s["z"])
            h = _mm_res_rows(f"a_out_fwd{l}", s["z"], w["w_a_out"], 0, h, _to_bf16, tm)
        else:
            i = l - n_a
            if i == 0:
                h_kv = h
                w["w_kv"] = weights.get(l, "w_kv", h)
                nkv, kv = _norm_mm("kv_fwd", h, row(norm_kv), w["w_kv"], 0, 1, F32, tm)
                kv = kv.reshape(B, S, 2 * 3 * C)
            w["w_q"] = weights.get(l, "w_q", h)
            s["n1"], q = _norm_mm(f"q_fwd{i}", h, row(norm_mix[l]), w["w_q"], 0, 1, F32, tm)
            s["q"] = q.reshape(B, S, 3 * C)
            o, lse = _attn_fwd(f"attn_fwd{i}", s["q"], kv, slopes, n_heads)
            s["o"], s["lse"] = o.reshape(T, C), lse.reshape(T, C)
            w["w_o"] = weights.get(l, "w_o", o)
            h = _mm_res_cols(f"o_fwd{i}", s["o"], w["w_o"], 0, h, tm)
        s["h_mid"] = h
        w["w_up"] = weights.get(l, "w_up", h)
        w["w_down"] = weights.get(l, "w_down", h)
        s["n2"], s["a"], h = _mlp_fwd(f"mlp_fwd{l}", h, row(norm_mlp[l]), w["w_up"], w["w_down"], tm)
        F = s["a"].shape[1]
        saved.append(s)

    loss, dh, dh16, dg_final = _final_loss("loss_head", h, row(norm_final), target.reshape(T, D), tm)

    g_mix, g_mlp = [None] * depth, [None] * depth
    g_conv = [None] * n_a
    dkv = None
    tt = min(512, T)
    deps = []
    for l in reversed(range(depth)):
        s, w = saved[l], Wl[l]
        g_down = _tn(f"down_wgrad{l}", s["a"], _relu2_bf16, [_seg2d(dh16, tt, 2)], None, False,
                     min(2048, F), tt, deps, BF16).reshape(N_CHIPS, F // N_CHIPS, D)
        da, dh, dh16, g_mlp[l] = _mlp_bwd(f"mlp_bwd{l}", dh, dh16, s["a"], w["w_down"], w["w_up"], s["h_mid"],
                                          row(norm_mlp[l]), tm)
        g_up = _tn(f"up_wgrad{l}", s["n2"], _to_bf16, [_seg2d(da, tt, 2)], F // N_CHIPS, True, D, tt, (), BF16)
        deps = sink.pump(dh) + [sink.submit({("w_up", l): g_up, ("w_down", l): g_down})]
        if l < n_a:
            g_out = _tn(f"a_out_wgrad{l}", s["z"], _to_bf16, [_seg2d(dh16, tt, 2)], None, False,
                        D, tt, deps, BF16).reshape(N_CHIPS, D // N_CHIPS, D)
            dz = _nt_rows(f"a_out_bwd{l}", dh16, w["w_a_out"], 0, None, F32, tm)
            deps = sink.pump(dz) + [sink.submit({("w_a_out", l): g_out})]
            dbcu, g_conv[l] = _conv_bwd(f"conv_bwd{l}", s["bcu"], dz.reshape(B, S, D), cwg, l, LANES)
            dbcu = dbcu.reshape(3, T, D)
            g_in = _tn(f"a_in_wgrad{l}", s["n1"], _to_bf16, [_seg_plane(dbcu, p, tt, 2) for p in range(3)],
                       3 * D // N_CHIPS, True, D, tt, deps, BF16)
            dh, dh16, g_mix[l] = _nt_cols(f"a_in_bwd{l}", [_seg_plane(dbcu, p, tm, 1) for p in range(3)],
                                          w["w_a_in"], 0, tm, (s["h_in"], row(norm_mix[l]), dh))
            mixer = {("w_a_in", l): g_in}
        else:
            i = l - n_a
            g_o = _tn(f"o_wgrad{i}", s["o"], _to_bf16, [_seg2d(dh16, tt, 2)], D // N_CHIPS, True, C, tt, deps,
                      BF16)
            do = _nt_cols(f"o_bwd{i}", [_seg2d(dh16, tm, 1)], w["w_o"], 0, tm, None)
            deps = sink.pump(do) + [sink.submit({("w_o", i): g_o})]
            dq, dk, dv = _attn_bwd(f"attn_bwd{i}", s["q"], kv, slopes, s["o"].reshape(B, S, C),
                                   s["lse"].reshape(B, S, C), do.reshape(B, S, C), n_heads, dkv)
            dkv = (dk, dv)
            dq = dq.reshape(T, 3 * C)
            g_q = _tn(f"q_wgrad{i}", s["n1"], _to_bf16, [_seg2d(dq, tt, 2)], 3 * C // N_CHIPS, True, D, tt, deps,
                      BF16)
            dh, dh16, g_mix[l] = _nt_cols(f"q_bwd{i}", [_seg2d(dq, tm, 1)], w["w_q"], 0, tm,
                                          (s["h_in"], row(norm_mix[l]), dh))
            mixer = {("w_q", i): g_q}
            if i == 0:
                dk2, dv2 = (t.reshape(T, 3 * C) for t in dkv)
                mixer[("w_kv", 0)] = _tn("kv_wgrad", nkv, _to_bf16, _kv_segments(dk2, dv2, C, tt, 2),
                                         6 * C // N_CHIPS, True, D, tt, (), BF16)
                dh, dh16, g_kv = _nt_cols("kv_bwd", _kv_segments(dk2, dv2, C, tm, 1), w["w_kv"], 0, tm,
                                          (h_kv, row(norm_kv), dh))
        deps = sink.pump(dh) + [sink.submit(mixer)]
    small = dict(norm_mix=jnp.concatenate(g_mix, axis=0), norm_mlp=jnp.concatenate(g_mlp, axis=0),
                 norm_kv=g_kv, norm_final=dg_final, conv_w=jnp.stack(g_conv))
    return loss, dh.reshape(B, S, D), small


BIG = ("w_a_in", "w_a_out", "w_kv", "w_q", "w_o", "w_up", "w_down")
CONV_PAD_ROWS = 16


def kernel(x, norm_mix, norm_mlp, w_a_in, conv_w, w_a_out, norm_kv, w_kv, w_q, w_o, w_up, w_down, norm_final, loss_target, m_norm_mix, m_norm_mlp, m_w_a_in, m_conv_w, m_w_a_out, m_norm_kv, m_w_kv, m_w_q, m_w_o, m_w_up, m_w_down, m_norm_final, v_norm_mix, v_norm_mlp, v_w_a_in, v_conv_w, v_w_a_out, v_norm_kv, v_w_kv, v_w_q, v_w_o, v_w_up, v_w_down, v_norm_final):
    D = x.shape[-1]
    w = dict(norm_mix=norm_mix, norm_mlp=norm_mlp, w_a_in=w_a_in, conv_w=conv_w, w_a_out=w_a_out, norm_kv=norm_kv,
             w_kv=w_kv[None], w_q=w_q, w_o=w_o, w_up=w_up, w_down=w_down, norm_final=norm_final)
    m = dict(norm_mix=m_norm_mix, norm_mlp=m_norm_mlp, w_a_in=m_w_a_in, conv_w=m_conv_w, w_a_out=m_w_a_out,
             norm_kv=m_norm_kv, w_kv=m_w_kv[None], w_q=m_w_q, w_o=m_w_o, w_up=m_w_up, w_down=m_w_down,
             norm_final=m_norm_final)
    v = dict(norm_mix=v_norm_mix, norm_mlp=v_norm_mlp, w_a_in=v_w_a_in, conv_w=v_conv_w, w_a_out=v_w_a_out,
             norm_kv=v_norm_kv, w_kv=v_w_kv[None], w_q=v_w_q, w_o=v_w_o, w_up=v_w_up, w_down=v_w_down,
             norm_final=v_norm_final)
    depth = norm_mix.shape[0]
    n_a, taps, cwc = conv_w.shape
    n_heads = w_o.shape[1] // HEAD_DIM

    conv_rows = jnp.zeros((CONV_PAD_ROWS, cwc), F32).at[:n_a * taps].set(conv_w.reshape(n_a * taps, cwc))
    blocks = {}
    for l in range(depth):
        if l < n_a:
            blocks[(l, "w_a_in")] = w_a_in[l].astype(BF16)
            if l == 0:
                blocks[(0, "conv")] = conv_rows
            blocks[(l, "w_a_out")] = w_a_out[l].astype(BF16)
        else:
            if l == n_a:
                blocks[(l, "w_kv")] = w_kv.astype(BF16)
            blocks[(l, "w_q")] = w_q[l - n_a].astype(BF16)
            blocks[(l, "w_o")] = w_o[l - n_a].astype(BF16)
        blocks[(l, "w_up")] = w_up[l].astype(BF16)
        blocks[(l, "w_down")] = w_down[l].astype(BF16)
    weights = _WeightGather(blocks)
    place = jnp.stack([2 * lax.axis_index("x") + lax.axis_index("y"), lax.axis_index("c")]).astype(jnp.int32)
    sink = _GradReduce(place)

    loss, grad_x, small = _local_step(x, loss_target, norm_mix, norm_mlp, norm_kv, norm_final, weights, sink,
                                      n_a, n_heads)
    loss = lax.psum(loss[0, 0], ("x", "y", "c"))

    share = _PairShare(sink.finish(grad_x), BIG)
    grads = {}

    packed = jnp.concatenate([small["norm_mix"], small["norm_mlp"], small["norm_kv"], small["norm_final"],
                              small["conv_w"].reshape(n_a * taps, D)], axis=0)
    pad = (-packed.shape[0]) % 8
    packed = jnp.pad(packed, ((0, pad), (0, 0)))
    total = _small_allreduce(packed)
    grads["norm_mix"] = total[:depth]
    grads["norm_mlp"] = total[depth:2 * depth]
    grads["norm_kv"] = total[2 * depth]
    grads["norm_final"] = total[2 * depth + 1]
    chip = 2 * lax.axis_index("x") + lax.axis_index("y")
    conv_full = total[2 * depth + 2:2 * depth + 2 + n_a * taps].reshape(n_a, taps, N_CHIPS, cwc)
    grads["conv_w"] = lax.dynamic_index_in_dim(conv_full, chip, axis=2, keepdims=False)

    order = ("norm_mix", "norm_mlp", "w_a_in", "conv_w", "w_a_out", "norm_kv", "w_kv", "w_q", "w_o", "w_up",
             "w_down", "norm_final")
    delta, new_m, new_v = {}, {}, {}
    vec_names = ("norm_mix", "norm_mlp", "norm_kv", "norm_final")
    rows_of = lambda a: a.reshape(-1, D)
    vw, vg, vm_, vv = (jnp.concatenate([rows_of(t[k]) for k in vec_names], axis=0) for t in (w, grads, m, v))
    vpad = (-vw.shape[0]) % 8
    padrows = lambda a: jnp.pad(a, ((0, vpad), (0, 0)))
    vd, vnm, vnv = _adamw("adamw_norms", padrows(vw), padrows(vg), padrows(vm_), padrows(vv))
    off = 0
    for k in vec_names:
        r = rows_of(w[k]).shape[0]
        delta[k] = vd[off:off + r].reshape(w[k].shape)
        new_m[k] = vnm[off:off + r].reshape(w[k].shape)
        new_v[k] = vnv[off:off + r].reshape(w[k].shape)
        off += r
    cpad = (-n_a * taps) % 8
    two_d = lambda a: jnp.pad(a.reshape(-1, cwc), ((0, cpad), (0, 0)))
    cd, cnm, cnv = _adamw("adamw_conv_w", two_d(w["conv_w"]), two_d(grads["conv_w"]), two_d(m["conv_w"]),
                          two_d(v["conv_w"]))
    delta["conv_w"], new_m["conv_w"], new_v["conv_w"] = (t[:n_a * taps].reshape(conv_w.shape) for t in (cd, cnm, cnv))
    after = cd
    for k in sorted(BIG, key=lambda k: w[k].size):
        shared = share.get(k, after)
        per_layer = [shared[(k, l)].reshape(w[k].shape[1:]) for l in range(w[k].shape[0])]
        grads[k], delta[k], new_m[k], new_v[k] = _adamw_layers(f"adamw_{k}", w[k], per_layer, m[k], v[k])
        after = delta[k]
    fix = lambda k, a: a[0] if k == "w_kv" else a
    return (loss, grad_x, *[fix(k, grads[k]) for k in order], *[fix(k, delta[k]) for k in order],
            *[fix(k, new_m[k]) for k in order], *[fix(k, new_v[k]) for k in order])
```

```python
import functools

import jax
import jax.numpy as jnp
from jax import lax
from jax.experimental import pallas as pl
from jax.experimental.pallas import tpu as pltpu

F32 = jnp.float32
BF16 = jnp.bfloat16
MESH = pl.DeviceIdType.MESH

EPS = 1e-5
PATTERNS = ((128, 1), (512, 4), (2048, 16))
HEAD_DIM = 64
ALIBI_MAX_BIAS = 8.0
NEG_INF = -1e30
ATT_BLK = 128
BWD_UNROLL = 16
N_CHIPS = 4
LANES = 128
VMEM_LIMIT = 56 * 1024 * 1024

ADAM_LR = 0.001
ADAM_B1 = 0.9
ADAM_B2 = 0.999
ADAM_EPS = 1e-08
ADAM_WD = 0.01
ADAM_STEP = 10


ANY = pl.BlockSpec(memory_space=pl.ANY)


def _params(n_grid_axes):
    return pltpu.CompilerParams(dimension_semantics=("arbitrary",) * n_grid_axes, vmem_limit_bytes=VMEM_LIMIT)


def _dot(a, b):
    return jnp.dot(a, b, preferred_element_type=F32)


def _dot_nt(a, b):
    return lax.dot_general(a, b, (((1,), (1,)), ((), ())), preferred_element_type=F32)


def _dot_tn(a, b):
    return lax.dot_general(a, b, (((0,), (0,)), ((), ())), preferred_element_type=F32)


def _relu2(a):
    return jnp.square(jnp.maximum(a, 0.0))


def _rms(hf, g):
    y = hf * lax.rsqrt(jnp.mean(hf * hf, axis=-1, keepdims=True) + EPS)
    return y * g


def _rms_bwd(hf, g, dn):
    rstd = lax.rsqrt(jnp.mean(hf * hf, axis=-1, keepdims=True) + EPS)
    xhat = hf * rstd
    dg = jnp.sum(dn * xhat, axis=0, keepdims=True)
    dx = dn * g
    dh = rstd * (dx - xhat * jnp.mean(dx * xhat, axis=-1, keepdims=True))
    return dh, dg


def _pieces(seg_widths, chunk_width, max_width):
    total = sum(seg_widths)
    cuts = {0, total}
    acc = 0
    for w in seg_widths:
        cuts.add(acc)
        acc += w
    cuts.update(range(0, total, chunk_width))
    cuts = sorted(cuts)
    fine = []
    for lo, hi in zip(cuts[:-1], cuts[1:]):
        while hi - lo > max_width:
            fine.append((lo, lo + max_width))
            lo += max_width
        fine.append((lo, hi))
    out = []
    for lo, hi in fine:
        acc = 0
        for s, w in enumerate(seg_widths):
            if lo < acc + w:
                break
            acc += w
        out.append((s, lo - acc, lo // chunk_width, lo % chunk_width, hi - lo))
    return out


def _relu2_bf16(a):
    return _relu2(a.astype(F32)).astype(BF16)


def _to_bf16(a):
    return a.astype(BF16)


def _norm_mm(name, h, g, wg, layer, planes, out_dtype, tm, deps=()):
    T, D = h.shape
    cw = wg.shape[3]
    N = N_CHIPS * cw
    pw = N // planes
    pieces = _pieces([pw] * planes, cw, 512)

    def body(h_ref, g_ref, w_ref, *rest):
        n_ref, o_ref = rest[len(deps):]
        n = _rms(h_ref[...], g_ref[...]).astype(BF16)
        n_ref[...] = n
        for s, a0, ch, b0, wd in pieces:
            o_ref[s, :, a0:a0 + wd] = _dot(n, w_ref[ch, :, b0:b0 + wd]).astype(out_dtype)

    return pl.pallas_call(
        body, name=name, grid=(T // tm,),
        in_specs=[pl.BlockSpec((tm, D), lambda i: (i, 0)),
                  pl.BlockSpec((1, D), lambda i: (0, 0)),
                  pl.BlockSpec((N_CHIPS, None, D, cw), lambda i: (0, layer, 0, 0))] + [ANY] * len(deps),
        out_specs=[pl.BlockSpec((tm, D), lambda i: (i, 0)),
                   pl.BlockSpec((planes, tm, pw), lambda i: (0, i, 0))],
        out_shape=[jax.ShapeDtypeStruct((T, D), BF16), jax.ShapeDtypeStruct((planes, T, pw), out_dtype)],
        compiler_params=_params(1))(h, g, wg, *deps)


def _resident(shape, index_map):
    return pl.BlockSpec(shape, index_map, pipeline_mode=pl.Buffered(1))


def _mm_res_rows(name, a, wg, layer, h, act, tm):
    T = a.shape[0]
    rk, D = wg.shape[2], wg.shape[3]

    def body(a_ref, w_ref, h_ref, o_ref):
        acc = h_ref[...]
        for k in range(N_CHIPS):
            acc = acc + _dot(act(a_ref[:, k * rk:(k + 1) * rk]), w_ref[k])
        o_ref[...] = acc

    return pl.pallas_call(
        body, name=name, grid=(T // tm,),
        in_specs=[pl.BlockSpec((tm, N_CHIPS * rk), lambda i: (i, 0)),
                  pl.BlockSpec((N_CHIPS, None, rk, D), lambda i: (0, layer, 0, 0)),
                  pl.BlockSpec((tm, D), lambda i: (i, 0))],
        out_specs=pl.BlockSpec((tm, D), lambda i: (i, 0)),
        out_shape=jax.ShapeDtypeStruct((T, D), F32),
        compiler_params=_params(1))(a, wg, h)


def _mm_res_cols(name, a, wg, layer, h, tm):
    T, K = a.shape
    cw = wg.shape[3]
    D = N_CHIPS * cw

    def body(a_ref, w_ref, h_ref, o_ref):
        a16 = a_ref[...].astype(BF16)
        for j in range(N_CHIPS):
            o_ref[:, j * cw:(j + 1) * cw] = h_ref[:, j * cw:(j + 1) * cw] + _dot(a16, w_ref[j])

    return pl.pallas_call(
        body, name=name, grid=(T // tm,),
        in_specs=[pl.BlockSpec((tm, K), lambda i: (i, 0)),
                  pl.BlockSpec((N_CHIPS, None, K, cw), lambda i: (0, layer, 0, 0)),
                  pl.BlockSpec((tm, D), lambda i: (i, 0))],
        out_specs=pl.BlockSpec((tm, D), lambda i: (i, 0)),
        out_shape=jax.ShapeDtypeStruct((T, D), F32),
        compiler_params=_params(1))(a, wg, h)


def _mlp_fwd(name, h, g, wup, wdown, tm):
    T, D = h.shape
    cw = wup.shape[3]

    def body(h_ref, g_ref, wu_ref, wd_ref, n_ref, a_ref, o_ref):
        hf = h_ref[...]
        n = _rms(hf, g_ref[...]).astype(BF16)
        n_ref[...] = n
        acc = hf
        for ch in range(N_CHIPS):
            a16 = _dot(n, wu_ref[ch]).astype(BF16)
            a_ref[:, ch * cw:(ch + 1) * cw] = a16
            acc = acc + _dot(_relu2_bf16(a16), wd_ref[ch])
        o_ref[...] = acc

    row = pl.BlockSpec((tm, D), lambda i: (i, 0))
    return pl.pallas_call(
        body, name=name, grid=(T // tm,),
        in_specs=[row, pl.BlockSpec((1, D), lambda i: (0, 0)),
                  _resident((N_CHIPS, None, D, cw), lambda i: (0, 0, 0, 0)),
                  _resident((N_CHIPS, None, cw, D), lambda i: (0, 0, 0, 0))],
        out_specs=[row, pl.BlockSpec((tm, N_CHIPS * cw), lambda i: (i, 0)), row],
        out_shape=[jax.ShapeDtypeStruct((T, D), BF16), jax.ShapeDtypeStruct((T, N_CHIPS * cw), BF16),
                   jax.ShapeDtypeStruct((T, D), F32)],
        compiler_params=_params(1))(h, g, wup, wdown)


def _mlp_bwd(name, dh, dh16, a, wdown, wup, h_mid, g, tm, deps=()):
    T, D = dh.shape
    cw = wup.shape[3]
    F = N_CHIPS * cw

    def body(dh_ref, dh16_ref, a_ref, wd_ref, wu_ref, h_ref, g_ref, *rest):
        da_ref, out_ref, out16_ref, dg_ref = rest[len(deps):]
        d16 = dh16_ref[...]
        acc = None
        for ch in range(N_CHIPS):
            cols = slice(ch * cw, (ch + 1) * cw)
            da = (_dot_nt(d16, wd_ref[ch]) * (2.0 * jnp.maximum(a_ref[:, cols].astype(F32), 0.0))).astype(BF16)
            da_ref[:, cols] = da
            d = _dot_nt(da, wu_ref[ch])
            acc = d if acc is None else acc + d
        dh_c, dg = _rms_bwd(h_ref[...], g_ref[...], acc)
        out = dh_ref[...] + dh_c
        out_ref[...] = out
        out16_ref[...] = out.astype(BF16)

        @pl.when(pl.program_id(0) == 0)
        def _():
            dg_ref[...] = dg

        @pl.when(pl.program_id(0) > 0)
        def _():
            dg_ref[...] += dg

    row = pl.BlockSpec((tm, D), lambda i: (i, 0))
    wide = pl.BlockSpec((tm, F), lambda i: (i, 0))
    vec = pl.BlockSpec((1, D), lambda i: (0, 0))
    return pl.pallas_call(
        body, name=name, grid=(T // tm,),
        in_specs=[row, row, wide, _resident((N_CHIPS, None, cw, D), lambda i: (0, 0, 0, 0)),
                  _resident((N_CHIPS, None, D, cw), lambda i: (0, 0, 0, 0)), row, vec] + [ANY] * len(deps),
        out_specs=[wide, row, row, vec],
        out_shape=[jax.ShapeDtypeStruct((T, F), BF16), jax.ShapeDtypeStruct((T, D), F32),
                   jax.ShapeDtypeStruct((T, D), BF16), jax.ShapeDtypeStruct((1, D), F32)],
        compiler_params=_params(1))(dh, dh16, a, wdown, wup, h_mid, g, *deps)


CONV_ROWS = 256
CONV_HALO = 16


def _conv_shifted(ext, k, r0, rows):
    rolled = pltpu.roll(ext, k, 0)[CONV_HALO:]
    t = r0 + lax.broadcasted_iota(jnp.int32, rolled.shape, 0)
    return jnp.where(t >= k, rolled, 0.0)


def _conv_ahead(ext, k, r0, rows, S):
    rolled = pltpu.roll(ext, rows + CONV_HALO - k, 0)[:rows]
    t = r0 + lax.broadcasted_iota(jnp.int32, rolled.shape, 0)
    return jnp.where(t + k < S, rolled, 0.0)


def _conv_fwd(name, bcu, cwg, layer, tc):
    _, B, S, D = bcu.shape
    cwc = cwg.shape[3]
    per_chunk = cwc // tc
    R = min(CONV_ROWS, S)

    def body(x_ref, w_ref, z_ref):
        w = [w_ref[k:k + 1, :] for k in range(3)]

        def step(i, carry):
            r0 = pl.multiple_of(i * R, R)
            h0 = pl.multiple_of(jnp.maximum(r0 - CONV_HALO, 0), CONV_HALO)
            ld = lambda p, start, rows: x_ref[p, pl.ds(start, rows), :].astype(F32)
            cu = jnp.concatenate([ld(1, h0, CONV_HALO) * ld(2, h0, CONV_HALO), ld(1, r0, R) * ld(2, r0, R)], axis=0)
            conv = w[0] * cu[CONV_HALO:]
            conv = conv + w[1] * _conv_shifted(cu, 1, r0, R)
            conv = conv + w[2] * _conv_shifted(cu, 2, r0, R)
            z_ref[pl.ds(r0, R), :] = (ld(0, r0, R) * conv).astype(BF16)
            return carry

        lax.fori_loop(0, S // R, step, 0)

    return pl.pallas_call(
        body, name=name, grid=(B, D // tc),
        in_specs=[pl.BlockSpec((3, None, S, tc), lambda b, j: (0, b, 0, j)),
                  pl.BlockSpec((None, None, 3, tc), lambda b, j: (j // per_chunk, layer, 0, j % per_chunk))],
        out_specs=pl.BlockSpec((None, S, tc), lambda b, j: (b, 0, j)),
        out_shape=jax.ShapeDtypeStruct((B, S, D), BF16),
        compiler_params=_params(2))(bcu, cwg)


def _conv_bwd(name, bcu, dz, cwg, layer, tc):
    _, B, S, D = bcu.shape
    cwc = cwg.shape[3]
    per_chunk = cwc // tc
    R = min(CONV_ROWS, S)

    def body(x_ref, dz_ref, w_ref, d_ref, dw_ref):
        w = [w_ref[k:k + 1, :] for k in range(3)]

        @pl.when(pl.program_id(1) == 0)
        def _():
            dw_ref[...] = jnp.zeros_like(dw_ref)

        def step(i, carry):
            r0 = pl.multiple_of(i * R, R)
            h0 = pl.multiple_of(jnp.maximum(r0 - CONV_HALO, 0), CONV_HALO)
            a0 = pl.multiple_of(jnp.minimum(r0 + R, S - CONV_HALO), CONV_HALO)
            ld = lambda p, start, rows: x_ref[p, pl.ds(start, rows), :].astype(F32)
            b, c, u = ld(0, r0, R), ld(1, r0, R), ld(2, r0, R)
            dz = dz_ref[pl.ds(r0, R), :]
            cu = jnp.concatenate([ld(1, h0, CONV_HALO) * ld(2, h0, CONV_HALO), c * u], axis=0)
            cu1 = _conv_shifted(cu, 1, r0, R)
            cu2 = _conv_shifted(cu, 2, r0, R)
            conv = w[0] * (c * u) + w[1] * cu1 + w[2] * cu2
            dconv = dz * b
            dca = jnp.concatenate([dconv, dz_ref[pl.ds(a0, CONV_HALO), :] * ld(0, a0, CONV_HALO)], axis=0)
            dcu = w[0] * dconv + w[1] * _conv_ahead(dca, 1, r0, R, S) + w[2] * _conv_ahead(dca, 2, r0, R, S)
            d_ref[0, pl.ds(r0, R), :] = (dz * conv).astype(BF16)
            d_ref[1, pl.ds(r0, R), :] = (dcu * u).astype(BF16)
            d_ref[2, pl.ds(r0, R), :] = (dcu * c).astype(BF16)
            return (carry[0] + jnp.sum(dconv * (c * u), axis=0, keepdims=True),
                    carry[1] + jnp.sum(dconv * cu1, axis=0, keepdims=True),
                    carry[2] + jnp.sum(dconv * cu2, axis=0, keepdims=True))

        zero = jnp.zeros((1, tc), F32)
        s0, s1, s2 = lax.fori_loop(0, S // R, step, (zero, zero, zero))
        for k, sk in enumerate((s0, s1, s2)):
            dw_ref[k:k + 1, :] += sk

    return pl.pallas_call(
        body, name=name, grid=(D // tc, B),
        in_specs=[pl.BlockSpec((3, None, S, tc), lambda j, b: (0, b, 0, j)),
                  pl.BlockSpec((None, S, tc), lambda j, b: (b, 0, j)),
                  pl.BlockSpec((None, None, 3, tc), lambda j, b: (j // per_chunk, layer, 0, j % per_chunk))],
        out_specs=[pl.BlockSpec((3, None, S, tc), lambda j, b: (0, b, 0, j)),
                   pl.BlockSpec((3, tc), lambda j, b: (0, j))],
        out_shape=[jax.ShapeDtypeStruct((3, B, S, D), BF16), jax.ShapeDtypeStruct((3, D), F32)],
        compiler_params=_params(2))(bcu, dz, cwg)


def _att_rows(dil, idx, nb):
    r, n = idx // nb, idx % nb
    if dil == 1:
        cur = pl.ds(pl.multiple_of(n * ATT_BLK, ATT_BLK), ATT_BLK)
        prev = pl.ds(pl.multiple_of(jnp.maximum(n - 1, 0) * ATT_BLK, ATT_BLK), ATT_BLK)
    else:
        cur = pl.ds(n * (ATT_BLK * dil) + r, ATT_BLK, stride=dil)
        prev = pl.ds(jnp.maximum(n - 1, 0) * (ATT_BLK * dil) + r, ATT_BLK, stride=dil)
    return n, cur, prev


def _att_bias(bias_ref, dil, sl_ref, hp):
    row = lax.broadcasted_iota(jnp.int32, (2 * ATT_BLK, 2 * ATT_BLK), 0)
    ci = lax.broadcasted_iota(jnp.int32, (2 * ATT_BLK, 2 * ATT_BLK), 1)
    j = ATT_BLK + (row & (ATT_BLK - 1)) - ci
    slope = jnp.where(row < ATT_BLK, sl_ref[2 * hp], sl_ref[2 * hp + 1])
    rest = jnp.where((j >= 0) & (j <= ATT_BLK), -slope * (dil * j).astype(F32), NEG_INF)
    bias_ref[1] = rest
    bias_ref[0] = jnp.where(ci >= ATT_BLK, rest, NEG_INF)


def _stack_heads(x16, lane):
    first = lane < HEAD_DIM
    return jnp.concatenate([jnp.where(first, x16, jnp.zeros_like(x16)),
                            jnp.where(first, jnp.zeros_like(x16), x16)], axis=0)


def _per_head(col, lane):
    return jnp.where(lane < HEAD_DIM, col[:ATT_BLK], col[ATT_BLK:])


def _attn_fwd(name, q, kv, slopes, n_heads):
    B, S, CQ = q.shape
    HP = n_heads * HEAD_DIM // LANES
    scale = HEAD_DIM ** -0.5
    n_groups = len(PATTERNS)
    CH = 256

    def body(sl_ref, q_ref, k_ref, v_ref, o_ref, lse_ref, bias_ref, *parts):
        og, lg = parts[:n_groups], parts[n_groups:]
        hp, g = pl.program_id(1), pl.program_id(2)
        lane = lax.broadcasted_iota(jnp.int32, (1, LANES), 1)

        for gi, (window, dil) in enumerate(PATTERNS):
            nb = S // dil // ATT_BLK

            @pl.when(g == gi)
            def _(gi=gi, dil=dil, nb=nb):
                _att_bias(bias_ref, dil, sl_ref, hp)

                def step(idx, carry):
                    n, cur, prev = _att_rows(dil, idx, nb)
                    qs = _stack_heads((q_ref[cur, :] * scale).astype(BF16), lane)
                    kc = jnp.concatenate([k_ref[prev, :], k_ref[cur, :]], axis=0).astype(BF16)
                    vc = jnp.concatenate([v_ref[prev, :], v_ref[cur, :]], axis=0).astype(BF16)
                    s = _dot_nt(qs, kc) + bias_ref[jnp.minimum(n, 1)]
                    m = jnp.max(s, axis=-1, keepdims=True)
                    p = jnp.exp(s - m)
                    l = jnp.sum(p, axis=-1, keepdims=True)
                    p16 = p.astype(BF16)
                    o_un = _dot(jnp.concatenate([p16[:ATT_BLK], p16[ATT_BLK:]], axis=1), _stack_heads_rows(vc, lane))
                    og[gi][cur, :] = o_un / _per_head(l, lane)
                    lg[gi][cur, :] = _per_head(m + jnp.log(l), lane)
                    return carry

                lax.fori_loop(0, S // ATT_BLK, step, 0, unroll=16)

        @pl.when(g == n_groups - 1)
        def _():
            def comb(i, carry):
                rows = pl.ds(pl.multiple_of(i * CH, CH), CH)
                a, b, c = lg[0][rows, :], lg[1][rows, :], lg[2][rows, :]
                m = jnp.maximum(jnp.maximum(a, b), c)
                ea, eb, ec = jnp.exp(a - m), jnp.exp(b - m), jnp.exp(c - m)
                z = ea + eb + ec
                o_ref[rows, :] = (ea / z) * og[0][rows, :] + (eb / z) * og[1][rows, :] + (ec / z) * og[2][rows, :]
                lse_ref[rows, :] = m + jnp.log(z)
                return carry

            lax.fori_loop(0, S // CH, comb, 0)

    blk = (None, S, LANES)
    out = pl.BlockSpec(blk, lambda b, hp, g: (b, 0, hp))
    return pl.pallas_call(
        body, name=name, grid=(B, HP, n_groups),
        in_specs=[pl.BlockSpec(memory_space=pltpu.SMEM),
                  pl.BlockSpec(blk, lambda b, hp, g: (b, 0, g * HP + hp)),
                  pl.BlockSpec(blk, lambda b, hp, g: (b, 0, g * 2 * HP + hp)),
                  pl.BlockSpec(blk, lambda b, hp, g: (b, 0, g * 2 * HP + HP + hp))],
        out_specs=[out, out],
        out_shape=[jax.ShapeDtypeStruct((B, S, HP * LANES), F32)] * 2,
        scratch_shapes=[pltpu.VMEM((2, 2 * ATT_BLK, 2 * ATT_BLK), F32)] + [pltpu.VMEM((S, LANES), F32)] * (2 * n_groups),
        compiler_params=_params(3))(slopes, q, kv, kv)


def _stack_heads_rows(x16, lane):
    first = lane < HEAD_DIM
    return jnp.concatenate([jnp.where(first, x16, jnp.zeros_like(x16)),
                            jnp.where(first, jnp.zeros_like(x16), x16)], axis=0)


def _attn_bwd(name, q, kv, slopes, o, lse, do, n_heads, dkv_prev):
    B, S, CQ = q.shape
    HP = n_heads * HEAD_DIM // LANES
    scale = HEAD_DIM ** -0.5
    n_groups = len(PATTERNS)
    n_prev = 0 if dkv_prev is None else 2

    def body(sl_ref, q_ref, k_ref, v_ref, o_ref, lse_ref, do_ref, *rest):
        dq_ref, dk_ref, dv_ref, bias_ref = rest[n_prev:]
        hp, g = pl.program_id(1), pl.program_id(2)
        lane = lax.broadcasted_iota(jnp.int32, (1, LANES), 1)
        first = lane < HEAD_DIM

        def flush(rows, dk, dv):
            if n_prev:
                dk = dk + rest[0][rows, :]
                dv = dv + rest[1][rows, :]
            dk_ref[rows, :] = dk
            dv_ref[rows, :] = dv

        for gi, (window, dil) in enumerate(PATTERNS):
            nb = S // dil // ATT_BLK
            n_blocks = S // ATT_BLK

            @pl.when(g == gi)
            def _(dil=dil, nb=nb, n_blocks=n_blocks):
                _att_bias(bias_ref, dil, sl_ref, hp)

                def block(idx, carry, first_of_all):
                    n, cur, prev = _att_rows(dil, idx, nb)
                    qs = _stack_heads((q_ref[cur, :] * scale).astype(BF16), lane)
                    kc = jnp.concatenate([k_ref[prev, :], k_ref[cur, :]], axis=0).astype(BF16)
                    vc = jnp.concatenate([v_ref[prev, :], v_ref[cur, :]], axis=0).astype(BF16)
                    dob = do_ref[cur, :]
                    prod = dob * o_ref[cur, :]
                    lseb = lse_ref[cur, :]
                    dos = _stack_heads(dob.astype(BF16), lane)
                    delta = jnp.concatenate(
                        [jnp.sum(jnp.where(first, prod, 0.0), axis=-1, keepdims=True),
                         jnp.sum(jnp.where(first, 0.0, prod), axis=-1, keepdims=True)], axis=0)
                    lse_col = jnp.concatenate(
                        [jnp.max(jnp.where(first, lseb, -jnp.inf), axis=-1, keepdims=True),
                         jnp.max(jnp.where(first, -jnp.inf, lseb), axis=-1, keepdims=True)], axis=0)
                    s = _dot_nt(qs, kc) + bias_ref[jnp.minimum(n, 1)]
                    p = jnp.exp(s - lse_col)
                    ds = p * (_dot_nt(dos, vc) - delta)
                    ds16 = ds.astype(BF16)
                    dq = _dot(jnp.concatenate([ds16[:ATT_BLK], ds16[ATT_BLK:]], axis=1), _stack_heads_rows(kc, lane))
                    dq_ref[cur, :] = dq * scale
                    dk = _dot_tn(ds16, qs)
                    dv = _dot_tn(p.astype(BF16), dos)

                    def flush_before():
                        _, before, _ = _att_rows(dil, idx - 1, nb)
                        flush(before, carry[0] + dk[:ATT_BLK], carry[1] + dv[:ATT_BLK])

                    if first_of_all:
                        pl.when(idx > 0)(flush_before)
                    else:
                        flush_before()
                    return dk[ATT_BLK:], dv[ATT_BLK:]

                def step(i, carry):
                    for u in range(BWD_UNROLL):
                        carry = block(i * BWD_UNROLL + u, carry, u == 0)
                    return carry

                zero = jnp.zeros((ATT_BLK, LANES), F32)
                dk_last, dv_last = lax.fori_loop(0, n_blocks // BWD_UNROLL, step, (zero, zero))
                _, last, _ = _att_rows(dil, n_blocks - 1, nb)
                flush(last, dk_last, dv_last)

    blk = (None, S, LANES)
    shared = pl.BlockSpec(blk, lambda b, hp, g: (b, 0, hp))
    grouped = pl.BlockSpec(blk, lambda b, hp, g: (b, 0, g * HP + hp))
    prev = [] if dkv_prev is None else list(dkv_prev)
    gshape = jax.ShapeDtypeStruct((B, S, n_groups * HP * LANES), F32)
    return pl.pallas_call(
        body, name=name, grid=(B, HP, n_groups),
        in_specs=[pl.BlockSpec(memory_space=pltpu.SMEM), grouped,
                  pl.BlockSpec(blk, lambda b, hp, g: (b, 0, g * 2 * HP + hp)),
                  pl.BlockSpec(blk, lambda b, hp, g: (b, 0, g * 2 * HP + HP + hp)),
                  shared, shared, shared] + [grouped] * n_prev,
        out_specs=[grouped] * 3, out_shape=[gshape] * 3,
        scratch_shapes=[pltpu.VMEM((2, 2 * ATT_BLK, 2 * ATT_BLK), F32)],
        compiler_params=_params(3))(slopes, q, kv, kv, o, lse, do, *prev)


def _final_loss(name, h, g, target, tm):
    T, D = h.shape

    def body(h_ref, g_ref, t_ref, loss_ref, dh_ref, dh16_ref, dg_ref):
        hf = h_ref[...]
        gv = g_ref[...]
        rstd = lax.rsqrt(jnp.mean(hf * hf, axis=-1, keepdims=True) + EPS)
        xhat = hf * rstd
        err = xhat * gv - t_ref[...]
        part = 0.5 * jnp.sum(jnp.mean(err * err, axis=-1, keepdims=True), axis=0, keepdims=True)
        dy = err * (1.0 / D)
        dg = jnp.sum(dy * xhat, axis=0, keepdims=True)
        dx = dy * gv
        dh = rstd * (dx - xhat * jnp.mean(dx * xhat, axis=-1, keepdims=True))
        dh_ref[...] = dh
        dh16_ref[...] = dh.astype(BF16)

        @pl.when(pl.program_id(0) == 0)
        def _():
            loss_ref[...] = part
            dg_ref[...] = dg

        @pl.when(pl.program_id(0) > 0)
        def _():
            loss_ref[...] += part
            dg_ref[...] += dg

    return pl.pallas_call(
        body, name=name, grid=(T // tm,),
        in_specs=[pl.BlockSpec((tm, D), lambda i: (i, 0)), pl.BlockSpec((1, D), lambda i: (0, 0)),
                  pl.BlockSpec((tm, D), lambda i: (i, 0))],
        out_specs=[pl.BlockSpec((1, 1), lambda i: (0, 0)), pl.BlockSpec((tm, D), lambda i: (i, 0)),
                   pl.BlockSpec((tm, D), lambda i: (i, 0)), pl.BlockSpec((1, D), lambda i: (0, 0))],
        out_shape=[jax.ShapeDtypeStruct((1, 1), F32), jax.ShapeDtypeStruct((T, D), F32),
                   jax.ShapeDtypeStruct((T, D), BF16), jax.ShapeDtypeStruct((1, D), F32)],
        compiler_params=_params(1))(h, g, target)


def _nt_rows(name, dh, wg, layer, a_mul, out_dtype, tm, deps=()):
    T, D = dh.shape
    rk = wg.shape[2]
    N = N_CHIPS * rk
    with_a = a_mul is not None

    def body(dh_ref, w_ref, *rest):
        o_ref = rest[-1]
        d16 = dh_ref[...]
        for ch in range(N_CHIPS):
            r = _dot_nt(d16, w_ref[ch])
            if with_a:
                r = r * (2.0 * jnp.maximum(rest[0][:, ch * rk:(ch + 1) * rk].astype(F32), 0.0))
            o_ref[:, ch * rk:(ch + 1) * rk] = r.astype(out_dtype)

    in_specs = [pl.BlockSpec((tm, D), lambda i: (i, 0)),
                pl.BlockSpec((N_CHIPS, None, rk, D), lambda i: (0, layer, 0, 0))]
    args = [dh, wg]
    if with_a:
        in_specs.append(pl.BlockSpec((tm, N), lambda i: (i, 0)))
        args.append(a_mul)
    in_specs += [ANY] * len(deps)
    args += list(deps)
    return pl.pallas_call(
        body, name=name, grid=(T // tm,), in_specs=in_specs,
        out_specs=pl.BlockSpec((tm, N), lambda i: (i, 0)),
        out_shape=jax.ShapeDtypeStruct((T, N), out_dtype),
        compiler_params=_params(1))(*args)


def _nt_cols(name, ysegs, wg, layer, tm, norm):
    Nw, cw = wg.shape[2], wg.shape[3]
    widths = [bs[-1] for _, bs, _ in ysegs]
    pieces = _pieces(widths, cw, 1024)
    ns = len(ysegs)
    T = norm[0].shape[0] if norm is not None else ysegs[0][0].shape[-2]

    def body(*refs):
        y_refs = refs[:ns]
        w_ref = refs[ns]
        acc = refs[-1]
        for n, (s, a0, ch, b0, wd) in enumerate(pieces):
            d = _dot_nt(y_refs[s][:, a0:a0 + wd].astype(BF16), w_ref[ch, :, b0:b0 + wd])
            if n == 0:
                acc[...] = d
            else:
                acc[...] += d
        if norm is None:
            refs[ns + 1][...] = acc[...]
        else:
            h_ref, g_ref, dhin_ref, out_ref, out16_ref, dg_ref = refs[ns + 1:ns + 7]
            dh_c, dg = _rms_bwd(h_ref[...], g_ref[...], acc[...])
            dh = dhin_ref[...] + dh_c
            out_ref[...] = dh
            out16_ref[...] = dh.astype(BF16)

            @pl.when(pl.program_id(0) == 0)
            def _():
                dg_ref[...] = dg

            @pl.when(pl.program_id(0) > 0)
            def _():
                dg_ref[...] += dg

    in_specs = [pl.BlockSpec(bs, im) for _, bs, im in ysegs]
    in_specs.append(pl.BlockSpec((N_CHIPS, None, Nw, cw), lambda i: (0, layer, 0, 0)))
    args = [a for a, _, _ in ysegs] + [wg]
    row = pl.BlockSpec((tm, Nw), lambda i: (i, 0))
    vec = pl.BlockSpec((1, Nw), lambda i: (0, 0))
    if norm is None:
        out_specs = row
        out_shape = jax.ShapeDtypeStruct((T, Nw), F32)
    else:
        in_specs += [row, vec, row]
        args += list(norm)
        out_specs = [row, row, vec]
        out_shape = [jax.ShapeDtypeStruct((T, Nw), F32), jax.ShapeDtypeStruct((T, Nw), BF16),
                     jax.ShapeDtypeStruct((1, Nw), F32)]
    return pl.pallas_call(
        body, name=name, grid=(T // tm,), in_specs=in_specs, out_specs=out_specs, out_shape=out_shape,
        scratch_shapes=[pltpu.VMEM((tm, Nw), F32)], compiler_params=_params(1))(*args)


def _tn(name, x, x_act, ysegs, cw, cols_layout, tmm, tt, deps=(), out_dtype=F32):
    T, M = x.shape
    widths = [bs[-1] for _, bs, _ in ysegs]
    N = sum(widths)
    pieces = _pieces(widths, cw if cols_layout else N, 1024)
    ns = len(ysegs)
    n_t = T // tt
    block = (N_CHIPS, tmm, cw) if cols_layout else (tmm, N)
    narrow = out_dtype != F32

    def body(x_ref, *refs):
        y_refs = refs[:ns]
        o_ref = refs[ns + len(deps)]
        acc = refs[-1] if narrow else o_ref

        @pl.when(pl.program_id(1) == 0)
        def _():
            acc[...] = jnp.zeros_like(acc)

        xt = x_act(x_ref[...])
        for s, a0, ch, b0, wd in pieces:
            d = _dot_tn(xt, y_refs[s][:, a0:a0 + wd].astype(BF16))
            if cols_layout:
                acc[ch, :, b0:b0 + wd] += d
            else:
                acc[:, b0:b0 + wd] += d
        if narrow:
            @pl.when(pl.program_id(1) == n_t - 1)
            def _():
                o_ref[...] = acc[...].astype(out_dtype)

    in_specs = [pl.BlockSpec((tt, tmm), lambda m, t: (t, m))] + [pl.BlockSpec(bs, im) for _, bs, im in ysegs]
    in_specs += [ANY] * len(deps)
    if cols_layout:
        out_specs = pl.BlockSpec(block, lambda m, t: (0, m, 0))
        out_shape = jax.ShapeDtypeStruct((N_CHIPS, M, cw), out_dtype)
    else:
        out_specs = pl.BlockSpec(block, lambda m, t: (m, 0))
        out_shape = jax.ShapeDtypeStruct((M, N), out_dtype)
    return pl.pallas_call(
        body, name=name, grid=(M // tmm, n_t), in_specs=in_specs, out_specs=out_specs, out_shape=out_shape,
        scratch_shapes=[pltpu.VMEM(block, F32)] if narrow else [],
        compiler_params=_params(2))(x, *[a for a, _, _ in ysegs], *deps)


def _seg2d(a, t_rows, grid_rank):
    w = a.shape[1]
    if grid_rank == 1:
        return (a, (t_rows, w), lambda i: (i, 0))
    return (a, (t_rows, w), lambda m, t: (t, 0))


def _kv_segments(dk, dv, C, t_rows, grid_rank):
    segs = []
    for g in range(len(PATTERNS)):
        for a in (dk, dv):
            if grid_rank == 1:
                segs.append((a, (t_rows, C), lambda i, g=g: (i, g)))
            else:
                segs.append((a, (t_rows, C), lambda m, t, g=g: (t, g)))
    return segs


def _seg_plane(a, plane, t_rows, grid_rank):
    w = a.shape[2]
    if grid_rank == 1:
        return (a, (None, t_rows, w), lambda i: (plane, i, 0))
    return (a, (None, t_rows, w), lambda m, t: (plane, t, 0))


def _row_tile(rows, row_bytes, budget_bytes=2 * 1024 * 1024):
    t = rows
    while t * row_bytes > budget_bytes and t % 32 == 0:
        t //= 2
    return t


N_DEVICES = 8


def _device_add(name, own, slots, place):
    _, _, hr, c = own.shape
    tr = _row_tile(hr, c * 4, 1024 * 1024)

    def body(place_ref, own_ref, *refs):
        o_ref = refs[-1]
        acc = own_ref[...].astype(F32)
        for r in refs[:-1]:
            acc = acc + r[...].astype(F32)
        o_ref[...] = acc

    def slot(k):
        return pl.BlockSpec((None, tr, c), lambda i, pr: ((2 * pr[0] + pr[1] + k) % N_DEVICES, i, 0))

    grid_spec = pltpu.PrefetchScalarGridSpec(
        num_scalar_prefetch=1, grid=(hr // tr,),
        in_specs=[pl.BlockSpec((None, None, tr, c), lambda i, pr: (pr[0], pr[1], i, 0))]
        + [slot(k) for k in range(1, N_DEVICES)],
        out_specs=pl.BlockSpec((None, tr, c), lambda i, pr: (pr[1], i, 0)))
    return pl.pallas_call(body, name=name, grid_spec=grid_spec,
                          out_shape=jax.ShapeDtypeStruct((2, hr, c), F32),
                          compiler_params=_params(1))(place, own, *[slots] * (N_DEVICES - 1))


def _adamw(name, w, g, m, v):
    rows, cols = w.shape
    tr = _row_tile(rows, cols * 4, 1024 * 1024)

    def body(w_ref, g_ref, m_ref, v_ref, d_ref, nm_ref, nv_ref):
        d_ref[...], nm_ref[...], nv_ref[...] = _adamw_math(w_ref[...], g_ref[...], m_ref[...], v_ref[...])

    spec = pl.BlockSpec((tr, cols), lambda i: (i, 0))
    return pl.pallas_call(
        body, name=name, grid=(rows // tr,), in_specs=[spec] * 4, out_specs=[spec] * 3,
        out_shape=[jax.ShapeDtypeStruct((rows, cols), F32)] * 3, compiler_params=_params(1))(w, g, m, v)


def _adamw_math(w, g, m, v):
    nm = ADAM_B1 * m + (1.0 - ADAM_B1) * g
    nv = ADAM_B2 * v + (1.0 - ADAM_B2) * jnp.square(g)
    m_hat = nm / (1.0 - ADAM_B1 ** ADAM_STEP)
    v_hat = nv / (1.0 - ADAM_B2 ** ADAM_STEP)
    return -ADAM_LR * (m_hat / (jnp.sqrt(v_hat) + ADAM_EPS) + ADAM_WD * w), nm, nv


def _adamw_layers(name, w, grads, m, v):
    L, r, c = w.shape
    tr = _row_tile(r, L * c * 4, 1024 * 1024)

    def body(*refs):
        w_ref, m_ref, v_ref = refs[:3]
        g_refs = refs[3:3 + L]
        go_ref, d_ref, nm_ref, nv_ref = refs[3 + L:]
        for l in range(L):
            g = g_refs[l][...]
            go_ref[l] = g
            d_ref[l], nm_ref[l], nv_ref[l] = _adamw_math(w_ref[l], g, m_ref[l], v_ref[l])

    stacked = pl.BlockSpec((L, tr, c), lambda i: (0, i, 0))
    return pl.pallas_call(
        body, name=name, grid=(r // tr,),
        in_specs=[stacked] * 3 + [pl.BlockSpec((tr, c), lambda i: (i, 0))] * L, out_specs=[stacked] * 4,
        out_shape=[jax.ShapeDtypeStruct((L, r, c), F32)] * 4, compiler_params=_params(1))(w, m, v, *grads)


def _place():
    x, y, c = lax.axis_index("x"), lax.axis_index("y"), lax.axis_index("c")
    chips = [(1 - x, y), (x, 1 - y), (1 - x, 1 - y)]
    return x, y, c, chips


HBM = pl.BlockSpec(memory_space=pltpu.HBM)
SEM = pl.BlockSpec(memory_space=pltpu.SEMAPHORE)
EFFECT = pltpu.SideEffectType.DATAFLOW_SIDE_EFFECTING


class _Copy:
    def __init__(self, src, src_view, land, dst_view, recv_view, target):
        self.src, self.src_view, self.land, self.dst_view, self.recv_view, self.target = (
            src, src_view, land, dst_view, recv_view, target)


def _whole(ref, place):
    return ref


def _split_start(name, srcs, land_shapes, plans):
    skeys, lkeys = list(srcs), list(land_shapes)
    ns, nl, ng = len(skeys), len(lkeys), len(plans)

    def body(*refs):
        src = dict(zip(skeys, refs[:ns]))
        land = dict(zip(lkeys, refs[ns:ns + nl]))
        sems = refs[ns + nl:ns + nl + 2 * ng]
        token = refs[-1]
        place = _place()
        for gi, plan in enumerate(plans):
            for k, cp in enumerate(plan):
                dst = land[cp.land] if cp.land in land else src[cp.land]
                pltpu.make_async_remote_copy(
                    src_ref=cp.src_view(src[cp.src], place), dst_ref=cp.dst_view(dst, place),
                    send_sem=sems[2 * gi].at[k], recv_sem=sems[2 * gi + 1].at[k],
                    device_id=cp.target(place), device_id_type=MESH).start()
        token[...] = jnp.zeros_like(token)

    sem_shapes = []
    for plan in plans:
        sem_shapes += [pltpu.SemaphoreType.DMA((len(plan),))] * 2
    buffers = [srcs[k] for k in skeys] + [lax.empty(land_shapes[k].shape, land_shapes[k].dtype) for k in lkeys]
    outs = pl.pallas_call(
        body, name=name,
        out_shape=(*sem_shapes, *[pltpu.HBM(a.shape, a.dtype) for a in buffers], jax.ShapeDtypeStruct((8, LANES), F32)),
        in_specs=[HBM] * (ns + nl),
        out_specs=(*[SEM] * (2 * ng), *[HBM] * (ns + nl), pl.BlockSpec(memory_space=pltpu.VMEM)),
        input_output_aliases={i: 2 * ng + i for i in range(ns + nl)},
        compiler_params=pltpu.CompilerParams(has_side_effects=EFFECT),
    )(*[pltpu.with_memory_space_constraint(a, pltpu.HBM) for a in buffers])
    sems = [(outs[2 * gi], outs[2 * gi + 1]) for gi in range(ng)]
    thru = outs[2 * ng:2 * ng + ns + nl]
    return sems, dict(zip(skeys, thru[:ns])), dict(zip(lkeys, thru[ns:])), outs[-1]


def _split_wait(name, sems, srcs, lands, plan, after):
    skeys, lkeys = list(srcs), list(lands)
    ns, nl = len(skeys), len(lkeys)

    def body(*refs):
        src = dict(zip(skeys, refs[:ns]))
        land = dict(zip(lkeys, refs[ns:ns + nl]))
        ssem, rsem = refs[ns + nl], refs[ns + nl + 1]
        place = _place()
        for k, cp in enumerate(plan):
            dst = land[cp.land] if cp.land in land else src[cp.land]
            pltpu.make_async_remote_copy(
                src_ref=cp.src_view(src[cp.src], place), dst_ref=cp.dst_view(dst, place),
                send_sem=ssem.at[k], recv_sem=rsem.at[k],
                device_id=cp.target(place), device_id_type=MESH).wait_send()
            got = cp.recv_view(dst, place)
            pltpu.make_async_remote_copy(
                src_ref=got, dst_ref=got, send_sem=ssem.at[k], recv_sem=rsem.at[k],
                device_id=cp.target(place), device_id_type=MESH).wait_recv()

    buffers = [srcs[k] for k in skeys] + [lands[k] for k in lkeys]
    outs = pl.pallas_call(
        body, name=name, out_shape=tuple(pltpu.HBM(a.shape, a.dtype) for a in buffers),
        in_specs=(*[HBM] * (ns + nl), SEM, SEM, ANY), out_specs=tuple([HBM] * (ns + nl)),
        input_output_aliases={i: i for i in range(ns + nl)},
        compiler_params=pltpu.CompilerParams(has_side_effects=EFFECT),
    )(*buffers, sems[0], sems[1], after)
    return dict(zip(skeys, outs[:ns])), dict(zip(lkeys, outs[ns:]))


def _chip_of(place):
    x, y, c, chips = place
    return 2 * x + y


GATHER_FIRST = 2


class _WeightGather:
    def __init__(self, blocks):
        self.plans, shapes = {}, {}
        for key, a in blocks.items():
            shapes[key] = jax.ShapeDtypeStruct((N_CHIPS,) + a.shape, a.dtype)
            slot = lambda ref, place: ref.at[_chip_of(place)]
            plan = [_Copy(key, _whole, key, slot,
                          lambda ref, place, k=k: ref.at[2 * place[3][k][0] + place[3][k][1]],
                          lambda place, k=k: (place[3][k][0], place[3][k][1], place[2])) for k in range(3)]
            plan.append(_Copy(key, _whole, key, slot, slot, lambda place: (place[0], place[1], 1 - place[2])))
            self.plans[key] = plan
        keys = list(blocks)
        self.sems, self.srcs, self.lands = {}, {}, {}
        for name, part in (("gather_start_first", keys[:GATHER_FIRST]), ("gather_start", keys[GATHER_FIRST:])):
            sems, srcs, lands, self.token = _split_start(name, {k: blocks[k] for k in part}, {k: shapes[k] for k in part},
                                                         [self.plans[k] for k in part])
            self.sems.update(zip(part, sems))
            self.srcs.update(srcs)
            self.lands.update(lands)

    def get(self, l, name, after):
        key = (l, name)
        _, lands = _split_wait(f"gather_wait_{name}{l}", self.sems[key], {key: self.srcs[key]},
                               {key: self.lands[key]}, self.plans[key], after)
        return lands[key][:, None]


class _GradReduce:
    def __init__(self, place):
        self.place = place
        self.jobs = []
        self.done = {}
        self.n = 0

    def submit(self, grads):
        views = {k: a.reshape(N_CHIPS, 2, a.shape[1] // 2, a.shape[2]) for k, a in grads.items()}
        shapes = {k: jax.ShapeDtypeStruct((N_DEVICES,) + a.shape[2:], a.dtype) for k, a in views.items()}

        def peer(place, k):
            x, y, c, _ = place
            return (1 - x if k & 4 else x, 1 - y if k & 2 else y, 1 - c if k & 1 else c)

        def index(dev):
            return 4 * dev[0] + 2 * dev[1] + dev[2]

        plan = []
        for key in views:
            for k in range(1, N_DEVICES):
                plan.append(_Copy(
                    key, lambda ref, place, k=k: ref.at[2 * peer(place, k)[0] + peer(place, k)[1], peer(place, k)[2]],
                    key, lambda ref, place: ref.at[index(place[:3])],
                    lambda ref, place, k=k: ref.at[index(peer(place, k))],
                    lambda place, k=k: peer(place, k)))
        sems, srcs, lands, token = _split_start(f"grad_start{self.n}", views, shapes, [plan])
        self.jobs.append(dict(id=self.n, sems=sems[0], srcs=srcs, lands=lands, plan=plan))
        self.n += 1
        return token

    def pump(self, after):
        return []

    def finish(self, after):
        for job in self.jobs:
            srcs, lands = _split_wait(f"grad_wait{job['id']}", job["sems"], job["srcs"], job["lands"], job["plan"],
                                      after)
            for i, k in enumerate(srcs):
                self.done[k] = _device_add(f"grad_add{job['id']}_{i}", srcs[k], lands[k], self.place)
        self.jobs = []
        return self.done


class _PairShare:
    def __init__(self, halves, types):
        sibling = lambda place: (place[0], place[1], 1 - place[2])
        mine = lambda ref, place: ref.at[place[2]]
        theirs = lambda ref, place: ref.at[1 - place[2]]
        self.plans = {t: [_Copy(k, mine, k, mine, theirs, sibling) for k in halves if k[0] == t] for t in types}
        sems, self.bufs, _, self.token = _split_start("share_start", halves, {}, list(self.plans.values()))
        self.sems = dict(zip(self.plans, sems))

    def get(self, t, after):
        keys = [cp.src for cp in self.plans[t]]
        bufs, _ = _split_wait(f"share_wait_{t}", self.sems[t], {k: self.bufs[k] for k in keys}, {}, self.plans[t], after)
        return bufs


def _small_allreduce(part):
    R, C = part.shape
    N_DEV = 8

    def body(in_ref, out_ref, slots, ssem, rsem):
        x, y, c, _ = _place()
        me = 4 * x + 2 * y + c
        sends = []
        for k in range(1, N_DEV):
            kx, ky, kc = (k >> 2) & 1, (k >> 1) & 1, k & 1
            peer = (1 - x if kx else x, 1 - y if ky else y, 1 - c if kc else c)
            cp = pltpu.make_async_remote_copy(
                src_ref=in_ref, dst_ref=slots.at[me], send_sem=ssem.at[k], recv_sem=rsem.at[k],
                device_id=peer, device_id_type=MESH)
            cp.start()
            sends.append(cp)
        slots[me] = in_ref[...]
        for k in range(1, N_DEV):
            kx, ky, kc = (k >> 2) & 1, (k >> 1) & 1, k & 1
            peer = (1 - x if kx else x, 1 - y if ky else y, 1 - c if kc else c)
            slot = slots.at[4 * peer[0] + 2 * peer[1] + peer[2]]
            pltpu.make_async_remote_copy(
                src_ref=slot, dst_ref=slot, send_sem=ssem.at[k], recv_sem=rsem.at[k],
                device_id=peer, device_id_type=MESH).wait_recv()
        acc = slots[0]
        for d in range(1, N_DEV):
            acc = acc + slots[d]
        out_ref[...] = acc
        for cp in sends:
            cp.wait_send()

    vm = pl.BlockSpec(memory_space=pltpu.VMEM)
    return pl.pallas_call(
        body, name="small_allreduce", in_specs=[vm], out_specs=vm,
        out_shape=jax.ShapeDtypeStruct((R, C), F32),
        scratch_shapes=[pltpu.VMEM((N_DEV, R, C), F32), pltpu.SemaphoreType.DMA((N_DEV,)),
                        pltpu.SemaphoreType.DMA((N_DEV,))])(part)


def _local_step(x, target, norm_mix, norm_mlp, norm_kv, norm_final, weights, sink, n_a, n_heads):
    B, S, D = x.shape
    T = B * S
    C = n_heads * HEAD_DIM
    depth = norm_mix.shape[0]
    slopes = 2.0 ** (-ALIBI_MAX_BIAS * jnp.arange(1, n_heads + 1, dtype=F32) / n_heads)
    tm = min(512, T)
    row = lambda v: v.reshape(1, -1)

    h = x.reshape(T, D)
    saved, Wl = [], []
    kv = nkv = h_kv = cwg = None
    for l in range(depth):
        s = {"h_in": h}
        w = {}
        Wl.append(w)
        if l < n_a:
            w["w_a_in"] = weights.get(l, "w_a_in", h)
            first = [weights.token] if l == 0 and hasattr(weights, "token") else []
            s["n1"], bcu = _norm_mm(f"a_in_fwd{l}", h, row(norm_mix[l]), w["w_a_in"], 0, 3, BF16, tm, first)
            s["bcu"] = bcu.reshape(3, B, S, D)
            if l == 0:
                cwg = weights.get(0, "conv", bcu)[:, 0, :n_a * 3].reshape(N_CHIPS, n_a, 3, -1)
            s["z"] = _conv_fwd(f"conv_fwd{l}", s["bcu"], cwg, l, LANES).reshape(T, D)
            w["w_a_out"] = weights.get(l, "w_a_out", s["z"])
            h = _mm_res_rows(f"a_out_fwd{l}", s["z"], w["w_a_out"], 0, h, _to_bf16, tm)
        else:
            i = l - n_a
            if i == 0:
                h_kv = h
                w["w_kv"] = weights.get(l, "w_kv", h)
                nkv, kv = _norm_mm("kv_fwd", h, row(norm_kv), w["w_kv"], 0, 1, F32, tm)
                kv = kv.reshape(B, S, 2 * 3 * C)
            w["w_q"] = weights.get(l, "w_q", h)
            s["n1"], q = _norm_mm(f"q_fwd{i}", h, row(norm_mix[l]), w["w_q"], 0, 1, F32, tm)
            s["q"] = q.reshape(B, S, 3 * C)
            o, lse = _attn_fwd(f"attn_fwd{i}", s["q"], kv, slopes, n_heads)
            s["o"], s["lse"] = o.reshape(T, C), lse.reshape(T, C)
            w["w_o"] = weights.get(l, "w_o", o)
            h = _mm_res_cols(f"o_fwd{i}", s["o"], w["w_o"], 0, h, tm)
        s["h_mid"] = h
        w["w_up"] = weights.get(l, "w_up", h)
        w["w_down"] = weights.get(l, "w_down", h)
        s["n2"], s["a"], h = _mlp_fwd(f"mlp_fwd{l}", h, row(norm_mlp[l]), w["w_up"], w["w_down"], tm)
        F = s["a"].shape[1]
        saved.append(s)

    loss, dh, dh16, dg_final = _final_loss("loss_head", h, row(norm_final), target.reshape(T, D), tm)

    g_mix, g_mlp = [None] * depth, [None] * depth
    g_conv = [None] * n_a
    dkv = None
    tt = min(512, T)
    deps = []
    for l in reversed(range(depth)):
        s, w = saved[l], Wl[l]
        g_down = _tn(f"down_wgrad{l}", s["a"], _relu2_bf16, [_seg2d(dh16, tt, 2)], None, False,
                     min(2048, F), tt, deps, BF16).reshape(N_CHIPS, F // N_CHIPS, D)
        da, dh, dh16, g_mlp[l] = _mlp_bwd(f"mlp_bwd{l}", dh, dh16, s["a"], w["w_down"], w["w_up"], s["h_mid"],
                                          row(norm_mlp[l]), tm)
        g_up = _tn(f"up_wgrad{l}", s["n2"], _to_bf16, [_seg2d(da, tt, 2)], F // N_CHIPS, True, D, tt, (), BF16)
        deps = sink.pump(dh) + [sink.submit({("w_up", l): g_up, ("w_down", l): g_down})]
        if l < n_a:
            g_out = _tn(f"a_out_wgrad{l}", s["z"], _to_bf16, [_seg2d(dh16, tt, 2)], None, False,
                        D, tt, deps, BF16).reshape(N_CHIPS, D // N_CHIPS, D)
            dz = _nt_rows(f"a_out_bwd{l}", dh16, w["w_a_out"], 0, None, F32, tm)
            deps = sink.pump(dz) + [sink.submit({("w_a_out", l): g_out})]
            dbcu, g_conv[l] = _conv_bwd(f"conv_bwd{l}", s["bcu"], dz.reshape(B, S, D), cwg, l, LANES)
            dbcu = dbcu.reshape(3, T, D)
            g_in = _tn(f"a_in_wgrad{l}", s["n1"], _to_bf16, [_seg_plane(dbcu, p, tt, 2) for p in range(3)],
                       3 * D // N_CHIPS, True, D, tt, deps, BF16)
            dh, dh16, g_mix[l] = _nt_cols(f"a_in_bwd{l}", [_seg_plane(dbcu, p, tm, 1) for p in range(3)],
                                          w["w_a_in"], 0, tm, (s["h_in"], row(norm_mix[l]), dh))
            mixer = {("w_a_in", l): g_in}
        else:
            i = l - n_a
            g_o = _tn(f"o_wgrad{i}", s["o"], _to_bf16, [_seg2d(dh16, tt, 2)], D // N_CHIPS, True, C, tt, deps,
                      BF16)
            do = _nt_cols(f"o_bwd{i}", [_seg2d(dh16, tm, 1)], w["w_o"], 0, tm, None)
            deps = sink.pump(do) + [sink.submit({("w_o", i): g_o})]
            dq, dk, dv = _attn_bwd(f"attn_bwd{i}", s["q"], kv, slopes, s["o"].reshape(B, S, C),
                                   s["lse"].reshape(B, S, C), do.reshape(B, S, C), n_heads, dkv)
            dkv = (dk, dv)
            dq = dq.reshape(T, 3 * C)
            g_q = _tn(f"q_wgrad{i}", s["n1"], _to_bf16, [_seg2d(dq, tt, 2)], 3 * C // N_CHIPS, True, D, tt, deps,
                      BF16)
            dh, dh16, g_mix[l] = _nt_cols(f"q_bwd{i}", [_seg2d(dq, tm, 1)], w["w_q"], 0, tm,
                                          (s["h_in"], row(norm_mix[l]), dh))
            mixer = {("w_q", i): g_q}
            if i == 0:
                dk2, dv2 = (t.reshape(T, 3 * C) for t in dkv)
                mixer[("w_kv", 0)] = _tn("kv_wgrad", nkv, _to_bf16, _kv_segments(dk2, dv2, C, tt, 2),
                                         6 * C // N_CHIPS, True, D, tt, (), BF16)
                dh, dh16, g_kv = _nt_cols("kv_bwd", _kv_segments(dk2, dv2, C, tm, 1), w["w_kv"], 0, tm,
                                          (h_kv, row(norm_kv), dh))
        deps = sink.pump(dh) + [sink.submit(mixer)]
    small = dict(norm_mix=jnp.concatenate(g_mix, axis=0), norm_mlp=jnp.concatenate(g_mlp, axis=0),
                 norm_kv=g_kv, norm_final=dg_final, conv_w=jnp.stack(g_conv))
    return loss, dh.reshape(B, S, D), small


BIG = ("w_a_in", "w_a_out", "w_kv", "w_q", "w_o", "w_up", "w_down")
CONV_PAD_ROWS = 16


def kernel(x, norm_mix, norm_mlp, w_a_in, conv_w, w_a_out, norm_kv, w_kv, w_q, w_o, w_up, w_down, norm_final, loss_target, m_norm_mix, m_norm_mlp, m_w_a_in, m_conv_w, m_w_a_out, m_norm_kv, m_w_kv, m_w_q, m_w_o, m_w_up, m_w_down, m_norm_final, v_norm_mix, v_norm_mlp, v_w_a_in, v_conv_w, v_w_a_out, v_norm_kv, v_w_kv, v_w_q, v_w_o, v_w_up, v_w_down, v_norm_final):
    D = x.shape[-1]
    w = dict(norm_mix=norm_mix, norm_mlp=norm_mlp, w_a_in=w_a_in, conv_w=conv_w, w_a_out=w_a_out, norm_kv=norm_kv,
             w_kv=w_kv[None], w_q=w_q, w_o=w_o, w_up=w_up, w_down=w_down, norm_final=norm_final)
    m = dict(norm_mix=m_norm_mix, norm_mlp=m_norm_mlp, w_a_in=m_w_a_in, conv_w=m_conv_w, w_a_out=m_w_a_out,
             norm_kv=m_norm_kv, w_kv=m_w_kv[None], w_q=m_w_q, w_o=m_w_o, w_up=m_w_up, w_down=m_w_down,
             norm_final=m_norm_final)
    v = dict(norm_mix=v_norm_mix, norm_mlp=v_norm_mlp, w_a_in=v_w_a_in, conv_w=v_conv_w, w_a_out=v_w_a_out,
             norm_kv=v_norm_kv, w_kv=v_w_kv[None], w_q=v_w_q, w_o=v_w_o, w_up=v_w_up, w_down=v_w_down,
             norm_final=v_norm_final)
    depth = norm_mix.shape[0]
    n_a, taps, cwc = conv_w.shape
    n_heads = w_o.shape[1] // HEAD_DIM

    conv_rows = jnp.zeros((CONV_PAD_ROWS, cwc), F32).at[:n_a * taps].set(conv_w.reshape(n_a * taps, cwc))
    blocks = {}
    for l in range(depth):
        if l < n_a:
            blocks[(l, "w_a_in")] = w_a_in[l].astype(BF16)
            if l == 0:
                blocks[(0, "conv")] = conv_rows
            blocks[(l, "w_a_out")] = w_a_out[l].astype(BF16)
        else:
            if l == n_a:
                blocks[(l, "w_kv")] = w_kv.astype(BF16)
            blocks[(l, "w_q")] = w_q[l - n_a].astype(BF16)
            blocks[(l, "w_o")] = w_o[l - n_a].astype(BF16)
        blocks[(l, "w_up")] = w_up[l].astype(BF16)
        blocks[(l, "w_down")] = w_down[l].astype(BF16)
    weights = _WeightGather(blocks)
    place = jnp.stack([2 * lax.axis_index("x") + lax.axis_index("y"), lax.axis_index("c")]).astype(jnp.int32)
    sink = _GradReduce(place)

    loss, grad_x, small = _local_step(x, loss_target, norm_mix, norm_mlp, norm_kv, norm_final, weights, sink,
                                      n_a, n_heads)
    loss = lax.psum(loss[0, 0], ("x", "y", "c"))

    share = _PairShare(sink.finish(grad_x), BIG)
    grads = {}

    packed = jnp.concatenate([small["norm_mix"], small["norm_mlp"], small["norm_kv"], small["norm_final"],
                              small["conv_w"].reshape(n_a * taps, D)], axis=0)
    pad = (-packed.shape[0]) % 8
    packed = jnp.pad(packed, ((0, pad), (0, 0)))
    total = _small_allreduce(packed)
    grads["norm_mix"] = total[:depth]
    grads["norm_mlp"] = total[depth:2 * depth]
    grads["norm_kv"] = total[2 * depth]
    grads["norm_final"] = total[2 * depth + 1]
    chip = 2 * lax.axis_index("x") + lax.axis_index("y")
    conv_full = total[2 * depth + 2:2 * depth + 2 + n_a * taps].reshape(n_a, taps, N_CHIPS, cwc)
    grads["conv_w"] = lax.dynamic_index_in_dim(conv_full, chip, axis=2, keepdims=False)

    order = ("norm_mix", "norm_mlp", "w_a_in", "conv_w", "w_a_out", "norm_kv", "w_kv", "w_q", "w_o", "w_up",
             "w_down", "norm_final")
    delta, new_m, new_v = {}, {}, {}
    vec_names = ("norm_mix", "norm_mlp", "norm_kv", "norm_final")
    rows_of = lambda a: a.reshape(-1, D)
    vw, vg, vm_, vv = (jnp.concatenate([rows_of(t[k]) for k in vec_names], axis=0) for t in (w, grads, m, v))
    vpad = (-vw.shape[0]) % 8
    padrows = lambda a: jnp.pad(a, ((0, vpad), (0, 0)))
    vd, vnm, vnv = _adamw("adamw_norms", padrows(vw), padrows(vg), padrows(vm_), padrows(vv))
    off = 0
    for k in vec_names:
        r = rows_of(w[k]).shape[0]
        delta[k] = vd[off:off + r].reshape(w[k].shape)
        new_m[k] = vnm[off:off + r].reshape(w[k].shape)
        new_v[k] = vnv[off:off + r].reshape(w[k].shape)
        off += r
    cpad = (-n_a * taps) % 8
    two_d = lambda a: jnp.pad(a.reshape(-1, cwc), ((0, cpad), (0, 0)))
    cd, cnm, cnv = _adamw("adamw_conv_w", two_d(w["conv_w"]), two_d(grads["conv_w"]), two_d(m["conv_w"]),
                          two_d(v["conv_w"]))
    delta["conv_w"], new_m["conv_w"], new_v["conv_w"] = (t[:n_a * taps].reshape(conv_w.shape) for t in (cd, cnm, cnv))
    after = cd
    for k in sorted(BIG, key=lambda k: w[k].size):
        shared = share.get(k, after)
        per_layer = [shared[(k, l)].reshape(w[k].shape[1:]) for l in range(w[k].shape[0])]
        grads[k], delta[k], new_m[k], new_v[k] = _adamw_layers(f"adamw_{k}", w[k], per_layer, m[k], v[k])
        after = delta[k]
    fix = lambda k, a: a[0] if k == "w_kv" else a
    return (loss, grad_x, *[fix(k, grads[k]) for k in order], *[fix(k, delta[k]) for k in order],
            *[fix(k, new_m[k]) for k in order], *[fix(k, new_v[k]) for k in order])
```

```python
import jax
import jax.numpy as jnp
from jax import lax
from jax.experimental import pallas as pl
from jax.experimental.pallas import tpu as pltpu

F32 = jnp.float32
BF16 = jnp.bfloat16
MESH = pl.DeviceIdType.MESH

EPS = 1e-5
PATTERNS = ((128, 1), (512, 4), (2048, 16))
HEAD_DIM = 64
ALIBI_MAX_BIAS = 8.0
NEG_INF = -1e30
ATT_BLK = 128
BWD_UNROLL = 16
N_CHIPS = 4
LANES = 128
VMEM_LIMIT = 56 * 1024 * 1024

ADAM_LR = 0.001
ADAM_B1 = 0.9
ADAM_B2 = 0.999
ADAM_EPS = 1e-08
ADAM_WD = 0.01
ADAM_STEP = 10


ANY = pl.BlockSpec(memory_space=pl.ANY)


def _params(n_grid_axes):
    return pltpu.CompilerParams(dimension_semantics=("arbitrary",) * n_grid_axes, vmem_limit_bytes=VMEM_LIMIT)


def _dot(a, b):
    return jnp.dot(a, b, preferred_element_type=F32)


def _dot_nt(a, b):
    return lax.dot_general(a, b, (((1,), (1,)), ((), ())), preferred_element_type=F32)


def _dot_tn(a, b):
    return lax.dot_general(a, b, (((0,), (0,)), ((), ())), preferred_element_type=F32)


def _relu2(a):
    return jnp.square(jnp.maximum(a, 0.0))


def _rms(hf, g):
    y = hf * lax.rsqrt(jnp.mean(hf * hf, axis=-1, keepdims=True) + EPS)
    return y * g


def _rms_bwd(hf, g, dn):
    rstd = lax.rsqrt(jnp.mean(hf * hf, axis=-1, keepdims=True) + EPS)
    xhat = hf * rstd
    dg = jnp.sum(dn * xhat, axis=0, keepdims=True)
    dx = dn * g
    dh = rstd * (dx - xhat * jnp.mean(dx * xhat, axis=-1, keepdims=True))
    return dh, dg


def _pieces(seg_widths, chunk_width, max_width):
    total = sum(seg_widths)
    cuts = {0, total}
    acc = 0
    for w in seg_widths:
        cuts.add(acc)
        acc += w
    cuts.update(range(0, total, chunk_width))
    cuts = sorted(cuts)
    fine = []
    for lo, hi in zip(cuts[:-1], cuts[1:]):
        while hi - lo > max_width:
            fine.append((lo, lo + max_width))
            lo += max_width
        fine.append((lo, hi))
    out = []
    for lo, hi in fine:
        acc = 0
        for s, w in enumerate(seg_widths):
            if lo < acc + w:
                break
            acc += w
        out.append((s, lo - acc, lo // chunk_width, lo % chunk_width, hi - lo))
    return out


def _relu2_bf16(a):
    return _relu2(a.astype(F32)).astype(BF16)


def _to_bf16(a):
    return a.astype(BF16)


def _norm_mm(name, h, g, wg, layer, planes, out_dtype, tm, deps=()):
    T, D = h.shape
    cw = wg.shape[3]
    N = N_CHIPS * cw
    pw = N // planes
    pieces = _pieces([pw] * planes, cw, 512)

    def body(h_ref, g_ref, w_ref, *rest):
        n_ref, o_ref = rest[len(deps):]
        n = _rms(h_ref[...], g_ref[...]).astype(BF16)
        n_ref[...] = n
        for s, a0, ch, b0, wd in pieces:
            o_ref[s, :, a0:a0 + wd] = _dot(n, w_ref[ch, :, b0:b0 + wd]).astype(out_dtype)

    return pl.pallas_call(
        body, name=name, grid=(T // tm,),
        in_specs=[pl.BlockSpec((tm, D), lambda i: (i, 0)),
                  pl.BlockSpec((1, D), lambda i: (0, 0)),
                  pl.BlockSpec((N_CHIPS, None, D, cw), lambda i: (0, layer, 0, 0))] + [ANY] * len(deps),
        out_specs=[pl.BlockSpec((tm, D), lambda i: (i, 0)),
                   pl.BlockSpec((planes, tm, pw), lambda i: (0, i, 0))],
        out_shape=[jax.ShapeDtypeStruct((T, D), BF16), jax.ShapeDtypeStruct((planes, T, pw), out_dtype)],
        compiler_params=_params(1))(h, g, wg, *deps)


def _resident(shape, index_map):
    return pl.BlockSpec(shape, index_map, pipeline_mode=pl.Buffered(1))


def _mm_res_rows(name, a, wg, layer, h, act, tm):
    T = a.shape[0]
    rk, D = wg.shape[2], wg.shape[3]

    def body(a_ref, w_ref, h_ref, o_ref):
        acc = h_ref[...]
        for k in range(N_CHIPS):
            acc = acc + _dot(act(a_ref[:, k * rk:(k + 1) * rk]), w_ref[k])
        o_ref[...] = acc

    return pl.pallas_call(
        body, name=name, grid=(T // tm,),
        in_specs=[pl.BlockSpec((tm, N_CHIPS * rk), lambda i: (i, 0)),
                  pl.BlockSpec((N_CHIPS, None, rk, D), lambda i: (0, layer, 0, 0)),
                  pl.BlockSpec((tm, D), lambda i: (i, 0))],
        out_specs=pl.BlockSpec((tm, D), lambda i: (i, 0)),
        out_shape=jax.ShapeDtypeStruct((T, D), F32),
        compiler_params=_params(1))(a, wg, h)


def _mm_res_cols(name, a, wg, layer, h, tm):
    T, K = a.shape
    cw = wg.shape[3]
    D = N_CHIPS * cw

    def body(a_ref, w_ref, h_ref, o_ref):
        a16 = a_ref[...].astype(BF16)
        for j in range(N_CHIPS):
            o_ref[:, j * cw:(j + 1) * cw] = h_ref[:, j * cw:(j + 1) * cw] + _dot(a16, w_ref[j])

    return pl.pallas_call(
        body, name=name, grid=(T // tm,),
        in_specs=[pl.BlockSpec((tm, K), lambda i: (i, 0)),
                  pl.BlockSpec((N_CHIPS, None, K, cw), lambda i: (0, layer, 0, 0)),
                  pl.BlockSpec((tm, D), lambda i: (i, 0))],
        out_specs=pl.BlockSpec((tm, D), lambda i: (i, 0)),
        out_shape=jax.ShapeDtypeStruct((T, D), F32),
        compiler_params=_params(1))(a, wg, h)


def _mlp_fwd(name, h, g, wup, wdown, tm):
    T, D = h.shape
    cw = wup.shape[3]

    def body(h_ref, g_ref, wu_ref, wd_ref, n_ref, a_ref, o_ref):
        hf = h_ref[...]
        n = _rms(hf, g_ref[...]).astype(BF16)
        n_ref[...] = n
        acc = hf
        for ch in range(N_CHIPS):
            a16 = _dot(n, wu_ref[ch]).astype(BF16)
            a_ref[:, ch * cw:(ch + 1) * cw] = a16
            acc = acc + _dot(_relu2_bf16(a16), wd_ref[ch])
        o_ref[...] = acc

    row = pl.BlockSpec((tm, D), lambda i: (i, 0))
    return pl.pallas_call(
        body, name=name, grid=(T // tm,),
        in_specs=[row, pl.BlockSpec((1, D), lambda i: (0, 0)),
                  _resident((N_CHIPS, None, D, cw), lambda i: (0, 0, 0, 0)),
                  _resident((N_CHIPS, None, cw, D), lambda i: (0, 0, 0, 0))],
        out_specs=[row, pl.BlockSpec((tm, N_CHIPS * cw), lambda i: (i, 0)), row],
        out_shape=[jax.ShapeDtypeStruct((T, D), BF16), jax.ShapeDtypeStruct((T, N_CHIPS * cw), BF16),
                   jax.ShapeDtypeStruct((T, D), F32)],
        compiler_params=_params(1))(h, g, wup, wdown)


def _mlp_bwd(name, dh, dh16, a, wdown, wup, h_mid, g, tm, deps=()):
    T, D = dh.shape
    cw = wup.shape[3]
    F = N_CHIPS * cw

    def body(dh_ref, dh16_ref, a_ref, wd_ref, wu_ref, h_ref, g_ref, *rest):
        da_ref, out_ref, out16_ref, dg_ref = rest[len(deps):]
        d16 = dh16_ref[...]
        acc = None
        for ch in range(N_CHIPS):
            cols = slice(ch * cw, (ch + 1) * cw)
            da = (_dot_nt(d16, wd_ref[ch]) * (2.0 * jnp.maximum(a_ref[:, cols].astype(F32), 0.0))).astype(BF16)
            da_ref[:, cols] = da
            d = _dot_nt(da, wu_ref[ch])
            acc = d if acc is None else acc + d
        dh_c, dg = _rms_bwd(h_ref[...], g_ref[...], acc)
        out = dh_ref[...] + dh_c
        out_ref[...] = out
        out16_ref[...] = out.astype(BF16)

        @pl.when(pl.program_id(0) == 0)
        def _():
            dg_ref[...] = dg

        @pl.when(pl.program_id(0) > 0)
        def _():
            dg_ref[...] += dg

    row = pl.BlockSpec((tm, D), lambda i: (i, 0))
    wide = pl.BlockSpec((tm, F), lambda i: (i, 0))
    vec = pl.BlockSpec((1, D), lambda i: (0, 0))
    return pl.pallas_call(
        body, name=name, grid=(T // tm,),
        in_specs=[row, row, wide, _resident((N_CHIPS, None, cw, D), lambda i: (0, 0, 0, 0)),
                  _resident((N_CHIPS, None, D, cw), lambda i: (0, 0, 0, 0)), row, vec] + [ANY] * len(deps),
        out_specs=[wide, row, row, vec],
        out_shape=[jax.ShapeDtypeStruct((T, F), BF16), jax.ShapeDtypeStruct((T, D), F32),
                   jax.ShapeDtypeStruct((T, D), BF16), jax.ShapeDtypeStruct((1, D), F32)],
        compiler_params=_params(1))(dh, dh16, a, wdown, wup, h_mid, g, *deps)


CONV_ROWS = 256
CONV_HALO = 16


def _conv_shifted(ext, k, r0, rows):
    rolled = pltpu.roll(ext, k, 0)[CONV_HALO:]
    t = r0 + lax.broadcasted_iota(jnp.int32, rolled.shape, 0)
    return jnp.where(t >= k, rolled, 0.0)


def _conv_ahead(ext, k, r0, rows, S):
    rolled = pltpu.roll(ext, rows + CONV_HALO - k, 0)[:rows]
    t = r0 + lax.broadcasted_iota(jnp.int32, rolled.shape, 0)
    return jnp.where(t + k < S, rolled, 0.0)


def _conv_fwd(name, bcu, cwg, layer, tc):
    _, B, S, D = bcu.shape
    cwc = cwg.shape[3]
    per_chunk = cwc // tc
    R = min(CONV_ROWS, S)

    def body(x_ref, w_ref, z_ref):
        w = [w_ref[k:k + 1, :] for k in range(3)]

        def step(i, carry):
            r0 = pl.multiple_of(i * R, R)
            h0 = pl.multiple_of(jnp.maximum(r0 - CONV_HALO, 0), CONV_HALO)
            ld = lambda p, start, rows: x_ref[p, pl.ds(start, rows), :].astype(F32)
            cu = jnp.concatenate([ld(1, h0, CONV_HALO) * ld(2, h0, CONV_HALO), ld(1, r0, R) * ld(2, r0, R)], axis=0)
            conv = w[0] * cu[CONV_HALO:]
            conv = conv + w[1] * _conv_shifted(cu, 1, r0, R)
            conv = conv + w[2] * _conv_shifted(cu, 2, r0, R)
            z_ref[pl.ds(r0, R), :] = (ld(0, r0, R) * conv).astype(BF16)
            return carry

        lax.fori_loop(0, S // R, step, 0)

    return pl.pallas_call(
        body, name=name, grid=(B, D // tc),
        in_specs=[pl.BlockSpec((3, None, S, tc), lambda b, j: (0, b, 0, j)),
                  pl.BlockSpec((None, None, 3, tc), lambda b, j: (j // per_chunk, layer, 0, j % per_chunk))],
        out_specs=pl.BlockSpec((None, S, tc), lambda b, j: (b, 0, j)),
        out_shape=jax.ShapeDtypeStruct((B, S, D), BF16),
        compiler_params=_params(2))(bcu, cwg)


def _conv_bwd(name, bcu, dz, cwg, layer, tc):
    _, B, S, D = bcu.shape
    cwc = cwg.shape[3]
    per_chunk = cwc // tc
    R = min(CONV_ROWS, S)

    def body(x_ref, dz_ref, w_ref, d_ref, dw_ref):
        w = [w_ref[k:k + 1, :] for k in range(3)]

        @pl.when(pl.program_id(1) == 0)
        def _():
            dw_ref[...] = jnp.zeros_like(dw_ref)

        def step(i, carry):
            r0 = pl.multiple_of(i * R, R)
            h0 = pl.multiple_of(jnp.maximum(r0 - CONV_HALO, 0), CONV_HALO)
            a0 = pl.multiple_of(jnp.minimum(r0 + R, S - CONV_HALO), CONV_HALO)
            ld = lambda p, start, rows: x_ref[p, pl.ds(start, rows), :].astype(F32)
            b, c, u = ld(0, r0, R), ld(1, r0, R), ld(2, r0, R)
            dz = dz_ref[pl.ds(r0, R), :]
            cu = jnp.concatenate([ld(1, h0, CONV_HALO) * ld(2, h0, CONV_HALO), c * u], axis=0)
            cu1 = _conv_shifted(cu, 1, r0, R)
            cu2 = _conv_shifted(cu, 2, r0, R)
            conv = w[0] * (c * u) + w[1] * cu1 + w[2] * cu2
            dconv = dz * b
            dca = jnp.concatenate([dconv, dz_ref[pl.ds(a0, CONV_HALO), :] * ld(0, a0, CONV_HALO)], axis=0)
            dcu = w[0] * dconv + w[1] * _conv_ahead(dca, 1, r0, R, S) + w[2] * _conv_ahead(dca, 2, r0, R, S)
            d_ref[0, pl.ds(r0, R), :] = (dz * conv).astype(BF16)
            d_ref[1, pl.ds(r0, R), :] = (dcu * u).astype(BF16)
            d_ref[2, pl.ds(r0, R), :] = (dcu * c).astype(BF16)
            return (carry[0] + jnp.sum(dconv * (c * u), axis=0, keepdims=True),
                    carry[1] + jnp.sum(dconv * cu1, axis=0, keepdims=True),
                    carry[2] + jnp.sum(dconv * cu2, axis=0, keepdims=True))

        zero = jnp.zeros((1, tc), F32)
        s0, s1, s2 = lax.fori_loop(0, S // R, step, (zero, zero, zero))
        for k, sk in enumerate((s0, s1, s2)):
            dw_ref[k:k + 1, :] += sk

    return pl.pallas_call(
        body, name=name, grid=(D // tc, B),
        in_specs=[pl.BlockSpec((3, None, S, tc), lambda j, b: (0, b, 0, j)),
                  pl.BlockSpec((None, S, tc), lambda j, b: (b, 0, j)),
                  pl.BlockSpec((None, None, 3, tc), lambda j, b: (j // per_chunk, layer, 0, j % per_chunk))],
        out_specs=[pl.BlockSpec((3, None, S, tc), lambda j, b: (0, b, 0, j)),
                   pl.BlockSpec((3, tc), lambda j, b: (0, j))],
        out_shape=[jax.ShapeDtypeStruct((3, B, S, D), BF16), jax.ShapeDtypeStruct((3, D), F32)],
        compiler_params=_params(2))(bcu, dz, cwg)


def _att_rows(dil, idx, nb):
    r, n = idx // nb, idx % nb
    if dil == 1:
        cur = pl.ds(pl.multiple_of(n * ATT_BLK, ATT_BLK), ATT_BLK)
        prev = pl.ds(pl.multiple_of(jnp.maximum(n - 1, 0) * ATT_BLK, ATT_BLK), ATT_BLK)
    else:
        cur = pl.ds(n * (ATT_BLK * dil) + r, ATT_BLK, stride=dil)
        prev = pl.ds(jnp.maximum(n - 1, 0) * (ATT_BLK * dil) + r, ATT_BLK, stride=dil)
    return n, cur, prev


def _att_bias(bias_ref, dil, sl_ref, hp):
    row = lax.broadcasted_iota(jnp.int32, (2 * ATT_BLK, 2 * ATT_BLK), 0)
    ci = lax.broadcasted_iota(jnp.int32, (2 * ATT_BLK, 2 * ATT_BLK), 1)
    j = ATT_BLK + (row & (ATT_BLK - 1)) - ci
    slope = jnp.where(row < ATT_BLK, sl_ref[2 * hp], sl_ref[2 * hp + 1])
    rest = jnp.where((j >= 0) & (j <= ATT_BLK), -slope * (dil * j).astype(F32), NEG_INF)
    bias_ref[1] = rest
    bias_ref[0] = jnp.where(ci >= ATT_BLK, rest, NEG_INF)


def _stack_heads(x16, lane):
    first = lane < HEAD_DIM
    return jnp.concatenate([jnp.where(first, x16, jnp.zeros_like(x16)),
                            jnp.where(first, jnp.zeros_like(x16), x16)], axis=0)


def _per_head(col, lane):
    return jnp.where(lane < HEAD_DIM, col[:ATT_BLK], col[ATT_BLK:])


def _attn_fwd(name, q, kv, slopes, n_heads):
    B, S, CQ = q.shape
    HP = n_heads * HEAD_DIM // LANES
    scale = HEAD_DIM ** -0.5
    n_groups = len(PATTERNS)
    CH = 256

    def body(sl_ref, q_ref, k_ref, v_ref, o_ref, lse_ref, bias_ref, *parts):
        og, lg = parts[:n_groups], parts[n_groups:]
        hp, g = pl.program_id(1), pl.program_id(2)
        lane = lax.broadcasted_iota(jnp.int32, (1, LANES), 1)

        for gi, (window, dil) in enumerate(PATTERNS):
            nb = S // dil // ATT_BLK

            @pl.when(g == gi)
            def _(gi=gi, dil=dil, nb=nb):
                _att_bias(bias_ref, dil, sl_ref, hp)

                def step(idx, carry):
                    n, cur, prev = _att_rows(dil, idx, nb)
                    qs = _stack_heads((q_ref[cur, :] * scale).astype(BF16), lane)
                    kc = jnp.concatenate([k_ref[prev, :], k_ref[cur, :]], axis=0).astype(BF16)
                    vc = jnp.concatenate([v_ref[prev, :], v_ref[cur, :]], axis=0).astype(BF16)
                    s = _dot_nt(qs, kc) + bias_ref[jnp.minimum(n, 1)]
                    m = jnp.max(s, axis=-1, keepdims=True)
                    p = jnp.exp(s - m)
                    l = jnp.sum(p, axis=-1, keepdims=True)
                    p16 = p.astype(BF16)
                    o_un = _dot(jnp.concatenate([p16[:ATT_BLK], p16[ATT_BLK:]], axis=1), _stack_heads_rows(vc, lane))
                    og[gi][cur, :] = o_un / _per_head(l, lane)
                    lg[gi][cur, :] = _per_head(m + jnp.log(l), lane)
                    return carry

                lax.fori_loop(0, S // ATT_BLK, step, 0, unroll=16)

        @pl.when(g == n_groups - 1)
        def _():
            def comb(i, carry):
                rows = pl.ds(pl.multiple_of(i * CH, CH), CH)
                a, b, c = lg[0][rows, :], lg[1][rows, :], lg[2][rows, :]
                m = jnp.maximum(jnp.maximum(a, b), c)
                ea, eb, ec = jnp.exp(a - m), jnp.exp(b - m), jnp.exp(c - m)
                z = ea + eb + ec
                o_ref[rows, :] = (ea / z) * og[0][rows, :] + (eb / z) * og[1][rows, :] + (ec / z) * og[2][rows, :]
                lse_ref[rows, :] = m + jnp.log(z)
                return carry

            lax.fori_loop(0, S // CH, comb, 0)

    blk = (None, S, LANES)
    out = pl.BlockSpec(blk, lambda b, hp, g: (b, 0, hp))
    return pl.pallas_call(
        body, name=name, grid=(B, HP, n_groups),
        in_specs=[pl.BlockSpec(memory_space=pltpu.SMEM),
                  pl.BlockSpec(blk, lambda b, hp, g: (b, 0, g * HP + hp)),
                  pl.BlockSpec(blk, lambda b, hp, g: (b, 0, g * 2 * HP + hp)),
                  pl.BlockSpec(blk, lambda b, hp, g: (b, 0, g * 2 * HP + HP + hp))],
        out_specs=[out, out],
        out_shape=[jax.ShapeDtypeStruct((B, S, HP * LANES), F32)] * 2,
        scratch_shapes=[pltpu.VMEM((2, 2 * ATT_BLK, 2 * ATT_BLK), F32)] + [pltpu.VMEM((S, LANES), F32)] * (2 * n_groups),
        compiler_params=_params(3))(slopes, q, kv, kv)


def _stack_heads_rows(x16, lane):
    first = lane < HEAD_DIM
    return jnp.concatenate([jnp.where(first, x16, jnp.zeros_like(x16)),
                            jnp.where(first, jnp.zeros_like(x16), x16)], axis=0)


def _attn_bwd(name, q, kv, slopes, o, lse, do, n_heads, dkv_prev):
    B, S, CQ = q.shape
    HP = n_heads * HEAD_DIM // LANES
    scale = HEAD_DIM ** -0.5
    n_groups = len(PATTERNS)
    n_prev = 0 if dkv_prev is None else 2

    def body(sl_ref, q_ref, k_ref, v_ref, o_ref, lse_ref, do_ref, *rest):
        dq_ref, dk_ref, dv_ref, bias_ref = rest[n_prev:]
        hp, g = pl.program_id(1), pl.program_id(2)
        lane = lax.broadcasted_iota(jnp.int32, (1, LANES), 1)
        first = lane < HEAD_DIM

        def flush(rows, dk, dv):
            if n_prev:
                dk = dk + rest[0][rows, :]
                dv = dv + rest[1][rows, :]
            dk_ref[rows, :] = dk
            dv_ref[rows, :] = dv

        for gi, (window, dil) in enumerate(PATTERNS):
            nb = S // dil // ATT_BLK
            n_blocks = S // ATT_BLK

            @pl.when(g == gi)
            def _(dil=dil, nb=nb, n_blocks=n_blocks):
                _att_bias(bias_ref, dil, sl_ref, hp)

                def block(idx, carry, first_of_all):
                    n, cur, prev = _att_rows(dil, idx, nb)
                    qs = _stack_heads((q_ref[cur, :] * scale).astype(BF16), lane)
                    kc = jnp.concatenate([k_ref[prev, :], k_ref[cur, :]], axis=0).astype(BF16)
                    vc = jnp.concatenate([v_ref[prev, :], v_ref[cur, :]], axis=0).astype(BF16)
                    dob = do_ref[cur, :]
                    prod = dob * o_ref[cur, :]
                    lseb = lse_ref[cur, :]
                    dos = _stack_heads(dob.astype(BF16), lane)
                    delta = jnp.concatenate(
                        [jnp.sum(jnp.where(first, prod, 0.0), axis=-1, keepdims=True),
                         jnp.sum(jnp.where(first, 0.0, prod), axis=-1, keepdims=True)], axis=0)
                    lse_col = jnp.concatenate(
                        [jnp.max(jnp.where(first, lseb, -jnp.inf), axis=-1, keepdims=True),
                         jnp.max(jnp.where(first, -jnp.inf, lseb), axis=-1, keepdims=True)], axis=0)
                    s = _dot_nt(qs, kc) + bias_ref[jnp.minimum(n, 1)]
                    p = jnp.exp(s - lse_col)
                    ds = p * (_dot_nt(dos, vc) - delta)
                    ds16 = ds.astype(BF16)
                    dq = _dot(jnp.concatenate([ds16[:ATT_BLK], ds16[ATT_BLK:]], axis=1), _stack_heads_rows(kc, lane))
                    dq_ref[cur, :] = dq * scale
                    dk = _dot_tn(ds16, qs)
                    dv = _dot_tn(p.astype(BF16), dos)

                    def flush_before():
                        _, before, _ = _att_rows(dil, idx - 1, nb)
                        flush(before, carry[0] + dk[:ATT_BLK], carry[1] + dv[:ATT_BLK])

                    if first_of_all:
                        pl.when(idx > 0)(flush_before)
                    else:
                        flush_before()
                    return dk[ATT_BLK:], dv[ATT_BLK:]

                def step(i, carry):
                    for u in range(BWD_UNROLL):
                        carry = block(i * BWD_UNROLL + u, carry, u == 0)
                    return carry

                zero = jnp.zeros((ATT_BLK, LANES), F32)
                dk_last, dv_last = lax.fori_loop(0, n_blocks // BWD_UNROLL, step, (zero, zero))
                _, last, _ = _att_rows(dil, n_blocks - 1, nb)
                flush(last, dk_last, dv_last)

    blk = (None, S, LANES)
    shared = pl.BlockSpec(blk, lambda b, hp, g: (b, 0, hp))
    grouped = pl.BlockSpec(blk, lambda b, hp, g: (b, 0, g * HP + hp))
    prev = [] if dkv_prev is None else list(dkv_prev)
    gshape = jax.ShapeDtypeStruct((B, S, n_groups * HP * LANES), F32)
    return pl.pallas_call(
        body, name=name, grid=(B, HP, n_groups),
        in_specs=[pl.BlockSpec(memory_space=pltpu.SMEM), grouped,
                  pl.BlockSpec(blk, lambda b, hp, g: (b, 0, g * 2 * HP + hp)),
                  pl.BlockSpec(blk, lambda b, hp, g: (b, 0, g * 2 * HP + HP + hp)),
                  shared, shared, shared] + [grouped] * n_prev,
        out_specs=[grouped] * 3, out_shape=[gshape] * 3,
        scratch_shapes=[pltpu.VMEM((2, 2 * ATT_BLK, 2 * ATT_BLK), F32)],
        compiler_params=_params(3))(slopes, q, kv, kv, o, lse, do, *prev)


def _final_loss(name, h, g, target, tm):
    T, D = h.shape

    def body(h_ref, g_ref, t_ref, loss_ref, dh_ref, dh16_ref, dg_ref):
        hf = h_ref[...]
        gv = g_ref[...]
        rstd = lax.rsqrt(jnp.mean(hf * hf, axis=-1, keepdims=True) + EPS)
        xhat = hf * rstd
        err = xhat * gv - t_ref[...]
        part = 0.5 * jnp.sum(jnp.mean(err * err, axis=-1, keepdims=True), axis=0, keepdims=True)
        dy = err * (1.0 / D)
        dg = jnp.sum(dy * xhat, axis=0, keepdims=True)
        dx = dy * gv
        dh = rstd * (dx - xhat * jnp.mean(dx * xhat, axis=-1, keepdims=True))
        dh_ref[...] = dh
        dh16_ref[...] = dh.astype(BF16)

        @pl.when(pl.program_id(0) == 0)
        def _():
            loss_ref[...] = part
            dg_ref[...] = dg

        @pl.when(pl.program_id(0) > 0)
        def _():
            loss_ref[...] += part
            dg_ref[...] += dg

    return pl.pallas_call(
        body, name=name, grid=(T // tm,),
        in_specs=[pl.BlockSpec((tm, D), lambda i: (i, 0)), pl.BlockSpec((1, D), lambda i: (0, 0)),
                  pl.BlockSpec((tm, D), lambda i: (i, 0))],
        out_specs=[pl.BlockSpec((1, 1), lambda i: (0, 0)), pl.BlockSpec((tm, D), lambda i: (i, 0)),
                   pl.BlockSpec((tm, D), lambda i: (i, 0)), pl.BlockSpec((1, D), lambda i: (0, 0))],
        out_shape=[jax.ShapeDtypeStruct((1, 1), F32), jax.ShapeDtypeStruct((T, D), F32),
                   jax.ShapeDtypeStruct((T, D), BF16), jax.ShapeDtypeStruct((1, D), F32)],
        compiler_params=_params(1))(h, g, target)


def _nt_rows(name, dh, wg, layer, a_mul, out_dtype, tm, deps=()):
    T, D = dh.shape
    rk = wg.shape[2]
    N = N_CHIPS * rk
    with_a = a_mul is not None

    def body(dh_ref, w_ref, *rest):
        o_ref = rest[-1]
        d16 = dh_ref[...]
        for ch in range(N_CHIPS):
            r = _dot_nt(d16, w_ref[ch])
            if with_a:
                r = r * (2.0 * jnp.maximum(rest[0][:, ch * rk:(ch + 1) * rk].astype(F32), 0.0))
            o_ref[:, ch * rk:(ch + 1) * rk] = r.astype(out_dtype)

    in_specs = [pl.BlockSpec((tm, D), lambda i: (i, 0)),
                pl.BlockSpec((N_CHIPS, None, rk, D), lambda i: (0, layer, 0, 0))]
    args = [dh, wg]
    if with_a:
        in_specs.append(pl.BlockSpec((tm, N), lambda i: (i, 0)))
        args.append(a_mul)
    in_specs += [ANY] * len(deps)
    args += list(deps)
    return pl.pallas_call(
        body, name=name, grid=(T // tm,), in_specs=in_specs,
        out_specs=pl.BlockSpec((tm, N), lambda i: (i, 0)),
        out_shape=jax.ShapeDtypeStruct((T, N), out_dtype),
        compiler_params=_params(1))(*args)


def _nt_cols(name, ysegs, wg, layer, tm, norm):
    Nw, cw = wg.shape[2], wg.shape[3]
    widths = [bs[-1] for _, bs, _ in ysegs]
    pieces = _pieces(widths, cw, 1024)
    ns = len(ysegs)
    T = norm[0].shape[0] if norm is not None else ysegs[0][0].shape[-2]

    def body(*refs):
        y_refs = refs[:ns]
        w_ref = refs[ns]
        acc = refs[-1]
        for n, (s, a0, ch, b0, wd) in enumerate(pieces):
            d = _dot_nt(y_refs[s][:, a0:a0 + wd].astype(BF16), w_ref[ch, :, b0:b0 + wd])
            if n == 0:
                acc[...] = d
            else:
                acc[...] += d
        if norm is None:
            refs[ns + 1][...] = acc[...]
        else:
            h_ref, g_ref, dhin_ref, out_ref, out16_ref, dg_ref = refs[ns + 1:ns + 7]
            dh_c, dg = _rms_bwd(h_ref[...], g_ref[...], acc[...])
            dh = dhin_ref[...] + dh_c
            out_ref[...] = dh
            out16_ref[...] = dh.astype(BF16)

            @pl.when(pl.program_id(0) == 0)
            def _():
                dg_ref[...] = dg

            @pl.when(pl.program_id(0) > 0)
            def _():
                dg_ref[...] += dg

    in_specs = [pl.BlockSpec(bs, im) for _, bs, im in ysegs]
    in_specs.append(pl.BlockSpec((N_CHIPS, None, Nw, cw), lambda i: (0, layer, 0, 0)))
    args = [a for a, _, _ in ysegs] + [wg]
    row = pl.BlockSpec((tm, Nw), lambda i: (i, 0))
    vec = pl.BlockSpec((1, Nw), lambda i: (0, 0))
    if norm is None:
        out_specs = row
        out_shape = jax.ShapeDtypeStruct((T, Nw), F32)
    else:
        in_specs += [row, vec, row]
        args += list(norm)
        out_specs = [row, row, vec]
        out_shape = [jax.ShapeDtypeStruct((T, Nw), F32), jax.ShapeDtypeStruct((T, Nw), BF16),
                     jax.ShapeDtypeStruct((1, Nw), F32)]
    return pl.pallas_call(
        body, name=name, grid=(T // tm,), in_specs=in_specs, out_specs=out_specs, out_shape=out_shape,
        scratch_shapes=[pltpu.VMEM((tm, Nw), F32)], compiler_params=_params(1))(*args)


def _tn(name, x, x_act, ysegs, cw, cols_layout, tmm, tt, deps=(), out_dtype=F32):
    T, M = x.shape
    widths = [bs[-1] for _, bs, _ in ysegs]
    N = sum(widths)
    pieces = _pieces(widths, cw if cols_layout else N, 1024)
    ns = len(ysegs)
    n_t = T // tt
    block = (N_CHIPS, tmm, cw) if cols_layout else (tmm, N)
    narrow = out_dtype != F32

    def body(x_ref, *refs):
        y_refs = refs[:ns]
        o_ref = refs[ns + len(deps)]
        acc = refs[-1] if narrow else o_ref

        @pl.when(pl.program_id(1) == 0)
        def _():
            acc[...] = jnp.zeros_like(acc)

        xt = x_act(x_ref[...])
        for s, a0, ch, b0, wd in pieces:
            d = _dot_tn(xt, y_refs[s][:, a0:a0 + wd].astype(BF16))
            if cols_layout:
                acc[ch, :, b0:b0 + wd] += d
            else:
                acc[:, b0:b0 + wd] += d
        if narrow:
            @pl.when(pl.program_id(1) == n_t - 1)
            def _():
                o_ref[...] = acc[...].astype(out_dtype)

    in_specs = [pl.BlockSpec((tt, tmm), lambda m, t: (t, m))] + [pl.BlockSpec(bs, im) for _, bs, im in ysegs]
    in_specs += [ANY] * len(deps)
    if cols_layout:
        out_specs = pl.BlockSpec(block, lambda m, t: (0, m, 0))
        out_shape = jax.ShapeDtypeStruct((N_CHIPS, M, cw), out_dtype)
    else:
        out_specs = pl.BlockSpec(block, lambda m, t: (m, 0))
        out_shape = jax.ShapeDtypeStruct((M, N), out_dtype)
    return pl.pallas_call(
        body, name=name, grid=(M // tmm, n_t), in_specs=in_specs, out_specs=out_specs, out_shape=out_shape,
        scratch_shapes=[pltpu.VMEM(block, F32)] if narrow else [],
        compiler_params=_params(2))(x, *[a for a, _, _ in ysegs], *deps)


def _seg2d(a, t_rows, grid_rank):
    w = a.shape[1]
    if grid_rank == 1:
        return (a, (t_rows, w), lambda i: (i, 0))
    return (a, (t_rows, w), lambda m, t: (t, 0))


def _kv_segments(dk, dv, C, t_rows, grid_rank):
    segs = []
    for g in range(len(PATTERNS)):
        for a in (dk, dv):
            if grid_rank == 1:
                segs.append((a, (t_rows, C), lambda i, g=g: (i, g)))
            else:
                segs.append((a, (t_rows, C), lambda m, t, g=g: (t, g)))
    return segs


def _seg_plane(a, plane, t_rows, grid_rank):
    w = a.shape[2]
    if grid_rank == 1:
        return (a, (None, t_rows, w), lambda i: (plane, i, 0))
    return (a, (None, t_rows, w), lambda m, t: (plane, t, 0))


def _row_tile(rows, row_bytes, budget_bytes=2 * 1024 * 1024):
    t = rows
    while t * row_bytes > budget_bytes and t % 32 == 0:
        t //= 2
    return t


N_DEVICES = 8


def _device_add(name, own, slots, place):
    _, _, hr, c = own.shape
    tr = _row_tile(hr, c * 4, 1024 * 1024)

    def body(place_ref, own_ref, *refs):
        o_ref = refs[-1]
        acc = own_ref[...].astype(F32)
        for r in refs[:-1]:
            acc = acc + r[...].astype(F32)
        o_ref[...] = acc

    def slot(k):
        return pl.BlockSpec((None, tr, c), lambda i, pr: ((2 * pr[0] + pr[1] + k) % N_DEVICES, i, 0))

    grid_spec = pltpu.PrefetchScalarGridSpec(
        num_scalar_prefetch=1, grid=(hr // tr,),
        in_specs=[pl.BlockSpec((None, None, tr, c), lambda i, pr: (pr[0], pr[1], i, 0))]
        + [slot(k) for k in range(1, N_DEVICES)],
        out_specs=pl.BlockSpec((None, tr, c), lambda i, pr: (pr[1], i, 0)))
    return pl.pallas_call(body, name=name, grid_spec=grid_spec,
                          out_shape=jax.ShapeDtypeStruct((2, hr, c), F32),
                          compiler_params=_params(1))(place, own, *[slots] * (N_DEVICES - 1))


def _adamw(name, w, g, m, v):
    rows, cols = w.shape
    tr = _row_tile(rows, cols * 4, 1024 * 1024)

    def body(w_ref, g_ref, m_ref, v_ref, d_ref, nm_ref, nv_ref):
        d_ref[...], nm_ref[...], nv_ref[...] = _adamw_math(w_ref[...], g_ref[...], m_ref[...], v_ref[...])

    spec = pl.BlockSpec((tr, cols), lambda i: (i, 0))
    return pl.pallas_call(
        body, name=name, grid=(rows // tr,), in_specs=[spec] * 4, out_specs=[spec] * 3,
        out_shape=[jax.ShapeDtypeStruct((rows, cols), F32)] * 3, compiler_params=_params(1))(w, g, m, v)


def _adamw_math(w, g, m, v):
    nm = ADAM_B1 * m + (1.0 - ADAM_B1) * g
    nv = ADAM_B2 * v + (1.0 - ADAM_B2) * jnp.square(g)
    m_hat = nm / (1.0 - ADAM_B1 ** ADAM_STEP)
    v_hat = nv / (1.0 - ADAM_B2 ** ADAM_STEP)
    return -ADAM_LR * (m_hat / (jnp.sqrt(v_hat) + ADAM_EPS) + ADAM_WD * w), nm, nv


def _adamw_layers(name, w, grads, m, v):
    L, r, c = w.shape
    tr = _row_tile(r, L * c * 4, 1024 * 1024)

    def body(*refs):
        w_ref, m_ref, v_ref = refs[:3]
        g_refs = refs[3:3 + L]
        go_ref, d_ref, nm_ref, nv_ref = refs[3 + L:]
        for l in range(L):
            g = g_refs[l][...]
            go_ref[l] = g
            d_ref[l], nm_ref[l], nv_ref[l] = _adamw_math(w_ref[l], g, m_ref[l], v_ref[l])

    stacked = pl.BlockSpec((L, tr, c), lambda i: (0, i, 0))
    return pl.pallas_call(
        body, name=name, grid=(r // tr,),
        in_specs=[stacked] * 3 + [pl.BlockSpec((tr, c), lambda i: (i, 0))] * L, out_specs=[stacked] * 4,
        out_shape=[jax.ShapeDtypeStruct((L, r, c), F32)] * 4, compiler_params=_params(1))(w, m, v, *grads)


def _place():
    x, y, c = lax.axis_index("x"), lax.axis_index("y"), lax.axis_index("c")
    chips = [(1 - x, y), (x, 1 - y), (1 - x, 1 - y)]
    return x, y, c, chips


HBM = pl.BlockSpec(memory_space=pltpu.HBM)
SEM = pl.BlockSpec(memory_space=pltpu.SEMAPHORE)
EFFECT = pltpu.SideEffectType.DATAFLOW_SIDE_EFFECTING


class _Copy:
    def __init__(self, src, src_view, land, dst_view, recv_view, target):
        self.src, self.src_view, self.land, self.dst_view, self.recv_view, self.target = (
            src, src_view, land, dst_view, recv_view, target)


def _whole(ref, place):
    return ref


def _split_start(name, srcs, land_shapes, plans):
    skeys, lkeys = list(srcs), list(land_shapes)
    ns, nl, ng = len(skeys), len(lkeys), len(plans)

    def body(*refs):
        src = dict(zip(skeys, refs[:ns]))
        land = dict(zip(lkeys, refs[ns:ns + nl]))
        sems = refs[ns + nl:ns + nl + 2 * ng]
        token = refs[-1]
        place = _place()
        for gi, plan in enumerate(plans):
            for k, cp in enumerate(plan):
                dst = land[cp.land] if cp.land in land else src[cp.land]
                pltpu.make_async_remote_copy(
                    src_ref=cp.src_view(src[cp.src], place), dst_ref=cp.dst_view(dst, place),
                    send_sem=sems[2 * gi].at[k], recv_sem=sems[2 * gi + 1].at[k],
                    device_id=cp.target(place), device_id_type=MESH).start()
        token[...] = jnp.zeros_like(token)

    sem_shapes = []
    for plan in plans:
        sem_shapes += [pltpu.SemaphoreType.DMA((len(plan),))] * 2
    buffers = [srcs[k] for k in skeys] + [lax.empty(land_shapes[k].shape, land_shapes[k].dtype) for k in lkeys]
    outs = pl.pallas_call(
        body, name=name,
        out_shape=(*sem_shapes, *[pltpu.HBM(a.shape, a.dtype) for a in buffers], jax.ShapeDtypeStruct((8, LANES), F32)),
        in_specs=[HBM] * (ns + nl),
        out_specs=(*[SEM] * (2 * ng), *[HBM] * (ns + nl), pl.BlockSpec(memory_space=pltpu.VMEM)),
        input_output_aliases={i: 2 * ng + i for i in range(ns + nl)},
        compiler_params=pltpu.CompilerParams(has_side_effects=EFFECT),
    )(*[pltpu.with_memory_space_constraint(a, pltpu.HBM) for a in buffers])
    sems = [(outs[2 * gi], outs[2 * gi + 1]) for gi in range(ng)]
    thru = outs[2 * ng:2 * ng + ns + nl]
    return sems, dict(zip(skeys, thru[:ns])), dict(zip(lkeys, thru[ns:])), outs[-1]


def _split_wait(name, sems, srcs, lands, plan, after):
    skeys, lkeys = list(srcs), list(lands)
    ns, nl = len(skeys), len(lkeys)

    def body(*refs):
        src = dict(zip(skeys, refs[:ns]))
        land = dict(zip(lkeys, refs[ns:ns + nl]))
        ssem, rsem = refs[ns + nl], refs[ns + nl + 1]
        place = _place()
        for k, cp in enumerate(plan):
            dst = land[cp.land] if cp.land in land else src[cp.land]
            pltpu.make_async_remote_copy(
                src_ref=cp.src_view(src[cp.src], place), dst_ref=cp.dst_view(dst, place),
                send_sem=ssem.at[k], recv_sem=rsem.at[k],
                device_id=cp.target(place), device_id_type=MESH).wait_send()
            got = cp.recv_view(dst, place)
            pltpu.make_async_remote_copy(
                src_ref=got, dst_ref=got, send_sem=ssem.at[k], recv_sem=rsem.at[k],
                device_id=cp.target(place), device_id_type=MESH).wait_recv()

    buffers = [srcs[k] for k in skeys] + [lands[k] for k in lkeys]
    outs = pl.pallas_call(
        body, name=name, out_shape=tuple(pltpu.HBM(a.shape, a.dtype) for a in buffers),
        in_specs=(*[HBM] * (ns + nl), SEM, SEM, ANY), out_specs=tuple([HBM] * (ns + nl)),
        input_output_aliases={i: i for i in range(ns + nl)},
        compiler_params=pltpu.CompilerParams(has_side_effects=EFFECT),
    )(*buffers, sems[0], sems[1], after)
    return dict(zip(skeys, outs[:ns])), dict(zip(lkeys, outs[ns:]))


def _chip_of(place):
    x, y, c, chips = place
    return 2 * x + y


GATHER_FIRST = 2


class _WeightGather:
    def __init__(self, blocks):
        self.plans, shapes = {}, {}
        for key, a in blocks.items():
            shapes[key] = jax.ShapeDtypeStruct((N_CHIPS,) + a.shape, a.dtype)
            slot = lambda ref, place: ref.at[_chip_of(place)]
            plan = [_Copy(key, _whole, key, slot,
                          lambda ref, place, k=k: ref.at[2 * place[3][k][0] + place[3][k][1]],
                          lambda place, k=k: (place[3][k][0], place[3][k][1], place[2])) for k in range(3)]
            plan.append(_Copy(key, _whole, key, slot, slot, lambda place: (place[0], place[1], 1 - place[2])))
            self.plans[key] = plan
        keys = list(blocks)
        self.sems, self.srcs, self.lands = {}, {}, {}
        for name, part in (("gather_start_first", keys[:GATHER_FIRST]), ("gather_start", keys[GATHER_FIRST:])):
            sems, srcs, lands, self.token = _split_start(name, {k: blocks[k] for k in part}, {k: shapes[k] for k in part},
                                                         [self.plans[k] for k in part])
            self.sems.update(zip(part, sems))
            self.srcs.update(srcs)
            self.lands.update(lands)

    def get(self, l, name, after):
        key = (l, name)
        _, lands = _split_wait(f"gather_wait_{name}{l}", self.sems[key], {key: self.srcs[key]},
                               {key: self.lands[key]}, self.plans[key], after)
        return lands[key][:, None]


class _GradReduce:
    def __init__(self, place):
        self.place = place
        self.jobs = []
        self.done = {}
        self.n = 0

    def submit(self, grads):
        views = {k: a.reshape(N_CHIPS, 2, a.shape[1] // 2, a.shape[2]) for k, a in grads.items()}
        shapes = {k: jax.ShapeDtypeStruct((N_DEVICES,) + a.shape[2:], a.dtype) for k, a in views.items()}

        def peer(place, k):
            x, y, c, _ = place
            return (1 - x if k & 4 else x, 1 - y if k & 2 else y, 1 - c if k & 1 else c)

        def index(dev):
            return 4 * dev[0] + 2 * dev[1] + dev[2]

        plan = []
        for key in views:
            for k in range(1, N_DEVICES):
                plan.append(_Copy(
                    key, lambda ref, place, k=k: ref.at[2 * peer(place, k)[0] + peer(place, k)[1], peer(place, k)[2]],
                    key, lambda ref, place: ref.at[index(place[:3])],
                    lambda ref, place, k=k: ref.at[index(peer(place, k))],
                    lambda place, k=k: peer(place, k)))
        sems, srcs, lands, token = _split_start(f"grad_start{self.n}", views, shapes, [plan])
        self.jobs.append(dict(id=self.n, sems=sems[0], srcs=srcs, lands=lands, plan=plan))
        self.n += 1
        return token

    def pump(self, after):
        return []

    def finish(self, after):
        for job in self.jobs:
            srcs, lands = _split_wait(f"grad_wait{job['id']}", job["sems"], job["srcs"], job["lands"], job["plan"],
                                      after)
            for i, k in enumerate(srcs):
                self.done[k] = _device_add(f"grad_add{job['id']}_{i}", srcs[k], lands[k], self.place)
        self.jobs = []
        return self.done


class _PairShare:
    def __init__(self, halves, types):
        sibling = lambda place: (place[0], place[1], 1 - place[2])
        mine = lambda ref, place: ref.at[place[2]]
        theirs = lambda ref, place: ref.at[1 - place[2]]
        self.plans = {t: [_Copy(k, mine, k, mine, theirs, sibling) for k in halves if k[0] == t] for t in types}
        sems, self.bufs, _, self.token = _split_start("share_start", halves, {}, list(self.plans.values()))
        self.sems = dict(zip(self.plans, sems))

    def get(self, t, after):
        keys = [cp.src for cp in self.plans[t]]
        bufs, _ = _split_wait(f"share_wait_{t}", self.sems[t], {k: self.bufs[k] for k in keys}, {}, self.plans[t], after)
        return bufs


def _small_allreduce(part):
    R, C = part.shape
    N_DEV = 8

    def body(in_ref, out_ref, slots, ssem, rsem):
        x, y, c, _ = _place()
        me = 4 * x + 2 * y + c
        sends = []
        for k in range(1, N_DEV):
            kx, ky, kc = (k >> 2) & 1, (k >> 1) & 1, k & 1
            peer = (1 - x if kx else x, 1 - y if ky else y, 1 - c if kc else c)
            cp = pltpu.make_async_remote_copy(
                src_ref=in_ref, dst_ref=slots.at[me], send_sem=ssem.at[k], recv_sem=rsem.at[k],
                device_id=peer, device_id_type=MESH)
            cp.start()
            sends.append(cp)
        slots[me] = in_ref[...]
        for k in range(1, N_DEV):
            kx, ky, kc = (k >> 2) & 1, (k >> 1) & 1, k & 1
            peer = (1 - x if kx else x, 1 - y if ky else y, 1 - c if kc else c)
            slot = slots.at[4 * peer[0] + 2 * peer[1] + peer[2]]
            pltpu.make_async_remote_copy(
                src_ref=slot, dst_ref=slot, send_sem=ssem.at[k], recv_sem=rsem.at[k],
                device_id=peer, device_id_type=MESH).wait_recv()
        acc = slots[0]
        for d in range(1, N_DEV):
            acc = acc + slots[d]
        out_ref[...] = acc
        for cp in sends:
            cp.wait_send()

    vm = pl.BlockSpec(memory_space=pltpu.VMEM)
    return pl.pallas_call(
        body, name="small_allreduce", in_specs=[vm], out_specs=vm,
        out_shape=jax.ShapeDtypeStruct((R, C), F32),
        scratch_shapes=[pltpu.VMEM((N_DEV, R, C), F32), pltpu.SemaphoreType.DMA((N_DEV,)),
                        pltpu.SemaphoreType.DMA((N_DEV,))])(part)


def _local_step(x, target, norm_mix, norm_mlp, norm_kv, norm_final, weights, sink, n_a, n_heads):
    B, S, D = x.shape
    T = B * S
    C = n_heads * HEAD_DIM
    depth = norm_mix.shape[0]
    slopes = 2.0 ** (-ALIBI_MAX_BIAS * jnp.arange(1, n_heads + 1, dtype=F32) / n_heads)
    tm = min(512, T)
    row = lambda v: v.reshape(1, -1)

    h = x.reshape(T, D)
    saved, Wl = [], []
    kv = nkv = h_kv = cwg = None
    for l in range(depth):
        s = {"h_in": h}
        w = {}
        Wl.append(w)
        if l < n_a:
            w["w_a_in"] = weights.get(l, "w_a_in", h)
            first = [weights.token] if l == 0 and hasattr(weights, "token") else []
            s["n1"], bcu = _norm_mm(f"a_in_fwd{l}", h, row(norm_mix[l]), w["w_a_in"], 0, 3, BF16, tm, first)
            s["bcu"] = bcu.reshape(3, B, S, D)
            if l == 0:
                cwg = weights.get(0, "conv", bcu)[:, 0, :n_a * 3].reshape(N_CHIPS, n_a, 3, -1)
            s["z"] = _conv_fwd(f"conv_fwd{l}", s["bcu"], cwg, l, LANES).reshape(T, D)
            w["w_a_out"] = weights.get(l, "w_a_out", s["z"])
            h = _mm_res_rows(f"a_out_fwd{l}", s["z"], w["w_a_out"], 0, h, _to_bf16, tm)
        else:
            i = l - n_a
            if i == 0:
                h_kv = h
                w["w_kv"] = weights.get(l, "w_kv", h)
                nkv, kv = _norm_mm("kv_fwd", h, row(norm_kv), w["w_kv"], 0, 1, F32, tm)
                kv = kv.reshape(B, S, 2 * 3 * C)
            w["w_q"] = weights.get(l, "w_q", h)
            s["n1"], q = _norm_mm(f"q_fwd{i}", h, row(norm_mix[l]), w["w_q"], 0, 1, F32, tm)
            s["q"] = q.reshape(B, S, 3 * C)
            o, lse = _attn_fwd(f"attn_fwd{i}", s["q"], kv, slopes, n_heads)
            s["o"], s["lse"] = o.reshape(T, C), lse.reshape(T, C)
            w["w_o"] = weights.get(l, "w_o", o)
            h = _mm_res_cols(f"o_fwd{i}", s["o"], w["w_o"], 0, h, tm)
        s["h_mid"] = h
        w["w_up"] = weights.get(l, "w_up", h)
        if l == 0:
            s["n2"], a = _norm_mm("up_fwd0", h, row(norm_mlp[l]), w["w_up"], 0, 1, BF16, tm)
            s["a"] = a[0]
            w["w_down"] = weights.get(l, "w_down", a)
            h = _mm_res_rows("down_fwd0", s["a"], w["w_down"], 0, h, _relu2_bf16, tm)
        else:
            w["w_down"] = weights.get(l, "w_down", h)
            s["n2"], s["a"], h = _mlp_fwd(f"mlp_fwd{l}", h, row(norm_mlp[l]), w["w_up"], w["w_down"], tm)
        F = s["a"].shape[1]
        saved.append(s)

    loss, dh, dh16, dg_final = _final_loss("loss_head", h, row(norm_final), target.reshape(T, D), tm)

    g_mix, g_mlp = [None] * depth, [None] * depth
    g_conv = [None] * n_a
    dkv = None
    tt = min(512, T)
    deps = []
    for l in reversed(range(depth)):
        s, w = saved[l], Wl[l]
        g_down = _tn(f"down_wgrad{l}", s["a"], _relu2_bf16, [_seg2d(dh16, tt, 2)], None, False,
                     min(2048, F), tt, deps, BF16).reshape(N_CHIPS, F // N_CHIPS, D)
        deps = [sink.submit({("w_down", l): g_down})]
        da, dh, dh16, g_mlp[l] = _mlp_bwd(f"mlp_bwd{l}", dh, dh16, s["a"], w["w_down"], w["w_up"], s["h_mid"],
                                          row(norm_mlp[l]), tm, deps)
        g_up = _tn(f"up_wgrad{l}", s["n2"], _to_bf16, [_seg2d(da, tt, 2)], F // N_CHIPS, True, D, tt, (), BF16)
        deps = sink.pump(dh) + [sink.submit({("w_up", l): g_up})]
        if l < n_a:
            g_out = _tn(f"a_out_wgrad{l}", s["z"], _to_bf16, [_seg2d(dh16, tt, 2)], None, False,
                        D, tt, deps, BF16).reshape(N_CHIPS, D // N_CHIPS, D)
            dz = _nt_rows(f"a_out_bwd{l}", dh16, w["w_a_out"], 0, None, F32, tm)
            deps = sink.pump(dz) + [sink.submit({("w_a_out", l): g_out})]
            dbcu, g_conv[l] = _conv_bwd(f"conv_bwd{l}", s["bcu"], dz.reshape(B, S, D), cwg, l, LANES)
            dbcu = dbcu.reshape(3, T, D)
            g_in = _tn(f"a_in_wgrad{l}", s["n1"], _to_bf16, [_seg_plane(dbcu, p, tt, 2) for p in range(3)],
                       3 * D // N_CHIPS, True, D, tt, deps, BF16)
            dh, dh16, g_mix[l] = _nt_cols(f"a_in_bwd{l}", [_seg_plane(dbcu, p, tm, 1) for p in range(3)],
                                          w["w_a_in"], 0, tm, (s["h_in"], row(norm_mix[l]), dh))
            mixer = {("w_a_in", l): g_in}
        else:
            i = l - n_a
            g_o = _tn(f"o_wgrad{i}", s["o"], _to_bf16, [_seg2d(dh16, tt, 2)], D // N_CHIPS, True, C, tt, deps,
                      BF16)
            do = _nt_cols(f"o_bwd{i}", [_seg2d(dh16, tm, 1)], w["w_o"], 0, tm, None)
            deps = sink.pump(do) + [sink.submit({("w_o", i): g_o})]
            dq, dk, dv = _attn_bwd(f"attn_bwd{i}", s["q"], kv, slopes, s["o"].reshape(B, S, C),
                                   s["lse"].reshape(B, S, C), do.reshape(B, S, C), n_heads, dkv)
            dkv = (dk, dv)
            dq = dq.reshape(T, 3 * C)
            g_q = _tn(f"q_wgrad{i}", s["n1"], _to_bf16, [_seg2d(dq, tt, 2)], 3 * C // N_CHIPS, True, D, tt, deps,
                      BF16)
            dh, dh16, g_mix[l] = _nt_cols(f"q_bwd{i}", [_seg2d(dq, tm, 1)], w["w_q"], 0, tm,
                                          (s["h_in"], row(norm_mix[l]), dh))
            mixer = {("w_q", i): g_q}
            if i == 0:
                dk2, dv2 = (t.reshape(T, 3 * C) for t in dkv)
                mixer[("w_kv", 0)] = _tn("kv_wgrad", nkv, _to_bf16, _kv_segments(dk2, dv2, C, tt, 2),
                                         6 * C // N_CHIPS, True, D, tt, (), BF16)
                dh, dh16, g_kv = _nt_cols("kv_bwd", _kv_segments(dk2, dv2, C, tm, 1), w["w_kv"], 0, tm,
                                          (h_kv, row(norm_kv), dh))
        deps = sink.pump(dh) + [sink.submit(mixer)]
    small = dict(norm_mix=jnp.concatenate(g_mix, axis=0), norm_mlp=jnp.concatenate(g_mlp, axis=0),
                 norm_kv=g_kv, norm_final=dg_final, conv_w=jnp.stack(g_conv))
    return loss, dh.reshape(B, S, D), small


BIG = ("w_a_in", "w_a_out", "w_kv", "w_q", "w_o", "w_up", "w_down")
CONV_PAD_ROWS = 16


def kernel(x, norm_mix, norm_mlp, w_a_in, conv_w, w_a_out, norm_kv, w_kv, w_q, w_o, w_up, w_down, norm_final, loss_target, m_norm_mix, m_norm_mlp, m_w_a_in, m_conv_w, m_w_a_out, m_norm_kv, m_w_kv, m_w_q, m_w_o, m_w_up, m_w_down, m_norm_final, v_norm_mix, v_norm_mlp, v_w_a_in, v_conv_w, v_w_a_out, v_norm_kv, v_w_kv, v_w_q, v_w_o, v_w_up, v_w_down, v_norm_final):
    D = x.shape[-1]
    w = dict(norm_mix=norm_mix, norm_mlp=norm_mlp, w_a_in=w_a_in, conv_w=conv_w, w_a_out=w_a_out, norm_kv=norm_kv,
             w_kv=w_kv[None], w_q=w_q, w_o=w_o, w_up=w_up, w_down=w_down, norm_final=norm_final)
    m = dict(norm_mix=m_norm_mix, norm_mlp=m_norm_mlp, w_a_in=m_w_a_in, conv_w=m_conv_w, w_a_out=m_w_a_out,
             norm_kv=m_norm_kv, w_kv=m_w_kv[None], w_q=m_w_q, w_o=m_w_o, w_up=m_w_up, w_down=m_w_down,
             norm_final=m_norm_final)
    v = dict(norm_mix=v_norm_mix, norm_mlp=v_norm_mlp, w_a_in=v_w_a_in, conv_w=v_conv_w, w_a_out=v_w_a_out,
             norm_kv=v_norm_kv, w_kv=v_w_kv[None], w_q=v_w_q, w_o=v_w_o, w_up=v_w_up, w_down=v_w_down,
             norm_final=v_norm_final)
    depth = norm_mix.shape[0]
    n_a, taps, cwc = conv_w.shape
    n_heads = w_o.shape[1] // HEAD_DIM

    conv_rows = jnp.zeros((CONV_PAD_ROWS, cwc), F32).at[:n_a * taps].set(conv_w.reshape(n_a * taps, cwc))
    blocks = {}
    for l in range(depth):
        if l < n_a:
            blocks[(l, "w_a_in")] = w_a_in[l].astype(BF16)
            if l == 0:
                blocks[(0, "conv")] = conv_rows
            blocks[(l, "w_a_out")] = w_a_out[l].astype(BF16)
        else:
            if l == n_a:
                blocks[(l, "w_kv")] = w_kv.astype(BF16)
            blocks[(l, "w_q")] = w_q[l - n_a].astype(BF16)
            blocks[(l, "w_o")] = w_o[l - n_a].astype(BF16)
        blocks[(l, "w_up")] = w_up[l].astype(BF16)
        blocks[(l, "w_down")] = w_down[l].astype(BF16)
    weights = _WeightGather(blocks)
    place = jnp.stack([2 * lax.axis_index("x") + lax.axis_index("y"), lax.axis_index("c")]).astype(jnp.int32)
    sink = _GradReduce(place)

    loss, grad_x, small = _local_step(x, loss_target, norm_mix, norm_mlp, norm_kv, norm_final, weights, sink,
                                      n_a, n_heads)
    loss = lax.psum(loss[0, 0], ("x", "y", "c"))

    share = _PairShare(sink.finish(grad_x), BIG)
    grads = {}

    packed = jnp.concatenate([small["norm_mix"], small["norm_mlp"], small["norm_kv"], small["norm_final"],
                              small["conv_w"].reshape(n_a * taps, D)], axis=0)
    pad = (-packed.shape[0]) % 8
    packed = jnp.pad(packed, ((0, pad), (0, 0)))
    total = _small_allreduce(packed)
    grads["norm_mix"] = total[:depth]
    grads["norm_mlp"] = total[depth:2 * depth]
    grads["norm_kv"] = total[2 * depth]
    grads["norm_final"] = total[2 * depth + 1]
    chip = 2 * lax.axis_index("x") + lax.axis_index("y")
    conv_full = total[2 * depth + 2:2 * depth + 2 + n_a * taps].reshape(n_a, taps, N_CHIPS, cwc)
    grads["conv_w"] = lax.dynamic_index_in_dim(conv_full, chip, axis=2, keepdims=False)

    order = ("norm_mix", "norm_mlp", "w_a_in", "conv_w", "w_a_out", "norm_kv", "w_kv", "w_q", "w_o", "w_up",
             "w_down", "norm_final")
    delta, new_m, new_v = {}, {}, {}
    vec_names = ("norm_mix", "norm_mlp", "norm_kv", "norm_final")
    rows_of = lambda a: a.reshape(-1, D)
    vw, vg, vm_, vv = (jnp.concatenate([rows_of(t[k]) for k in vec_names], axis=0) for t in (w, grads, m, v))
    vpad = (-vw.shape[0]) % 8
    padrows = lambda a: jnp.pad(a, ((0, vpad), (0, 0)))
    vd, vnm, vnv = _adamw("adamw_norms", padrows(vw), padrows(vg), padrows(vm_), padrows(vv))
    off = 0
    for k in vec_names:
        r = rows_of(w[k]).shape[0]
        delta[k] = vd[off:off + r].reshape(w[k].shape)
        new_m[k] = vnm[off:off + r].reshape(w[k].shape)
        new_v[k] = vnv[off:off + r].reshape(w[k].shape)
        off += r
    cpad = (-n_a * taps) % 8
    two_d = lambda a: jnp.pad(a.reshape(-1, cwc), ((0, cpad), (0, 0)))
    cd, cnm, cnv = _adamw("adamw_conv_w", two_d(w["conv_w"]), two_d(grads["conv_w"]), two_d(m["conv_w"]),
                          two_d(v["conv_w"]))
    delta["conv_w"], new_m["conv_w"], new_v["conv_w"] = (t[:n_a * taps].reshape(conv_w.shape) for t in (cd, cnm, cnv))
    after = cd
    for k in sorted(BIG, key=lambda k: w[k].size):
        shared = share.get(k, after)
        per_layer = [shared[(k, l)].reshape(w[k].shape[1:]) for l in range(w[k].shape[0])]
        grads[k], delta[k], new_m[k], new_v[k] = _adamw_layers(f"adamw_{k}", w[k], per_layer, m[k], v[k])
        after = delta[k]
    fix = lambda k, a: a[0] if k == "w_kv" else a
    return (loss, grad_x, *[fix(k, grads[k]) for k in order], *[fix(k, delta[k]) for k in order],
            *[fix(k, new_m[k]) for k in order], *[fix(k, new_v[k]) for k in order])
```

```python
import jax
import jax.numpy as jnp
from jax import lax
from jax.experimental import pallas as pl
from jax.experimental.pallas import tpu as pltpu

F32 = jnp.float32
BF16 = jnp.bfloat16
MESH = pl.DeviceIdType.MESH

EPS = 1e-5
PATTERNS = ((128, 1), (512, 4), (2048, 16))
HEAD_DIM = 64
ALIBI_MAX_BIAS = 8.0
NEG_INF = -1e30
ATT_BLK = 128
BWD_UNROLL = 16
N_CHIPS = 4
LANES = 128
VMEM_LIMIT = 56 * 1024 * 1024

ADAM_LR = 0.001
ADAM_B1 = 0.9
ADAM_B2 = 0.999
ADAM_EPS = 1e-08
ADAM_WD = 0.01
ADAM_STEP = 10


ANY = pl.BlockSpec(memory_space=pl.ANY)


def _params(n_grid_axes):
    return pltpu.CompilerParams(dimension_semantics=("arbitrary",) * n_grid_axes, vmem_limit_bytes=VMEM_LIMIT)


def _dot(a, b):
    return jnp.dot(a, b, preferred_element_type=F32)


def _dot_nt(a, b):
    return lax.dot_general(a, b, (((1,), (1,)), ((), ())), preferred_element_type=F32)


def _dot_tn(a, b):
    return lax.dot_general(a, b, (((0,), (0,)), ((), ())), preferred_element_type=F32)


def _relu2(a):
    return jnp.square(jnp.maximum(a, 0.0))


def _rms(hf, g):
    y = hf * lax.rsqrt(jnp.mean(hf * hf, axis=-1, keepdims=True) + EPS)
    return y * g


def _rms_bwd(hf, g, dn):
    rstd = lax.rsqrt(jnp.mean(hf * hf, axis=-1, keepdims=True) + EPS)
    xhat = hf * rstd
    dg = jnp.sum(dn * xhat, axis=0, keepdims=True)
    dx = dn * g
    dh = rstd * (dx - xhat * jnp.mean(dx * xhat, axis=-1, keepdims=True))
    return dh, dg


def _pieces(seg_widths, chunk_width, max_width):
    total = sum(seg_widths)
    cuts = {0, total}
    acc = 0
    for w in seg_widths:
        cuts.add(acc)
        acc += w
    cuts.update(range(0, total, chunk_width))
    cuts = sorted(cuts)
    fine = []
    for lo, hi in zip(cuts[:-1], cuts[1:]):
        while hi - lo > max_width:
            fine.append((lo, lo + max_width))
            lo += max_width
        fine.append((lo, hi))
    out = []
    for lo, hi in fine:
        acc = 0
        for s, w in enumerate(seg_widths):
            if lo < acc + w:
                break
            acc += w
        out.append((s, lo - acc, lo // chunk_width, lo % chunk_width, hi - lo))
    return out


def _relu2_bf16(a):
    return _relu2(a.astype(F32)).astype(BF16)


def _to_bf16(a):
    return a.astype(BF16)


def _norm_mm(name, h, g, wg, layer, planes, out_dtype, tm, deps=()):
    T, D = h.shape
    cw = wg.shape[3]
    N = N_CHIPS * cw
    pw = N // planes
    pieces = _pieces([pw] * planes, cw, 512)

    def body(h_ref, g_ref, w_ref, *rest):
        n_ref, o_ref = rest[len(deps):]
        n = _rms(h_ref[...], g_ref[...]).astype(BF16)
        n_ref[...] = n
        for s, a0, ch, b0, wd in pieces:
            o_ref[s, :, a0:a0 + wd] = _dot(n, w_ref[ch, :, b0:b0 + wd]).astype(out_dtype)

    return pl.pallas_call(
        body, name=name, grid=(T // tm,),
        in_specs=[pl.BlockSpec((tm, D), lambda i: (i, 0)),
                  pl.BlockSpec((1, D), lambda i: (0, 0)),
                  pl.BlockSpec((N_CHIPS, None, D, cw), lambda i: (0, layer, 0, 0))] + [ANY] * len(deps),
        out_specs=[pl.BlockSpec((tm, D), lambda i: (i, 0)),
                   pl.BlockSpec((planes, tm, pw), lambda i: (0, i, 0))],
        out_shape=[jax.ShapeDtypeStruct((T, D), BF16), jax.ShapeDtypeStruct((planes, T, pw), out_dtype)],
        compiler_params=_params(1))(h, g, wg, *deps)


def _resident(shape, index_map):
    return pl.BlockSpec(shape, index_map, pipeline_mode=pl.Buffered(1))


def _mm_res_rows(name, a, wg, layer, h, act, tm):
    T = a.shape[0]
    rk, D = wg.shape[2], wg.shape[3]

    def body(a_ref, w_ref, h_ref, o_ref):
        acc = h_ref[...]
        for k in range(N_CHIPS):
            acc = acc + _dot(act(a_ref[:, k * rk:(k + 1) * rk]), w_ref[k])
        o_ref[...] = acc

    return pl.pallas_call(
        body, name=name, grid=(T // tm,),
        in_specs=[pl.BlockSpec((tm, N_CHIPS * rk), lambda i: (i, 0)),
                  pl.BlockSpec((N_CHIPS, None, rk, D), lambda i: (0, layer, 0, 0)),
                  pl.BlockSpec((tm, D), lambda i: (i, 0))],
        out_specs=pl.BlockSpec((tm, D), lambda i: (i, 0)),
        out_shape=jax.ShapeDtypeStruct((T, D), F32),
        compiler_params=_params(1))(a, wg, h)


def _mm_res_cols(name, a, wg, layer, h, tm):
    T, K = a.shape
    cw = wg.shape[3]
    D = N_CHIPS * cw

    def body(a_ref, w_ref, h_ref, o_ref):
        a16 = a_ref[...].astype(BF16)
        for j in range(N_CHIPS):
            o_ref[:, j * cw:(j + 1) * cw] = h_ref[:, j * cw:(j + 1) * cw] + _dot(a16, w_ref[j])

    return pl.pallas_call(
        body, name=name, grid=(T // tm,),
        in_specs=[pl.BlockSpec((tm, K), lambda i: (i, 0)),
                  pl.BlockSpec((N_CHIPS, None, K, cw), lambda i: (0, layer, 0, 0)),
                  pl.BlockSpec((tm, D), lambda i: (i, 0))],
        out_specs=pl.BlockSpec((tm, D), lambda i: (i, 0)),
        out_shape=jax.ShapeDtypeStruct((T, D), F32),
        compiler_params=_params(1))(a, wg, h)


def _mlp_fwd(name, h, g, wup, wdown, tm):
    T, D = h.shape
    cw = wup.shape[3]

    def body(h_ref, g_ref, wu_ref, wd_ref, n_ref, a_ref, o_ref):
        hf = h_ref[...]
        n = _rms(hf, g_ref[...]).astype(BF16)
        n_ref[...] = n
        acc = hf
        for ch in range(N_CHIPS):
            a16 = _dot(n, wu_ref[ch]).astype(BF16)
            a_ref[:, ch * cw:(ch + 1) * cw] = a16
            acc = acc + _dot(_relu2_bf16(a16), wd_ref[ch])
        o_ref[...] = acc

    row = pl.BlockSpec((tm, D), lambda i: (i, 0))
    return pl.pallas_call(
        body, name=name, grid=(T // tm,),
        in_specs=[row, pl.BlockSpec((1, D), lambda i: (0, 0)),
                  _resident((N_CHIPS, None, D, cw), lambda i: (0, 0, 0, 0)),
                  _resident((N_CHIPS, None, cw, D), lambda i: (0, 0, 0, 0))],
        out_specs=[row, pl.BlockSpec((tm, N_CHIPS * cw), lambda i: (i, 0)), row],
        out_shape=[jax.ShapeDtypeStruct((T, D), BF16), jax.ShapeDtypeStruct((T, N_CHIPS * cw), BF16),
                   jax.ShapeDtypeStruct((T, D), F32)],
        compiler_params=_params(1))(h, g, wup, wdown)


def _mlp_bwd(name, dh, dh16, a, wdown, wup, h_mid, g, tm, deps=()):
    T, D = dh.shape
    cw = wup.shape[3]
    F = N_CHIPS * cw

    def body(dh_ref, dh16_ref, a_ref, wd_ref, wu_ref, h_ref, g_ref, *rest):
        da_ref, out_ref, out16_ref, dg_ref = rest[len(deps):]
        d16 = dh16_ref[...]
        acc = None
        for ch in range(N_CHIPS):
            cols = slice(ch * cw, (ch + 1) * cw)
            da = (_dot_nt(d16, wd_ref[ch]) * (2.0 * jnp.maximum(a_ref[:, cols].astype(F32), 0.0))).astype(BF16)
            da_ref[:, cols] = da
            d = _dot_nt(da, wu_ref[ch])
            acc = d if acc is None else acc + d
        dh_c, dg = _rms_bwd(h_ref[...], g_ref[...], acc)
        out = dh_ref[...] + dh_c
        out_ref[...] = out
        out16_ref[...] = out.astype(BF16)

        @pl.when(pl.program_id(0) == 0)
        def _():
            dg_ref[...] = dg

        @pl.when(pl.program_id(0) > 0)
        def _():
            dg_ref[...] += dg

    row = pl.BlockSpec((tm, D), lambda i: (i, 0))
    wide = pl.BlockSpec((tm, F), lambda i: (i, 0))
    vec = pl.BlockSpec((1, D), lambda i: (0, 0))
    return pl.pallas_call(
        body, name=name, grid=(T // tm,),
        in_specs=[row, row, wide, _resident((N_CHIPS, None, cw, D), lambda i: (0, 0, 0, 0)),
                  _resident((N_CHIPS, None, D, cw), lambda i: (0, 0, 0, 0)), row, vec] + [ANY] * len(deps),
        out_specs=[wide, row, row, vec],
        out_shape=[jax.ShapeDtypeStruct((T, F), BF16), jax.ShapeDtypeStruct((T, D), F32),
                   jax.ShapeDtypeStruct((T, D), BF16), jax.ShapeDtypeStruct((1, D), F32)],
        compiler_params=_params(1))(dh, dh16, a, wdown, wup, h_mid, g, *deps)


CONV_ROWS = 256
CONV_HALO = 16


def _conv_shifted(ext, k, r0, rows):
    rolled = pltpu.roll(ext, k, 0)[CONV_HALO:]
    t = r0 + lax.broadcasted_iota(jnp.int32, rolled.shape, 0)
    return jnp.where(t >= k, rolled, 0.0)


def _conv_ahead(ext, k, r0, rows, S):
    rolled = pltpu.roll(ext, rows + CONV_HALO - k, 0)[:rows]
    t = r0 + lax.broadcasted_iota(jnp.int32, rolled.shape, 0)
    return jnp.where(t + k < S, rolled, 0.0)


def _conv_fwd(name, bcu, cwg, layer, tc):
    _, B, S, D = bcu.shape
    cwc = cwg.shape[3]
    per_chunk = cwc // tc
    R = min(CONV_ROWS, S)

    def body(x_ref, w_ref, z_ref):
        w = [w_ref[k:k + 1, :] for k in range(3)]

        def step(i, carry):
            r0 = pl.multiple_of(i * R, R)
            h0 = pl.multiple_of(jnp.maximum(r0 - CONV_HALO, 0), CONV_HALO)
            ld = lambda p, start, rows: x_ref[p, pl.ds(start, rows), :].astype(F32)
            cu = jnp.concatenate([ld(1, h0, CONV_HALO) * ld(2, h0, CONV_HALO), ld(1, r0, R) * ld(2, r0, R)], axis=0)
            conv = w[0] * cu[CONV_HALO:]
            conv = conv + w[1] * _conv_shifted(cu, 1, r0, R)
            conv = conv + w[2] * _conv_shifted(cu, 2, r0, R)
            z_ref[pl.ds(r0, R), :] = (ld(0, r0, R) * conv).astype(BF16)
            return carry

        lax.fori_loop(0, S // R, step, 0)

    return pl.pallas_call(
        body, name=name, grid=(B, D // tc),
        in_specs=[pl.BlockSpec((3, None, S, tc), lambda b, j: (0, b, 0, j)),
                  pl.BlockSpec((None, None, 3, tc), lambda b, j: (j // per_chunk, layer, 0, j % per_chunk))],
        out_specs=pl.BlockSpec((None, S, tc), lambda b, j: (b, 0, j)),
        out_shape=jax.ShapeDtypeStruct((B, S, D), BF16),
        compiler_params=_params(2))(bcu, cwg)


def _conv_bwd(name, bcu, dz, cwg, layer, tc):
    _, B, S, D = bcu.shape
    cwc = cwg.shape[3]
    per_chunk = cwc // tc
    R = min(CONV_ROWS, S)

    def body(x_ref, dz_ref, w_ref, d_ref, dw_ref):
        w = [w_ref[k:k + 1, :] for k in range(3)]

        @pl.when(pl.program_id(1) == 0)
        def _():
            dw_ref[...] = jnp.zeros_like(dw_ref)

        def step(i, carry):
            r0 = pl.multiple_of(i * R, R)
            h0 = pl.multiple_of(jnp.maximum(r0 - CONV_HALO, 0), CONV_HALO)
            a0 = pl.multiple_of(jnp.minimum(r0 + R, S - CONV_HALO), CONV_HALO)
            ld = lambda p, start, rows: x_ref[p, pl.ds(start, rows), :].astype(F32)
            b, c, u = ld(0, r0, R), ld(1, r0, R), ld(2, r0, R)
            dz = dz_ref[pl.ds(r0, R), :]
            cu = jnp.concatenate([ld(1, h0, CONV_HALO) * ld(2, h0, CONV_HALO), c * u], axis=0)
            cu1 = _conv_shifted(cu, 1, r0, R)
            cu2 = _conv_shifted(cu, 2, r0, R)
            conv = w[0] * (c * u) + w[1] * cu1 + w[2] * cu2
            dconv = dz * b
            dca = jnp.concatenate([dconv, dz_ref[pl.ds(a0, CONV_HALO), :] * ld(0, a0, CONV_HALO)], axis=0)
            dcu = w[0] * dconv + w[1] * _conv_ahead(dca, 1, r0, R, S) + w[2] * _conv_ahead(dca, 2, r0, R, S)
            d_ref[0, pl.ds(r0, R), :] = (dz * conv).astype(BF16)
            d_ref[1, pl.ds(r0, R), :] = (dcu * u).astype(BF16)
            d_ref[2, pl.ds(r0, R), :] = (dcu * c).astype(BF16)
            return (carry[0] + jnp.sum(dconv * (c * u), axis=0, keepdims=True),
                    carry[1] + jnp.sum(dconv * cu1, axis=0, keepdims=True),
                    carry[2] + jnp.sum(dconv * cu2, axis=0, keepdims=True))

        zero = jnp.zeros((1, tc), F32)
        s0, s1, s2 = lax.fori_loop(0, S // R, step, (zero, zero, zero))
        for k, sk in enumerate((s0, s1, s2)):
            dw_ref[k:k + 1, :] += sk

    return pl.pallas_call(
        body, name=name, grid=(D // tc, B),
        in_specs=[pl.BlockSpec((3, None, S, tc), lambda j, b: (0, b, 0, j)),
                  pl.BlockSpec((None, S, tc), lambda j, b: (b, 0, j)),
                  pl.BlockSpec((None, None, 3, tc), lambda j, b: (j // per_chunk, layer, 0, j % per_chunk))],
        out_specs=[pl.BlockSpec((3, None, S, tc), lambda j, b: (0, b, 0, j)),
                   pl.BlockSpec((3, tc), lambda j, b: (0, j))],
        out_shape=[jax.ShapeDtypeStruct((3, B, S, D), BF16), jax.ShapeDtypeStruct((3, D), F32)],
        compiler_params=_params(2))(bcu, dz, cwg)


def _att_rows(dil, idx, nb):
    r, n = idx // nb, idx % nb
    if dil == 1:
        cur = pl.ds(pl.multiple_of(n * ATT_BLK, ATT_BLK), ATT_BLK)
        prev = pl.ds(pl.multiple_of(jnp.maximum(n - 1, 0) * ATT_BLK, ATT_BLK), ATT_BLK)
    else:
        cur = pl.ds(n * (ATT_BLK * dil) + r, ATT_BLK, stride=dil)
        prev = pl.ds(jnp.maximum(n - 1, 0) * (ATT_BLK * dil) + r, ATT_BLK, stride=dil)
    return n, cur, prev


def _att_bias(bias_ref, dil, sl_ref, hp):
    row = lax.broadcasted_iota(jnp.int32, (2 * ATT_BLK, 2 * ATT_BLK), 0)
    ci = lax.broadcasted_iota(jnp.int32, (2 * ATT_BLK, 2 * ATT_BLK), 1)
    j = ATT_BLK + (row & (ATT_BLK - 1)) - ci
    slope = jnp.where(row < ATT_BLK, sl_ref[2 * hp], sl_ref[2 * hp + 1])
    rest = jnp.where((j >= 0) & (j <= ATT_BLK), -slope * (dil * j).astype(F32), NEG_INF)
    bias_ref[1] = rest
    bias_ref[0] = jnp.where(ci >= ATT_BLK, rest, NEG_INF)


def _stack_heads(x16, lane):
    first = lane < HEAD_DIM
    return jnp.concatenate([jnp.where(first, x16, jnp.zeros_like(x16)),
                            jnp.where(first, jnp.zeros_like(x16), x16)], axis=0)


def _per_head(col, lane):
    return jnp.where(lane < HEAD_DIM, col[:ATT_BLK], col[ATT_BLK:])


def _attn_fwd(name, q, kv, slopes, n_heads):
    B, S, CQ = q.shape
    HP = n_heads * HEAD_DIM // LANES
    scale = HEAD_DIM ** -0.5
    n_groups = len(PATTERNS)
    CH = 256

    def body(sl_ref, q_ref, k_ref, v_ref, o_ref, lse_ref, bias_ref, *parts):
        og, lg = parts[:n_groups], parts[n_groups:]
        hp, g = pl.program_id(1), pl.program_id(2)
        lane = lax.broadcasted_iota(jnp.int32, (1, LANES), 1)

        for gi, (window, dil) in enumerate(PATTERNS):
            nb = S // dil // ATT_BLK

            @pl.when(g == gi)
            def _(gi=gi, dil=dil, nb=nb):
                _att_bias(bias_ref, dil, sl_ref, hp)

                def step(idx, carry):
                    n, cur, prev = _att_rows(dil, idx, nb)
                    qs = _stack_heads((q_ref[cur, :] * scale).astype(BF16), lane)
                    kc = jnp.concatenate([k_ref[prev, :], k_ref[cur, :]], axis=0).astype(BF16)
                    vc = jnp.concatenate([v_ref[prev, :], v_ref[cur, :]], axis=0).astype(BF16)
                    s = _dot_nt(qs, kc) + bias_ref[jnp.minimum(n, 1)]
                    m = jnp.max(s, axis=-1, keepdims=True)
                    p = jnp.exp(s - m)
                    l = jnp.sum(p, axis=-1, keepdims=True)
                    p16 = p.astype(BF16)
                    o_un = _dot(jnp.concatenate([p16[:ATT_BLK], p16[ATT_BLK:]], axis=1), _stack_heads_rows(vc, lane))
                    og[gi][cur, :] = o_un / _per_head(l, lane)
                    lg[gi][cur, :] = _per_head(m + jnp.log(l), lane)
                    return carry

                lax.fori_loop(0, S // ATT_BLK, step, 0, unroll=16)

        @pl.when(g == n_groups - 1)
        def _():
            def comb(i, carry):
                rows = pl.ds(pl.multiple_of(i * CH, CH), CH)
                a, b, c = lg[0][rows, :], lg[1][rows, :], lg[2][rows, :]
                m = jnp.maximum(jnp.maximum(a, b), c)
                ea, eb, ec = jnp.exp(a - m), jnp.exp(b - m), jnp.exp(c - m)
                z = ea + eb + ec
                o_ref[rows, :] = (ea / z) * og[0][rows, :] + (eb / z) * og[1][rows, :] + (ec / z) * og[2][rows, :]
                lse_ref[rows, :] = m + jnp.log(z)
                return carry

            lax.fori_loop(0, S // CH, comb, 0)

    blk = (None, S, LANES)
    out = pl.BlockSpec(blk, lambda b, hp, g: (b, 0, hp))
    return pl.pallas_call(
        body, name=name, grid=(B, HP, n_groups),
        in_specs=[pl.BlockSpec(memory_space=pltpu.SMEM),
                  pl.BlockSpec(blk, lambda b, hp, g: (b, 0, g * HP + hp)),
                  pl.BlockSpec(blk, lambda b, hp, g: (b, 0, g * 2 * HP + hp)),
                  pl.BlockSpec(blk, lambda b, hp, g: (b, 0, g * 2 * HP + HP + hp))],
        out_specs=[out, out],
        out_shape=[jax.ShapeDtypeStruct((B, S, HP * LANES), F32)] * 2,
        scratch_shapes=[pltpu.VMEM((2, 2 * ATT_BLK, 2 * ATT_BLK), F32)] + [pltpu.VMEM((S, LANES), F32)] * (2 * n_groups),
        compiler_params=_params(3))(slopes, q, kv, kv)


def _stack_heads_rows(x16, lane):
    first = lane < HEAD_DIM
    return jnp.concatenate([jnp.where(first, x16, jnp.zeros_like(x16)),
                            jnp.where(first, jnp.zeros_like(x16), x16)], axis=0)


def _attn_bwd(name, q, kv, slopes, o, lse, do, n_heads, dkv_prev):
    B, S, CQ = q.shape
    HP = n_heads * HEAD_DIM // LANES
    scale = HEAD_DIM ** -0.5
    n_groups = len(PATTERNS)
    n_prev = 0 if dkv_prev is None else 2

    def body(sl_ref, q_ref, k_ref, v_ref, o_ref, lse_ref, do_ref, *rest):
        dq_ref, dk_ref, dv_ref, bias_ref = rest[n_prev:]
        hp, g = pl.program_id(1), pl.program_id(2)
        lane = lax.broadcasted_iota(jnp.int32, (1, LANES), 1)
        first = lane < HEAD_DIM

        def flush(rows, dk, dv):
            if n_prev:
                dk = dk + rest[0][rows, :]
                dv = dv + rest[1][rows, :]
            dk_ref[rows, :] = dk
            dv_ref[rows, :] = dv

        for gi, (window, dil) in enumerate(PATTERNS):
            nb = S // dil // ATT_BLK
            n_blocks = S // ATT_BLK

            @pl.when(g == gi)
            def _(dil=dil, nb=nb, n_blocks=n_blocks):
                _att_bias(bias_ref, dil, sl_ref, hp)

                def block(idx, carry, first_of_all):
                    n, cur, prev = _att_rows(dil, idx, nb)
                    qs = _stack_heads((q_ref[cur, :] * scale).astype(BF16), lane)
                    kc = jnp.concatenate([k_ref[prev, :], k_ref[cur, :]], axis=0).astype(BF16)
                    vc = jnp.concatenate([v_ref[prev, :], v_ref[cur, :]], axis=0).astype(BF16)
                    dob = do_ref[cur, :]
                    prod = dob * o_ref[cur, :]
                    lseb = lse_ref[cur, :]
                    dos = _stack_heads(dob.astype(BF16), lane)
                    delta = jnp.concatenate(
                        [jnp.sum(jnp.where(first, prod, 0.0), axis=-1, keepdims=True),
                         jnp.sum(jnp.where(first, 0.0, prod), axis=-1, keepdims=True)], axis=0)
                    lse_col = jnp.concatenate(
                        [jnp.max(jnp.where(first, lseb, -jnp.inf), axis=-1, keepdims=True),
                         jnp.max(jnp.where(first, -jnp.inf, lseb), axis=-1, keepdims=True)], axis=0)
                    s = _dot_nt(qs, kc) + bias_ref[jnp.minimum(n, 1)]
                    p = jnp.exp(s - lse_col)
                    ds = p * (_dot_nt(dos, vc) - delta)
                    ds16 = ds.astype(BF16)
                    dq = _dot(jnp.concatenate([ds16[:ATT_BLK], ds16[ATT_BLK:]], axis=1), _stack_heads_rows(kc, lane))
                    dq_ref[cur, :] = dq * scale
                    dk = _dot_tn(ds16, qs)
                    dv = _dot_tn(p.astype(BF16), dos)

                    def flush_before():
                        _, before, _ = _att_rows(dil, idx - 1, nb)
                        flush(before, carry[0] + dk[:ATT_BLK], carry[1] + dv[:ATT_BLK])

                    if first_of_all:
                        pl.when(idx > 0)(flush_before)
                    else:
                        flush_before()
                    return dk[ATT_BLK:], dv[ATT_BLK:]

                def step(i, carry):
                    for u in range(BWD_UNROLL):
                        carry = block(i * BWD_UNROLL + u, carry, u == 0)
                    return carry

                zero = jnp.zeros((ATT_BLK, LANES), F32)
                dk_last, dv_last = lax.fori_loop(0, n_blocks // BWD_UNROLL, step, (zero, zero))
                _, last, _ = _att_rows(dil, n_blocks - 1, nb)
                flush(last, dk_last, dv_last)

    blk = (None, S, LANES)
    shared = pl.BlockSpec(blk, lambda b, hp, g: (b, 0, hp))
    grouped = pl.BlockSpec(blk, lambda b, hp, g: (b, 0, g * HP + hp))
    prev = [] if dkv_prev is None else list(dkv_prev)
    gshape = jax.ShapeDtypeStruct((B, S, n_groups * HP * LANES), F32)
    return pl.pallas_call(
        body, name=name, grid=(B, HP, n_groups),
        in_specs=[pl.BlockSpec(memory_space=pltpu.SMEM), grouped,
                  pl.BlockSpec(blk, lambda b, hp, g: (b, 0, g * 2 * HP + hp)),
                  pl.BlockSpec(blk, lambda b, hp, g: (b, 0, g * 2 * HP + HP + hp)),
                  shared, shared, shared] + [grouped] * n_prev,
        out_specs=[grouped] * 3, out_shape=[gshape] * 3,
        scratch_shapes=[pltpu.VMEM((2, 2 * ATT_BLK, 2 * ATT_BLK), F32)],
        compiler_params=_params(3))(slopes, q, kv, kv, o, lse, do, *prev)


def _final_loss(name, h, g, target, tm):
    T, D = h.shape

    def body(h_ref, g_ref, t_ref, loss_ref, dh_ref, dh16_ref, dg_ref):
        hf = h_ref[...]
        gv = g_ref[...]
        rstd = lax.rsqrt(jnp.mean(hf * hf, axis=-1, keepdims=True) + EPS)
        xhat = hf * rstd
        err = xhat * gv - t_ref[...]
        part = 0.5 * jnp.sum(jnp.mean(err * err, axis=-1, keepdims=True), axis=0, keepdims=True)
        dy = err * (1.0 / D)
        dg = jnp.sum(dy * xhat, axis=0, keepdims=True)
        dx = dy * gv
        dh = rstd * (dx - xhat * jnp.mean(dx * xhat, axis=-1, keepdims=True))
        dh_ref[...] = dh
        dh16_ref[...] = dh.astype(BF16)

        @pl.when(pl.program_id(0) == 0)
        def _():
            loss_ref[...] = part
            dg_ref[...] = dg

        @pl.when(pl.program_id(0) > 0)
        def _():
            loss_ref[...] += part
            dg_ref[...] += dg

    return pl.pallas_call(
        body, name=name, grid=(T // tm,),
        in_specs=[pl.BlockSpec((tm, D), lambda i: (i, 0)), pl.BlockSpec((1, D), lambda i: (0, 0)),
                  pl.BlockSpec((tm, D), lambda i: (i, 0))],
        out_specs=[pl.BlockSpec((1, 1), lambda i: (0, 0)), pl.BlockSpec((tm, D), lambda i: (i, 0)),
                   pl.BlockSpec((tm, D), lambda i: (i, 0)), pl.BlockSpec((1, D), lambda i: (0, 0))],
        out_shape=[jax.ShapeDtypeStruct((1, 1), F32), jax.ShapeDtypeStruct((T, D), F32),
                   jax.ShapeDtypeStruct((T, D), BF16), jax.ShapeDtypeStruct((1, D), F32)],
        compiler_params=_params(1))(h, g, target)


def _nt_rows(name, dh, wg, layer, a_mul, out_dtype, tm, deps=()):
    T, D = dh.shape
    rk = wg.shape[2]
    N = N_CHIPS * rk
    with_a = a_mul is not None

    def body(dh_ref, w_ref, *rest):
        o_ref = rest[-1]
        d16 = dh_ref[...]
        for ch in range(N_CHIPS):
            r = _dot_nt(d16, w_ref[ch])
            if with_a:
                r = r * (2.0 * jnp.maximum(rest[0][:, ch * rk:(ch + 1) * rk].astype(F32), 0.0))
            o_ref[:, ch * rk:(ch + 1) * rk] = r.astype(out_dtype)

    in_specs = [pl.BlockSpec((tm, D), lambda i: (i, 0)),
                pl.BlockSpec((N_CHIPS, None, rk, D), lambda i: (0, layer, 0, 0))]
    args = [dh, wg]
    if with_a:
        in_specs.append(pl.BlockSpec((tm, N), lambda i: (i, 0)))
        args.append(a_mul)
    in_specs += [ANY] * len(deps)
    args += list(deps)
    return pl.pallas_call(
        body, name=name, grid=(T // tm,), in_specs=in_specs,
        out_specs=pl.BlockSpec((tm, N), lambda i: (i, 0)),
        out_shape=jax.ShapeDtypeStruct((T, N), out_dtype),
        compiler_params=_params(1))(*args)


def _nt_cols(name, ysegs, wg, layer, tm, norm):
    Nw, cw = wg.shape[2], wg.shape[3]
    widths = [bs[-1] for _, bs, _ in ysegs]
    pieces = _pieces(widths, cw, 1024)
    ns = len(ysegs)
    T = norm[0].shape[0] if norm is not None else ysegs[0][0].shape[-2]

    def body(*refs):
        y_refs = refs[:ns]
        w_ref = refs[ns]
        acc = refs[-1]
        for n, (s, a0, ch, b0, wd) in enumerate(pieces):
            d = _dot_nt(y_refs[s][:, a0:a0 + wd].astype(BF16), w_ref[ch, :, b0:b0 + wd])
            if n == 0:
                acc[...] = d
            else:
                acc[...] += d
        if norm is None:
            refs[ns + 1][...] = acc[...]
        else:
            h_ref, g_ref, dhin_ref, out_ref, out16_ref, dg_ref = refs[ns + 1:ns + 7]
            dh_c, dg = _rms_bwd(h_ref[...], g_ref[...], acc[...])
            dh = dhin_ref[...] + dh_c
            out_ref[...] = dh
            out16_ref[...] = dh.astype(BF16)

            @pl.when(pl.program_id(0) == 0)
            def _():
                dg_ref[...] = dg

            @pl.when(pl.program_id(0) > 0)
            def _():
                dg_ref[...] += dg

    in_specs = [pl.BlockSpec(bs, im) for _, bs, im in ysegs]
    in_specs.append(pl.BlockSpec((N_CHIPS, None, Nw, cw), lambda i: (0, layer, 0, 0)))
    args = [a for a, _, _ in ysegs] + [wg]
    row = pl.BlockSpec((tm, Nw), lambda i: (i, 0))
    vec = pl.BlockSpec((1, Nw), lambda i: (0, 0))
    if norm is None:
        out_specs = row
        out_shape = jax.ShapeDtypeStruct((T, Nw), F32)
    else:
        in_specs += [row, vec, row]
        args += list(norm)
        out_specs = [row, row, vec]
        out_shape = [jax.ShapeDtypeStruct((T, Nw), F32), jax.ShapeDtypeStruct((T, Nw), BF16),
                     jax.ShapeDtypeStruct((1, Nw), F32)]
    return pl.pallas_call(
        body, name=name, grid=(T // tm,), in_specs=in_specs, out_specs=out_specs, out_shape=out_shape,
        scratch_shapes=[pltpu.VMEM((tm, Nw), F32)], compiler_params=_params(1))(*args)


def _tn(name, x, x_act, ysegs, cw, cols_layout, tmm, tt, deps=(), out_dtype=F32):
    T, M = x.shape
    widths = [bs[-1] for _, bs, _ in ysegs]
    N = sum(widths)
    pieces = _pieces(widths, cw if cols_layout else N, 1024)
    ns = len(ysegs)
    n_t = T // tt
    block = (N_CHIPS, tmm, cw) if cols_layout else (tmm, N)
    narrow = out_dtype != F32

    def body(x_ref, *refs):
        y_refs = refs[:ns]
        o_ref = refs[ns + len(deps)]
        acc = refs[-1] if narrow else o_ref

        @pl.when(pl.program_id(1) == 0)
        def _():
            acc[...] = jnp.zeros_like(acc)

        xt = x_act(x_ref[...])
        for s, a0, ch, b0, wd in pieces:
            d = _dot_tn(xt, y_refs[s][:, a0:a0 + wd].astype(BF16))
            if cols_layout:
                acc[ch, :, b0:b0 + wd] += d
            else:
                acc[:, b0:b0 + wd] += d
        if narrow:
            @pl.when(pl.program_id(1) == n_t - 1)
            def _():
                o_ref[...] = acc[...].astype(out_dtype)

    in_specs = [pl.BlockSpec((tt, tmm), lambda m, t: (t, m))] + [pl.BlockSpec(bs, im) for _, bs, im in ysegs]
    in_specs += [ANY] * len(deps)
    if cols_layout:
        out_specs = pl.BlockSpec(block, lambda m, t: (0, m, 0))
        out_shape = jax.ShapeDtypeStruct((N_CHIPS, M, cw), out_dtype)
    else:
        out_specs = pl.BlockSpec(block, lambda m, t: (m, 0))
        out_shape = jax.ShapeDtypeStruct((M, N), out_dtype)
    return pl.pallas_call(
        body, name=name, grid=(M // tmm, n_t), in_specs=in_specs, out_specs=out_specs, out_shape=out_shape,
        scratch_shapes=[pltpu.VMEM(block, F32)] if narrow else [],
        compiler_params=_params(2))(x, *[a for a, _, _ in ysegs], *deps)


def _seg2d(a, t_rows, grid_rank):
    w = a.shape[1]
    if grid_rank == 1:
        return (a, (t_rows, w), lambda i: (i, 0))
    return (a, (t_rows, w), lambda m, t: (t, 0))


def _kv_segments(dk, dv, C, t_rows, grid_rank):
    segs = []
    for g in range(len(PATTERNS)):
        for a in (dk, dv):
            if grid_rank == 1:
                segs.append((a, (t_rows, C), lambda i, g=g: (i, g)))
            else:
                segs.append((a, (t_rows, C), lambda m, t, g=g: (t, g)))
    return segs


def _seg_plane(a, plane, t_rows, grid_rank):
    w = a.shape[2]
    if grid_rank == 1:
        return (a, (None, t_rows, w), lambda i: (plane, i, 0))
    return (a, (None, t_rows, w), lambda m, t: (plane, t, 0))


def _row_tile(rows, row_bytes, budget_bytes=2 * 1024 * 1024):
    t = rows
    while t * row_bytes > budget_bytes and t % 32 == 0:
        t //= 2
    return t


N_DEVICES = 8


def _device_add(name, own, slots, place):
    _, _, hr, c = own.shape
    tr = _row_tile(hr, c * 4, 1024 * 1024)

    def body(place_ref, own_ref, *refs):
        o_ref = refs[-1]
        acc = own_ref[...].astype(F32)
        for r in refs[:-1]:
            acc = acc + r[...].astype(F32)
        o_ref[...] = acc

    def slot(k):
        return pl.BlockSpec((None, tr, c), lambda i, pr: ((2 * pr[0] + pr[1] + k) % N_DEVICES, i, 0))

    grid_spec = pltpu.PrefetchScalarGridSpec(
        num_scalar_prefetch=1, grid=(hr // tr,),
        in_specs=[pl.BlockSpec((None, None, tr, c), lambda i, pr: (pr[0], pr[1], i, 0))]
        + [slot(k) for k in range(1, N_DEVICES)],
        out_specs=pl.BlockSpec((None, tr, c), lambda i, pr: (pr[1], i, 0)))
    return pl.pallas_call(body, name=name, grid_spec=grid_spec,
                          out_shape=jax.ShapeDtypeStruct((2, hr, c), F32),
                          compiler_params=_params(1))(place, own, *[slots] * (N_DEVICES - 1))


def _adamw(name, w, g, m, v):
    rows, cols = w.shape
    tr = _row_tile(rows, cols * 4, 1024 * 1024)

    def body(w_ref, g_ref, m_ref, v_ref, d_ref, nm_ref, nv_ref):
        d_ref[...], nm_ref[...], nv_ref[...] = _adamw_math(w_ref[...], g_ref[...], m_ref[...], v_ref[...])

    spec = pl.BlockSpec((tr, cols), lambda i: (i, 0))
    return pl.pallas_call(
        body, name=name, grid=(rows // tr,), in_specs=[spec] * 4, out_specs=[spec] * 3,
        out_shape=[jax.ShapeDtypeStruct((rows, cols), F32)] * 3, compiler_params=_params(1))(w, g, m, v)


def _adamw_math(w, g, m, v):
    nm = ADAM_B1 * m + (1.0 - ADAM_B1) * g
    nv = ADAM_B2 * v + (1.0 - ADAM_B2) * jnp.square(g)
    m_hat = nm / (1.0 - ADAM_B1 ** ADAM_STEP)
    v_hat = nv / (1.0 - ADAM_B2 ** ADAM_STEP)
    return -ADAM_LR * (m_hat / (jnp.sqrt(v_hat) + ADAM_EPS) + ADAM_WD * w), nm, nv


def _adamw_layers(name, w, grads, m, v):
    L, r, c = w.shape
    tr = _row_tile(r, L * c * 4, 1024 * 1024)

    def body(*refs):
        w_ref, m_ref, v_ref = refs[:3]
        g_refs = refs[3:3 + L]
        go_ref, d_ref, nm_ref, nv_ref = refs[3 + L:]
        for l in range(L):
            g = g_refs[l][...]
            go_ref[l] = g
            d_ref[l], nm_ref[l], nv_ref[l] = _adamw_math(w_ref[l], g, m_ref[l], v_ref[l])

    stacked = pl.BlockSpec((L, tr, c), lambda i: (0, i, 0))
    return pl.pallas_call(
        body, name=name, grid=(r // tr,),
        in_specs=[stacked] * 3 + [pl.BlockSpec((tr, c), lambda i: (i, 0))] * L, out_specs=[stacked] * 4,
        out_shape=[jax.ShapeDtypeStruct((L, r, c), F32)] * 4, compiler_params=_params(1))(w, m, v, *grads)


def _place():
    x, y, c = lax.axis_index("x"), lax.axis_index("y"), lax.axis_index("c")
    chips = [(1 - x, y), (x, 1 - y), (1 - x, 1 - y)]
    return x, y, c, chips


HBM = pl.BlockSpec(memory_space=pltpu.HBM)
SEM = pl.BlockSpec(memory_space=pltpu.SEMAPHORE)
EFFECT = pltpu.SideEffectType.DATAFLOW_SIDE_EFFECTING


class _Copy:
    def __init__(self, src, src_view, land, dst_view, recv_view, target):
        self.src, self.src_view, self.land, self.dst_view, self.recv_view, self.target = (
            src, src_view, land, dst_view, recv_view, target)


def _whole(ref, place):
    return ref


def _split_start(name, srcs, land_shapes, plans):
    skeys, lkeys = list(srcs), list(land_shapes)
    ns, nl, ng = len(skeys), len(lkeys), len(plans)

    def body(*refs):
        src = dict(zip(skeys, refs[:ns]))
        land = dict(zip(lkeys, refs[ns:ns + nl]))
        sems = refs[ns + nl:ns + nl + 2 * ng]
        token = refs[-1]
        place = _place()
        for gi, plan in enumerate(plans):
            for k, cp in enumerate(plan):
                dst = land[cp.land] if cp.land in land else src[cp.land]
                pltpu.make_async_remote_copy(
                    src_ref=cp.src_view(src[cp.src], place), dst_ref=cp.dst_view(dst, place),
                    send_sem=sems[2 * gi].at[k], recv_sem=sems[2 * gi + 1].at[k],
                    device_id=cp.target(place), device_id_type=MESH).start()
        token[...] = jnp.zeros_like(token)

    sem_shapes = []
    for plan in plans:
        sem_shapes += [pltpu.SemaphoreType.DMA((len(plan),))] * 2
    buffers = [srcs[k] for k in skeys] + [lax.empty(land_shapes[k].shape, land_shapes[k].dtype) for k in lkeys]
    outs = pl.pallas_call(
        body, name=name,
        out_shape=(*sem_shapes, *[pltpu.HBM(a.shape, a.dtype) for a in buffers], jax.ShapeDtypeStruct((8, LANES), F32)),
        in_specs=[HBM] * (ns + nl),
        out_specs=(*[SEM] * (2 * ng), *[HBM] * (ns + nl), pl.BlockSpec(memory_space=pltpu.VMEM)),
        input_output_aliases={i: 2 * ng + i for i in range(ns + nl)},
        compiler_params=pltpu.CompilerParams(has_side_effects=EFFECT),
    )(*[pltpu.with_memory_space_constraint(a, pltpu.HBM) for a in buffers])
    sems = [(outs[2 * gi], outs[2 * gi + 1]) for gi in range(ng)]
    thru = outs[2 * ng:2 * ng + ns + nl]
    return sems, dict(zip(skeys, thru[:ns])), dict(zip(lkeys, thru[ns:])), outs[-1]


def _split_wait(name, sems, srcs, lands, plan, after):
    skeys, lkeys = list(srcs), list(lands)
    ns, nl = len(skeys), len(lkeys)

    def body(*refs):
        src = dict(zip(skeys, refs[:ns]))
        land = dict(zip(lkeys, refs[ns:ns + nl]))
        ssem, rsem = refs[ns + nl], refs[ns + nl + 1]
        place = _place()
        for k, cp in enumerate(plan):
            dst = land[cp.land] if cp.land in land else src[cp.land]
            pltpu.make_async_remote_copy(
                src_ref=cp.src_view(src[cp.src], place), dst_ref=cp.dst_view(dst, place),
                send_sem=ssem.at[k], recv_sem=rsem.at[k],
                device_id=cp.target(place), device_id_type=MESH).wait_send()
            got = cp.recv_view(dst, place)
            pltpu.make_async_remote_copy(
                src_ref=got, dst_ref=got, send_sem=ssem.at[k], recv_sem=rsem.at[k],
                device_id=cp.target(place), device_id_type=MESH).wait_recv()

    buffers = [srcs[k] for k in skeys] + [lands[k] for k in lkeys]
    outs = pl.pallas_call(
        body, name=name, out_shape=tuple(pltpu.HBM(a.shape, a.dtype) for a in buffers),
        in_specs=(*[HBM] * (ns + nl), SEM, SEM, ANY), out_specs=tuple([HBM] * (ns + nl)),
        input_output_aliases={i: i for i in range(ns + nl)},
        compiler_params=pltpu.CompilerParams(has_side_effects=EFFECT),
    )(*buffers, sems[0], sems[1], after)
    return dict(zip(skeys, outs[:ns])), dict(zip(lkeys, outs[ns:]))


def _chip_of(place):
    x, y, c, chips = place
    return 2 * x + y


GATHER_FIRST = 2


class _WeightGather:
    def __init__(self, blocks):
        self.plans, shapes = {}, {}
        for key, a in blocks.items():
            shapes[key] = jax.ShapeDtypeStruct((N_CHIPS,) + a.shape, a.dtype)
            slot = lambda ref, place: ref.at[_chip_of(place)]
            plan = [_Copy(key, _whole, key, slot,
                          lambda ref, place, k=k: ref.at[2 * place[3][k][0] + place[3][k][1]],
                          lambda place, k=k: (place[3][k][0], place[3][k][1], place[2])) for k in range(3)]
            plan.append(_Copy(key, _whole, key, slot, slot, lambda place: (place[0], place[1], 1 - place[2])))
            self.plans[key] = plan
        keys = list(blocks)
        self.sems, self.srcs, self.lands = {}, {}, {}
        for name, part in (("gather_start_first", keys[:GATHER_FIRST]), ("gather_start", keys[GATHER_FIRST:])):
            sems, srcs, lands, self.token = _split_start(name, {k: blocks[k] for k in part}, {k: shapes[k] for k in part},
                                                         [self.plans[k] for k in part])
            self.sems.update(zip(part, sems))
            self.srcs.update(srcs)
            self.lands.update(lands)

    def get(self, l, name, after):
        key = (l, name)
        _, lands = _split_wait(f"gather_wait_{name}{l}", self.sems[key], {key: self.srcs[key]},
                               {key: self.lands[key]}, self.plans[key], after)
        return lands[key][:, None]


class _GradReduce:
    def __init__(self, place):
        self.place = place
        self.jobs = []
        self.done = {}
        self.n = 0

    def submit(self, grads):
        views = {k: a.reshape(N_CHIPS, 2, a.shape[1] // 2, a.shape[2]) for k, a in grads.items()}
        shapes = {k: jax.ShapeDtypeStruct((N_DEVICES,) + a.shape[2:], a.dtype) for k, a in views.items()}

        def peer(place, k):
            x, y, c, _ = place
            return (1 - x if k & 4 else x, 1 - y if k & 2 else y, 1 - c if k & 1 else c)

        def index(dev):
            return 4 * dev[0] + 2 * dev[1] + dev[2]

        plan = []
        for key in views:
            for k in range(1, N_DEVICES):
                plan.append(_Copy(
                    key, lambda ref, place, k=k: ref.at[2 * peer(place, k)[0] + peer(place, k)[1], peer(place, k)[2]],
                    key, lambda ref, place: ref.at[index(place[:3])],
                    lambda ref, place, k=k: ref.at[index(peer(place, k))],
                    lambda place, k=k: peer(place, k)))
        sems, srcs, lands, token = _split_start(f"grad_start{self.n}", views, shapes, [plan])
        self.jobs.append(dict(id=self.n, sems=sems[0], srcs=srcs, lands=lands, plan=plan))
        self.n += 1
        return token

    def pump(self, after):
        return []

    def finish(self, after):
        for job in self.jobs:
            srcs, lands = _split_wait(f"grad_wait{job['id']}", job["sems"], job["srcs"], job["lands"], job["plan"],
                                      after)
            for i, k in enumerate(srcs):
                self.done[k] = _device_add(f"grad_add{job['id']}_{i}", srcs[k], lands[k], self.place)
        self.jobs = []
        return self.done


class _PairShare:
    def __init__(self, halves, types):
        sibling = lambda place: (place[0], place[1], 1 - place[2])
        mine = lambda ref, place: ref.at[place[2]]
        theirs = lambda ref, place: ref.at[1 - place[2]]
        self.plans = {t: [_Copy(k, mine, k, mine, theirs, sibling) for k in halves if k[0] == t] for t in types}
        sems, self.bufs, _, self.token = _split_start("share_start", halves, {}, list(self.plans.values()))
        self.sems = dict(zip(self.plans, sems))

    def get(self, t, after):
        keys = [cp.src for cp in self.plans[t]]
        bufs, _ = _split_wait(f"share_wait_{t}", self.sems[t], {k: self.bufs[k] for k in keys}, {}, self.plans[t], after)
        return bufs


def _small_allreduce(part):
    R, C = part.shape
    N_DEV = 8

    def body(in_ref, out_ref, slots, ssem, rsem):
        x, y, c, _ = _place()
        me = 4 * x + 2 * y + c
        sends = []
        for k in range(1, N_DEV):
            kx, ky, kc = (k >> 2) & 1, (k >> 1) & 1, k & 1
            peer = (1 - x if kx else x, 1 - y if ky else y, 1 - c if kc else c)
            cp = pltpu.make_async_remote_copy(
                src_ref=in_ref, dst_ref=slots.at[me], send_sem=ssem.at[k], recv_sem=rsem.at[k],
                device_id=peer, device_id_type=MESH)
            cp.start()
            sends.append(cp)
        slots[me] = in_ref[...]
        for k in range(1, N_DEV):
            kx, ky, kc = (k >> 2) & 1, (k >> 1) & 1, k & 1
            peer = (1 - x if kx else x, 1 - y if ky else y, 1 - c if kc else c)
            slot = slots.at[4 * peer[0] + 2 * peer[1] + peer[2]]
            pltpu.make_async_remote_copy(
                src_ref=slot, dst_ref=slot, send_sem=ssem.at[k], recv_sem=rsem.at[k],
                device_id=peer, device_id_type=MESH).wait_recv()
        acc = slots[0]
        for d in range(1, N_DEV):
            acc = acc + slots[d]
        out_ref[...] = acc
        for cp in sends:
            cp.wait_send()

    vm = pl.BlockSpec(memory_space=pltpu.VMEM)
    return pl.pallas_call(
        body, name="small_allreduce", in_specs=[vm], out_specs=vm,
        out_shape=jax.ShapeDtypeStruct((R, C), F32),
        scratch_shapes=[pltpu.VMEM((N_DEV, R, C), F32), pltpu.SemaphoreType.DMA((N_DEV,)),
                        pltpu.SemaphoreType.DMA((N_DEV,))])(part)


def _local_step(x, target, norm_mix, norm_mlp, norm_kv, norm_final, weights, sink, n_a, n_heads):
    B, S, D = x.shape
    T = B * S
    C = n_heads * HEAD_DIM
    depth = norm_mix.shape[0]
    slopes = 2.0 ** (-ALIBI_MAX_BIAS * jnp.arange(1, n_heads + 1, dtype=F32) / n_heads)
    tm = min(512, T)
    row = lambda v: v.reshape(1, -1)

    h = x.reshape(T, D)
    saved, Wl = [], []
    kv = nkv = h_kv = cwg = None
    for l in range(depth):
        s = {"h_in": h}
        w = {}
        Wl.append(w)
        if l < n_a:
            w["w_a_in"] = weights.get(l, "w_a_in", h)
            first = [weights.token] if l == 0 and hasattr(weights, "token") else []
            s["n1"], bcu = _norm_mm(f"a_in_fwd{l}", h, row(norm_mix[l]), w["w_a_in"], 0, 3, BF16, tm, first)
            s["bcu"] = bcu.reshape(3, B, S, D)
            if l == 0:
                cwg = weights.get(0, "conv", bcu)[:, 0, :n_a * 3].reshape(N_CHIPS, n_a, 3, -1)
            s["z"] = _conv_fwd(f"conv_fwd{l}", s["bcu"], cwg, l, LANES).reshape(T, D)
            w["w_a_out"] = weights.get(l, "w_a_out", s["z"])
            h = _mm_res_rows(f"a_out_fwd{l}", s["z"], w["w_a_out"], 0, h, _to_bf16, tm)
        else:
            i = l - n_a
            if i == 0:
                h_kv = h
                w["w_kv"] = weights.get(l, "w_kv", h)
                nkv, kv = _norm_mm("kv_fwd", h, row(norm_kv), w["w_kv"], 0, 1, F32, tm)
                kv = kv.reshape(B, S, 2 * 3 * C)
            w["w_q"] = weights.get(l, "w_q", h)
            s["n1"], q = _norm_mm(f"q_fwd{i}", h, row(norm_mix[l]), w["w_q"], 0, 1, F32, tm)
            s["q"] = q.reshape(B, S, 3 * C)
            o, lse = _attn_fwd(f"attn_fwd{i}", s["q"], kv, slopes, n_heads)
            s["o"], s["lse"] = o.reshape(T, C), lse.reshape(T, C)
            w["w_o"] = weights.get(l, "w_o", o)
            h = _mm_res_cols(f"o_fwd{i}", s["o"], w["w_o"], 0, h, tm)
        s["h_mid"] = h
        w["w_up"] = weights.get(l, "w_up", h)
        if l == 0:
            s["n2"], a = _norm_mm("up_fwd0", h, row(norm_mlp[l]), w["w_up"], 0, 1, BF16, tm)
            s["a"] = a[0]
            w["w_down"] = weights.get(l, "w_down", a)
            h = _mm_res_rows("down_fwd0", s["a"], w["w_down"], 0, h, _relu2_bf16, tm)
        else:
            w["w_down"] = weights.get(l, "w_down", h)
            s["n2"], s["a"], h = _mlp_fwd(f"mlp_fwd{l}", h, row(norm_mlp[l]), w["w_up"], w["w_down"], tm)
        F = s["a"].shape[1]
        saved.append(s)

    loss, dh, dh16, dg_final = _final_loss("loss_head", h, row(norm_final), target.reshape(T, D), tm)

    g_mix, g_mlp = [None] * depth, [None] * depth
    g_conv = [None] * n_a
    dkv = None
    tt = min(512, T)
    deps = []
    for l in reversed(range(depth)):
        s, w = saved[l], Wl[l]
        g_down = _tn(f"down_wgrad{l}", s["a"], _relu2_bf16, [_seg2d(dh16, tt, 2)], None, False,
                     min(2048, F), tt, deps, BF16).reshape(N_CHIPS, F // N_CHIPS, D)
        da, dh, dh16, g_mlp[l] = _mlp_bwd(f"mlp_bwd{l}", dh, dh16, s["a"], w["w_down"], w["w_up"], s["h_mid"],
                                          row(norm_mlp[l]), tm)
        g_up = _tn(f"up_wgrad{l}", s["n2"], _to_bf16, [_seg2d(da, tt, 2)], F // N_CHIPS, True, D, tt, (), BF16)
        deps = sink.pump(dh) + [sink.submit({("w_up", l): g_up, ("w_down", l): g_down})]
        if l < n_a:
            g_out = _tn(f"a_out_wgrad{l}", s["z"], _to_bf16, [_seg2d(dh16, tt, 2)], None, False,
                        D, tt, deps, BF16).reshape(N_CHIPS, D // N_CHIPS, D)
            dz = _nt_rows(f"a_out_bwd{l}", dh16, w["w_a_out"], 0, None, F32, tm)
            deps = sink.pump(dz) + [sink.submit({("w_a_out", l): g_out})]
            dbcu, g_conv[l] = _conv_bwd(f"conv_bwd{l}", s["bcu"], dz.reshape(B, S, D), cwg, l, LANES)
            dbcu = dbcu.reshape(3, T, D)
            g_in = _tn(f"a_in_wgrad{l}", s["n1"], _to_bf16, [_seg_plane(dbcu, p, tt, 2) for p in range(3)],
                       3 * D // N_CHIPS, True, D, tt, deps, BF16)
            dh, dh16, g_mix[l] = _nt_cols(f"a_in_bwd{l}", [_seg_plane(dbcu, p, tm, 1) for p in range(3)],
                                          w["w_a_in"], 0, tm, (s["h_in"], row(norm_mix[l]), dh))
            mixer = {("w_a_in", l): g_in}
        else:
            i = l - n_a
            g_o = _tn(f"o_wgrad{i}", s["o"], _to_bf16, [_seg2d(dh16, tt, 2)], D // N_CHIPS, True, C, tt, deps,
                      BF16)
            do = _nt_cols(f"o_bwd{i}", [_seg2d(dh16, tm, 1)], w["w_o"], 0, tm, None)
            deps = sink.pump(do) + [sink.submit({("w_o", i): g_o})]
            dq, dk, dv = _attn_bwd(f"attn_bwd{i}", s["q"], kv, slopes, s["o"].reshape(B, S, C),
                                   s["lse"].reshape(B, S, C), do.reshape(B, S, C), n_heads, dkv)
            dkv = (dk, dv)
            dq = dq.reshape(T, 3 * C)
            g_q = _tn(f"q_wgrad{i}", s["n1"], _to_bf16, [_seg2d(dq, tt, 2)], 3 * C // N_CHIPS, True, D, tt, deps,
                      BF16)
            dh, dh16, g_mix[l] = _nt_cols(f"q_bwd{i}", [_seg2d(dq, tm, 1)], w["w_q"], 0, tm,
                                          (s["h_in"], row(norm_mix[l]), dh))
            mixer = {("w_q", i): g_q}
            if i == 0:
                dk2, dv2 = (t.reshape(T, 3 * C) for t in dkv)
                mixer[("w_kv", 0)] = _tn("kv_wgrad", nkv, _to_bf16, _kv_segments(dk2, dv2, C, tt, 2),
                                         6 * C // N_CHIPS, True, D, tt, (), BF16)
                dh, dh16, g_kv = _nt_cols("kv_bwd", _kv_segments(dk2, dv2, C, tm, 1), w["w_kv"], 0, tm,
                                          (h_kv, row(norm_kv), dh))
        deps = sink.pump(dh) + [sink.submit(mixer)]
    small = dict(norm_mix=jnp.concatenate(g_mix, axis=0), norm_mlp=jnp.concatenate(g_mlp, axis=0),
                 norm_kv=g_kv, norm_final=dg_final, conv_w=jnp.stack(g_conv))
    return loss, dh.reshape(B, S, D), small


BIG = ("w_a_in", "w_a_out", "w_kv", "w_q", "w_o", "w_up", "w_down")
CONV_PAD_ROWS = 16


def kernel(x, norm_mix, norm_mlp, w_a_in, conv_w, w_a_out, norm_kv, w_kv, w_q, w_o, w_up, w_down, norm_final, loss_target, m_norm_mix, m_norm_mlp, m_w_a_in, m_conv_w, m_w_a_out, m_norm_kv, m_w_kv, m_w_q, m_w_o, m_w_up, m_w_down, m_norm_final, v_norm_mix, v_norm_mlp, v_w_a_in, v_conv_w, v_w_a_out, v_norm_kv, v_w_kv, v_w_q, v_w_o, v_w_up, v_w_down, v_norm_final):
    D = x.shape[-1]
    w = dict(norm_mix=norm_mix, norm_mlp=norm_mlp, w_a_in=w_a_in, conv_w=conv_w, w_a_out=w_a_out, norm_kv=norm_kv,
             w_kv=w_kv[None], w_q=w_q, w_o=w_o, w_up=w_up, w_down=w_down, norm_final=norm_final)
    m = dict(norm_mix=m_norm_mix, norm_mlp=m_norm_mlp, w_a_in=m_w_a_in, conv_w=m_conv_w, w_a_out=m_w_a_out,
             norm_kv=m_norm_kv, w_kv=m_w_kv[None], w_q=m_w_q, w_o=m_w_o, w_up=m_w_up, w_down=m_w_down,
             norm_final=m_norm_final)
    v = dict(norm_mix=v_norm_mix, norm_mlp=v_norm_mlp, w_a_in=v_w_a_in, conv_w=v_conv_w, w_a_out=v_w_a_out,
             norm_kv=v_norm_kv, w_kv=v_w_kv[None], w_q=v_w_q, w_o=v_w_o, w_up=v_w_up, w_down=v_w_down,
             norm_final=v_norm_final)
    depth = norm_mix.shape[0]
    n_a, taps, cwc = conv_w.shape
    n_heads = w_o.shape[1] // HEAD_DIM

    conv_rows = jnp.zeros((CONV_PAD_ROWS, cwc), F32).at[:n_a * taps].set(conv_w.reshape(n_a * taps, cwc))
    blocks = {}
    for l in range(depth):
        if l < n_a:
            blocks[(l, "w_a_in")] = w_a_in[l].astype(BF16)
            if l == 0:
                blocks[(0, "conv")] = conv_rows
            blocks[(l, "w_a_out")] = w_a_out[l].astype(BF16)
        else:
            if l == n_a:
                blocks[(l, "w_kv")] = w_kv.astype(BF16)
            blocks[(l, "w_q")] = w_q[l - n_a].astype(BF16)
            blocks[(l, "w_o")] = w_o[l - n_a].astype(BF16)
        blocks[(l, "w_up")] = w_up[l].astype(BF16)
        blocks[(l, "w_down")] = w_down[l].astype(BF16)
    weights = _WeightGather(blocks)
    place = jnp.stack([2 * lax.axis_index("x") + lax.axis_index("y"), lax.axis_index("c")]).astype(jnp.int32)
    sink = _GradReduce(place)

    loss, grad_x, small = _local_step(x, loss_target, norm_mix, norm_mlp, norm_kv, norm_final, weights, sink,
                                      n_a, n_heads)
    loss = lax.psum(loss[0, 0], ("x", "y", "c"))

    share = _PairShare(sink.finish(grad_x), BIG)
    grads = {}

    packed = jnp.concatenate([small["norm_mix"], small["norm_mlp"], small["norm_kv"], small["norm_final"],
                              small["conv_w"].reshape(n_a * taps, D)], axis=0)
    pad = (-packed.shape[0]) % 8
    packed = jnp.pad(packed, ((0, pad), (0, 0)))
    total = _small_allreduce(packed)
    grads["norm_mix"] = total[:depth]
    grads["norm_mlp"] = total[depth:2 * depth]
    grads["norm_kv"] = total[2 * depth]
    grads["norm_final"] = total[2 * depth + 1]
    chip = 2 * lax.axis_index("x") + lax.axis_index("y")
    conv_full = total[2 * depth + 2:2 * depth + 2 + n_a * taps].reshape(n_a, taps, N_CHIPS, cwc)
    grads["conv_w"] = lax.dynamic_index_in_dim(conv_full, chip, axis=2, keepdims=False)

    order = ("norm_mix", "norm_mlp", "w_a_in", "conv_w", "w_a_out", "norm_kv", "w_kv", "w_q", "w_o", "w_up",
             "w_down", "norm_final")
    delta, new_m, new_v = {}, {}, {}
    vec_names = ("norm_mix", "norm_mlp", "norm_kv", "norm_final")
    rows_of = lambda a: a.reshape(-1, D)
    vw, vg, vm_, vv = (jnp.concatenate([rows_of(t[k]) for k in vec_names], axis=0) for t in (w, grads, m, v))
    vpad = (-vw.shape[0]) % 8
    padrows = lambda a: jnp.pad(a, ((0, vpad), (0, 0)))
    vd, vnm, vnv = _adamw("adamw_norms", padrows(vw), padrows(vg), padrows(vm_), padrows(vv))
    off = 0
    for k in vec_names:
        r = rows_of(w[k]).shape[0]
        delta[k] = vd[off:off + r].reshape(w[k].shape)
        new_m[k] = vnm[off:off + r].reshape(w[k].shape)
        new_v[k] = vnv[off:off + r].reshape(w[k].shape)
        off += r
    cpad = (-n_a * taps) % 8
    two_d = lambda a: jnp.pad(a.reshape(-1, cwc), ((0, cpad), (0, 0)))
    cd, cnm, cnv = _adamw("adamw_conv_w", two_d(w["conv_w"]), two_d(grads["conv_w"]), two_d(m["conv_w"]),
                          two_d(v["conv_w"]))
    delta["conv_w"], new_m["conv_w"], new_v["conv_w"] = (t[:n_a * taps].reshape(conv_w.shape) for t in (cd, cnm, cnv))
    after = cd
    for k in sorted(BIG, key=lambda k: w[k].size):
        shared = share.get(k, after)
        per_layer = [shared[(k, l)].reshape(w[k].shape[1:]) for l in range(w[k].shape[0])]
        grads[k], delta[k], new_m[k], new_v[k] = _adamw_layers(f"adamw_{k}", w[k], per_layer, m[k], v[k])
        after = delta[k]
    fix = lambda k, a: a[0] if k == "w_kv" else a
    return (loss, grad_x, *[fix(k, grads[k]) for k in order], *[fix(k, delta[k]) for k in order],
            *[fix(k, new_m[k]) for k in order], *[fix(k, new_v[k]) for k in order])
```

```python
import jax
import jax.numpy as jnp
from jax import lax
from jax.experimental import pallas as pl
from jax.experimental.pallas import tpu as pltpu

F32 = jnp.float32
BF16 = jnp.bfloat16
MESH = pl.DeviceIdType.MESH

EPS = 1e-5
PATTERNS = ((128, 1), (512, 4), (2048, 16))
HEAD_DIM = 64
ALIBI_MAX_BIAS = 8.0
NEG_INF = -1e30
ATT_BLK = 128
BWD_UNROLL = 16
N_CHIPS = 4
LANES = 128
VMEM_LIMIT = 56 * 1024 * 1024

ADAM_LR = 0.001
ADAM_B1 = 0.9
ADAM_B2 = 0.999
ADAM_EPS = 1e-08
ADAM_WD = 0.01
ADAM_STEP = 10


ANY = pl.BlockSpec(memory_space=pl.ANY)


def _params(n_grid_axes):
    return pltpu.CompilerParams(dimension_semantics=("arbitrary",) * n_grid_axes, vmem_limit_bytes=VMEM_LIMIT)


def _dot(a, b):
    return jnp.dot(a, b, preferred_element_type=F32)


def _dot_nt(a, b):
    return lax.dot_general(a, b, (((1,), (1,)), ((), ())), preferred_element_type=F32)


def _dot_tn(a, b):
    return lax.dot_general(a, b, (((0,), (0,)), ((), ())), preferred_element_type=F32)


def _relu2(a):
    return jnp.square(jnp.maximum(a, 0.0))


def _rms(hf, g):
    y = hf * lax.rsqrt(jnp.mean(hf * hf, axis=-1, keepdims=True) + EPS)
    return y * g


def _rms_bwd(hf, g, dn):
    rstd = lax.rsqrt(jnp.mean(hf * hf, axis=-1, keepdims=True) + EPS)
    xhat = hf * rstd
    dg = jnp.sum(dn * xhat, axis=0, keepdims=True)
    dx = dn * g
    dh = rstd * (dx - xhat * jnp.mean(dx * xhat, axis=-1, keepdims=True))
    return dh, dg


def _pieces(seg_widths, chunk_width, max_width):
    total = sum(seg_widths)
    cuts = {0, total}
    acc = 0
    for w in seg_widths:
        cuts.add(acc)
        acc += w
    cuts.update(range(0, total, chunk_width))
    cuts = sorted(cuts)
    fine = []
    for lo, hi in zip(cuts[:-1], cuts[1:]):
        while hi - lo > max_width:
            fine.append((lo, lo + max_width))
            lo += max_width
        fine.append((lo, hi))
    out = []
    for lo, hi in fine:
        acc = 0
        for s, w in enumerate(seg_widths):
            if lo < acc + w:
                break
            acc += w
        out.append((s, lo - acc, lo // chunk_width, lo % chunk_width, hi - lo))
    return out


def _relu2_bf16(a):
    return _relu2(a.astype(F32)).astype(BF16)


def _to_bf16(a):
    return a.astype(BF16)


def _norm_mm(name, h, g, wg, layer, planes, out_dtype, tm, deps=()):
    T, D = h.shape
    cw = wg.shape[3]
    N = N_CHIPS * cw
    pw = N // planes
    pieces = _pieces([pw] * planes, cw, 512)

    def body(h_ref, g_ref, w_ref, *rest):
        n_ref, o_ref = rest[len(deps):]
        n = _rms(h_ref[...], g_ref[...]).astype(BF16)
        n_ref[...] = n
        for s, a0, ch, b0, wd in pieces:
            o_ref[s, :, a0:a0 + wd] = _dot(n, w_ref[ch, :, b0:b0 + wd]).astype(out_dtype)

    return pl.pallas_call(
        body, name=name, grid=(T // tm,),
        in_specs=[pl.BlockSpec((tm, D), lambda i: (i, 0)),
                  pl.BlockSpec((1, D), lambda i: (0, 0)),
                  pl.BlockSpec((N_CHIPS, None, D, cw), lambda i: (0, layer, 0, 0))] + [ANY] * len(deps),
        out_specs=[pl.BlockSpec((tm, D), lambda i: (i, 0)),
                   pl.BlockSpec((planes, tm, pw), lambda i: (0, i, 0))],
        out_shape=[jax.ShapeDtypeStruct((T, D), BF16), jax.ShapeDtypeStruct((planes, T, pw), out_dtype)],
        compiler_params=_params(1))(h, g, wg, *deps)


def _resident(shape, index_map):
    return pl.BlockSpec(shape, index_map, pipeline_mode=pl.Buffered(1))


def _mm_res_rows(name, a, wg, layer, h, act, tm):
    T = a.shape[0]
    rk, D = wg.shape[2], wg.shape[3]

    def body(a_ref, w_ref, h_ref, o_ref):
        acc = h_ref[...]
        for k in range(N_CHIPS):
            acc = acc + _dot(act(a_ref[:, k * rk:(k + 1) * rk]), w_ref[k])
        o_ref[...] = acc

    return pl.pallas_call(
        body, name=name, grid=(T // tm,),
        in_specs=[pl.BlockSpec((tm, N_CHIPS * rk), lambda i: (i, 0)),
                  pl.BlockSpec((N_CHIPS, None, rk, D), lambda i: (0, layer, 0, 0)),
                  pl.BlockSpec((tm, D), lambda i: (i, 0))],
        out_specs=pl.BlockSpec((tm, D), lambda i: (i, 0)),
        out_shape=jax.ShapeDtypeStruct((T, D), F32),
        compiler_params=_params(1))(a, wg, h)


def _mm_res_cols(name, a, wg, layer, h, tm):
    T, K = a.shape
    cw = wg.shape[3]
    D = N_CHIPS * cw

    def body(a_ref, w_ref, h_ref, o_ref):
        a16 = a_ref[...].astype(BF16)
        for j in range(N_CHIPS):
            o_ref[:, j * cw:(j + 1) * cw] = h_ref[:, j * cw:(j + 1) * cw] + _dot(a16, w_ref[j])

    return pl.pallas_call(
        body, name=name, grid=(T // tm,),
        in_specs=[pl.BlockSpec((tm, K), lambda i: (i, 0)),
                  pl.BlockSpec((N_CHIPS, None, K, cw), lambda i: (0, layer, 0, 0)),
                  pl.BlockSpec((tm, D), lambda i: (i, 0))],
        out_specs=pl.BlockSpec((tm, D), lambda i: (i, 0)),
        out_shape=jax.ShapeDtypeStruct((T, D), F32),
        compiler_params=_params(1))(a, wg, h)


def _mlp_fwd(name, h, g, wup, wdown, tm):
    T, D = h.shape
    cw = wup.shape[3]

    def body(h_ref, g_ref, wu_ref, wd_ref, n_ref, a_ref, o_ref):
        hf = h_ref[...]
        n = _rms(hf, g_ref[...]).astype(BF16)
        n_ref[...] = n
        acc = hf
        for ch in range(N_CHIPS):
            a16 = _dot(n, wu_ref[ch]).astype(BF16)
            a_ref[:, ch * cw:(ch + 1) * cw] = a16
            acc = acc + _dot(_relu2_bf16(a16), wd_ref[ch])
        o_ref[...] = acc

    row = pl.BlockSpec((tm, D), lambda i: (i, 0))
    return pl.pallas_call(
        body, name=name, grid=(T // tm,),
        in_specs=[row, pl.BlockSpec((1, D), lambda i: (0, 0)),
                  _resident((N_CHIPS, None, D, cw), lambda i: (0, 0, 0, 0)),
                  _resident((N_CHIPS, None, cw, D), lambda i: (0, 0, 0, 0))],
        out_specs=[row, pl.BlockSpec((tm, N_CHIPS * cw), lambda i: (i, 0)), row],
        out_shape=[jax.ShapeDtypeStruct((T, D), BF16), jax.ShapeDtypeStruct((T, N_CHIPS * cw), BF16),
                   jax.ShapeDtypeStruct((T, D), F32)],
        compiler_params=_params(1))(h, g, wup, wdown)


def _mlp_bwd(name, dh, dh16, a, wdown, wup, h_mid, g, tm, deps=()):
    T, D = dh.shape
    cw = wup.shape[3]
    F = N_CHIPS * cw

    def body(dh_ref, dh16_ref, a_ref, wd_ref, wu_ref, h_ref, g_ref, *rest):
        da_ref, out_ref, out16_ref, dg_ref = rest[len(deps):]
        d16 = dh16_ref[...]
        acc = None
        for ch in range(N_CHIPS):
            cols = slice(ch * cw, (ch + 1) * cw)
            da = (_dot_nt(d16, wd_ref[ch]) * (2.0 * jnp.maximum(a_ref[:, cols].astype(F32), 0.0))).astype(BF16)
            da_ref[:, cols] = da
            d = _dot_nt(da, wu_ref[ch])
            acc = d if acc is None else acc + d
        dh_c, dg = _rms_bwd(h_ref[...], g_ref[...], acc)
        out = dh_ref[...] + dh_c
        out_ref[...] = out
        out16_ref[...] = out.astype(BF16)

        @pl.when(pl.program_id(0) == 0)
        def _():
            dg_ref[...] = dg

        @pl.when(pl.program_id(0) > 0)
        def _():
            dg_ref[...] += dg

    row = pl.BlockSpec((tm, D), lambda i: (i, 0))
    wide = pl.BlockSpec((tm, F), lambda i: (i, 0))
    vec = pl.BlockSpec((1, D), lambda i: (0, 0))
    return pl.pallas_call(
        body, name=name, grid=(T // tm,),
        in_specs=[row, row, wide, _resident((N_CHIPS, None, cw, D), lambda i: (0, 0, 0, 0)),
                  _resident((N_CHIPS, None, D, cw), lambda i: (0, 0, 0, 0)), row, vec] + [ANY] * len(deps),
        out_specs=[wide, row, row, vec],
        out_shape=[jax.ShapeDtypeStruct((T, F), BF16), jax.ShapeDtypeStruct((T, D), F32),
                   jax.ShapeDtypeStruct((T, D), BF16), jax.ShapeDtypeStruct((1, D), F32)],
        compiler_params=_params(1))(dh, dh16, a, wdown, wup, h_mid, g, *deps)


CONV_ROWS = 256
CONV_HALO = 16
CONV_COLS = 2 * LANES


def _conv_shifted(ext, k, r0, rows):
    rolled = pltpu.roll(ext, k, 0)[CONV_HALO:]
    t = r0 + lax.broadcasted_iota(jnp.int32, rolled.shape, 0)
    return jnp.where(t >= k, rolled, 0.0)


def _conv_ahead(ext, k, r0, rows, S):
    rolled = pltpu.roll(ext, rows + CONV_HALO - k, 0)[:rows]
    t = r0 + lax.broadcasted_iota(jnp.int32, rolled.shape, 0)
    return jnp.where(t + k < S, rolled, 0.0)


def _conv_fwd(name, bcu, cwg, layer, tc):
    _, B, S, D = bcu.shape
    cwc = cwg.shape[3]
    per_chunk = cwc // tc
    R = min(CONV_ROWS, S)

    def body(x_ref, w_ref, z_ref):
        w = [w_ref[k:k + 1, :] for k in range(3)]

        def step(i, carry):
            r0 = pl.multiple_of(i * R, R)
            h0 = pl.multiple_of(jnp.maximum(r0 - CONV_HALO, 0), CONV_HALO)
            ld = lambda p, start, rows: x_ref[p, pl.ds(start, rows), :].astype(F32)
            cu = jnp.concatenate([ld(1, h0, CONV_HALO) * ld(2, h0, CONV_HALO), ld(1, r0, R) * ld(2, r0, R)], axis=0)
            conv = w[0] * cu[CONV_HALO:]
            conv = conv + w[1] * _conv_shifted(cu, 1, r0, R)
            conv = conv + w[2] * _conv_shifted(cu, 2, r0, R)
            z_ref[pl.ds(r0, R), :] = (ld(0, r0, R) * conv).astype(BF16)
            return carry

        lax.fori_loop(0, S // R, step, 0)

    return pl.pallas_call(
        body, name=name, grid=(B, D // tc),
        in_specs=[pl.BlockSpec((3, None, S, tc), lambda b, j: (0, b, 0, j)),
                  pl.BlockSpec((None, None, 3, tc), lambda b, j: (j // per_chunk, layer, 0, j % per_chunk))],
        out_specs=pl.BlockSpec((None, S, tc), lambda b, j: (b, 0, j)),
        out_shape=jax.ShapeDtypeStruct((B, S, D), BF16),
        compiler_params=_params(2))(bcu, cwg)


def _conv_bwd(name, bcu, dz, cwg, layer, tc):
    _, B, S, D = bcu.shape
    cwc = cwg.shape[3]
    per_chunk = cwc // tc
    R = min(CONV_ROWS, S)

    def body(x_ref, dz_ref, w_ref, d_ref, dw_ref):
        w = [w_ref[k:k + 1, :] for k in range(3)]

        @pl.when(pl.program_id(1) == 0)
        def _():
            dw_ref[...] = jnp.zeros_like(dw_ref)

        def step(i, carry):
            r0 = pl.multiple_of(i * R, R)
            h0 = pl.multiple_of(jnp.maximum(r0 - CONV_HALO, 0), CONV_HALO)
            a0 = pl.multiple_of(jnp.minimum(r0 + R, S - CONV_HALO), CONV_HALO)
            ld = lambda p, start, rows: x_ref[p, pl.ds(start, rows), :].astype(F32)
            b, c, u = ld(0, r0, R), ld(1, r0, R), ld(2, r0, R)
            dz = dz_ref[pl.ds(r0, R), :]
            cu = jnp.concatenate([ld(1, h0, CONV_HALO) * ld(2, h0, CONV_HALO), c * u], axis=0)
            cu1 = _conv_shifted(cu, 1, r0, R)
            cu2 = _conv_shifted(cu, 2, r0, R)
            conv = w[0] * (c * u) + w[1] * cu1 + w[2] * cu2
            dconv = dz * b
            dca = jnp.concatenate([dconv, dz_ref[pl.ds(a0, CONV_HALO), :] * ld(0, a0, CONV_HALO)], axis=0)
            dcu = w[0] * dconv + w[1] * _conv_ahead(dca, 1, r0, R, S) + w[2] * _conv_ahead(dca, 2, r0, R, S)
            d_ref[0, pl.ds(r0, R), :] = (dz * conv).astype(BF16)
            d_ref[1, pl.ds(r0, R), :] = (dcu * u).astype(BF16)
            d_ref[2, pl.ds(r0, R), :] = (dcu * c).astype(BF16)
            return (carry[0] + jnp.sum(dconv * (c * u), axis=0, keepdims=True),
                    carry[1] + jnp.sum(dconv * cu1, axis=0, keepdims=True),
                    carry[2] + jnp.sum(dconv * cu2, axis=0, keepdims=True))

        zero = jnp.zeros((1, tc), F32)
        s0, s1, s2 = lax.fori_loop(0, S // R, step, (zero, zero, zero))
        for k, sk in enumerate((s0, s1, s2)):
            dw_ref[k:k + 1, :] += sk

    return pl.pallas_call(
        body, name=name, grid=(D // tc, B),
        in_specs=[pl.BlockSpec((3, None, S, tc), lambda j, b: (0, b, 0, j)),
                  pl.BlockSpec((None, S, tc), lambda j, b: (b, 0, j)),
                  pl.BlockSpec((None, None, 3, tc), lambda j, b: (j // per_chunk, layer, 0, j % per_chunk))],
        out_specs=[pl.BlockSpec((3, None, S, tc), lambda j, b: (0, b, 0, j)),
                   pl.BlockSpec((3, tc), lambda j, b: (0, j))],
        out_shape=[jax.ShapeDtypeStruct((3, B, S, D), BF16), jax.ShapeDtypeStruct((3, D), F32)],
        compiler_params=_params(2))(bcu, dz, cwg)


def _att_rows(dil, idx, nb):
    r, n = idx // nb, idx % nb
    if dil == 1:
        cur = pl.ds(pl.multiple_of(n * ATT_BLK, ATT_BLK), ATT_BLK)
        prev = pl.ds(pl.multiple_of(jnp.maximum(n - 1, 0) * ATT_BLK, ATT_BLK), ATT_BLK)
    else:
        cur = pl.ds(n * (ATT_BLK * dil) + r, ATT_BLK, stride=dil)
        prev = pl.ds(jnp.maximum(n - 1, 0) * (ATT_BLK * dil) + r, ATT_BLK, stride=dil)
    return n, cur, prev


def _att_bias(bias_ref, dil, sl_ref, hp):
    row = lax.broadcasted_iota(jnp.int32, (2 * ATT_BLK, 2 * ATT_BLK), 0)
    ci = lax.broadcasted_iota(jnp.int32, (2 * ATT_BLK, 2 * ATT_BLK), 1)
    j = ATT_BLK + (row & (ATT_BLK - 1)) - ci
    slope = jnp.where(row < ATT_BLK, sl_ref[2 * hp], sl_ref[2 * hp + 1])
    rest = jnp.where((j >= 0) & (j <= ATT_BLK), -slope * (dil * j).astype(F32), NEG_INF)
    bias_ref[1] = rest
    bias_ref[0] = jnp.where(ci >= ATT_BLK, rest, NEG_INF)


def _stack_heads(x16, lane):
    first = lane < HEAD_DIM
    return jnp.concatenate([jnp.where(first, x16, jnp.zeros_like(x16)),
                            jnp.where(first, jnp.zeros_like(x16), x16)], axis=0)


def _per_head(col, lane):
    return jnp.where(lane < HEAD_DIM, col[:ATT_BLK], col[ATT_BLK:])


def _attn_fwd(name, q, kv, slopes, n_heads):
    B, S, CQ = q.shape
    HP = n_heads * HEAD_DIM // LANES
    scale = HEAD_DIM ** -0.5
    n_groups = len(PATTERNS)
    CH = 256

    def body(sl_ref, q_ref, k_ref, v_ref, o_ref, lse_ref, bias_ref, *parts):
        og, lg = parts[:n_groups], parts[n_groups:]
        hp, g = pl.program_id(1), pl.program_id(2)
        lane = lax.broadcasted_iota(jnp.int32, (1, LANES), 1)

        for gi, (window, dil) in enumerate(PATTERNS):
            nb = S // dil // ATT_BLK

            @pl.when(g == gi)
            def _(gi=gi, dil=dil, nb=nb):
                _att_bias(bias_ref, dil, sl_ref, hp)

                def step(idx, carry):
                    n, cur, prev = _att_rows(dil, idx, nb)
                    qs = _stack_heads((q_ref[cur, :] * scale).astype(BF16), lane)
                    kc = jnp.concatenate([k_ref[prev, :], k_ref[cur, :]], axis=0).astype(BF16)
                    vc = jnp.concatenate([v_ref[prev, :], v_ref[cur, :]], axis=0).astype(BF16)
                    s = _dot_nt(qs, kc) + bias_ref[jnp.minimum(n, 1)]
                    m = jnp.max(s, axis=-1, keepdims=True)
                    p = jnp.exp(s - m)
                    l = jnp.sum(p, axis=-1, keepdims=True)
                    p16 = p.astype(BF16)
                    o_un = _dot(jnp.concatenate([p16[:ATT_BLK], p16[ATT_BLK:]], axis=1), _stack_heads_rows(vc, lane))
                    og[gi][cur, :] = o_un / _per_head(l, lane)
                    lg[gi][cur, :] = _per_head(m + jnp.log(l), lane)
                    return carry

                lax.fori_loop(0, S // ATT_BLK, step, 0, unroll=16)

        @pl.when(g == n_groups - 1)
        def _():
            def comb(i, carry):
                rows = pl.ds(pl.multiple_of(i * CH, CH), CH)
                a, b, c = lg[0][rows, :], lg[1][rows, :], lg[2][rows, :]
                m = jnp.maximum(jnp.maximum(a, b), c)
                ea, eb, ec = jnp.exp(a - m), jnp.exp(b - m), jnp.exp(c - m)
                z = ea + eb + ec
                o_ref[rows, :] = (ea / z) * og[0][rows, :] + (eb / z) * og[1][rows, :] + (ec / z) * og[2][rows, :]
                lse_ref[rows, :] = m + jnp.log(z)
                return carry

            lax.fori_loop(0, S // CH, comb, 0)

    blk = (None, S, LANES)
    out = pl.BlockSpec(blk, lambda b, hp, g: (b, 0, hp))
    return pl.pallas_call(
        body, name=name, grid=(B, HP, n_groups),
        in_specs=[pl.BlockSpec(memory_space=pltpu.SMEM),
                  pl.BlockSpec(blk, lambda b, hp, g: (b, 0, g * HP + hp)),
                  pl.BlockSpec(blk, lambda b, hp, g: (b, 0, g * 2 * HP + hp)),
                  pl.BlockSpec(blk, lambda b, hp, g: (b, 0, g * 2 * HP + HP + hp))],
        out_specs=[out, out],
        out_shape=[jax.ShapeDtypeStruct((B, S, HP * LANES), F32)] * 2,
        scratch_shapes=[pltpu.VMEM((2, 2 * ATT_BLK, 2 * ATT_BLK), F32)] + [pltpu.VMEM((S, LANES), F32)] * (2 * n_groups),
        compiler_params=_params(3))(slopes, q, kv, kv)


def _stack_heads_rows(x16, lane):
    first = lane < HEAD_DIM
    return jnp.concatenate([jnp.where(first, x16, jnp.zeros_like(x16)),
                            jnp.where(first, jnp.zeros_like(x16), x16)], axis=0)


def _attn_bwd(name, q, kv, slopes, o, lse, do, n_heads, dkv_prev):
    B, S, CQ = q.shape
    HP = n_heads * HEAD_DIM // LANES
    scale = HEAD_DIM ** -0.5
    n_groups = len(PATTERNS)
    n_prev = 0 if dkv_prev is None else 2

    def body(sl_ref, q_ref, k_ref, v_ref, o_ref, lse_ref, do_ref, *rest):
        dq_ref, dk_ref, dv_ref, bias_ref = rest[n_prev:]
        hp, g = pl.program_id(1), pl.program_id(2)
        lane = lax.broadcasted_iota(jnp.int32, (1, LANES), 1)
        first = lane < HEAD_DIM

        def flush(rows, dk, dv):
            if n_prev:
                dk = dk + rest[0][rows, :]
                dv = dv + rest[1][rows, :]
            dk_ref[rows, :] = dk
            dv_ref[rows, :] = dv

        for gi, (window, dil) in enumerate(PATTERNS):
            nb = S // dil // ATT_BLK
            n_blocks = S // ATT_BLK

            @pl.when(g == gi)
            def _(dil=dil, nb=nb, n_blocks=n_blocks):
                _att_bias(bias_ref, dil, sl_ref, hp)

                def block(idx, carry, first_of_all):
                    n, cur, prev = _att_rows(dil, idx, nb)
                    qs = _stack_heads((q_ref[cur, :] * scale).astype(BF16), lane)
                    kc = jnp.concatenate([k_ref[prev, :], k_ref[cur, :]], axis=0).astype(BF16)
                    vc = jnp.concatenate([v_ref[prev, :], v_ref[cur, :]], axis=0).astype(BF16)
                    dob = do_ref[cur, :]
                    prod = dob * o_ref[cur, :]
                    lseb = lse_ref[cur, :]
                    dos = _stack_heads(dob.astype(BF16), lane)
                    delta = jnp.concatenate(
                        [jnp.sum(jnp.where(first, prod, 0.0), axis=-1, keepdims=True),
                         jnp.sum(jnp.where(first, 0.0, prod), axis=-1, keepdims=True)], axis=0)
                    lse_col = jnp.concatenate(
                        [jnp.max(jnp.where(first, lseb, -jnp.inf), axis=-1, keepdims=True),
                         jnp.max(jnp.where(first, -jnp.inf, lseb), axis=-1, keepdims=True)], axis=0)
                    s = _dot_nt(qs, kc) + bias_ref[jnp.minimum(n, 1)]
                    p = jnp.exp(s - lse_col)
                    ds = p * (_dot_nt(dos, vc) - delta)
                    ds16 = ds.astype(BF16)
                    dq = _dot(jnp.concatenate([ds16[:ATT_BLK], ds16[ATT_BLK:]], axis=1), _stack_heads_rows(kc, lane))
                    dq_ref[cur, :] = dq * scale
                    dk = _dot_tn(ds16, qs)
                    dv = _dot_tn(p.astype(BF16), dos)

                    def flush_before():
                        _, before, _ = _att_rows(dil, idx - 1, nb)
                        flush(before, carry[0] + dk[:ATT_BLK], carry[1] + dv[:ATT_BLK])

                    if first_of_all:
                        pl.when(idx > 0)(flush_before)
                    else:
                        flush_before()
                    return dk[ATT_BLK:], dv[ATT_BLK:]

                def step(i, carry):
                    for u in range(BWD_UNROLL):
                        carry = block(i * BWD_UNROLL + u, carry, u == 0)
                    return carry

                zero = jnp.zeros((ATT_BLK, LANES), F32)
                dk_last, dv_last = lax.fori_loop(0, n_blocks // BWD_UNROLL, step, (zero, zero))
                _, last, _ = _att_rows(dil, n_blocks - 1, nb)
                flush(last, dk_last, dv_last)

    blk = (None, S, LANES)
    shared = pl.BlockSpec(blk, lambda b, hp, g: (b, 0, hp))
    grouped = pl.BlockSpec(blk, lambda b, hp, g: (b, 0, g * HP + hp))
    prev = [] if dkv_prev is None else list(dkv_prev)
    gshape = jax.ShapeDtypeStruct((B, S, n_groups * HP * LANES), F32)
    return pl.pallas_call(
        body, name=name, grid=(B, HP, n_groups),
        in_specs=[pl.BlockSpec(memory_space=pltpu.SMEM), grouped,
                  pl.BlockSpec(blk, lambda b, hp, g: (b, 0, g * 2 * HP + hp)),
                  pl.BlockSpec(blk, lambda b, hp, g: (b, 0, g * 2 * HP + HP + hp)),
                  shared, shared, shared] + [grouped] * n_prev,
        out_specs=[grouped] * 3, out_shape=[gshape] * 3,
        scratch_shapes=[pltpu.VMEM((2, 2 * ATT_BLK, 2 * ATT_BLK), F32)],
        compiler_params=_params(3))(slopes, q, kv, kv, o, lse, do, *prev)


def _final_loss(name, h, g, target, tm):
    T, D = h.shape

    def body(h_ref, g_ref, t_ref, loss_ref, dh_ref, dh16_ref, dg_ref):
        hf = h_ref[...]
        gv = g_ref[...]
        rstd = lax.rsqrt(jnp.mean(hf * hf, axis=-1, keepdims=True) + EPS)
        xhat = hf * rstd
        err = xhat * gv - t_ref[...]
        part = 0.5 * jnp.sum(jnp.mean(err * err, axis=-1, keepdims=True), axis=0, keepdims=True)
        dy = err * (1.0 / D)
        dg = jnp.sum(dy * xhat, axis=0, keepdims=True)
        dx = dy * gv
        dh = rstd * (dx - xhat * jnp.mean(dx * xhat, axis=-1, keepdims=True))
        dh_ref[...] = dh
        dh16_ref[...] = dh.astype(BF16)

        @pl.when(pl.program_id(0) == 0)
        def _():
            loss_ref[...] = part
            dg_ref[...] = dg

        @pl.when(pl.program_id(0) > 0)
        def _():
            loss_ref[...] += part
            dg_ref[...] += dg

    return pl.pallas_call(
        body, name=name, grid=(T // tm,),
        in_specs=[pl.BlockSpec((tm, D), lambda i: (i, 0)), pl.BlockSpec((1, D), lambda i: (0, 0)),
                  pl.BlockSpec((tm, D), lambda i: (i, 0))],
        out_specs=[pl.BlockSpec((1, 1), lambda i: (0, 0)), pl.BlockSpec((tm, D), lambda i: (i, 0)),
                   pl.BlockSpec((tm, D), lambda i: (i, 0)), pl.BlockSpec((1, D), lambda i: (0, 0))],
        out_shape=[jax.ShapeDtypeStruct((1, 1), F32), jax.ShapeDtypeStruct((T, D), F32),
                   jax.ShapeDtypeStruct((T, D), BF16), jax.ShapeDtypeStruct((1, D), F32)],
        compiler_params=_params(1))(h, g, target)


def _nt_rows(name, dh, wg, layer, a_mul, out_dtype, tm, deps=()):
    T, D = dh.shape
    rk = wg.shape[2]
    N = N_CHIPS * rk
    with_a = a_mul is not None

    def body(dh_ref, w_ref, *rest):
        o_ref = rest[-1]
        d16 = dh_ref[...]
        for ch in range(N_CHIPS):
            r = _dot_nt(d16, w_ref[ch])
            if with_a:
                r = r * (2.0 * jnp.maximum(rest[0][:, ch * rk:(ch + 1) * rk].astype(F32), 0.0))
            o_ref[:, ch * rk:(ch + 1) * rk] = r.astype(out_dtype)

    in_specs = [pl.BlockSpec((tm, D), lambda i: (i, 0)),
                pl.BlockSpec((N_CHIPS, None, rk, D), lambda i: (0, layer, 0, 0))]
    args = [dh, wg]
    if with_a:
        in_specs.append(pl.BlockSpec((tm, N), lambda i: (i, 0)))
        args.append(a_mul)
    in_specs += [ANY] * len(deps)
    args += list(deps)
    return pl.pallas_call(
        body, name=name, grid=(T // tm,), in_specs=in_specs,
        out_specs=pl.BlockSpec((tm, N), lambda i: (i, 0)),
        out_shape=jax.ShapeDtypeStruct((T, N), out_dtype),
        compiler_params=_params(1))(*args)


def _nt_cols(name, ysegs, wg, layer, tm, norm):
    Nw, cw = wg.shape[2], wg.shape[3]
    widths = [bs[-1] for _, bs, _ in ysegs]
    pieces = _pieces(widths, cw, 1024)
    ns = len(ysegs)
    T = norm[0].shape[0] if norm is not None else ysegs[0][0].shape[-2]

    def body(*refs):
        y_refs = refs[:ns]
        w_ref = refs[ns]
        acc = refs[-1]
        for n, (s, a0, ch, b0, wd) in enumerate(pieces):
            d = _dot_nt(y_refs[s][:, a0:a0 + wd].astype(BF16), w_ref[ch, :, b0:b0 + wd])
            if n == 0:
                acc[...] = d
            else:
                acc[...] += d
        if norm is None:
            refs[ns + 1][...] = acc[...]
        else:
            h_ref, g_ref, dhin_ref, out_ref, out16_ref, dg_ref = refs[ns + 1:ns + 7]
            dh_c, dg = _rms_bwd(h_ref[...], g_ref[...], acc[...])
            dh = dhin_ref[...] + dh_c
            out_ref[...] = dh
            out16_ref[...] = dh.astype(BF16)

            @pl.when(pl.program_id(0) == 0)
            def _():
                dg_ref[...] = dg

            @pl.when(pl.program_id(0) > 0)
            def _():
                dg_ref[...] += dg

    in_specs = [pl.BlockSpec(bs, im) for _, bs, im in ysegs]
    in_specs.append(pl.BlockSpec((N_CHIPS, None, Nw, cw), lambda i: (0, layer, 0, 0)))
    args = [a for a, _, _ in ysegs] + [wg]
    row = pl.BlockSpec((tm, Nw), lambda i: (i, 0))
    vec = pl.BlockSpec((1, Nw), lambda i: (0, 0))
    if norm is None:
        out_specs = row
        out_shape = jax.ShapeDtypeStruct((T, Nw), F32)
    else:
        in_specs += [row, vec, row]
        args += list(norm)
        out_specs = [row, row, vec]
        out_shape = [jax.ShapeDtypeStruct((T, Nw), F32), jax.ShapeDtypeStruct((T, Nw), BF16),
                     jax.ShapeDtypeStruct((1, Nw), F32)]
    return pl.pallas_call(
        body, name=name, grid=(T // tm,), in_specs=in_specs, out_specs=out_specs, out_shape=out_shape,
        scratch_shapes=[pltpu.VMEM((tm, Nw), F32)], compiler_params=_params(1))(*args)


def _tn(name, x, x_act, ysegs, cw, cols_layout, tmm, tt, deps=(), out_dtype=F32):
    T, M = x.shape
    widths = [bs[-1] for _, bs, _ in ysegs]
    N = sum(widths)
    pieces = _pieces(widths, cw if cols_layout else N, 1024)
    ns = len(ysegs)
    n_t = T // tt
    block = (N_CHIPS, tmm, cw) if cols_layout else (tmm, N)
    narrow = out_dtype != F32

    def body(x_ref, *refs):
        y_refs = refs[:ns]
        o_ref = refs[ns + len(deps)]
        acc = refs[-1] if narrow else o_ref

        @pl.when(pl.program_id(1) == 0)
        def _():
            acc[...] = jnp.zeros_like(acc)

        xt = x_act(x_ref[...])
        for s, a0, ch, b0, wd in pieces:
            d = _dot_tn(xt, y_refs[s][:, a0:a0 + wd].astype(BF16))
            if cols_layout:
                acc[ch, :, b0:b0 + wd] += d
            else:
                acc[:, b0:b0 + wd] += d
        if narrow:
            @pl.when(pl.program_id(1) == n_t - 1)
            def _():
                o_ref[...] = acc[...].astype(out_dtype)

    in_specs = [pl.BlockSpec((tt, tmm), lambda m, t: (t, m))] + [pl.BlockSpec(bs, im) for _, bs, im in ysegs]
    in_specs += [ANY] * len(deps)
    if cols_layout:
        out_specs = pl.BlockSpec(block, lambda m, t: (0, m, 0))
        out_shape = jax.ShapeDtypeStruct((N_CHIPS, M, cw), out_dtype)
    else:
        out_specs = pl.BlockSpec(block, lambda m, t: (m, 0))
        out_shape = jax.ShapeDtypeStruct((M, N), out_dtype)
    return pl.pallas_call(
        body, name=name, grid=(M // tmm, n_t), in_specs=in_specs, out_specs=out_specs, out_shape=out_shape,
        scratch_shapes=[pltpu.VMEM(block, F32)] if narrow else [],
        compiler_params=_params(2))(x, *[a for a, _, _ in ysegs], *deps)


def _seg2d(a, t_rows, grid_rank):
    w = a.shape[1]
    if grid_rank == 1:
        return (a, (t_rows, w), lambda i: (i, 0))
    return (a, (t_rows, w), lambda m, t: (t, 0))


def _kv_segments(dk, dv, C, t_rows, grid_rank):
    segs = []
    for g in range(len(PATTERNS)):
        for a in (dk, dv):
            if grid_rank == 1:
                segs.append((a, (t_rows, C), lambda i, g=g: (i, g)))
            else:
                segs.append((a, (t_rows, C), lambda m, t, g=g: (t, g)))
    return segs


def _seg_plane(a, plane, t_rows, grid_rank):
    w = a.shape[2]
    if grid_rank == 1:
        return (a, (None, t_rows, w), lambda i: (plane, i, 0))
    return (a, (None, t_rows, w), lambda m, t: (plane, t, 0))


def _row_tile(rows, row_bytes, budget_bytes=2 * 1024 * 1024):
    t = rows
    while t * row_bytes > budget_bytes and t % 32 == 0:
        t //= 2
    return t


N_DEVICES = 8


def _device_add(name, own, slots, place):
    _, _, hr, c = own.shape
    tr = _row_tile(hr, c * 4, 1024 * 1024)

    def body(place_ref, own_ref, *refs):
        o_ref = refs[-1]
        acc = own_ref[...].astype(F32)
        for r in refs[:-1]:
            acc = acc + r[...].astype(F32)
        o_ref[...] = acc

    def slot(k):
        return pl.BlockSpec((None, tr, c), lambda i, pr: ((2 * pr[0] + pr[1] + k) % N_DEVICES, i, 0))

    grid_spec = pltpu.PrefetchScalarGridSpec(
        num_scalar_prefetch=1, grid=(hr // tr,),
        in_specs=[pl.BlockSpec((None, None, tr, c), lambda i, pr: (pr[0], pr[1], i, 0))]
        + [slot(k) for k in range(1, N_DEVICES)],
        out_specs=pl.BlockSpec((None, tr, c), lambda i, pr: (pr[1], i, 0)))
    return pl.pallas_call(body, name=name, grid_spec=grid_spec,
                          out_shape=jax.ShapeDtypeStruct((2, hr, c), F32),
                          compiler_params=_params(1))(place, own, *[slots] * (N_DEVICES - 1))


def _adamw(name, w, g, m, v):
    rows, cols = w.shape
    tr = _row_tile(rows, cols * 4, 1024 * 1024)

    def body(w_ref, g_ref, m_ref, v_ref, d_ref, nm_ref, nv_ref):
        d_ref[...], nm_ref[...], nv_ref[...] = _adamw_math(w_ref[...], g_ref[...], m_ref[...], v_ref[...])

    spec = pl.BlockSpec((tr, cols), lambda i: (i, 0))
    return pl.pallas_call(
        body, name=name, grid=(rows // tr,), in_specs=[spec] * 4, out_specs=[spec] * 3,
        out_shape=[jax.ShapeDtypeStruct((rows, cols), F32)] * 3, compiler_params=_params(1))(w, g, m, v)


def _adamw_math(w, g, m, v):
    nm = ADAM_B1 * m + (1.0 - ADAM_B1) * g
    nv = ADAM_B2 * v + (1.0 - ADAM_B2) * jnp.square(g)
    m_hat = nm / (1.0 - ADAM_B1 ** ADAM_STEP)
    v_hat = nv / (1.0 - ADAM_B2 ** ADAM_STEP)
    return -ADAM_LR * (m_hat / (jnp.sqrt(v_hat) + ADAM_EPS) + ADAM_WD * w), nm, nv


def _adamw_layers(name, w, grads, m, v):
    L, r, c = w.shape
    tr = _row_tile(r, L * c * 4, 1024 * 1024)

    def body(*refs):
        w_ref, m_ref, v_ref = refs[:3]
        g_refs = refs[3:3 + L]
        go_ref, d_ref, nm_ref, nv_ref = refs[3 + L:]
        for l in range(L):
            g = g_refs[l][...]
            go_ref[l] = g
            d_ref[l], nm_ref[l], nv_ref[l] = _adamw_math(w_ref[l], g, m_ref[l], v_ref[l])

    stacked = pl.BlockSpec((L, tr, c), lambda i: (0, i, 0))
    return pl.pallas_call(
        body, name=name, grid=(r // tr,),
        in_specs=[stacked] * 3 + [pl.BlockSpec((tr, c), lambda i: (i, 0))] * L, out_specs=[stacked] * 4,
        out_shape=[jax.ShapeDtypeStruct((L, r, c), F32)] * 4, compiler_params=_params(1))(w, m, v, *grads)


def _place():
    x, y, c = lax.axis_index("x"), lax.axis_index("y"), lax.axis_index("c")
    chips = [(1 - x, y), (x, 1 - y), (1 - x, 1 - y)]
    return x, y, c, chips


HBM = pl.BlockSpec(memory_space=pltpu.HBM)
SEM = pl.BlockSpec(memory_space=pltpu.SEMAPHORE)
EFFECT = pltpu.SideEffectType.DATAFLOW_SIDE_EFFECTING


class _Copy:
    def __init__(self, src, src_view, land, dst_view, recv_view, target):
        self.src, self.src_view, self.land, self.dst_view, self.recv_view, self.target = (
            src, src_view, land, dst_view, recv_view, target)


def _whole(ref, place):
    return ref


def _split_start(name, srcs, land_shapes, plans):
    skeys, lkeys = list(srcs), list(land_shapes)
    ns, nl, ng = len(skeys), len(lkeys), len(plans)

    def body(*refs):
        src = dict(zip(skeys, refs[:ns]))
        land = dict(zip(lkeys, refs[ns:ns + nl]))
        sems = refs[ns + nl:ns + nl + 2 * ng]
        token = refs[-1]
        place = _place()
        for gi, plan in enumerate(plans):
            for k, cp in enumerate(plan):
                dst = land[cp.land] if cp.land in land else src[cp.land]
                pltpu.make_async_remote_copy(
                    src_ref=cp.src_view(src[cp.src], place), dst_ref=cp.dst_view(dst, place),
                    send_sem=sems[2 * gi].at[k], recv_sem=sems[2 * gi + 1].at[k],
                    device_id=cp.target(place), device_id_type=MESH).start()
        token[...] = jnp.zeros_like(token)

    sem_shapes = []
    for plan in plans:
        sem_shapes += [pltpu.SemaphoreType.DMA((len(plan),))] * 2
    buffers = [srcs[k] for k in skeys] + [lax.empty(land_shapes[k].shape, land_shapes[k].dtype) for k in lkeys]
    outs = pl.pallas_call(
        body, name=name,
        out_shape=(*sem_shapes, *[pltpu.HBM(a.shape, a.dtype) for a in buffers], jax.ShapeDtypeStruct((8, LANES), F32)),
        in_specs=[HBM] * (ns + nl),
        out_specs=(*[SEM] * (2 * ng), *[HBM] * (ns + nl), pl.BlockSpec(memory_space=pltpu.VMEM)),
        input_output_aliases={i: 2 * ng + i for i in range(ns + nl)},
        compiler_params=pltpu.CompilerParams(has_side_effects=EFFECT),
    )(*[pltpu.with_memory_space_constraint(a, pltpu.HBM) for a in buffers])
    sems = [(outs[2 * gi], outs[2 * gi + 1]) for gi in range(ng)]
    thru = outs[2 * ng:2 * ng + ns + nl]
    return sems, dict(zip(skeys, thru[:ns])), dict(zip(lkeys, thru[ns:])), outs[-1]


def _split_wait(name, sems, srcs, lands, plan, after):
    skeys, lkeys = list(srcs), list(lands)
    ns, nl = len(skeys), len(lkeys)

    def body(*refs):
        src = dict(zip(skeys, refs[:ns]))
        land = dict(zip(lkeys, refs[ns:ns + nl]))
        ssem, rsem = refs[ns + nl], refs[ns + nl + 1]
        place = _place()
        for k, cp in enumerate(plan):
            dst = land[cp.land] if cp.land in land else src[cp.land]
            pltpu.make_async_remote_copy(
                src_ref=cp.src_view(src[cp.src], place), dst_ref=cp.dst_view(dst, place),
                send_sem=ssem.at[k], recv_sem=rsem.at[k],
                device_id=cp.target(place), device_id_type=MESH).wait_send()
            got = cp.recv_view(dst, place)
            pltpu.make_async_remote_copy(
                src_ref=got, dst_ref=got, send_sem=ssem.at[k], recv_sem=rsem.at[k],
                device_id=cp.target(place), device_id_type=MESH).wait_recv()

    buffers = [srcs[k] for k in skeys] + [lands[k] for k in lkeys]
    outs = pl.pallas_call(
        body, name=name, out_shape=tuple(pltpu.HBM(a.shape, a.dtype) for a in buffers),
        in_specs=(*[HBM] * (ns + nl), SEM, SEM, ANY), out_specs=tuple([HBM] * (ns + nl)),
        input_output_aliases={i: i for i in range(ns + nl)},
        compiler_params=pltpu.CompilerParams(has_side_effects=EFFECT),
    )(*buffers, sems[0], sems[1], after)
    return dict(zip(skeys, outs[:ns])), dict(zip(lkeys, outs[ns:]))


def _chip_of(place):
    x, y, c, chips = place
    return 2 * x + y


GATHER_FIRST = 2


class _WeightGather:
    def __init__(self, blocks):
        self.plans, shapes = {}, {}
        for key, a in blocks.items():
            shapes[key] = jax.ShapeDtypeStruct((N_CHIPS,) + a.shape, a.dtype)
            slot = lambda ref, place: ref.at[_chip_of(place)]
            plan = [_Copy(key, _whole, key, slot,
                          lambda ref, place, k=k: ref.at[2 * place[3][k][0] + place[3][k][1]],
                          lambda place, k=k: (place[3][k][0], place[3][k][1], place[2])) for k in range(3)]
            plan.append(_Copy(key, _whole, key, slot, slot, lambda place: (place[0], place[1], 1 - place[2])))
            self.plans[key] = plan
        keys = list(blocks)
        self.sems, self.srcs, self.lands = {}, {}, {}
        for name, part in (("gather_start_first", keys[:GATHER_FIRST]), ("gather_start", keys[GATHER_FIRST:])):
            sems, srcs, lands, self.token = _split_start(name, {k: blocks[k] for k in part}, {k: shapes[k] for k in part},
                                                         [self.plans[k] for k in part])
            self.sems.update(zip(part, sems))
            self.srcs.update(srcs)
            self.lands.update(lands)

    def get(self, l, name, after):
        key = (l, name)
        _, lands = _split_wait(f"gather_wait_{name}{l}", self.sems[key], {key: self.srcs[key]},
                               {key: self.lands[key]}, self.plans[key], after)
        return lands[key][:, None]


class _GradReduce:
    def __init__(self, place):
        self.place = place
        self.jobs = []
        self.done = {}
        self.n = 0

    def submit(self, grads):
        views = {k: a.reshape(N_CHIPS, 2, a.shape[1] // 2, a.shape[2]) for k, a in grads.items()}
        shapes = {k: jax.ShapeDtypeStruct((N_DEVICES,) + a.shape[2:], a.dtype) for k, a in views.items()}

        def peer(place, k):
            x, y, c, _ = place
            return (1 - x if k & 4 else x, 1 - y if k & 2 else y, 1 - c if k & 1 else c)

        def index(dev):
            return 4 * dev[0] + 2 * dev[1] + dev[2]

        plan = []
        for key in views:
            for k in range(1, N_DEVICES):
                plan.append(_Copy(
                    key, lambda ref, place, k=k: ref.at[2 * peer(place, k)[0] + peer(place, k)[1], peer(place, k)[2]],
                    key, lambda ref, place: ref.at[index(place[:3])],
                    lambda ref, place, k=k: ref.at[index(peer(place, k))],
                    lambda place, k=k: peer(place, k)))
        sems, srcs, lands, token = _split_start(f"grad_start{self.n}", views, shapes, [plan])
        self.jobs.append(dict(id=self.n, sems=sems[0], srcs=srcs, lands=lands, plan=plan))
        self.n += 1
        return token

    def pump(self, after):
        return []

    def finish(self, after):
        for job in self.jobs:
            srcs, lands = _split_wait(f"grad_wait{job['id']}", job["sems"], job["srcs"], job["lands"], job["plan"],
                                      after)
            for i, k in enumerate(srcs):
                self.done[k] = _device_add(f"grad_add{job['id']}_{i}", srcs[k], lands[k], self.place)
        self.jobs = []
        return self.done


class _PairShare:
    def __init__(self, halves, types):
        sibling = lambda place: (place[0], place[1], 1 - place[2])
        mine = lambda ref, place: ref.at[place[2]]
        theirs = lambda ref, place: ref.at[1 - place[2]]
        self.plans = {t: [_Copy(k, mine, k, mine, theirs, sibling) for k in halves if k[0] == t] for t in types}
        sems, self.bufs, _, self.token = _split_start("share_start", halves, {}, list(self.plans.values()))
        self.sems = dict(zip(self.plans, sems))

    def get(self, t, after):
        keys = [cp.src for cp in self.plans[t]]
        bufs, _ = _split_wait(f"share_wait_{t}", self.sems[t], {k: self.bufs[k] for k in keys}, {}, self.plans[t], after)
        return bufs


def _small_allreduce(part):
    R, C = part.shape
    N_DEV = 8

    def body(in_ref, out_ref, slots, ssem, rsem):
        x, y, c, _ = _place()
        me = 4 * x + 2 * y + c
        sends = []
        for k in range(1, N_DEV):
            kx, ky, kc = (k >> 2) & 1, (k >> 1) & 1, k & 1
            peer = (1 - x if kx else x, 1 - y if ky else y, 1 - c if kc else c)
            cp = pltpu.make_async_remote_copy(
                src_ref=in_ref, dst_ref=slots.at[me], send_sem=ssem.at[k], recv_sem=rsem.at[k],
                device_id=peer, device_id_type=MESH)
            cp.start()
            sends.append(cp)
        slots[me] = in_ref[...]
        for k in range(1, N_DEV):
            kx, ky, kc = (k >> 2) & 1, (k >> 1) & 1, k & 1
            peer = (1 - x if kx else x, 1 - y if ky else y, 1 - c if kc else c)
            slot = slots.at[4 * peer[0] + 2 * peer[1] + peer[2]]
            pltpu.make_async_remote_copy(
                src_ref=slot, dst_ref=slot, send_sem=ssem.at[k], recv_sem=rsem.at[k],
                device_id=peer, device_id_type=MESH).wait_recv()
        acc = slots[0]
        for d in range(1, N_DEV):
            acc = acc + slots[d]
        out_ref[...] = acc
        for cp in sends:
            cp.wait_send()

    vm = pl.BlockSpec(memory_space=pltpu.VMEM)
    return pl.pallas_call(
        body, name="small_allreduce", in_specs=[vm], out_specs=vm,
        out_shape=jax.ShapeDtypeStruct((R, C), F32),
        scratch_shapes=[pltpu.VMEM((N_DEV, R, C), F32), pltpu.SemaphoreType.DMA((N_DEV,)),
                        pltpu.SemaphoreType.DMA((N_DEV,))])(part)


def _local_step(x, target, norm_mix, norm_mlp, norm_kv, norm_final, weights, sink, n_a, n_heads):
    B, S, D = x.shape
    T = B * S
    C = n_heads * HEAD_DIM
    depth = norm_mix.shape[0]
    slopes = 2.0 ** (-ALIBI_MAX_BIAS * jnp.arange(1, n_heads + 1, dtype=F32) / n_heads)
    tm = min(512, T)
    row = lambda v: v.reshape(1, -1)

    h = x.reshape(T, D)
    saved, Wl = [], []
    kv = nkv = h_kv = cwg = None
    for l in range(depth):
        s = {"h_in": h}
        w = {}
        Wl.append(w)
        if l < n_a:
            w["w_a_in"] = weights.get(l, "w_a_in", h)
            first = [weights.token] if l == 0 else []
            s["n1"], bcu = _norm_mm(f"a_in_fwd{l}", h, row(norm_mix[l]), w["w_a_in"], 0, 3, BF16, tm, first)
            s["bcu"] = bcu.reshape(3, B, S, D)
            if l == 0:
                cwg = weights.get(0, "conv", bcu)[:, 0, :n_a * 3].reshape(N_CHIPS, n_a, 3, -1)
            s["z"] = _conv_fwd(f"conv_fwd{l}", s["bcu"], cwg, l, CONV_COLS).reshape(T, D)
            w["w_a_out"] = weights.get(l, "w_a_out", s["z"])
            h = _mm_res_rows(f"a_out_fwd{l}", s["z"], w["w_a_out"], 0, h, _to_bf16, tm)
        else:
            i = l - n_a
            if i == 0:
                h_kv = h
                w["w_kv"] = weights.get(l, "w_kv", h)
                nkv, kv = _norm_mm("kv_fwd", h, row(norm_kv), w["w_kv"], 0, 1, F32, tm)
                kv = kv.reshape(B, S, 2 * 3 * C)
            w["w_q"] = weights.get(l, "w_q", h)
            s["n1"], q = _norm_mm(f"q_fwd{i}", h, row(norm_mix[l]), w["w_q"], 0, 1, F32, tm)
            s["q"] = q.reshape(B, S, 3 * C)
            o, lse = _attn_fwd(f"attn_fwd{i}", s["q"], kv, slopes, n_heads)
            s["o"], s["lse"] = o.reshape(T, C), lse.reshape(T, C)
            w["w_o"] = weights.get(l, "w_o", o)
            h = _mm_res_cols(f"o_fwd{i}", s["o"], w["w_o"], 0, h, tm)
        s["h_mid"] = h
        w["w_up"] = weights.get(l, "w_up", h)
        if l < n_a:
            s["n2"], a = _norm_mm(f"up_fwd{l}", h, row(norm_mlp[l]), w["w_up"], 0, 1, BF16, tm)
            s["a"] = a[0]
            w["w_down"] = weights.get(l, "w_down", a)
            h = _mm_res_rows(f"down_fwd{l}", s["a"], w["w_down"], 0, h, _relu2_bf16, tm)
        else:
            w["w_down"] = weights.get(l, "w_down", h)
            s["n2"], s["a"], h = _mlp_fwd(f"mlp_fwd{l}", h, row(norm_mlp[l]), w["w_up"], w["w_down"], tm)
        F = s["a"].shape[1]
        saved.append(s)

    loss, dh, dh16, dg_final = _final_loss("loss_head", h, row(norm_final), target.reshape(T, D), tm)

    g_mix, g_mlp = [None] * depth, [None] * depth
    g_conv = [None] * n_a
    dkv = None
    tt = min(512, T)
    deps = []
    for l in reversed(range(depth)):
        s, w = saved[l], Wl[l]
        g_down = _tn(f"down_wgrad{l}", s["a"], _relu2_bf16, [_seg2d(dh16, tt, 2)], None, False,
                     min(2048, F), tt, deps, BF16).reshape(N_CHIPS, F // N_CHIPS, D)
        da, dh, dh16, g_mlp[l] = _mlp_bwd(f"mlp_bwd{l}", dh, dh16, s["a"], w["w_down"], w["w_up"], s["h_mid"],
                                          row(norm_mlp[l]), tm)
        g_up = _tn(f"up_wgrad{l}", s["n2"], _to_bf16, [_seg2d(da, tt, 2)], F // N_CHIPS, True, D, tt, (), BF16)
        deps = sink.pump(dh) + [sink.submit({("w_up", l): g_up, ("w_down", l): g_down})]
        if l < n_a:
            g_out = _tn(f"a_out_wgrad{l}", s["z"], _to_bf16, [_seg2d(dh16, tt, 2)], None, False,
                        D, tt, deps, BF16).reshape(N_CHIPS, D // N_CHIPS, D)
            dz = _nt_rows(f"a_out_bwd{l}", dh16, w["w_a_out"], 0, None, F32, tm)
            deps = sink.pump(dz) + [sink.submit({("w_a_out", l): g_out})]
            dbcu, g_conv[l] = _conv_bwd(f"conv_bwd{l}", s["bcu"], dz.reshape(B, S, D), cwg, l, CONV_COLS)
            dbcu = dbcu.reshape(3, T, D)
            g_in = _tn(f"a_in_wgrad{l}", s["n1"], _to_bf16, [_seg_plane(dbcu, p, tt, 2) for p in range(3)],
                       3 * D // N_CHIPS, True, D, tt, deps, BF16)
            dh, dh16, g_mix[l] = _nt_cols(f"a_in_bwd{l}", [_seg_plane(dbcu, p, tm, 1) for p in range(3)],
                                          w["w_a_in"], 0, tm, (s["h_in"], row(norm_mix[l]), dh))
            mixer = {("w_a_in", l): g_in}
        else:
            i = l - n_a
            g_o = _tn(f"o_wgrad{i}", s["o"], _to_bf16, [_seg2d(dh16, tt, 2)], D // N_CHIPS, True, C, tt, deps,
                      BF16)
            do = _nt_cols(f"o_bwd{i}", [_seg2d(dh16, tm, 1)], w["w_o"], 0, tm, None)
            deps = sink.pump(do) + [sink.submit({("w_o", i): g_o})]
            dq, dk, dv = _attn_bwd(f"attn_bwd{i}", s["q"], kv, slopes, s["o"].reshape(B, S, C),
                                   s["lse"].reshape(B, S, C), do.reshape(B, S, C), n_heads, dkv)
            dkv = (dk, dv)
            dq = dq.reshape(T, 3 * C)
            g_q = _tn(f"q_wgrad{i}", s["n1"], _to_bf16, [_seg2d(dq, tt, 2)], 3 * C // N_CHIPS, True, D, tt, deps,
                      BF16)
            dh, dh16, g_mix[l] = _nt_cols(f"q_bwd{i}", [_seg2d(dq, tm, 1)], w["w_q"], 0, tm,
                                          (s["h_in"], row(norm_mix[l]), dh))
            mixer = {("w_q", i): g_q}
            if i == 0:
                dk2, dv2 = (t.reshape(T, 3 * C) for t in dkv)
                mixer[("w_kv", 0)] = _tn("kv_wgrad", nkv, _to_bf16, _kv_segments(dk2, dv2, C, tt, 2),
                                         6 * C // N_CHIPS, True, D, tt, (), BF16)
                dh, dh16, g_kv = _nt_cols("kv_bwd", _kv_segments(dk2, dv2, C, tm, 1), w["w_kv"], 0, tm,
                                          (h_kv, row(norm_kv), dh))
        deps = sink.pump(dh) + [sink.submit(mixer)]
    small = dict(norm_mix=jnp.concatenate(g_mix, axis=0), norm_mlp=jnp.concatenate(g_mlp, axis=0),
                 norm_kv=g_kv, norm_final=dg_final, conv_w=jnp.stack(g_conv))
    return loss, dh.reshape(B, S, D), small


BIG = ("w_a_in", "w_a_out", "w_kv", "w_q", "w_o", "w_up", "w_down")
CONV_PAD_ROWS = 16


def kernel(x, norm_mix, norm_mlp, w_a_in, conv_w, w_a_out, norm_kv, w_kv, w_q, w_o, w_up, w_down, norm_final, loss_target, m_norm_mix, m_norm_mlp, m_w_a_in, m_conv_w, m_w_a_out, m_norm_kv, m_w_kv, m_w_q, m_w_o, m_w_up, m_w_down, m_norm_final, v_norm_mix, v_norm_mlp, v_w_a_in, v_conv_w, v_w_a_out, v_norm_kv, v_w_kv, v_w_q, v_w_o, v_w_up, v_w_down, v_norm_final):
    D = x.shape[-1]
    w = dict(norm_mix=norm_mix, norm_mlp=norm_mlp, w_a_in=w_a_in, conv_w=conv_w, w_a_out=w_a_out, norm_kv=norm_kv,
             w_kv=w_kv[None], w_q=w_q, w_o=w_o, w_up=w_up, w_down=w_down, norm_final=norm_final)
    m = dict(norm_mix=m_norm_mix, norm_mlp=m_norm_mlp, w_a_in=m_w_a_in, conv_w=m_conv_w, w_a_out=m_w_a_out,
             norm_kv=m_norm_kv, w_kv=m_w_kv[None], w_q=m_w_q, w_o=m_w_o, w_up=m_w_up, w_down=m_w_down,
             norm_final=m_norm_final)
    v = dict(norm_mix=v_norm_mix, norm_mlp=v_norm_mlp, w_a_in=v_w_a_in, conv_w=v_conv_w, w_a_out=v_w_a_out,
             norm_kv=v_norm_kv, w_kv=v_w_kv[None], w_q=v_w_q, w_o=v_w_o, w_up=v_w_up, w_down=v_w_down,
             norm_final=v_norm_final)
    depth = norm_mix.shape[0]
    n_a, taps, cwc = conv_w.shape
    n_heads = w_o.shape[1] // HEAD_DIM

    conv_rows = jnp.zeros((CONV_PAD_ROWS, cwc), F32).at[:n_a * taps].set(conv_w.reshape(n_a * taps, cwc))
    blocks = {}
    for l in range(depth):
        if l < n_a:
            blocks[(l, "w_a_in")] = w_a_in[l].astype(BF16)
            if l == 0:
                blocks[(0, "conv")] = conv_rows
            blocks[(l, "w_a_out")] = w_a_out[l].astype(BF16)
        else:
            if l == n_a:
                blocks[(l, "w_kv")] = w_kv.astype(BF16)
            blocks[(l, "w_q")] = w_q[l - n_a].astype(BF16)
            blocks[(l, "w_o")] = w_o[l - n_a].astype(BF16)
        blocks[(l, "w_up")] = w_up[l].astype(BF16)
        blocks[(l, "w_down")] = w_down[l].astype(BF16)
    weights = _WeightGather(blocks)
    place = jnp.stack([2 * lax.axis_index("x") + lax.axis_index("y"), lax.axis_index("c")]).astype(jnp.int32)
    sink = _GradReduce(place)

    loss, grad_x, small = _local_step(x, loss_target, norm_mix, norm_mlp, norm_kv, norm_final, weights, sink,
                                      n_a, n_heads)
    loss = lax.psum(loss[0, 0], ("x", "y", "c"))

    share = _PairShare(sink.finish(grad_x), BIG)
    grads = {}

    packed = jnp.concatenate([small["norm_mix"], small["norm_mlp"], small["norm_kv"], small["norm_final"],
                              small["conv_w"].reshape(n_a * taps, D)], axis=0)
    pad = (-packed.shape[0]) % 8
    packed = jnp.pad(packed, ((0, pad), (0, 0)))
    total = _small_allreduce(packed)
    grads["norm_mix"] = total[:depth]
    grads["norm_mlp"] = total[depth:2 * depth]
    grads["norm_kv"] = total[2 * depth]
    grads["norm_final"] = total[2 * depth + 1]
    chip = 2 * lax.axis_index("x") + lax.axis_index("y")
    conv_full = total[2 * depth + 2:2 * depth + 2 + n_a * taps].reshape(n_a, taps, N_CHIPS, cwc)
    grads["conv_w"] = lax.dynamic_index_in_dim(conv_full, chip, axis=2, keepdims=False)

    order = ("norm_mix", "norm_mlp", "w_a_in", "conv_w", "w_a_out", "norm_kv", "w_kv", "w_q", "w_o", "w_up",
             "w_down", "norm_final")
    delta, new_m, new_v = {}, {}, {}
    vec_names = ("norm_mix", "norm_mlp", "norm_kv", "norm_final")
    rows_of = lambda a: a.reshape(-1, D)
    vw, vg, vm_, vv = (jnp.concatenate([rows_of(t[k]) for k in vec_names], axis=0) for t in (w, grads, m, v))
    vpad = (-vw.shape[0]) % 8
    padrows = lambda a: jnp.pad(a, ((0, vpad), (0, 0)))
    vd, vnm, vnv = _adamw("adamw_norms", padrows(vw), padrows(vg), padrows(vm_), padrows(vv))
    off = 0
    for k in vec_names:
        r = rows_of(w[k]).shape[0]
        delta[k] = vd[off:off + r].reshape(w[k].shape)
        new_m[k] = vnm[off:off + r].reshape(w[k].shape)
        new_v[k] = vnv[off:off + r].reshape(w[k].shape)
        off += r
    cpad = (-n_a * taps) % 8
    two_d = lambda a: jnp.pad(a.reshape(-1, cwc), ((0, cpad), (0, 0)))
    cd, cnm, cnv = _adamw("adamw_conv_w", two_d(w["conv_w"]), two_d(grads["conv_w"]), two_d(m["conv_w"]),
                          two_d(v["conv_w"]))
    delta["conv_w"], new_m["conv_w"], new_v["conv_w"] = (t[:n_a * taps].reshape(conv_w.shape) for t in (cd, cnm, cnv))
    after = cd
    for k in sorted(BIG, key=lambda k: w[k].size):
        shared = share.get(k, after)
        per_layer = [shared[(k, l)].reshape(w[k].shape[1:]) for l in range(w[k].shape[0])]
        grads[k], delta[k], new_m[k], new_v[k] = _adamw_layers(f"adamw_{k}", w[k], per_layer, m[k], v[k])
        after = delta[k]
    fix = lambda k, a: a[0] if k == "w_kv" else a
    return (loss, grad_x, *[fix(k, grads[k]) for k in order], *[fix(k, delta[k]) for k in order],
            *[fix(k, new_m[k]) for k in order], *[fix(k, new_v[k]) for k in order])
```

```python
import jax
import jax.numpy as jnp
from jax import lax
from jax.experimental import pallas as pl
from jax.experimental.pallas import tpu as pltpu

F32 = jnp.float32
BF16 = jnp.bfloat16
MESH = pl.DeviceIdType.MESH

EPS = 1e-5
PATTERNS = ((128, 1), (512, 4), (2048, 16))
HEAD_DIM = 64
ALIBI_MAX_BIAS = 8.0
NEG_INF = -1e30
ATT_BLK = 128
BWD_UNROLL = 16
N_CHIPS = 4
LANES = 128
VMEM_LIMIT = 56 * 1024 * 1024

ADAM_LR = 0.001
ADAM_B1 = 0.9
ADAM_B2 = 0.999
ADAM_EPS = 1e-08
ADAM_WD = 0.01
ADAM_STEP = 10


ANY = pl.BlockSpec(memory_space=pl.ANY)


def _params(n_grid_axes):
    return pltpu.CompilerParams(dimension_semantics=("arbitrary",) * n_grid_axes, vmem_limit_bytes=VMEM_LIMIT)


def _dot(a, b):
    return jnp.dot(a, b, preferred_element_type=F32)


def _dot_nt(a, b):
    return lax.dot_general(a, b, (((1,), (1,)), ((), ())), preferred_element_type=F32)


def _dot_tn(a, b):
    return lax.dot_general(a, b, (((0,), (0,)), ((), ())), preferred_element_type=F32)


def _relu2(a):
    return jnp.square(jnp.maximum(a, 0.0))


def _rms(hf, g):
    y = hf * lax.rsqrt(jnp.mean(hf * hf, axis=-1, keepdims=True) + EPS)
    return y * g


def _rms_bwd(hf, g, dn):
    rstd = lax.rsqrt(jnp.mean(hf * hf, axis=-1, keepdims=True) + EPS)
    xhat = hf * rstd
    dg = jnp.sum(dn * xhat, axis=0, keepdims=True)
    dx = dn * g
    dh = rstd * (dx - xhat * jnp.mean(dx * xhat, axis=-1, keepdims=True))
    return dh, dg


def _pieces(seg_widths, chunk_width, max_width):
    total = sum(seg_widths)
    cuts = {0, total}
    acc = 0
    for w in seg_widths:
        cuts.add(acc)
        acc += w
    cuts.update(range(0, total, chunk_width))
    cuts = sorted(cuts)
    fine = []
    for lo, hi in zip(cuts[:-1], cuts[1:]):
        while hi - lo > max_width:
            fine.append((lo, lo + max_width))
            lo += max_width
        fine.append((lo, hi))
    out = []
    for lo, hi in fine:
        acc = 0
        for s, w in enumerate(seg_widths):
            if lo < acc + w:
                break
            acc += w
        out.append((s, lo - acc, lo // chunk_width, lo % chunk_width, hi - lo))
    return out


def _relu2_bf16(a):
    return _relu2(a.astype(F32)).astype(BF16)


def _to_bf16(a):
    return a.astype(BF16)


def _norm_mm(name, h, g, wg, layer, planes, out_dtype, tm, deps=()):
    T, D = h.shape
    cw = wg.shape[3]
    N = N_CHIPS * cw
    pw = N // planes
    pieces = _pieces([pw] * planes, cw, 512)

    def body(h_ref, g_ref, w_ref, *rest):
        n_ref, o_ref = rest[len(deps):]
        n = _rms(h_ref[...], g_ref[...]).astype(BF16)
        n_ref[...] = n
        for s, a0, ch, b0, wd in pieces:
            o_ref[s, :, a0:a0 + wd] = _dot(n, w_ref[ch, :, b0:b0 + wd]).astype(out_dtype)

    return pl.pallas_call(
        body, name=name, grid=(T // tm,),
        in_specs=[pl.BlockSpec((tm, D), lambda i: (i, 0)),
                  pl.BlockSpec((1, D), lambda i: (0, 0)),
                  pl.BlockSpec((N_CHIPS, None, D, cw), lambda i: (0, layer, 0, 0))] + [ANY] * len(deps),
        out_specs=[pl.BlockSpec((tm, D), lambda i: (i, 0)),
                   pl.BlockSpec((planes, tm, pw), lambda i: (0, i, 0))],
        out_shape=[jax.ShapeDtypeStruct((T, D), BF16), jax.ShapeDtypeStruct((planes, T, pw), out_dtype)],
        compiler_params=_params(1))(h, g, wg, *deps)


def _resident(shape, index_map):
    return pl.BlockSpec(shape, index_map, pipeline_mode=pl.Buffered(1))


def _mm_res_rows(name, a, wg, layer, h, act, tm):
    T = a.shape[0]
    rk, D = wg.shape[2], wg.shape[3]

    def body(a_ref, w_ref, h_ref, o_ref):
        acc = h_ref[...]
        for k in range(N_CHIPS):
            acc = acc + _dot(act(a_ref[:, k * rk:(k + 1) * rk]), w_ref[k])
        o_ref[...] = acc

    return pl.pallas_call(
        body, name=name, grid=(T // tm,),
        in_specs=[pl.BlockSpec((tm, N_CHIPS * rk), lambda i: (i, 0)),
                  pl.BlockSpec((N_CHIPS, None, rk, D), lambda i: (0, layer, 0, 0)),
                  pl.BlockSpec((tm, D), lambda i: (i, 0))],
        out_specs=pl.BlockSpec((tm, D), lambda i: (i, 0)),
        out_shape=jax.ShapeDtypeStruct((T, D), F32),
        compiler_params=_params(1))(a, wg, h)


def _mm_res_cols(name, a, wg, layer, h, tm):
    T, K = a.shape
    cw = wg.shape[3]
    D = N_CHIPS * cw

    def body(a_ref, w_ref, h_ref, o_ref):
        a16 = a_ref[...].astype(BF16)
        for j in range(N_CHIPS):
            o_ref[:, j * cw:(j + 1) * cw] = h_ref[:, j * cw:(j + 1) * cw] + _dot(a16, w_ref[j])

    return pl.pallas_call(
        body, name=name, grid=(T // tm,),
        in_specs=[pl.BlockSpec((tm, K), lambda i: (i, 0)),
                  pl.BlockSpec((N_CHIPS, None, K, cw), lambda i: (0, layer, 0, 0)),
                  pl.BlockSpec((tm, D), lambda i: (i, 0))],
        out_specs=pl.BlockSpec((tm, D), lambda i: (i, 0)),
        out_shape=jax.ShapeDtypeStruct((T, D), F32),
        compiler_params=_params(1))(a, wg, h)


def _mlp_fwd(name, h, g, wup, wdown, tm):
    T, D = h.shape
    cw = wup.shape[3]

    def body(h_ref, g_ref, wu_ref, wd_ref, n_ref, a_ref, o_ref):
        hf = h_ref[...]
        n = _rms(hf, g_ref[...]).astype(BF16)
        n_ref[...] = n
        acc = hf
        for ch in range(N_CHIPS):
            a16 = _dot(n, wu_ref[ch]).astype(BF16)
            a_ref[:, ch * cw:(ch + 1) * cw] = a16
            acc = acc + _dot(_relu2_bf16(a16), wd_ref[ch])
        o_ref[...] = acc

    row = pl.BlockSpec((tm, D), lambda i: (i, 0))
    return pl.pallas_call(
        body, name=name, grid=(T // tm,),
        in_specs=[row, pl.BlockSpec((1, D), lambda i: (0, 0)),
                  _resident((N_CHIPS, None, D, cw), lambda i: (0, 0, 0, 0)),
                  _resident((N_CHIPS, None, cw, D), lambda i: (0, 0, 0, 0))],
        out_specs=[row, pl.BlockSpec((tm, N_CHIPS * cw), lambda i: (i, 0)), row],
        out_shape=[jax.ShapeDtypeStruct((T, D), BF16), jax.ShapeDtypeStruct((T, N_CHIPS * cw), BF16),
                   jax.ShapeDtypeStruct((T, D), F32)],
        compiler_params=_params(1))(h, g, wup, wdown)


def _mlp_bwd(name, dh, dh16, a, wdown, wup, h_mid, g, tm, deps=()):
    T, D = dh.shape
    cw = wup.shape[3]
    F = N_CHIPS * cw

    def body(dh_ref, dh16_ref, a_ref, wd_ref, wu_ref, h_ref, g_ref, *rest):
        da_ref, out_ref, out16_ref, dg_ref = rest[len(deps):]
        d16 = dh16_ref[...]
        acc = None
        for ch in range(N_CHIPS):
            cols = slice(ch * cw, (ch + 1) * cw)
            da = (_dot_nt(d16, wd_ref[ch]) * (2.0 * jnp.maximum(a_ref[:, cols].astype(F32), 0.0))).astype(BF16)
            da_ref[:, cols] = da
            d = _dot_nt(da, wu_ref[ch])
            acc = d if acc is None else acc + d
        dh_c, dg = _rms_bwd(h_ref[...], g_ref[...], acc)
        out = dh_ref[...] + dh_c
        out_ref[...] = out
        out16_ref[...] = out.astype(BF16)

        @pl.when(pl.program_id(0) == 0)
        def _():
            dg_ref[...] = dg

        @pl.when(pl.program_id(0) > 0)
        def _():
            dg_ref[...] += dg

    row = pl.BlockSpec((tm, D), lambda i: (i, 0))
    wide = pl.BlockSpec((tm, F), lambda i: (i, 0))
    vec = pl.BlockSpec((1, D), lambda i: (0, 0))
    return pl.pallas_call(
        body, name=name, grid=(T // tm,),
        in_specs=[row, row, wide, _resident((N_CHIPS, None, cw, D), lambda i: (0, 0, 0, 0)),
                  _resident((N_CHIPS, None, D, cw), lambda i: (0, 0, 0, 0)), row, vec] + [ANY] * len(deps),
        out_specs=[wide, row, row, vec],
        out_shape=[jax.ShapeDtypeStruct((T, F), BF16), jax.ShapeDtypeStruct((T, D), F32),
                   jax.ShapeDtypeStruct((T, D), BF16), jax.ShapeDtypeStruct((1, D), F32)],
        compiler_params=_params(1))(dh, dh16, a, wdown, wup, h_mid, g, *deps)


CONV_ROWS = 256
CONV_HALO = 16
CONV_COLS = 2 * LANES


def _conv_shifted(ext, k, r0, rows, at_start):
    rolled = pltpu.roll(ext, k, 0)[CONV_HALO:]
    if not at_start:
        return rolled
    t = r0 + lax.broadcasted_iota(jnp.int32, rolled.shape, 0)
    return jnp.where(t >= k, rolled, 0.0)


def _conv_ahead(ext, k, r0, rows, S, at_end):
    rolled = pltpu.roll(ext, rows + CONV_HALO - k, 0)[:rows]
    if not at_end:
        return rolled
    t = r0 + lax.broadcasted_iota(jnp.int32, rolled.shape, 0)
    return jnp.where(t + k < S, rolled, 0.0)


def _conv_chunks(step, n, carry):
    carry = step(0, carry, True, n == 1)
    if n > 2:
        carry = lax.fori_loop(1, n - 1, lambda i, c: step(i, c, False, False), carry)
    if n > 1:
        carry = step(n - 1, carry, False, True)
    return carry


def _conv_fwd(name, bcu, cwg, layer, tc):
    _, B, S, D = bcu.shape
    cwc = cwg.shape[3]
    per_chunk = cwc // tc
    R = min(CONV_ROWS, S)

    def body(x_ref, w_ref, z_ref):
        w = [w_ref[k:k + 1, :] for k in range(3)]

        def step(i, carry, at_start, at_end):
            r0 = pl.multiple_of(i * R, R)
            h0 = pl.multiple_of(jnp.maximum(r0 - CONV_HALO, 0), CONV_HALO)
            ld = lambda p, start, rows: x_ref[p, pl.ds(start, rows), :].astype(F32)
            cu = jnp.concatenate([ld(1, h0, CONV_HALO) * ld(2, h0, CONV_HALO), ld(1, r0, R) * ld(2, r0, R)], axis=0)
            conv = w[0] * cu[CONV_HALO:]
            conv = conv + w[1] * _conv_shifted(cu, 1, r0, R, at_start)
            conv = conv + w[2] * _conv_shifted(cu, 2, r0, R, at_start)
            z_ref[pl.ds(r0, R), :] = (ld(0, r0, R) * conv).astype(BF16)
            return carry

        _conv_chunks(step, S // R, 0)

    return pl.pallas_call(
        body, name=name, grid=(B, D // tc),
        in_specs=[pl.BlockSpec((3, None, S, tc), lambda b, j: (0, b, 0, j)),
                  pl.BlockSpec((None, None, 3, tc), lambda b, j: (j // per_chunk, layer, 0, j % per_chunk))],
        out_specs=pl.BlockSpec((None, S, tc), lambda b, j: (b, 0, j)),
        out_shape=jax.ShapeDtypeStruct((B, S, D), BF16),
        compiler_params=_params(2))(bcu, cwg)


def _conv_bwd(name, bcu, dz, cwg, layer, tc):
    _, B, S, D = bcu.shape
    cwc = cwg.shape[3]
    per_chunk = cwc // tc
    R = min(CONV_ROWS, S)

    def body(x_ref, dz_ref, w_ref, d_ref, dw_ref):
        w = [w_ref[k:k + 1, :] for k in range(3)]

        @pl.when(pl.program_id(1) == 0)
        def _():
            dw_ref[...] = jnp.zeros_like(dw_ref)

        def step(i, carry, at_start, at_end):
            r0 = pl.multiple_of(i * R, R)
            h0 = pl.multiple_of(jnp.maximum(r0 - CONV_HALO, 0), CONV_HALO)
            a0 = pl.multiple_of(jnp.minimum(r0 + R, S - CONV_HALO), CONV_HALO)
            ld = lambda p, start, rows: x_ref[p, pl.ds(start, rows), :].astype(F32)
            b, c, u = ld(0, r0, R), ld(1, r0, R), ld(2, r0, R)
            dz = dz_ref[pl.ds(r0, R), :]
            cu = jnp.concatenate([ld(1, h0, CONV_HALO) * ld(2, h0, CONV_HALO), c * u], axis=0)
            cu1 = _conv_shifted(cu, 1, r0, R, at_start)
            cu2 = _conv_shifted(cu, 2, r0, R, at_start)
            conv = w[0] * (c * u) + w[1] * cu1 + w[2] * cu2
            dconv = dz * b
            dca = jnp.concatenate([dconv, dz_ref[pl.ds(a0, CONV_HALO), :] * ld(0, a0, CONV_HALO)], axis=0)
            dcu = (w[0] * dconv + w[1] * _conv_ahead(dca, 1, r0, R, S, at_end)
                   + w[2] * _conv_ahead(dca, 2, r0, R, S, at_end))
            d_ref[0, pl.ds(r0, R), :] = (dz * conv).astype(BF16)
            d_ref[1, pl.ds(r0, R), :] = (dcu * u).astype(BF16)
            d_ref[2, pl.ds(r0, R), :] = (dcu * c).astype(BF16)
            return (carry[0] + jnp.sum(dconv * (c * u), axis=0, keepdims=True),
                    carry[1] + jnp.sum(dconv * cu1, axis=0, keepdims=True),
                    carry[2] + jnp.sum(dconv * cu2, axis=0, keepdims=True))

        zero = jnp.zeros((1, tc), F32)
        s0, s1, s2 = _conv_chunks(step, S // R, (zero, zero, zero))
        for k, sk in enumerate((s0, s1, s2)):
            dw_ref[k:k + 1, :] += sk

    return pl.pallas_call(
        body, name=name, grid=(D // tc, B),
        in_specs=[pl.BlockSpec((3, None, S, tc), lambda j, b: (0, b, 0, j)),
                  pl.BlockSpec((None, S, tc), lambda j, b: (b, 0, j)),
                  pl.BlockSpec((None, None, 3, tc), lambda j, b: (j // per_chunk, layer, 0, j % per_chunk))],
        out_specs=[pl.BlockSpec((3, None, S, tc), lambda j, b: (0, b, 0, j)),
                   pl.BlockSpec((3, tc), lambda j, b: (0, j))],
        out_shape=[jax.ShapeDtypeStruct((3, B, S, D), BF16), jax.ShapeDtypeStruct((3, D), F32)],
        compiler_params=_params(2))(bcu, dz, cwg)


def _att_rows(dil, idx, nb):
    r, n = idx // nb, idx % nb
    if dil == 1:
        cur = pl.ds(pl.multiple_of(n * ATT_BLK, ATT_BLK), ATT_BLK)
        prev = pl.ds(pl.multiple_of(jnp.maximum(n - 1, 0) * ATT_BLK, ATT_BLK), ATT_BLK)
    else:
        cur = pl.ds(n * (ATT_BLK * dil) + r, ATT_BLK, stride=dil)
        prev = pl.ds(jnp.maximum(n - 1, 0) * (ATT_BLK * dil) + r, ATT_BLK, stride=dil)
    return n, cur, prev


def _att_bias(bias_ref, dil, sl_ref, hp):
    row = lax.broadcasted_iota(jnp.int32, (2 * ATT_BLK, 2 * ATT_BLK), 0)
    ci = lax.broadcasted_iota(jnp.int32, (2 * ATT_BLK, 2 * ATT_BLK), 1)
    j = ATT_BLK + (row & (ATT_BLK - 1)) - ci
    slope = jnp.where(row < ATT_BLK, sl_ref[2 * hp], sl_ref[2 * hp + 1])
    rest = jnp.where((j >= 0) & (j <= ATT_BLK), -slope * (dil * j).astype(F32), NEG_INF)
    bias_ref[1] = rest
    bias_ref[0] = jnp.where(ci >= ATT_BLK, rest, NEG_INF)


def _stack_heads(x16, lane):
    first = lane < HEAD_DIM
    return jnp.concatenate([jnp.where(first, x16, jnp.zeros_like(x16)),
                            jnp.where(first, jnp.zeros_like(x16), x16)], axis=0)


def _per_head(col, lane):
    return jnp.where(lane < HEAD_DIM, col[:ATT_BLK], col[ATT_BLK:])


def _attn_fwd(name, q, kv, slopes, n_heads):
    B, S, CQ = q.shape
    HP = n_heads * HEAD_DIM // LANES
    scale = HEAD_DIM ** -0.5
    n_groups = len(PATTERNS)
    CH = 256

    def body(sl_ref, q_ref, k_ref, v_ref, o_ref, lse_ref, bias_ref, *parts):
        og, lg = parts[:n_groups], parts[n_groups:]
        hp, g = pl.program_id(1), pl.program_id(2)
        lane = lax.broadcasted_iota(jnp.int32, (1, LANES), 1)

        for gi, (window, dil) in enumerate(PATTERNS):
            nb = S // dil // ATT_BLK

            @pl.when(g == gi)
            def _(gi=gi, dil=dil, nb=nb):
                _att_bias(bias_ref, dil, sl_ref, hp)

                def step(idx, carry):
                    n, cur, prev = _att_rows(dil, idx, nb)
                    qs = _stack_heads((q_ref[cur, :] * scale).astype(BF16), lane)
                    kc = jnp.concatenate([k_ref[prev, :], k_ref[cur, :]], axis=0).astype(BF16)
                    vc = jnp.concatenate([v_ref[prev, :], v_ref[cur, :]], axis=0).astype(BF16)
                    s = _dot_nt(qs, kc) + bias_ref[jnp.minimum(n, 1)]
                    m = jnp.max(s, axis=-1, keepdims=True)
                    p = jnp.exp(s - m)
                    l = jnp.sum(p, axis=-1, keepdims=True)
                    p16 = p.astype(BF16)
                    o_un = _dot(jnp.concatenate([p16[:ATT_BLK], p16[ATT_BLK:]], axis=1), _stack_heads_rows(vc, lane))
                    og[gi][cur, :] = o_un / _per_head(l, lane)
                    lg[gi][cur, :] = _per_head(m + jnp.log(l), lane)
                    return carry

                lax.fori_loop(0, S // ATT_BLK, step, 0, unroll=16)

        @pl.when(g == n_groups - 1)
        def _():
            def comb(i, carry):
                rows = pl.ds(pl.multiple_of(i * CH, CH), CH)
                a, b, c = lg[0][rows, :], lg[1][rows, :], lg[2][rows, :]
                m = jnp.maximum(jnp.maximum(a, b), c)
                ea, eb, ec = jnp.exp(a - m), jnp.exp(b - m), jnp.exp(c - m)
                z = ea + eb + ec
                o_ref[rows, :] = (ea / z) * og[0][rows, :] + (eb / z) * og[1][rows, :] + (ec / z) * og[2][rows, :]
                lse_ref[rows, :] = m + jnp.log(z)
                return carry

            lax.fori_loop(0, S // CH, comb, 0)

    blk = (None, S, LANES)
    out = pl.BlockSpec(blk, lambda b, hp, g: (b, 0, hp))
    return pl.pallas_call(
        body, name=name, grid=(B, HP, n_groups),
        in_specs=[pl.BlockSpec(memory_space=pltpu.SMEM),
                  pl.BlockSpec(blk, lambda b, hp, g: (b, 0, g * HP + hp)),
                  pl.BlockSpec(blk, lambda b, hp, g: (b, 0, g * 2 * HP + hp)),
                  pl.BlockSpec(blk, lambda b, hp, g: (b, 0, g * 2 * HP + HP + hp))],
        out_specs=[out, out],
        out_shape=[jax.ShapeDtypeStruct((B, S, HP * LANES), F32)] * 2,
        scratch_shapes=[pltpu.VMEM((2, 2 * ATT_BLK, 2 * ATT_BLK), F32)] + [pltpu.VMEM((S, LANES), F32)] * (2 * n_groups),
        compiler_params=_params(3))(slopes, q, kv, kv)


def _stack_heads_rows(x16, lane):
    first = lane < HEAD_DIM
    return jnp.concatenate([jnp.where(first, x16, jnp.zeros_like(x16)),
                            jnp.where(first, jnp.zeros_like(x16), x16)], axis=0)


def _attn_bwd(name, q, kv, slopes, o, lse, do, n_heads, dkv_prev):
    B, S, CQ = q.shape
    HP = n_heads * HEAD_DIM // LANES
    scale = HEAD_DIM ** -0.5
    n_groups = len(PATTERNS)
    n_prev = 0 if dkv_prev is None else 2

    def body(sl_ref, q_ref, k_ref, v_ref, o_ref, lse_ref, do_ref, *rest):
        dq_ref, dk_ref, dv_ref, bias_ref = rest[n_prev:]
        hp, g = pl.program_id(1), pl.program_id(2)
        lane = lax.broadcasted_iota(jnp.int32, (1, LANES), 1)
        first = lane < HEAD_DIM

        def flush(rows, dk, dv):
            if n_prev:
                dk = dk + rest[0][rows, :]
                dv = dv + rest[1][rows, :]
            dk_ref[rows, :] = dk
            dv_ref[rows, :] = dv

        for gi, (window, dil) in enumerate(PATTERNS):
            nb = S // dil // ATT_BLK
            n_blocks = S // ATT_BLK

            @pl.when(g == gi)
            def _(dil=dil, nb=nb, n_blocks=n_blocks):
                _att_bias(bias_ref, dil, sl_ref, hp)

                def block(idx, carry, first_of_all):
                    n, cur, prev = _att_rows(dil, idx, nb)
                    qs = _stack_heads((q_ref[cur, :] * scale).astype(BF16), lane)
                    kc = jnp.concatenate([k_ref[prev, :], k_ref[cur, :]], axis=0).astype(BF16)
                    vc = jnp.concatenate([v_ref[prev, :], v_ref[cur, :]], axis=0).astype(BF16)
                    dob = do_ref[cur, :]
                    prod = dob * o_ref[cur, :]
                    lseb = lse_ref[cur, :]
                    dos = _stack_heads(dob.astype(BF16), lane)
                    delta = jnp.concatenate(
                        [jnp.sum(jnp.where(first, prod, 0.0), axis=-1, keepdims=True),
                         jnp.sum(jnp.where(first, 0.0, prod), axis=-1, keepdims=True)], axis=0)
                    lse_col = jnp.concatenate(
                        [jnp.max(jnp.where(first, lseb, -jnp.inf), axis=-1, keepdims=True),
                         jnp.max(jnp.where(first, -jnp.inf, lseb), axis=-1, keepdims=True)], axis=0)
                    s = _dot_nt(qs, kc) + bias_ref[jnp.minimum(n, 1)]
                    p = jnp.exp(s - lse_col)
                    ds = p * (_dot_nt(dos, vc) - delta)
                    ds16 = ds.astype(BF16)
                    dq = _dot(jnp.concatenate([ds16[:ATT_BLK], ds16[ATT_BLK:]], axis=1), _stack_heads_rows(kc, lane))
                    dq_ref[cur, :] = dq * scale
                    dk = _dot_tn(ds16, qs)
                    dv = _dot_tn(p.astype(BF16), dos)

                    def flush_before():
                        _, before, _ = _att_rows(dil, idx - 1, nb)
                        flush(before, carry[0] + dk[:ATT_BLK], carry[1] + dv[:ATT_BLK])

                    if first_of_all:
                        pl.when(idx > 0)(flush_before)
                    else:
                        flush_before()
                    return dk[ATT_BLK:], dv[ATT_BLK:]

                def step(i, carry):
                    for u in range(BWD_UNROLL):
                        carry = block(i * BWD_UNROLL + u, carry, u == 0)
                    return carry

                zero = jnp.zeros((ATT_BLK, LANES), F32)
                dk_last, dv_last = lax.fori_loop(0, n_blocks // BWD_UNROLL, step, (zero, zero))
                _, last, _ = _att_rows(dil, n_blocks - 1, nb)
                flush(last, dk_last, dv_last)

    blk = (None, S, LANES)
    shared = pl.BlockSpec(blk, lambda b, hp, g: (b, 0, hp))
    grouped = pl.BlockSpec(blk, lambda b, hp, g: (b, 0, g * HP + hp))
    prev = [] if dkv_prev is None else list(dkv_prev)
    gshape = jax.ShapeDtypeStruct((B, S, n_groups * HP * LANES), F32)
    return pl.pallas_call(
        body, name=name, grid=(B, HP, n_groups),
        in_specs=[pl.BlockSpec(memory_space=pltpu.SMEM), grouped,
                  pl.BlockSpec(blk, lambda b, hp, g: (b, 0, g * 2 * HP + hp)),
                  pl.BlockSpec(blk, lambda b, hp, g: (b, 0, g * 2 * HP + HP + hp)),
                  shared, shared, shared] + [grouped] * n_prev,
        out_specs=[grouped] * 3, out_shape=[gshape] * 3,
        scratch_shapes=[pltpu.VMEM((2, 2 * ATT_BLK, 2 * ATT_BLK), F32)],
        compiler_params=_params(3))(slopes, q, kv, kv, o, lse, do, *prev)


def _final_loss(name, h, g, target, tm):
    T, D = h.shape

    def body(h_ref, g_ref, t_ref, loss_ref, dh_ref, dh16_ref, dg_ref):
        hf = h_ref[...]
        gv = g_ref[...]
        rstd = lax.rsqrt(jnp.mean(hf * hf, axis=-1, keepdims=True) + EPS)
        xhat = hf * rstd
        err = xhat * gv - t_ref[...]
        part = 0.5 * jnp.sum(jnp.mean(err * err, axis=-1, keepdims=True), axis=0, keepdims=True)
        dy = err * (1.0 / D)
        dg = jnp.sum(dy * xhat, axis=0, keepdims=True)
        dx = dy * gv
        dh = rstd * (dx - xhat * jnp.mean(dx * xhat, axis=-1, keepdims=True))
        dh_ref[...] = dh
        dh16_ref[...] = dh.astype(BF16)

        @pl.when(pl.program_id(0) == 0)
        def _():
            loss_ref[...] = part
            dg_ref[...] = dg

        @pl.when(pl.program_id(0) > 0)
        def _():
            loss_ref[...] += part
            dg_ref[...] += dg

    return pl.pallas_call(
        body, name=name, grid=(T // tm,),
        in_specs=[pl.BlockSpec((tm, D), lambda i: (i, 0)), pl.BlockSpec((1, D), lambda i: (0, 0)),
                  pl.BlockSpec((tm, D), lambda i: (i, 0))],
        out_specs=[pl.BlockSpec((1, 1), lambda i: (0, 0)), pl.BlockSpec((tm, D), lambda i: (i, 0)),
                   pl.BlockSpec((tm, D), lambda i: (i, 0)), pl.BlockSpec((1, D), lambda i: (0, 0))],
        out_shape=[jax.ShapeDtypeStruct((1, 1), F32), jax.ShapeDtypeStruct((T, D), F32),
                   jax.ShapeDtypeStruct((T, D), BF16), jax.ShapeDtypeStruct((1, D), F32)],
        compiler_params=_params(1))(h, g, target)


def _nt_rows(name, dh, wg, layer, a_mul, out_dtype, tm, deps=()):
    T, D = dh.shape
    rk = wg.shape[2]
    N = N_CHIPS * rk
    with_a = a_mul is not None

    def body(dh_ref, w_ref, *rest):
        o_ref = rest[-1]
        d16 = dh_ref[...]
        for ch in range(N_CHIPS):
            r = _dot_nt(d16, w_ref[ch])
            if with_a:
                r = r * (2.0 * jnp.maximum(rest[0][:, ch * rk:(ch + 1) * rk].astype(F32), 0.0))
            o_ref[:, ch * rk:(ch + 1) * rk] = r.astype(out_dtype)

    in_specs = [pl.BlockSpec((tm, D), lambda i: (i, 0)),
                pl.BlockSpec((N_CHIPS, None, rk, D), lambda i: (0, layer, 0, 0))]
    args = [dh, wg]
    if with_a:
        in_specs.append(pl.BlockSpec((tm, N), lambda i: (i, 0)))
        args.append(a_mul)
    in_specs += [ANY] * len(deps)
    args += list(deps)
    return pl.pallas_call(
        body, name=name, grid=(T // tm,), in_specs=in_specs,
        out_specs=pl.BlockSpec((tm, N), lambda i: (i, 0)),
        out_shape=jax.ShapeDtypeStruct((T, N), out_dtype),
        compiler_params=_params(1))(*args)


def _nt_cols(name, ysegs, wg, layer, tm, norm):
    Nw, cw = wg.shape[2], wg.shape[3]
    widths = [bs[-1] for _, bs, _ in ysegs]
    pieces = _pieces(widths, cw, 1024)
    ns = len(ysegs)
    T = norm[0].shape[0] if norm is not None else ysegs[0][0].shape[-2]

    def body(*refs):
        y_refs = refs[:ns]
        w_ref = refs[ns]
        acc = refs[-1]
        for n, (s, a0, ch, b0, wd) in enumerate(pieces):
            d = _dot_nt(y_refs[s][:, a0:a0 + wd].astype(BF16), w_ref[ch, :, b0:b0 + wd])
            if n == 0:
                acc[...] = d
            else:
                acc[...] += d
        if norm is None:
            refs[ns + 1][...] = acc[...]
        else:
            h_ref, g_ref, dhin_ref, out_ref, out16_ref, dg_ref = refs[ns + 1:ns + 7]
            dh_c, dg = _rms_bwd(h_ref[...], g_ref[...], acc[...])
            dh = dhin_ref[...] + dh_c
            out_ref[...] = dh
            out16_ref[...] = dh.astype(BF16)

            @pl.when(pl.program_id(0) == 0)
            def _():
                dg_ref[...] = dg

            @pl.when(pl.program_id(0) > 0)
            def _():
                dg_ref[...] += dg

    in_specs = [pl.BlockSpec(bs, im) for _, bs, im in ysegs]
    in_specs.append(pl.BlockSpec((N_CHIPS, None, Nw, cw), lambda i: (0, layer, 0, 0)))
    args = [a for a, _, _ in ysegs] + [wg]
    row = pl.BlockSpec((tm, Nw), lambda i: (i, 0))
    vec = pl.BlockSpec((1, Nw), lambda i: (0, 0))
    if norm is None:
        out_specs = row
        out_shape = jax.ShapeDtypeStruct((T, Nw), F32)
    else:
        in_specs += [row, vec, row]
        args += list(norm)
        out_specs = [row, row, vec]
        out_shape = [jax.ShapeDtypeStruct((T, Nw), F32), jax.ShapeDtypeStruct((T, Nw), BF16),
                     jax.ShapeDtypeStruct((1, Nw), F32)]
    return pl.pallas_call(
        body, name=name, grid=(T // tm,), in_specs=in_specs, out_specs=out_specs, out_shape=out_shape,
        scratch_shapes=[pltpu.VMEM((tm, Nw), F32)], compiler_params=_params(1))(*args)


def _tn(name, x, x_act, ysegs, cw, cols_layout, tmm, tt, deps=(), out_dtype=F32):
    T, M = x.shape
    widths = [bs[-1] for _, bs, _ in ysegs]
    N = sum(widths)
    pieces = _pieces(widths, cw if cols_layout else N, 1024)
    ns = len(ysegs)
    n_t = T // tt
    block = (N_CHIPS, tmm, cw) if cols_layout else (tmm, N)
    narrow = out_dtype != F32

    def body(x_ref, *refs):
        y_refs = refs[:ns]
        o_ref = refs[ns + len(deps)]
        acc = refs[-1] if narrow else o_ref

        @pl.when(pl.program_id(1) == 0)
        def _():
            acc[...] = jnp.zeros_like(acc)

        xt = x_act(x_ref[...])
        for s, a0, ch, b0, wd in pieces:
            d = _dot_tn(xt, y_refs[s][:, a0:a0 + wd].astype(BF16))
            if cols_layout:
                acc[ch, :, b0:b0 + wd] += d
            else:
                acc[:, b0:b0 + wd] += d
        if narrow:
            @pl.when(pl.program_id(1) == n_t - 1)
            def _():
                o_ref[...] = acc[...].astype(out_dtype)

    in_specs = [pl.BlockSpec((tt, tmm), lambda m, t: (t, m))] + [pl.BlockSpec(bs, im) for _, bs, im in ysegs]
    in_specs += [ANY] * len(deps)
    if cols_layout:
        out_specs = pl.BlockSpec(block, lambda m, t: (0, m, 0))
        out_shape = jax.ShapeDtypeStruct((N_CHIPS, M, cw), out_dtype)
    else:
        out_specs = pl.BlockSpec(block, lambda m, t: (m, 0))
        out_shape = jax.ShapeDtypeStruct((M, N), out_dtype)
    return pl.pallas_call(
        body, name=name, grid=(M // tmm, n_t), in_specs=in_specs, out_specs=out_specs, out_shape=out_shape,
        scratch_shapes=[pltpu.VMEM(block, F32)] if narrow else [],
        compiler_params=_params(2))(x, *[a for a, _, _ in ysegs], *deps)


def _seg2d(a, t_rows, grid_rank):
    w = a.shape[1]
    if grid_rank == 1:
        return (a, (t_rows, w), lambda i: (i, 0))
    return (a, (t_rows, w), lambda m, t: (t, 0))


def _kv_segments(dk, dv, C, t_rows, grid_rank):
    segs = []
    for g in range(len(PATTERNS)):
        for a in (dk, dv):
            if grid_rank == 1:
                segs.append((a, (t_rows, C), lambda i, g=g: (i, g)))
            else:
                segs.append((a, (t_rows, C), lambda m, t, g=g: (t, g)))
    return segs


def _seg_plane(a, plane, t_rows, grid_rank):
    w = a.shape[2]
    if grid_rank == 1:
        return (a, (None, t_rows, w), lambda i: (plane, i, 0))
    return (a, (None, t_rows, w), lambda m, t: (plane, t, 0))


def _row_tile(rows, row_bytes, budget_bytes=2 * 1024 * 1024):
    t = rows
    while t * row_bytes > budget_bytes and t % 32 == 0:
        t //= 2
    return t


N_DEVICES = 8


def _device_add(name, own, slots, place):
    _, _, hr, c = own.shape
    tr = _row_tile(hr, c * 4, 1024 * 1024)

    def body(place_ref, own_ref, *refs):
        o_ref = refs[-1]
        acc = own_ref[...].astype(F32)
        for r in refs[:-1]:
            acc = acc + r[...].astype(F32)
        o_ref[...] = acc

    def slot(k):
        return pl.BlockSpec((None, tr, c), lambda i, pr: ((2 * pr[0] + pr[1] + k) % N_DEVICES, i, 0))

    grid_spec = pltpu.PrefetchScalarGridSpec(
        num_scalar_prefetch=1, grid=(hr // tr,),
        in_specs=[pl.BlockSpec((None, None, tr, c), lambda i, pr: (pr[0], pr[1], i, 0))]
        + [slot(k) for k in range(1, N_DEVICES)],
        out_specs=pl.BlockSpec((None, tr, c), lambda i, pr: (pr[1], i, 0)))
    return pl.pallas_call(body, name=name, grid_spec=grid_spec,
                          out_shape=jax.ShapeDtypeStruct((2, hr, c), F32),
                          compiler_params=_params(1))(place, own, *[slots] * (N_DEVICES - 1))


def _adamw(name, w, g, m, v):
    rows, cols = w.shape
    tr = _row_tile(rows, cols * 4, 1024 * 1024)

    def body(w_ref, g_ref, m_ref, v_ref, d_ref, nm_ref, nv_ref):
        d_ref[...], nm_ref[...], nv_ref[...] = _adamw_math(w_ref[...], g_ref[...], m_ref[...], v_ref[...])

    spec = pl.BlockSpec((tr, cols), lambda i: (i, 0))
    return pl.pallas_call(
        body, name=name, grid=(rows // tr,), in_specs=[spec] * 4, out_specs=[spec] * 3,
        out_shape=[jax.ShapeDtypeStruct((rows, cols), F32)] * 3, compiler_params=_params(1))(w, g, m, v)


def _adamw_math(w, g, m, v):
    nm = ADAM_B1 * m + (1.0 - ADAM_B1) * g
    nv = ADAM_B2 * v + (1.0 - ADAM_B2) * jnp.square(g)
    m_hat = nm / (1.0 - ADAM_B1 ** ADAM_STEP)
    v_hat = nv / (1.0 - ADAM_B2 ** ADAM_STEP)
    return -ADAM_LR * (m_hat / (jnp.sqrt(v_hat) + ADAM_EPS) + ADAM_WD * w), nm, nv


def _adamw_layers(name, w, grads, m, v):
    L, r, c = w.shape
    tr = _row_tile(r, L * c * 4, 1024 * 1024)

    def body(*refs):
        w_ref, m_ref, v_ref = refs[:3]
        g_refs = refs[3:3 + L]
        go_ref, d_ref, nm_ref, nv_ref = refs[3 + L:]
        for l in range(L):
            g = g_refs[l][...]
            go_ref[l] = g
            d_ref[l], nm_ref[l], nv_ref[l] = _adamw_math(w_ref[l], g, m_ref[l], v_ref[l])

    stacked = pl.BlockSpec((L, tr, c), lambda i: (0, i, 0))
    return pl.pallas_call(
        body, name=name, grid=(r // tr,),
        in_specs=[stacked] * 3 + [pl.BlockSpec((tr, c), lambda i: (i, 0))] * L, out_specs=[stacked] * 4,
        out_shape=[jax.ShapeDtypeStruct((L, r, c), F32)] * 4, compiler_params=_params(1))(w, m, v, *grads)


def _place():
    x, y, c = lax.axis_index("x"), lax.axis_index("y"), lax.axis_index("c")
    chips = [(1 - x, y), (x, 1 - y), (1 - x, 1 - y)]
    return x, y, c, chips


HBM = pl.BlockSpec(memory_space=pltpu.HBM)
SEM = pl.BlockSpec(memory_space=pltpu.SEMAPHORE)
EFFECT = pltpu.SideEffectType.DATAFLOW_SIDE_EFFECTING


class _Copy:
    def __init__(self, src, src_view, land, dst_view, recv_view, target):
        self.src, self.src_view, self.land, self.dst_view, self.recv_view, self.target = (
            src, src_view, land, dst_view, recv_view, target)


def _whole(ref, place):
    return ref


def _split_start(name, srcs, land_shapes, plans):
    skeys, lkeys = list(srcs), list(land_shapes)
    ns, nl, ng = len(skeys), len(lkeys), len(plans)

    def body(*refs):
        src = dict(zip(skeys, refs[:ns]))
        land = dict(zip(lkeys, refs[ns:ns + nl]))
        sems = refs[ns + nl:ns + nl + 2 * ng]
        token = refs[-1]
        place = _place()
        for gi, plan in enumerate(plans):
            for k, cp in enumerate(plan):
                dst = land[cp.land] if cp.land in land else src[cp.land]
                pltpu.make_async_remote_copy(
                    src_ref=cp.src_view(src[cp.src], place), dst_ref=cp.dst_view(dst, place),
                    send_sem=sems[2 * gi].at[k], recv_sem=sems[2 * gi + 1].at[k],
                    device_id=cp.target(place), device_id_type=MESH).start()
        token[...] = jnp.zeros_like(token)

    sem_shapes = []
    for plan in plans:
        sem_shapes += [pltpu.SemaphoreType.DMA((len(plan),))] * 2
    buffers = [srcs[k] for k in skeys] + [lax.empty(land_shapes[k].shape, land_shapes[k].dtype) for k in lkeys]
    outs = pl.pallas_call(
        body, name=name,
        out_shape=(*sem_shapes, *[pltpu.HBM(a.shape, a.dtype) for a in buffers], jax.ShapeDtypeStruct((8, LANES), F32)),
        in_specs=[HBM] * (ns + nl),
        out_specs=(*[SEM] * (2 * ng), *[HBM] * (ns + nl), pl.BlockSpec(memory_space=pltpu.VMEM)),
        input_output_aliases={i: 2 * ng + i for i in range(ns + nl)},
        compiler_params=pltpu.CompilerParams(has_side_effects=EFFECT),
    )(*[pltpu.with_memory_space_constraint(a, pltpu.HBM) for a in buffers])
    sems = [(outs[2 * gi], outs[2 * gi + 1]) for gi in range(ng)]
    thru = outs[2 * ng:2 * ng + ns + nl]
    return sems, dict(zip(skeys, thru[:ns])), dict(zip(lkeys, thru[ns:])), outs[-1]


def _split_wait(name, sems, srcs, lands, plan, after):
    skeys, lkeys = list(srcs), list(lands)
    ns, nl = len(skeys), len(lkeys)

    def body(*refs):
        src = dict(zip(skeys, refs[:ns]))
        land = dict(zip(lkeys, refs[ns:ns + nl]))
        ssem, rsem = refs[ns + nl], refs[ns + nl + 1]
        place = _place()
        for k, cp in enumerate(plan):
            dst = land[cp.land] if cp.land in land else src[cp.land]
            pltpu.make_async_remote_copy(
                src_ref=cp.src_view(src[cp.src], place), dst_ref=cp.dst_view(dst, place),
                send_sem=ssem.at[k], recv_sem=rsem.at[k],
                device_id=cp.target(place), device_id_type=MESH).wait_send()
            got = cp.recv_view(dst, place)
            pltpu.make_async_remote_copy(
                src_ref=got, dst_ref=got, send_sem=ssem.at[k], recv_sem=rsem.at[k],
                device_id=cp.target(place), device_id_type=MESH).wait_recv()

    buffers = [srcs[k] for k in skeys] + [lands[k] for k in lkeys]
    outs = pl.pallas_call(
        body, name=name, out_shape=tuple(pltpu.HBM(a.shape, a.dtype) for a in buffers),
        in_specs=(*[HBM] * (ns + nl), SEM, SEM, ANY), out_specs=tuple([HBM] * (ns + nl)),
        input_output_aliases={i: i for i in range(ns + nl)},
        compiler_params=pltpu.CompilerParams(has_side_effects=EFFECT),
    )(*buffers, sems[0], sems[1], after)
    return dict(zip(skeys, outs[:ns])), dict(zip(lkeys, outs[ns:]))


def _chip_of(place):
    x, y, c, chips = place
    return 2 * x + y


GATHER_FIRST = 2


class _WeightGather:
    def __init__(self, blocks):
        self.plans, shapes = {}, {}
        for key, a in blocks.items():
            shapes[key] = jax.ShapeDtypeStruct((N_CHIPS,) + a.shape, a.dtype)
            slot = lambda ref, place: ref.at[_chip_of(place)]
            plan = [_Copy(key, _whole, key, slot,
                          lambda ref, place, k=k: ref.at[2 * place[3][k][0] + place[3][k][1]],
                          lambda place, k=k: (place[3][k][0], place[3][k][1], place[2])) for k in range(3)]
            plan.append(_Copy(key, _whole, key, slot, slot, lambda place: (place[0], place[1], 1 - place[2])))
            self.plans[key] = plan
        keys = list(blocks)
        self.sems, self.srcs, self.lands = {}, {}, {}
        for name, part in (("gather_start_first", keys[:GATHER_FIRST]), ("gather_start", keys[GATHER_FIRST:])):
            sems, srcs, lands, self.token = _split_start(name, {k: blocks[k] for k in part}, {k: shapes[k] for k in part},
                                                         [self.plans[k] for k in part])
            self.sems.update(zip(part, sems))
            self.srcs.update(srcs)
            self.lands.update(lands)

    def get(self, l, name, after):
        key = (l, name)
        _, lands = _split_wait(f"gather_wait_{name}{l}", self.sems[key], {key: self.srcs[key]},
                               {key: self.lands[key]}, self.plans[key], after)
        return lands[key][:, None]


class _GradReduce:
    def __init__(self, place):
        self.place = place
        self.jobs = []
        self.done = {}
        self.n = 0

    def submit(self, grads):
        views = {k: a.reshape(N_CHIPS, 2, a.shape[1] // 2, a.shape[2]) for k, a in grads.items()}
        shapes = {k: jax.ShapeDtypeStruct((N_DEVICES,) + a.shape[2:], a.dtype) for k, a in views.items()}

        def peer(place, k):
            x, y, c, _ = place
            return (1 - x if k & 4 else x, 1 - y if k & 2 else y, 1 - c if k & 1 else c)

        def index(dev):
            return 4 * dev[0] + 2 * dev[1] + dev[2]

        plan = []
        for key in views:
            for k in range(1, N_DEVICES):
                plan.append(_Copy(
                    key, lambda ref, place, k=k: ref.at[2 * peer(place, k)[0] + peer(place, k)[1], peer(place, k)[2]],
                    key, lambda ref, place: ref.at[index(place[:3])],
                    lambda ref, place, k=k: ref.at[index(peer(place, k))],
                    lambda place, k=k: peer(place, k)))
        sems, srcs, lands, token = _split_start(f"grad_start{self.n}", views, shapes, [plan])
        self.jobs.append(dict(id=self.n, sems=sems[0], srcs=srcs, lands=lands, plan=plan))
        self.n += 1
        return token

    def pump(self, after):
        return []

    def finish(self, after):
        for job in self.jobs:
            srcs, lands = _split_wait(f"grad_wait{job['id']}", job["sems"], job["srcs"], job["lands"], job["plan"],
                                      after)
            for i, k in enumerate(srcs):
                self.done[k] = _device_add(f"grad_add{job['id']}_{i}", srcs[k], lands[k], self.place)
        self.jobs = []
        return self.done


class _PairShare:
    def __init__(self, halves, types):
        sibling = lambda place: (place[0], place[1], 1 - place[2])
        mine = lambda ref, place: ref.at[place[2]]
        theirs = lambda ref, place: ref.at[1 - place[2]]
        self.plans = {t: [_Copy(k, mine, k, mine, theirs, sibling) for k in halves if k[0] == t] for t in types}
        sems, self.bufs, _, self.token = _split_start("share_start", halves, {}, list(self.plans.values()))
        self.sems = dict(zip(self.plans, sems))

    def get(self, t, after):
        keys = [cp.src for cp in self.plans[t]]
        bufs, _ = _split_wait(f"share_wait_{t}", self.sems[t], {k: self.bufs[k] for k in keys}, {}, self.plans[t], after)
        return bufs


def _small_allreduce(part):
    R, C = part.shape
    N_DEV = 8

    def body(in_ref, out_ref, slots, ssem, rsem):
        x, y, c, _ = _place()
        me = 4 * x + 2 * y + c
        sends = []
        for k in range(1, N_DEV):
            kx, ky, kc = (k >> 2) & 1, (k >> 1) & 1, k & 1
            peer = (1 - x if kx else x, 1 - y if ky else y, 1 - c if kc else c)
            cp = pltpu.make_async_remote_copy(
                src_ref=in_ref, dst_ref=slots.at[me], send_sem=ssem.at[k], recv_sem=rsem.at[k],
                device_id=peer, device_id_type=MESH)
            cp.start()
            sends.append(cp)
        slots[me] = in_ref[...]
        for k in range(1, N_DEV):
            kx, ky, kc = (k >> 2) & 1, (k >> 1) & 1, k & 1
            peer = (1 - x if kx else x, 1 - y if ky else y, 1 - c if kc else c)
            slot = slots.at[4 * peer[0] + 2 * peer[1] + peer[2]]
            pltpu.make_async_remote_copy(
                src_ref=slot, dst_ref=slot, send_sem=ssem.at[k], recv_sem=rsem.at[k],
                device_id=peer, device_id_type=MESH).wait_recv()
        acc = slots[0]
        for d in range(1, N_DEV):
            acc = acc + slots[d]
        out_ref[...] = acc
        for cp in sends:
            cp.wait_send()

    vm = pl.BlockSpec(memory_space=pltpu.VMEM)
    return pl.pallas_call(
        body, name="small_allreduce", in_specs=[vm], out_specs=vm,
        out_shape=jax.ShapeDtypeStruct((R, C), F32),
        scratch_shapes=[pltpu.VMEM((N_DEV, R, C), F32), pltpu.SemaphoreType.DMA((N_DEV,)),
                        pltpu.SemaphoreType.DMA((N_DEV,))])(part)


def _local_step(x, target, norm_mix, norm_mlp, norm_kv, norm_final, weights, sink, n_a, n_heads):
    B, S, D = x.shape
    T = B * S
    C = n_heads * HEAD_DIM
    depth = norm_mix.shape[0]
    slopes = 2.0 ** (-ALIBI_MAX_BIAS * jnp.arange(1, n_heads + 1, dtype=F32) / n_heads)
    tm = min(512, T)
    row = lambda v: v.reshape(1, -1)

    h = x.reshape(T, D)
    saved, Wl = [], []
    kv = nkv = h_kv = cwg = None
    for l in range(depth):
        s = {"h_in": h}
        w = {}
        Wl.append(w)
        if l < n_a:
            w["w_a_in"] = weights.get(l, "w_a_in", h)
            first = [weights.token] if l == 0 else []
            s["n1"], bcu = _norm_mm(f"a_in_fwd{l}", h, row(norm_mix[l]), w["w_a_in"], 0, 3, BF16, tm, first)
            s["bcu"] = bcu.reshape(3, B, S, D)
            if l == 0:
                cwg = weights.get(0, "conv", bcu)[:, 0, :n_a * 3].reshape(N_CHIPS, n_a, 3, -1)
            s["z"] = _conv_fwd(f"conv_fwd{l}", s["bcu"], cwg, l, CONV_COLS).reshape(T, D)
            w["w_a_out"] = weights.get(l, "w_a_out", s["z"])
            h = _mm_res_rows(f"a_out_fwd{l}", s["z"], w["w_a_out"], 0, h, _to_bf16, tm)
        else:
            i = l - n_a
            if i == 0:
                h_kv = h
                w["w_kv"] = weights.get(l, "w_kv", h)
                nkv, kv = _norm_mm("kv_fwd", h, row(norm_kv), w["w_kv"], 0, 1, F32, tm)
                kv = kv.reshape(B, S, 2 * 3 * C)
            w["w_q"] = weights.get(l, "w_q", h)
            s["n1"], q = _norm_mm(f"q_fwd{i}", h, row(norm_mix[l]), w["w_q"], 0, 1, F32, tm)
            s["q"] = q.reshape(B, S, 3 * C)
            o, lse = _attn_fwd(f"attn_fwd{i}", s["q"], kv, slopes, n_heads)
            s["o"], s["lse"] = o.reshape(T, C), lse.reshape(T, C)
            w["w_o"] = weights.get(l, "w_o", o)
            h = _mm_res_cols(f"o_fwd{i}", s["o"], w["w_o"], 0, h, tm)
        s["h_mid"] = h
        w["w_up"] = weights.get(l, "w_up", h)
        if l < n_a:
            s["n2"], a = _norm_mm(f"up_fwd{l}", h, row(norm_mlp[l]), w["w_up"], 0, 1, BF16, tm)
            s["a"] = a[0]
            w["w_down"] = weights.get(l, "w_down", a)
            h = _mm_res_rows(f"down_fwd{l}", s["a"], w["w_down"], 0, h, _relu2_bf16, tm)
        else:
            w["w_down"] = weights.get(l, "w_down", h)
            s["n2"], s["a"], h = _mlp_fwd(f"mlp_fwd{l}", h, row(norm_mlp[l]), w["w_up"], w["w_down"], tm)
        F = s["a"].shape[1]
        saved.append(s)

    loss, dh, dh16, dg_final = _final_loss("loss_head", h, row(norm_final), target.reshape(T, D), tm)

    g_mix, g_mlp = [None] * depth, [None] * depth
    g_conv = [None] * n_a
    dkv = None
    tt = min(512, T)
    deps = []
    for l in reversed(range(depth)):
        s, w = saved[l], Wl[l]
        g_down = _tn(f"down_wgrad{l}", s["a"], _relu2_bf16, [_seg2d(dh16, tt, 2)], None, False,
                     min(2048, F), tt, deps, BF16).reshape(N_CHIPS, F // N_CHIPS, D)
        da, dh, dh16, g_mlp[l] = _mlp_bwd(f"mlp_bwd{l}", dh, dh16, s["a"], w["w_down"], w["w_up"], s["h_mid"],
                                          row(norm_mlp[l]), tm)
        g_up = _tn(f"up_wgrad{l}", s["n2"], _to_bf16, [_seg2d(da, tt, 2)], F // N_CHIPS, True, D, tt, (), BF16)
        deps = sink.pump(dh) + [sink.submit({("w_up", l): g_up, ("w_down", l): g_down})]
        if l < n_a:
            g_out = _tn(f"a_out_wgrad{l}", s["z"], _to_bf16, [_seg2d(dh16, tt, 2)], None, False,
                        D, tt, deps, BF16).reshape(N_CHIPS, D // N_CHIPS, D)
            dz = _nt_rows(f"a_out_bwd{l}", dh16, w["w_a_out"], 0, None, F32, tm)
            deps = sink.pump(dz) + [sink.submit({("w_a_out", l): g_out})]
            dbcu, g_conv[l] = _conv_bwd(f"conv_bwd{l}", s["bcu"], dz.reshape(B, S, D), cwg, l, CONV_COLS)
            dbcu = dbcu.reshape(3, T, D)
            g_in = _tn(f"a_in_wgrad{l}", s["n1"], _to_bf16, [_seg_plane(dbcu, p, tt, 2) for p in range(3)],
                       3 * D // N_CHIPS, True, D, tt, deps, BF16)
            dh, dh16, g_mix[l] = _nt_cols(f"a_in_bwd{l}", [_seg_plane(dbcu, p, tm, 1) for p in range(3)],
                                          w["w_a_in"], 0, tm, (s["h_in"], row(norm_mix[l]), dh))
            mixer = {("w_a_in", l): g_in}
        else:
            i = l - n_a
            g_o = _tn(f"o_wgrad{i}", s["o"], _to_bf16, [_seg2d(dh16, tt, 2)], D // N_CHIPS, True, C, tt, deps,
                      BF16)
            do = _nt_cols(f"o_bwd{i}", [_seg2d(dh16, tm, 1)], w["w_o"], 0, tm, None)
            deps = sink.pump(do) + [sink.submit({("w_o", i): g_o})]
            dq, dk, dv = _attn_bwd(f"attn_bwd{i}", s["q"], kv, slopes, s["o"].reshape(B, S, C),
                                   s["lse"].reshape(B, S, C), do.reshape(B, S, C), n_heads, dkv)
            dkv = (dk, dv)
            dq = dq.reshape(T, 3 * C)
            g_q = _tn(f"q_wgrad{i}", s["n1"], _to_bf16, [_seg2d(dq, tt, 2)], 3 * C // N_CHIPS, True, D, tt, deps,
                      BF16)
            dh, dh16, g_mix[l] = _nt_cols(f"q_bwd{i}", [_seg2d(dq, tm, 1)], w["w_q"], 0, tm,
                                          (s["h_in"], row(norm_mix[l]), dh))
            mixer = {("w_q", i): g_q}
            if i == 0:
                dk2, dv2 = (t.reshape(T, 3 * C) for t in dkv)
                mixer[("w_kv", 0)] = _tn("kv_wgrad", nkv, _to_bf16, _kv_segments(dk2, dv2, C, tt, 2),
                                         6 * C // N_CHIPS, True, D, tt, (), BF16)
                dh, dh16, g_kv = _nt_cols("kv_bwd", _kv_segments(dk2, dv2, C, tm, 1), w["w_kv"], 0, tm,
                                          (h_kv, row(norm_kv), dh))
        deps = sink.pump(dh) + [sink.submit(mixer)]
    small = dict(norm_mix=jnp.concatenate(g_mix, axis=0), norm_mlp=jnp.concatenate(g_mlp, axis=0),
                 norm_kv=g_kv, norm_final=dg_final, conv_w=jnp.stack(g_conv))
    return loss, dh.reshape(B, S, D), small


BIG = ("w_a_in", "w_a_out", "w_kv", "w_q", "w_o", "w_up", "w_down")
CONV_PAD_ROWS = 16


def kernel(x, norm_mix, norm_mlp, w_a_in, conv_w, w_a_out, norm_kv, w_kv, w_q, w_o, w_up, w_down, norm_final, loss_target, m_norm_mix, m_norm_mlp, m_w_a_in, m_conv_w, m_w_a_out, m_norm_kv, m_w_kv, m_w_q, m_w_o, m_w_up, m_w_down, m_norm_final, v_norm_mix, v_norm_mlp, v_w_a_in, v_conv_w, v_w_a_out, v_norm_kv, v_w_kv, v_w_q, v_w_o, v_w_up, v_w_down, v_norm_final):
    D = x.shape[-1]
    w = dict(norm_mix=norm_mix, norm_mlp=norm_mlp, w_a_in=w_a_in, conv_w=conv_w, w_a_out=w_a_out, norm_kv=norm_kv,
             w_kv=w_kv[None], w_q=w_q, w_o=w_o, w_up=w_up, w_down=w_down, norm_final=norm_final)
    m = dict(norm_mix=m_norm_mix, norm_mlp=m_norm_mlp, w_a_in=m_w_a_in, conv_w=m_conv_w, w_a_out=m_w_a_out,
             norm_kv=m_norm_kv, w_kv=m_w_kv[None], w_q=m_w_q, w_o=m_w_o, w_up=m_w_up, w_down=m_w_down,
             norm_final=m_norm_final)
    v = dict(norm_mix=v_norm_mix, norm_mlp=v_norm_mlp, w_a_in=v_w_a_in, conv_w=v_conv_w, w_a_out=v_w_a_out,
             norm_kv=v_norm_kv, w_kv=v_w_kv[None], w_q=v_w_q, w_o=v_w_o, w_up=v_w_up, w_down=v_w_down,
             norm_final=v_norm_final)
    depth = norm_mix.shape[0]
    n_a, taps, cwc = conv_w.shape
    n_heads = w_o.shape[1] // HEAD_DIM

    conv_rows = jnp.zeros((CONV_PAD_ROWS, cwc), F32).at[:n_a * taps].set(conv_w.reshape(n_a * taps, cwc))
    blocks = {}
    for l in range(depth):
        if l < n_a:
            blocks[(l, "w_a_in")] = w_a_in[l].astype(BF16)
            if l == 0:
                blocks[(0, "conv")] = conv_rows
            blocks[(l, "w_a_out")] = w_a_out[l].astype(BF16)
        else:
            if l == n_a:
                blocks[(l, "w_kv")] = w_kv.astype(BF16)
            blocks[(l, "w_q")] = w_q[l - n_a].astype(BF16)
            blocks[(l, "w_o")] = w_o[l - n_a].astype(BF16)
        blocks[(l, "w_up")] = w_up[l].astype(BF16)
        blocks[(l, "w_down")] = w_down[l].astype(BF16)
    weights = _WeightGather(blocks)
    place = jnp.stack([2 * lax.axis_index("x") + lax.axis_index("y"), lax.axis_index("c")]).astype(jnp.int32)
    sink = _GradReduce(place)

    loss, grad_x, small = _local_step(x, loss_target, norm_mix, norm_mlp, norm_kv, norm_final, weights, sink,
                                      n_a, n_heads)
    loss = lax.psum(loss[0, 0], ("x", "y", "c"))

    share = _PairShare(sink.finish(grad_x), BIG)
    grads = {}

    packed = jnp.concatenate([small["norm_mix"], small["norm_mlp"], small["norm_kv"], small["norm_final"],
                              small["conv_w"].reshape(n_a * taps, D)], axis=0)
    pad = (-packed.shape[0]) % 8
    packed = jnp.pad(packed, ((0, pad), (0, 0)))
    total = _small_allreduce(packed)
    grads["norm_mix"] = total[:depth]
    grads["norm_mlp"] = total[depth:2 * depth]
    grads["norm_kv"] = total[2 * depth]
    grads["norm_final"] = total[2 * depth + 1]
    chip = 2 * lax.axis_index("x") + lax.axis_index("y")
    conv_full = total[2 * depth + 2:2 * depth + 2 + n_a * taps].reshape(n_a, taps, N_CHIPS, cwc)
    grads["conv_w"] = lax.dynamic_index_in_dim(conv_full, chip, axis=2, keepdims=False)

    order = ("norm_mix", "norm_mlp", "w_a_in", "conv_w", "w_a_out", "norm_kv", "w_kv", "w_q", "w_o", "w_up",
             "w_down", "norm_final")
    delta, new_m, new_v = {}, {}, {}
    vec_names = ("norm_mix", "norm_mlp", "norm_kv", "norm_final")
    rows_of = lambda a: a.reshape(-1, D)
    vw, vg, vm_, vv = (jnp.concatenate([rows_of(t[k]) for k in vec_names], axis=0) for t in (w, grads, m, v))
    vpad = (-vw.shape[0]) % 8
    padrows = lambda a: jnp.pad(a, ((0, vpad), (0, 0)))
    vd, vnm, vnv = _adamw("adamw_norms", padrows(vw), padrows(vg), padrows(vm_), padrows(vv))
    off = 0
    for k in vec_names:
        r = rows_of(w[k]).shape[0]
        delta[k] = vd[off:off + r].reshape(w[k].shape)
        new_m[k] = vnm[off:off + r].reshape(w[k].shape)
        new_v[k] = vnv[off:off + r].reshape(w[k].shape)
        off += r
    cpad = (-n_a * taps) % 8
    two_d = lambda a: jnp.pad(a.reshape(-1, cwc), ((0, cpad), (0, 0)))
    cd, cnm, cnv = _adamw("adamw_conv_w", two_d(w["conv_w"]), two_d(grads["conv_w"]), two_d(m["conv_w"]),
                          two_d(v["conv_w"]))
    delta["conv_w"], new_m["conv_w"], new_v["conv_w"] = (t[:n_a * taps].reshape(conv_w.shape) for t in (cd, cnm, cnv))
    after = cd
    for k in sorted(BIG, key=lambda k: w[k].size):
        shared = share.get(k, after)
        per_layer = [shared[(k, l)].reshape(w[k].shape[1:]) for l in range(w[k].shape[0])]
        grads[k], delta[k], new_m[k], new_v[k] = _adamw_layers(f"adamw_{k}", w[k], per_layer, m[k], v[k])
        after = delta[k]
    fix = lambda k, a: a[0] if k == "w_kv" else a
    return (loss, grad_x, *[fix(k, grads[k]) for k in order], *[fix(k, delta[k]) for k in order],
            *[fix(k, new_m[k]) for k in order], *[fix(k, new_v[k]) for k in order])
```

```python
import jax
import jax.numpy as jnp
from jax import lax
from jax.experimental import pallas as pl
from jax.experimental.pallas import tpu as pltpu

F32 = jnp.float32
BF16 = jnp.bfloat16
MESH = pl.DeviceIdType.MESH

EPS = 1e-5
PATTERNS = ((128, 1), (512, 4), (2048, 16))
HEAD_DIM = 64
ALIBI_MAX_BIAS = 8.0
NEG_INF = -1e30
ATT_BLK = 128
BWD_UNROLL = 16
N_CHIPS = 4
LANES = 128
VMEM_LIMIT = 56 * 1024 * 1024

ADAM_LR = 0.001
ADAM_B1 = 0.9
ADAM_B2 = 0.999
ADAM_EPS = 1e-08
ADAM_WD = 0.01
ADAM_STEP = 10


ANY = pl.BlockSpec(memory_space=pl.ANY)


def _params(n_grid_axes):
    return pltpu.CompilerParams(dimension_semantics=("arbitrary",) * n_grid_axes, vmem_limit_bytes=VMEM_LIMIT)


def _dot(a, b):
    return jnp.dot(a, b, preferred_element_type=F32)


def _dot_nt(a, b):
    return lax.dot_general(a, b, (((1,), (1,)), ((), ())), preferred_element_type=F32)


def _dot_tn(a, b):
    return lax.dot_general(a, b, (((0,), (0,)), ((), ())), preferred_element_type=F32)


def _relu2(a):
    return jnp.square(jnp.maximum(a, 0.0))


def _rms(hf, g):
    y = hf * lax.rsqrt(jnp.mean(hf * hf, axis=-1, keepdims=True) + EPS)
    return y * g


def _rms_bwd(hf, g, dn):
    rstd = lax.rsqrt(jnp.mean(hf * hf, axis=-1, keepdims=True) + EPS)
    xhat = hf * rstd
    dg = jnp.sum(dn * xhat, axis=0, keepdims=True)
    dx = dn * g
    dh = rstd * (dx - xhat * jnp.mean(dx * xhat, axis=-1, keepdims=True))
    return dh, dg


def _pieces(seg_widths, chunk_width, max_width):
    total = sum(seg_widths)
    cuts = {0, total}
    acc = 0
    for w in seg_widths:
        cuts.add(acc)
        acc += w
    cuts.update(range(0, total, chunk_width))
    cuts = sorted(cuts)
    fine = []
    for lo, hi in zip(cuts[:-1], cuts[1:]):
        while hi - lo > max_width:
            fine.append((lo, lo + max_width))
            lo += max_width
        fine.append((lo, hi))
    out = []
    for lo, hi in fine:
        acc = 0
        for s, w in enumerate(seg_widths):
            if lo < acc + w:
                break
            acc += w
        out.append((s, lo - acc, lo // chunk_width, lo % chunk_width, hi - lo))
    return out


def _relu2_bf16(a):
    return _relu2(a.astype(F32)).astype(BF16)


def _to_bf16(a):
    return a.astype(BF16)


def _norm_mm(name, h, g, wg, layer, planes, out_dtype, tm, deps=()):
    T, D = h.shape
    cw = wg.shape[3]
    N = N_CHIPS * cw
    pw = N // planes
    pieces = _pieces([pw] * planes, cw, 512)

    def body(h_ref, g_ref, w_ref, *rest):
        n_ref, o_ref = rest[len(deps):]
        n = _rms(h_ref[...], g_ref[...]).astype(BF16)
        n_ref[...] = n
        for s, a0, ch, b0, wd in pieces:
            o_ref[s, :, a0:a0 + wd] = _dot(n, w_ref[ch, :, b0:b0 + wd]).astype(out_dtype)

    return pl.pallas_call(
        body, name=name, grid=(T // tm,),
        in_specs=[pl.BlockSpec((tm, D), lambda i: (i, 0)),
                  pl.BlockSpec((1, D), lambda i: (0, 0)),
                  pl.BlockSpec((N_CHIPS, None, D, cw), lambda i: (0, layer, 0, 0))] + [ANY] * len(deps),
        out_specs=[pl.BlockSpec((tm, D), lambda i: (i, 0)),
                   pl.BlockSpec((planes, tm, pw), lambda i: (0, i, 0))],
        out_shape=[jax.ShapeDtypeStruct((T, D), BF16), jax.ShapeDtypeStruct((planes, T, pw), out_dtype)],
        compiler_params=_params(1))(h, g, wg, *deps)


def _resident(shape, index_map):
    return pl.BlockSpec(shape, index_map, pipeline_mode=pl.Buffered(1))


def _mm_res_rows(name, a, wg, layer, h, act, tm):
    T = a.shape[0]
    rk, D = wg.shape[2], wg.shape[3]

    def body(a_ref, w_ref, h_ref, o_ref):
        acc = h_ref[...]
        for k in range(N_CHIPS):
            acc = acc + _dot(act(a_ref[:, k * rk:(k + 1) * rk]), w_ref[k])
        o_ref[...] = acc

    return pl.pallas_call(
        body, name=name, grid=(T // tm,),
        in_specs=[pl.BlockSpec((tm, N_CHIPS * rk), lambda i: (i, 0)),
                  pl.BlockSpec((N_CHIPS, None, rk, D), lambda i: (0, layer, 0, 0)),
                  pl.BlockSpec((tm, D), lambda i: (i, 0))],
        out_specs=pl.BlockSpec((tm, D), lambda i: (i, 0)),
        out_shape=jax.ShapeDtypeStruct((T, D), F32),
        compiler_params=_params(1))(a, wg, h)


def _mm_res_cols(name, a, wg, layer, h, tm):
    T, K = a.shape
    cw = wg.shape[3]
    D = N_CHIPS * cw

    def body(a_ref, w_ref, h_ref, o_ref):
        a16 = a_ref[...].astype(BF16)
        for j in range(N_CHIPS):
            o_ref[:, j * cw:(j + 1) * cw] = h_ref[:, j * cw:(j + 1) * cw] + _dot(a16, w_ref[j])

    return pl.pallas_call(
        body, name=name, grid=(T // tm,),
        in_specs=[pl.BlockSpec((tm, K), lambda i: (i, 0)),
                  pl.BlockSpec((N_CHIPS, None, K, cw), lambda i: (0, layer, 0, 0)),
                  pl.BlockSpec((tm, D), lambda i: (i, 0))],
        out_specs=pl.BlockSpec((tm, D), lambda i: (i, 0)),
        out_shape=jax.ShapeDtypeStruct((T, D), F32),
        compiler_params=_params(1))(a, wg, h)


def _mlp_fwd(name, h, g, wup, wdown, tm):
    T, D = h.shape
    cw = wup.shape[3]

    def body(h_ref, g_ref, wu_ref, wd_ref, n_ref, a_ref, o_ref):
        hf = h_ref[...]
        n = _rms(hf, g_ref[...]).astype(BF16)
        n_ref[...] = n
        acc = hf
        for ch in range(N_CHIPS):
            a16 = _dot(n, wu_ref[ch]).astype(BF16)
            a_ref[:, ch * cw:(ch + 1) * cw] = a16
            acc = acc + _dot(_relu2_bf16(a16), wd_ref[ch])
        o_ref[...] = acc

    row = pl.BlockSpec((tm, D), lambda i: (i, 0))
    return pl.pallas_call(
        body, name=name, grid=(T // tm,),
        in_specs=[row, pl.BlockSpec((1, D), lambda i: (0, 0)),
                  _resident((N_CHIPS, None, D, cw), lambda i: (0, 0, 0, 0)),
                  _resident((N_CHIPS, None, cw, D), lambda i: (0, 0, 0, 0))],
        out_specs=[row, pl.BlockSpec((tm, N_CHIPS * cw), lambda i: (i, 0)), row],
        out_shape=[jax.ShapeDtypeStruct((T, D), BF16), jax.ShapeDtypeStruct((T, N_CHIPS * cw), BF16),
                   jax.ShapeDtypeStruct((T, D), F32)],
        compiler_params=_params(1))(h, g, wup, wdown)


def _mlp_bwd(name, dh, dh16, a, wdown, wup, h_mid, g, tm, deps=()):
    T, D = dh.shape
    cw = wup.shape[3]
    F = N_CHIPS * cw

    def body(dh_ref, dh16_ref, a_ref, wd_ref, wu_ref, h_ref, g_ref, *rest):
        da_ref, out_ref, out16_ref, dg_ref = rest[len(deps):]
        d16 = dh16_ref[...]
        acc = None
        for ch in range(N_CHIPS):
            cols = slice(ch * cw, (ch + 1) * cw)
            da = (_dot_nt(d16, wd_ref[ch]) * (2.0 * jnp.maximum(a_ref[:, cols].astype(F32), 0.0))).astype(BF16)
            da_ref[:, cols] = da
            d = _dot_nt(da, wu_ref[ch])
            acc = d if acc is None else acc + d
        dh_c, dg = _rms_bwd(h_ref[...], g_ref[...], acc)
        out = dh_ref[...] + dh_c
        out_ref[...] = out
        out16_ref[...] = out.astype(BF16)

        @pl.when(pl.program_id(0) == 0)
        def _():
            dg_ref[...] = dg

        @pl.when(pl.program_id(0) > 0)
        def _():
            dg_ref[...] += dg

    row = pl.BlockSpec((tm, D), lambda i: (i, 0))
    wide = pl.BlockSpec((tm, F), lambda i: (i, 0))
    vec = pl.BlockSpec((1, D), lambda i: (0, 0))
    return pl.pallas_call(
        body, name=name, grid=(T // tm,),
        in_specs=[row, row, wide, _resident((N_CHIPS, None, cw, D), lambda i: (0, 0, 0, 0)),
                  _resident((N_CHIPS, None, D, cw), lambda i: (0, 0, 0, 0)), row, vec] + [ANY] * len(deps),
        out_specs=[wide, row, row, vec],
        out_shape=[jax.ShapeDtypeStruct((T, F), BF16), jax.ShapeDtypeStruct((T, D), F32),
                   jax.ShapeDtypeStruct((T, D), BF16), jax.ShapeDtypeStruct((1, D), F32)],
        compiler_params=_params(1))(dh, dh16, a, wdown, wup, h_mid, g, *deps)


CONV_ROWS = 256
CONV_HALO = 16
CONV_COLS = 2 * LANES


def _conv_shifted(ext, k, r0, rows, at_start):
    rolled = pltpu.roll(ext, k, 0)[CONV_HALO:]
    if not at_start:
        return rolled
    t = r0 + lax.broadcasted_iota(jnp.int32, rolled.shape, 0)
    return jnp.where(t >= k, rolled, 0.0)


def _conv_ahead(ext, k, r0, rows, S, at_end):
    rolled = pltpu.roll(ext, rows + CONV_HALO - k, 0)[:rows]
    if not at_end:
        return rolled
    t = r0 + lax.broadcasted_iota(jnp.int32, rolled.shape, 0)
    return jnp.where(t + k < S, rolled, 0.0)


def _conv_chunks(step, n, carry):
    carry = step(0, carry, True, n == 1)
    if n > 2:
        carry = lax.fori_loop(1, n - 1, lambda i, c: step(i, c, False, False), carry)
    if n > 1:
        carry = step(n - 1, carry, False, True)
    return carry


def _conv_fwd(name, bcu, cwg, layer, tc):
    _, B, S, D = bcu.shape
    cwc = cwg.shape[3]
    per_chunk = cwc // tc
    R = min(CONV_ROWS, S)

    def body(x_ref, w_ref, z_ref):
        w = [w_ref[k:k + 1, :] for k in range(3)]

        def step(i, carry, at_start, at_end):
            r0 = pl.multiple_of(i * R, R)
            h0 = pl.multiple_of(jnp.maximum(r0 - CONV_HALO, 0), CONV_HALO)
            ld = lambda p, start, rows: x_ref[p, pl.ds(start, rows), :].astype(F32)
            cu = jnp.concatenate([ld(1, h0, CONV_HALO) * ld(2, h0, CONV_HALO), ld(1, r0, R) * ld(2, r0, R)], axis=0)
            conv = w[0] * cu[CONV_HALO:]
            conv = conv + w[1] * _conv_shifted(cu, 1, r0, R, at_start)
            conv = conv + w[2] * _conv_shifted(cu, 2, r0, R, at_start)
            z_ref[pl.ds(r0, R), :] = (ld(0, r0, R) * conv).astype(BF16)
            return carry

        _conv_chunks(step, S // R, 0)

    return pl.pallas_call(
        body, name=name, grid=(B, D // tc),
        in_specs=[pl.BlockSpec((3, None, S, tc), lambda b, j: (0, b, 0, j)),
                  pl.BlockSpec((None, None, 3, tc), lambda b, j: (j // per_chunk, layer, 0, j % per_chunk))],
        out_specs=pl.BlockSpec((None, S, tc), lambda b, j: (b, 0, j)),
        out_shape=jax.ShapeDtypeStruct((B, S, D), BF16),
        compiler_params=_params(2))(bcu, cwg)


def _conv_bwd(name, bcu, dz, cwg, layer, tc):
    _, B, S, D = bcu.shape
    cwc = cwg.shape[3]
    per_chunk = cwc // tc
    R = min(CONV_ROWS, S)

    def body(x_ref, dz_ref, w_ref, d_ref, dw_ref):
        w = [w_ref[k:k + 1, :] for k in range(3)]

        @pl.when(pl.program_id(1) == 0)
        def _():
            dw_ref[...] = jnp.zeros_like(dw_ref)

        def step(i, carry, at_start, at_end):
            r0 = pl.multiple_of(i * R, R)
            h0 = pl.multiple_of(jnp.maximum(r0 - CONV_HALO, 0), CONV_HALO)
            a0 = pl.multiple_of(jnp.minimum(r0 + R, S - CONV_HALO), CONV_HALO)
            ld = lambda p, start, rows: x_ref[p, pl.ds(start, rows), :].astype(F32)
            b, c, u = ld(0, r0, R), ld(1, r0, R), ld(2, r0, R)
            dz = dz_ref[pl.ds(r0, R), :]
            cu = jnp.concatenate([ld(1, h0, CONV_HALO) * ld(2, h0, CONV_HALO), c * u], axis=0)
            cu1 = _conv_shifted(cu, 1, r0, R, at_start)
            cu2 = _conv_shifted(cu, 2, r0, R, at_start)
            conv = w[0] * (c * u) + w[1] * cu1 + w[2] * cu2
            dconv = dz * b
            dca = jnp.concatenate([dconv, dz_ref[pl.ds(a0, CONV_HALO), :] * ld(0, a0, CONV_HALO)], axis=0)
            dcu = (w[0] * dconv + w[1] * _conv_ahead(dca, 1, r0, R, S, at_end)
                   + w[2] * _conv_ahead(dca, 2, r0, R, S, at_end))
            d_ref[0, pl.ds(r0, R), :] = (dz * conv).astype(BF16)
            d_ref[1, pl.ds(r0, R), :] = (dcu * u).astype(BF16)
            d_ref[2, pl.ds(r0, R), :] = (dcu * c).astype(BF16)
            return (carry[0] + jnp.sum(dconv * (c * u), axis=0, keepdims=True),
                    carry[1] + jnp.sum(dconv * cu1, axis=0, keepdims=True),
                    carry[2] + jnp.sum(dconv * cu2, axis=0, keepdims=True))

        zero = jnp.zeros((1, tc), F32)
        s0, s1, s2 = _conv_chunks(step, S // R, (zero, zero, zero))
        for k, sk in enumerate((s0, s1, s2)):
            dw_ref[k:k + 1, :] += sk

    return pl.pallas_call(
        body, name=name, grid=(D // tc, B),
        in_specs=[pl.BlockSpec((3, None, S, tc), lambda j, b: (0, b, 0, j)),
                  pl.BlockSpec((None, S, tc), lambda j, b: (b, 0, j)),
                  pl.BlockSpec((None, None, 3, tc), lambda j, b: (j // per_chunk, layer, 0, j % per_chunk))],
        out_specs=[pl.BlockSpec((3, None, S, tc), lambda j, b: (0, b, 0, j)),
                   pl.BlockSpec((3, tc), lambda j, b: (0, j))],
        out_shape=[jax.ShapeDtypeStruct((3, B, S, D), BF16), jax.ShapeDtypeStruct((3, D), F32)],
        compiler_params=_params(2))(bcu, dz, cwg)


def _att_rows(dil, idx, nb):
    r, n = idx // nb, idx % nb
    if dil == 1:
        cur = pl.ds(pl.multiple_of(n * ATT_BLK, ATT_BLK), ATT_BLK)
        prev = pl.ds(pl.multiple_of(jnp.maximum(n - 1, 0) * ATT_BLK, ATT_BLK), ATT_BLK)
    else:
        cur = pl.ds(n * (ATT_BLK * dil) + r, ATT_BLK, stride=dil)
        prev = pl.ds(jnp.maximum(n - 1, 0) * (ATT_BLK * dil) + r, ATT_BLK, stride=dil)
    return n, cur, prev


def _att_bias(bias_ref, dil, sl_ref, hp):
    row = lax.broadcasted_iota(jnp.int32, (2 * ATT_BLK, 2 * ATT_BLK), 0)
    ci = lax.broadcasted_iota(jnp.int32, (2 * ATT_BLK, 2 * ATT_BLK), 1)
    j = ATT_BLK + (row & (ATT_BLK - 1)) - ci
    slope = jnp.where(row < ATT_BLK, sl_ref[2 * hp], sl_ref[2 * hp + 1])
    rest = jnp.where((j >= 0) & (j <= ATT_BLK), -slope * (dil * j).astype(F32), NEG_INF)
    bias_ref[1] = rest
    bias_ref[0] = jnp.where(ci >= ATT_BLK, rest, NEG_INF)


def _stack_heads(x16, lane):
    first = lane < HEAD_DIM
    return jnp.concatenate([jnp.where(first, x16, jnp.zeros_like(x16)),
                            jnp.where(first, jnp.zeros_like(x16), x16)], axis=0)


def _per_head(col, lane):
    return jnp.where(lane < HEAD_DIM, col[:ATT_BLK], col[ATT_BLK:])


def _attn_fwd(name, q, kv, slopes, n_heads):
    B, S, CQ = q.shape
    HP = n_heads * HEAD_DIM // LANES
    scale = HEAD_DIM ** -0.5
    n_groups = len(PATTERNS)
    CH = 256

    def body(sl_ref, q_ref, k_ref, v_ref, o_ref, lse_ref, bias_ref, *parts):
        og, lg = parts[:n_groups], parts[n_groups:]
        hp, g = pl.program_id(1), pl.program_id(2)
        lane = lax.broadcasted_iota(jnp.int32, (1, LANES), 1)

        for gi, (window, dil) in enumerate(PATTERNS):
            nb = S // dil // ATT_BLK

            @pl.when(g == gi)
            def _(gi=gi, dil=dil, nb=nb):
                _att_bias(bias_ref, dil, sl_ref, hp)

                def step(idx, carry):
                    n, cur, prev = _att_rows(dil, idx, nb)
                    qs = _stack_heads((q_ref[cur, :] * scale).astype(BF16), lane)
                    kc = jnp.concatenate([k_ref[prev, :], k_ref[cur, :]], axis=0).astype(BF16)
                    vc = jnp.concatenate([v_ref[prev, :], v_ref[cur, :]], axis=0).astype(BF16)
                    s = _dot_nt(qs, kc) + bias_ref[jnp.minimum(n, 1)]
                    m = jnp.max(s, axis=-1, keepdims=True)
                    p = jnp.exp(s - m)
                    l = jnp.sum(p, axis=-1, keepdims=True)
                    p16 = p.astype(BF16)
                    o_un = _dot(jnp.concatenate([p16[:ATT_BLK], p16[ATT_BLK:]], axis=1), _stack_heads_rows(vc, lane))
                    og[gi][cur, :] = o_un / _per_head(l, lane)
                    lg[gi][cur, :] = _per_head(m + jnp.log(l), lane)
                    return carry

                lax.fori_loop(0, S // ATT_BLK, step, 0, unroll=16)

        @pl.when(g == n_groups - 1)
        def _():
            def comb(i, carry):
                rows = pl.ds(pl.multiple_of(i * CH, CH), CH)
                a, b, c = lg[0][rows, :], lg[1][rows, :], lg[2][rows, :]
                m = jnp.maximum(jnp.maximum(a, b), c)
                ea, eb, ec = jnp.exp(a - m), jnp.exp(b - m), jnp.exp(c - m)
                z = ea + eb + ec
                o_ref[rows, :] = (ea / z) * og[0][rows, :] + (eb / z) * og[1][rows, :] + (ec / z) * og[2][rows, :]
                lse_ref[rows, :] = m + jnp.log(z)
                return carry

            lax.fori_loop(0, S // CH, comb, 0)

    blk = (None, S, LANES)
    out = pl.BlockSpec(blk, lambda b, hp, g: (b, 0, hp))
    return pl.pallas_call(
        body, name=name, grid=(B, HP, n_groups),
        in_specs=[pl.BlockSpec(memory_space=pltpu.SMEM),
                  pl.BlockSpec(blk, lambda b, hp, g: (b, 0, g * HP + hp)),
                  pl.BlockSpec(blk, lambda b, hp, g: (b, 0, g * 2 * HP + hp)),
                  pl.BlockSpec(blk, lambda b, hp, g: (b, 0, g * 2 * HP + HP + hp))],
        out_specs=[out, out],
        out_shape=[jax.ShapeDtypeStruct((B, S, HP * LANES), F32)] * 2,
        scratch_shapes=[pltpu.VMEM((2, 2 * ATT_BLK, 2 * ATT_BLK), F32)] + [pltpu.VMEM((S, LANES), F32)] * (2 * n_groups),
        compiler_params=_params(3))(slopes, q, kv, kv)


def _stack_heads_rows(x16, lane):
    first = lane < HEAD_DIM
    return jnp.concatenate([jnp.where(first, x16, jnp.zeros_like(x16)),
                            jnp.where(first, jnp.zeros_like(x16), x16)], axis=0)


def _attn_bwd(name, q, kv, slopes, o, lse, do, n_heads, dkv_prev):
    B, S, CQ = q.shape
    HP = n_heads * HEAD_DIM // LANES
    scale = HEAD_DIM ** -0.5
    n_groups = len(PATTERNS)
    n_prev = 0 if dkv_prev is None else 2

    def body(sl_ref, q_ref, k_ref, v_ref, o_ref, lse_ref, do_ref, *rest):
        dq_ref, dk_ref, dv_ref, bias_ref = rest[n_prev:]
        hp, g = pl.program_id(1), pl.program_id(2)
        lane = lax.broadcasted_iota(jnp.int32, (1, LANES), 1)
        first = lane < HEAD_DIM

        def flush(rows, dk, dv):
            if n_prev:
                dk = dk + rest[0][rows, :]
                dv = dv + rest[1][rows, :]
            dk_ref[rows, :] = dk
            dv_ref[rows, :] = dv

        for gi, (window, dil) in enumerate(PATTERNS):
            nb = S // dil // ATT_BLK
            n_blocks = S // ATT_BLK

            @pl.when(g == gi)
            def _(dil=dil, nb=nb, n_blocks=n_blocks):
                _att_bias(bias_ref, dil, sl_ref, hp)

                def block(idx, carry, first_of_all):
                    n, cur, prev = _att_rows(dil, idx, nb)
                    qs = _stack_heads((q_ref[cur, :] * scale).astype(BF16), lane)
                    kc = jnp.concatenate([k_ref[prev, :], k_ref[cur, :]], axis=0).astype(BF16)
                    vc = jnp.concatenate([v_ref[prev, :], v_ref[cur, :]], axis=0).astype(BF16)
                    dob = do_ref[cur, :]
                    prod = dob * o_ref[cur, :]
                    lseb = lse_ref[cur, :]
                    dos = _stack_heads(dob.astype(BF16), lane)
                    delta = jnp.concatenate(
                        [jnp.sum(jnp.where(first, prod, 0.0), axis=-1, keepdims=True),
                         jnp.sum(jnp.where(first, 0.0, prod), axis=-1, keepdims=True)], axis=0)
                    lse_col = jnp.concatenate(
                        [jnp.max(jnp.where(first, lseb, -jnp.inf), axis=-1, keepdims=True),
                         jnp.max(jnp.where(first, -jnp.inf, lseb), axis=-1, keepdims=True)], axis=0)
                    s = _dot_nt(qs, kc) + bias_ref[jnp.minimum(n, 1)]
                    p = jnp.exp(s - lse_col)
                    ds = p * (_dot_nt(dos, vc) - delta)
                    ds16 = ds.astype(BF16)
                    dq = _dot(jnp.concatenate([ds16[:ATT_BLK], ds16[ATT_BLK:]], axis=1), _stack_heads_rows(kc, lane))
                    dq_ref[cur, :] = dq * scale
                    dk = _dot_tn(ds16, qs)
                    dv = _dot_tn(p.astype(BF16), dos)

                    def flush_before():
                        _, before, _ = _att_rows(dil, idx - 1, nb)
                        flush(before, carry[0] + dk[:ATT_BLK], carry[1] + dv[:ATT_BLK])

                    if first_of_all:
                        pl.when(idx > 0)(flush_before)
                    else:
                        flush_before()
                    return dk[ATT_BLK:], dv[ATT_BLK:]

                def step(i, carry):
                    for u in range(BWD_UNROLL):
                        carry = block(i * BWD_UNROLL + u, carry, u == 0)
                    return carry

                zero = jnp.zeros((ATT_BLK, LANES), F32)
                dk_last, dv_last = lax.fori_loop(0, n_blocks // BWD_UNROLL, step, (zero, zero))
                _, last, _ = _att_rows(dil, n_blocks - 1, nb)
                flush(last, dk_last, dv_last)

    blk = (None, S, LANES)
    shared = pl.BlockSpec(blk, lambda b, hp, g: (b, 0, hp))
    grouped = pl.BlockSpec(blk, lambda b, hp, g: (b, 0, g * HP + hp))
    prev = [] if dkv_prev is None else list(dkv_prev)
    gshape = jax.ShapeDtypeStruct((B, S, n_groups * HP * LANES), F32)
    return pl.pallas_call(
        body, name=name, grid=(B, HP, n_groups),
        in_specs=[pl.BlockSpec(memory_space=pltpu.SMEM), grouped,
                  pl.BlockSpec(blk, lambda b, hp, g: (b, 0, g * 2 * HP + hp)),
                  pl.BlockSpec(blk, lambda b, hp, g: (b, 0, g * 2 * HP + HP + hp)),
                  shared, shared, shared] + [grouped] * n_prev,
        out_specs=[grouped] * 3, out_shape=[gshape] * 3,
        scratch_shapes=[pltpu.VMEM((2, 2 * ATT_BLK, 2 * ATT_BLK), F32)],
        compiler_params=_params(3))(slopes, q, kv, kv, o, lse, do, *prev)


def _final_loss(name, h, g, target, tm):
    T, D = h.shape

    def body(h_ref, g_ref, t_ref, loss_ref, dh_ref, dh16_ref, dg_ref):
        hf = h_ref[...]
        gv = g_ref[...]
        rstd = lax.rsqrt(jnp.mean(hf * hf, axis=-1, keepdims=True) + EPS)
        xhat = hf * rstd
        err = xhat * gv - t_ref[...]
        part = 0.5 * jnp.sum(jnp.mean(err * err, axis=-1, keepdims=True), axis=0, keepdims=True)
        dy = err * (1.0 / D)
        dg = jnp.sum(dy * xhat, axis=0, keepdims=True)
        dx = dy * gv
        dh = rstd * (dx - xhat * jnp.mean(dx * xhat, axis=-1, keepdims=True))
        dh_ref[...] = dh
        dh16_ref[...] = dh.astype(BF16)

        @pl.when(pl.program_id(0) == 0)
        def _():
            loss_ref[...] = part
            dg_ref[...] = dg

        @pl.when(pl.program_id(0) > 0)
        def _():
            loss_ref[...] += part
            dg_ref[...] += dg

    return pl.pallas_call(
        body, name=name, grid=(T // tm,),
        in_specs=[pl.BlockSpec((tm, D), lambda i: (i, 0)), pl.BlockSpec((1, D), lambda i: (0, 0)),
                  pl.BlockSpec((tm, D), lambda i: (i, 0))],
        out_specs=[pl.BlockSpec((1, 1), lambda i: (0, 0)), pl.BlockSpec((tm, D), lambda i: (i, 0)),
                   pl.BlockSpec((tm, D), lambda i: (i, 0)), pl.BlockSpec((1, D), lambda i: (0, 0))],
        out_shape=[jax.ShapeDtypeStruct((1, 1), F32), jax.ShapeDtypeStruct((T, D), F32),
                   jax.ShapeDtypeStruct((T, D), BF16), jax.ShapeDtypeStruct((1, D), F32)],
        compiler_params=_params(1))(h, g, target)


def _nt_rows(name, dh, wg, layer, a_mul, out_dtype, tm, deps=()):
    T, D = dh.shape
    rk = wg.shape[2]
    N = N_CHIPS * rk
    with_a = a_mul is not None

    def body(dh_ref, w_ref, *rest):
        o_ref = rest[-1]
        d16 = dh_ref[...]
        for ch in range(N_CHIPS):
            r = _dot_nt(d16, w_ref[ch])
            if with_a:
                r = r * (2.0 * jnp.maximum(rest[0][:, ch * rk:(ch + 1) * rk].astype(F32), 0.0))
            o_ref[:, ch * rk:(ch + 1) * rk] = r.astype(out_dtype)

    in_specs = [pl.BlockSpec((tm, D), lambda i: (i, 0)),
                pl.BlockSpec((N_CHIPS, None, rk, D), lambda i: (0, layer, 0, 0))]
    args = [dh, wg]
    if with_a:
        in_specs.append(pl.BlockSpec((tm, N), lambda i: (i, 0)))
        args.append(a_mul)
    in_specs += [ANY] * len(deps)
    args += list(deps)
    return pl.pallas_call(
        body, name=name, grid=(T // tm,), in_specs=in_specs,
        out_specs=pl.BlockSpec((tm, N), lambda i: (i, 0)),
        out_shape=jax.ShapeDtypeStruct((T, N), out_dtype),
        compiler_params=_params(1))(*args)


def _nt_cols(name, ysegs, wg, layer, tm, norm, deps=()):
    Nw, cw = wg.shape[2], wg.shape[3]
    widths = [bs[-1] for _, bs, _ in ysegs]
    pieces = _pieces(widths, cw, 1024)
    ns = len(ysegs)
    T = norm[0].shape[0] if norm is not None else ysegs[0][0].shape[-2]

    def body(*refs):
        y_refs = refs[:ns]
        w_ref = refs[ns]
        acc = refs[-1]
        for n, (s, a0, ch, b0, wd) in enumerate(pieces):
            d = _dot_nt(y_refs[s][:, a0:a0 + wd].astype(BF16), w_ref[ch, :, b0:b0 + wd])
            if n == 0:
                acc[...] = d
            else:
                acc[...] += d
        if norm is None:
            refs[ns + 1 + len(deps)][...] = acc[...]
        else:
            h_ref, g_ref, dhin_ref = refs[ns + 1:ns + 4]
            out_ref, out16_ref, dg_ref = refs[ns + 4 + len(deps):ns + 7 + len(deps)]
            dh_c, dg = _rms_bwd(h_ref[...], g_ref[...], acc[...])
            dh = dhin_ref[...] + dh_c
            out_ref[...] = dh
            out16_ref[...] = dh.astype(BF16)

            @pl.when(pl.program_id(0) == 0)
            def _():
                dg_ref[...] = dg

            @pl.when(pl.program_id(0) > 0)
            def _():
                dg_ref[...] += dg

    in_specs = [pl.BlockSpec(bs, im) for _, bs, im in ysegs]
    in_specs.append(pl.BlockSpec((N_CHIPS, None, Nw, cw), lambda i: (0, layer, 0, 0)))
    args = [a for a, _, _ in ysegs] + [wg]
    row = pl.BlockSpec((tm, Nw), lambda i: (i, 0))
    vec = pl.BlockSpec((1, Nw), lambda i: (0, 0))
    if norm is None:
        out_specs = row
        out_shape = jax.ShapeDtypeStruct((T, Nw), F32)
    else:
        in_specs += [row, vec, row]
        args += list(norm)
    in_specs += [ANY] * len(deps)
    args += list(deps)
    if norm is not None:
        out_specs = [row, row, vec]
        out_shape = [jax.ShapeDtypeStruct((T, Nw), F32), jax.ShapeDtypeStruct((T, Nw), BF16),
                     jax.ShapeDtypeStruct((1, Nw), F32)]
    return pl.pallas_call(
        body, name=name, grid=(T // tm,), in_specs=in_specs, out_specs=out_specs, out_shape=out_shape,
        scratch_shapes=[pltpu.VMEM((tm, Nw), F32)], compiler_params=_params(1))(*args)


def _tn(name, x, x_act, ysegs, cw, cols_layout, tmm, tt, deps=(), out_dtype=F32):
    T, M = x.shape
    widths = [bs[-1] for _, bs, _ in ysegs]
    N = sum(widths)
    pieces = _pieces(widths, cw if cols_layout else N, 1024)
    ns = len(ysegs)
    n_t = T // tt
    block = (N_CHIPS, tmm, cw) if cols_layout else (tmm, N)
    narrow = out_dtype != F32

    def body(x_ref, *refs):
        y_refs = refs[:ns]
        o_ref = refs[ns + len(deps)]
        acc = refs[-1] if narrow else o_ref

        @pl.when(pl.program_id(1) == 0)
        def _():
            acc[...] = jnp.zeros_like(acc)

        xt = x_act(x_ref[...])
        for s, a0, ch, b0, wd in pieces:
            d = _dot_tn(xt, y_refs[s][:, a0:a0 + wd].astype(BF16))
            if cols_layout:
                acc[ch, :, b0:b0 + wd] += d
            else:
                acc[:, b0:b0 + wd] += d
        if narrow:
            @pl.when(pl.program_id(1) == n_t - 1)
            def _():
                o_ref[...] = acc[...].astype(out_dtype)

    in_specs = [pl.BlockSpec((tt, tmm), lambda m, t: (t, m))] + [pl.BlockSpec(bs, im) for _, bs, im in ysegs]
    in_specs += [ANY] * len(deps)
    if cols_layout:
        out_specs = pl.BlockSpec(block, lambda m, t: (0, m, 0))
        out_shape = jax.ShapeDtypeStruct((N_CHIPS, M, cw), out_dtype)
    else:
        out_specs = pl.BlockSpec(block, lambda m, t: (m, 0))
        out_shape = jax.ShapeDtypeStruct((M, N), out_dtype)
    return pl.pallas_call(
        body, name=name, grid=(M // tmm, n_t), in_specs=in_specs, out_specs=out_specs, out_shape=out_shape,
        scratch_shapes=[pltpu.VMEM(block, F32)] if narrow else [],
        compiler_params=_params(2))(x, *[a for a, _, _ in ysegs], *deps)


def _seg2d(a, t_rows, grid_rank):
    w = a.shape[1]
    if grid_rank == 1:
        return (a, (t_rows, w), lambda i: (i, 0))
    return (a, (t_rows, w), lambda m, t: (t, 0))


def _kv_segments(dk, dv, C, t_rows, grid_rank):
    segs = []
    for g in range(len(PATTERNS)):
        for a in (dk, dv):
            if grid_rank == 1:
                segs.append((a, (t_rows, C), lambda i, g=g: (i, g)))
            else:
                segs.append((a, (t_rows, C), lambda m, t, g=g: (t, g)))
    return segs


def _seg_plane(a, plane, t_rows, grid_rank):
    w = a.shape[2]
    if grid_rank == 1:
        return (a, (None, t_rows, w), lambda i: (plane, i, 0))
    return (a, (None, t_rows, w), lambda m, t: (plane, t, 0))


def _row_tile(rows, row_bytes, budget_bytes=2 * 1024 * 1024):
    t = rows
    while t * row_bytes > budget_bytes and t % 32 == 0:
        t //= 2
    return t


N_DEVICES = 8


def _device_add(name, own, slots, place):
    _, _, hr, c = own.shape
    tr = _row_tile(hr, c * 4, 1024 * 1024)

    def body(place_ref, own_ref, *refs):
        o_ref = refs[-1]
        acc = own_ref[...].astype(F32)
        for r in refs[:-1]:
            acc = acc + r[...].astype(F32)
        o_ref[...] = acc

    def slot(k):
        return pl.BlockSpec((None, tr, c), lambda i, pr: ((2 * pr[0] + pr[1] + k) % N_DEVICES, i, 0))

    grid_spec = pltpu.PrefetchScalarGridSpec(
        num_scalar_prefetch=1, grid=(hr // tr,),
        in_specs=[pl.BlockSpec((None, None, tr, c), lambda i, pr: (pr[0], pr[1], i, 0))]
        + [slot(k) for k in range(1, N_DEVICES)],
        out_specs=pl.BlockSpec((None, tr, c), lambda i, pr: (pr[1], i, 0)))
    return pl.pallas_call(body, name=name, grid_spec=grid_spec,
                          out_shape=jax.ShapeDtypeStruct((2, hr, c), F32),
                          compiler_params=_params(1))(place, own, *[slots] * (N_DEVICES - 1))


def _adamw(name, w, g, m, v):
    rows, cols = w.shape
    tr = _row_tile(rows, cols * 4, 1024 * 1024)

    def body(w_ref, g_ref, m_ref, v_ref, d_ref, nm_ref, nv_ref):
        d_ref[...], nm_ref[...], nv_ref[...] = _adamw_math(w_ref[...], g_ref[...], m_ref[...], v_ref[...])

    spec = pl.BlockSpec((tr, cols), lambda i: (i, 0))
    return pl.pallas_call(
        body, name=name, grid=(rows // tr,), in_specs=[spec] * 4, out_specs=[spec] * 3,
        out_shape=[jax.ShapeDtypeStruct((rows, cols), F32)] * 3, compiler_params=_params(1))(w, g, m, v)


def _adamw_math(w, g, m, v):
    nm = ADAM_B1 * m + (1.0 - ADAM_B1) * g
    nv = ADAM_B2 * v + (1.0 - ADAM_B2) * jnp.square(g)
    m_hat = nm / (1.0 - ADAM_B1 ** ADAM_STEP)
    v_hat = nv / (1.0 - ADAM_B2 ** ADAM_STEP)
    return -ADAM_LR * (m_hat / (jnp.sqrt(v_hat) + ADAM_EPS) + ADAM_WD * w), nm, nv


def _adamw_layers(name, w, grads, m, v):
    L, r, c = w.shape
    tr = _row_tile(r, L * c * 4, 1024 * 1024)

    def body(*refs):
        w_ref, m_ref, v_ref = refs[:3]
        g_refs = refs[3:3 + L]
        go_ref, d_ref, nm_ref, nv_ref = refs[3 + L:]
        for l in range(L):
            g = g_refs[l][...]
            go_ref[l] = g
            d_ref[l], nm_ref[l], nv_ref[l] = _adamw_math(w_ref[l], g, m_ref[l], v_ref[l])

    stacked = pl.BlockSpec((L, tr, c), lambda i: (0, i, 0))
    return pl.pallas_call(
        body, name=name, grid=(r // tr,),
        in_specs=[stacked] * 3 + [pl.BlockSpec((tr, c), lambda i: (i, 0))] * L, out_specs=[stacked] * 4,
        out_shape=[jax.ShapeDtypeStruct((L, r, c), F32)] * 4, compiler_params=_params(1))(w, m, v, *grads)


def _place():
    x, y, c = lax.axis_index("x"), lax.axis_index("y"), lax.axis_index("c")
    chips = [(1 - x, y), (x, 1 - y), (1 - x, 1 - y)]
    return x, y, c, chips


HBM = pl.BlockSpec(memory_space=pltpu.HBM)
SEM = pl.BlockSpec(memory_space=pltpu.SEMAPHORE)
EFFECT = pltpu.SideEffectType.DATAFLOW_SIDE_EFFECTING


class _Copy:
    def __init__(self, src, src_view, land, dst_view, recv_view, target):
        self.src, self.src_view, self.land, self.dst_view, self.recv_view, self.target = (
            src, src_view, land, dst_view, recv_view, target)


def _whole(ref, place):
    return ref


def _split_start(name, srcs, land_shapes, plans):
    skeys, lkeys = list(srcs), list(land_shapes)
    ns, nl, ng = len(skeys), len(lkeys), len(plans)

    def body(*refs):
        src = dict(zip(skeys, refs[:ns]))
        land = dict(zip(lkeys, refs[ns:ns + nl]))
        sems = refs[ns + nl:ns + nl + 2 * ng]
        token = refs[-1]
        place = _place()
        for gi, plan in enumerate(plans):
            for k, cp in enumerate(plan):
                dst = land[cp.land] if cp.land in land else src[cp.land]
                pltpu.make_async_remote_copy(
                    src_ref=cp.src_view(src[cp.src], place), dst_ref=cp.dst_view(dst, place),
                    send_sem=sems[2 * gi].at[k], recv_sem=sems[2 * gi + 1].at[k],
                    device_id=cp.target(place), device_id_type=MESH).start()
        token[...] = jnp.zeros_like(token)

    sem_shapes = []
    for plan in plans:
        sem_shapes += [pltpu.SemaphoreType.DMA((len(plan),))] * 2
    buffers = [srcs[k] for k in skeys] + [lax.empty(land_shapes[k].shape, land_shapes[k].dtype) for k in lkeys]
    outs = pl.pallas_call(
        body, name=name,
        out_shape=(*sem_shapes, *[pltpu.HBM(a.shape, a.dtype) for a in buffers], jax.ShapeDtypeStruct((8, LANES), F32)),
        in_specs=[HBM] * (ns + nl),
        out_specs=(*[SEM] * (2 * ng), *[HBM] * (ns + nl), pl.BlockSpec(memory_space=pltpu.VMEM)),
        input_output_aliases={i: 2 * ng + i for i in range(ns + nl)},
        compiler_params=pltpu.CompilerParams(has_side_effects=EFFECT),
    )(*[pltpu.with_memory_space_constraint(a, pltpu.HBM) for a in buffers])
    sems = [(outs[2 * gi], outs[2 * gi + 1]) for gi in range(ng)]
    thru = outs[2 * ng:2 * ng + ns + nl]
    return sems, dict(zip(skeys, thru[:ns])), dict(zip(lkeys, thru[ns:])), outs[-1]


def _split_wait(name, sems, srcs, lands, plan, after):
    skeys, lkeys = list(srcs), list(lands)
    ns, nl = len(skeys), len(lkeys)

    def body(*refs):
        src = dict(zip(skeys, refs[:ns]))
        land = dict(zip(lkeys, refs[ns:ns + nl]))
        ssem, rsem = refs[ns + nl], refs[ns + nl + 1]
        place = _place()
        for k, cp in enumerate(plan):
            dst = land[cp.land] if cp.land in land else src[cp.land]
            pltpu.make_async_remote_copy(
                src_ref=cp.src_view(src[cp.src], place), dst_ref=cp.dst_view(dst, place),
                send_sem=ssem.at[k], recv_sem=rsem.at[k],
                device_id=cp.target(place), device_id_type=MESH).wait_send()
            got = cp.recv_view(dst, place)
            pltpu.make_async_remote_copy(
                src_ref=got, dst_ref=got, send_sem=ssem.at[k], recv_sem=rsem.at[k],
                device_id=cp.target(place), device_id_type=MESH).wait_recv()

    buffers = [srcs[k] for k in skeys] + [lands[k] for k in lkeys]
    outs = pl.pallas_call(
        body, name=name, out_shape=tuple(pltpu.HBM(a.shape, a.dtype) for a in buffers),
        in_specs=(*[HBM] * (ns + nl), SEM, SEM, ANY), out_specs=tuple([HBM] * (ns + nl)),
        input_output_aliases={i: i for i in range(ns + nl)},
        compiler_params=pltpu.CompilerParams(has_side_effects=EFFECT),
    )(*buffers, sems[0], sems[1], after)
    return dict(zip(skeys, outs[:ns])), dict(zip(lkeys, outs[ns:]))


def _chip_of(place):
    x, y, c, chips = place
    return 2 * x + y


GATHER_FIRST = 2


class _WeightGather:
    def __init__(self, blocks):
        self.plans, shapes = {}, {}
        for key, a in blocks.items():
            shapes[key] = jax.ShapeDtypeStruct((N_CHIPS,) + a.shape, a.dtype)
            slot = lambda ref, place: ref.at[_chip_of(place)]
            plan = [_Copy(key, _whole, key, slot,
                          lambda ref, place, k=k: ref.at[2 * place[3][k][0] + place[3][k][1]],
                          lambda place, k=k: (place[3][k][0], place[3][k][1], place[2])) for k in range(3)]
            plan.append(_Copy(key, _whole, key, slot, slot, lambda place: (place[0], place[1], 1 - place[2])))
            self.plans[key] = plan
        keys = list(blocks)
        self.sems, self.srcs, self.lands = {}, {}, {}
        for name, part in (("gather_start_first", keys[:GATHER_FIRST]), ("gather_start", keys[GATHER_FIRST:])):
            sems, srcs, lands, self.token = _split_start(name, {k: blocks[k] for k in part}, {k: shapes[k] for k in part},
                                                         [self.plans[k] for k in part])
            self.sems.update(zip(part, sems))
            self.srcs.update(srcs)
            self.lands.update(lands)

    def get(self, l, name, after):
        key = (l, name)
        _, lands = _split_wait(f"gather_wait_{name}{l}", self.sems[key], {key: self.srcs[key]},
                               {key: self.lands[key]}, self.plans[key], after)
        return lands[key][:, None]


class _GradReduce:
    def __init__(self, place):
        self.place = place
        self.jobs = []
        self.done = {}
        self.n = 0

    def submit(self, grads):
        views = {k: a.reshape(N_CHIPS, 2, a.shape[1] // 2, a.shape[2]) for k, a in grads.items()}
        shapes = {k: jax.ShapeDtypeStruct((N_DEVICES,) + a.shape[2:], a.dtype) for k, a in views.items()}

        def peer(place, k):
            x, y, c, _ = place
            return (1 - x if k & 4 else x, 1 - y if k & 2 else y, 1 - c if k & 1 else c)

        def index(dev):
            return 4 * dev[0] + 2 * dev[1] + dev[2]

        plan = []
        for key in views:
            for k in range(1, N_DEVICES):
                plan.append(_Copy(
                    key, lambda ref, place, k=k: ref.at[2 * peer(place, k)[0] + peer(place, k)[1], peer(place, k)[2]],
                    key, lambda ref, place: ref.at[index(place[:3])],
                    lambda ref, place, k=k: ref.at[index(peer(place, k))],
                    lambda place, k=k: peer(place, k)))
        sems, srcs, lands, token = _split_start(f"grad_start{self.n}", views, shapes, [plan])
        self.jobs.append(dict(id=self.n, sems=sems[0], srcs=srcs, lands=lands, plan=plan))
        self.n += 1
        return token

    def pump(self, after):
        return []

    def finish(self, after):
        for job in self.jobs:
            srcs, lands = _split_wait(f"grad_wait{job['id']}", job["sems"], job["srcs"], job["lands"], job["plan"],
                                      after)
            for i, k in enumerate(srcs):
                self.done[k] = _device_add(f"grad_add{job['id']}_{i}", srcs[k], lands[k], self.place)
        self.jobs = []
        return self.done


class _PairShare:
    def __init__(self, halves, types):
        sibling = lambda place: (place[0], place[1], 1 - place[2])
        mine = lambda ref, place: ref.at[place[2]]
        theirs = lambda ref, place: ref.at[1 - place[2]]
        self.plans = {t: [_Copy(k, mine, k, mine, theirs, sibling) for k in halves if k[0] == t] for t in types}
        sems, self.bufs, _, self.token = _split_start("share_start", halves, {}, list(self.plans.values()))
        self.sems = dict(zip(self.plans, sems))

    def get(self, t, after):
        keys = [cp.src for cp in self.plans[t]]
        bufs, _ = _split_wait(f"share_wait_{t}", self.sems[t], {k: self.bufs[k] for k in keys}, {}, self.plans[t], after)
        return bufs


def _small_allreduce(part):
    R, C = part.shape
    N_DEV = 8

    def body(in_ref, out_ref, slots, ssem, rsem):
        x, y, c, _ = _place()
        me = 4 * x + 2 * y + c
        sends = []
        for k in range(1, N_DEV):
            kx, ky, kc = (k >> 2) & 1, (k >> 1) & 1, k & 1
            peer = (1 - x if kx else x, 1 - y if ky else y, 1 - c if kc else c)
            cp = pltpu.make_async_remote_copy(
                src_ref=in_ref, dst_ref=slots.at[me], send_sem=ssem.at[k], recv_sem=rsem.at[k],
                device_id=peer, device_id_type=MESH)
            cp.start()
            sends.append(cp)
        slots[me] = in_ref[...]
        for k in range(1, N_DEV):
            kx, ky, kc = (k >> 2) & 1, (k >> 1) & 1, k & 1
            peer = (1 - x if kx else x, 1 - y if ky else y, 1 - c if kc else c)
            slot = slots.at[4 * peer[0] + 2 * peer[1] + peer[2]]
            pltpu.make_async_remote_copy(
                src_ref=slot, dst_ref=slot, send_sem=ssem.at[k], recv_sem=rsem.at[k],
                device_id=peer, device_id_type=MESH).wait_recv()
        acc = slots[0]
        for d in range(1, N_DEV):
            acc = acc + slots[d]
        out_ref[...] = acc
        for cp in sends:
            cp.wait_send()

    vm = pl.BlockSpec(memory_space=pltpu.VMEM)
    return pl.pallas_call(
        body, name="small_allreduce", in_specs=[vm], out_specs=vm,
        out_shape=jax.ShapeDtypeStruct((R, C), F32),
        scratch_shapes=[pltpu.VMEM((N_DEV, R, C), F32), pltpu.SemaphoreType.DMA((N_DEV,)),
                        pltpu.SemaphoreType.DMA((N_DEV,))])(part)


def _local_step(x, target, norm_mix, norm_mlp, norm_kv, norm_final, weights, sink, n_a, n_heads):
    B, S, D = x.shape
    T = B * S
    C = n_heads * HEAD_DIM
    depth = norm_mix.shape[0]
    slopes = 2.0 ** (-ALIBI_MAX_BIAS * jnp.arange(1, n_heads + 1, dtype=F32) / n_heads)
    tm = min(512, T)
    row = lambda v: v.reshape(1, -1)

    h = x.reshape(T, D)
    saved, Wl = [], []
    kv = nkv = h_kv = cwg = None
    for l in range(depth):
        s = {"h_in": h}
        w = {}
        Wl.append(w)
        if l < n_a:
            w["w_a_in"] = weights.get(l, "w_a_in", h)
            first = [weights.token] if l == 0 else []
            s["n1"], bcu = _norm_mm(f"a_in_fwd{l}", h, row(norm_mix[l]), w["w_a_in"], 0, 3, BF16, tm, first)
            s["bcu"] = bcu.reshape(3, B, S, D)
            if l == 0:
                cwg = weights.get(0, "conv", bcu)[:, 0, :n_a * 3].reshape(N_CHIPS, n_a, 3, -1)
            s["z"] = _conv_fwd(f"conv_fwd{l}", s["bcu"], cwg, l, CONV_COLS).reshape(T, D)
            w["w_a_out"] = weights.get(l, "w_a_out", s["z"])
            h = _mm_res_rows(f"a_out_fwd{l}", s["z"], w["w_a_out"], 0, h, _to_bf16, tm)
        else:
            i = l - n_a
            if i == 0:
                h_kv = h
                w["w_kv"] = weights.get(l, "w_kv", h)
                nkv, kv = _norm_mm("kv_fwd", h, row(norm_kv), w["w_kv"], 0, 1, F32, tm)
                kv = kv.reshape(B, S, 2 * 3 * C)
            w["w_q"] = weights.get(l, "w_q", h)
            s["n1"], q = _norm_mm(f"q_fwd{i}", h, row(norm_mix[l]), w["w_q"], 0, 1, F32, tm)
            s["q"] = q.reshape(B, S, 3 * C)
            o, lse = _attn_fwd(f"attn_fwd{i}", s["q"], kv, slopes, n_heads)
            s["o"], s["lse"] = o.reshape(T, C), lse.reshape(T, C)
            w["w_o"] = weights.get(l, "w_o", o)
            h = _mm_res_cols(f"o_fwd{i}", s["o"], w["w_o"], 0, h, tm)
        s["h_mid"] = h
        w["w_up"] = weights.get(l, "w_up", h)
        if l < n_a:
            s["n2"], a = _norm_mm(f"up_fwd{l}", h, row(norm_mlp[l]), w["w_up"], 0, 1, BF16, tm)
            s["a"] = a[0]
            w["w_down"] = weights.get(l, "w_down", a)
            h = _mm_res_rows(f"down_fwd{l}", s["a"], w["w_down"], 0, h, _relu2_bf16, tm)
        else:
            w["w_down"] = weights.get(l, "w_down", h)
            s["n2"], s["a"], h = _mlp_fwd(f"mlp_fwd{l}", h, row(norm_mlp[l]), w["w_up"], w["w_down"], tm)
        F = s["a"].shape[1]
        saved.append(s)

    loss, dh, dh16, dg_final = _final_loss("loss_head", h, row(norm_final), target.reshape(T, D), tm)

    g_mix, g_mlp = [None] * depth, [None] * depth
    g_conv = [None] * n_a
    dkv = None
    tt = min(512, T)
    deps = []
    for l in reversed(range(depth)):
        s, w = saved[l], Wl[l]
        g_down = _tn(f"down_wgrad{l}", s["a"], _relu2_bf16, [_seg2d(dh16, tt, 2)], None, False,
                     min(2048, F), tt, deps, BF16).reshape(N_CHIPS, F // N_CHIPS, D)
        da, dh, dh16, g_mlp[l] = _mlp_bwd(f"mlp_bwd{l}", dh, dh16, s["a"], w["w_down"], w["w_up"], s["h_mid"],
                                          row(norm_mlp[l]), tm)
        g_up = _tn(f"up_wgrad{l}", s["n2"], _to_bf16, [_seg2d(da, tt, 2)], F // N_CHIPS, True, D, tt, (), BF16)
        deps = sink.pump(dh) + [sink.submit({("w_up", l): g_up, ("w_down", l): g_down})]
        if l < n_a:
            g_out = _tn(f"a_out_wgrad{l}", s["z"], _to_bf16, [_seg2d(dh16, tt, 2)], None, False,
                        D, tt, deps, BF16).reshape(N_CHIPS, D // N_CHIPS, D)
            dz = _nt_rows(f"a_out_bwd{l}", dh16, w["w_a_out"], 0, None, F32, tm)
            deps = sink.pump(dz) + [sink.submit({("w_a_out", l): g_out})]
            dbcu, g_conv[l] = _conv_bwd(f"conv_bwd{l}", s["bcu"], dz.reshape(B, S, D), cwg, l, CONV_COLS)
            dbcu = dbcu.reshape(3, T, D)
            g_in = _tn(f"a_in_wgrad{l}", s["n1"], _to_bf16, [_seg_plane(dbcu, p, tt, 2) for p in range(3)],
                       3 * D // N_CHIPS, True, D, tt, deps, BF16)
            deps = [sink.submit({("w_a_in", l): g_in})]
            dh, dh16, g_mix[l] = _nt_cols(f"a_in_bwd{l}", [_seg_plane(dbcu, p, tm, 1) for p in range(3)],
                                          w["w_a_in"], 0, tm, (s["h_in"], row(norm_mix[l]), dh), deps)
        else:
            i = l - n_a
            g_o = _tn(f"o_wgrad{i}", s["o"], _to_bf16, [_seg2d(dh16, tt, 2)], D // N_CHIPS, True, C, tt, deps,
                      BF16)
            do = _nt_cols(f"o_bwd{i}", [_seg2d(dh16, tm, 1)], w["w_o"], 0, tm, None)
            deps = sink.pump(do) + [sink.submit({("w_o", i): g_o})]
            dq, dk, dv = _attn_bwd(f"attn_bwd{i}", s["q"], kv, slopes, s["o"].reshape(B, S, C),
                                   s["lse"].reshape(B, S, C), do.reshape(B, S, C), n_heads, dkv)
            dkv = (dk, dv)
            dq = dq.reshape(T, 3 * C)
            g_q = _tn(f"q_wgrad{i}", s["n1"], _to_bf16, [_seg2d(dq, tt, 2)], 3 * C // N_CHIPS, True, D, tt, deps,
                      BF16)
            mixer = {("w_q", i): g_q}
            if i == 0:
                dk2, dv2 = (t.reshape(T, 3 * C) for t in dkv)
                mixer[("w_kv", 0)] = _tn("kv_wgrad", nkv, _to_bf16, _kv_segments(dk2, dv2, C, tt, 2),
                                         6 * C // N_CHIPS, True, D, tt, (), BF16)
            deps = [sink.submit(mixer)]
            dh, dh16, g_mix[l] = _nt_cols(f"q_bwd{i}", [_seg2d(dq, tm, 1)], w["w_q"], 0, tm,
                                          (s["h_in"], row(norm_mix[l]), dh), deps)
            if i == 0:
                dh, dh16, g_kv = _nt_cols("kv_bwd", _kv_segments(dk2, dv2, C, tm, 1), w["w_kv"], 0, tm,
                                          (h_kv, row(norm_kv), dh))
        deps = sink.pump(dh)
    small = dict(norm_mix=jnp.concatenate(g_mix, axis=0), norm_mlp=jnp.concatenate(g_mlp, axis=0),
                 norm_kv=g_kv, norm_final=dg_final, conv_w=jnp.stack(g_conv))
    return loss, dh.reshape(B, S, D), small


BIG = ("w_a_in", "w_a_out", "w_kv", "w_q", "w_o", "w_up", "w_down")
CONV_PAD_ROWS = 16


def kernel(x, norm_mix, norm_mlp, w_a_in, conv_w, w_a_out, norm_kv, w_kv, w_q, w_o, w_up, w_down, norm_final, loss_target, m_norm_mix, m_norm_mlp, m_w_a_in, m_conv_w, m_w_a_out, m_norm_kv, m_w_kv, m_w_q, m_w_o, m_w_up, m_w_down, m_norm_final, v_norm_mix, v_norm_mlp, v_w_a_in, v_conv_w, v_w_a_out, v_norm_kv, v_w_kv, v_w_q, v_w_o, v_w_up, v_w_down, v_norm_final):
    D = x.shape[-1]
    w = dict(norm_mix=norm_mix, norm_mlp=norm_mlp, w_a_in=w_a_in, conv_w=conv_w, w_a_out=w_a_out, norm_kv=norm_kv,
             w_kv=w_kv[None], w_q=w_q, w_o=w_o, w_up=w_up, w_down=w_down, norm_final=norm_final)
    m = dict(norm_mix=m_norm_mix, norm_mlp=m_norm_mlp, w_a_in=m_w_a_in, conv_w=m_conv_w, w_a_out=m_w_a_out,
             norm_kv=m_norm_kv, w_kv=m_w_kv[None], w_q=m_w_q, w_o=m_w_o, w_up=m_w_up, w_down=m_w_down,
             norm_final=m_norm_final)
    v = dict(norm_mix=v_norm_mix, norm_mlp=v_norm_mlp, w_a_in=v_w_a_in, conv_w=v_conv_w, w_a_out=v_w_a_out,
             norm_kv=v_norm_kv, w_kv=v_w_kv[None], w_q=v_w_q, w_o=v_w_o, w_up=v_w_up, w_down=v_w_down,
             norm_final=v_norm_final)
    depth = norm_mix.shape[0]
    n_a, taps, cwc = conv_w.shape
    n_heads = w_o.shape[1] // HEAD_DIM

    conv_rows = jnp.zeros((CONV_PAD_ROWS, cwc), F32).at[:n_a * taps].set(conv_w.reshape(n_a * taps, cwc))
    blocks = {}
    for l in range(depth):
        if l < n_a:
            blocks[(l, "w_a_in")] = w_a_in[l].astype(BF16)
            if l == 0:
                blocks[(0, "conv")] = conv_rows
            blocks[(l, "w_a_out")] = w_a_out[l].astype(BF16)
        else:
            if l == n_a:
                blocks[(l, "w_kv")] = w_kv.astype(BF16)
            blocks[(l, "w_q")] = w_q[l - n_a].astype(BF16)
            blocks[(l, "w_o")] = w_o[l - n_a].astype(BF16)
        blocks[(l, "w_up")] = w_up[l].astype(BF16)
        blocks[(l, "w_down")] = w_down[l].astype(BF16)
    weights = _WeightGather(blocks)
    place = jnp.stack([2 * lax.axis_index("x") + lax.axis_index("y"), lax.axis_index("c")]).astype(jnp.int32)
    sink = _GradReduce(place)

    loss, grad_x, small = _local_step(x, loss_target, norm_mix, norm_mlp, norm_kv, norm_final, weights, sink,
                                      n_a, n_heads)
    loss = lax.psum(loss[0, 0], ("x", "y", "c"))

    share = _PairShare(sink.finish(grad_x), BIG)
    grads = {}

    packed = jnp.concatenate([small["norm_mix"], small["norm_mlp"], small["norm_kv"], small["norm_final"],
                              small["conv_w"].reshape(n_a * taps, D)], axis=0)
    pad = (-packed.shape[0]) % 8
    packed = jnp.pad(packed, ((0, pad), (0, 0)))
    total = _small_allreduce(packed)
    grads["norm_mix"] = total[:depth]
    grads["norm_mlp"] = total[depth:2 * depth]
    grads["norm_kv"] = total[2 * depth]
    grads["norm_final"] = total[2 * depth + 1]
    chip = 2 * lax.axis_index("x") + lax.axis_index("y")
    conv_full = total[2 * depth + 2:2 * depth + 2 + n_a * taps].reshape(n_a, taps, N_CHIPS, cwc)
    grads["conv_w"] = lax.dynamic_index_in_dim(conv_full, chip, axis=2, keepdims=False)

    order = ("norm_mix", "norm_mlp", "w_a_in", "conv_w", "w_a_out", "norm_kv", "w_kv", "w_q", "w_o", "w_up",
             "w_down", "norm_final")
    delta, new_m, new_v = {}, {}, {}
    vec_names = ("norm_mix", "norm_mlp", "norm_kv", "norm_final")
    rows_of = lambda a: a.reshape(-1, D)
    vw, vg, vm_, vv = (jnp.concatenate([rows_of(t[k]) for k in vec_names], axis=0) for t in (w, grads, m, v))
    vpad = (-vw.shape[0]) % 8
    padrows = lambda a: jnp.pad(a, ((0, vpad), (0, 0)))
    vd, vnm, vnv = _adamw("adamw_norms", padrows(vw), padrows(vg), padrows(vm_), padrows(vv))
    off = 0
    for k in vec_names:
        r = rows_of(w[k]).shape[0]
        delta[k] = vd[off:off + r].reshape(w[k].shape)
        new_m[k] = vnm[off:off + r].reshape(w[k].shape)
        new_v[k] = vnv[off:off + r].reshape(w[k].shape)
        off += r
    cpad = (-n_a * taps) % 8
    two_d = lambda a: jnp.pad(a.reshape(-1, cwc), ((0, cpad), (0, 0)))
    cd, cnm, cnv = _adamw("adamw_conv_w", two_d(w["conv_w"]), two_d(grads["conv_w"]), two_d(m["conv_w"]),
                          two_d(v["conv_w"]))
    delta["conv_w"], new_m["conv_w"], new_v["conv_w"] = (t[:n_a * taps].reshape(conv_w.shape) for t in (cd, cnm, cnv))
    after = cd
    for k in sorted(BIG, key=lambda k: w[k].size):
        shared = share.get(k, after)
        per_layer = [shared[(k, l)].reshape(w[k].shape[1:]) for l in range(w[k].shape[0])]
        grads[k], delta[k], new_m[k], new_v[k] = _adamw_layers(f"adamw_{k}", w[k], per_layer, m[k], v[k])
        after = delta[k]
    fix = lambda k, a: a[0] if k == "w_kv" else a
    return (loss, grad_x, *[fix(k, grads[k]) for k in order], *[fix(k, delta[k]) for k in order],
            *[fix(k, new_m[k]) for k in order], *[fix(k, new_v[k]) for k in order])
```

```python
import jax
import jax.numpy as jnp
from jax import lax
from jax.experimental import pallas as pl
from jax.experimental.pallas import tpu as pltpu

F32 = jnp.float32
BF16 = jnp.bfloat16
MESH = pl.DeviceIdType.MESH

EPS = 1e-5
PATTERNS = ((128, 1), (512, 4), (2048, 16))
HEAD_DIM = 64
ALIBI_MAX_BIAS = 8.0
NEG_INF = -1e30
ATT_BLK = 128
BWD_UNROLL = 16
N_CHIPS = 4
LANES = 128
VMEM_LIMIT = 56 * 1024 * 1024

ADAM_LR = 0.001
ADAM_B1 = 0.9
ADAM_B2 = 0.999
ADAM_EPS = 1e-08
ADAM_WD = 0.01
ADAM_STEP = 10


ANY = pl.BlockSpec(memory_space=pl.ANY)


def _params(n_grid_axes):
    return pltpu.CompilerParams(dimension_semantics=("arbitrary",) * n_grid_axes, vmem_limit_bytes=VMEM_LIMIT)


def _dot(a, b):
    return jnp.dot(a, b, preferred_element_type=F32)


def _dot_nt(a, b):
    return lax.dot_general(a, b, (((1,), (1,)), ((), ())), preferred_element_type=F32)


def _dot_tn(a, b):
    return lax.dot_general(a, b, (((0,), (0,)), ((), ())), preferred_element_type=F32)


def _relu2(a):
    return jnp.square(jnp.maximum(a, 0.0))


def _rms(hf, g):
    y = hf * lax.rsqrt(jnp.mean(hf * hf, axis=-1, keepdims=True) + EPS)
    return y * g


def _rms_bwd(hf, g, dn):
    rstd = lax.rsqrt(jnp.mean(hf * hf, axis=-1, keepdims=True) + EPS)
    xhat = hf * rstd
    dg = jnp.sum(dn * xhat, axis=0, keepdims=True)
    dx = dn * g
    dh = rstd * (dx - xhat * jnp.mean(dx * xhat, axis=-1, keepdims=True))
    return dh, dg


def _pieces(seg_widths, chunk_width, max_width):
    total = sum(seg_widths)
    cuts = {0, total}
    acc = 0
    for w in seg_widths:
        cuts.add(acc)
        acc += w
    cuts.update(range(0, total, chunk_width))
    cuts = sorted(cuts)
    fine = []
    for lo, hi in zip(cuts[:-1], cuts[1:]):
        while hi - lo > max_width:
            fine.append((lo, lo + max_width))
            lo += max_width
        fine.append((lo, hi))
    out = []
    for lo, hi in fine:
        acc = 0
        for s, w in enumerate(seg_widths):
            if lo < acc + w:
                break
            acc += w
        out.append((s, lo - acc, lo // chunk_width, lo % chunk_width, hi - lo))
    return out


def _relu2_bf16(a):
    return _relu2(a.astype(F32)).astype(BF16)


def _to_bf16(a):
    return a.astype(BF16)


def _norm_mm(name, h, g, wg, layer, planes, out_dtype, tm, deps=()):
    T, D = h.shape
    cw = wg.shape[3]
    N = N_CHIPS * cw
    pw = N // planes
    pieces = _pieces([pw] * planes, cw, 512)

    def body(h_ref, g_ref, w_ref, *rest):
        n_ref, o_ref = rest[len(deps):]
        n = _rms(h_ref[...], g_ref[...]).astype(BF16)
        n_ref[...] = n
        for s, a0, ch, b0, wd in pieces:
            o_ref[s, :, a0:a0 + wd] = _dot(n, w_ref[ch, :, b0:b0 + wd]).astype(out_dtype)

    return pl.pallas_call(
        body, name=name, grid=(T // tm,),
        in_specs=[pl.BlockSpec((tm, D), lambda i: (i, 0)),
                  pl.BlockSpec((1, D), lambda i: (0, 0)),
                  pl.BlockSpec((N_CHIPS, None, D, cw), lambda i: (0, layer, 0, 0))] + [ANY] * len(deps),
        out_specs=[pl.BlockSpec((tm, D), lambda i: (i, 0)),
                   pl.BlockSpec((planes, tm, pw), lambda i: (0, i, 0))],
        out_shape=[jax.ShapeDtypeStruct((T, D), BF16), jax.ShapeDtypeStruct((planes, T, pw), out_dtype)],
        compiler_params=_params(1))(h, g, wg, *deps)


def _resident(shape, index_map):
    return pl.BlockSpec(shape, index_map, pipeline_mode=pl.Buffered(1))


def _mm_res_rows(name, a, wg, layer, h, act, tm):
    T = a.shape[0]
    rk, D = wg.shape[2], wg.shape[3]

    def body(a_ref, w_ref, h_ref, o_ref):
        acc = h_ref[...]
        for k in range(N_CHIPS):
            acc = acc + _dot(act(a_ref[:, k * rk:(k + 1) * rk]), w_ref[k])
        o_ref[...] = acc

    return pl.pallas_call(
        body, name=name, grid=(T // tm,),
        in_specs=[pl.BlockSpec((tm, N_CHIPS * rk), lambda i: (i, 0)),
                  pl.BlockSpec((N_CHIPS, None, rk, D), lambda i: (0, layer, 0, 0)),
                  pl.BlockSpec((tm, D), lambda i: (i, 0))],
        out_specs=pl.BlockSpec((tm, D), lambda i: (i, 0)),
        out_shape=jax.ShapeDtypeStruct((T, D), F32),
        compiler_params=_params(1))(a, wg, h)


def _mm_res_cols(name, a, wg, layer, h, tm):
    T, K = a.shape
    cw = wg.shape[3]
    D = N_CHIPS * cw

    def body(a_ref, w_ref, h_ref, o_ref):
        a16 = a_ref[...].astype(BF16)
        for j in range(N_CHIPS):
            o_ref[:, j * cw:(j + 1) * cw] = h_ref[:, j * cw:(j + 1) * cw] + _dot(a16, w_ref[j])

    return pl.pallas_call(
        body, name=name, grid=(T // tm,),
        in_specs=[pl.BlockSpec((tm, K), lambda i: (i, 0)),
                  pl.BlockSpec((N_CHIPS, None, K, cw), lambda i: (0, layer, 0, 0)),
                  pl.BlockSpec((tm, D), lambda i: (i, 0))],
        out_specs=pl.BlockSpec((tm, D), lambda i: (i, 0)),
        out_shape=jax.ShapeDtypeStruct((T, D), F32),
        compiler_params=_params(1))(a, wg, h)


def _mlp_fwd(name, h, g, wup, wdown, tm):
    T, D = h.shape
    cw = wup.shape[3]

    def body(h_ref, g_ref, wu_ref, wd_ref, n_ref, a_ref, o_ref):
        hf = h_ref[...]
        n = _rms(hf, g_ref[...]).astype(BF16)
        n_ref[...] = n
        acc = hf
        for ch in range(N_CHIPS):
            a16 = _dot(n, wu_ref[ch]).astype(BF16)
            a_ref[:, ch * cw:(ch + 1) * cw] = a16
            acc = acc + _dot(_relu2_bf16(a16), wd_ref[ch])
        o_ref[...] = acc

    row = pl.BlockSpec((tm, D), lambda i: (i, 0))
    return pl.pallas_call(
        body, name=name, grid=(T // tm,),
        in_specs=[row, pl.BlockSpec((1, D), lambda i: (0, 0)),
                  _resident((N_CHIPS, None, D, cw), lambda i: (0, 0, 0, 0)),
                  _resident((N_CHIPS, None, cw, D), lambda i: (0, 0, 0, 0))],
        out_specs=[row, pl.BlockSpec((tm, N_CHIPS * cw), lambda i: (i, 0)), row],
        out_shape=[jax.ShapeDtypeStruct((T, D), BF16), jax.ShapeDtypeStruct((T, N_CHIPS * cw), BF16),
                   jax.ShapeDtypeStruct((T, D), F32)],
        compiler_params=_params(1))(h, g, wup, wdown)


def _mlp_bwd(name, dh, dh16, a, wdown, wup, h_mid, g, tm, deps=()):
    T, D = dh.shape
    cw = wup.shape[3]
    F = N_CHIPS * cw

    def body(dh_ref, dh16_ref, a_ref, wd_ref, wu_ref, h_ref, g_ref, *rest):
        da_ref, out_ref, out16_ref, dg_ref = rest[len(deps):]
        d16 = dh16_ref[...]
        acc = None
        for ch in range(N_CHIPS):
            cols = slice(ch * cw, (ch + 1) * cw)
            da = (_dot_nt(d16, wd_ref[ch]) * (2.0 * jnp.maximum(a_ref[:, cols].astype(F32), 0.0))).astype(BF16)
            da_ref[:, cols] = da
            d = _dot_nt(da, wu_ref[ch])
            acc = d if acc is None else acc + d
        dh_c, dg = _rms_bwd(h_ref[...], g_ref[...], acc)
        out = dh_ref[...] + dh_c
        out_ref[...] = out
        out16_ref[...] = out.astype(BF16)

        @pl.when(pl.program_id(0) == 0)
        def _():
            dg_ref[...] = dg

        @pl.when(pl.program_id(0) > 0)
        def _():
            dg_ref[...] += dg

    row = pl.BlockSpec((tm, D), lambda i: (i, 0))
    wide = pl.BlockSpec((tm, F), lambda i: (i, 0))
    vec = pl.BlockSpec((1, D), lambda i: (0, 0))
    return pl.pallas_call(
        body, name=name, grid=(T // tm,),
        in_specs=[row, row, wide, _resident((N_CHIPS, None, cw, D), lambda i: (0, 0, 0, 0)),
                  _resident((N_CHIPS, None, D, cw), lambda i: (0, 0, 0, 0)), row, vec] + [ANY] * len(deps),
        out_specs=[wide, row, row, vec],
        out_shape=[jax.ShapeDtypeStruct((T, F), BF16), jax.ShapeDtypeStruct((T, D), F32),
                   jax.ShapeDtypeStruct((T, D), BF16), jax.ShapeDtypeStruct((1, D), F32)],
        compiler_params=_params(1))(dh, dh16, a, wdown, wup, h_mid, g, *deps)


CONV_ROWS = 256
CONV_HALO = 16
CONV_COLS = 2 * LANES


def _conv_shifted(ext, k, r0, rows, at_start):
    rolled = pltpu.roll(ext, k, 0)[CONV_HALO:]
    if not at_start:
        return rolled
    t = r0 + lax.broadcasted_iota(jnp.int32, rolled.shape, 0)
    return jnp.where(t >= k, rolled, 0.0)


def _conv_ahead(ext, k, r0, rows, S, at_end):
    rolled = pltpu.roll(ext, rows + CONV_HALO - k, 0)[:rows]
    if not at_end:
        return rolled
    t = r0 + lax.broadcasted_iota(jnp.int32, rolled.shape, 0)
    return jnp.where(t + k < S, rolled, 0.0)


def _conv_chunks(step, n, carry):
    carry = step(0, carry, True, n == 1)
    if n > 2:
        carry = lax.fori_loop(1, n - 1, lambda i, c: step(i, c, False, False), carry)
    if n > 1:
        carry = step(n - 1, carry, False, True)
    return carry


def _conv_fwd(name, bcu, cwg, layer, tc):
    _, B, S, D = bcu.shape
    cwc = cwg.shape[3]
    per_chunk = cwc // tc
    R = min(CONV_ROWS, S)

    def body(x_ref, w_ref, z_ref):
        w = [w_ref[k:k + 1, :] for k in range(3)]

        def step(i, carry, at_start, at_end):
            r0 = pl.multiple_of(i * R, R)
            h0 = pl.multiple_of(jnp.maximum(r0 - CONV_HALO, 0), CONV_HALO)
            ld = lambda p, start, rows: x_ref[p, pl.ds(start, rows), :].astype(F32)
            cu = jnp.concatenate([ld(1, h0, CONV_HALO) * ld(2, h0, CONV_HALO), ld(1, r0, R) * ld(2, r0, R)], axis=0)
            conv = w[0] * cu[CONV_HALO:]
            conv = conv + w[1] * _conv_shifted(cu, 1, r0, R, at_start)
            conv = conv + w[2] * _conv_shifted(cu, 2, r0, R, at_start)
            z_ref[pl.ds(r0, R), :] = (ld(0, r0, R) * conv).astype(BF16)
            return carry

        _conv_chunks(step, S // R, 0)

    return pl.pallas_call(
        body, name=name, grid=(B, D // tc),
        in_specs=[pl.BlockSpec((3, None, S, tc), lambda b, j: (0, b, 0, j)),
                  pl.BlockSpec((None, None, 3, tc), lambda b, j: (j // per_chunk, layer, 0, j % per_chunk))],
        out_specs=pl.BlockSpec((None, S, tc), lambda b, j: (b, 0, j)),
        out_shape=jax.ShapeDtypeStruct((B, S, D), BF16),
        compiler_params=_params(2))(bcu, cwg)


def _conv_bwd(name, bcu, dz, cwg, layer, tc):
    _, B, S, D = bcu.shape
    cwc = cwg.shape[3]
    per_chunk = cwc // tc
    R = min(CONV_ROWS, S)

    def body(x_ref, dz_ref, w_ref, d_ref, dw_ref):
        w = [w_ref[k:k + 1, :] for k in range(3)]

        @pl.when(pl.program_id(1) == 0)
        def _():
            dw_ref[...] = jnp.zeros_like(dw_ref)

        def step(i, carry, at_start, at_end):
            r0 = pl.multiple_of(i * R, R)
            h0 = pl.multiple_of(jnp.maximum(r0 - CONV_HALO, 0), CONV_HALO)
            a0 = pl.multiple_of(jnp.minimum(r0 + R, S - CONV_HALO), CONV_HALO)
            ld = lambda p, start, rows: x_ref[p, pl.ds(start, rows), :].astype(F32)
            b, c, u = ld(0, r0, R), ld(1, r0, R), ld(2, r0, R)
            dz = dz_ref[pl.ds(r0, R), :]
            cu = jnp.concatenate([ld(1, h0, CONV_HALO) * ld(2, h0, CONV_HALO), c * u], axis=0)
            cu1 = _conv_shifted(cu, 1, r0, R, at_start)
            cu2 = _conv_shifted(cu, 2, r0, R, at_start)
            conv = w[0] * (c * u) + w[1] * cu1 + w[2] * cu2
            dconv = dz * b
            dca = jnp.concatenate([dconv, dz_ref[pl.ds(a0, CONV_HALO), :] * ld(0, a0, CONV_HALO)], axis=0)
            dcu = (w[0] * dconv + w[1] * _conv_ahead(dca, 1, r0, R, S, at_end)
                   + w[2] * _conv_ahead(dca, 2, r0, R, S, at_end))
            d_ref[0, pl.ds(r0, R), :] = (dz * conv).astype(BF16)
            d_ref[1, pl.ds(r0, R), :] = (dcu * u).astype(BF16)
            d_ref[2, pl.ds(r0, R), :] = (dcu * c).astype(BF16)
            return (carry[0] + jnp.sum(dconv * (c * u), axis=0, keepdims=True),
                    carry[1] + jnp.sum(dconv * cu1, axis=0, keepdims=True),
                    carry[2] + jnp.sum(dconv * cu2, axis=0, keepdims=True))

        zero = jnp.zeros((1, tc), F32)
        s0, s1, s2 = _conv_chunks(step, S // R, (zero, zero, zero))
        for k, sk in enumerate((s0, s1, s2)):
            dw_ref[k:k + 1, :] += sk

    return pl.pallas_call(
        body, name=name, grid=(D // tc, B),
        in_specs=[pl.BlockSpec((3, None, S, tc), lambda j, b: (0, b, 0, j)),
                  pl.BlockSpec((None, S, tc), lambda j, b: (b, 0, j)),
                  pl.BlockSpec((None, None, 3, tc), lambda j, b: (j // per_chunk, layer, 0, j % per_chunk))],
        out_specs=[pl.BlockSpec((3, None, S, tc), lambda j, b: (0, b, 0, j)),
                   pl.BlockSpec((3, tc), lambda j, b: (0, j))],
        out_shape=[jax.ShapeDtypeStruct((3, B, S, D), BF16), jax.ShapeDtypeStruct((3, D), F32)],
        compiler_params=_params(2))(bcu, dz, cwg)


def _att_rows(dil, idx, nb):
    r, n = idx // nb, idx % nb
    if dil == 1:
        cur = pl.ds(pl.multiple_of(n * ATT_BLK, ATT_BLK), ATT_BLK)
        prev = pl.ds(pl.multiple_of(jnp.maximum(n - 1, 0) * ATT_BLK, ATT_BLK), ATT_BLK)
    else:
        cur = pl.ds(n * (ATT_BLK * dil) + r, ATT_BLK, stride=dil)
        prev = pl.ds(jnp.maximum(n - 1, 0) * (ATT_BLK * dil) + r, ATT_BLK, stride=dil)
    return n, cur, prev


def _att_bias(bias_ref, dil, sl_ref, hp):
    row = lax.broadcasted_iota(jnp.int32, (2 * ATT_BLK, 2 * ATT_BLK), 0)
    ci = lax.broadcasted_iota(jnp.int32, (2 * ATT_BLK, 2 * ATT_BLK), 1)
    j = ATT_BLK + (row & (ATT_BLK - 1)) - ci
    slope = jnp.where(row < ATT_BLK, sl_ref[2 * hp], sl_ref[2 * hp + 1])
    rest = jnp.where((j >= 0) & (j <= ATT_BLK), -slope * (dil * j).astype(F32), NEG_INF)
    bias_ref[1] = rest
    bias_ref[0] = jnp.where(ci >= ATT_BLK, rest, NEG_INF)


def _stack_heads(x16, lane):
    first = lane < HEAD_DIM
    return jnp.concatenate([jnp.where(first, x16, jnp.zeros_like(x16)),
                            jnp.where(first, jnp.zeros_like(x16), x16)], axis=0)


def _per_head(col, lane):
    return jnp.where(lane < HEAD_DIM, col[:ATT_BLK], col[ATT_BLK:])


def _attn_fwd(name, q, kv, slopes, n_heads):
    B, S, CQ = q.shape
    HP = n_heads * HEAD_DIM // LANES
    scale = HEAD_DIM ** -0.5
    n_groups = len(PATTERNS)
    CH = 256

    def body(sl_ref, q_ref, k_ref, v_ref, o_ref, lse_ref, bias_ref, *parts):
        og, lg = parts[:n_groups], parts[n_groups:]
        hp, g = pl.program_id(1), pl.program_id(2)
        lane = lax.broadcasted_iota(jnp.int32, (1, LANES), 1)

        for gi, (window, dil) in enumerate(PATTERNS):
            nb = S // dil // ATT_BLK

            @pl.when(g == gi)
            def _(gi=gi, dil=dil, nb=nb):
                _att_bias(bias_ref, dil, sl_ref, hp)

                def step(idx, carry):
                    n, cur, prev = _att_rows(dil, idx, nb)
                    qs = _stack_heads((q_ref[cur, :] * scale).astype(BF16), lane)
                    kc = jnp.concatenate([k_ref[prev, :], k_ref[cur, :]], axis=0).astype(BF16)
                    vc = jnp.concatenate([v_ref[prev, :], v_ref[cur, :]], axis=0).astype(BF16)
                    s = _dot_nt(qs, kc) + bias_ref[jnp.minimum(n, 1)]
                    m = jnp.max(s, axis=-1, keepdims=True)
                    p = jnp.exp(s - m)
                    l = jnp.sum(p, axis=-1, keepdims=True)
                    p16 = p.astype(BF16)
                    o_un = _dot(jnp.concatenate([p16[:ATT_BLK], p16[ATT_BLK:]], axis=1), _stack_heads_rows(vc, lane))
                    og[gi][cur, :] = o_un / _per_head(l, lane)
                    lg[gi][cur, :] = _per_head(m + jnp.log(l), lane)
                    return carry

                lax.fori_loop(0, S // ATT_BLK, step, 0, unroll=16)

        @pl.when(g == n_groups - 1)
        def _():
            def comb(i, carry):
                rows = pl.ds(pl.multiple_of(i * CH, CH), CH)
                a, b, c = lg[0][rows, :], lg[1][rows, :], lg[2][rows, :]
                m = jnp.maximum(jnp.maximum(a, b), c)
                ea, eb, ec = jnp.exp(a - m), jnp.exp(b - m), jnp.exp(c - m)
                z = ea + eb + ec
                o_ref[rows, :] = (ea / z) * og[0][rows, :] + (eb / z) * og[1][rows, :] + (ec / z) * og[2][rows, :]
                lse_ref[rows, :] = m + jnp.log(z)
                return carry

            lax.fori_loop(0, S // CH, comb, 0)

    blk = (None, S, LANES)
    out = pl.BlockSpec(blk, lambda b, hp, g: (b, 0, hp))
    return pl.pallas_call(
        body, name=name, grid=(B, HP, n_groups),
        in_specs=[pl.BlockSpec(memory_space=pltpu.SMEM),
                  pl.BlockSpec(blk, lambda b, hp, g: (b, 0, g * HP + hp)),
                  pl.BlockSpec(blk, lambda b, hp, g: (b, 0, g * 2 * HP + hp)),
                  pl.BlockSpec(blk, lambda b, hp, g: (b, 0, g * 2 * HP + HP + hp))],
        out_specs=[out, out],
        out_shape=[jax.ShapeDtypeStruct((B, S, HP * LANES), F32)] * 2,
        scratch_shapes=[pltpu.VMEM((2, 2 * ATT_BLK, 2 * ATT_BLK), F32)] + [pltpu.VMEM((S, LANES), F32)] * (2 * n_groups),
        compiler_params=_params(3))(slopes, q, kv, kv)


def _stack_heads_rows(x16, lane):
    first = lane < HEAD_DIM
    return jnp.concatenate([jnp.where(first, x16, jnp.zeros_like(x16)),
                            jnp.where(first, jnp.zeros_like(x16), x16)], axis=0)


def _attn_bwd(name, q, kv, slopes, o, lse, do, n_heads, dkv_prev):
    B, S, CQ = q.shape
    HP = n_heads * HEAD_DIM // LANES
    scale = HEAD_DIM ** -0.5
    n_groups = len(PATTERNS)
    n_prev = 0 if dkv_prev is None else 2

    def body(sl_ref, q_ref, k_ref, v_ref, o_ref, lse_ref, do_ref, *rest):
        dq_ref, dk_ref, dv_ref, bias_ref = rest[n_prev:]
        hp, g = pl.program_id(1), pl.program_id(2)
        lane = lax.broadcasted_iota(jnp.int32, (1, LANES), 1)
        first = lane < HEAD_DIM

        def flush(rows, dk, dv):
            if n_prev:
                dk = dk + rest[0][rows, :]
                dv = dv + rest[1][rows, :]
            dk_ref[rows, :] = dk
            dv_ref[rows, :] = dv

        for gi, (window, dil) in enumerate(PATTERNS):
            nb = S // dil // ATT_BLK
            n_blocks = S // ATT_BLK

            @pl.when(g == gi)
            def _(dil=dil, nb=nb, n_blocks=n_blocks):
                _att_bias(bias_ref, dil, sl_ref, hp)

                def block(idx, carry, first_of_all):
                    n, cur, prev = _att_rows(dil, idx, nb)
                    qs = _stack_heads((q_ref[cur, :] * scale).astype(BF16), lane)
                    kc = jnp.concatenate([k_ref[prev, :], k_ref[cur, :]], axis=0).astype(BF16)
                    vc = jnp.concatenate([v_ref[prev, :], v_ref[cur, :]], axis=0).astype(BF16)
                    dob = do_ref[cur, :]
                    prod = dob * o_ref[cur, :]
                    lseb = lse_ref[cur, :]
                    dos = _stack_heads(dob.astype(BF16), lane)
                    delta = jnp.concatenate(
                        [jnp.sum(jnp.where(first, prod, 0.0), axis=-1, keepdims=True),
                         jnp.sum(jnp.where(first, 0.0, prod), axis=-1, keepdims=True)], axis=0)
                    lse_col = jnp.concatenate(
                        [jnp.max(jnp.where(first, lseb, -jnp.inf), axis=-1, keepdims=True),
                         jnp.max(jnp.where(first, -jnp.inf, lseb), axis=-1, keepdims=True)], axis=0)
                    s = _dot_nt(qs, kc) + bias_ref[jnp.minimum(n, 1)]
                    p = jnp.exp(s - lse_col)
                    ds = p * (_dot_nt(dos, vc) - delta)
                    ds16 = ds.astype(BF16)
                    dq = _dot(jnp.concatenate([ds16[:ATT_BLK], ds16[ATT_BLK:]], axis=1), _stack_heads_rows(kc, lane))
                    dq_ref[cur, :] = dq * scale
                    dk = _dot_tn(ds16, qs)
                    dv = _dot_tn(p.astype(BF16), dos)

                    def flush_before():
                        _, before, _ = _att_rows(dil, idx - 1, nb)
                        flush(before, carry[0] + dk[:ATT_BLK], carry[1] + dv[:ATT_BLK])

                    if first_of_all:
                        pl.when(idx > 0)(flush_before)
                    else:
                        flush_before()
                    return dk[ATT_BLK:], dv[ATT_BLK:]

                def step(i, carry):
                    for u in range(BWD_UNROLL):
                        carry = block(i * BWD_UNROLL + u, carry, u == 0)
                    return carry

                zero = jnp.zeros((ATT_BLK, LANES), F32)
                dk_last, dv_last = lax.fori_loop(0, n_blocks // BWD_UNROLL, step, (zero, zero))
                _, last, _ = _att_rows(dil, n_blocks - 1, nb)
                flush(last, dk_last, dv_last)

    blk = (None, S, LANES)
    shared = pl.BlockSpec(blk, lambda b, hp, g: (b, 0, hp))
    grouped = pl.BlockSpec(blk, lambda b, hp, g: (b, 0, g * HP + hp))
    prev = [] if dkv_prev is None else list(dkv_prev)
    gshape = jax.ShapeDtypeStruct((B, S, n_groups * HP * LANES), F32)
    return pl.pallas_call(
        body, name=name, grid=(B, HP, n_groups),
        in_specs=[pl.BlockSpec(memory_space=pltpu.SMEM), grouped,
                  pl.BlockSpec(blk, lambda b, hp, g: (b, 0, g * 2 * HP + hp)),
                  pl.BlockSpec(blk, lambda b, hp, g: (b, 0, g * 2 * HP + HP + hp)),
                  shared, shared, shared] + [grouped] * n_prev,
        out_specs=[grouped] * 3, out_shape=[gshape] * 3,
        scratch_shapes=[pltpu.VMEM((2, 2 * ATT_BLK, 2 * ATT_BLK), F32)],
        compiler_params=_params(3))(slopes, q, kv, kv, o, lse, do, *prev)


def _final_loss(name, h, g, target, tm):
    T, D = h.shape

    def body(h_ref, g_ref, t_ref, loss_ref, dh_ref, dh16_ref, dg_ref):
        hf = h_ref[...]
        gv = g_ref[...]
        rstd = lax.rsqrt(jnp.mean(hf * hf, axis=-1, keepdims=True) + EPS)
        xhat = hf * rstd
        err = xhat * gv - t_ref[...]
        part = 0.5 * jnp.sum(jnp.mean(err * err, axis=-1, keepdims=True), axis=0, keepdims=True)
        dy = err * (1.0 / D)
        dg = jnp.sum(dy * xhat, axis=0, keepdims=True)
        dx = dy * gv
        dh = rstd * (dx - xhat * jnp.mean(dx * xhat, axis=-1, keepdims=True))
        dh_ref[...] = dh
        dh16_ref[...] = dh.astype(BF16)

        @pl.when(pl.program_id(0) == 0)
        def _():
            loss_ref[...] = part
            dg_ref[...] = dg

        @pl.when(pl.program_id(0) > 0)
        def _():
            loss_ref[...] += part
            dg_ref[...] += dg

    return pl.pallas_call(
        body, name=name, grid=(T // tm,),
        in_specs=[pl.BlockSpec((tm, D), lambda i: (i, 0)), pl.BlockSpec((1, D), lambda i: (0, 0)),
                  pl.BlockSpec((tm, D), lambda i: (i, 0))],
        out_specs=[pl.BlockSpec((1, 1), lambda i: (0, 0)), pl.BlockSpec((tm, D), lambda i: (i, 0)),
                   pl.BlockSpec((tm, D), lambda i: (i, 0)), pl.BlockSpec((1, D), lambda i: (0, 0))],
        out_shape=[jax.ShapeDtypeStruct((1, 1), F32), jax.ShapeDtypeStruct((T, D), F32),
                   jax.ShapeDtypeStruct((T, D), BF16), jax.ShapeDtypeStruct((1, D), F32)],
        compiler_params=_params(1))(h, g, target)


def _nt_rows(name, dh, wg, layer, a_mul, out_dtype, tm, deps=()):
    T, D = dh.shape
    rk = wg.shape[2]
    N = N_CHIPS * rk
    with_a = a_mul is not None

    def body(dh_ref, w_ref, *rest):
        o_ref = rest[-1]
        d16 = dh_ref[...]
        for ch in range(N_CHIPS):
            r = _dot_nt(d16, w_ref[ch])
            if with_a:
                r = r * (2.0 * jnp.maximum(rest[0][:, ch * rk:(ch + 1) * rk].astype(F32), 0.0))
            o_ref[:, ch * rk:(ch + 1) * rk] = r.astype(out_dtype)

    in_specs = [pl.BlockSpec((tm, D), lambda i: (i, 0)),
                pl.BlockSpec((N_CHIPS, None, rk, D), lambda i: (0, layer, 0, 0))]
    args = [dh, wg]
    if with_a:
        in_specs.append(pl.BlockSpec((tm, N), lambda i: (i, 0)))
        args.append(a_mul)
    in_specs += [ANY] * len(deps)
    args += list(deps)
    return pl.pallas_call(
        body, name=name, grid=(T // tm,), in_specs=in_specs,
        out_specs=pl.BlockSpec((tm, N), lambda i: (i, 0)),
        out_shape=jax.ShapeDtypeStruct((T, N), out_dtype),
        compiler_params=_params(1))(*args)


def _nt_cols(name, ysegs, wg, layer, tm, norm, deps=()):
    Nw, cw = wg.shape[2], wg.shape[3]
    widths = [bs[-1] for _, bs, _ in ysegs]
    pieces = _pieces(widths, cw, 1024)
    ns = len(ysegs)
    T = norm[0].shape[0] if norm is not None else ysegs[0][0].shape[-2]

    def body(*refs):
        y_refs = refs[:ns]
        w_ref = refs[ns]
        acc = refs[-1]
        for n, (s, a0, ch, b0, wd) in enumerate(pieces):
            d = _dot_nt(y_refs[s][:, a0:a0 + wd].astype(BF16), w_ref[ch, :, b0:b0 + wd])
            if n == 0:
                acc[...] = d
            else:
                acc[...] += d
        if norm is None:
            refs[ns + 1 + len(deps)][...] = acc[...]
        else:
            h_ref, g_ref, dhin_ref = refs[ns + 1:ns + 4]
            out_ref, out16_ref, dg_ref = refs[ns + 4 + len(deps):ns + 7 + len(deps)]
            dh_c, dg = _rms_bwd(h_ref[...], g_ref[...], acc[...])
            dh = dhin_ref[...] + dh_c
            out_ref[...] = dh
            out16_ref[...] = dh.astype(BF16)

            @pl.when(pl.program_id(0) == 0)
            def _():
                dg_ref[...] = dg

            @pl.when(pl.program_id(0) > 0)
            def _():
                dg_ref[...] += dg

    in_specs = [pl.BlockSpec(bs, im) for _, bs, im in ysegs]
    in_specs.append(pl.BlockSpec((N_CHIPS, None, Nw, cw), lambda i: (0, layer, 0, 0)))
    args = [a for a, _, _ in ysegs] + [wg]
    row = pl.BlockSpec((tm, Nw), lambda i: (i, 0))
    vec = pl.BlockSpec((1, Nw), lambda i: (0, 0))
    if norm is None:
        out_specs = row
        out_shape = jax.ShapeDtypeStruct((T, Nw), F32)
    else:
        in_specs += [row, vec, row]
        args += list(norm)
    in_specs += [ANY] * len(deps)
    args += list(deps)
    if norm is not None:
        out_specs = [row, row, vec]
        out_shape = [jax.ShapeDtypeStruct((T, Nw), F32), jax.ShapeDtypeStruct((T, Nw), BF16),
                     jax.ShapeDtypeStruct((1, Nw), F32)]
    return pl.pallas_call(
        body, name=name, grid=(T // tm,), in_specs=in_specs, out_specs=out_specs, out_shape=out_shape,
        scratch_shapes=[pltpu.VMEM((tm, Nw), F32)], compiler_params=_params(1))(*args)


def _tn(name, x, x_act, ysegs, cw, cols_layout, tmm, tt, deps=(), out_dtype=F32):
    T, M = x.shape
    widths = [bs[-1] for _, bs, _ in ysegs]
    N = sum(widths)
    pieces = _pieces(widths, cw if cols_layout else N, 1024)
    ns = len(ysegs)
    n_t = T // tt
    block = (N_CHIPS, tmm, cw) if cols_layout else (tmm, N)
    narrow = out_dtype != F32

    def body(x_ref, *refs):
        y_refs = refs[:ns]
        o_ref = refs[ns + len(deps)]
        acc = refs[-1] if narrow else o_ref

        @pl.when(pl.program_id(1) == 0)
        def _():
            acc[...] = jnp.zeros_like(acc)

        xt = x_act(x_ref[...])
        for s, a0, ch, b0, wd in pieces:
            d = _dot_tn(xt, y_refs[s][:, a0:a0 + wd].astype(BF16))
            if cols_layout:
                acc[ch, :, b0:b0 + wd] += d
            else:
                acc[:, b0:b0 + wd] += d
        if narrow:
            @pl.when(pl.program_id(1) == n_t - 1)
            def _():
                o_ref[...] = acc[...].astype(out_dtype)

    in_specs = [pl.BlockSpec((tt, tmm), lambda m, t: (t, m))] + [pl.BlockSpec(bs, im) for _, bs, im in ysegs]
    in_specs += [ANY] * len(deps)
    if cols_layout:
        out_specs = pl.BlockSpec(block, lambda m, t: (0, m, 0))
        out_shape = jax.ShapeDtypeStruct((N_CHIPS, M, cw), out_dtype)
    else:
        out_specs = pl.BlockSpec(block, lambda m, t: (m, 0))
        out_shape = jax.ShapeDtypeStruct((M, N), out_dtype)
    return pl.pallas_call(
        body, name=name, grid=(M // tmm, n_t), in_specs=in_specs, out_specs=out_specs, out_shape=out_shape,
        scratch_shapes=[pltpu.VMEM(block, F32)] if narrow else [],
        compiler_params=_params(2))(x, *[a for a, _, _ in ysegs], *deps)


def _seg2d(a, t_rows, grid_rank):
    w = a.shape[1]
    if grid_rank == 1:
        return (a, (t_rows, w), lambda i: (i, 0))
    return (a, (t_rows, w), lambda m, t: (t, 0))


def _kv_segments(dk, dv, C, t_rows, grid_rank):
    segs = []
    for g in range(len(PATTERNS)):
        for a in (dk, dv):
            if grid_rank == 1:
                segs.append((a, (t_rows, C), lambda i, g=g: (i, g)))
            else:
                segs.append((a, (t_rows, C), lambda m, t, g=g: (t, g)))
    return segs


def _seg_plane(a, plane, t_rows, grid_rank):
    w = a.shape[2]
    if grid_rank == 1:
        return (a, (None, t_rows, w), lambda i: (plane, i, 0))
    return (a, (None, t_rows, w), lambda m, t: (plane, t, 0))


def _row_tile(rows, row_bytes, budget_bytes=2 * 1024 * 1024):
    t = rows
    while t * row_bytes > budget_bytes and t % 32 == 0:
        t //= 2
    return t


N_DEVICES = 8


def _device_add(name, own, slots, place):
    _, _, hr, c = own.shape
    tr = _row_tile(hr, c * 4, 1024 * 1024)

    def body(place_ref, own_ref, *refs):
        o_ref = refs[-1]
        acc = own_ref[...].astype(F32)
        for r in refs[:-1]:
            acc = acc + r[...].astype(F32)
        o_ref[...] = acc

    def slot(k):
        return pl.BlockSpec((None, tr, c), lambda i, pr: ((2 * pr[0] + pr[1] + k) % N_DEVICES, i, 0))

    grid_spec = pltpu.PrefetchScalarGridSpec(
        num_scalar_prefetch=1, grid=(hr // tr,),
        in_specs=[pl.BlockSpec((None, None, tr, c), lambda i, pr: (pr[0], pr[1], i, 0))]
        + [slot(k) for k in range(1, N_DEVICES)],
        out_specs=pl.BlockSpec((None, tr, c), lambda i, pr: (pr[1], i, 0)))
    return pl.pallas_call(body, name=name, grid_spec=grid_spec,
                          out_shape=jax.ShapeDtypeStruct((2, hr, c), F32),
                          compiler_params=_params(1))(place, own, *[slots] * (N_DEVICES - 1))


def _adamw(name, w, g, m, v):
    rows, cols = w.shape
    tr = _row_tile(rows, cols * 4, 1024 * 1024)

    def body(w_ref, g_ref, m_ref, v_ref, d_ref, nm_ref, nv_ref):
        d_ref[...], nm_ref[...], nv_ref[...] = _adamw_math(w_ref[...], g_ref[...], m_ref[...], v_ref[...])

    spec = pl.BlockSpec((tr, cols), lambda i: (i, 0))
    return pl.pallas_call(
        body, name=name, grid=(rows // tr,), in_specs=[spec] * 4, out_specs=[spec] * 3,
        out_shape=[jax.ShapeDtypeStruct((rows, cols), F32)] * 3, compiler_params=_params(1))(w, g, m, v)


def _adamw_math(w, g, m, v):
    nm = ADAM_B1 * m + (1.0 - ADAM_B1) * g
    nv = ADAM_B2 * v + (1.0 - ADAM_B2) * jnp.square(g)
    m_hat = nm / (1.0 - ADAM_B1 ** ADAM_STEP)
    v_hat = nv / (1.0 - ADAM_B2 ** ADAM_STEP)
    return -ADAM_LR * (m_hat / (jnp.sqrt(v_hat) + ADAM_EPS) + ADAM_WD * w), nm, nv


def _adamw_layers(name, w, grads, m, v):
    L, r, c = w.shape
    tr = _row_tile(r, L * c * 4, 1024 * 1024)

    def body(*refs):
        w_ref, m_ref, v_ref = refs[:3]
        g_refs = refs[3:3 + L]
        go_ref, d_ref, nm_ref, nv_ref = refs[3 + L:]
        for l in range(L):
            g = g_refs[l][...]
            go_ref[l] = g
            d_ref[l], nm_ref[l], nv_ref[l] = _adamw_math(w_ref[l], g, m_ref[l], v_ref[l])

    stacked = pl.BlockSpec((L, tr, c), lambda i: (0, i, 0))
    return pl.pallas_call(
        body, name=name, grid=(r // tr,),
        in_specs=[stacked] * 3 + [pl.BlockSpec((tr, c), lambda i: (i, 0))] * L, out_specs=[stacked] * 4,
        out_shape=[jax.ShapeDtypeStruct((L, r, c), F32)] * 4, compiler_params=_params(1))(w, m, v, *grads)


def _place():
    x, y, c = lax.axis_index("x"), lax.axis_index("y"), lax.axis_index("c")
    chips = [(1 - x, y), (x, 1 - y), (1 - x, 1 - y)]
    return x, y, c, chips


HBM = pl.BlockSpec(memory_space=pltpu.HBM)
SEM = pl.BlockSpec(memory_space=pltpu.SEMAPHORE)
EFFECT = pltpu.SideEffectType.DATAFLOW_SIDE_EFFECTING


class _Copy:
    def __init__(self, src, src_view, land, dst_view, recv_view, target):
        self.src, self.src_view, self.land, self.dst_view, self.recv_view, self.target = (
            src, src_view, land, dst_view, recv_view, target)


def _whole(ref, place):
    return ref


def _split_start(name, srcs, land_shapes, plans, deps=()):
    skeys, lkeys = list(srcs), list(land_shapes)
    ns, nl, ng, nd = len(skeys), len(lkeys), len(plans), len(deps)

    def body(*refs):
        src = dict(zip(skeys, refs[:ns]))
        land = dict(zip(lkeys, refs[ns:ns + nl]))
        sems = refs[ns + nl + nd:ns + nl + nd + 2 * ng]
        token = refs[-1]
        place = _place()
        for gi, plan in enumerate(plans):
            for k, cp in enumerate(plan):
                dst = land[cp.land] if cp.land in land else src[cp.land]
                pltpu.make_async_remote_copy(
                    src_ref=cp.src_view(src[cp.src], place), dst_ref=cp.dst_view(dst, place),
                    send_sem=sems[2 * gi].at[k], recv_sem=sems[2 * gi + 1].at[k],
                    device_id=cp.target(place), device_id_type=MESH).start()
        token[...] = jnp.zeros_like(token)

    sem_shapes = []
    for plan in plans:
        sem_shapes += [pltpu.SemaphoreType.DMA((len(plan),))] * 2
    buffers = [srcs[k] for k in skeys] + [lax.empty(land_shapes[k].shape, land_shapes[k].dtype) for k in lkeys]
    outs = pl.pallas_call(
        body, name=name,
        out_shape=(*sem_shapes, *[pltpu.HBM(a.shape, a.dtype) for a in buffers], jax.ShapeDtypeStruct((8, LANES), F32)),
        in_specs=[HBM] * (ns + nl) + [ANY] * nd,
        out_specs=(*[SEM] * (2 * ng), *[HBM] * (ns + nl), pl.BlockSpec(memory_space=pltpu.VMEM)),
        input_output_aliases={i: 2 * ng + i for i in range(ns + nl)},
        compiler_params=pltpu.CompilerParams(has_side_effects=EFFECT),
    )(*[pltpu.with_memory_space_constraint(a, pltpu.HBM) for a in buffers], *deps)
    sems = [(outs[2 * gi], outs[2 * gi + 1]) for gi in range(ng)]
    thru = outs[2 * ng:2 * ng + ns + nl]
    return sems, dict(zip(skeys, thru[:ns])), dict(zip(lkeys, thru[ns:])), outs[-1]


def _split_wait(name, sems, srcs, lands, plan, after):
    skeys, lkeys = list(srcs), list(lands)
    ns, nl = len(skeys), len(lkeys)

    def body(*refs):
        src = dict(zip(skeys, refs[:ns]))
        land = dict(zip(lkeys, refs[ns:ns + nl]))
        ssem, rsem = refs[ns + nl], refs[ns + nl + 1]
        place = _place()
        for k, cp in enumerate(plan):
            dst = land[cp.land] if cp.land in land else src[cp.land]
            pltpu.make_async_remote_copy(
                src_ref=cp.src_view(src[cp.src], place), dst_ref=cp.dst_view(dst, place),
                send_sem=ssem.at[k], recv_sem=rsem.at[k],
                device_id=cp.target(place), device_id_type=MESH).wait_send()
            got = cp.recv_view(dst, place)
            pltpu.make_async_remote_copy(
                src_ref=got, dst_ref=got, send_sem=ssem.at[k], recv_sem=rsem.at[k],
                device_id=cp.target(place), device_id_type=MESH).wait_recv()

    buffers = [srcs[k] for k in skeys] + [lands[k] for k in lkeys]
    outs = pl.pallas_call(
        body, name=name, out_shape=tuple(pltpu.HBM(a.shape, a.dtype) for a in buffers),
        in_specs=(*[HBM] * (ns + nl), SEM, SEM, ANY), out_specs=tuple([HBM] * (ns + nl)),
        input_output_aliases={i: i for i in range(ns + nl)},
        compiler_params=pltpu.CompilerParams(has_side_effects=EFFECT),
    )(*buffers, sems[0], sems[1], after)
    return dict(zip(skeys, outs[:ns])), dict(zip(lkeys, outs[ns:]))


def _chip_of(place):
    x, y, c, chips = place
    return 2 * x + y


GATHER_FIRST = 2


class _WeightGather:
    def __init__(self, blocks):
        self.plans, shapes = {}, {}
        for key, a in blocks.items():
            shapes[key] = jax.ShapeDtypeStruct((N_CHIPS,) + a.shape, a.dtype)
            slot = lambda ref, place: ref.at[_chip_of(place)]
            plan = [_Copy(key, _whole, key, slot,
                          lambda ref, place, k=k: ref.at[2 * place[3][k][0] + place[3][k][1]],
                          lambda place, k=k: (place[3][k][0], place[3][k][1], place[2])) for k in range(3)]
            plan.append(_Copy(key, _whole, key, slot, slot, lambda place: (place[0], place[1], 1 - place[2])))
            self.plans[key] = plan
        keys = list(blocks)
        self.blocks, self.shapes = blocks, shapes
        self.sems, self.srcs, self.lands = {}, {}, {}
        self._start("gather_start_first", keys[:GATHER_FIRST], ())
        self.rest = keys[GATHER_FIRST:]

    def _start(self, name, part, deps):
        sems, srcs, lands, self.token = _split_start(name, {k: self.blocks[k] for k in part},
                                                     {k: self.shapes[k] for k in part}, [self.plans[k] for k in part], deps)
        self.sems.update(zip(part, sems))
        self.srcs.update(srcs)
        self.lands.update(lands)

    def get(self, l, name, after):
        key = (l, name)
        _, lands = _split_wait(f"gather_wait_{name}{l}", self.sems[key], {key: self.srcs[key]},
                               {key: self.lands[key]}, self.plans[key], after)
        if self.rest:
            self._start("gather_start", self.rest, [lands[key]])
            self.rest = []
        return lands[key][:, None]


class _GradReduce:
    def __init__(self, place):
        self.place = place
        self.jobs = []
        self.done = {}
        self.n = 0

    def submit(self, grads):
        views = {k: a.reshape(N_CHIPS, 2, a.shape[1] // 2, a.shape[2]) for k, a in grads.items()}
        shapes = {k: jax.ShapeDtypeStruct((N_DEVICES,) + a.shape[2:], a.dtype) for k, a in views.items()}

        def peer(place, k):
            x, y, c, _ = place
            return (1 - x if k & 4 else x, 1 - y if k & 2 else y, 1 - c if k & 1 else c)

        def index(dev):
            return 4 * dev[0] + 2 * dev[1] + dev[2]

        plan = []
        for key in views:
            for k in range(1, N_DEVICES):
                plan.append(_Copy(
                    key, lambda ref, place, k=k: ref.at[2 * peer(place, k)[0] + peer(place, k)[1], peer(place, k)[2]],
                    key, lambda ref, place: ref.at[index(place[:3])],
                    lambda ref, place, k=k: ref.at[index(peer(place, k))],
                    lambda place, k=k: peer(place, k)))
        sems, srcs, lands, token = _split_start(f"grad_start{self.n}", views, shapes, [plan])
        self.jobs.append(dict(id=self.n, sems=sems[0], srcs=srcs, lands=lands, plan=plan))
        self.n += 1
        return token

    def pump(self, after):
        return []

    def finish(self, after):
        for job in self.jobs:
            srcs, lands = _split_wait(f"grad_wait{job['id']}", job["sems"], job["srcs"], job["lands"], job["plan"],
                                      after)
            for i, k in enumerate(srcs):
                self.done[k] = _device_add(f"grad_add{job['id']}_{i}", srcs[k], lands[k], self.place)
        self.jobs = []
        return self.done


class _PairShare:
    def __init__(self, halves, types):
        sibling = lambda place: (place[0], place[1], 1 - place[2])
        mine = lambda ref, place: ref.at[place[2]]
        theirs = lambda ref, place: ref.at[1 - place[2]]
        self.plans = {t: [_Copy(k, mine, k, mine, theirs, sibling) for k in halves if k[0] == t] for t in types}
        sems, self.bufs, _, self.token = _split_start("share_start", halves, {}, list(self.plans.values()))
        self.sems = dict(zip(self.plans, sems))

    def get(self, t, after):
        keys = [cp.src for cp in self.plans[t]]
        bufs, _ = _split_wait(f"share_wait_{t}", self.sems[t], {k: self.bufs[k] for k in keys}, {}, self.plans[t], after)
        return bufs


def _small_allreduce(part):
    R, C = part.shape
    N_DEV = 8

    def body(in_ref, out_ref, slots, ssem, rsem):
        x, y, c, _ = _place()
        me = 4 * x + 2 * y + c
        sends = []
        for k in range(1, N_DEV):
            kx, ky, kc = (k >> 2) & 1, (k >> 1) & 1, k & 1
            peer = (1 - x if kx else x, 1 - y if ky else y, 1 - c if kc else c)
            cp = pltpu.make_async_remote_copy(
                src_ref=in_ref, dst_ref=slots.at[me], send_sem=ssem.at[k], recv_sem=rsem.at[k],
                device_id=peer, device_id_type=MESH)
            cp.start()
            sends.append(cp)
        slots[me] = in_ref[...]
        for k in range(1, N_DEV):
            kx, ky, kc = (k >> 2) & 1, (k >> 1) & 1, k & 1
            peer = (1 - x if kx else x, 1 - y if ky else y, 1 - c if kc else c)
            slot = slots.at[4 * peer[0] + 2 * peer[1] + peer[2]]
            pltpu.make_async_remote_copy(
                src_ref=slot, dst_ref=slot, send_sem=ssem.at[k], recv_sem=rsem.at[k],
                device_id=peer, device_id_type=MESH).wait_recv()
        acc = slots[0]
        for d in range(1, N_DEV):
            acc = acc + slots[d]
        out_ref[...] = acc
        for cp in sends:
            cp.wait_send()

    vm = pl.BlockSpec(memory_space=pltpu.VMEM)
    return pl.pallas_call(
        body, name="small_allreduce", in_specs=[vm], out_specs=vm,
        out_shape=jax.ShapeDtypeStruct((R, C), F32),
        scratch_shapes=[pltpu.VMEM((N_DEV, R, C), F32), pltpu.SemaphoreType.DMA((N_DEV,)),
                        pltpu.SemaphoreType.DMA((N_DEV,))])(part)


def _local_step(x, target, norm_mix, norm_mlp, norm_kv, norm_final, weights, sink, n_a, n_heads):
    B, S, D = x.shape
    T = B * S
    C = n_heads * HEAD_DIM
    depth = norm_mix.shape[0]
    slopes = 2.0 ** (-ALIBI_MAX_BIAS * jnp.arange(1, n_heads + 1, dtype=F32) / n_heads)
    tm = min(512, T)
    row = lambda v: v.reshape(1, -1)

    h = x.reshape(T, D)
    saved, Wl = [], []
    kv = nkv = h_kv = cwg = None
    for l in range(depth):
        s = {"h_in": h}
        w = {}
        Wl.append(w)
        if l < n_a:
            w["w_a_in"] = weights.get(l, "w_a_in", h)
            first = [weights.token] if l == 0 else []
            s["n1"], bcu = _norm_mm(f"a_in_fwd{l}", h, row(norm_mix[l]), w["w_a_in"], 0, 3, BF16, tm, first)
            s["bcu"] = bcu.reshape(3, B, S, D)
            if l == 0:
                cwg = weights.get(0, "conv", bcu)[:, 0, :n_a * 3].reshape(N_CHIPS, n_a, 3, -1)
            s["z"] = _conv_fwd(f"conv_fwd{l}", s["bcu"], cwg, l, CONV_COLS).reshape(T, D)
            w["w_a_out"] = weights.get(l, "w_a_out", s["z"])
            h = _mm_res_rows(f"a_out_fwd{l}", s["z"], w["w_a_out"], 0, h, _to_bf16, tm)
        else:
            i = l - n_a
            if i == 0:
                h_kv = h
                w["w_kv"] = weights.get(l, "w_kv", h)
                nkv, kv = _norm_mm("kv_fwd", h, row(norm_kv), w["w_kv"], 0, 1, F32, tm)
                kv = kv.reshape(B, S, 2 * 3 * C)
            w["w_q"] = weights.get(l, "w_q", h)
            s["n1"], q = _norm_mm(f"q_fwd{i}", h, row(norm_mix[l]), w["w_q"], 0, 1, F32, tm)
            s["q"] = q.reshape(B, S, 3 * C)
            o, lse = _attn_fwd(f"attn_fwd{i}", s["q"], kv, slopes, n_heads)
            s["o"], s["lse"] = o.reshape(T, C), lse.reshape(T, C)
            w["w_o"] = weights.get(l, "w_o", o)
            h = _mm_res_cols(f"o_fwd{i}", s["o"], w["w_o"], 0, h, tm)
        s["h_mid"] = h
        w["w_up"] = weights.get(l, "w_up", h)
        if l < n_a:
            s["n2"], a = _norm_mm(f"up_fwd{l}", h, row(norm_mlp[l]), w["w_up"], 0, 1, BF16, tm)
            s["a"] = a[0]
            w["w_down"] = weights.get(l, "w_down", a)
            h = _mm_res_rows(f"down_fwd{l}", s["a"], w["w_down"], 0, h, _relu2_bf16, tm)
        else:
            w["w_down"] = weights.get(l, "w_down", h)
            s["n2"], s["a"], h = _mlp_fwd(f"mlp_fwd{l}", h, row(norm_mlp[l]), w["w_up"], w["w_down"], tm)
        F = s["a"].shape[1]
        saved.append(s)

    loss, dh, dh16, dg_final = _final_loss("loss_head", h, row(norm_final), target.reshape(T, D), tm)

    g_mix, g_mlp = [None] * depth, [None] * depth
    g_conv = [None] * n_a
    dkv = None
    tt = min(512, T)
    deps = []
    for l in reversed(range(depth)):
        s, w = saved[l], Wl[l]
        g_down = _tn(f"down_wgrad{l}", s["a"], _relu2_bf16, [_seg2d(dh16, tt, 2)], None, False,
                     min(2048, F), tt, deps, BF16).reshape(N_CHIPS, F // N_CHIPS, D)
        da, dh, dh16, g_mlp[l] = _mlp_bwd(f"mlp_bwd{l}", dh, dh16, s["a"], w["w_down"], w["w_up"], s["h_mid"],
                                          row(norm_mlp[l]), tm)
        g_up = _tn(f"up_wgrad{l}", s["n2"], _to_bf16, [_seg2d(da, tt, 2)], F // N_CHIPS, True, D, tt, (), BF16)
        deps = sink.pump(dh) + [sink.submit({("w_up", l): g_up, ("w_down", l): g_down})]
        if l < n_a:
            g_out = _tn(f"a_out_wgrad{l}", s["z"], _to_bf16, [_seg2d(dh16, tt, 2)], None, False,
                        D, tt, deps, BF16).reshape(N_CHIPS, D // N_CHIPS, D)
            dz = _nt_rows(f"a_out_bwd{l}", dh16, w["w_a_out"], 0, None, F32, tm)
            deps = sink.pump(dz) + [sink.submit({("w_a_out", l): g_out})]
            dbcu, g_conv[l] = _conv_bwd(f"conv_bwd{l}", s["bcu"], dz.reshape(B, S, D), cwg, l, CONV_COLS)
            dbcu = dbcu.reshape(3, T, D)
            g_in = _tn(f"a_in_wgrad{l}", s["n1"], _to_bf16, [_seg_plane(dbcu, p, tt, 2) for p in range(3)],
                       3 * D // N_CHIPS, True, D, tt, deps, BF16)
            deps = [sink.submit({("w_a_in", l): g_in})]
            dh, dh16, g_mix[l] = _nt_cols(f"a_in_bwd{l}", [_seg_plane(dbcu, p, tm, 1) for p in range(3)],
                                          w["w_a_in"], 0, tm, (s["h_in"], row(norm_mix[l]), dh), deps)
        else:
            i = l - n_a
            g_o = _tn(f"o_wgrad{i}", s["o"], _to_bf16, [_seg2d(dh16, tt, 2)], D // N_CHIPS, True, C, tt, deps,
                      BF16)
            do = _nt_cols(f"o_bwd{i}", [_seg2d(dh16, tm, 1)], w["w_o"], 0, tm, None)
            deps = sink.pump(do) + [sink.submit({("w_o", i): g_o})]
            dq, dk, dv = _attn_bwd(f"attn_bwd{i}", s["q"], kv, slopes, s["o"].reshape(B, S, C),
                                   s["lse"].reshape(B, S, C), do.reshape(B, S, C), n_heads, dkv)
            dkv = (dk, dv)
            dq = dq.reshape(T, 3 * C)
            g_q = _tn(f"q_wgrad{i}", s["n1"], _to_bf16, [_seg2d(dq, tt, 2)], 3 * C // N_CHIPS, True, D, tt, deps,
                      BF16)
            mixer = {("w_q", i): g_q}
            if i == 0:
                dk2, dv2 = (t.reshape(T, 3 * C) for t in dkv)
                mixer[("w_kv", 0)] = _tn("kv_wgrad", nkv, _to_bf16, _kv_segments(dk2, dv2, C, tt, 2),
                                         6 * C // N_CHIPS, True, D, tt, (), BF16)
            deps = [sink.submit(mixer)]
            dh, dh16, g_mix[l] = _nt_cols(f"q_bwd{i}", [_seg2d(dq, tm, 1)], w["w_q"], 0, tm,
                                          (s["h_in"], row(norm_mix[l]), dh), deps)
            if i == 0:
                dh, dh16, g_kv = _nt_cols("kv_bwd", _kv_segments(dk2, dv2, C, tm, 1), w["w_kv"], 0, tm,
                                          (h_kv, row(norm_kv), dh))
        deps = sink.pump(dh)
    small = dict(norm_mix=jnp.concatenate(g_mix, axis=0), norm_mlp=jnp.concatenate(g_mlp, axis=0),
                 norm_kv=g_kv, norm_final=dg_final, conv_w=jnp.stack(g_conv))
    return loss, dh.reshape(B, S, D), small


BIG = ("w_a_in", "w_a_out", "w_kv", "w_q", "w_o", "w_up", "w_down")
CONV_PAD_ROWS = 16


def kernel(x, norm_mix, norm_mlp, w_a_in, conv_w, w_a_out, norm_kv, w_kv, w_q, w_o, w_up, w_down, norm_final, loss_target, m_norm_mix, m_norm_mlp, m_w_a_in, m_conv_w, m_w_a_out, m_norm_kv, m_w_kv, m_w_q, m_w_o, m_w_up, m_w_down, m_norm_final, v_norm_mix, v_norm_mlp, v_w_a_in, v_conv_w, v_w_a_out, v_norm_kv, v_w_kv, v_w_q, v_w_o, v_w_up, v_w_down, v_norm_final):
    D = x.shape[-1]
    w = dict(norm_mix=norm_mix, norm_mlp=norm_mlp, w_a_in=w_a_in, conv_w=conv_w, w_a_out=w_a_out, norm_kv=norm_kv,
             w_kv=w_kv[None], w_q=w_q, w_o=w_o, w_up=w_up, w_down=w_down, norm_final=norm_final)
    m = dict(norm_mix=m_norm_mix, norm_mlp=m_norm_mlp, w_a_in=m_w_a_in, conv_w=m_conv_w, w_a_out=m_w_a_out,
             norm_kv=m_norm_kv, w_kv=m_w_kv[None], w_q=m_w_q, w_o=m_w_o, w_up=m_w_up, w_down=m_w_down,
             norm_final=m_norm_final)
    v = dict(norm_mix=v_norm_mix, norm_mlp=v_norm_mlp, w_a_in=v_w_a_in, conv_w=v_conv_w, w_a_out=v_w_a_out,
             norm_kv=v_norm_kv, w_kv=v_w_kv[None], w_q=v_w_q, w_o=v_w_o, w_up=v_w_up, w_down=v_w_down,
             norm_final=v_norm_final)
    depth = norm_mix.shape[0]
    n_a, taps, cwc = conv_w.shape
    n_heads = w_o.shape[1] // HEAD_DIM

    conv_rows = jnp.zeros((CONV_PAD_ROWS, cwc), F32).at[:n_a * taps].set(conv_w.reshape(n_a * taps, cwc))
    blocks = {}
    for l in range(depth):
        if l < n_a:
            blocks[(l, "w_a_in")] = w_a_in[l].astype(BF16)
            if l == 0:
                blocks[(0, "conv")] = conv_rows
            blocks[(l, "w_a_out")] = w_a_out[l].astype(BF16)
        else:
            if l == n_a:
                blocks[(l, "w_kv")] = w_kv.astype(BF16)
            blocks[(l, "w_q")] = w_q[l - n_a].astype(BF16)
            blocks[(l, "w_o")] = w_o[l - n_a].astype(BF16)
        blocks[(l, "w_up")] = w_up[l].astype(BF16)
        blocks[(l, "w_down")] = w_down[l].astype(BF16)
    weights = _WeightGather(blocks)
    place = jnp.stack([2 * lax.axis_index("x") + lax.axis_index("y"), lax.axis_index("c")]).astype(jnp.int32)
    sink = _GradReduce(place)

    loss, grad_x, small = _local_step(x, loss_target, norm_mix, norm_mlp, norm_kv, norm_final, weights, sink,
                                      n_a, n_heads)
    loss = lax.psum(loss[0, 0], ("x", "y", "c"))

    share = _PairShare(sink.finish(grad_x), BIG)
    grads = {}

    packed = jnp.concatenate([small["norm_mix"], small["norm_mlp"], small["norm_kv"], small["norm_final"],
                              small["conv_w"].reshape(n_a * taps, D)], axis=0)
    pad = (-packed.shape[0]) % 8
    packed = jnp.pad(packed, ((0, pad), (0, 0)))
    total = _small_allreduce(packed)
    grads["norm_mix"] = total[:depth]
    grads["norm_mlp"] = total[depth:2 * depth]
    grads["norm_kv"] = total[2 * depth]
    grads["norm_final"] = total[2 * depth + 1]
    chip = 2 * lax.axis_index("x") + lax.axis_index("y")
    conv_full = total[2 * depth + 2:2 * depth + 2 + n_a * taps].reshape(n_a, taps, N_CHIPS, cwc)
    grads["conv_w"] = lax.dynamic_index_in_dim(conv_full, chip, axis=2, keepdims=False)

    order = ("norm_mix", "norm_mlp", "w_a_in", "conv_w", "w_a_out", "norm_kv", "w_kv", "w_q", "w_o", "w_up",
             "w_down", "norm_final")
    delta, new_m, new_v = {}, {}, {}
    vec_names = ("norm_mix", "norm_mlp", "norm_kv", "norm_final")
    rows_of = lambda a: a.reshape(-1, D)
    vw, vg, vm_, vv = (jnp.concatenate([rows_of(t[k]) for k in vec_names], axis=0) for t in (w, grads, m, v))
    vpad = (-vw.shape[0]) % 8
    padrows = lambda a: jnp.pad(a, ((0, vpad), (0, 0)))
    vd, vnm, vnv = _adamw("adamw_norms", padrows(vw), padrows(vg), padrows(vm_), padrows(vv))
    off = 0
    for k in vec_names:
        r = rows_of(w[k]).shape[0]
        delta[k] = vd[off:off + r].reshape(w[k].shape)
        new_m[k] = vnm[off:off + r].reshape(w[k].shape)
        new_v[k] = vnv[off:off + r].reshape(w[k].shape)
        off += r
    cpad = (-n_a * taps) % 8
    two_d = lambda a: jnp.pad(a.reshape(-1, cwc), ((0, cpad), (0, 0)))
    cd, cnm, cnv = _adamw("adamw_conv_w", two_d(w["conv_w"]), two_d(grads["conv_w"]), two_d(m["conv_w"]),
                          two_d(v["conv_w"]))
    delta["conv_w"], new_m["conv_w"], new_v["conv_w"] = (t[:n_a * taps].reshape(conv_w.shape) for t in (cd, cnm, cnv))
    after = cd
    for k in sorted(BIG, key=lambda k: w[k].size):
        shared = share.get(k, after)
        per_layer = [shared[(k, l)].reshape(w[k].shape[1:]) for l in range(w[k].shape[0])]
        grads[k], delta[k], new_m[k], new_v[k] = _adamw_layers(f"adamw_{k}", w[k], per_layer, m[k], v[k])
        after = delta[k]
    fix = lambda k, a: a[0] if k == "w_kv" else a
    return (loss, grad_x, *[fix(k, grads[k]) for k in order], *[fix(k, delta[k]) for k in order],
            *[fix(k, new_m[k]) for k in order], *[fix(k, new_v[k]) for k in order])
```

```python
import jax
import jax.numpy as jnp
from jax import lax
from jax.experimental import pallas as pl
from jax.experimental.pallas import tpu as pltpu

F32 = jnp.float32
BF16 = jnp.bfloat16
MESH = pl.DeviceIdType.MESH

EPS = 1e-5
PATTERNS = ((128, 1), (512, 4), (2048, 16))
HEAD_DIM = 64
ALIBI_MAX_BIAS = 8.0
NEG_INF = -1e30
ATT_BLK = 128
BWD_UNROLL = 16
N_CHIPS = 4
LANES = 128
VMEM_LIMIT = 56 * 1024 * 1024

ADAM_LR = 0.001
ADAM_B1 = 0.9
ADAM_B2 = 0.999
ADAM_EPS = 1e-08
ADAM_WD = 0.01
ADAM_STEP = 10


ANY = pl.BlockSpec(memory_space=pl.ANY)


def _params(n_grid_axes):
    return pltpu.CompilerParams(dimension_semantics=("arbitrary",) * n_grid_axes, vmem_limit_bytes=VMEM_LIMIT)


def _dot(a, b):
    return jnp.dot(a, b, preferred_element_type=F32)


def _dot_nt(a, b):
    return lax.dot_general(a, b, (((1,), (1,)), ((), ())), preferred_element_type=F32)


def _dot_tn(a, b):
    return lax.dot_general(a, b, (((0,), (0,)), ((), ())), preferred_element_type=F32)


def _relu2(a):
    return jnp.square(jnp.maximum(a, 0.0))


def _rms(hf, g):
    y = hf * lax.rsqrt(jnp.mean(hf * hf, axis=-1, keepdims=True) + EPS)
    return y * g


def _rms_bwd(hf, g, dn):
    rstd = lax.rsqrt(jnp.mean(hf * hf, axis=-1, keepdims=True) + EPS)
    xhat = hf * rstd
    dg = jnp.sum(dn * xhat, axis=0, keepdims=True)
    dx = dn * g
    dh = rstd * (dx - xhat * jnp.mean(dx * xhat, axis=-1, keepdims=True))
    return dh, dg


def _pieces(seg_widths, chunk_width, max_width):
    total = sum(seg_widths)
    cuts = {0, total}
    acc = 0
    for w in seg_widths:
        cuts.add(acc)
        acc += w
    cuts.update(range(0, total, chunk_width))
    cuts = sorted(cuts)
    fine = []
    for lo, hi in zip(cuts[:-1], cuts[1:]):
        while hi - lo > max_width:
            fine.append((lo, lo + max_width))
            lo += max_width
        fine.append((lo, hi))
    out = []
    for lo, hi in fine:
        acc = 0
        for s, w in enumerate(seg_widths):
            if lo < acc + w:
                break
            acc += w
        out.append((s, lo - acc, lo // chunk_width, lo % chunk_width, hi - lo))
    return out


def _relu2_bf16(a):
    return _relu2(a.astype(F32)).astype(BF16)


def _to_bf16(a):
    return a.astype(BF16)


def _rms_only(name, h, g, tm):
    T, D = h.shape

    def body(h_ref, g_ref, n_ref):
        n_ref[...] = _rms(h_ref[...], g_ref[...]).astype(BF16)

    row = pl.BlockSpec((tm, D), lambda i: (i, 0))
    return pl.pallas_call(
        body, name=name, grid=(T // tm,), in_specs=[row, pl.BlockSpec((1, D), lambda i: (0, 0))], out_specs=row,
        out_shape=jax.ShapeDtypeStruct((T, D), BF16), compiler_params=_params(1))(h, g)


def _norm_mm(name, h, g, wg, layer, planes, out_dtype, tm, deps=(), normed=False):
    T, D = h.shape
    cw = wg.shape[3]
    N = N_CHIPS * cw
    pw = N // planes
    pieces = _pieces([pw] * planes, cw, 512)

    def body(h_ref, g_ref, w_ref, *rest):
        n_ref, o_ref = rest[len(deps):]
        n = h_ref[...] if normed else _rms(h_ref[...], g_ref[...]).astype(BF16)
        n_ref[...] = n
        for s, a0, ch, b0, wd in pieces:
            o_ref[s, :, a0:a0 + wd] = _dot(n, w_ref[ch, :, b0:b0 + wd]).astype(out_dtype)

    return pl.pallas_call(
        body, name=name, grid=(T // tm,),
        in_specs=[pl.BlockSpec((tm, D), lambda i: (i, 0)),
                  pl.BlockSpec((1, D), lambda i: (0, 0)),
                  pl.BlockSpec((N_CHIPS, None, D, cw), lambda i: (0, layer, 0, 0))] + [ANY] * len(deps),
        out_specs=[pl.BlockSpec((tm, D), lambda i: (i, 0)),
                   pl.BlockSpec((planes, tm, pw), lambda i: (0, i, 0))],
        out_shape=[jax.ShapeDtypeStruct((T, D), BF16), jax.ShapeDtypeStruct((planes, T, pw), out_dtype)],
        compiler_params=_params(1))(h, g, wg, *deps)


def _resident(shape, index_map):
    return pl.BlockSpec(shape, index_map, pipeline_mode=pl.Buffered(1))


def _mm_res_rows(name, a, wg, layer, h, act, tm):
    T = a.shape[0]
    rk, D = wg.shape[2], wg.shape[3]

    def body(a_ref, w_ref, h_ref, o_ref):
        acc = h_ref[...]
        for k in range(N_CHIPS):
            acc = acc + _dot(act(a_ref[:, k * rk:(k + 1) * rk]), w_ref[k])
        o_ref[...] = acc

    return pl.pallas_call(
        body, name=name, grid=(T // tm,),
        in_specs=[pl.BlockSpec((tm, N_CHIPS * rk), lambda i: (i, 0)),
                  pl.BlockSpec((N_CHIPS, None, rk, D), lambda i: (0, layer, 0, 0)),
                  pl.BlockSpec((tm, D), lambda i: (i, 0))],
        out_specs=pl.BlockSpec((tm, D), lambda i: (i, 0)),
        out_shape=jax.ShapeDtypeStruct((T, D), F32),
        compiler_params=_params(1))(a, wg, h)


def _mm_res_cols(name, a, wg, layer, h, tm):
    T, K = a.shape
    cw = wg.shape[3]
    D = N_CHIPS * cw

    def body(a_ref, w_ref, h_ref, o_ref):
        a16 = a_ref[...].astype(BF16)
        for j in range(N_CHIPS):
            o_ref[:, j * cw:(j + 1) * cw] = h_ref[:, j * cw:(j + 1) * cw] + _dot(a16, w_ref[j])

    return pl.pallas_call(
        body, name=name, grid=(T // tm,),
        in_specs=[pl.BlockSpec((tm, K), lambda i: (i, 0)),
                  pl.BlockSpec((N_CHIPS, None, K, cw), lambda i: (0, layer, 0, 0)),
                  pl.BlockSpec((tm, D), lambda i: (i, 0))],
        out_specs=pl.BlockSpec((tm, D), lambda i: (i, 0)),
        out_shape=jax.ShapeDtypeStruct((T, D), F32),
        compiler_params=_params(1))(a, wg, h)


def _mlp_fwd(name, h, g, wup, wdown, tm):
    T, D = h.shape
    cw = wup.shape[3]

    def body(h_ref, g_ref, wu_ref, wd_ref, n_ref, a_ref, o_ref):
        hf = h_ref[...]
        n = _rms(hf, g_ref[...]).astype(BF16)
        n_ref[...] = n
        acc = hf
        for ch in range(N_CHIPS):
            a16 = _dot(n, wu_ref[ch]).astype(BF16)
            a_ref[:, ch * cw:(ch + 1) * cw] = a16
            acc = acc + _dot(_relu2_bf16(a16), wd_ref[ch])
        o_ref[...] = acc

    row = pl.BlockSpec((tm, D), lambda i: (i, 0))
    return pl.pallas_call(
        body, name=name, grid=(T // tm,),
        in_specs=[row, pl.BlockSpec((1, D), lambda i: (0, 0)),
                  _resident((N_CHIPS, None, D, cw), lambda i: (0, 0, 0, 0)),
                  _resident((N_CHIPS, None, cw, D), lambda i: (0, 0, 0, 0))],
        out_specs=[row, pl.BlockSpec((tm, N_CHIPS * cw), lambda i: (i, 0)), row],
        out_shape=[jax.ShapeDtypeStruct((T, D), BF16), jax.ShapeDtypeStruct((T, N_CHIPS * cw), BF16),
                   jax.ShapeDtypeStruct((T, D), F32)],
        compiler_params=_params(1))(h, g, wup, wdown)


def _mlp_bwd(name, dh, dh16, a, wdown, wup, h_mid, g, tm, deps=()):
    T, D = dh.shape
    cw = wup.shape[3]
    F = N_CHIPS * cw

    def body(dh_ref, dh16_ref, a_ref, wd_ref, wu_ref, h_ref, g_ref, *rest):
        da_ref, out_ref, out16_ref, dg_ref = rest[len(deps):]
        d16 = dh16_ref[...]
        acc = None
        for ch in range(N_CHIPS):
            cols = slice(ch * cw, (ch + 1) * cw)
            da = (_dot_nt(d16, wd_ref[ch]) * (2.0 * jnp.maximum(a_ref[:, cols].astype(F32), 0.0))).astype(BF16)
            da_ref[:, cols] = da
            d = _dot_nt(da, wu_ref[ch])
            acc = d if acc is None else acc + d
        dh_c, dg = _rms_bwd(h_ref[...], g_ref[...], acc)
        out = dh_ref[...] + dh_c
        out_ref[...] = out
        out16_ref[...] = out.astype(BF16)

        @pl.when(pl.program_id(0) == 0)
        def _():
            dg_ref[...] = dg

        @pl.when(pl.program_id(0) > 0)
        def _():
            dg_ref[...] += dg

    row = pl.BlockSpec((tm, D), lambda i: (i, 0))
    wide = pl.BlockSpec((tm, F), lambda i: (i, 0))
    vec = pl.BlockSpec((1, D), lambda i: (0, 0))
    return pl.pallas_call(
        body, name=name, grid=(T // tm,),
        in_specs=[row, row, wide, _resident((N_CHIPS, None, cw, D), lambda i: (0, 0, 0, 0)),
                  _resident((N_CHIPS, None, D, cw), lambda i: (0, 0, 0, 0)), row, vec] + [ANY] * len(deps),
        out_specs=[wide, row, row, vec],
        out_shape=[jax.ShapeDtypeStruct((T, F), BF16), jax.ShapeDtypeStruct((T, D), F32),
                   jax.ShapeDtypeStruct((T, D), BF16), jax.ShapeDtypeStruct((1, D), F32)],
        compiler_params=_params(1))(dh, dh16, a, wdown, wup, h_mid, g, *deps)


CONV_ROWS = 256
CONV_HALO = 16
CONV_COLS = 2 * LANES


def _conv_shifted(ext, k, r0, rows, at_start):
    rolled = pltpu.roll(ext, k, 0)[CONV_HALO:]
    if not at_start:
        return rolled
    t = r0 + lax.broadcasted_iota(jnp.int32, rolled.shape, 0)
    return jnp.where(t >= k, rolled, 0.0)


def _conv_ahead(ext, k, r0, rows, S, at_end):
    rolled = pltpu.roll(ext, rows + CONV_HALO - k, 0)[:rows]
    if not at_end:
        return rolled
    t = r0 + lax.broadcasted_iota(jnp.int32, rolled.shape, 0)
    return jnp.where(t + k < S, rolled, 0.0)


def _conv_chunks(step, n, carry):
    carry = step(0, carry, True, n == 1)
    if n > 2:
        carry = lax.fori_loop(1, n - 1, lambda i, c: step(i, c, False, False), carry)
    if n > 1:
        carry = step(n - 1, carry, False, True)
    return carry


def _conv_fwd(name, bcu, cwg, layer, tc):
    _, B, S, D = bcu.shape
    cwc = cwg.shape[3]
    per_chunk = cwc // tc
    R = min(CONV_ROWS, S)

    def body(x_ref, w_ref, z_ref):
        w = [w_ref[k:k + 1, :] for k in range(3)]

        def step(i, carry, at_start, at_end):
            r0 = pl.multiple_of(i * R, R)
            h0 = pl.multiple_of(jnp.maximum(r0 - CONV_HALO, 0), CONV_HALO)
            ld = lambda p, start, rows: x_ref[p, pl.ds(start, rows), :].astype(F32)
            cu = jnp.concatenate([ld(1, h0, CONV_HALO) * ld(2, h0, CONV_HALO), ld(1, r0, R) * ld(2, r0, R)], axis=0)
            conv = w[0] * cu[CONV_HALO:]
            conv = conv + w[1] * _conv_shifted(cu, 1, r0, R, at_start)
            conv = conv + w[2] * _conv_shifted(cu, 2, r0, R, at_start)
            z_ref[pl.ds(r0, R), :] = (ld(0, r0, R) * conv).astype(BF16)
            return carry

        _conv_chunks(step, S // R, 0)

    return pl.pallas_call(
        body, name=name, grid=(B, D // tc),
        in_specs=[pl.BlockSpec((3, None, S, tc), lambda b, j: (0, b, 0, j)),
                  pl.BlockSpec((None, None, 3, tc), lambda b, j: (j // per_chunk, layer, 0, j % per_chunk))],
        out_specs=pl.BlockSpec((None, S, tc), lambda b, j: (b, 0, j)),
        out_shape=jax.ShapeDtypeStruct((B, S, D), BF16),
        compiler_params=_params(2))(bcu, cwg)


def _conv_bwd(name, bcu, dz, cwg, layer, tc):
    _, B, S, D = bcu.shape
    cwc = cwg.shape[3]
    per_chunk = cwc // tc
    R = min(CONV_ROWS, S)

    def body(x_ref, dz_ref, w_ref, d_ref, dw_ref):
        w = [w_ref[k:k + 1, :] for k in range(3)]

        @pl.when(pl.program_id(1) == 0)
        def _():
            dw_ref[...] = jnp.zeros_like(dw_ref)

        def step(i, carry, at_start, at_end):
            r0 = pl.multiple_of(i * R, R)
            h0 = pl.multiple_of(jnp.maximum(r0 - CONV_HALO, 0), CONV_HALO)
            a0 = pl.multiple_of(jnp.minimum(r0 + R, S - CONV_HALO), CONV_HALO)
            ld = lambda p, start, rows: x_ref[p, pl.ds(start, rows), :].astype(F32)
            b, c, u = ld(0, r0, R), ld(1, r0, R), ld(2, r0, R)
            dz = dz_ref[pl.ds(r0, R), :]
            cu = jnp.concatenate([ld(1, h0, CONV_HALO) * ld(2, h0, CONV_HALO), c * u], axis=0)
            cu1 = _conv_shifted(cu, 1, r0, R, at_start)
            cu2 = _conv_shifted(cu, 2, r0, R, at_start)
            conv = w[0] * (c * u) + w[1] * cu1 + w[2] * cu2
            dconv = dz * b
            dca = jnp.concatenate([dconv, dz_ref[pl.ds(a0, CONV_HALO), :] * ld(0, a0, CONV_HALO)], axis=0)
            dcu = (w[0] * dconv + w[1] * _conv_ahead(dca, 1, r0, R, S, at_end)
                   + w[2] * _conv_ahead(dca, 2, r0, R, S, at_end))
            d_ref[0, pl.ds(r0, R), :] = (dz * conv).astype(BF16)
            d_ref[1, pl.ds(r0, R), :] = (dcu * u).astype(BF16)
            d_ref[2, pl.ds(r0, R), :] = (dcu * c).astype(BF16)
            return (carry[0] + jnp.sum(dconv * (c * u), axis=0, keepdims=True),
                    carry[1] + jnp.sum(dconv * cu1, axis=0, keepdims=True),
                    carry[2] + jnp.sum(dconv * cu2, axis=0, keepdims=True))

        zero = jnp.zeros((1, tc), F32)
        s0, s1, s2 = _conv_chunks(step, S // R, (zero, zero, zero))
        for k, sk in enumerate((s0, s1, s2)):
            dw_ref[k:k + 1, :] += sk

    return pl.pallas_call(
        body, name=name, grid=(D // tc, B),
        in_specs=[pl.BlockSpec((3, None, S, tc), lambda j, b: (0, b, 0, j)),
                  pl.BlockSpec((None, S, tc), lambda j, b: (b, 0, j)),
                  pl.BlockSpec((None, None, 3, tc), lambda j, b: (j // per_chunk, layer, 0, j % per_chunk))],
        out_specs=[pl.BlockSpec((3, None, S, tc), lambda j, b: (0, b, 0, j)),
                   pl.BlockSpec((3, tc), lambda j, b: (0, j))],
        out_shape=[jax.ShapeDtypeStruct((3, B, S, D), BF16), jax.ShapeDtypeStruct((3, D), F32)],
        compiler_params=_params(2))(bcu, dz, cwg)


def _att_rows(dil, idx, nb):
    r, n = idx // nb, idx % nb
    if dil == 1:
        cur = pl.ds(pl.multiple_of(n * ATT_BLK, ATT_BLK), ATT_BLK)
        prev = pl.ds(pl.multiple_of(jnp.maximum(n - 1, 0) * ATT_BLK, ATT_BLK), ATT_BLK)
    else:
        cur = pl.ds(n * (ATT_BLK * dil) + r, ATT_BLK, stride=dil)
        prev = pl.ds(jnp.maximum(n - 1, 0) * (ATT_BLK * dil) + r, ATT_BLK, stride=dil)
    return n, cur, prev


def _att_bias(bias_ref, dil, sl_ref, hp):
    row = lax.broadcasted_iota(jnp.int32, (2 * ATT_BLK, 2 * ATT_BLK), 0)
    ci = lax.broadcasted_iota(jnp.int32, (2 * ATT_BLK, 2 * ATT_BLK), 1)
    j = ATT_BLK + (row & (ATT_BLK - 1)) - ci
    slope = jnp.where(row < ATT_BLK, sl_ref[2 * hp], sl_ref[2 * hp + 1])
    rest = jnp.where((j >= 0) & (j <= ATT_BLK), -slope * (dil * j).astype(F32), NEG_INF)
    bias_ref[1] = rest
    bias_ref[0] = jnp.where(ci >= ATT_BLK, rest, NEG_INF)


def _stack_heads(x16, lane):
    first = lane < HEAD_DIM
    return jnp.concatenate([jnp.where(first, x16, jnp.zeros_like(x16)),
                            jnp.where(first, jnp.zeros_like(x16), x16)], axis=0)


def _per_head(col, lane):
    return jnp.where(lane < HEAD_DIM, col[:ATT_BLK], col[ATT_BLK:])


def _attn_fwd(name, q, kv, slopes, n_heads):
    B, S, CQ = q.shape
    HP = n_heads * HEAD_DIM // LANES
    scale = HEAD_DIM ** -0.5
    n_groups = len(PATTERNS)
    CH = 256

    def body(sl_ref, q_ref, k_ref, v_ref, o_ref, lse_ref, bias_ref, *parts):
        og, lg = parts[:n_groups], parts[n_groups:]
        hp, g = pl.program_id(1), pl.program_id(2)
        lane = lax.broadcasted_iota(jnp.int32, (1, LANES), 1)

        for gi, (window, dil) in enumerate(PATTERNS):
            nb = S // dil // ATT_BLK

            @pl.when(g == gi)
            def _(gi=gi, dil=dil, nb=nb):
                _att_bias(bias_ref, dil, sl_ref, hp)

                def step(idx, carry):
                    n, cur, prev = _att_rows(dil, idx, nb)
                    qs = _stack_heads((q_ref[cur, :] * scale).astype(BF16), lane)
                    kc = jnp.concatenate([k_ref[prev, :], k_ref[cur, :]], axis=0).astype(BF16)
                    vc = jnp.concatenate([v_ref[prev, :], v_ref[cur, :]], axis=0).astype(BF16)
                    s = _dot_nt(qs, kc) + bias_ref[jnp.minimum(n, 1)]
                    m = jnp.max(s, axis=-1, keepdims=True)
                    p = jnp.exp(s - m)
                    l = jnp.sum(p, axis=-1, keepdims=True)
                    p16 = p.astype(BF16)
                    o_un = _dot(jnp.concatenate([p16[:ATT_BLK], p16[ATT_BLK:]], axis=1), _stack_heads_rows(vc, lane))
                    og[gi][cur, :] = o_un / _per_head(l, lane)
                    lg[gi][cur, :] = _per_head(m + jnp.log(l), lane)
                    return carry

                lax.fori_loop(0, S // ATT_BLK, step, 0, unroll=16)

        @pl.when(g == n_groups - 1)
        def _():
            def comb(i, carry):
                rows = pl.ds(pl.multiple_of(i * CH, CH), CH)
                a, b, c = lg[0][rows, :], lg[1][rows, :], lg[2][rows, :]
                m = jnp.maximum(jnp.maximum(a, b), c)
                ea, eb, ec = jnp.exp(a - m), jnp.exp(b - m), jnp.exp(c - m)
                z = ea + eb + ec
                o_ref[rows, :] = (ea / z) * og[0][rows, :] + (eb / z) * og[1][rows, :] + (ec / z) * og[2][rows, :]
                lse_ref[rows, :] = m + jnp.log(z)
                return carry

            lax.fori_loop(0, S // CH, comb, 0)

    blk = (None, S, LANES)
    out = pl.BlockSpec(blk, lambda b, hp, g: (b, 0, hp))
    return pl.pallas_call(
        body, name=name, grid=(B, HP, n_groups),
        in_specs=[pl.BlockSpec(memory_space=pltpu.SMEM),
                  pl.BlockSpec(blk, lambda b, hp, g: (b, 0, g * HP + hp)),
                  pl.BlockSpec(blk, lambda b, hp, g: (b, 0, g * 2 * HP + hp)),
                  pl.BlockSpec(blk, lambda b, hp, g: (b, 0, g * 2 * HP + HP + hp))],
        out_specs=[out, out],
        out_shape=[jax.ShapeDtypeStruct((B, S, HP * LANES), F32)] * 2,
        scratch_shapes=[pltpu.VMEM((2, 2 * ATT_BLK, 2 * ATT_BLK), F32)] + [pltpu.VMEM((S, LANES), F32)] * (2 * n_groups),
        compiler_params=_params(3))(slopes, q, kv, kv)


def _stack_heads_rows(x16, lane):
    first = lane < HEAD_DIM
    return jnp.concatenate([jnp.where(first, x16, jnp.zeros_like(x16)),
                            jnp.where(first, jnp.zeros_like(x16), x16)], axis=0)


def _attn_bwd(name, q, kv, slopes, o, lse, do, n_heads, dkv_prev):
    B, S, CQ = q.shape
    HP = n_heads * HEAD_DIM // LANES
    scale = HEAD_DIM ** -0.5
    n_groups = len(PATTERNS)
    n_prev = 0 if dkv_prev is None else 2

    def body(sl_ref, q_ref, k_ref, v_ref, o_ref, lse_ref, do_ref, *rest):
        dq_ref, dk_ref, dv_ref, bias_ref = rest[n_prev:]
        hp, g = pl.program_id(1), pl.program_id(2)
        lane = lax.broadcasted_iota(jnp.int32, (1, LANES), 1)
        first = lane < HEAD_DIM

        def flush(rows, dk, dv):
            if n_prev:
                dk = dk + rest[0][rows, :]
                dv = dv + rest[1][rows, :]
            dk_ref[rows, :] = dk
            dv_ref[rows, :] = dv

        for gi, (window, dil) in enumerate(PATTERNS):
            nb = S // dil // ATT_BLK
            n_blocks = S // ATT_BLK

            @pl.when(g == gi)
            def _(dil=dil, nb=nb, n_blocks=n_blocks):
                _att_bias(bias_ref, dil, sl_ref, hp)

                def block(idx, carry, first_of_all):
                    n, cur, prev = _att_rows(dil, idx, nb)
                    qs = _stack_heads((q_ref[cur, :] * scale).astype(BF16), lane)
                    kc = jnp.concatenate([k_ref[prev, :], k_ref[cur, :]], axis=0).astype(BF16)
                    vc = jnp.concatenate([v_ref[prev, :], v_ref[cur, :]], axis=0).astype(BF16)
                    dob = do_ref[cur, :]
                    prod = dob * o_ref[cur, :]
                    lseb = lse_ref[cur, :]
                    dos = _stack_heads(dob.astype(BF16), lane)
                    delta = jnp.concatenate(
                        [jnp.sum(jnp.where(first, prod, 0.0), axis=-1, keepdims=True),
                         jnp.sum(jnp.where(first, 0.0, prod), axis=-1, keepdims=True)], axis=0)
                    lse_col = jnp.concatenate(
                        [jnp.max(jnp.where(first, lseb, -jnp.inf), axis=-1, keepdims=True),
                         jnp.max(jnp.where(first, -jnp.inf, lseb), axis=-1, keepdims=True)], axis=0)
                    s = _dot_nt(qs, kc) + bias_ref[jnp.minimum(n, 1)]
                    p = jnp.exp(s - lse_col)
                    ds = p * (_dot_nt(dos, vc) - delta)
                    ds16 = ds.astype(BF16)
                    dq = _dot(jnp.concatenate([ds16[:ATT_BLK], ds16[ATT_BLK:]], axis=1), _stack_heads_rows(kc, lane))
                    dq_ref[cur, :] = dq * scale
                    dk = _dot_tn(ds16, qs)
                    dv = _dot_tn(p.astype(BF16), dos)

                    def flush_before():
                        _, before, _ = _att_rows(dil, idx - 1, nb)
                        flush(before, carry[0] + dk[:ATT_BLK], carry[1] + dv[:ATT_BLK])

                    if first_of_all:
                        pl.when(idx > 0)(flush_before)
                    else:
                        flush_before()
                    return dk[ATT_BLK:], dv[ATT_BLK:]

                def step(i, carry):
                    for u in range(BWD_UNROLL):
                        carry = block(i * BWD_UNROLL + u, carry, u == 0)
                    return carry

                zero = jnp.zeros((ATT_BLK, LANES), F32)
                dk_last, dv_last = lax.fori_loop(0, n_blocks // BWD_UNROLL, step, (zero, zero))
                _, last, _ = _att_rows(dil, n_blocks - 1, nb)
                flush(last, dk_last, dv_last)

    blk = (None, S, LANES)
    shared = pl.BlockSpec(blk, lambda b, hp, g: (b, 0, hp))
    grouped = pl.BlockSpec(blk, lambda b, hp, g: (b, 0, g * HP + hp))
    prev = [] if dkv_prev is None else list(dkv_prev)
    gshape = jax.ShapeDtypeStruct((B, S, n_groups * HP * LANES), F32)
    return pl.pallas_call(
        body, name=name, grid=(B, HP, n_groups),
        in_specs=[pl.BlockSpec(memory_space=pltpu.SMEM), grouped,
                  pl.BlockSpec(blk, lambda b, hp, g: (b, 0, g * 2 * HP + hp)),
                  pl.BlockSpec(blk, lambda b, hp, g: (b, 0, g * 2 * HP + HP + hp)),
                  shared, shared, shared] + [grouped] * n_prev,
        out_specs=[grouped] * 3, out_shape=[gshape] * 3,
        scratch_shapes=[pltpu.VMEM((2, 2 * ATT_BLK, 2 * ATT_BLK), F32)],
        compiler_params=_params(3))(slopes, q, kv, kv, o, lse, do, *prev)


def _final_loss(name, h, g, target, tm):
    T, D = h.shape

    def body(h_ref, g_ref, t_ref, loss_ref, dh_ref, dh16_ref, dg_ref):
        hf = h_ref[...]
        gv = g_ref[...]
        rstd = lax.rsqrt(jnp.mean(hf * hf, axis=-1, keepdims=True) + EPS)
        xhat = hf * rstd
        err = xhat * gv - t_ref[...]
        part = 0.5 * jnp.sum(jnp.mean(err * err, axis=-1, keepdims=True), axis=0, keepdims=True)
        dy = err * (1.0 / D)
        dg = jnp.sum(dy * xhat, axis=0, keepdims=True)
        dx = dy * gv
        dh = rstd * (dx - xhat * jnp.mean(dx * xhat, axis=-1, keepdims=True))
        dh_ref[...] = dh
        dh16_ref[...] = dh.astype(BF16)

        @pl.when(pl.program_id(0) == 0)
        def _():
            loss_ref[...] = part
            dg_ref[...] = dg

        @pl.when(pl.program_id(0) > 0)
        def _():
            loss_ref[...] += part
            dg_ref[...] += dg

    return pl.pallas_call(
        body, name=name, grid=(T // tm,),
        in_specs=[pl.BlockSpec((tm, D), lambda i: (i, 0)), pl.BlockSpec((1, D), lambda i: (0, 0)),
                  pl.BlockSpec((tm, D), lambda i: (i, 0))],
        out_specs=[pl.BlockSpec((1, 1), lambda i: (0, 0)), pl.BlockSpec((tm, D), lambda i: (i, 0)),
                   pl.BlockSpec((tm, D), lambda i: (i, 0)), pl.BlockSpec((1, D), lambda i: (0, 0))],
        out_shape=[jax.ShapeDtypeStruct((1, 1), F32), jax.ShapeDtypeStruct((T, D), F32),
                   jax.ShapeDtypeStruct((T, D), BF16), jax.ShapeDtypeStruct((1, D), F32)],
        compiler_params=_params(1))(h, g, target)


def _nt_rows(name, dh, wg, layer, a_mul, out_dtype, tm, deps=()):
    T, D = dh.shape
    rk = wg.shape[2]
    N = N_CHIPS * rk
    with_a = a_mul is not None

    def body(dh_ref, w_ref, *rest):
        o_ref = rest[-1]
        d16 = dh_ref[...]
        for ch in range(N_CHIPS):
            r = _dot_nt(d16, w_ref[ch])
            if with_a:
                r = r * (2.0 * jnp.maximum(rest[0][:, ch * rk:(ch + 1) * rk].astype(F32), 0.0))
            o_ref[:, ch * rk:(ch + 1) * rk] = r.astype(out_dtype)

    in_specs = [pl.BlockSpec((tm, D), lambda i: (i, 0)),
                pl.BlockSpec((N_CHIPS, None, rk, D), lambda i: (0, layer, 0, 0))]
    args = [dh, wg]
    if with_a:
        in_specs.append(pl.BlockSpec((tm, N), lambda i: (i, 0)))
        args.append(a_mul)
    in_specs += [ANY] * len(deps)
    args += list(deps)
    return pl.pallas_call(
        body, name=name, grid=(T // tm,), in_specs=in_specs,
        out_specs=pl.BlockSpec((tm, N), lambda i: (i, 0)),
        out_shape=jax.ShapeDtypeStruct((T, N), out_dtype),
        compiler_params=_params(1))(*args)


def _nt_cols(name, ysegs, wg, layer, tm, norm, deps=()):
    Nw, cw = wg.shape[2], wg.shape[3]
    widths = [bs[-1] for _, bs, _ in ysegs]
    pieces = _pieces(widths, cw, 1024)
    ns = len(ysegs)
    T = norm[0].shape[0] if norm is not None else ysegs[0][0].shape[-2]

    def body(*refs):
        y_refs = refs[:ns]
        w_ref = refs[ns]
        acc = refs[-1]
        for n, (s, a0, ch, b0, wd) in enumerate(pieces):
            d = _dot_nt(y_refs[s][:, a0:a0 + wd].astype(BF16), w_ref[ch, :, b0:b0 + wd])
            if n == 0:
                acc[...] = d
            else:
                acc[...] += d
        if norm is None:
            refs[ns + 1 + len(deps)][...] = acc[...]
        else:
            h_ref, g_ref, dhin_ref = refs[ns + 1:ns + 4]
            out_ref, out16_ref, dg_ref = refs[ns + 4 + len(deps):ns + 7 + len(deps)]
            dh_c, dg = _rms_bwd(h_ref[...], g_ref[...], acc[...])
            dh = dhin_ref[...] + dh_c
            out_ref[...] = dh
            out16_ref[...] = dh.astype(BF16)

            @pl.when(pl.program_id(0) == 0)
            def _():
                dg_ref[...] = dg

            @pl.when(pl.program_id(0) > 0)
            def _():
                dg_ref[...] += dg

    in_specs = [pl.BlockSpec(bs, im) for _, bs, im in ysegs]
    in_specs.append(pl.BlockSpec((N_CHIPS, None, Nw, cw), lambda i: (0, layer, 0, 0)))
    args = [a for a, _, _ in ysegs] + [wg]
    row = pl.BlockSpec((tm, Nw), lambda i: (i, 0))
    vec = pl.BlockSpec((1, Nw), lambda i: (0, 0))
    if norm is None:
        out_specs = row
        out_shape = jax.ShapeDtypeStruct((T, Nw), F32)
    else:
        in_specs += [row, vec, row]
        args += list(norm)
    in_specs += [ANY] * len(deps)
    args += list(deps)
    if norm is not None:
        out_specs = [row, row, vec]
        out_shape = [jax.ShapeDtypeStruct((T, Nw), F32), jax.ShapeDtypeStruct((T, Nw), BF16),
                     jax.ShapeDtypeStruct((1, Nw), F32)]
    return pl.pallas_call(
        body, name=name, grid=(T // tm,), in_specs=in_specs, out_specs=out_specs, out_shape=out_shape,
        scratch_shapes=[pltpu.VMEM((tm, Nw), F32)], compiler_params=_params(1))(*args)


def _tn(name, x, x_act, ysegs, cw, cols_layout, tmm, tt, deps=(), out_dtype=F32):
    T, M = x.shape
    widths = [bs[-1] for _, bs, _ in ysegs]
    N = sum(widths)
    pieces = _pieces(widths, cw if cols_layout else N, 1024)
    ns = len(ysegs)
    n_t = T // tt
    block = (N_CHIPS, tmm, cw) if cols_layout else (tmm, N)
    narrow = out_dtype != F32

    def body(x_ref, *refs):
        y_refs = refs[:ns]
        o_ref = refs[ns + len(deps)]
        acc = refs[-1] if narrow else o_ref

        @pl.when(pl.program_id(1) == 0)
        def _():
            acc[...] = jnp.zeros_like(acc)

        xt = x_act(x_ref[...])
        for s, a0, ch, b0, wd in pieces:
            d = _dot_tn(xt, y_refs[s][:, a0:a0 + wd].astype(BF16))
            if cols_layout:
                acc[ch, :, b0:b0 + wd] += d
            else:
                acc[:, b0:b0 + wd] += d
        if narrow:
            @pl.when(pl.program_id(1) == n_t - 1)
            def _():
                o_ref[...] = acc[...].astype(out_dtype)

    in_specs = [pl.BlockSpec((tt, tmm), lambda m, t: (t, m))] + [pl.BlockSpec(bs, im) for _, bs, im in ysegs]
    in_specs += [ANY] * len(deps)
    if cols_layout:
        out_specs = pl.BlockSpec(block, lambda m, t: (0, m, 0))
        out_shape = jax.ShapeDtypeStruct((N_CHIPS, M, cw), out_dtype)
    else:
        out_specs = pl.BlockSpec(block, lambda m, t: (m, 0))
        out_shape = jax.ShapeDtypeStruct((M, N), out_dtype)
    return pl.pallas_call(
        body, name=name, grid=(M // tmm, n_t), in_specs=in_specs, out_specs=out_specs, out_shape=out_shape,
        scratch_shapes=[pltpu.VMEM(block, F32)] if narrow else [],
        compiler_params=_params(2))(x, *[a for a, _, _ in ysegs], *deps)


def _seg2d(a, t_rows, grid_rank):
    w = a.shape[1]
    if grid_rank == 1:
        return (a, (t_rows, w), lambda i: (i, 0))
    return (a, (t_rows, w), lambda m, t: (t, 0))


def _kv_segments(dk, dv, C, t_rows, grid_rank):
    segs = []
    for g in range(len(PATTERNS)):
        for a in (dk, dv):
            if grid_rank == 1:
                segs.append((a, (t_rows, C), lambda i, g=g: (i, g)))
            else:
                segs.append((a, (t_rows, C), lambda m, t, g=g: (t, g)))
    return segs


def _seg_plane(a, plane, t_rows, grid_rank):
    w = a.shape[2]
    if grid_rank == 1:
        return (a, (None, t_rows, w), lambda i: (plane, i, 0))
    return (a, (None, t_rows, w), lambda m, t: (plane, t, 0))


def _row_tile(rows, row_bytes, budget_bytes=2 * 1024 * 1024):
    t = rows
    while t * row_bytes > budget_bytes and t % 32 == 0:
        t //= 2
    return t


N_DEVICES = 8


def _device_add(name, own, slots, place):
    _, _, hr, c = own.shape
    tr = _row_tile(hr, c * 4, 1024 * 1024)

    def body(place_ref, own_ref, *refs):
        o_ref = refs[-1]
        acc = own_ref[...].astype(F32)
        for r in refs[:-1]:
            acc = acc + r[...].astype(F32)
        o_ref[...] = acc

    def slot(k):
        return pl.BlockSpec((None, tr, c), lambda i, pr: ((2 * pr[0] + pr[1] + k) % N_DEVICES, i, 0))

    grid_spec = pltpu.PrefetchScalarGridSpec(
        num_scalar_prefetch=1, grid=(hr // tr,),
        in_specs=[pl.BlockSpec((None, None, tr, c), lambda i, pr: (pr[0], pr[1], i, 0))]
        + [slot(k) for k in range(1, N_DEVICES)],
        out_specs=pl.BlockSpec((None, tr, c), lambda i, pr: (pr[1], i, 0)))
    return pl.pallas_call(body, name=name, grid_spec=grid_spec,
                          out_shape=jax.ShapeDtypeStruct((2, hr, c), F32),
                          compiler_params=_params(1))(place, own, *[slots] * (N_DEVICES - 1))


def _adamw(name, w, g, m, v):
    rows, cols = w.shape
    tr = _row_tile(rows, cols * 4, 1024 * 1024)

    def body(w_ref, g_ref, m_ref, v_ref, d_ref, nm_ref, nv_ref):
        d_ref[...], nm_ref[...], nv_ref[...] = _adamw_math(w_ref[...], g_ref[...], m_ref[...], v_ref[...])

    spec = pl.BlockSpec((tr, cols), lambda i: (i, 0))
    return pl.pallas_call(
        body, name=name, grid=(rows // tr,), in_specs=[spec] * 4, out_specs=[spec] * 3,
        out_shape=[jax.ShapeDtypeStruct((rows, cols), F32)] * 3, compiler_params=_params(1))(w, g, m, v)


def _adamw_math(w, g, m, v):
    nm = ADAM_B1 * m + (1.0 - ADAM_B1) * g
    nv = ADAM_B2 * v + (1.0 - ADAM_B2) * jnp.square(g)
    m_hat = nm / (1.0 - ADAM_B1 ** ADAM_STEP)
    v_hat = nv / (1.0 - ADAM_B2 ** ADAM_STEP)
    return -ADAM_LR * (m_hat / (jnp.sqrt(v_hat) + ADAM_EPS) + ADAM_WD * w), nm, nv


def _adamw_layers(name, w, grads, m, v):
    L, r, c = w.shape
    tr = _row_tile(r, L * c * 4, 1024 * 1024)

    def body(*refs):
        w_ref, m_ref, v_ref = refs[:3]
        g_refs = refs[3:3 + L]
        go_ref, d_ref, nm_ref, nv_ref = refs[3 + L:]
        for l in range(L):
            g = g_refs[l][...]
            go_ref[l] = g
            d_ref[l], nm_ref[l], nv_ref[l] = _adamw_math(w_ref[l], g, m_ref[l], v_ref[l])

    stacked = pl.BlockSpec((L, tr, c), lambda i: (0, i, 0))
    return pl.pallas_call(
        body, name=name, grid=(r // tr,),
        in_specs=[stacked] * 3 + [pl.BlockSpec((tr, c), lambda i: (i, 0))] * L, out_specs=[stacked] * 4,
        out_shape=[jax.ShapeDtypeStruct((L, r, c), F32)] * 4, compiler_params=_params(1))(w, m, v, *grads)


def _place():
    x, y, c = lax.axis_index("x"), lax.axis_index("y"), lax.axis_index("c")
    chips = [(1 - x, y), (x, 1 - y), (1 - x, 1 - y)]
    return x, y, c, chips


HBM = pl.BlockSpec(memory_space=pltpu.HBM)
SEM = pl.BlockSpec(memory_space=pltpu.SEMAPHORE)
EFFECT = pltpu.SideEffectType.DATAFLOW_SIDE_EFFECTING


class _Copy:
    def __init__(self, src, src_view, land, dst_view, recv_view, target):
        self.src, self.src_view, self.land, self.dst_view, self.recv_view, self.target = (
            src, src_view, land, dst_view, recv_view, target)


def _whole(ref, place):
    return ref


def _split_start(name, srcs, land_shapes, plans, deps=()):
    skeys, lkeys = list(srcs), list(land_shapes)
    ns, nl, ng, nd = len(skeys), len(lkeys), len(plans), len(deps)

    def body(*refs):
        src = dict(zip(skeys, refs[:ns]))
        land = dict(zip(lkeys, refs[ns:ns + nl]))
        sems = refs[ns + nl + nd:ns + nl + nd + 2 * ng]
        token = refs[-1]
        place = _place()
        for gi, plan in enumerate(plans):
            for k, cp in enumerate(plan):
                dst = land[cp.land] if cp.land in land else src[cp.land]
                pltpu.make_async_remote_copy(
                    src_ref=cp.src_view(src[cp.src], place), dst_ref=cp.dst_view(dst, place),
                    send_sem=sems[2 * gi].at[k], recv_sem=sems[2 * gi + 1].at[k],
                    device_id=cp.target(place), device_id_type=MESH).start()
        token[...] = jnp.zeros_like(token)

    sem_shapes = []
    for plan in plans:
        sem_shapes += [pltpu.SemaphoreType.DMA((len(plan),))] * 2
    buffers = [srcs[k] for k in skeys] + [lax.empty(land_shapes[k].shape, land_shapes[k].dtype) for k in lkeys]
    outs = pl.pallas_call(
        body, name=name,
        out_shape=(*sem_shapes, *[pltpu.HBM(a.shape, a.dtype) for a in buffers], jax.ShapeDtypeStruct((8, LANES), F32)),
        in_specs=[HBM] * (ns + nl) + [ANY] * nd,
        out_specs=(*[SEM] * (2 * ng), *[HBM] * (ns + nl), pl.BlockSpec(memory_space=pltpu.VMEM)),
        input_output_aliases={i: 2 * ng + i for i in range(ns + nl)},
        compiler_params=pltpu.CompilerParams(has_side_effects=EFFECT),
    )(*[pltpu.with_memory_space_constraint(a, pltpu.HBM) for a in buffers], *deps)
    sems = [(outs[2 * gi], outs[2 * gi + 1]) for gi in range(ng)]
    thru = outs[2 * ng:2 * ng + ns + nl]
    return sems, dict(zip(skeys, thru[:ns])), dict(zip(lkeys, thru[ns:])), outs[-1]


def _split_wait(name, sems, srcs, lands, plan, after):
    skeys, lkeys = list(srcs), list(lands)
    ns, nl = len(skeys), len(lkeys)

    def body(*refs):
        src = dict(zip(skeys, refs[:ns]))
        land = dict(zip(lkeys, refs[ns:ns + nl]))
        ssem, rsem = refs[ns + nl], refs[ns + nl + 1]
        place = _place()
        for k, cp in enumerate(plan):
            dst = land[cp.land] if cp.land in land else src[cp.land]
            pltpu.make_async_remote_copy(
                src_ref=cp.src_view(src[cp.src], place), dst_ref=cp.dst_view(dst, place),
                send_sem=ssem.at[k], recv_sem=rsem.at[k],
                device_id=cp.target(place), device_id_type=MESH).wait_send()
            got = cp.recv_view(dst, place)
            pltpu.make_async_remote_copy(
                src_ref=got, dst_ref=got, send_sem=ssem.at[k], recv_sem=rsem.at[k],
                device_id=cp.target(place), device_id_type=MESH).wait_recv()

    buffers = [srcs[k] for k in skeys] + [lands[k] for k in lkeys]
    outs = pl.pallas_call(
        body, name=name, out_shape=tuple(pltpu.HBM(a.shape, a.dtype) for a in buffers),
        in_specs=(*[HBM] * (ns + nl), SEM, SEM, ANY), out_specs=tuple([HBM] * (ns + nl)),
        input_output_aliases={i: i for i in range(ns + nl)},
        compiler_params=pltpu.CompilerParams(has_side_effects=EFFECT),
    )(*buffers, sems[0], sems[1], after)
    return dict(zip(skeys, outs[:ns])), dict(zip(lkeys, outs[ns:]))


def _chip_of(place):
    x, y, c, chips = place
    return 2 * x + y


GATHER_FIRST = 2


class _WeightGather:
    def __init__(self, blocks):
        self.plans, shapes = {}, {}
        for key, a in blocks.items():
            shapes[key] = jax.ShapeDtypeStruct((N_CHIPS,) + a.shape, a.dtype)
            slot = lambda ref, place: ref.at[_chip_of(place)]
            plan = [_Copy(key, _whole, key, slot,
                          lambda ref, place, k=k: ref.at[2 * place[3][k][0] + place[3][k][1]],
                          lambda place, k=k: (place[3][k][0], place[3][k][1], place[2])) for k in range(3)]
            plan.append(_Copy(key, _whole, key, slot, slot, lambda place: (place[0], place[1], 1 - place[2])))
            self.plans[key] = plan
        keys = list(blocks)
        first, a = keys[0], blocks[keys[0]]
        hr = a.shape[0] // 2
        mine = lambda ref, place, q: ref.at[q, pl.ds(pl.multiple_of(place[2] * hr, 16), hr)]
        theirs = lambda ref, place, q: ref.at[q, pl.ds(pl.multiple_of((1 - place[2]) * hr, 16), hr)]
        chip = lambda place, k: 2 * place[3][k][0] + place[3][k][1]
        sibling = lambda place: (place[0], place[1], 1 - place[2])
        self.plans[first] = [
            _Copy(first, lambda ref, place: ref.at[pl.ds(pl.multiple_of(place[2] * hr, 16), hr)], first,
                  lambda ref, place: mine(ref, place, _chip_of(place)),
                  lambda ref, place, k=k: mine(ref, place, chip(place, k)),
                  lambda place, k=k: (place[3][k][0], place[3][k][1], place[2])) for k in range(3)]
        self.plans[first].append(_Copy(first, _whole, first, lambda ref, place: ref.at[_chip_of(place)],
                                       lambda ref, place: ref.at[_chip_of(place)], sibling))
        self.forward = [_Copy(first, lambda ref, place, k=k: mine(ref, place, chip(place, k)), first,
                              lambda ref, place, k=k: mine(ref, place, chip(place, k)),
                              lambda ref, place, k=k: theirs(ref, place, chip(place, k)), sibling) for k in range(3)]
        self.blocks, self.shapes = blocks, shapes
        self.sems, self.srcs, self.lands = {}, {}, {}
        self._start("gather_start_first", keys[:GATHER_FIRST], ())
        self.rest = keys[GATHER_FIRST:]

    def _start(self, name, part, deps):
        sems, srcs, lands, self.token = _split_start(name, {k: self.blocks[k] for k in part},
                                                     {k: self.shapes[k] for k in part}, [self.plans[k] for k in part], deps)
        self.sems.update(zip(part, sems))
        self.srcs.update(srcs)
        self.lands.update(lands)

    def get(self, l, name, after):
        key = (l, name)
        _, lands = _split_wait(f"gather_wait_{name}{l}", self.sems[key], {key: self.srcs[key]},
                               {key: self.lands[key]}, self.plans[key], after)
        if self.rest:
            sems, bufs, _, _ = _split_start("gather_forward", {key: lands[key]}, {}, [self.forward])
            lands, _ = _split_wait("gather_forward_wait", sems[0], bufs, {}, self.forward, after)
            self._start("gather_start", self.rest, [lands[key]])
            self.rest = []
        return lands[key][:, None]


class _GradReduce:
    def __init__(self, place):
        self.place = place
        self.jobs = []
        self.done = {}
        self.n = 0

    def submit(self, grads):
        views = {k: a.reshape(N_CHIPS, 2, a.shape[1] // 2, a.shape[2]) for k, a in grads.items()}
        shapes = {k: jax.ShapeDtypeStruct((N_DEVICES,) + a.shape[2:], a.dtype) for k, a in views.items()}

        def peer(place, k):
            x, y, c, _ = place
            return (1 - x if k & 4 else x, 1 - y if k & 2 else y, 1 - c if k & 1 else c)

        def index(dev):
            return 4 * dev[0] + 2 * dev[1] + dev[2]

        plan = []
        for key in views:
            for k in range(1, N_DEVICES):
                plan.append(_Copy(
                    key, lambda ref, place, k=k: ref.at[2 * peer(place, k)[0] + peer(place, k)[1], peer(place, k)[2]],
                    key, lambda ref, place: ref.at[index(place[:3])],
                    lambda ref, place, k=k: ref.at[index(peer(place, k))],
                    lambda place, k=k: peer(place, k)))
        sems, srcs, lands, token = _split_start(f"grad_start{self.n}", views, shapes, [plan])
        self.jobs.append(dict(id=self.n, sems=sems[0], srcs=srcs, lands=lands, plan=plan))
        self.n += 1
        return token

    def pump(self, after):
        return []

    def finish(self, after):
        for job in self.jobs:
            srcs, lands = _split_wait(f"grad_wait{job['id']}", job["sems"], job["srcs"], job["lands"], job["plan"],
                                      after)
            for i, k in enumerate(srcs):
                self.done[k] = _device_add(f"grad_add{job['id']}_{i}", srcs[k], lands[k], self.place)
        self.jobs = []
        return self.done


class _PairShare:
    def __init__(self, halves, types):
        sibling = lambda place: (place[0], place[1], 1 - place[2])
        mine = lambda ref, place: ref.at[place[2]]
        theirs = lambda ref, place: ref.at[1 - place[2]]
        self.plans = {t: [_Copy(k, mine, k, mine, theirs, sibling) for k in halves if k[0] == t] for t in types}
        sems, self.bufs, _, self.token = _split_start("share_start", halves, {}, list(self.plans.values()))
        self.sems = dict(zip(self.plans, sems))

    def get(self, t, after):
        keys = [cp.src for cp in self.plans[t]]
        bufs, _ = _split_wait(f"share_wait_{t}", self.sems[t], {k: self.bufs[k] for k in keys}, {}, self.plans[t], after)
        return bufs


def _small_allreduce(part):
    R, C = part.shape
    N_DEV = 8

    def body(in_ref, out_ref, slots, ssem, rsem):
        x, y, c, _ = _place()
        me = 4 * x + 2 * y + c
        sends = []
        for k in range(1, N_DEV):
            kx, ky, kc = (k >> 2) & 1, (k >> 1) & 1, k & 1
            peer = (1 - x if kx else x, 1 - y if ky else y, 1 - c if kc else c)
            cp = pltpu.make_async_remote_copy(
                src_ref=in_ref, dst_ref=slots.at[me], send_sem=ssem.at[k], recv_sem=rsem.at[k],
                device_id=peer, device_id_type=MESH)
            cp.start()
            sends.append(cp)
        slots[me] = in_ref[...]
        for k in range(1, N_DEV):
            kx, ky, kc = (k >> 2) & 1, (k >> 1) & 1, k & 1
            peer = (1 - x if kx else x, 1 - y if ky else y, 1 - c if kc else c)
            slot = slots.at[4 * peer[0] + 2 * peer[1] + peer[2]]
            pltpu.make_async_remote_copy(
                src_ref=slot, dst_ref=slot, send_sem=ssem.at[k], recv_sem=rsem.at[k],
                device_id=peer, device_id_type=MESH).wait_recv()
        acc = slots[0]
        for d in range(1, N_DEV):
            acc = acc + slots[d]
        out_ref[...] = acc
        for cp in sends:
            cp.wait_send()

    vm = pl.BlockSpec(memory_space=pltpu.VMEM)
    return pl.pallas_call(
        body, name="small_allreduce", in_specs=[vm], out_specs=vm,
        out_shape=jax.ShapeDtypeStruct((R, C), F32),
        scratch_shapes=[pltpu.VMEM((N_DEV, R, C), F32), pltpu.SemaphoreType.DMA((N_DEV,)),
                        pltpu.SemaphoreType.DMA((N_DEV,))])(part)


def _local_step(x, target, norm_mix, norm_mlp, norm_kv, norm_final, weights, sink, n_a, n_heads):
    B, S, D = x.shape
    T = B * S
    C = n_heads * HEAD_DIM
    depth = norm_mix.shape[0]
    slopes = 2.0 ** (-ALIBI_MAX_BIAS * jnp.arange(1, n_heads + 1, dtype=F32) / n_heads)
    tm = min(512, T)
    row = lambda v: v.reshape(1, -1)

    h = x.reshape(T, D)
    saved, Wl = [], []
    kv = nkv = h_kv = cwg = None
    for l in range(depth):
        s = {"h_in": h}
        w = {}
        Wl.append(w)
        if l < n_a:
            n_in = _rms_only("a_in_norm0", h, row(norm_mix[l]), tm) if l == 0 else h
            w["w_a_in"] = weights.get(l, "w_a_in", n_in)
            first = [weights.token] if l == 0 else []
            s["n1"], bcu = _norm_mm(f"a_in_fwd{l}", n_in, row(norm_mix[l]), w["w_a_in"], 0, 3, BF16, tm, first, l == 0)
            s["bcu"] = bcu.reshape(3, B, S, D)
            if l == 0:
                cwg = weights.get(0, "conv", bcu)[:, 0, :n_a * 3].reshape(N_CHIPS, n_a, 3, -1)
            s["z"] = _conv_fwd(f"conv_fwd{l}", s["bcu"], cwg, l, CONV_COLS).reshape(T, D)
            w["w_a_out"] = weights.get(l, "w_a_out", s["z"])
            h = _mm_res_rows(f"a_out_fwd{l}", s["z"], w["w_a_out"], 0, h, _to_bf16, tm)
        else:
            i = l - n_a
            if i == 0:
                h_kv = h
                w["w_kv"] = weights.get(l, "w_kv", h)
                nkv, kv = _norm_mm("kv_fwd", h, row(norm_kv), w["w_kv"], 0, 1, F32, tm)
                kv = kv.reshape(B, S, 2 * 3 * C)
            w["w_q"] = weights.get(l, "w_q", h)
            s["n1"], q = _norm_mm(f"q_fwd{i}", h, row(norm_mix[l]), w["w_q"], 0, 1, F32, tm)
            s["q"] = q.reshape(B, S, 3 * C)
            o, lse = _attn_fwd(f"attn_fwd{i}", s["q"], kv, slopes, n_heads)
            s["o"], s["lse"] = o.reshape(T, C), lse.reshape(T, C)
            w["w_o"] = weights.get(l, "w_o", o)
            h = _mm_res_cols(f"o_fwd{i}", s["o"], w["w_o"], 0, h, tm)
        s["h_mid"] = h
        w["w_up"] = weights.get(l, "w_up", h)
        if l < n_a:
            s["n2"], a = _norm_mm(f"up_fwd{l}", h, row(norm_mlp[l]), w["w_up"], 0, 1, BF16, tm)
            s["a"] = a[0]
            w["w_down"] = weights.get(l, "w_down", a)
            h = _mm_res_rows(f"down_fwd{l}", s["a"], w["w_down"], 0, h, _relu2_bf16, tm)
        else:
            w["w_down"] = weights.get(l, "w_down", h)
            s["n2"], s["a"], h = _mlp_fwd(f"mlp_fwd{l}", h, row(norm_mlp[l]), w["w_up"], w["w_down"], tm)
        F = s["a"].shape[1]
        saved.append(s)

    loss, dh, dh16, dg_final = _final_loss("loss_head", h, row(norm_final), target.reshape(T, D), tm)

    g_mix, g_mlp = [None] * depth, [None] * depth
    g_conv = [None] * n_a
    dkv = None
    tt = min(512, T)
    deps = []
    for l in reversed(range(depth)):
        s, w = saved[l], Wl[l]
        g_down = _tn(f"down_wgrad{l}", s["a"], _relu2_bf16, [_seg2d(dh16, tt, 2)], None, False,
                     min(2048, F), tt, deps, BF16).reshape(N_CHIPS, F // N_CHIPS, D)
        da, dh, dh16, g_mlp[l] = _mlp_bwd(f"mlp_bwd{l}", dh, dh16, s["a"], w["w_down"], w["w_up"], s["h_mid"],
                                          row(norm_mlp[l]), tm)
        g_up = _tn(f"up_wgrad{l}", s["n2"], _to_bf16, [_seg2d(da, tt, 2)], F // N_CHIPS, True, D, tt, (), BF16)
        deps = sink.pump(dh) + [sink.submit({("w_up", l): g_up, ("w_down", l): g_down})]
        if l < n_a:
            g_out = _tn(f"a_out_wgrad{l}", s["z"], _to_bf16, [_seg2d(dh16, tt, 2)], None, False,
                        D, tt, deps, BF16).reshape(N_CHIPS, D // N_CHIPS, D)
            dz = _nt_rows(f"a_out_bwd{l}", dh16, w["w_a_out"], 0, None, F32, tm)
            deps = sink.pump(dz) + [sink.submit({("w_a_out", l): g_out})]
            dbcu, g_conv[l] = _conv_bwd(f"conv_bwd{l}", s["bcu"], dz.reshape(B, S, D), cwg, l, CONV_COLS)
            dbcu = dbcu.reshape(3, T, D)
            g_in = _tn(f"a_in_wgrad{l}", s["n1"], _to_bf16, [_seg_plane(dbcu, p, tt, 2) for p in range(3)],
                       3 * D // N_CHIPS, True, D, tt, deps, BF16)
            deps = [sink.submit({("w_a_in", l): g_in})]
            dh, dh16, g_mix[l] = _nt_cols(f"a_in_bwd{l}", [_seg_plane(dbcu, p, tm, 1) for p in range(3)],
                                          w["w_a_in"], 0, tm, (s["h_in"], row(norm_mix[l]), dh), deps)
        else:
            i = l - n_a
            g_o = _tn(f"o_wgrad{i}", s["o"], _to_bf16, [_seg2d(dh16, tt, 2)], D // N_CHIPS, True, C, tt, deps,
                      BF16)
            do = _nt_cols(f"o_bwd{i}", [_seg2d(dh16, tm, 1)], w["w_o"], 0, tm, None)
            deps = sink.pump(do) + [sink.submit({("w_o", i): g_o})]
            dq, dk, dv = _attn_bwd(f"attn_bwd{i}", s["q"], kv, slopes, s["o"].reshape(B, S, C),
                                   s["lse"].reshape(B, S, C), do.reshape(B, S, C), n_heads, dkv)
            dkv = (dk, dv)
            dq = dq.reshape(T, 3 * C)
            g_q = _tn(f"q_wgrad{i}", s["n1"], _to_bf16, [_seg2d(dq, tt, 2)], 3 * C // N_CHIPS, True, D, tt, deps,
                      BF16)
            mixer = {("w_q", i): g_q}
            if i == 0:
                dk2, dv2 = (t.reshape(T, 3 * C) for t in dkv)
                mixer[("w_kv", 0)] = _tn("kv_wgrad", nkv, _to_bf16, _kv_segments(dk2, dv2, C, tt, 2),
                                         6 * C // N_CHIPS, True, D, tt, (), BF16)
            deps = [sink.submit(mixer)]
            dh, dh16, g_mix[l] = _nt_cols(f"q_bwd{i}", [_seg2d(dq, tm, 1)], w["w_q"], 0, tm,
                                          (s["h_in"], row(norm_mix[l]), dh), deps)
            if i == 0:
                dh, dh16, g_kv = _nt_cols("kv_bwd", _kv_segments(dk2, dv2, C, tm, 1), w["w_kv"], 0, tm,
                                          (h_kv, row(norm_kv), dh))
        deps = sink.pump(dh)
    small = dict(norm_mix=jnp.concatenate(g_mix, axis=0), norm_mlp=jnp.concatenate(g_mlp, axis=0),
                 norm_kv=g_kv, norm_final=dg_final, conv_w=jnp.stack(g_conv))
    return loss, dh.reshape(B, S, D), small


BIG = ("w_a_in", "w_a_out", "w_kv", "w_q", "w_o", "w_up", "w_down")
CONV_PAD_ROWS = 16


def kernel(x, norm_mix, norm_mlp, w_a_in, conv_w, w_a_out, norm_kv, w_kv, w_q, w_o, w_up, w_down, norm_final, loss_target, m_norm_mix, m_norm_mlp, m_w_a_in, m_conv_w, m_w_a_out, m_norm_kv, m_w_kv, m_w_q, m_w_o, m_w_up, m_w_down, m_norm_final, v_norm_mix, v_norm_mlp, v_w_a_in, v_conv_w, v_w_a_out, v_norm_kv, v_w_kv, v_w_q, v_w_o, v_w_up, v_w_down, v_norm_final):
    D = x.shape[-1]
    w = dict(norm_mix=norm_mix, norm_mlp=norm_mlp, w_a_in=w_a_in, conv_w=conv_w, w_a_out=w_a_out, norm_kv=norm_kv,
             w_kv=w_kv[None], w_q=w_q, w_o=w_o, w_up=w_up, w_down=w_down, norm_final=norm_final)
    m = dict(norm_mix=m_norm_mix, norm_mlp=m_norm_mlp, w_a_in=m_w_a_in, conv_w=m_conv_w, w_a_out=m_w_a_out,
             norm_kv=m_norm_kv, w_kv=m_w_kv[None], w_q=m_w_q, w_o=m_w_o, w_up=m_w_up, w_down=m_w_down,
             norm_final=m_norm_final)
    v = dict(norm_mix=v_norm_mix, norm_mlp=v_norm_mlp, w_a_in=v_w_a_in, conv_w=v_conv_w, w_a_out=v_w_a_out,
             norm_kv=v_norm_kv, w_kv=v_w_kv[None], w_q=v_w_q, w_o=v_w_o, w_up=v_w_up, w_down=v_w_down,
             norm_final=v_norm_final)
    depth = norm_mix.shape[0]
    n_a, taps, cwc = conv_w.shape
    n_heads = w_o.shape[1] // HEAD_DIM

    conv_rows = jnp.zeros((CONV_PAD_ROWS, cwc), F32).at[:n_a * taps].set(conv_w.reshape(n_a * taps, cwc))
    blocks = {}
    for l in range(depth):
        if l < n_a:
            blocks[(l, "w_a_in")] = w_a_in[l].astype(BF16)
            if l == 0:
                blocks[(0, "conv")] = conv_rows
            blocks[(l, "w_a_out")] = w_a_out[l].astype(BF16)
        else:
            if l == n_a:
                blocks[(l, "w_kv")] = w_kv.astype(BF16)
            blocks[(l, "w_q")] = w_q[l - n_a].astype(BF16)
            blocks[(l, "w_o")] = w_o[l - n_a].astype(BF16)
        blocks[(l, "w_up")] = w_up[l].astype(BF16)
        blocks[(l, "w_down")] = w_down[l].astype(BF16)
    weights = _WeightGather(blocks)
    place = jnp.stack([2 * lax.axis_index("x") + lax.axis_index("y"), lax.axis_index("c")]).astype(jnp.int32)
    sink = _GradReduce(place)

    loss, grad_x, small = _local_step(x, loss_target, norm_mix, norm_mlp, norm_kv, norm_final, weights, sink,
                                      n_a, n_heads)
    loss = lax.psum(loss[0, 0], ("x", "y", "c"))

    share = _PairShare(sink.finish(grad_x), BIG)
    grads = {}

    packed = jnp.concatenate([small["norm_mix"], small["norm_mlp"], small["norm_kv"], small["norm_final"],
                              small["conv_w"].reshape(n_a * taps, D)], axis=0)
    pad = (-packed.shape[0]) % 8
    packed = jnp.pad(packed, ((0, pad), (0, 0)))
    total = _small_allreduce(packed)
    grads["norm_mix"] = total[:depth]
    grads["norm_mlp"] = total[depth:2 * depth]
    grads["norm_kv"] = total[2 * depth]
    grads["norm_final"] = total[2 * depth + 1]
    chip = 2 * lax.axis_index("x") + lax.axis_index("y")
    conv_full = total[2 * depth + 2:2 * depth + 2 + n_a * taps].reshape(n_a, taps, N_CHIPS, cwc)
    grads["conv_w"] = lax.dynamic_index_in_dim(conv_full, chip, axis=2, keepdims=False)

    order = ("norm_mix", "norm_mlp", "w_a_in", "conv_w", "w_a_out", "norm_kv", "w_kv", "w_q", "w_o", "w_up",
             "w_down", "norm_final")
    delta, new_m, new_v = {}, {}, {}
    vec_names = ("norm_mix", "norm_mlp", "norm_kv", "norm_final")
    rows_of = lambda a: a.reshape(-1, D)
    vw, vg, vm_, vv = (jnp.concatenate([rows_of(t[k]) for k in vec_names], axis=0) for t in (w, grads, m, v))
    vpad = (-vw.shape[0]) % 8
    padrows = lambda a: jnp.pad(a, ((0, vpad), (0, 0)))
    vd, vnm, vnv = _adamw("adamw_norms", padrows(vw), padrows(vg), padrows(vm_), padrows(vv))
    off = 0
    for k in vec_names:
        r = rows_of(w[k]).shape[0]
        delta[k] = vd[off:off + r].reshape(w[k].shape)
        new_m[k] = vnm[off:off + r].reshape(w[k].shape)
        new_v[k] = vnv[off:off + r].reshape(w[k].shape)
        off += r
    cpad = (-n_a * taps) % 8
    two_d = lambda a: jnp.pad(a.reshape(-1, cwc), ((0, cpad), (0, 0)))
    cd, cnm, cnv = _adamw("adamw_conv_w", two_d(w["conv_w"]), two_d(grads["conv_w"]), two_d(m["conv_w"]),
                          two_d(v["conv_w"]))
    delta["conv_w"], new_m["conv_w"], new_v["conv_w"] = (t[:n_a * taps].reshape(conv_w.shape) for t in (cd, cnm, cnv))
    after = cd
    for k in sorted(BIG, key=lambda k: w[k].size):
        shared = share.get(k, after)
        per_layer = [shared[(k, l)].reshape(w[k].shape[1:]) for l in range(w[k].shape[0])]
        grads[k], delta[k], new_m[k], new_v[k] = _adamw_layers(f"adamw_{k}", w[k], per_layer, m[k], v[k])
        after = delta[k]
    fix = lambda k, a: a[0] if k == "w_kv" else a
    return (loss, grad_x, *[fix(k, grads[k]) for k in order], *[fix(k, delta[k]) for k in order],
            *[fix(k, new_m[k]) for k in order], *[fix(k, new_v[k]) for k in order])
```

```python
import jax
import jax.numpy as jnp
from jax import lax
from jax.experimental import pallas as pl
from jax.experimental.pallas import tpu as pltpu

F32 = jnp.float32
BF16 = jnp.bfloat16
MESH = pl.DeviceIdType.MESH

EPS = 1e-5
PATTERNS = ((128, 1), (512, 4), (2048, 16))
HEAD_DIM = 64
ALIBI_MAX_BIAS = 8.0
NEG_INF = -1e30
ATT_BLK = 128
BWD_UNROLL = 16
N_CHIPS = 4
LANES = 128
VMEM_LIMIT = 56 * 1024 * 1024

ADAM_LR = 0.001
ADAM_B1 = 0.9
ADAM_B2 = 0.999
ADAM_EPS = 1e-08
ADAM_WD = 0.01
ADAM_STEP = 10


ANY = pl.BlockSpec(memory_space=pl.ANY)


def _params(n_grid_axes):
    return pltpu.CompilerParams(dimension_semantics=("arbitrary",) * n_grid_axes, vmem_limit_bytes=VMEM_LIMIT)


def _dot(a, b):
    return jnp.dot(a, b, preferred_element_type=F32)


def _dot_nt(a, b):
    return lax.dot_general(a, b, (((1,), (1,)), ((), ())), preferred_element_type=F32)


def _dot_tn(a, b):
    return lax.dot_general(a, b, (((0,), (0,)), ((), ())), preferred_element_type=F32)


def _relu2(a):
    return jnp.square(jnp.maximum(a, 0.0))


def _rms(hf, g):
    y = hf * lax.rsqrt(jnp.mean(hf * hf, axis=-1, keepdims=True) + EPS)
    return y * g


def _rms_bwd(hf, g, dn):
    rstd = lax.rsqrt(jnp.mean(hf * hf, axis=-1, keepdims=True) + EPS)
    xhat = hf * rstd
    dg = jnp.sum(dn * xhat, axis=0, keepdims=True)
    dx = dn * g
    dh = rstd * (dx - xhat * jnp.mean(dx * xhat, axis=-1, keepdims=True))
    return dh, dg


def _pieces(seg_widths, chunk_width, max_width):
    total = sum(seg_widths)
    cuts = {0, total}
    acc = 0
    for w in seg_widths:
        cuts.add(acc)
        acc += w
    cuts.update(range(0, total, chunk_width))
    cuts = sorted(cuts)
    fine = []
    for lo, hi in zip(cuts[:-1], cuts[1:]):
        while hi - lo > max_width:
            fine.append((lo, lo + max_width))
            lo += max_width
        fine.append((lo, hi))
    out = []
    for lo, hi in fine:
        acc = 0
        for s, w in enumerate(seg_widths):
            if lo < acc + w:
                break
            acc += w
        out.append((s, lo - acc, lo // chunk_width, lo % chunk_width, hi - lo))
    return out


def _relu2_bf16(a):
    return _relu2(a.astype(F32)).astype(BF16)


def _to_bf16(a):
    return a.astype(BF16)


def _rms_only(name, h, g, tm):
    T, D = h.shape

    def body(h_ref, g_ref, n_ref):
        n_ref[...] = _rms(h_ref[...], g_ref[...]).astype(BF16)

    row = pl.BlockSpec((tm, D), lambda i: (i, 0))
    return pl.pallas_call(
        body, name=name, grid=(T // tm,), in_specs=[row, pl.BlockSpec((1, D), lambda i: (0, 0))], out_specs=row,
        out_shape=jax.ShapeDtypeStruct((T, D), BF16), compiler_params=_params(1))(h, g)


def _norm_mm(name, h, g, wg, layer, planes, out_dtype, tm, deps=(), normed=False):
    T, D = h.shape
    cw = wg.shape[3]
    N = N_CHIPS * cw
    pw = N // planes
    pieces = _pieces([pw] * planes, cw, 512)

    def body(h_ref, g_ref, w_ref, *rest):
        n_ref, o_ref = rest[len(deps):]
        n = h_ref[...] if normed else _rms(h_ref[...], g_ref[...]).astype(BF16)
        n_ref[...] = n
        for s, a0, ch, b0, wd in pieces:
            o_ref[s, :, a0:a0 + wd] = _dot(n, w_ref[ch, :, b0:b0 + wd]).astype(out_dtype)

    return pl.pallas_call(
        body, name=name, grid=(T // tm,),
        in_specs=[pl.BlockSpec((tm, D), lambda i: (i, 0)),
                  pl.BlockSpec((1, D), lambda i: (0, 0)),
                  pl.BlockSpec((N_CHIPS, None, D, cw), lambda i: (0, layer, 0, 0))] + [ANY] * len(deps),
        out_specs=[pl.BlockSpec((tm, D), lambda i: (i, 0)),
                   pl.BlockSpec((planes, tm, pw), lambda i: (0, i, 0))],
        out_shape=[jax.ShapeDtypeStruct((T, D), BF16), jax.ShapeDtypeStruct((planes, T, pw), out_dtype)],
        compiler_params=_params(1))(h, g, wg, *deps)


def _resident(shape, index_map):
    return pl.BlockSpec(shape, index_map, pipeline_mode=pl.Buffered(1))


def _mm_res_rows(name, a, wg, layer, h, act, tm):
    T = a.shape[0]
    rk, D = wg.shape[2], wg.shape[3]

    def body(a_ref, w_ref, h_ref, o_ref):
        acc = h_ref[...]
        for k in range(N_CHIPS):
            acc = acc + _dot(act(a_ref[:, k * rk:(k + 1) * rk]), w_ref[k])
        o_ref[...] = acc

    return pl.pallas_call(
        body, name=name, grid=(T // tm,),
        in_specs=[pl.BlockSpec((tm, N_CHIPS * rk), lambda i: (i, 0)),
                  pl.BlockSpec((N_CHIPS, None, rk, D), lambda i: (0, layer, 0, 0)),
                  pl.BlockSpec((tm, D), lambda i: (i, 0))],
        out_specs=pl.BlockSpec((tm, D), lambda i: (i, 0)),
        out_shape=jax.ShapeDtypeStruct((T, D), F32),
        compiler_params=_params(1))(a, wg, h)


def _mm_res_cols(name, a, wg, layer, h, tm):
    T, K = a.shape
    cw = wg.shape[3]
    D = N_CHIPS * cw

    def body(a_ref, w_ref, h_ref, o_ref):
        a16 = a_ref[...].astype(BF16)
        for j in range(N_CHIPS):
            o_ref[:, j * cw:(j + 1) * cw] = h_ref[:, j * cw:(j + 1) * cw] + _dot(a16, w_ref[j])

    return pl.pallas_call(
        body, name=name, grid=(T // tm,),
        in_specs=[pl.BlockSpec((tm, K), lambda i: (i, 0)),
                  pl.BlockSpec((N_CHIPS, None, K, cw), lambda i: (0, layer, 0, 0)),
                  pl.BlockSpec((tm, D), lambda i: (i, 0))],
        out_specs=pl.BlockSpec((tm, D), lambda i: (i, 0)),
        out_shape=jax.ShapeDtypeStruct((T, D), F32),
        compiler_params=_params(1))(a, wg, h)


def _mlp_fwd(name, h, g, wup, wdown, tm):
    T, D = h.shape
    cw = wup.shape[3]

    def body(h_ref, g_ref, wu_ref, wd_ref, n_ref, a_ref, o_ref):
        hf = h_ref[...]
        n = _rms(hf, g_ref[...]).astype(BF16)
        n_ref[...] = n
        acc = hf
        for ch in range(N_CHIPS):
            a16 = _dot(n, wu_ref[ch]).astype(BF16)
            a_ref[:, ch * cw:(ch + 1) * cw] = a16
            acc = acc + _dot(_relu2_bf16(a16), wd_ref[ch])
        o_ref[...] = acc

    row = pl.BlockSpec((tm, D), lambda i: (i, 0))
    return pl.pallas_call(
        body, name=name, grid=(T // tm,),
        in_specs=[row, pl.BlockSpec((1, D), lambda i: (0, 0)),
                  _resident((N_CHIPS, None, D, cw), lambda i: (0, 0, 0, 0)),
                  _resident((N_CHIPS, None, cw, D), lambda i: (0, 0, 0, 0))],
        out_specs=[row, pl.BlockSpec((tm, N_CHIPS * cw), lambda i: (i, 0)), row],
        out_shape=[jax.ShapeDtypeStruct((T, D), BF16), jax.ShapeDtypeStruct((T, N_CHIPS * cw), BF16),
                   jax.ShapeDtypeStruct((T, D), F32)],
        compiler_params=_params(1))(h, g, wup, wdown)


def _mlp_bwd(name, dh, dh16, a, wdown, wup, h_mid, g, tm, deps=()):
    T, D = dh.shape
    cw = wup.shape[3]
    F = N_CHIPS * cw

    def body(dh_ref, dh16_ref, a_ref, wd_ref, wu_ref, h_ref, g_ref, *rest):
        da_ref, out_ref, out16_ref, dg_ref = rest[len(deps):]
        d16 = dh16_ref[...]
        acc = None
        for ch in range(N_CHIPS):
            cols = slice(ch * cw, (ch + 1) * cw)
            da = (_dot_nt(d16, wd_ref[ch]) * (2.0 * jnp.maximum(a_ref[:, cols].astype(F32), 0.0))).astype(BF16)
            da_ref[:, cols] = da
            d = _dot_nt(da, wu_ref[ch])
            acc = d if acc is None else acc + d
        dh_c, dg = _rms_bwd(h_ref[...], g_ref[...], acc)
        out = dh_ref[...] + dh_c
        out_ref[...] = out
        out16_ref[...] = out.astype(BF16)

        @pl.when(pl.program_id(0) == 0)
        def _():
            dg_ref[...] = dg

        @pl.when(pl.program_id(0) > 0)
        def _():
            dg_ref[...] += dg

    row = pl.BlockSpec((tm, D), lambda i: (i, 0))
    wide = pl.BlockSpec((tm, F), lambda i: (i, 0))
    vec = pl.BlockSpec((1, D), lambda i: (0, 0))
    return pl.pallas_call(
        body, name=name, grid=(T // tm,),
        in_specs=[row, row, wide, _resident((N_CHIPS, None, cw, D), lambda i: (0, 0, 0, 0)),
                  _resident((N_CHIPS, None, D, cw), lambda i: (0, 0, 0, 0)), row, vec] + [ANY] * len(deps),
        out_specs=[wide, row, row, vec],
        out_shape=[jax.ShapeDtypeStruct((T, F), BF16), jax.ShapeDtypeStruct((T, D), F32),
                   jax.ShapeDtypeStruct((T, D), BF16), jax.ShapeDtypeStruct((1, D), F32)],
        compiler_params=_params(1))(dh, dh16, a, wdown, wup, h_mid, g, *deps)


CONV_ROWS = 256
CONV_HALO = 16
CONV_COLS = 2 * LANES


def _conv_shifted(ext, k, r0, rows, at_start):
    rolled = pltpu.roll(ext, k, 0)[CONV_HALO:]
    if not at_start:
        return rolled
    t = r0 + lax.broadcasted_iota(jnp.int32, rolled.shape, 0)
    return jnp.where(t >= k, rolled, 0.0)


def _conv_ahead(ext, k, r0, rows, S, at_end):
    rolled = pltpu.roll(ext, rows + CONV_HALO - k, 0)[:rows]
    if not at_end:
        return rolled
    t = r0 + lax.broadcasted_iota(jnp.int32, rolled.shape, 0)
    return jnp.where(t + k < S, rolled, 0.0)


def _conv_chunks(step, n, carry):
    carry = step(0, carry, True, n == 1)
    if n > 2:
        carry = lax.fori_loop(1, n - 1, lambda i, c: step(i, c, False, False), carry)
    if n > 1:
        carry = step(n - 1, carry, False, True)
    return carry


def _conv_fwd(name, bcu, cwg, layer, tc):
    _, B, S, D = bcu.shape
    cwc = cwg.shape[3]
    per_chunk = cwc // tc
    R = min(CONV_ROWS, S)

    def body(x_ref, w_ref, z_ref):
        w = [w_ref[k:k + 1, :] for k in range(3)]

        def step(i, carry, at_start, at_end):
            r0 = pl.multiple_of(i * R, R)
            h0 = pl.multiple_of(jnp.maximum(r0 - CONV_HALO, 0), CONV_HALO)
            ld = lambda p, start, rows: x_ref[p, pl.ds(start, rows), :].astype(F32)
            cu = jnp.concatenate([ld(1, h0, CONV_HALO) * ld(2, h0, CONV_HALO), ld(1, r0, R) * ld(2, r0, R)], axis=0)
            conv = w[0] * cu[CONV_HALO:]
            conv = conv + w[1] * _conv_shifted(cu, 1, r0, R, at_start)
            conv = conv + w[2] * _conv_shifted(cu, 2, r0, R, at_start)
            z_ref[pl.ds(r0, R), :] = (ld(0, r0, R) * conv).astype(BF16)
            return carry

        _conv_chunks(step, S // R, 0)

    return pl.pallas_call(
        body, name=name, grid=(B, D // tc),
        in_specs=[pl.BlockSpec((3, None, S, tc), lambda b, j: (0, b, 0, j)),
                  pl.BlockSpec((None, None, 3, tc), lambda b, j: (j // per_chunk, layer, 0, j % per_chunk))],
        out_specs=pl.BlockSpec((None, S, tc), lambda b, j: (b, 0, j)),
        out_shape=jax.ShapeDtypeStruct((B, S, D), BF16),
        compiler_params=_params(2))(bcu, cwg)


def _conv_bwd(name, bcu, dz, cwg, layer, tc):
    _, B, S, D = bcu.shape
    cwc = cwg.shape[3]
    per_chunk = cwc // tc
    R = min(CONV_ROWS, S)

    def body(x_ref, dz_ref, w_ref, d_ref, dw_ref):
        w = [w_ref[k:k + 1, :] for k in range(3)]

        @pl.when(pl.program_id(1) == 0)
        def _():
            dw_ref[...] = jnp.zeros_like(dw_ref)

        def step(i, carry, at_start, at_end):
            r0 = pl.multiple_of(i * R, R)
            h0 = pl.multiple_of(jnp.maximum(r0 - CONV_HALO, 0), CONV_HALO)
            a0 = pl.multiple_of(jnp.minimum(r0 + R, S - CONV_HALO), CONV_HALO)
            ld = lambda p, start, rows: x_ref[p, pl.ds(start, rows), :].astype(F32)
            b, c, u = ld(0, r0, R), ld(1, r0, R), ld(2, r0, R)
            dz = dz_ref[pl.ds(r0, R), :]
            cu = jnp.concatenate([ld(1, h0, CONV_HALO) * ld(2, h0, CONV_HALO), c * u], axis=0)
            cu1 = _conv_shifted(cu, 1, r0, R, at_start)
            cu2 = _conv_shifted(cu, 2, r0, R, at_start)
            conv = w[0] * (c * u) + w[1] * cu1 + w[2] * cu2
            dconv = dz * b
            dca = jnp.concatenate([dconv, dz_ref[pl.ds(a0, CONV_HALO), :] * ld(0, a0, CONV_HALO)], axis=0)
            dcu = (w[0] * dconv + w[1] * _conv_ahead(dca, 1, r0, R, S, at_end)
                   + w[2] * _conv_ahead(dca, 2, r0, R, S, at_end))
            d_ref[0, pl.ds(r0, R), :] = (dz * conv).astype(BF16)
            d_ref[1, pl.ds(r0, R), :] = (dcu * u).astype(BF16)
            d_ref[2, pl.ds(r0, R), :] = (dcu * c).astype(BF16)
            return (carry[0] + jnp.sum(dconv * (c * u), axis=0, keepdims=True),
                    carry[1] + jnp.sum(dconv * cu1, axis=0, keepdims=True),
                    carry[2] + jnp.sum(dconv * cu2, axis=0, keepdims=True))

        zero = jnp.zeros((1, tc), F32)
        s0, s1, s2 = _conv_chunks(step, S // R, (zero, zero, zero))
        for k, sk in enumerate((s0, s1, s2)):
            dw_ref[k:k + 1, :] += sk

    return pl.pallas_call(
        body, name=name, grid=(D // tc, B),
        in_specs=[pl.BlockSpec((3, None, S, tc), lambda j, b: (0, b, 0, j)),
                  pl.BlockSpec((None, S, tc), lambda j, b: (b, 0, j)),
                  pl.BlockSpec((None, None, 3, tc), lambda j, b: (j // per_chunk, layer, 0, j % per_chunk))],
        out_specs=[pl.BlockSpec((3, None, S, tc), lambda j, b: (0, b, 0, j)),
                   pl.BlockSpec((3, tc), lambda j, b: (0, j))],
        out_shape=[jax.ShapeDtypeStruct((3, B, S, D), BF16), jax.ShapeDtypeStruct((3, D), F32)],
        compiler_params=_params(2))(bcu, dz, cwg)


def _att_rows(dil, idx, nb):
    r, n = idx // nb, idx % nb
    if dil == 1:
        cur = pl.ds(pl.multiple_of(n * ATT_BLK, ATT_BLK), ATT_BLK)
        prev = pl.ds(pl.multiple_of(jnp.maximum(n - 1, 0) * ATT_BLK, ATT_BLK), ATT_BLK)
    else:
        cur = pl.ds(n * (ATT_BLK * dil) + r, ATT_BLK, stride=dil)
        prev = pl.ds(jnp.maximum(n - 1, 0) * (ATT_BLK * dil) + r, ATT_BLK, stride=dil)
    return n, cur, prev


def _att_bias(bias_ref, dil, sl_ref, hp):
    row = lax.broadcasted_iota(jnp.int32, (2 * ATT_BLK, 2 * ATT_BLK), 0)
    ci = lax.broadcasted_iota(jnp.int32, (2 * ATT_BLK, 2 * ATT_BLK), 1)
    j = ATT_BLK + (row & (ATT_BLK - 1)) - ci
    slope = jnp.where(row < ATT_BLK, sl_ref[2 * hp], sl_ref[2 * hp + 1])
    rest = jnp.where((j >= 0) & (j <= ATT_BLK), -slope * (dil * j).astype(F32), NEG_INF)
    bias_ref[1] = rest
    bias_ref[0] = jnp.where(ci >= ATT_BLK, rest, NEG_INF)


def _stack_heads(x16, lane):
    first = lane < HEAD_DIM
    return jnp.concatenate([jnp.where(first, x16, jnp.zeros_like(x16)),
                            jnp.where(first, jnp.zeros_like(x16), x16)], axis=0)


def _per_head(col, lane):
    return jnp.where(lane < HEAD_DIM, col[:ATT_BLK], col[ATT_BLK:])


def _attn_fwd(name, q, kv, slopes, n_heads):
    B, S, CQ = q.shape
    HP = n_heads * HEAD_DIM // LANES
    scale = HEAD_DIM ** -0.5
    n_groups = len(PATTERNS)
    CH = 256

    def body(sl_ref, q_ref, k_ref, v_ref, o_ref, lse_ref, bias_ref, *parts):
        og, lg = parts[:n_groups], parts[n_groups:]
        hp, g = pl.program_id(1), pl.program_id(2)
        lane = lax.broadcasted_iota(jnp.int32, (1, LANES), 1)

        for gi, (window, dil) in enumerate(PATTERNS):
            nb = S // dil // ATT_BLK

            @pl.when(g == gi)
            def _(gi=gi, dil=dil, nb=nb):
                _att_bias(bias_ref, dil, sl_ref, hp)

                def step(idx, carry):
                    n, cur, prev = _att_rows(dil, idx, nb)
                    qs = _stack_heads((q_ref[cur, :] * scale).astype(BF16), lane)
                    kc = jnp.concatenate([k_ref[prev, :], k_ref[cur, :]], axis=0).astype(BF16)
                    vc = jnp.concatenate([v_ref[prev, :], v_ref[cur, :]], axis=0).astype(BF16)
                    s = _dot_nt(qs, kc) + bias_ref[jnp.minimum(n, 1)]
                    m = jnp.max(s, axis=-1, keepdims=True)
                    p = jnp.exp(s - m)
                    l = jnp.sum(p, axis=-1, keepdims=True)
                    p16 = p.astype(BF16)
                    o_un = _dot(jnp.concatenate([p16[:ATT_BLK], p16[ATT_BLK:]], axis=1), _stack_heads_rows(vc, lane))
                    og[gi][cur, :] = o_un / _per_head(l, lane)
                    lg[gi][cur, :] = _per_head(m + jnp.log(l), lane)
                    return carry

                lax.fori_loop(0, S // ATT_BLK, step, 0, unroll=S // ATT_BLK)

        @pl.when(g == n_groups - 1)
        def _():
            def comb(i, carry):
                rows = pl.ds(pl.multiple_of(i * CH, CH), CH)
                a, b, c = lg[0][rows, :], lg[1][rows, :], lg[2][rows, :]
                m = jnp.maximum(jnp.maximum(a, b), c)
                ea, eb, ec = jnp.exp(a - m), jnp.exp(b - m), jnp.exp(c - m)
                z = ea + eb + ec
                o_ref[rows, :] = (ea / z) * og[0][rows, :] + (eb / z) * og[1][rows, :] + (ec / z) * og[2][rows, :]
                lse_ref[rows, :] = m + jnp.log(z)
                return carry

            lax.fori_loop(0, S // CH, comb, 0)

    blk = (None, S, LANES)
    out = pl.BlockSpec(blk, lambda b, hp, g: (b, 0, hp))
    return pl.pallas_call(
        body, name=name, grid=(B, HP, n_groups),
        in_specs=[pl.BlockSpec(memory_space=pltpu.SMEM),
                  pl.BlockSpec(blk, lambda b, hp, g: (b, 0, g * HP + hp)),
                  pl.BlockSpec(blk, lambda b, hp, g: (b, 0, g * 2 * HP + hp)),
                  pl.BlockSpec(blk, lambda b, hp, g: (b, 0, g * 2 * HP + HP + hp))],
        out_specs=[out, out],
        out_shape=[jax.ShapeDtypeStruct((B, S, HP * LANES), F32)] * 2,
        scratch_shapes=[pltpu.VMEM((2, 2 * ATT_BLK, 2 * ATT_BLK), F32)] + [pltpu.VMEM((S, LANES), F32)] * (2 * n_groups),
        compiler_params=_params(3))(slopes, q, kv, kv)


def _stack_heads_rows(x16, lane):
    first = lane < HEAD_DIM
    return jnp.concatenate([jnp.where(first, x16, jnp.zeros_like(x16)),
                            jnp.where(first, jnp.zeros_like(x16), x16)], axis=0)


def _attn_bwd(name, q, kv, slopes, o, lse, do, n_heads, dkv_prev):
    B, S, CQ = q.shape
    HP = n_heads * HEAD_DIM // LANES
    scale = HEAD_DIM ** -0.5
    n_groups = len(PATTERNS)
    n_prev = 0 if dkv_prev is None else 2

    def body(sl_ref, q_ref, k_ref, v_ref, o_ref, lse_ref, do_ref, *rest):
        dq_ref, dk_ref, dv_ref, bias_ref = rest[n_prev:]
        hp, g = pl.program_id(1), pl.program_id(2)
        lane = lax.broadcasted_iota(jnp.int32, (1, LANES), 1)
        first = lane < HEAD_DIM

        def flush(rows, dk, dv):
            if n_prev:
                dk = dk + rest[0][rows, :]
                dv = dv + rest[1][rows, :]
            dk_ref[rows, :] = dk
            dv_ref[rows, :] = dv

        for gi, (window, dil) in enumerate(PATTERNS):
            nb = S // dil // ATT_BLK
            n_blocks = S // ATT_BLK

            @pl.when(g == gi)
            def _(dil=dil, nb=nb, n_blocks=n_blocks):
                _att_bias(bias_ref, dil, sl_ref, hp)

                def block(idx, carry, first_of_all):
                    n, cur, prev = _att_rows(dil, idx, nb)
                    qs = _stack_heads((q_ref[cur, :] * scale).astype(BF16), lane)
                    kc = jnp.concatenate([k_ref[prev, :], k_ref[cur, :]], axis=0).astype(BF16)
                    vc = jnp.concatenate([v_ref[prev, :], v_ref[cur, :]], axis=0).astype(BF16)
                    dob = do_ref[cur, :]
                    prod = dob * o_ref[cur, :]
                    lseb = lse_ref[cur, :]
                    dos = _stack_heads(dob.astype(BF16), lane)
                    delta = jnp.concatenate(
                        [jnp.sum(jnp.where(first, prod, 0.0), axis=-1, keepdims=True),
                         jnp.sum(jnp.where(first, 0.0, prod), axis=-1, keepdims=True)], axis=0)
                    lse_col = jnp.concatenate(
                        [jnp.max(jnp.where(first, lseb, -jnp.inf), axis=-1, keepdims=True),
                         jnp.max(jnp.where(first, -jnp.inf, lseb), axis=-1, keepdims=True)], axis=0)
                    s = _dot_nt(qs, kc) + bias_ref[jnp.minimum(n, 1)]
                    p = jnp.exp(s - lse_col)
                    ds = p * (_dot_nt(dos, vc) - delta)
                    ds16 = ds.astype(BF16)
                    dq = _dot(jnp.concatenate([ds16[:ATT_BLK], ds16[ATT_BLK:]], axis=1), _stack_heads_rows(kc, lane))
                    dq_ref[cur, :] = dq * scale
                    dk = _dot_tn(ds16, qs)
                    dv = _dot_tn(p.astype(BF16), dos)

                    def flush_before():
                        _, before, _ = _att_rows(dil, idx - 1, nb)
                        flush(before, carry[0] + dk[:ATT_BLK], carry[1] + dv[:ATT_BLK])

                    if first_of_all:
                        pl.when(idx > 0)(flush_before)
                    else:
                        flush_before()
                    return dk[ATT_BLK:], dv[ATT_BLK:]

                def step(i, carry):
                    for u in range(BWD_UNROLL):
                        carry = block(i * BWD_UNROLL + u, carry, u == 0)
                    return carry

                zero = jnp.zeros((ATT_BLK, LANES), F32)
                dk_last, dv_last = lax.fori_loop(0, n_blocks // BWD_UNROLL, step, (zero, zero))
                _, last, _ = _att_rows(dil, n_blocks - 1, nb)
                flush(last, dk_last, dv_last)

    blk = (None, S, LANES)
    shared = pl.BlockSpec(blk, lambda b, hp, g: (b, 0, hp))
    grouped = pl.BlockSpec(blk, lambda b, hp, g: (b, 0, g * HP + hp))
    prev = [] if dkv_prev is None else list(dkv_prev)
    gshape = jax.ShapeDtypeStruct((B, S, n_groups * HP * LANES), F32)
    return pl.pallas_call(
        body, name=name, grid=(B, HP, n_groups),
        in_specs=[pl.BlockSpec(memory_space=pltpu.SMEM), grouped,
                  pl.BlockSpec(blk, lambda b, hp, g: (b, 0, g * 2 * HP + hp)),
                  pl.BlockSpec(blk, lambda b, hp, g: (b, 0, g * 2 * HP + HP + hp)),
                  shared, shared, shared] + [grouped] * n_prev,
        out_specs=[grouped] * 3, out_shape=[gshape] * 3,
        scratch_shapes=[pltpu.VMEM((2, 2 * ATT_BLK, 2 * ATT_BLK), F32)],
        compiler_params=_params(3))(slopes, q, kv, kv, o, lse, do, *prev)


def _final_loss(name, h, g, target, tm):
    T, D = h.shape

    def body(h_ref, g_ref, t_ref, loss_ref, dh_ref, dh16_ref, dg_ref):
        hf = h_ref[...]
        gv = g_ref[...]
        rstd = lax.rsqrt(jnp.mean(hf * hf, axis=-1, keepdims=True) + EPS)
        xhat = hf * rstd
        err = xhat * gv - t_ref[...]
        part = 0.5 * jnp.sum(jnp.mean(err * err, axis=-1, keepdims=True), axis=0, keepdims=True)
        dy = err * (1.0 / D)
        dg = jnp.sum(dy * xhat, axis=0, keepdims=True)
        dx = dy * gv
        dh = rstd * (dx - xhat * jnp.mean(dx * xhat, axis=-1, keepdims=True))
        dh_ref[...] = dh
        dh16_ref[...] = dh.astype(BF16)

        @pl.when(pl.program_id(0) == 0)
        def _():
            loss_ref[...] = part
            dg_ref[...] = dg

        @pl.when(pl.program_id(0) > 0)
        def _():
            loss_ref[...] += part
            dg_ref[...] += dg

    return pl.pallas_call(
        body, name=name, grid=(T // tm,),
        in_specs=[pl.BlockSpec((tm, D), lambda i: (i, 0)), pl.BlockSpec((1, D), lambda i: (0, 0)),
                  pl.BlockSpec((tm, D), lambda i: (i, 0))],
        out_specs=[pl.BlockSpec((1, 1), lambda i: (0, 0)), pl.BlockSpec((tm, D), lambda i: (i, 0)),
                   pl.BlockSpec((tm, D), lambda i: (i, 0)), pl.BlockSpec((1, D), lambda i: (0, 0))],
        out_shape=[jax.ShapeDtypeStruct((1, 1), F32), jax.ShapeDtypeStruct((T, D), F32),
                   jax.ShapeDtypeStruct((T, D), BF16), jax.ShapeDtypeStruct((1, D), F32)],
        compiler_params=_params(1))(h, g, target)


def _nt_rows(name, dh, wg, layer, a_mul, out_dtype, tm, deps=()):
    T, D = dh.shape
    rk = wg.shape[2]
    N = N_CHIPS * rk
    with_a = a_mul is not None

    def body(dh_ref, w_ref, *rest):
        o_ref = rest[-1]
        d16 = dh_ref[...]
        for ch in range(N_CHIPS):
            r = _dot_nt(d16, w_ref[ch])
            if with_a:
                r = r * (2.0 * jnp.maximum(rest[0][:, ch * rk:(ch + 1) * rk].astype(F32), 0.0))
            o_ref[:, ch * rk:(ch + 1) * rk] = r.astype(out_dtype)

    in_specs = [pl.BlockSpec((tm, D), lambda i: (i, 0)),
                pl.BlockSpec((N_CHIPS, None, rk, D), lambda i: (0, layer, 0, 0))]
    args = [dh, wg]
    if with_a:
        in_specs.append(pl.BlockSpec((tm, N), lambda i: (i, 0)))
        args.append(a_mul)
    in_specs += [ANY] * len(deps)
    args += list(deps)
    return pl.pallas_call(
        body, name=name, grid=(T // tm,), in_specs=in_specs,
        out_specs=pl.BlockSpec((tm, N), lambda i: (i, 0)),
        out_shape=jax.ShapeDtypeStruct((T, N), out_dtype),
        compiler_params=_params(1))(*args)


def _nt_cols(name, ysegs, wg, layer, tm, norm, deps=()):
    Nw, cw = wg.shape[2], wg.shape[3]
    widths = [bs[-1] for _, bs, _ in ysegs]
    pieces = _pieces(widths, cw, 1024)
    ns = len(ysegs)
    T = norm[0].shape[0] if norm is not None else ysegs[0][0].shape[-2]

    def body(*refs):
        y_refs = refs[:ns]
        w_ref = refs[ns]
        acc = refs[-1]
        for n, (s, a0, ch, b0, wd) in enumerate(pieces):
            d = _dot_nt(y_refs[s][:, a0:a0 + wd].astype(BF16), w_ref[ch, :, b0:b0 + wd])
            if n == 0:
                acc[...] = d
            else:
                acc[...] += d
        if norm is None:
            refs[ns + 1 + len(deps)][...] = acc[...]
        else:
            h_ref, g_ref, dhin_ref = refs[ns + 1:ns + 4]
            out_ref, out16_ref, dg_ref = refs[ns + 4 + len(deps):ns + 7 + len(deps)]
            dh_c, dg = _rms_bwd(h_ref[...], g_ref[...], acc[...])
            dh = dhin_ref[...] + dh_c
            out_ref[...] = dh
            out16_ref[...] = dh.astype(BF16)

            @pl.when(pl.program_id(0) == 0)
            def _():
                dg_ref[...] = dg

            @pl.when(pl.program_id(0) > 0)
            def _():
                dg_ref[...] += dg

    in_specs = [pl.BlockSpec(bs, im) for _, bs, im in ysegs]
    in_specs.append(pl.BlockSpec((N_CHIPS, None, Nw, cw), lambda i: (0, layer, 0, 0)))
    args = [a for a, _, _ in ysegs] + [wg]
    row = pl.BlockSpec((tm, Nw), lambda i: (i, 0))
    vec = pl.BlockSpec((1, Nw), lambda i: (0, 0))
    if norm is None:
        out_specs = row
        out_shape = jax.ShapeDtypeStruct((T, Nw), F32)
    else:
        in_specs += [row, vec, row]
        args += list(norm)
    in_specs += [ANY] * len(deps)
    args += list(deps)
    if norm is not None:
        out_specs = [row, row, vec]
        out_shape = [jax.ShapeDtypeStruct((T, Nw), F32), jax.ShapeDtypeStruct((T, Nw), BF16),
                     jax.ShapeDtypeStruct((1, Nw), F32)]
    return pl.pallas_call(
        body, name=name, grid=(T // tm,), in_specs=in_specs, out_specs=out_specs, out_shape=out_shape,
        scratch_shapes=[pltpu.VMEM((tm, Nw), F32)], compiler_params=_params(1))(*args)


def _tn(name, x, x_act, ysegs, cw, cols_layout, tmm, tt, deps=(), out_dtype=F32):
    T, M = x.shape
    widths = [bs[-1] for _, bs, _ in ysegs]
    N = sum(widths)
    pieces = _pieces(widths, cw if cols_layout else N, 1024)
    ns = len(ysegs)
    n_t = T // tt
    block = (N_CHIPS, tmm, cw) if cols_layout else (tmm, N)
    narrow = out_dtype != F32

    def body(x_ref, *refs):
        y_refs = refs[:ns]
        o_ref = refs[ns + len(deps)]
        acc = refs[-1] if narrow else o_ref

        @pl.when(pl.program_id(1) == 0)
        def _():
            acc[...] = jnp.zeros_like(acc)

        xt = x_act(x_ref[...])
        for s, a0, ch, b0, wd in pieces:
            d = _dot_tn(xt, y_refs[s][:, a0:a0 + wd].astype(BF16))
            if cols_layout:
                acc[ch, :, b0:b0 + wd] += d
            else:
                acc[:, b0:b0 + wd] += d
        if narrow:
            @pl.when(pl.program_id(1) == n_t - 1)
            def _():
                o_ref[...] = acc[...].astype(out_dtype)

    in_specs = [pl.BlockSpec((tt, tmm), lambda m, t: (t, m))] + [pl.BlockSpec(bs, im) for _, bs, im in ysegs]
    in_specs += [ANY] * len(deps)
    if cols_layout:
        out_specs = pl.BlockSpec(block, lambda m, t: (0, m, 0))
        out_shape = jax.ShapeDtypeStruct((N_CHIPS, M, cw), out_dtype)
    else:
        out_specs = pl.BlockSpec(block, lambda m, t: (m, 0))
        out_shape = jax.ShapeDtypeStruct((M, N), out_dtype)
    return pl.pallas_call(
        body, name=name, grid=(M // tmm, n_t), in_specs=in_specs, out_specs=out_specs, out_shape=out_shape,
        scratch_shapes=[pltpu.VMEM(block, F32)] if narrow else [],
        compiler_params=_params(2))(x, *[a for a, _, _ in ysegs], *deps)


def _seg2d(a, t_rows, grid_rank):
    w = a.shape[1]
    if grid_rank == 1:
        return (a, (t_rows, w), lambda i: (i, 0))
    return (a, (t_rows, w), lambda m, t: (t, 0))


def _kv_segments(dk, dv, C, t_rows, grid_rank):
    segs = []
    for g in range(len(PATTERNS)):
        for a in (dk, dv):
            if grid_rank == 1:
                segs.append((a, (t_rows, C), lambda i, g=g: (i, g)))
            else:
                segs.append((a, (t_rows, C), lambda m, t, g=g: (t, g)))
    return segs


def _seg_plane(a, plane, t_rows, grid_rank):
    w = a.shape[2]
    if grid_rank == 1:
        return (a, (None, t_rows, w), lambda i: (plane, i, 0))
    return (a, (None, t_rows, w), lambda m, t: (plane, t, 0))


def _row_tile(rows, row_bytes, budget_bytes=2 * 1024 * 1024):
    t = rows
    while t * row_bytes > budget_bytes and t % 32 == 0:
        t //= 2
    return t


N_DEVICES = 8


def _device_add(name, own, slots, place):
    _, _, hr, c = own.shape
    tr = _row_tile(hr, c * 4, 1024 * 1024)

    def body(place_ref, own_ref, *refs):
        o_ref = refs[-1]
        acc = own_ref[...].astype(F32)
        for r in refs[:-1]:
            acc = acc + r[...].astype(F32)
        o_ref[...] = acc

    def slot(k):
        return pl.BlockSpec((None, tr, c), lambda i, pr: ((2 * pr[0] + pr[1] + k) % N_DEVICES, i, 0))

    grid_spec = pltpu.PrefetchScalarGridSpec(
        num_scalar_prefetch=1, grid=(hr // tr,),
        in_specs=[pl.BlockSpec((None, None, tr, c), lambda i, pr: (pr[0], pr[1], i, 0))]
        + [slot(k) for k in range(1, N_DEVICES)],
        out_specs=pl.BlockSpec((None, tr, c), lambda i, pr: (pr[1], i, 0)))
    return pl.pallas_call(body, name=name, grid_spec=grid_spec,
                          out_shape=jax.ShapeDtypeStruct((2, hr, c), F32),
                          compiler_params=_params(1))(place, own, *[slots] * (N_DEVICES - 1))


def _adamw(name, w, g, m, v):
    rows, cols = w.shape
    tr = _row_tile(rows, cols * 4, 1024 * 1024)

    def body(w_ref, g_ref, m_ref, v_ref, d_ref, nm_ref, nv_ref):
        d_ref[...], nm_ref[...], nv_ref[...] = _adamw_math(w_ref[...], g_ref[...], m_ref[...], v_ref[...])

    spec = pl.BlockSpec((tr, cols), lambda i: (i, 0))
    return pl.pallas_call(
        body, name=name, grid=(rows // tr,), in_specs=[spec] * 4, out_specs=[spec] * 3,
        out_shape=[jax.ShapeDtypeStruct((rows, cols), F32)] * 3, compiler_params=_params(1))(w, g, m, v)


def _adamw_math(w, g, m, v):
    nm = ADAM_B1 * m + (1.0 - ADAM_B1) * g
    nv = ADAM_B2 * v + (1.0 - ADAM_B2) * jnp.square(g)
    m_hat = nm / (1.0 - ADAM_B1 ** ADAM_STEP)
    v_hat = nv / (1.0 - ADAM_B2 ** ADAM_STEP)
    return -ADAM_LR * (m_hat / (jnp.sqrt(v_hat) + ADAM_EPS) + ADAM_WD * w), nm, nv


def _adamw_layers(name, w, grads, m, v):
    L, r, c = w.shape
    tr = _row_tile(r, L * c * 4, 1024 * 1024)

    def body(*refs):
        w_ref, m_ref, v_ref = refs[:3]
        g_refs = refs[3:3 + L]
        go_ref, d_ref, nm_ref, nv_ref = refs[3 + L:]
        for l in range(L):
            g = g_refs[l][...]
            go_ref[l] = g
            d_ref[l], nm_ref[l], nv_ref[l] = _adamw_math(w_ref[l], g, m_ref[l], v_ref[l])

    stacked = pl.BlockSpec((L, tr, c), lambda i: (0, i, 0))
    return pl.pallas_call(
        body, name=name, grid=(r // tr,),
        in_specs=[stacked] * 3 + [pl.BlockSpec((tr, c), lambda i: (i, 0))] * L, out_specs=[stacked] * 4,
        out_shape=[jax.ShapeDtypeStruct((L, r, c), F32)] * 4, compiler_params=_params(1))(w, m, v, *grads)


def _place():
    x, y, c = lax.axis_index("x"), lax.axis_index("y"), lax.axis_index("c")
    chips = [(1 - x, y), (x, 1 - y), (1 - x, 1 - y)]
    return x, y, c, chips


HBM = pl.BlockSpec(memory_space=pltpu.HBM)
SEM = pl.BlockSpec(memory_space=pltpu.SEMAPHORE)
EFFECT = pltpu.SideEffectType.DATAFLOW_SIDE_EFFECTING


class _Copy:
    def __init__(self, src, src_view, land, dst_view, recv_view, target):
        self.src, self.src_view, self.land, self.dst_view, self.recv_view, self.target = (
            src, src_view, land, dst_view, recv_view, target)


def _whole(ref, place):
    return ref


def _split_start(name, srcs, land_shapes, plans, deps=()):
    skeys, lkeys = list(srcs), list(land_shapes)
    ns, nl, ng, nd = len(skeys), len(lkeys), len(plans), len(deps)

    def body(*refs):
        src = dict(zip(skeys, refs[:ns]))
        land = dict(zip(lkeys, refs[ns:ns + nl]))
        sems = refs[ns + nl + nd:ns + nl + nd + 2 * ng]
        token = refs[-1]
        place = _place()
        for gi, plan in enumerate(plans):
            for k, cp in enumerate(plan):
                dst = land[cp.land] if cp.land in land else src[cp.land]
                pltpu.make_async_remote_copy(
                    src_ref=cp.src_view(src[cp.src], place), dst_ref=cp.dst_view(dst, place),
                    send_sem=sems[2 * gi].at[k], recv_sem=sems[2 * gi + 1].at[k],
                    device_id=cp.target(place), device_id_type=MESH).start()
        token[...] = jnp.zeros_like(token)

    sem_shapes = []
    for plan in plans:
        sem_shapes += [pltpu.SemaphoreType.DMA((len(plan),))] * 2
    buffers = [srcs[k] for k in skeys] + [lax.empty(land_shapes[k].shape, land_shapes[k].dtype) for k in lkeys]
    outs = pl.pallas_call(
        body, name=name,
        out_shape=(*sem_shapes, *[pltpu.HBM(a.shape, a.dtype) for a in buffers], jax.ShapeDtypeStruct((8, LANES), F32)),
        in_specs=[HBM] * (ns + nl) + [ANY] * nd,
        out_specs=(*[SEM] * (2 * ng), *[HBM] * (ns + nl), pl.BlockSpec(memory_space=pltpu.VMEM)),
        input_output_aliases={i: 2 * ng + i for i in range(ns + nl)},
        compiler_params=pltpu.CompilerParams(has_side_effects=EFFECT),
    )(*[pltpu.with_memory_space_constraint(a, pltpu.HBM) for a in buffers], *deps)
    sems = [(outs[2 * gi], outs[2 * gi + 1]) for gi in range(ng)]
    thru = outs[2 * ng:2 * ng + ns + nl]
    return sems, dict(zip(skeys, thru[:ns])), dict(zip(lkeys, thru[ns:])), outs[-1]


def _split_wait(name, sems, srcs, lands, plan, after):
    skeys, lkeys = list(srcs), list(lands)
    ns, nl = len(skeys), len(lkeys)

    def body(*refs):
        src = dict(zip(skeys, refs[:ns]))
        land = dict(zip(lkeys, refs[ns:ns + nl]))
        ssem, rsem = refs[ns + nl], refs[ns + nl + 1]
        place = _place()
        for k, cp in enumerate(plan):
            dst = land[cp.land] if cp.land in land else src[cp.land]
            pltpu.make_async_remote_copy(
                src_ref=cp.src_view(src[cp.src], place), dst_ref=cp.dst_view(dst, place),
                send_sem=ssem.at[k], recv_sem=rsem.at[k],
                device_id=cp.target(place), device_id_type=MESH).wait_send()
            got = cp.recv_view(dst, place)
            pltpu.make_async_remote_copy(
                src_ref=got, dst_ref=got, send_sem=ssem.at[k], recv_sem=rsem.at[k],
                device_id=cp.target(place), device_id_type=MESH).wait_recv()

    buffers = [srcs[k] for k in skeys] + [lands[k] for k in lkeys]
    outs = pl.pallas_call(
        body, name=name, out_shape=tuple(pltpu.HBM(a.shape, a.dtype) for a in buffers),
        in_specs=(*[HBM] * (ns + nl), SEM, SEM, ANY), out_specs=tuple([HBM] * (ns + nl)),
        input_output_aliases={i: i for i in range(ns + nl)},
        compiler_params=pltpu.CompilerParams(has_side_effects=EFFECT),
    )(*buffers, sems[0], sems[1], after)
    return dict(zip(skeys, outs[:ns])), dict(zip(lkeys, outs[ns:]))


def _chip_of(place):
    x, y, c, chips = place
    return 2 * x + y


GATHER_FIRST = 2


class _WeightGather:
    def __init__(self, blocks):
        self.plans, shapes = {}, {}
        for key, a in blocks.items():
            shapes[key] = jax.ShapeDtypeStruct((N_CHIPS,) + a.shape, a.dtype)
            slot = lambda ref, place: ref.at[_chip_of(place)]
            plan = [_Copy(key, _whole, key, slot,
                          lambda ref, place, k=k: ref.at[2 * place[3][k][0] + place[3][k][1]],
                          lambda place, k=k: (place[3][k][0], place[3][k][1], place[2])) for k in range(3)]
            plan.append(_Copy(key, _whole, key, slot, slot, lambda place: (place[0], place[1], 1 - place[2])))
            self.plans[key] = plan
        keys = list(blocks)
        first, a = keys[0], blocks[keys[0]]
        hr = a.shape[0] // 2
        mine = lambda ref, place, q: ref.at[q, pl.ds(pl.multiple_of(place[2] * hr, 16), hr)]
        theirs = lambda ref, place, q: ref.at[q, pl.ds(pl.multiple_of((1 - place[2]) * hr, 16), hr)]
        chip = lambda place, k: 2 * place[3][k][0] + place[3][k][1]
        sibling = lambda place: (place[0], place[1], 1 - place[2])
        self.plans[first] = [
            _Copy(first, lambda ref, place: ref.at[pl.ds(pl.multiple_of(place[2] * hr, 16), hr)], first,
                  lambda ref, place: mine(ref, place, _chip_of(place)),
                  lambda ref, place, k=k: mine(ref, place, chip(place, k)),
                  lambda place, k=k: (place[3][k][0], place[3][k][1], place[2])) for k in range(3)]
        self.plans[first].append(_Copy(first, _whole, first, lambda ref, place: ref.at[_chip_of(place)],
                                       lambda ref, place: ref.at[_chip_of(place)], sibling))
        self.forward = [_Copy(first, lambda ref, place, k=k: mine(ref, place, chip(place, k)), first,
                              lambda ref, place, k=k: mine(ref, place, chip(place, k)),
                              lambda ref, place, k=k: theirs(ref, place, chip(place, k)), sibling) for k in range(3)]
        self.blocks, self.shapes = blocks, shapes
        self.sems, self.srcs, self.lands = {}, {}, {}
        self._start("gather_start_first", keys[:GATHER_FIRST], ())
        self.rest = keys[GATHER_FIRST:]

    def _start(self, name, part, deps):
        sems, srcs, lands, self.token = _split_start(name, {k: self.blocks[k] for k in part},
                                                     {k: self.shapes[k] for k in part}, [self.plans[k] for k in part], deps)
        self.sems.update(zip(part, sems))
        self.srcs.update(srcs)
        self.lands.update(lands)

    def get(self, l, name, after):
        key = (l, name)
        _, lands = _split_wait(f"gather_wait_{name}{l}", self.sems[key], {key: self.srcs[key]},
                               {key: self.lands[key]}, self.plans[key], after)
        if self.rest:
            sems, bufs, _, _ = _split_start("gather_forward", {key: lands[key]}, {}, [self.forward])
            lands, _ = _split_wait("gather_forward_wait", sems[0], bufs, {}, self.forward, after)
            self._start("gather_start", self.rest, [lands[key]])
            self.rest = []
        return lands[key][:, None]


class _GradReduce:
    def __init__(self, place):
        self.place = place
        self.jobs = []
        self.done = {}
        self.n = 0

    def submit(self, grads):
        views = {k: a.reshape(N_CHIPS, 2, a.shape[1] // 2, a.shape[2]) for k, a in grads.items()}
        shapes = {k: jax.ShapeDtypeStruct((N_DEVICES,) + a.shape[2:], a.dtype) for k, a in views.items()}

        def peer(place, k):
            x, y, c, _ = place
            return (1 - x if k & 4 else x, 1 - y if k & 2 else y, 1 - c if k & 1 else c)

        def index(dev):
            return 4 * dev[0] + 2 * dev[1] + dev[2]

        plan = []
        for key in views:
            for k in range(1, N_DEVICES):
                plan.append(_Copy(
                    key, lambda ref, place, k=k: ref.at[2 * peer(place, k)[0] + peer(place, k)[1], peer(place, k)[2]],
                    key, lambda ref, place: ref.at[index(place[:3])],
                    lambda ref, place, k=k: ref.at[index(peer(place, k))],
                    lambda place, k=k: peer(place, k)))
        sems, srcs, lands, token = _split_start(f"grad_start{self.n}", views, shapes, [plan])
        self.jobs.append(dict(id=self.n, sems=sems[0], srcs=srcs, lands=lands, plan=plan))
        self.n += 1
        return token

    def pump(self, after):
        return []

    def finish(self, after):
        for job in self.jobs:
            srcs, lands = _split_wait(f"grad_wait{job['id']}", job["sems"], job["srcs"], job["lands"], job["plan"],
                                      after)
            for i, k in enumerate(srcs):
                self.done[k] = _device_add(f"grad_add{job['id']}_{i}", srcs[k], lands[k], self.place)
        self.jobs = []
        return self.done


class _PairShare:
    def __init__(self, halves, types):
        sibling = lambda place: (place[0], place[1], 1 - place[2])
        mine = lambda ref, place: ref.at[place[2]]
        theirs = lambda ref, place: ref.at[1 - place[2]]
        self.plans = {t: [_Copy(k, mine, k, mine, theirs, sibling) for k in halves if k[0] == t] for t in types}
        sems, self.bufs, _, self.token = _split_start("share_start", halves, {}, list(self.plans.values()))
        self.sems = dict(zip(self.plans, sems))

    def get(self, t, after):
        keys = [cp.src for cp in self.plans[t]]
        bufs, _ = _split_wait(f"share_wait_{t}", self.sems[t], {k: self.bufs[k] for k in keys}, {}, self.plans[t], after)
        return bufs


def _small_allreduce(part):
    R, C = part.shape
    N_DEV = 8

    def body(in_ref, out_ref, slots, ssem, rsem):
        x, y, c, _ = _place()
        me = 4 * x + 2 * y + c
        sends = []
        for k in range(1, N_DEV):
            kx, ky, kc = (k >> 2) & 1, (k >> 1) & 1, k & 1
            peer = (1 - x if kx else x, 1 - y if ky else y, 1 - c if kc else c)
            cp = pltpu.make_async_remote_copy(
                src_ref=in_ref, dst_ref=slots.at[me], send_sem=ssem.at[k], recv_sem=rsem.at[k],
                device_id=peer, device_id_type=MESH)
            cp.start()
            sends.append(cp)
        slots[me] = in_ref[...]
        for k in range(1, N_DEV):
            kx, ky, kc = (k >> 2) & 1, (k >> 1) & 1, k & 1
            peer = (1 - x if kx else x, 1 - y if ky else y, 1 - c if kc else c)
            slot = slots.at[4 * peer[0] + 2 * peer[1] + peer[2]]
            pltpu.make_async_remote_copy(
                src_ref=slot, dst_ref=slot, send_sem=ssem.at[k], recv_sem=rsem.at[k],
                device_id=peer, device_id_type=MESH).wait_recv()
        acc = slots[0]
        for d in range(1, N_DEV):
            acc = acc + slots[d]
        out_ref[...] = acc
        for cp in sends:
            cp.wait_send()

    vm = pl.BlockSpec(memory_space=pltpu.VMEM)
    return pl.pallas_call(
        body, name="small_allreduce", in_specs=[vm], out_specs=vm,
        out_shape=jax.ShapeDtypeStruct((R, C), F32),
        scratch_shapes=[pltpu.VMEM((N_DEV, R, C), F32), pltpu.SemaphoreType.DMA((N_DEV,)),
                        pltpu.SemaphoreType.DMA((N_DEV,))])(part)


def _local_step(x, target, norm_mix, norm_mlp, norm_kv, norm_final, weights, sink, n_a, n_heads):
    B, S, D = x.shape
    T = B * S
    C = n_heads * HEAD_DIM
    depth = norm_mix.shape[0]
    slopes = 2.0 ** (-ALIBI_MAX_BIAS * jnp.arange(1, n_heads + 1, dtype=F32) / n_heads)
    tm = min(512, T)
    row = lambda v: v.reshape(1, -1)

    h = x.reshape(T, D)
    saved, Wl = [], []
    kv = nkv = h_kv = cwg = None
    for l in range(depth):
        s = {"h_in": h}
        w = {}
        Wl.append(w)
        if l < n_a:
            n_in = _rms_only("a_in_norm0", h, row(norm_mix[l]), tm) if l == 0 else h
            w["w_a_in"] = weights.get(l, "w_a_in", n_in)
            first = [weights.token] if l == 0 else []
            s["n1"], bcu = _norm_mm(f"a_in_fwd{l}", n_in, row(norm_mix[l]), w["w_a_in"], 0, 3, BF16, tm, first, l == 0)
            s["bcu"] = bcu.reshape(3, B, S, D)
            if l == 0:
                cwg = weights.get(0, "conv", bcu)[:, 0, :n_a * 3].reshape(N_CHIPS, n_a, 3, -1)
            s["z"] = _conv_fwd(f"conv_fwd{l}", s["bcu"], cwg, l, CONV_COLS).reshape(T, D)
            w["w_a_out"] = weights.get(l, "w_a_out", s["z"])
            h = _mm_res_rows(f"a_out_fwd{l}", s["z"], w["w_a_out"], 0, h, _to_bf16, tm)
        else:
            i = l - n_a
            if i == 0:
                h_kv = h
                w["w_kv"] = weights.get(l, "w_kv", h)
                nkv, kv = _norm_mm("kv_fwd", h, row(norm_kv), w["w_kv"], 0, 1, F32, tm)
                kv = kv.reshape(B, S, 2 * 3 * C)
            w["w_q"] = weights.get(l, "w_q", h)
            s["n1"], q = _norm_mm(f"q_fwd{i}", h, row(norm_mix[l]), w["w_q"], 0, 1, F32, tm)
            s["q"] = q.reshape(B, S, 3 * C)
            o, lse = _attn_fwd(f"attn_fwd{i}", s["q"], kv, slopes, n_heads)
            s["o"], s["lse"] = o.reshape(T, C), lse.reshape(T, C)
            w["w_o"] = weights.get(l, "w_o", o)
            h = _mm_res_cols(f"o_fwd{i}", s["o"], w["w_o"], 0, h, tm)
        s["h_mid"] = h
        w["w_up"] = weights.get(l, "w_up", h)
        if l < n_a:
            s["n2"], a = _norm_mm(f"up_fwd{l}", h, row(norm_mlp[l]), w["w_up"], 0, 1, BF16, tm)
            s["a"] = a[0]
            w["w_down"] = weights.get(l, "w_down", a)
            h = _mm_res_rows(f"down_fwd{l}", s["a"], w["w_down"], 0, h, _relu2_bf16, tm)
        else:
            w["w_down"] = weights.get(l, "w_down", h)
            s["n2"], s["a"], h = _mlp_fwd(f"mlp_fwd{l}", h, row(norm_mlp[l]), w["w_up"], w["w_down"], tm)
        F = s["a"].shape[1]
        saved.append(s)

    loss, dh, dh16, dg_final = _final_loss("loss_head", h, row(norm_final), target.reshape(T, D), tm)

    g_mix, g_mlp = [None] * depth, [None] * depth
    g_conv = [None] * n_a
    dkv = None
    tt = min(512, T)
    deps = []
    for l in reversed(range(depth)):
        s, w = saved[l], Wl[l]
        g_down = _tn(f"down_wgrad{l}", s["a"], _relu2_bf16, [_seg2d(dh16, tt, 2)], None, False,
                     min(2048, F), tt, deps, BF16).reshape(N_CHIPS, F // N_CHIPS, D)
        da, dh, dh16, g_mlp[l] = _mlp_bwd(f"mlp_bwd{l}", dh, dh16, s["a"], w["w_down"], w["w_up"], s["h_mid"],
                                          row(norm_mlp[l]), tm)
        g_up = _tn(f"up_wgrad{l}", s["n2"], _to_bf16, [_seg2d(da, tt, 2)], F // N_CHIPS, True, D, tt, (), BF16)
        deps = sink.pump(dh) + [sink.submit({("w_up", l): g_up, ("w_down", l): g_down})]
        if l < n_a:
            g_out = _tn(f"a_out_wgrad{l}", s["z"], _to_bf16, [_seg2d(dh16, tt, 2)], None, False,
                        D, tt, deps, BF16).reshape(N_CHIPS, D // N_CHIPS, D)
            dz = _nt_rows(f"a_out_bwd{l}", dh16, w["w_a_out"], 0, None, F32, tm)
            deps = sink.pump(dz) + [sink.submit({("w_a_out", l): g_out})]
            dbcu, g_conv[l] = _conv_bwd(f"conv_bwd{l}", s["bcu"], dz.reshape(B, S, D), cwg, l, CONV_COLS)
            dbcu = dbcu.reshape(3, T, D)
            g_in = _tn(f"a_in_wgrad{l}", s["n1"], _to_bf16, [_seg_plane(dbcu, p, tt, 2) for p in range(3)],
                       3 * D // N_CHIPS, True, D, tt, deps, BF16)
            deps = [sink.submit({("w_a_in", l): g_in})]
            dh, dh16, g_mix[l] = _nt_cols(f"a_in_bwd{l}", [_seg_plane(dbcu, p, tm, 1) for p in range(3)],
                                          w["w_a_in"], 0, tm, (s["h_in"], row(norm_mix[l]), dh), deps)
        else:
            i = l - n_a
            g_o = _tn(f"o_wgrad{i}", s["o"], _to_bf16, [_seg2d(dh16, tt, 2)], D // N_CHIPS, True, C, tt, deps,
                      BF16)
            do = _nt_cols(f"o_bwd{i}", [_seg2d(dh16, tm, 1)], w["w_o"], 0, tm, None)
            deps = sink.pump(do) + [sink.submit({("w_o", i): g_o})]
            dq, dk, dv = _attn_bwd(f"attn_bwd{i}", s["q"], kv, slopes, s["o"].reshape(B, S, C),
                                   s["lse"].reshape(B, S, C), do.reshape(B, S, C), n_heads, dkv)
            dkv = (dk, dv)
            dq = dq.reshape(T, 3 * C)
            g_q = _tn(f"q_wgrad{i}", s["n1"], _to_bf16, [_seg2d(dq, tt, 2)], 3 * C // N_CHIPS, True, D, tt, deps,
                      BF16)
            mixer = {("w_q", i): g_q}
            if i == 0:
                dk2, dv2 = (t.reshape(T, 3 * C) for t in dkv)
                mixer[("w_kv", 0)] = _tn("kv_wgrad", nkv, _to_bf16, _kv_segments(dk2, dv2, C, tt, 2),
                                         6 * C // N_CHIPS, True, D, tt, (), BF16)
            deps = [sink.submit(mixer)]
            dh, dh16, g_mix[l] = _nt_cols(f"q_bwd{i}", [_seg2d(dq, tm, 1)], w["w_q"], 0, tm,
                                          (s["h_in"], row(norm_mix[l]), dh), deps)
            if i == 0:
                dh, dh16, g_kv = _nt_cols("kv_bwd", _kv_segments(dk2, dv2, C, tm, 1), w["w_kv"], 0, tm,
                                          (h_kv, row(norm_kv), dh))
        deps = sink.pump(dh)
    small = dict(norm_mix=jnp.concatenate(g_mix, axis=0), norm_mlp=jnp.concatenate(g_mlp, axis=0),
                 norm_kv=g_kv, norm_final=dg_final, conv_w=jnp.stack(g_conv))
    return loss, dh.reshape(B, S, D), small


BIG = ("w_a_in", "w_a_out", "w_kv", "w_q", "w_o", "w_up", "w_down")
CONV_PAD_ROWS = 16


def kernel(x, norm_mix, norm_mlp, w_a_in, conv_w, w_a_out, norm_kv, w_kv, w_q, w_o, w_up, w_down, norm_final, loss_target, m_norm_mix, m_norm_mlp, m_w_a_in, m_conv_w, m_w_a_out, m_norm_kv, m_w_kv, m_w_q, m_w_o, m_w_up, m_w_down, m_norm_final, v_norm_mix, v_norm_mlp, v_w_a_in, v_conv_w, v_w_a_out, v_norm_kv, v_w_kv, v_w_q, v_w_o, v_w_up, v_w_down, v_norm_final):
    D = x.shape[-1]
    w = dict(norm_mix=norm_mix, norm_mlp=norm_mlp, w_a_in=w_a_in, conv_w=conv_w, w_a_out=w_a_out, norm_kv=norm_kv,
             w_kv=w_kv[None], w_q=w_q, w_o=w_o, w_up=w_up, w_down=w_down, norm_final=norm_final)
    m = dict(norm_mix=m_norm_mix, norm_mlp=m_norm_mlp, w_a_in=m_w_a_in, conv_w=m_conv_w, w_a_out=m_w_a_out,
             norm_kv=m_norm_kv, w_kv=m_w_kv[None], w_q=m_w_q, w_o=m_w_o, w_up=m_w_up, w_down=m_w_down,
             norm_final=m_norm_final)
    v = dict(norm_mix=v_norm_mix, norm_mlp=v_norm_mlp, w_a_in=v_w_a_in, conv_w=v_conv_w, w_a_out=v_w_a_out,
             norm_kv=v_norm_kv, w_kv=v_w_kv[None], w_q=v_w_q, w_o=v_w_o, w_up=v_w_up, w_down=v_w_down,
             norm_final=v_norm_final)
    depth = norm_mix.shape[0]
    n_a, taps, cwc = conv_w.shape
    n_heads = w_o.shape[1] // HEAD_DIM

    conv_rows = jnp.zeros((CONV_PAD_ROWS, cwc), F32).at[:n_a * taps].set(conv_w.reshape(n_a * taps, cwc))
    blocks = {}
    for l in range(depth):
        if l < n_a:
            blocks[(l, "w_a_in")] = w_a_in[l].astype(BF16)
            if l == 0:
                blocks[(0, "conv")] = conv_rows
            blocks[(l, "w_a_out")] = w_a_out[l].astype(BF16)
        else:
            if l == n_a:
                blocks[(l, "w_kv")] = w_kv.astype(BF16)
            blocks[(l, "w_q")] = w_q[l - n_a].astype(BF16)
            blocks[(l, "w_o")] = w_o[l - n_a].astype(BF16)
        blocks[(l, "w_up")] = w_up[l].astype(BF16)
        blocks[(l, "w_down")] = w_down[l].astype(BF16)
    weights = _WeightGather(blocks)
    place = jnp.stack([2 * lax.axis_index("x") + lax.axis_index("y"), lax.axis_index("c")]).astype(jnp.int32)
    sink = _GradReduce(place)

    loss, grad_x, small = _local_step(x, loss_target, norm_mix, norm_mlp, norm_kv, norm_final, weights, sink,
                                      n_a, n_heads)
    loss = lax.psum(loss[0, 0], ("x", "y", "c"))

    share = _PairShare(sink.finish(grad_x), BIG)
    grads = {}

    packed = jnp.concatenate([small["norm_mix"], small["norm_mlp"], small["norm_kv"], small["norm_final"],
                              small["conv_w"].reshape(n_a * taps, D)], axis=0)
    pad = (-packed.shape[0]) % 8
    packed = jnp.pad(packed, ((0, pad), (0, 0)))
    total = _small_allreduce(packed)
    grads["norm_mix"] = total[:depth]
    grads["norm_mlp"] = total[depth:2 * depth]
    grads["norm_kv"] = total[2 * depth]
    grads["norm_final"] = total[2 * depth + 1]
    chip = 2 * lax.axis_index("x") + lax.axis_index("y")
    conv_full = total[2 * depth + 2:2 * depth + 2 + n_a * taps].reshape(n_a, taps, N_CHIPS, cwc)
    grads["conv_w"] = lax.dynamic_index_in_dim(conv_full, chip, axis=2, keepdims=False)

    order = ("norm_mix", "norm_mlp", "w_a_in", "conv_w", "w_a_out", "norm_kv", "w_kv", "w_q", "w_o", "w_up",
             "w_down", "norm_final")
    delta, new_m, new_v = {}, {}, {}
    vec_names = ("norm_mix", "norm_mlp", "norm_kv", "norm_final")
    rows_of = lambda a: a.reshape(-1, D)
    vw, vg, vm_, vv = (jnp.concatenate([rows_of(t[k]) for k in vec_names], axis=0) for t in (w, grads, m, v))
    vpad = (-vw.shape[0]) % 8
    padrows = lambda a: jnp.pad(a, ((0, vpad), (0, 0)))
    vd, vnm, vnv = _adamw("adamw_norms", padrows(vw), padrows(vg), padrows(vm_), padrows(vv))
    off = 0
    for k in vec_names:
        r = rows_of(w[k]).shape[0]
        delta[k] = vd[off:off + r].reshape(w[k].shape)
        new_m[k] = vnm[off:off + r].reshape(w[k].shape)
        new_v[k] = vnv[off:off + r].reshape(w[k].shape)
        off += r
    cpad = (-n_a * taps) % 8
    two_d = lambda a: jnp.pad(a.reshape(-1, cwc), ((0, cpad), (0, 0)))
    cd, cnm, cnv = _adamw("adamw_conv_w", two_d(w["conv_w"]), two_d(grads["conv_w"]), two_d(m["conv_w"]),
                          two_d(v["conv_w"]))
    delta["conv_w"], new_m["conv_w"], new_v["conv_w"] = (t[:n_a * taps].reshape(conv_w.shape) for t in (cd, cnm, cnv))
    after = cd
    for k in sorted(BIG, key=lambda k: w[k].size):
        shared = share.get(k, after)
        per_layer = [shared[(k, l)].reshape(w[k].shape[1:]) for l in range(w[k].shape[0])]
        grads[k], delta[k], new_m[k], new_v[k] = _adamw_layers(f"adamw_{k}", w[k], per_layer, m[k], v[k])
        after = delta[k]
    fix = lambda k, a: a[0] if k == "w_kv" else a
    return (loss, grad_x, *[fix(k, grads[k]) for k in order], *[fix(k, delta[k]) for k in order],
            *[fix(k, new_m[k]) for k in order], *[fix(k, new_v[k]) for k in order])
```

```python
import jax
import jax.numpy as jnp
from jax import lax
from jax.experimental import pallas as pl
from jax.experimental.pallas import tpu as pltpu

F32 = jnp.float32
BF16 = jnp.bfloat16
MESH = pl.DeviceIdType.MESH

EPS = 1e-5
PATTERNS = ((128, 1), (512, 4), (2048, 16))
HEAD_DIM = 64
ALIBI_MAX_BIAS = 8.0
NEG_INF = -1e30
ATT_BLK = 128
BWD_UNROLL = 32
N_CHIPS = 4
LANES = 128
VMEM_LIMIT = 56 * 1024 * 1024

ADAM_LR = 0.001
ADAM_B1 = 0.9
ADAM_B2 = 0.999
ADAM_EPS = 1e-08
ADAM_WD = 0.01
ADAM_STEP = 10


ANY = pl.BlockSpec(memory_space=pl.ANY)


def _params(n_grid_axes):
    return pltpu.CompilerParams(dimension_semantics=("arbitrary",) * n_grid_axes, vmem_limit_bytes=VMEM_LIMIT)


def _dot(a, b):
    return jnp.dot(a, b, preferred_element_type=F32)


def _dot_nt(a, b):
    return lax.dot_general(a, b, (((1,), (1,)), ((), ())), preferred_element_type=F32)


def _dot_tn(a, b):
    return lax.dot_general(a, b, (((0,), (0,)), ((), ())), preferred_element_type=F32)


def _relu2(a):
    return jnp.square(jnp.maximum(a, 0.0))


def _rms(hf, g):
    y = hf * lax.rsqrt(jnp.mean(hf * hf, axis=-1, keepdims=True) + EPS)
    return y * g


def _rms_bwd(hf, g, dn):
    rstd = lax.rsqrt(jnp.mean(hf * hf, axis=-1, keepdims=True) + EPS)
    xhat = hf * rstd
    dg = jnp.sum(dn * xhat, axis=0, keepdims=True)
    dx = dn * g
    dh = rstd * (dx - xhat * jnp.mean(dx * xhat, axis=-1, keepdims=True))
    return dh, dg


def _pieces(seg_widths, chunk_width, max_width):
    total = sum(seg_widths)
    cuts = {0, total}
    acc = 0
    for w in seg_widths:
        cuts.add(acc)
        acc += w
    cuts.update(range(0, total, chunk_width))
    cuts = sorted(cuts)
    fine = []
    for lo, hi in zip(cuts[:-1], cuts[1:]):
        while hi - lo > max_width:
            fine.append((lo, lo + max_width))
            lo += max_width
        fine.append((lo, hi))
    out = []
    for lo, hi in fine:
        acc = 0
        for s, w in enumerate(seg_widths):
            if lo < acc + w:
                break
            acc += w
        out.append((s, lo - acc, lo // chunk_width, lo % chunk_width, hi - lo))
    return out


def _relu2_bf16(a):
    return _relu2(a.astype(F32)).astype(BF16)


def _to_bf16(a):
    return a.astype(BF16)


def _rms_only(name, h, g, tm):
    T, D = h.shape

    def body(h_ref, g_ref, n_ref):
        n_ref[...] = _rms(h_ref[...], g_ref[...]).astype(BF16)

    row = pl.BlockSpec((tm, D), lambda i: (i, 0))
    return pl.pallas_call(
        body, name=name, grid=(T // tm,), in_specs=[row, pl.BlockSpec((1, D), lambda i: (0, 0))], out_specs=row,
        out_shape=jax.ShapeDtypeStruct((T, D), BF16), compiler_params=_params(1))(h, g)


def _norm_mm(name, h, g, wg, layer, planes, out_dtype, tm, deps=(), normed=False):
    T, D = h.shape
    cw = wg.shape[3]
    N = N_CHIPS * cw
    pw = N // planes
    pieces = _pieces([pw] * planes, cw, 512)

    def body(h_ref, g_ref, w_ref, *rest):
        n_ref, o_ref = rest[len(deps):]
        n = h_ref[...] if normed else _rms(h_ref[...], g_ref[...]).astype(BF16)
        n_ref[...] = n
        for s, a0, ch, b0, wd in pieces:
            o_ref[s, :, a0:a0 + wd] = _dot(n, w_ref[ch, :, b0:b0 + wd]).astype(out_dtype)

    return pl.pallas_call(
        body, name=name, grid=(T // tm,),
        in_specs=[pl.BlockSpec((tm, D), lambda i: (i, 0)),
                  pl.BlockSpec((1, D), lambda i: (0, 0)),
                  pl.BlockSpec((N_CHIPS, None, D, cw), lambda i: (0, layer, 0, 0))] + [ANY] * len(deps),
        out_specs=[pl.BlockSpec((tm, D), lambda i: (i, 0)),
                   pl.BlockSpec((planes, tm, pw), lambda i: (0, i, 0))],
        out_shape=[jax.ShapeDtypeStruct((T, D), BF16), jax.ShapeDtypeStruct((planes, T, pw), out_dtype)],
        compiler_params=_params(1))(h, g, wg, *deps)


def _resident(shape, index_map):
    return pl.BlockSpec(shape, index_map, pipeline_mode=pl.Buffered(1))


def _mm_res_rows(name, a, wg, layer, h, act, tm):
    T = a.shape[0]
    rk, D = wg.shape[2], wg.shape[3]

    def body(a_ref, w_ref, h_ref, o_ref):
        acc = h_ref[...]
        for k in range(N_CHIPS):
            acc = acc + _dot(act(a_ref[:, k * rk:(k + 1) * rk]), w_ref[k])
        o_ref[...] = acc

    return pl.pallas_call(
        body, name=name, grid=(T // tm,),
        in_specs=[pl.BlockSpec((tm, N_CHIPS * rk), lambda i: (i, 0)),
                  pl.BlockSpec((N_CHIPS, None, rk, D), lambda i: (0, layer, 0, 0)),
                  pl.BlockSpec((tm, D), lambda i: (i, 0))],
        out_specs=pl.BlockSpec((tm, D), lambda i: (i, 0)),
        out_shape=jax.ShapeDtypeStruct((T, D), F32),
        compiler_params=_params(1))(a, wg, h)


def _mm_res_cols(name, a, wg, layer, h, tm):
    T, K = a.shape
    cw = wg.shape[3]
    D = N_CHIPS * cw

    def body(a_ref, w_ref, h_ref, o_ref):
        a16 = a_ref[...].astype(BF16)
        for j in range(N_CHIPS):
            o_ref[:, j * cw:(j + 1) * cw] = h_ref[:, j * cw:(j + 1) * cw] + _dot(a16, w_ref[j])

    return pl.pallas_call(
        body, name=name, grid=(T // tm,),
        in_specs=[pl.BlockSpec((tm, K), lambda i: (i, 0)),
                  pl.BlockSpec((N_CHIPS, None, K, cw), lambda i: (0, layer, 0, 0)),
                  pl.BlockSpec((tm, D), lambda i: (i, 0))],
        out_specs=pl.BlockSpec((tm, D), lambda i: (i, 0)),
        out_shape=jax.ShapeDtypeStruct((T, D), F32),
        compiler_params=_params(1))(a, wg, h)


def _mlp_fwd(name, h, g, wup, wdown, tm):
    T, D = h.shape
    cw = wup.shape[3]

    def body(h_ref, g_ref, wu_ref, wd_ref, n_ref, a_ref, o_ref):
        hf = h_ref[...]
        n = _rms(hf, g_ref[...]).astype(BF16)
        n_ref[...] = n
        acc = hf
        for ch in range(N_CHIPS):
            a16 = _dot(n, wu_ref[ch]).astype(BF16)
            a_ref[:, ch * cw:(ch + 1) * cw] = a16
            acc = acc + _dot(_relu2_bf16(a16), wd_ref[ch])
        o_ref[...] = acc

    row = pl.BlockSpec((tm, D), lambda i: (i, 0))
    return pl.pallas_call(
        body, name=name, grid=(T // tm,),
        in_specs=[row, pl.BlockSpec((1, D), lambda i: (0, 0)),
                  _resident((N_CHIPS, None, D, cw), lambda i: (0, 0, 0, 0)),
                  _resident((N_CHIPS, None, cw, D), lambda i: (0, 0, 0, 0))],
        out_specs=[row, pl.BlockSpec((tm, N_CHIPS * cw), lambda i: (i, 0)), row],
        out_shape=[jax.ShapeDtypeStruct((T, D), BF16), jax.ShapeDtypeStruct((T, N_CHIPS * cw), BF16),
                   jax.ShapeDtypeStruct((T, D), F32)],
        compiler_params=_params(1))(h, g, wup, wdown)


def _mlp_bwd(name, dh, dh16, a, wdown, wup, h_mid, g, tm, deps=()):
    T, D = dh.shape
    cw = wup.shape[3]
    F = N_CHIPS * cw

    def body(dh_ref, dh16_ref, a_ref, wd_ref, wu_ref, h_ref, g_ref, *rest):
        da_ref, out_ref, out16_ref, dg_ref = rest[len(deps):]
        d16 = dh16_ref[...]
        acc = None
        for ch in range(N_CHIPS):
            cols = slice(ch * cw, (ch + 1) * cw)
            da = (_dot_nt(d16, wd_ref[ch]) * (2.0 * jnp.maximum(a_ref[:, cols].astype(F32), 0.0))).astype(BF16)
            da_ref[:, cols] = da
            d = _dot_nt(da, wu_ref[ch])
            acc = d if acc is None else acc + d
        dh_c, dg = _rms_bwd(h_ref[...], g_ref[...], acc)
        out = dh_ref[...] + dh_c
        out_ref[...] = out
        out16_ref[...] = out.astype(BF16)

        @pl.when(pl.program_id(0) == 0)
        def _():
            dg_ref[...] = dg

        @pl.when(pl.program_id(0) > 0)
        def _():
            dg_ref[...] += dg

    row = pl.BlockSpec((tm, D), lambda i: (i, 0))
    wide = pl.BlockSpec((tm, F), lambda i: (i, 0))
    vec = pl.BlockSpec((1, D), lambda i: (0, 0))
    return pl.pallas_call(
        body, name=name, grid=(T // tm,),
        in_specs=[row, row, wide, _resident((N_CHIPS, None, cw, D), lambda i: (0, 0, 0, 0)),
                  _resident((N_CHIPS, None, D, cw), lambda i: (0, 0, 0, 0)), row, vec] + [ANY] * len(deps),
        out_specs=[wide, row, row, vec],
        out_shape=[jax.ShapeDtypeStruct((T, F), BF16), jax.ShapeDtypeStruct((T, D), F32),
                   jax.ShapeDtypeStruct((T, D), BF16), jax.ShapeDtypeStruct((1, D), F32)],
        compiler_params=_params(1))(dh, dh16, a, wdown, wup, h_mid, g, *deps)


CONV_ROWS = 256
CONV_HALO = 16
CONV_COLS = 2 * LANES


def _conv_shifted(ext, k, r0, rows, at_start):
    rolled = pltpu.roll(ext, k, 0)[CONV_HALO:]
    if not at_start:
        return rolled
    t = r0 + lax.broadcasted_iota(jnp.int32, rolled.shape, 0)
    return jnp.where(t >= k, rolled, 0.0)


def _conv_ahead(ext, k, r0, rows, S, at_end):
    rolled = pltpu.roll(ext, rows + CONV_HALO - k, 0)[:rows]
    if not at_end:
        return rolled
    t = r0 + lax.broadcasted_iota(jnp.int32, rolled.shape, 0)
    return jnp.where(t + k < S, rolled, 0.0)


def _conv_chunks(step, n, carry):
    carry = step(0, carry, True, n == 1)
    if n > 2:
        carry = lax.fori_loop(1, n - 1, lambda i, c: step(i, c, False, False), carry)
    if n > 1:
        carry = step(n - 1, carry, False, True)
    return carry


def _conv_fwd(name, bcu, cwg, layer, tc):
    _, B, S, D = bcu.shape
    cwc = cwg.shape[3]
    per_chunk = cwc // tc
    R = min(CONV_ROWS, S)

    def body(x_ref, w_ref, z_ref):
        w = [w_ref[k:k + 1, :] for k in range(3)]

        def step(i, carry, at_start, at_end):
            r0 = pl.multiple_of(i * R, R)
            h0 = pl.multiple_of(jnp.maximum(r0 - CONV_HALO, 0), CONV_HALO)
            ld = lambda p, start, rows: x_ref[p, pl.ds(start, rows), :].astype(F32)
            cu = jnp.concatenate([ld(1, h0, CONV_HALO) * ld(2, h0, CONV_HALO), ld(1, r0, R) * ld(2, r0, R)], axis=0)
            conv = w[0] * cu[CONV_HALO:]
            conv = conv + w[1] * _conv_shifted(cu, 1, r0, R, at_start)
            conv = conv + w[2] * _conv_shifted(cu, 2, r0, R, at_start)
            z_ref[pl.ds(r0, R), :] = (ld(0, r0, R) * conv).astype(BF16)
            return carry

        _conv_chunks(step, S // R, 0)

    return pl.pallas_call(
        body, name=name, grid=(B, D // tc),
        in_specs=[pl.BlockSpec((3, None, S, tc), lambda b, j: (0, b, 0, j)),
                  pl.BlockSpec((None, None, 3, tc), lambda b, j: (j // per_chunk, layer, 0, j % per_chunk))],
        out_specs=pl.BlockSpec((None, S, tc), lambda b, j: (b, 0, j)),
        out_shape=jax.ShapeDtypeStruct((B, S, D), BF16),
        compiler_params=_params(2))(bcu, cwg)


def _conv_bwd(name, bcu, dz, cwg, layer, tc):
    _, B, S, D = bcu.shape
    cwc = cwg.shape[3]
    per_chunk = cwc // tc
    R = min(CONV_ROWS, S)

    def body(x_ref, dz_ref, w_ref, d_ref, dw_ref):
        w = [w_ref[k:k + 1, :] for k in range(3)]

        @pl.when(pl.program_id(1) == 0)
        def _():
            dw_ref[...] = jnp.zeros_like(dw_ref)

        def step(i, carry, at_start, at_end):
            r0 = pl.multiple_of(i * R, R)
            h0 = pl.multiple_of(jnp.maximum(r0 - CONV_HALO, 0), CONV_HALO)
            a0 = pl.multiple_of(jnp.minimum(r0 + R, S - CONV_HALO), CONV_HALO)
            ld = lambda p, start, rows: x_ref[p, pl.ds(start, rows), :].astype(F32)
            b, c, u = ld(0, r0, R), ld(1, r0, R), ld(2, r0, R)
            dz = dz_ref[pl.ds(r0, R), :]
            cu = jnp.concatenate([ld(1, h0, CONV_HALO) * ld(2, h0, CONV_HALO), c * u], axis=0)
            cu1 = _conv_shifted(cu, 1, r0, R, at_start)
            cu2 = _conv_shifted(cu, 2, r0, R, at_start)
            conv = w[0] * (c * u) + w[1] * cu1 + w[2] * cu2
            dconv = dz * b
            dca = jnp.concatenate([dconv, dz_ref[pl.ds(a0, CONV_HALO), :] * ld(0, a0, CONV_HALO)], axis=0)
            dcu = (w[0] * dconv + w[1] * _conv_ahead(dca, 1, r0, R, S, at_end)
                   + w[2] * _conv_ahead(dca, 2, r0, R, S, at_end))
            d_ref[0, pl.ds(r0, R), :] = (dz * conv).astype(BF16)
            d_ref[1, pl.ds(r0, R), :] = (dcu * u).astype(BF16)
            d_ref[2, pl.ds(r0, R), :] = (dcu * c).astype(BF16)
            return (carry[0] + jnp.sum(dconv * (c * u), axis=0, keepdims=True),
                    carry[1] + jnp.sum(dconv * cu1, axis=0, keepdims=True),
                    carry[2] + jnp.sum(dconv * cu2, axis=0, keepdims=True))

        zero = jnp.zeros((1, tc), F32)
        s0, s1, s2 = _conv_chunks(step, S // R, (zero, zero, zero))
        for k, sk in enumerate((s0, s1, s2)):
            dw_ref[k:k + 1, :] += sk

    return pl.pallas_call(
        body, name=name, grid=(D // tc, B),
        in_specs=[pl.BlockSpec((3, None, S, tc), lambda j, b: (0, b, 0, j)),
                  pl.BlockSpec((None, S, tc), lambda j, b: (b, 0, j)),
                  pl.BlockSpec((None, None, 3, tc), lambda j, b: (j // per_chunk, layer, 0, j % per_chunk))],
        out_specs=[pl.BlockSpec((3, None, S, tc), lambda j, b: (0, b, 0, j)),
                   pl.BlockSpec((3, tc), lambda j, b: (0, j))],
        out_shape=[jax.ShapeDtypeStruct((3, B, S, D), BF16), jax.ShapeDtypeStruct((3, D), F32)],
        compiler_params=_params(2))(bcu, dz, cwg)


def _att_rows(dil, idx, nb):
    r, n = idx // nb, idx % nb
    if dil == 1:
        cur = pl.ds(pl.multiple_of(n * ATT_BLK, ATT_BLK), ATT_BLK)
        prev = pl.ds(pl.multiple_of(jnp.maximum(n - 1, 0) * ATT_BLK, ATT_BLK), ATT_BLK)
    else:
        cur = pl.ds(n * (ATT_BLK * dil) + r, ATT_BLK, stride=dil)
        prev = pl.ds(jnp.maximum(n - 1, 0) * (ATT_BLK * dil) + r, ATT_BLK, stride=dil)
    return n, cur, prev


def _att_bias(bias_ref, dil, sl_ref, hp):
    row = lax.broadcasted_iota(jnp.int32, (2 * ATT_BLK, 2 * ATT_BLK), 0)
    ci = lax.broadcasted_iota(jnp.int32, (2 * ATT_BLK, 2 * ATT_BLK), 1)
    j = ATT_BLK + (row & (ATT_BLK - 1)) - ci
    slope = jnp.where(row < ATT_BLK, sl_ref[2 * hp], sl_ref[2 * hp + 1])
    rest = jnp.where((j >= 0) & (j <= ATT_BLK), -slope * (dil * j).astype(F32), NEG_INF)
    bias_ref[1] = rest
    bias_ref[0] = jnp.where(ci >= ATT_BLK, rest, NEG_INF)


def _stack_heads(x16, lane):
    first = lane < HEAD_DIM
    return jnp.concatenate([jnp.where(first, x16, jnp.zeros_like(x16)),
                            jnp.where(first, jnp.zeros_like(x16), x16)], axis=0)


def _per_head(col, lane):
    return jnp.where(lane < HEAD_DIM, col[:ATT_BLK], col[ATT_BLK:])


def _attn_fwd(name, q, kv, slopes, n_heads):
    B, S, CQ = q.shape
    HP = n_heads * HEAD_DIM // LANES
    scale = HEAD_DIM ** -0.5
    n_groups = len(PATTERNS)
    CH = 256

    def body(sl_ref, q_ref, k_ref, v_ref, o_ref, lse_ref, bias_ref, *parts):
        og, lg = parts[:n_groups], parts[n_groups:]
        hp, g = pl.program_id(1), pl.program_id(2)
        lane = lax.broadcasted_iota(jnp.int32, (1, LANES), 1)

        for gi, (window, dil) in enumerate(PATTERNS):
            nb = S // dil // ATT_BLK

            @pl.when(g == gi)
            def _(gi=gi, dil=dil, nb=nb):
                _att_bias(bias_ref, dil, sl_ref, hp)

                def step(idx, carry):
                    n, cur, prev = _att_rows(dil, idx, nb)
                    qs = _stack_heads((q_ref[cur, :] * scale).astype(BF16), lane)
                    kc = jnp.concatenate([k_ref[prev, :], k_ref[cur, :]], axis=0).astype(BF16)
                    vc = jnp.concatenate([v_ref[prev, :], v_ref[cur, :]], axis=0).astype(BF16)
                    s = _dot_nt(qs, kc) + bias_ref[jnp.minimum(n, 1)]
                    m = jnp.max(s, axis=-1, keepdims=True)
                    p = jnp.exp(s - m)
                    l = jnp.sum(p, axis=-1, keepdims=True)
                    p16 = p.astype(BF16)
                    o_un = _dot(jnp.concatenate([p16[:ATT_BLK], p16[ATT_BLK:]], axis=1), _stack_heads_rows(vc, lane))
                    og[gi][cur, :] = o_un / _per_head(l, lane)
                    lg[gi][cur, :] = _per_head(m + jnp.log(l), lane)
                    return carry

                lax.fori_loop(0, S // ATT_BLK, step, 0, unroll=S // ATT_BLK)

        @pl.when(g == n_groups - 1)
        def _():
            def comb(i, carry):
                rows = pl.ds(pl.multiple_of(i * CH, CH), CH)
                a, b, c = lg[0][rows, :], lg[1][rows, :], lg[2][rows, :]
                m = jnp.maximum(jnp.maximum(a, b), c)
                ea, eb, ec = jnp.exp(a - m), jnp.exp(b - m), jnp.exp(c - m)
                z = ea + eb + ec
                o_ref[rows, :] = (ea / z) * og[0][rows, :] + (eb / z) * og[1][rows, :] + (ec / z) * og[2][rows, :]
                lse_ref[rows, :] = m + jnp.log(z)
                return carry

            lax.fori_loop(0, S // CH, comb, 0)

    blk = (None, S, LANES)
    out = pl.BlockSpec(blk, lambda b, hp, g: (b, 0, hp))
    return pl.pallas_call(
        body, name=name, grid=(B, HP, n_groups),
        in_specs=[pl.BlockSpec(memory_space=pltpu.SMEM),
                  pl.BlockSpec(blk, lambda b, hp, g: (b, 0, g * HP + hp)),
                  pl.BlockSpec(blk, lambda b, hp, g: (b, 0, g * 2 * HP + hp)),
                  pl.BlockSpec(blk, lambda b, hp, g: (b, 0, g * 2 * HP + HP + hp))],
        out_specs=[out, out],
        out_shape=[jax.ShapeDtypeStruct((B, S, HP * LANES), F32)] * 2,
        scratch_shapes=[pltpu.VMEM((2, 2 * ATT_BLK, 2 * ATT_BLK), F32)] + [pltpu.VMEM((S, LANES), F32)] * (2 * n_groups),
        compiler_params=_params(3))(slopes, q, kv, kv)


def _stack_heads_rows(x16, lane):
    first = lane < HEAD_DIM
    return jnp.concatenate([jnp.where(first, x16, jnp.zeros_like(x16)),
                            jnp.where(first, jnp.zeros_like(x16), x16)], axis=0)


def _attn_bwd(name, q, kv, slopes, o, lse, do, n_heads, dkv_prev):
    B, S, CQ = q.shape
    HP = n_heads * HEAD_DIM // LANES
    scale = HEAD_DIM ** -0.5
    n_groups = len(PATTERNS)
    n_prev = 0 if dkv_prev is None else 2

    def body(sl_ref, q_ref, k_ref, v_ref, o_ref, lse_ref, do_ref, *rest):
        dq_ref, dk_ref, dv_ref, bias_ref = rest[n_prev:]
        hp, g = pl.program_id(1), pl.program_id(2)
        lane = lax.broadcasted_iota(jnp.int32, (1, LANES), 1)
        first = lane < HEAD_DIM

        def flush(rows, dk, dv):
            if n_prev:
                dk = dk + rest[0][rows, :]
                dv = dv + rest[1][rows, :]
            dk_ref[rows, :] = dk
            dv_ref[rows, :] = dv

        for gi, (window, dil) in enumerate(PATTERNS):
            nb = S // dil // ATT_BLK
            n_blocks = S // ATT_BLK

            @pl.when(g == gi)
            def _(dil=dil, nb=nb, n_blocks=n_blocks):
                _att_bias(bias_ref, dil, sl_ref, hp)

                def block(idx, carry, first_of_all):
                    n, cur, prev = _att_rows(dil, idx, nb)
                    qs = _stack_heads((q_ref[cur, :] * scale).astype(BF16), lane)
                    kc = jnp.concatenate([k_ref[prev, :], k_ref[cur, :]], axis=0).astype(BF16)
                    vc = jnp.concatenate([v_ref[prev, :], v_ref[cur, :]], axis=0).astype(BF16)
                    dob = do_ref[cur, :]
                    prod = dob * o_ref[cur, :]
                    lseb = lse_ref[cur, :]
                    dos = _stack_heads(dob.astype(BF16), lane)
                    delta = jnp.concatenate(
                        [jnp.sum(jnp.where(first, prod, 0.0), axis=-1, keepdims=True),
                         jnp.sum(jnp.where(first, 0.0, prod), axis=-1, keepdims=True)], axis=0)
                    lse_col = jnp.concatenate(
                        [jnp.max(jnp.where(first, lseb, -jnp.inf), axis=-1, keepdims=True),
                         jnp.max(jnp.where(first, -jnp.inf, lseb), axis=-1, keepdims=True)], axis=0)
                    s = _dot_nt(qs, kc) + bias_ref[jnp.minimum(n, 1)]
                    p = jnp.exp(s - lse_col)
                    ds = p * (_dot_nt(dos, vc) - delta)
                    ds16 = ds.astype(BF16)
                    dq = _dot(jnp.concatenate([ds16[:ATT_BLK], ds16[ATT_BLK:]], axis=1), _stack_heads_rows(kc, lane))
                    dq_ref[cur, :] = dq * scale
                    dk = _dot_tn(ds16, qs)
                    dv = _dot_tn(p.astype(BF16), dos)

                    def flush_before():
                        _, before, _ = _att_rows(dil, idx - 1, nb)
                        flush(before, carry[0] + dk[:ATT_BLK], carry[1] + dv[:ATT_BLK])

                    if first_of_all:
                        pl.when(idx > 0)(flush_before)
                    else:
                        flush_before()
                    return dk[ATT_BLK:], dv[ATT_BLK:]

                def step(i, carry):
                    for u in range(BWD_UNROLL):
                        carry = block(i * BWD_UNROLL + u, carry, u == 0)
                    return carry

                zero = jnp.zeros((ATT_BLK, LANES), F32)
                dk_last, dv_last = lax.fori_loop(0, n_blocks // BWD_UNROLL, step, (zero, zero))
                _, last, _ = _att_rows(dil, n_blocks - 1, nb)
                flush(last, dk_last, dv_last)

    blk = (None, S, LANES)
    shared = pl.BlockSpec(blk, lambda b, hp, g: (b, 0, hp))
    grouped = pl.BlockSpec(blk, lambda b, hp, g: (b, 0, g * HP + hp))
    prev = [] if dkv_prev is None else list(dkv_prev)
    gshape = jax.ShapeDtypeStruct((B, S, n_groups * HP * LANES), F32)
    return pl.pallas_call(
        body, name=name, grid=(B, HP, n_groups),
        in_specs=[pl.BlockSpec(memory_space=pltpu.SMEM), grouped,
                  pl.BlockSpec(blk, lambda b, hp, g: (b, 0, g * 2 * HP + hp)),
                  pl.BlockSpec(blk, lambda b, hp, g: (b, 0, g * 2 * HP + HP + hp)),
                  shared, shared, shared] + [grouped] * n_prev,
        out_specs=[grouped] * 3, out_shape=[gshape] * 3,
        scratch_shapes=[pltpu.VMEM((2, 2 * ATT_BLK, 2 * ATT_BLK), F32)],
        compiler_params=_params(3))(slopes, q, kv, kv, o, lse, do, *prev)


def _final_loss(name, h, g, target, tm):
    T, D = h.shape

    def body(h_ref, g_ref, t_ref, loss_ref, dh_ref, dh16_ref, dg_ref):
        hf = h_ref[...]
        gv = g_ref[...]
        rstd = lax.rsqrt(jnp.mean(hf * hf, axis=-1, keepdims=True) + EPS)
        xhat = hf * rstd
        err = xhat * gv - t_ref[...]
        part = 0.5 * jnp.sum(jnp.mean(err * err, axis=-1, keepdims=True), axis=0, keepdims=True)
        dy = err * (1.0 / D)
        dg = jnp.sum(dy * xhat, axis=0, keepdims=True)
        dx = dy * gv
        dh = rstd * (dx - xhat * jnp.mean(dx * xhat, axis=-1, keepdims=True))
        dh_ref[...] = dh
        dh16_ref[...] = dh.astype(BF16)

        @pl.when(pl.program_id(0) == 0)
        def _():
            loss_ref[...] = part
            dg_ref[...] = dg

        @pl.when(pl.program_id(0) > 0)
        def _():
            loss_ref[...] += part
            dg_ref[...] += dg

    return pl.pallas_call(
        body, name=name, grid=(T // tm,),
        in_specs=[pl.BlockSpec((tm, D), lambda i: (i, 0)), pl.BlockSpec((1, D), lambda i: (0, 0)),
                  pl.BlockSpec((tm, D), lambda i: (i, 0))],
        out_specs=[pl.BlockSpec((1, 1), lambda i: (0, 0)), pl.BlockSpec((tm, D), lambda i: (i, 0)),
                   pl.BlockSpec((tm, D), lambda i: (i, 0)), pl.BlockSpec((1, D), lambda i: (0, 0))],
        out_shape=[jax.ShapeDtypeStruct((1, 1), F32), jax.ShapeDtypeStruct((T, D), F32),
                   jax.ShapeDtypeStruct((T, D), BF16), jax.ShapeDtypeStruct((1, D), F32)],
        compiler_params=_params(1))(h, g, target)


def _nt_rows(name, dh, wg, layer, a_mul, out_dtype, tm, deps=()):
    T, D = dh.shape
    rk = wg.shape[2]
    N = N_CHIPS * rk
    with_a = a_mul is not None

    def body(dh_ref, w_ref, *rest):
        o_ref = rest[-1]
        d16 = dh_ref[...]
        for ch in range(N_CHIPS):
            r = _dot_nt(d16, w_ref[ch])
            if with_a:
                r = r * (2.0 * jnp.maximum(rest[0][:, ch * rk:(ch + 1) * rk].astype(F32), 0.0))
            o_ref[:, ch * rk:(ch + 1) * rk] = r.astype(out_dtype)

    in_specs = [pl.BlockSpec((tm, D), lambda i: (i, 0)),
                pl.BlockSpec((N_CHIPS, None, rk, D), lambda i: (0, layer, 0, 0))]
    args = [dh, wg]
    if with_a:
        in_specs.append(pl.BlockSpec((tm, N), lambda i: (i, 0)))
        args.append(a_mul)
    in_specs += [ANY] * len(deps)
    args += list(deps)
    return pl.pallas_call(
        body, name=name, grid=(T // tm,), in_specs=in_specs,
        out_specs=pl.BlockSpec((tm, N), lambda i: (i, 0)),
        out_shape=jax.ShapeDtypeStruct((T, N), out_dtype),
        compiler_params=_params(1))(*args)


def _nt_cols(name, ysegs, wg, layer, tm, norm, deps=()):
    Nw, cw = wg.shape[2], wg.shape[3]
    widths = [bs[-1] for _, bs, _ in ysegs]
    pieces = _pieces(widths, cw, 1024)
    ns = len(ysegs)
    T = norm[0].shape[0] if norm is not None else ysegs[0][0].shape[-2]

    def body(*refs):
        y_refs = refs[:ns]
        w_ref = refs[ns]
        acc = refs[-1]
        for n, (s, a0, ch, b0, wd) in enumerate(pieces):
            d = _dot_nt(y_refs[s][:, a0:a0 + wd].astype(BF16), w_ref[ch, :, b0:b0 + wd])
            if n == 0:
                acc[...] = d
            else:
                acc[...] += d
        if norm is None:
            refs[ns + 1 + len(deps)][...] = acc[...]
        else:
            h_ref, g_ref, dhin_ref = refs[ns + 1:ns + 4]
            out_ref, out16_ref, dg_ref = refs[ns + 4 + len(deps):ns + 7 + len(deps)]
            dh_c, dg = _rms_bwd(h_ref[...], g_ref[...], acc[...])
            dh = dhin_ref[...] + dh_c
            out_ref[...] = dh
            out16_ref[...] = dh.astype(BF16)

            @pl.when(pl.program_id(0) == 0)
            def _():
                dg_ref[...] = dg

            @pl.when(pl.program_id(0) > 0)
            def _():
                dg_ref[...] += dg

    in_specs = [pl.BlockSpec(bs, im) for _, bs, im in ysegs]
    in_specs.append(pl.BlockSpec((N_CHIPS, None, Nw, cw), lambda i: (0, layer, 0, 0)))
    args = [a for a, _, _ in ysegs] + [wg]
    row = pl.BlockSpec((tm, Nw), lambda i: (i, 0))
    vec = pl.BlockSpec((1, Nw), lambda i: (0, 0))
    if norm is None:
        out_specs = row
        out_shape = jax.ShapeDtypeStruct((T, Nw), F32)
    else:
        in_specs += [row, vec, row]
        args += list(norm)
    in_specs += [ANY] * len(deps)
    args += list(deps)
    if norm is not None:
        out_specs = [row, row, vec]
        out_shape = [jax.ShapeDtypeStruct((T, Nw), F32), jax.ShapeDtypeStruct((T, Nw), BF16),
                     jax.ShapeDtypeStruct((1, Nw), F32)]
    return pl.pallas_call(
        body, name=name, grid=(T // tm,), in_specs=in_specs, out_specs=out_specs, out_shape=out_shape,
        scratch_shapes=[pltpu.VMEM((tm, Nw), F32)], compiler_params=_params(1))(*args)


def _tn(name, x, x_act, ysegs, cw, cols_layout, tmm, tt, deps=(), out_dtype=F32):
    T, M = x.shape
    widths = [bs[-1] for _, bs, _ in ysegs]
    N = sum(widths)
    pieces = _pieces(widths, cw if cols_layout else N, 1024)
    ns = len(ysegs)
    n_t = T // tt
    block = (N_CHIPS, tmm, cw) if cols_layout else (tmm, N)
    narrow = out_dtype != F32

    def body(x_ref, *refs):
        y_refs = refs[:ns]
        o_ref = refs[ns + len(deps)]
        acc = refs[-1] if narrow else o_ref

        @pl.when(pl.program_id(1) == 0)
        def _():
            acc[...] = jnp.zeros_like(acc)

        xt = x_act(x_ref[...])
        for s, a0, ch, b0, wd in pieces:
            d = _dot_tn(xt, y_refs[s][:, a0:a0 + wd].astype(BF16))
            if cols_layout:
                acc[ch, :, b0:b0 + wd] += d
            else:
                acc[:, b0:b0 + wd] += d
        if narrow:
            @pl.when(pl.program_id(1) == n_t - 1)
            def _():
                o_ref[...] = acc[...].astype(out_dtype)

    in_specs = [pl.BlockSpec((tt, tmm), lambda m, t: (t, m))] + [pl.BlockSpec(bs, im) for _, bs, im in ysegs]
    in_specs += [ANY] * len(deps)
    if cols_layout:
        out_specs = pl.BlockSpec(block, lambda m, t: (0, m, 0))
        out_shape = jax.ShapeDtypeStruct((N_CHIPS, M, cw), out_dtype)
    else:
        out_specs = pl.BlockSpec(block, lambda m, t: (m, 0))
        out_shape = jax.ShapeDtypeStruct((M, N), out_dtype)
    return pl.pallas_call(
        body, name=name, grid=(M // tmm, n_t), in_specs=in_specs, out_specs=out_specs, out_shape=out_shape,
        scratch_shapes=[pltpu.VMEM(block, F32)] if narrow else [],
        compiler_params=_params(2))(x, *[a for a, _, _ in ysegs], *deps)


def _seg2d(a, t_rows, grid_rank):
    w = a.shape[1]
    if grid_rank == 1:
        return (a, (t_rows, w), lambda i: (i, 0))
    return (a, (t_rows, w), lambda m, t: (t, 0))


def _kv_segments(dk, dv, C, t_rows, grid_rank):
    segs = []
    for g in range(len(PATTERNS)):
        for a in (dk, dv):
            if grid_rank == 1:
                segs.append((a, (t_rows, C), lambda i, g=g: (i, g)))
            else:
                segs.append((a, (t_rows, C), lambda m, t, g=g: (t, g)))
    return segs


def _seg_plane(a, plane, t_rows, grid_rank):
    w = a.shape[2]
    if grid_rank == 1:
        return (a, (None, t_rows, w), lambda i: (plane, i, 0))
    return (a, (None, t_rows, w), lambda m, t: (plane, t, 0))


def _row_tile(rows, row_bytes, budget_bytes=2 * 1024 * 1024):
    t = rows
    while t * row_bytes > budget_bytes and t % 32 == 0:
        t //= 2
    return t


N_DEVICES = 8


def _device_add(name, own, slots, place):
    _, _, hr, c = own.shape
    tr = _row_tile(hr, c * 4, 1024 * 1024)

    def body(place_ref, own_ref, *refs):
        o_ref = refs[-1]
        acc = own_ref[...].astype(F32)
        for r in refs[:-1]:
            acc = acc + r[...].astype(F32)
        o_ref[...] = acc

    def slot(k):
        return pl.BlockSpec((None, tr, c), lambda i, pr: ((2 * pr[0] + pr[1] + k) % N_DEVICES, i, 0))

    grid_spec = pltpu.PrefetchScalarGridSpec(
        num_scalar_prefetch=1, grid=(hr // tr,),
        in_specs=[pl.BlockSpec((None, None, tr, c), lambda i, pr: (pr[0], pr[1], i, 0))]
        + [slot(k) for k in range(1, N_DEVICES)],
        out_specs=pl.BlockSpec((None, tr, c), lambda i, pr: (pr[1], i, 0)))
    return pl.pallas_call(body, name=name, grid_spec=grid_spec,
                          out_shape=jax.ShapeDtypeStruct((2, hr, c), F32),
                          compiler_params=_params(1))(place, own, *[slots] * (N_DEVICES - 1))


def _adamw(name, w, g, m, v):
    rows, cols = w.shape
    tr = _row_tile(rows, cols * 4, 1024 * 1024)

    def body(w_ref, g_ref, m_ref, v_ref, d_ref, nm_ref, nv_ref):
        d_ref[...], nm_ref[...], nv_ref[...] = _adamw_math(w_ref[...], g_ref[...], m_ref[...], v_ref[...])

    spec = pl.BlockSpec((tr, cols), lambda i: (i, 0))
    return pl.pallas_call(
        body, name=name, grid=(rows // tr,), in_specs=[spec] * 4, out_specs=[spec] * 3,
        out_shape=[jax.ShapeDtypeStruct((rows, cols), F32)] * 3, compiler_params=_params(1))(w, g, m, v)


def _adamw_math(w, g, m, v):
    nm = ADAM_B1 * m + (1.0 - ADAM_B1) * g
    nv = ADAM_B2 * v + (1.0 - ADAM_B2) * jnp.square(g)
    m_hat = nm / (1.0 - ADAM_B1 ** ADAM_STEP)
    v_hat = nv / (1.0 - ADAM_B2 ** ADAM_STEP)
    return -ADAM_LR * (m_hat / (jnp.sqrt(v_hat) + ADAM_EPS) + ADAM_WD * w), nm, nv


def _adamw_layers(name, w, grads, m, v):
    L, r, c = w.shape
    tr = _row_tile(r, L * c * 4, 1024 * 1024)

    def body(*refs):
        w_ref, m_ref, v_ref = refs[:3]
        g_refs = refs[3:3 + L]
        go_ref, d_ref, nm_ref, nv_ref = refs[3 + L:]
        for l in range(L):
            g = g_refs[l][...]
            go_ref[l] = g
            d_ref[l], nm_ref[l], nv_ref[l] = _adamw_math(w_ref[l], g, m_ref[l], v_ref[l])

    stacked = pl.BlockSpec((L, tr, c), lambda i: (0, i, 0))
    return pl.pallas_call(
        body, name=name, grid=(r // tr,),
        in_specs=[stacked] * 3 + [pl.BlockSpec((tr, c), lambda i: (i, 0))] * L, out_specs=[stacked] * 4,
        out_shape=[jax.ShapeDtypeStruct((L, r, c), F32)] * 4, compiler_params=_params(1))(w, m, v, *grads)


def _place():
    x, y, c = lax.axis_index("x"), lax.axis_index("y"), lax.axis_index("c")
    chips = [(1 - x, y), (x, 1 - y), (1 - x, 1 - y)]
    return x, y, c, chips


HBM = pl.BlockSpec(memory_space=pltpu.HBM)
SEM = pl.BlockSpec(memory_space=pltpu.SEMAPHORE)
EFFECT = pltpu.SideEffectType.DATAFLOW_SIDE_EFFECTING


class _Copy:
    def __init__(self, src, src_view, land, dst_view, recv_view, target):
        self.src, self.src_view, self.land, self.dst_view, self.recv_view, self.target = (
            src, src_view, land, dst_view, recv_view, target)


def _whole(ref, place):
    return ref


def _split_start(name, srcs, land_shapes, plans, deps=()):
    skeys, lkeys = list(srcs), list(land_shapes)
    ns, nl, ng, nd = len(skeys), len(lkeys), len(plans), len(deps)

    def body(*refs):
        src = dict(zip(skeys, refs[:ns]))
        land = dict(zip(lkeys, refs[ns:ns + nl]))
        sems = refs[ns + nl + nd:ns + nl + nd + 2 * ng]
        token = refs[-1]
        place = _place()
        for gi, plan in enumerate(plans):
            for k, cp in enumerate(plan):
                dst = land[cp.land] if cp.land in land else src[cp.land]
                pltpu.make_async_remote_copy(
                    src_ref=cp.src_view(src[cp.src], place), dst_ref=cp.dst_view(dst, place),
                    send_sem=sems[2 * gi].at[k], recv_sem=sems[2 * gi + 1].at[k],
                    device_id=cp.target(place), device_id_type=MESH).start()
        token[...] = jnp.zeros_like(token)

    sem_shapes = []
    for plan in plans:
        sem_shapes += [pltpu.SemaphoreType.DMA((len(plan),))] * 2
    buffers = [srcs[k] for k in skeys] + [lax.empty(land_shapes[k].shape, land_shapes[k].dtype) for k in lkeys]
    outs = pl.pallas_call(
        body, name=name,
        out_shape=(*sem_shapes, *[pltpu.HBM(a.shape, a.dtype) for a in buffers], jax.ShapeDtypeStruct((8, LANES), F32)),
        in_specs=[HBM] * (ns + nl) + [ANY] * nd,
        out_specs=(*[SEM] * (2 * ng), *[HBM] * (ns + nl), pl.BlockSpec(memory_space=pltpu.VMEM)),
        input_output_aliases={i: 2 * ng + i for i in range(ns + nl)},
        compiler_params=pltpu.CompilerParams(has_side_effects=EFFECT),
    )(*[pltpu.with_memory_space_constraint(a, pltpu.HBM) for a in buffers], *deps)
    sems = [(outs[2 * gi], outs[2 * gi + 1]) for gi in range(ng)]
    thru = outs[2 * ng:2 * ng + ns + nl]
    return sems, dict(zip(skeys, thru[:ns])), dict(zip(lkeys, thru[ns:])), outs[-1]


def _split_wait(name, sems, srcs, lands, plan, after):
    skeys, lkeys = list(srcs), list(lands)
    ns, nl = len(skeys), len(lkeys)

    def body(*refs):
        src = dict(zip(skeys, refs[:ns]))
        land = dict(zip(lkeys, refs[ns:ns + nl]))
        ssem, rsem = refs[ns + nl], refs[ns + nl + 1]
        place = _place()
        for k, cp in enumerate(plan):
            dst = land[cp.land] if cp.land in land else src[cp.land]
            pltpu.make_async_remote_copy(
                src_ref=cp.src_view(src[cp.src], place), dst_ref=cp.dst_view(dst, place),
                send_sem=ssem.at[k], recv_sem=rsem.at[k],
                device_id=cp.target(place), device_id_type=MESH).wait_send()
            got = cp.recv_view(dst, place)
            pltpu.make_async_remote_copy(
                src_ref=got, dst_ref=got, send_sem=ssem.at[k], recv_sem=rsem.at[k],
                device_id=cp.target(place), device_id_type=MESH).wait_recv()

    buffers = [srcs[k] for k in skeys] + [lands[k] for k in lkeys]
    outs = pl.pallas_call(
        body, name=name, out_shape=tuple(pltpu.HBM(a.shape, a.dtype) for a in buffers),
        in_specs=(*[HBM] * (ns + nl), SEM, SEM, ANY), out_specs=tuple([HBM] * (ns + nl)),
        input_output_aliases={i: i for i in range(ns + nl)},
        compiler_params=pltpu.CompilerParams(has_side_effects=EFFECT),
    )(*buffers, sems[0], sems[1], after)
    return dict(zip(skeys, outs[:ns])), dict(zip(lkeys, outs[ns:]))


def _chip_of(place):
    x, y, c, chips = place
    return 2 * x + y


GATHER_FIRST = 2


class _WeightGather:
    def __init__(self, blocks):
        self.plans, shapes = {}, {}
        for key, a in blocks.items():
            shapes[key] = jax.ShapeDtypeStruct((N_CHIPS,) + a.shape, a.dtype)
            slot = lambda ref, place: ref.at[_chip_of(place)]
            plan = [_Copy(key, _whole, key, slot,
                          lambda ref, place, k=k: ref.at[2 * place[3][k][0] + place[3][k][1]],
                          lambda place, k=k: (place[3][k][0], place[3][k][1], place[2])) for k in range(3)]
            plan.append(_Copy(key, _whole, key, slot, slot, lambda place: (place[0], place[1], 1 - place[2])))
            self.plans[key] = plan
        keys = list(blocks)
        first, a = keys[0], blocks[keys[0]]
        hr = a.shape[0] // 2
        mine = lambda ref, place, q: ref.at[q, pl.ds(pl.multiple_of(place[2] * hr, 16), hr)]
        theirs = lambda ref, place, q: ref.at[q, pl.ds(pl.multiple_of((1 - place[2]) * hr, 16), hr)]
        chip = lambda place, k: 2 * place[3][k][0] + place[3][k][1]
        sibling = lambda place: (place[0], place[1], 1 - place[2])
        self.plans[first] = [
            _Copy(first, lambda ref, place: ref.at[pl.ds(pl.multiple_of(place[2] * hr, 16), hr)], first,
                  lambda ref, place: mine(ref, place, _chip_of(place)),
                  lambda ref, place, k=k: mine(ref, place, chip(place, k)),
                  lambda place, k=k: (place[3][k][0], place[3][k][1], place[2])) for k in range(3)]
        self.plans[first].append(_Copy(first, _whole, first, lambda ref, place: ref.at[_chip_of(place)],
                                       lambda ref, place: ref.at[_chip_of(place)], sibling))
        self.forward = [_Copy(first, lambda ref, place, k=k: mine(ref, place, chip(place, k)), first,
                              lambda ref, place, k=k: mine(ref, place, chip(place, k)),
                              lambda ref, place, k=k: theirs(ref, place, chip(place, k)), sibling) for k in range(3)]
        self.blocks, self.shapes = blocks, shapes
        self.sems, self.srcs, self.lands = {}, {}, {}
        self._start("gather_start_first", keys[:GATHER_FIRST], ())
        self.rest = keys[GATHER_FIRST:]

    def _start(self, name, part, deps):
        sems, srcs, lands, self.token = _split_start(name, {k: self.blocks[k] for k in part},
                                                     {k: self.shapes[k] for k in part}, [self.plans[k] for k in part], deps)
        self.sems.update(zip(part, sems))
        self.srcs.update(srcs)
        self.lands.update(lands)

    def get(self, l, name, after):
        key = (l, name)
        _, lands = _split_wait(f"gather_wait_{name}{l}", self.sems[key], {key: self.srcs[key]},
                               {key: self.lands[key]}, self.plans[key], after)
        if self.rest:
            sems, bufs, _, _ = _split_start("gather_forward", {key: lands[key]}, {}, [self.forward])
            lands, _ = _split_wait("gather_forward_wait", sems[0], bufs, {}, self.forward, after)
            self._start("gather_start", self.rest, [lands[key]])
            self.rest = []
        return lands[key][:, None]


class _GradReduce:
    def __init__(self, place):
        self.place = place
        self.jobs = []
        self.done = {}
        self.n = 0

    def submit(self, grads):
        views = {k: a.reshape(N_CHIPS, 2, a.shape[1] // 2, a.shape[2]) for k, a in grads.items()}
        shapes = {k: jax.ShapeDtypeStruct((N_DEVICES,) + a.shape[2:], a.dtype) for k, a in views.items()}

        def peer(place, k):
            x, y, c, _ = place
            return (1 - x if k & 4 else x, 1 - y if k & 2 else y, 1 - c if k & 1 else c)

        def index(dev):
            return 4 * dev[0] + 2 * dev[1] + dev[2]

        plan = []
        for key in views:
            for k in range(1, N_DEVICES):
                plan.append(_Copy(
                    key, lambda ref, place, k=k: ref.at[2 * peer(place, k)[0] + peer(place, k)[1], peer(place, k)[2]],
                    key, lambda ref, place: ref.at[index(place[:3])],
                    lambda ref, place, k=k: ref.at[index(peer(place, k))],
                    lambda place, k=k: peer(place, k)))
        sems, srcs, lands, token = _split_start(f"grad_start{self.n}", views, shapes, [plan])
        self.jobs.append(dict(id=self.n, sems=sems[0], srcs=srcs, lands=lands, plan=plan))
        self.n += 1
        return token

    def pump(self, after):
        return []

    def finish(self, after):
        for job in self.jobs:
            srcs, lands = _split_wait(f"grad_wait{job['id']}", job["sems"], job["srcs"], job["lands"], job["plan"],
                                      after)
            for i, k in enumerate(srcs):
                self.done[k] = _device_add(f"grad_add{job['id']}_{i}", srcs[k], lands[k], self.place)
        self.jobs = []
        return self.done


class _PairShare:
    def __init__(self, halves, types):
        sibling = lambda place: (place[0], place[1], 1 - place[2])
        mine = lambda ref, place: ref.at[place[2]]
        theirs = lambda ref, place: ref.at[1 - place[2]]
        self.plans = {t: [_Copy(k, mine, k, mine, theirs, sibling) for k in halves if k[0] == t] for t in types}
        sems, self.bufs, _, self.token = _split_start("share_start", halves, {}, list(self.plans.values()))
        self.sems = dict(zip(self.plans, sems))

    def get(self, t, after):
        keys = [cp.src for cp in self.plans[t]]
        bufs, _ = _split_wait(f"share_wait_{t}", self.sems[t], {k: self.bufs[k] for k in keys}, {}, self.plans[t], after)
        return bufs


def _small_allreduce(part):
    R, C = part.shape
    N_DEV = 8

    def body(in_ref, out_ref, slots, ssem, rsem):
        x, y, c, _ = _place()
        me = 4 * x + 2 * y + c
        sends = []
        for k in range(1, N_DEV):
            kx, ky, kc = (k >> 2) & 1, (k >> 1) & 1, k & 1
            peer = (1 - x if kx else x, 1 - y if ky else y, 1 - c if kc else c)
            cp = pltpu.make_async_remote_copy(
                src_ref=in_ref, dst_ref=slots.at[me], send_sem=ssem.at[k], recv_sem=rsem.at[k],
                device_id=peer, device_id_type=MESH)
            cp.start()
            sends.append(cp)
        slots[me] = in_ref[...]
        for k in range(1, N_DEV):
            kx, ky, kc = (k >> 2) & 1, (k >> 1) & 1, k & 1
            peer = (1 - x if kx else x, 1 - y if ky else y, 1 - c if kc else c)
            slot = slots.at[4 * peer[0] + 2 * peer[1] + peer[2]]
            pltpu.make_async_remote_copy(
                src_ref=slot, dst_ref=slot, send_sem=ssem.at[k], recv_sem=rsem.at[k],
                device_id=peer, device_id_type=MESH).wait_recv()
        acc = slots[0]
        for d in range(1, N_DEV):
            acc = acc + slots[d]
        out_ref[...] = acc
        for cp in sends:
            cp.wait_send()

    vm = pl.BlockSpec(memory_space=pltpu.VMEM)
    return pl.pallas_call(
        body, name="small_allreduce", in_specs=[vm], out_specs=vm,
        out_shape=jax.ShapeDtypeStruct((R, C), F32),
        scratch_shapes=[pltpu.VMEM((N_DEV, R, C), F32), pltpu.SemaphoreType.DMA((N_DEV,)),
                        pltpu.SemaphoreType.DMA((N_DEV,))])(part)


def _local_step(x, target, norm_mix, norm_mlp, norm_kv, norm_final, weights, sink, n_a, n_heads):
    B, S, D = x.shape
    T = B * S
    C = n_heads * HEAD_DIM
    depth = norm_mix.shape[0]
    slopes = 2.0 ** (-ALIBI_MAX_BIAS * jnp.arange(1, n_heads + 1, dtype=F32) / n_heads)
    tm = min(512, T)
    row = lambda v: v.reshape(1, -1)

    h = x.reshape(T, D)
    saved, Wl = [], []
    kv = nkv = h_kv = cwg = None
    for l in range(depth):
        s = {"h_in": h}
        w = {}
        Wl.append(w)
        if l < n_a:
            n_in = _rms_only("a_in_norm0", h, row(norm_mix[l]), tm) if l == 0 else h
            w["w_a_in"] = weights.get(l, "w_a_in", n_in)
            first = [weights.token] if l == 0 else []
            s["n1"], bcu = _norm_mm(f"a_in_fwd{l}", n_in, row(norm_mix[l]), w["w_a_in"], 0, 3, BF16, tm, first, l == 0)
            s["bcu"] = bcu.reshape(3, B, S, D)
            if l == 0:
                cwg = weights.get(0, "conv", bcu)[:, 0, :n_a * 3].reshape(N_CHIPS, n_a, 3, -1)
            s["z"] = _conv_fwd(f"conv_fwd{l}", s["bcu"], cwg, l, CONV_COLS).reshape(T, D)
            w["w_a_out"] = weights.get(l, "w_a_out", s["z"])
            h = _mm_res_rows(f"a_out_fwd{l}", s["z"], w["w_a_out"], 0, h, _to_bf16, tm)
        else:
            i = l - n_a
            if i == 0:
                h_kv = h
                w["w_kv"] = weights.get(l, "w_kv", h)
                nkv, kv = _norm_mm("kv_fwd", h, row(norm_kv), w["w_kv"], 0, 1, F32, tm)
                kv = kv.reshape(B, S, 2 * 3 * C)
            w["w_q"] = weights.get(l, "w_q", h)
            s["n1"], q = _norm_mm(f"q_fwd{i}", h, row(norm_mix[l]), w["w_q"], 0, 1, F32, tm)
            s["q"] = q.reshape(B, S, 3 * C)
            o, lse = _attn_fwd(f"attn_fwd{i}", s["q"], kv, slopes, n_heads)
            s["o"], s["lse"] = o.reshape(T, C), lse.reshape(T, C)
            w["w_o"] = weights.get(l, "w_o", o)
            h = _mm_res_cols(f"o_fwd{i}", s["o"], w["w_o"], 0, h, tm)
        s["h_mid"] = h
        w["w_up"] = weights.get(l, "w_up", h)
        if l < n_a:
            s["n2"], a = _norm_mm(f"up_fwd{l}", h, row(norm_mlp[l]), w["w_up"], 0, 1, BF16, tm)
            s["a"] = a[0]
            w["w_down"] = weights.get(l, "w_down", a)
            h = _mm_res_rows(f"down_fwd{l}", s["a"], w["w_down"], 0, h, _relu2_bf16, tm)
        else:
            w["w_down"] = weights.get(l, "w_down", h)
            s["n2"], s["a"], h = _mlp_fwd(f"mlp_fwd{l}", h, row(norm_mlp[l]), w["w_up"], w["w_down"], tm)
        F = s["a"].shape[1]
        saved.append(s)

    loss, dh, dh16, dg_final = _final_loss("loss_head", h, row(norm_final), target.reshape(T, D), tm)

    g_mix, g_mlp = [None] * depth, [None] * depth
    g_conv = [None] * n_a
    dkv = None
    tt = min(512, T)
    deps = []
    for l in reversed(range(depth)):
        s, w = saved[l], Wl[l]
        g_down = _tn(f"down_wgrad{l}", s["a"], _relu2_bf16, [_seg2d(dh16, tt, 2)], None, False,
                     min(2048, F), tt, deps, BF16).reshape(N_CHIPS, F // N_CHIPS, D)
        da, dh, dh16, g_mlp[l] = _mlp_bwd(f"mlp_bwd{l}", dh, dh16, s["a"], w["w_down"], w["w_up"], s["h_mid"],
                                          row(norm_mlp[l]), tm)
        g_up = _tn(f"up_wgrad{l}", s["n2"], _to_bf16, [_seg2d(da, tt, 2)], F // N_CHIPS, True, D, tt, (), BF16)
        deps = sink.pump(dh) + [sink.submit({("w_up", l): g_up, ("w_down", l): g_down})]
        if l < n_a:
            g_out = _tn(f"a_out_wgrad{l}", s["z"], _to_bf16, [_seg2d(dh16, tt, 2)], None, False,
                        D, tt, deps, BF16).reshape(N_CHIPS, D // N_CHIPS, D)
            dz = _nt_rows(f"a_out_bwd{l}", dh16, w["w_a_out"], 0, None, F32, tm)
            deps = sink.pump(dz) + [sink.submit({("w_a_out", l): g_out})]
            dbcu, g_conv[l] = _conv_bwd(f"conv_bwd{l}", s["bcu"], dz.reshape(B, S, D), cwg, l, CONV_COLS)
            dbcu = dbcu.reshape(3, T, D)
            g_in = _tn(f"a_in_wgrad{l}", s["n1"], _to_bf16, [_seg_plane(dbcu, p, tt, 2) for p in range(3)],
                       3 * D // N_CHIPS, True, D, tt, deps, BF16)
            deps = [sink.submit({("w_a_in", l): g_in})]
            dh, dh16, g_mix[l] = _nt_cols(f"a_in_bwd{l}", [_seg_plane(dbcu, p, tm, 1) for p in range(3)],
                                          w["w_a_in"], 0, tm, (s["h_in"], row(norm_mix[l]), dh), deps)
        else:
            i = l - n_a
            g_o = _tn(f"o_wgrad{i}", s["o"], _to_bf16, [_seg2d(dh16, tt, 2)], D // N_CHIPS, True, C, tt, deps,
                      BF16)
            do = _nt_cols(f"o_bwd{i}", [_seg2d(dh16, tm, 1)], w["w_o"], 0, tm, None)
            deps = sink.pump(do) + [sink.submit({("w_o", i): g_o})]
            dq, dk, dv = _attn_bwd(f"attn_bwd{i}", s["q"], kv, slopes, s["o"].reshape(B, S, C),
                                   s["lse"].reshape(B, S, C), do.reshape(B, S, C), n_heads, dkv)
            dkv = (dk, dv)
            dq = dq.reshape(T, 3 * C)
            g_q = _tn(f"q_wgrad{i}", s["n1"], _to_bf16, [_seg2d(dq, tt, 2)], 3 * C // N_CHIPS, True, D, tt, deps,
                      BF16)
            mixer = {("w_q", i): g_q}
            if i == 0:
                dk2, dv2 = (t.reshape(T, 3 * C) for t in dkv)
                mixer[("w_kv", 0)] = _tn("kv_wgrad", nkv, _to_bf16, _kv_segments(dk2, dv2, C, tt, 2),
                                         6 * C // N_CHIPS, True, D, tt, (), BF16)
            deps = [sink.submit(mixer)]
            dh, dh16, g_mix[l] = _nt_cols(f"q_bwd{i}", [_seg2d(dq, tm, 1)], w["w_q"], 0, tm,
                                          (s["h_in"], row(norm_mix[l]), dh), deps)
            if i == 0:
                dh, dh16, g_kv = _nt_cols("kv_bwd", _kv_segments(dk2, dv2, C, tm, 1), w["w_kv"], 0, tm,
                                          (h_kv, row(norm_kv), dh))
        deps = sink.pump(dh)
    small = dict(norm_mix=jnp.concatenate(g_mix, axis=0), norm_mlp=jnp.concatenate(g_mlp, axis=0),
                 norm_kv=g_kv, norm_final=dg_final, conv_w=jnp.stack(g_conv))
    return loss, dh.reshape(B, S, D), small


BIG = ("w_a_in", "w_a_out", "w_kv", "w_q", "w_o", "w_up", "w_down")
CONV_PAD_ROWS = 16


def kernel(x, norm_mix, norm_mlp, w_a_in, conv_w, w_a_out, norm_kv, w_kv, w_q, w_o, w_up, w_down, norm_final, loss_target, m_norm_mix, m_norm_mlp, m_w_a_in, m_conv_w, m_w_a_out, m_norm_kv, m_w_kv, m_w_q, m_w_o, m_w_up, m_w_down, m_norm_final, v_norm_mix, v_norm_mlp, v_w_a_in, v_conv_w, v_w_a_out, v_norm_kv, v_w_kv, v_w_q, v_w_o, v_w_up, v_w_down, v_norm_final):
    D = x.shape[-1]
    w = dict(norm_mix=norm_mix, norm_mlp=norm_mlp, w_a_in=w_a_in, conv_w=conv_w, w_a_out=w_a_out, norm_kv=norm_kv,
             w_kv=w_kv[None], w_q=w_q, w_o=w_o, w_up=w_up, w_down=w_down, norm_final=norm_final)
    m = dict(norm_mix=m_norm_mix, norm_mlp=m_norm_mlp, w_a_in=m_w_a_in, conv_w=m_conv_w, w_a_out=m_w_a_out,
             norm_kv=m_norm_kv, w_kv=m_w_kv[None], w_q=m_w_q, w_o=m_w_o, w_up=m_w_up, w_down=m_w_down,
             norm_final=m_norm_final)
    v = dict(norm_mix=v_norm_mix, norm_mlp=v_norm_mlp, w_a_in=v_w_a_in, conv_w=v_conv_w, w_a_out=v_w_a_out,
             norm_kv=v_norm_kv, w_kv=v_w_kv[None], w_q=v_w_q, w_o=v_w_o, w_up=v_w_up, w_down=v_w_down,
             norm_final=v_norm_final)
    depth = norm_mix.shape[0]
    n_a, taps, cwc = conv_w.shape
    n_heads = w_o.shape[1] // HEAD_DIM

    conv_rows = jnp.zeros((CONV_PAD_ROWS, cwc), F32).at[:n_a * taps].set(conv_w.reshape(n_a * taps, cwc))
    blocks = {}
    for l in range(depth):
        if l < n_a:
            blocks[(l, "w_a_in")] = w_a_in[l].astype(BF16)
            if l == 0:
                blocks[(0, "conv")] = conv_rows
            blocks[(l, "w_a_out")] = w_a_out[l].astype(BF16)
        else:
            if l == n_a:
                blocks[(l, "w_kv")] = w_kv.astype(BF16)
            blocks[(l, "w_q")] = w_q[l - n_a].astype(BF16)
            blocks[(l, "w_o")] = w_o[l - n_a].astype(BF16)
        blocks[(l, "w_up")] = w_up[l].astype(BF16)
        blocks[(l, "w_down")] = w_down[l].astype(BF16)
    weights = _WeightGather(blocks)
    place = jnp.stack([2 * lax.axis_index("x") + lax.axis_index("y"), lax.axis_index("c")]).astype(jnp.int32)
    sink = _GradReduce(place)

    loss, grad_x, small = _local_step(x, loss_target, norm_mix, norm_mlp, norm_kv, norm_final, weights, sink,
                                      n_a, n_heads)
    loss = lax.psum(loss[0, 0], ("x", "y", "c"))

    share = _PairShare(sink.finish(grad_x), BIG)
    grads = {}

    packed = jnp.concatenate([small["norm_mix"], small["norm_mlp"], small["norm_kv"], small["norm_final"],
                              small["conv_w"].reshape(n_a * taps, D)], axis=0)
    pad = (-packed.shape[0]) % 8
    packed = jnp.pad(packed, ((0, pad), (0, 0)))
    total = _small_allreduce(packed)
    grads["norm_mix"] = total[:depth]
    grads["norm_mlp"] = total[depth:2 * depth]
    grads["norm_kv"] = total[2 * depth]
    grads["norm_final"] = total[2 * depth + 1]
    chip = 2 * lax.axis_index("x") + lax.axis_index("y")
    conv_full = total[2 * depth + 2:2 * depth + 2 + n_a * taps].reshape(n_a, taps, N_CHIPS, cwc)
    grads["conv_w"] = lax.dynamic_index_in_dim(conv_full, chip, axis=2, keepdims=False)

    order = ("norm_mix", "norm_mlp", "w_a_in", "conv_w", "w_a_out", "norm_kv", "w_kv", "w_q", "w_o", "w_up",
             "w_down", "norm_final")
    delta, new_m, new_v = {}, {}, {}
    vec_names = ("norm_mix", "norm_mlp", "norm_kv", "norm_final")
    rows_of = lambda a: a.reshape(-1, D)
    vw, vg, vm_, vv = (jnp.concatenate([rows_of(t[k]) for k in vec_names], axis=0) for t in (w, grads, m, v))
    vpad = (-vw.shape[0]) % 8
    padrows = lambda a: jnp.pad(a, ((0, vpad), (0, 0)))
    vd, vnm, vnv = _adamw("adamw_norms", padrows(vw), padrows(vg), padrows(vm_), padrows(vv))
    off = 0
    for k in vec_names:
        r = rows_of(w[k]).shape[0]
        delta[k] = vd[off:off + r].reshape(w[k].shape)
        new_m[k] = vnm[off:off + r].reshape(w[k].shape)
        new_v[k] = vnv[off:off + r].reshape(w[k].shape)
        off += r
    cpad = (-n_a * taps) % 8
    two_d = lambda a: jnp.pad(a.reshape(-1, cwc), ((0, cpad), (0, 0)))
    cd, cnm, cnv = _adamw("adamw_conv_w", two_d(w["conv_w"]), two_d(grads["conv_w"]), two_d(m["conv_w"]),
                          two_d(v["conv_w"]))
    delta["conv_w"], new_m["conv_w"], new_v["conv_w"] = (t[:n_a * taps].reshape(conv_w.shape) for t in (cd, cnm, cnv))
    after = cd
    for k in sorted(BIG, key=lambda k: w[k].size):
        shared = share.get(k, after)
        per_layer = [shared[(k, l)].reshape(w[k].shape[1:]) for l in range(w[k].shape[0])]
        grads[k], delta[k], new_m[k], new_v[k] = _adamw_layers(f"adamw_{k}", w[k], per_layer, m[k], v[k])
        after = delta[k]
    fix = lambda k, a: a[0] if k == "w_kv" else a
    return (loss, grad_x, *[fix(k, grads[k]) for k in order], *[fix(k, delta[k]) for k in order],
            *[fix(k, new_m[k]) for k in order], *[fix(k, new_v[k]) for k in order])
```

```python
import jax
import jax.numpy as jnp
from jax import lax
from jax.experimental import pallas as pl
from jax.experimental.pallas import tpu as pltpu

F32 = jnp.float32
BF16 = jnp.bfloat16
MESH = pl.DeviceIdType.MESH

EPS = 1e-5
PATTERNS = ((128, 1), (512, 4), (2048, 16))
HEAD_DIM = 64
ALIBI_MAX_BIAS = 8.0
NEG_INF = -1e30
ATT_BLK = 128
BWD_UNROLL = 32
N_CHIPS = 4
LANES = 128
VMEM_LIMIT = 56 * 1024 * 1024

ADAM_LR = 0.001
ADAM_B1 = 0.9
ADAM_B2 = 0.999
ADAM_EPS = 1e-08
ADAM_WD = 0.01
ADAM_STEP = 10


ANY = pl.BlockSpec(memory_space=pl.ANY)


def _params(n_grid_axes):
    return pltpu.CompilerParams(dimension_semantics=("arbitrary",) * n_grid_axes, vmem_limit_bytes=VMEM_LIMIT)


def _dot(a, b):
    return jnp.dot(a, b, preferred_element_type=F32)


def _dot_nt(a, b):
    return lax.dot_general(a, b, (((1,), (1,)), ((), ())), preferred_element_type=F32)


def _dot_tn(a, b):
    return lax.dot_general(a, b, (((0,), (0,)), ((), ())), preferred_element_type=F32)


def _relu2(a):
    return jnp.square(jnp.maximum(a, 0.0))


def _rms(hf, g):
    y = hf * lax.rsqrt(jnp.mean(hf * hf, axis=-1, keepdims=True) + EPS)
    return y * g


def _rms_bwd(hf, g, dn):
    rstd = lax.rsqrt(jnp.mean(hf * hf, axis=-1, keepdims=True) + EPS)
    xhat = hf * rstd
    dg = jnp.sum(dn * xhat, axis=0, keepdims=True)
    dx = dn * g
    dh = rstd * (dx - xhat * jnp.mean(dx * xhat, axis=-1, keepdims=True))
    return dh, dg


def _pieces(seg_widths, chunk_width, max_width):
    total = sum(seg_widths)
    cuts = {0, total}
    acc = 0
    for w in seg_widths:
        cuts.add(acc)
        acc += w
    cuts.update(range(0, total, chunk_width))
    cuts = sorted(cuts)
    fine = []
    for lo, hi in zip(cuts[:-1], cuts[1:]):
        while hi - lo > max_width:
            fine.append((lo, lo + max_width))
            lo += max_width
        fine.append((lo, hi))
    out = []
    for lo, hi in fine:
        acc = 0
        for s, w in enumerate(seg_widths):
            if lo < acc + w:
                break
            acc += w
        out.append((s, lo - acc, lo // chunk_width, lo % chunk_width, hi - lo))
    return out


def _relu2_bf16(a):
    return _relu2(a.astype(F32)).astype(BF16)


def _to_bf16(a):
    return a.astype(BF16)


def _rms_only(name, h, g, tm):
    T, D = h.shape

    def body(h_ref, g_ref, n_ref):
        n_ref[...] = _rms(h_ref[...], g_ref[...]).astype(BF16)

    row = pl.BlockSpec((tm, D), lambda i: (i, 0))
    return pl.pallas_call(
        body, name=name, grid=(T // tm,), in_specs=[row, pl.BlockSpec((1, D), lambda i: (0, 0))], out_specs=row,
        out_shape=jax.ShapeDtypeStruct((T, D), BF16), compiler_params=_params(1))(h, g)


def _norm_mm(name, h, g, wg, layer, planes, out_dtype, tm, deps=(), normed=False):
    T, D = h.shape
    cw = wg.shape[3]
    N = N_CHIPS * cw
    pw = N // planes
    pieces = _pieces([pw] * planes, cw, 512)

    def body(h_ref, g_ref, w_ref, *rest):
        outs = rest[len(deps):]
        o_ref = outs[-1]
        if normed:
            n = h_ref[...]
        else:
            n = _rms(h_ref[...], g_ref[...]).astype(BF16)
            outs[0][...] = n
        for s, a0, ch, b0, wd in pieces:
            o_ref[s, :, a0:a0 + wd] = _dot(n, w_ref[ch, :, b0:b0 + wd]).astype(out_dtype)

    out_specs = [pl.BlockSpec((tm, D), lambda i: (i, 0)), pl.BlockSpec((planes, tm, pw), lambda i: (0, i, 0))]
    out_shape = [jax.ShapeDtypeStruct((T, D), BF16), jax.ShapeDtypeStruct((planes, T, pw), out_dtype)]
    outs = pl.pallas_call(
        body, name=name, grid=(T // tm,),
        in_specs=[pl.BlockSpec((tm, D), lambda i: (i, 0)),
                  pl.BlockSpec((1, D), lambda i: (0, 0)),
                  pl.BlockSpec((N_CHIPS, None, D, cw), lambda i: (0, layer, 0, 0))] + [ANY] * len(deps),
        out_specs=out_specs[normed:], out_shape=out_shape[normed:],
        compiler_params=_params(1))(h, g, wg, *deps)
    return (h, outs[0]) if normed else tuple(outs)


def _resident(shape, index_map):
    return pl.BlockSpec(shape, index_map, pipeline_mode=pl.Buffered(1))


def _mm_res_rows(name, a, wg, layer, h, act, tm):
    T = a.shape[0]
    rk, D = wg.shape[2], wg.shape[3]

    def body(a_ref, w_ref, h_ref, o_ref):
        acc = h_ref[...]
        for k in range(N_CHIPS):
            acc = acc + _dot(act(a_ref[:, k * rk:(k + 1) * rk]), w_ref[k])
        o_ref[...] = acc

    return pl.pallas_call(
        body, name=name, grid=(T // tm,),
        in_specs=[pl.BlockSpec((tm, N_CHIPS * rk), lambda i: (i, 0)),
                  pl.BlockSpec((N_CHIPS, None, rk, D), lambda i: (0, layer, 0, 0)),
                  pl.BlockSpec((tm, D), lambda i: (i, 0))],
        out_specs=pl.BlockSpec((tm, D), lambda i: (i, 0)),
        out_shape=jax.ShapeDtypeStruct((T, D), F32),
        compiler_params=_params(1))(a, wg, h)


def _mm_res_cols(name, a, wg, layer, h, tm):
    T, K = a.shape
    cw = wg.shape[3]
    D = N_CHIPS * cw

    def body(a_ref, w_ref, h_ref, o_ref):
        a16 = a_ref[...].astype(BF16)
        for j in range(N_CHIPS):
            o_ref[:, j * cw:(j + 1) * cw] = h_ref[:, j * cw:(j + 1) * cw] + _dot(a16, w_ref[j])

    return pl.pallas_call(
        body, name=name, grid=(T // tm,),
        in_specs=[pl.BlockSpec((tm, K), lambda i: (i, 0)),
                  pl.BlockSpec((N_CHIPS, None, K, cw), lambda i: (0, layer, 0, 0)),
                  pl.BlockSpec((tm, D), lambda i: (i, 0))],
        out_specs=pl.BlockSpec((tm, D), lambda i: (i, 0)),
        out_shape=jax.ShapeDtypeStruct((T, D), F32),
        compiler_params=_params(1))(a, wg, h)


def _mlp_fwd(name, h, g, wup, wdown, tm):
    T, D = h.shape
    cw = wup.shape[3]

    def body(h_ref, g_ref, wu_ref, wd_ref, n_ref, a_ref, o_ref):
        hf = h_ref[...]
        n = _rms(hf, g_ref[...]).astype(BF16)
        n_ref[...] = n
        acc = hf
        for ch in range(N_CHIPS):
            a16 = _dot(n, wu_ref[ch]).astype(BF16)
            a_ref[:, ch * cw:(ch + 1) * cw] = a16
            acc = acc + _dot(_relu2_bf16(a16), wd_ref[ch])
        o_ref[...] = acc

    row = pl.BlockSpec((tm, D), lambda i: (i, 0))
    return pl.pallas_call(
        body, name=name, grid=(T // tm,),
        in_specs=[row, pl.BlockSpec((1, D), lambda i: (0, 0)),
                  _resident((N_CHIPS, None, D, cw), lambda i: (0, 0, 0, 0)),
                  _resident((N_CHIPS, None, cw, D), lambda i: (0, 0, 0, 0))],
        out_specs=[row, pl.BlockSpec((tm, N_CHIPS * cw), lambda i: (i, 0)), row],
        out_shape=[jax.ShapeDtypeStruct((T, D), BF16), jax.ShapeDtypeStruct((T, N_CHIPS * cw), BF16),
                   jax.ShapeDtypeStruct((T, D), F32)],
        compiler_params=_params(1))(h, g, wup, wdown)


def _mlp_bwd(name, dh, dh16, a, wdown, wup, h_mid, g, tm, deps=()):
    T, D = dh.shape
    cw = wup.shape[3]
    F = N_CHIPS * cw

    def body(dh_ref, dh16_ref, a_ref, wd_ref, wu_ref, h_ref, g_ref, *rest):
        da_ref, out_ref, out16_ref, dg_ref = rest[len(deps):]
        d16 = dh16_ref[...]
        acc = None
        for ch in range(N_CHIPS):
            cols = slice(ch * cw, (ch + 1) * cw)
            da = (_dot_nt(d16, wd_ref[ch]) * (2.0 * jnp.maximum(a_ref[:, cols].astype(F32), 0.0))).astype(BF16)
            da_ref[:, cols] = da
            d = _dot_nt(da, wu_ref[ch])
            acc = d if acc is None else acc + d
        dh_c, dg = _rms_bwd(h_ref[...], g_ref[...], acc)
        out = dh_ref[...] + dh_c
        out_ref[...] = out
        out16_ref[...] = out.astype(BF16)

        @pl.when(pl.program_id(0) == 0)
        def _():
            dg_ref[...] = dg

        @pl.when(pl.program_id(0) > 0)
        def _():
            dg_ref[...] += dg

    row = pl.BlockSpec((tm, D), lambda i: (i, 0))
    wide = pl.BlockSpec((tm, F), lambda i: (i, 0))
    vec = pl.BlockSpec((1, D), lambda i: (0, 0))
    return pl.pallas_call(
        body, name=name, grid=(T // tm,),
        in_specs=[row, row, wide, _resident((N_CHIPS, None, cw, D), lambda i: (0, 0, 0, 0)),
                  _resident((N_CHIPS, None, D, cw), lambda i: (0, 0, 0, 0)), row, vec] + [ANY] * len(deps),
        out_specs=[wide, row, row, vec],
        out_shape=[jax.ShapeDtypeStruct((T, F), BF16), jax.ShapeDtypeStruct((T, D), F32),
                   jax.ShapeDtypeStruct((T, D), BF16), jax.ShapeDtypeStruct((1, D), F32)],
        compiler_params=_params(1))(dh, dh16, a, wdown, wup, h_mid, g, *deps)


CONV_ROWS = 256
CONV_HALO = 16
CONV_COLS = 2 * LANES


def _conv_shifted(ext, k, r0, rows, at_start):
    rolled = pltpu.roll(ext, k, 0)[CONV_HALO:]
    if not at_start:
        return rolled
    t = r0 + lax.broadcasted_iota(jnp.int32, rolled.shape, 0)
    return jnp.where(t >= k, rolled, 0.0)


def _conv_ahead(ext, k, r0, rows, S, at_end):
    rolled = pltpu.roll(ext, rows + CONV_HALO - k, 0)[:rows]
    if not at_end:
        return rolled
    t = r0 + lax.broadcasted_iota(jnp.int32, rolled.shape, 0)
    return jnp.where(t + k < S, rolled, 0.0)


def _conv_chunks(step, n, carry):
    carry = step(0, carry, True, n == 1)
    if n > 2:
        carry = lax.fori_loop(1, n - 1, lambda i, c: step(i, c, False, False), carry)
    if n > 1:
        carry = step(n - 1, carry, False, True)
    return carry


def _conv_fwd(name, bcu, cwg, layer, tc):
    _, B, S, D = bcu.shape
    cwc = cwg.shape[3]
    per_chunk = cwc // tc
    R = min(CONV_ROWS, S)

    def body(x_ref, w_ref, z_ref):
        w = [w_ref[k:k + 1, :] for k in range(3)]

        def step(i, carry, at_start, at_end):
            r0 = pl.multiple_of(i * R, R)
            h0 = pl.multiple_of(jnp.maximum(r0 - CONV_HALO, 0), CONV_HALO)
            ld = lambda p, start, rows: x_ref[p, pl.ds(start, rows), :].astype(F32)
            cu = jnp.concatenate([ld(1, h0, CONV_HALO) * ld(2, h0, CONV_HALO), ld(1, r0, R) * ld(2, r0, R)], axis=0)
            conv = w[0] * cu[CONV_HALO:]
            conv = conv + w[1] * _conv_shifted(cu, 1, r0, R, at_start)
            conv = conv + w[2] * _conv_shifted(cu, 2, r0, R, at_start)
            z_ref[pl.ds(r0, R), :] = (ld(0, r0, R) * conv).astype(BF16)
            return carry

        _conv_chunks(step, S // R, 0)

    return pl.pallas_call(
        body, name=name, grid=(B, D // tc),
        in_specs=[pl.BlockSpec((3, None, S, tc), lambda b, j: (0, b, 0, j)),
                  pl.BlockSpec((None, None, 3, tc), lambda b, j: (j // per_chunk, layer, 0, j % per_chunk))],
        out_specs=pl.BlockSpec((None, S, tc), lambda b, j: (b, 0, j)),
        out_shape=jax.ShapeDtypeStruct((B, S, D), BF16),
        compiler_params=_params(2))(bcu, cwg)


def _conv_bwd(name, bcu, dz, cwg, layer, tc):
    _, B, S, D = bcu.shape
    cwc = cwg.shape[3]
    per_chunk = cwc // tc
    R = min(CONV_ROWS, S)

    def body(x_ref, dz_ref, w_ref, d_ref, dw_ref):
        w = [w_ref[k:k + 1, :] for k in range(3)]

        @pl.when(pl.program_id(1) == 0)
        def _():
            dw_ref[...] = jnp.zeros_like(dw_ref)

        def step(i, carry, at_start, at_end):
            r0 = pl.multiple_of(i * R, R)
            h0 = pl.multiple_of(jnp.maximum(r0 - CONV_HALO, 0), CONV_HALO)
            a0 = pl.multiple_of(jnp.minimum(r0 + R, S - CONV_HALO), CONV_HALO)
            ld = lambda p, start, rows: x_ref[p, pl.ds(start, rows), :].astype(F32)
            b, c, u = ld(0, r0, R), ld(1, r0, R), ld(2, r0, R)
            dz = dz_ref[pl.ds(r0, R), :]
            cu = jnp.concatenate([ld(1, h0, CONV_HALO) * ld(2, h0, CONV_HALO), c * u], axis=0)
            cu1 = _conv_shifted(cu, 1, r0, R, at_start)
            cu2 = _conv_shifted(cu, 2, r0, R, at_start)
            conv = w[0] * (c * u) + w[1] * cu1 + w[2] * cu2
            dconv = dz * b
            dca = jnp.concatenate([dconv, dz_ref[pl.ds(a0, CONV_HALO), :] * ld(0, a0, CONV_HALO)], axis=0)
            dcu = (w[0] * dconv + w[1] * _conv_ahead(dca, 1, r0, R, S, at_end)
                   + w[2] * _conv_ahead(dca, 2, r0, R, S, at_end))
            d_ref[0, pl.ds(r0, R), :] = (dz * conv).astype(BF16)
            d_ref[1, pl.ds(r0, R), :] = (dcu * u).astype(BF16)
            d_ref[2, pl.ds(r0, R), :] = (dcu * c).astype(BF16)
            return (carry[0] + jnp.sum(dconv * (c * u), axis=0, keepdims=True),
                    carry[1] + jnp.sum(dconv * cu1, axis=0, keepdims=True),
                    carry[2] + jnp.sum(dconv * cu2, axis=0, keepdims=True))

        zero = jnp.zeros((1, tc), F32)
        s0, s1, s2 = _conv_chunks(step, S // R, (zero, zero, zero))
        for k, sk in enumerate((s0, s1, s2)):
            dw_ref[k:k + 1, :] += sk

    return pl.pallas_call(
        body, name=name, grid=(D // tc, B),
        in_specs=[pl.BlockSpec((3, None, S, tc), lambda j, b: (0, b, 0, j)),
                  pl.BlockSpec((None, S, tc), lambda j, b: (b, 0, j)),
                  pl.BlockSpec((None, None, 3, tc), lambda j, b: (j // per_chunk, layer, 0, j % per_chunk))],
        out_specs=[pl.BlockSpec((3, None, S, tc), lambda j, b: (0, b, 0, j)),
                   pl.BlockSpec((3, tc), lambda j, b: (0, j))],
        out_shape=[jax.ShapeDtypeStruct((3, B, S, D), BF16), jax.ShapeDtypeStruct((3, D), F32)],
        compiler_params=_params(2))(bcu, dz, cwg)


def _att_rows(dil, idx, nb):
    r, n = idx // nb, idx % nb
    if dil == 1:
        cur = pl.ds(pl.multiple_of(n * ATT_BLK, ATT_BLK), ATT_BLK)
        prev = pl.ds(pl.multiple_of(jnp.maximum(n - 1, 0) * ATT_BLK, ATT_BLK), ATT_BLK)
    else:
        cur = pl.ds(n * (ATT_BLK * dil) + r, ATT_BLK, stride=dil)
        prev = pl.ds(jnp.maximum(n - 1, 0) * (ATT_BLK * dil) + r, ATT_BLK, stride=dil)
    return n, cur, prev


def _att_bias(bias_ref, dil, sl_ref, hp):
    row = lax.broadcasted_iota(jnp.int32, (2 * ATT_BLK, 2 * ATT_BLK), 0)
    ci = lax.broadcasted_iota(jnp.int32, (2 * ATT_BLK, 2 * ATT_BLK), 1)
    j = ATT_BLK + (row & (ATT_BLK - 1)) - ci
    slope = jnp.where(row < ATT_BLK, sl_ref[2 * hp], sl_ref[2 * hp + 1])
    rest = jnp.where((j >= 0) & (j <= ATT_BLK), -slope * (dil * j).astype(F32), NEG_INF)
    bias_ref[1] = rest
    bias_ref[0] = jnp.where(ci >= ATT_BLK, rest, NEG_INF)


def _stack_heads(x16, lane):
    first = lane < HEAD_DIM
    return jnp.concatenate([jnp.where(first, x16, jnp.zeros_like(x16)),
                            jnp.where(first, jnp.zeros_like(x16), x16)], axis=0)


def _per_head(col, lane):
    return jnp.where(lane < HEAD_DIM, col[:ATT_BLK], col[ATT_BLK:])


def _attn_fwd(name, q, kv, slopes, n_heads):
    B, S, CQ = q.shape
    HP = n_heads * HEAD_DIM // LANES
    scale = HEAD_DIM ** -0.5
    n_groups = len(PATTERNS)
    CH = 256

    def body(sl_ref, q_ref, k_ref, v_ref, o_ref, lse_ref, bias_ref, *parts):
        og, lg = parts[:n_groups], parts[n_groups:]
        hp, g = pl.program_id(1), pl.program_id(2)
        lane = lax.broadcasted_iota(jnp.int32, (1, LANES), 1)

        for gi, (window, dil) in enumerate(PATTERNS):
            nb = S // dil // ATT_BLK

            @pl.when(g == gi)
            def _(gi=gi, dil=dil, nb=nb):
                _att_bias(bias_ref, dil, sl_ref, hp)

                def step(idx, carry):
                    n, cur, prev = _att_rows(dil, idx, nb)
                    qs = _stack_heads((q_ref[cur, :] * scale).astype(BF16), lane)
                    kc = jnp.concatenate([k_ref[prev, :], k_ref[cur, :]], axis=0).astype(BF16)
                    vc = jnp.concatenate([v_ref[prev, :], v_ref[cur, :]], axis=0).astype(BF16)
                    s = _dot_nt(qs, kc) + bias_ref[jnp.minimum(n, 1)]
                    m = jnp.max(s, axis=-1, keepdims=True)
                    p = jnp.exp(s - m)
                    l = jnp.sum(p, axis=-1, keepdims=True)
                    p16 = p.astype(BF16)
                    o_un = _dot(jnp.concatenate([p16[:ATT_BLK], p16[ATT_BLK:]], axis=1), _stack_heads_rows(vc, lane))
                    og[gi][cur, :] = o_un / _per_head(l, lane)
                    lg[gi][cur, :] = _per_head(m + jnp.log(l), lane)
                    return carry

                lax.fori_loop(0, S // ATT_BLK, step, 0, unroll=S // ATT_BLK)

        @pl.when(g == n_groups - 1)
        def _():
            def comb(i, carry):
                rows = pl.ds(pl.multiple_of(i * CH, CH), CH)
                a, b, c = lg[0][rows, :], lg[1][rows, :], lg[2][rows, :]
                m = jnp.maximum(jnp.maximum(a, b), c)
                ea, eb, ec = jnp.exp(a - m), jnp.exp(b - m), jnp.exp(c - m)
                z = ea + eb + ec
                o_ref[rows, :] = (ea / z) * og[0][rows, :] + (eb / z) * og[1][rows, :] + (ec / z) * og[2][rows, :]
                lse_ref[rows, :] = m + jnp.log(z)
                return carry

            lax.fori_loop(0, S // CH, comb, 0)

    blk = (None, S, LANES)
    out = pl.BlockSpec(blk, lambda b, hp, g: (b, 0, hp))
    return pl.pallas_call(
        body, name=name, grid=(B, HP, n_groups),
        in_specs=[pl.BlockSpec(memory_space=pltpu.SMEM),
                  pl.BlockSpec(blk, lambda b, hp, g: (b, 0, g * HP + hp)),
                  pl.BlockSpec(blk, lambda b, hp, g: (b, 0, g * 2 * HP + hp)),
                  pl.BlockSpec(blk, lambda b, hp, g: (b, 0, g * 2 * HP + HP + hp))],
        out_specs=[out, out],
        out_shape=[jax.ShapeDtypeStruct((B, S, HP * LANES), F32)] * 2,
        scratch_shapes=[pltpu.VMEM((2, 2 * ATT_BLK, 2 * ATT_BLK), F32)] + [pltpu.VMEM((S, LANES), F32)] * (2 * n_groups),
        compiler_params=_params(3))(slopes, q, kv, kv)


def _stack_heads_rows(x16, lane):
    first = lane < HEAD_DIM
    return jnp.concatenate([jnp.where(first, x16, jnp.zeros_like(x16)),
                            jnp.where(first, jnp.zeros_like(x16), x16)], axis=0)


def _attn_bwd(name, q, kv, slopes, o, lse, do, n_heads, dkv_prev):
    B, S, CQ = q.shape
    HP = n_heads * HEAD_DIM // LANES
    scale = HEAD_DIM ** -0.5
    n_groups = len(PATTERNS)
    n_prev = 0 if dkv_prev is None else 2

    def body(sl_ref, q_ref, k_ref, v_ref, o_ref, lse_ref, do_ref, *rest):
        dq_ref, dk_ref, dv_ref, bias_ref = rest[n_prev:]
        hp, g = pl.program_id(1), pl.program_id(2)
        lane = lax.broadcasted_iota(jnp.int32, (1, LANES), 1)
        first = lane < HEAD_DIM

        def flush(rows, dk, dv):
            if n_prev:
                dk = dk + rest[0][rows, :]
                dv = dv + rest[1][rows, :]
            dk_ref[rows, :] = dk
            dv_ref[rows, :] = dv

        for gi, (window, dil) in enumerate(PATTERNS):
            nb = S // dil // ATT_BLK
            n_blocks = S // ATT_BLK

            @pl.when(g == gi)
            def _(dil=dil, nb=nb, n_blocks=n_blocks):
                _att_bias(bias_ref, dil, sl_ref, hp)

                def block(idx, carry, first_of_all):
                    n, cur, prev = _att_rows(dil, idx, nb)
                    qs = _stack_heads((q_ref[cur, :] * scale).astype(BF16), lane)
                    kc = jnp.concatenate([k_ref[prev, :], k_ref[cur, :]], axis=0).astype(BF16)
                    vc = jnp.concatenate([v_ref[prev, :], v_ref[cur, :]], axis=0).astype(BF16)
                    dob = do_ref[cur, :]
                    prod = dob * o_ref[cur, :]
                    lseb = lse_ref[cur, :]
                    dos = _stack_heads(dob.astype(BF16), lane)
                    delta = jnp.concatenate(
                        [jnp.sum(jnp.where(first, prod, 0.0), axis=-1, keepdims=True),
                         jnp.sum(jnp.where(first, 0.0, prod), axis=-1, keepdims=True)], axis=0)
                    lse_col = jnp.concatenate(
                        [jnp.max(jnp.where(first, lseb, -jnp.inf), axis=-1, keepdims=True),
                         jnp.max(jnp.where(first, -jnp.inf, lseb), axis=-1, keepdims=True)], axis=0)
                    s = _dot_nt(qs, kc) + bias_ref[jnp.minimum(n, 1)]
                    p = jnp.exp(s - lse_col)
                    ds = p * (_dot_nt(dos, vc) - delta)
                    ds16 = ds.astype(BF16)
                    dq = _dot(jnp.concatenate([ds16[:ATT_BLK], ds16[ATT_BLK:]], axis=1), _stack_heads_rows(kc, lane))
                    dq_ref[cur, :] = dq * scale
                    dk = _dot_tn(ds16, qs)
                    dv = _dot_tn(p.astype(BF16), dos)

                    def flush_before():
                        _, before, _ = _att_rows(dil, idx - 1, nb)
                        flush(before, carry[0] + dk[:ATT_BLK], carry[1] + dv[:ATT_BLK])

                    if first_of_all:
                        pl.when(idx > 0)(flush_before)
                    else:
                        flush_before()
                    return dk[ATT_BLK:], dv[ATT_BLK:]

                def step(i, carry):
                    for u in range(BWD_UNROLL):
                        carry = block(i * BWD_UNROLL + u, carry, u == 0)
                    return carry

                zero = jnp.zeros((ATT_BLK, LANES), F32)
                dk_last, dv_last = lax.fori_loop(0, n_blocks // BWD_UNROLL, step, (zero, zero))
                _, last, _ = _att_rows(dil, n_blocks - 1, nb)
                flush(last, dk_last, dv_last)

    blk = (None, S, LANES)
    shared = pl.BlockSpec(blk, lambda b, hp, g: (b, 0, hp))
    grouped = pl.BlockSpec(blk, lambda b, hp, g: (b, 0, g * HP + hp))
    prev = [] if dkv_prev is None else list(dkv_prev)
    gshape = jax.ShapeDtypeStruct((B, S, n_groups * HP * LANES), F32)
    return pl.pallas_call(
        body, name=name, grid=(B, HP, n_groups),
        in_specs=[pl.BlockSpec(memory_space=pltpu.SMEM), grouped,
                  pl.BlockSpec(blk, lambda b, hp, g: (b, 0, g * 2 * HP + hp)),
                  pl.BlockSpec(blk, lambda b, hp, g: (b, 0, g * 2 * HP + HP + hp)),
                  shared, shared, shared] + [grouped] * n_prev,
        out_specs=[grouped] * 3, out_shape=[gshape] * 3,
        scratch_shapes=[pltpu.VMEM((2, 2 * ATT_BLK, 2 * ATT_BLK), F32)],
        compiler_params=_params(3))(slopes, q, kv, kv, o, lse, do, *prev)


def _final_loss(name, h, g, target, tm):
    T, D = h.shape

    def body(h_ref, g_ref, t_ref, loss_ref, dh_ref, dh16_ref, dg_ref):
        hf = h_ref[...]
        gv = g_ref[...]
        rstd = lax.rsqrt(jnp.mean(hf * hf, axis=-1, keepdims=True) + EPS)
        xhat = hf * rstd
        err = xhat * gv - t_ref[...]
        part = 0.5 * jnp.sum(jnp.mean(err * err, axis=-1, keepdims=True), axis=0, keepdims=True)
        dy = err * (1.0 / D)
        dg = jnp.sum(dy * xhat, axis=0, keepdims=True)
        dx = dy * gv
        dh = rstd * (dx - xhat * jnp.mean(dx * xhat, axis=-1, keepdims=True))
        dh_ref[...] = dh
        dh16_ref[...] = dh.astype(BF16)

        @pl.when(pl.program_id(0) == 0)
        def _():
            loss_ref[...] = part
            dg_ref[...] = dg

        @pl.when(pl.program_id(0) > 0)
        def _():
            loss_ref[...] += part
            dg_ref[...] += dg

    return pl.pallas_call(
        body, name=name, grid=(T // tm,),
        in_specs=[pl.BlockSpec((tm, D), lambda i: (i, 0)), pl.BlockSpec((1, D), lambda i: (0, 0)),
                  pl.BlockSpec((tm, D), lambda i: (i, 0))],
        out_specs=[pl.BlockSpec((1, 1), lambda i: (0, 0)), pl.BlockSpec((tm, D), lambda i: (i, 0)),
                   pl.BlockSpec((tm, D), lambda i: (i, 0)), pl.BlockSpec((1, D), lambda i: (0, 0))],
        out_shape=[jax.ShapeDtypeStruct((1, 1), F32), jax.ShapeDtypeStruct((T, D), F32),
                   jax.ShapeDtypeStruct((T, D), BF16), jax.ShapeDtypeStruct((1, D), F32)],
        compiler_params=_params(1))(h, g, target)


def _nt_rows(name, dh, wg, layer, a_mul, out_dtype, tm, deps=()):
    T, D = dh.shape
    rk = wg.shape[2]
    N = N_CHIPS * rk
    with_a = a_mul is not None

    def body(dh_ref, w_ref, *rest):
        o_ref = rest[-1]
        d16 = dh_ref[...]
        for ch in range(N_CHIPS):
            r = _dot_nt(d16, w_ref[ch])
            if with_a:
                r = r * (2.0 * jnp.maximum(rest[0][:, ch * rk:(ch + 1) * rk].astype(F32), 0.0))
            o_ref[:, ch * rk:(ch + 1) * rk] = r.astype(out_dtype)

    in_specs = [pl.BlockSpec((tm, D), lambda i: (i, 0)),
                pl.BlockSpec((N_CHIPS, None, rk, D), lambda i: (0, layer, 0, 0))]
    args = [dh, wg]
    if with_a:
        in_specs.append(pl.BlockSpec((tm, N), lambda i: (i, 0)))
        args.append(a_mul)
    in_specs += [ANY] * len(deps)
    args += list(deps)
    return pl.pallas_call(
        body, name=name, grid=(T // tm,), in_specs=in_specs,
        out_specs=pl.BlockSpec((tm, N), lambda i: (i, 0)),
        out_shape=jax.ShapeDtypeStruct((T, N), out_dtype),
        compiler_params=_params(1))(*args)


def _nt_cols(name, ysegs, wg, layer, tm, norm, deps=()):
    Nw, cw = wg.shape[2], wg.shape[3]
    widths = [bs[-1] for _, bs, _ in ysegs]
    pieces = _pieces(widths, cw, 1024)
    ns = len(ysegs)
    T = norm[0].shape[0] if norm is not None else ysegs[0][0].shape[-2]

    def body(*refs):
        y_refs = refs[:ns]
        w_ref = refs[ns]
        acc = refs[-1]
        for n, (s, a0, ch, b0, wd) in enumerate(pieces):
            d = _dot_nt(y_refs[s][:, a0:a0 + wd].astype(BF16), w_ref[ch, :, b0:b0 + wd])
            if n == 0:
                acc[...] = d
            else:
                acc[...] += d
        if norm is None:
            refs[ns + 1 + len(deps)][...] = acc[...]
        else:
            h_ref, g_ref, dhin_ref = refs[ns + 1:ns + 4]
            out_ref, out16_ref, dg_ref = refs[ns + 4 + len(deps):ns + 7 + len(deps)]
            dh_c, dg = _rms_bwd(h_ref[...], g_ref[...], acc[...])
            dh = dhin_ref[...] + dh_c
            out_ref[...] = dh
            out16_ref[...] = dh.astype(BF16)

            @pl.when(pl.program_id(0) == 0)
            def _():
                dg_ref[...] = dg

            @pl.when(pl.program_id(0) > 0)
            def _():
                dg_ref[...] += dg

    in_specs = [pl.BlockSpec(bs, im) for _, bs, im in ysegs]
    in_specs.append(pl.BlockSpec((N_CHIPS, None, Nw, cw), lambda i: (0, layer, 0, 0)))
    args = [a for a, _, _ in ysegs] + [wg]
    row = pl.BlockSpec((tm, Nw), lambda i: (i, 0))
    vec = pl.BlockSpec((1, Nw), lambda i: (0, 0))
    if norm is None:
        out_specs = row
        out_shape = jax.ShapeDtypeStruct((T, Nw), F32)
    else:
        in_specs += [row, vec, row]
        args += list(norm)
    in_specs += [ANY] * len(deps)
    args += list(deps)
    if norm is not None:
        out_specs = [row, row, vec]
        out_shape = [jax.ShapeDtypeStruct((T, Nw), F32), jax.ShapeDtypeStruct((T, Nw), BF16),
                     jax.ShapeDtypeStruct((1, Nw), F32)]
    return pl.pallas_call(
        body, name=name, grid=(T // tm,), in_specs=in_specs, out_specs=out_specs, out_shape=out_shape,
        scratch_shapes=[pltpu.VMEM((tm, Nw), F32)], compiler_params=_params(1))(*args)


def _tn(name, x, x_act, ysegs, cw, cols_layout, tmm, tt, deps=(), out_dtype=F32):
    T, M = x.shape
    widths = [bs[-1] for _, bs, _ in ysegs]
    N = sum(widths)
    pieces = _pieces(widths, cw if cols_layout else N, 1024)
    ns = len(ysegs)
    n_t = T // tt
    block = (N_CHIPS, tmm, cw) if cols_layout else (tmm, N)
    narrow = out_dtype != F32

    def body(x_ref, *refs):
        y_refs = refs[:ns]
        o_ref = refs[ns + len(deps)]
        acc = refs[-1] if narrow else o_ref

        @pl.when(pl.program_id(1) == 0)
        def _():
            acc[...] = jnp.zeros_like(acc)

        xt = x_act(x_ref[...])
        for s, a0, ch, b0, wd in pieces:
            d = _dot_tn(xt, y_refs[s][:, a0:a0 + wd].astype(BF16))
            if cols_layout:
                acc[ch, :, b0:b0 + wd] += d
            else:
                acc[:, b0:b0 + wd] += d
        if narrow:
            @pl.when(pl.program_id(1) == n_t - 1)
            def _():
                o_ref[...] = acc[...].astype(out_dtype)

    in_specs = [pl.BlockSpec((tt, tmm), lambda m, t: (t, m))] + [pl.BlockSpec(bs, im) for _, bs, im in ysegs]
    in_specs += [ANY] * len(deps)
    if cols_layout:
        out_specs = pl.BlockSpec(block, lambda m, t: (0, m, 0))
        out_shape = jax.ShapeDtypeStruct((N_CHIPS, M, cw), out_dtype)
    else:
        out_specs = pl.BlockSpec(block, lambda m, t: (m, 0))
        out_shape = jax.ShapeDtypeStruct((M, N), out_dtype)
    return pl.pallas_call(
        body, name=name, grid=(M // tmm, n_t), in_specs=in_specs, out_specs=out_specs, out_shape=out_shape,
        scratch_shapes=[pltpu.VMEM(block, F32)] if narrow else [],
        compiler_params=_params(2))(x, *[a for a, _, _ in ysegs], *deps)


def _seg2d(a, t_rows, grid_rank):
    w = a.shape[1]
    if grid_rank == 1:
        return (a, (t_rows, w), lambda i: (i, 0))
    return (a, (t_rows, w), lambda m, t: (t, 0))


def _kv_segments(dk, dv, C, t_rows, grid_rank):
    segs = []
    for g in range(len(PATTERNS)):
        for a in (dk, dv):
            if grid_rank == 1:
                segs.append((a, (t_rows, C), lambda i, g=g: (i, g)))
            else:
                segs.append((a, (t_rows, C), lambda m, t, g=g: (t, g)))
    return segs


def _seg_plane(a, plane, t_rows, grid_rank):
    w = a.shape[2]
    if grid_rank == 1:
        return (a, (None, t_rows, w), lambda i: (plane, i, 0))
    return (a, (None, t_rows, w), lambda m, t: (plane, t, 0))


def _row_tile(rows, row_bytes, budget_bytes=2 * 1024 * 1024):
    t = rows
    while t * row_bytes > budget_bytes and t % 32 == 0:
        t //= 2
    return t


N_DEVICES = 8


def _device_add(name, own, slots, place):
    _, _, hr, c = own.shape
    tr = _row_tile(hr, c * 4, 1024 * 1024)

    def body(place_ref, own_ref, *refs):
        o_ref = refs[-1]
        acc = own_ref[...].astype(F32)
        for r in refs[:-1]:
            acc = acc + r[...].astype(F32)
        o_ref[...] = acc

    def slot(k):
        return pl.BlockSpec((None, tr, c), lambda i, pr: ((2 * pr[0] + pr[1] + k) % N_DEVICES, i, 0))

    grid_spec = pltpu.PrefetchScalarGridSpec(
        num_scalar_prefetch=1, grid=(hr // tr,),
        in_specs=[pl.BlockSpec((None, None, tr, c), lambda i, pr: (pr[0], pr[1], i, 0))]
        + [slot(k) for k in range(1, N_DEVICES)],
        out_specs=pl.BlockSpec((None, tr, c), lambda i, pr: (pr[1], i, 0)))
    return pl.pallas_call(body, name=name, grid_spec=grid_spec,
                          out_shape=jax.ShapeDtypeStruct((2, hr, c), F32),
                          compiler_params=_params(1))(place, own, *[slots] * (N_DEVICES - 1))


def _adamw(name, w, g, m, v):
    rows, cols = w.shape
    tr = _row_tile(rows, cols * 4, 1024 * 1024)

    def body(w_ref, g_ref, m_ref, v_ref, d_ref, nm_ref, nv_ref):
        d_ref[...], nm_ref[...], nv_ref[...] = _adamw_math(w_ref[...], g_ref[...], m_ref[...], v_ref[...])

    spec = pl.BlockSpec((tr, cols), lambda i: (i, 0))
    return pl.pallas_call(
        body, name=name, grid=(rows // tr,), in_specs=[spec] * 4, out_specs=[spec] * 3,
        out_shape=[jax.ShapeDtypeStruct((rows, cols), F32)] * 3, compiler_params=_params(1))(w, g, m, v)


def _adamw_math(w, g, m, v):
    nm = ADAM_B1 * m + (1.0 - ADAM_B1) * g
    nv = ADAM_B2 * v + (1.0 - ADAM_B2) * jnp.square(g)
    m_hat = nm / (1.0 - ADAM_B1 ** ADAM_STEP)
    v_hat = nv / (1.0 - ADAM_B2 ** ADAM_STEP)
    return -ADAM_LR * (m_hat / (jnp.sqrt(v_hat) + ADAM_EPS) + ADAM_WD * w), nm, nv


def _adamw_layers(name, w, grads, m, v):
    L, r, c = w.shape
    tr = _row_tile(r, L * c * 4, 1024 * 1024)

    def body(*refs):
        w_ref, m_ref, v_ref = refs[:3]
        g_refs = refs[3:3 + L]
        go_ref, d_ref, nm_ref, nv_ref = refs[3 + L:]
        for l in range(L):
            g = g_refs[l][...]
            go_ref[l] = g
            d_ref[l], nm_ref[l], nv_ref[l] = _adamw_math(w_ref[l], g, m_ref[l], v_ref[l])

    stacked = pl.BlockSpec((L, tr, c), lambda i: (0, i, 0))
    return pl.pallas_call(
        body, name=name, grid=(r // tr,),
        in_specs=[stacked] * 3 + [pl.BlockSpec((tr, c), lambda i: (i, 0))] * L, out_specs=[stacked] * 4,
        out_shape=[jax.ShapeDtypeStruct((L, r, c), F32)] * 4, compiler_params=_params(1))(w, m, v, *grads)


def _place():
    x, y, c = lax.axis_index("x"), lax.axis_index("y"), lax.axis_index("c")
    chips = [(1 - x, y), (x, 1 - y), (1 - x, 1 - y)]
    return x, y, c, chips


HBM = pl.BlockSpec(memory_space=pltpu.HBM)
SEM = pl.BlockSpec(memory_space=pltpu.SEMAPHORE)
EFFECT = pltpu.SideEffectType.DATAFLOW_SIDE_EFFECTING


class _Copy:
    def __init__(self, src, src_view, land, dst_view, recv_view, target):
        self.src, self.src_view, self.land, self.dst_view, self.recv_view, self.target = (
            src, src_view, land, dst_view, recv_view, target)


def _whole(ref, place):
    return ref


def _split_start(name, srcs, land_shapes, plans, deps=()):
    skeys, lkeys = list(srcs), list(land_shapes)
    ns, nl, ng, nd = len(skeys), len(lkeys), len(plans), len(deps)

    def body(*refs):
        src = dict(zip(skeys, refs[:ns]))
        land = dict(zip(lkeys, refs[ns:ns + nl]))
        sems = refs[ns + nl + nd:ns + nl + nd + 2 * ng]
        token = refs[-1]
        place = _place()
        for gi, plan in enumerate(plans):
            for k, cp in enumerate(plan):
                dst = land[cp.land] if cp.land in land else src[cp.land]
                pltpu.make_async_remote_copy(
                    src_ref=cp.src_view(src[cp.src], place), dst_ref=cp.dst_view(dst, place),
                    send_sem=sems[2 * gi].at[k], recv_sem=sems[2 * gi + 1].at[k],
                    device_id=cp.target(place), device_id_type=MESH).start()
        token[...] = jnp.zeros_like(token)

    sem_shapes = []
    for plan in plans:
        sem_shapes += [pltpu.SemaphoreType.DMA((len(plan),))] * 2
    buffers = [srcs[k] for k in skeys] + [lax.empty(land_shapes[k].shape, land_shapes[k].dtype) for k in lkeys]
    outs = pl.pallas_call(
        body, name=name,
        out_shape=(*sem_shapes, *[pltpu.HBM(a.shape, a.dtype) for a in buffers], jax.ShapeDtypeStruct((8, LANES), F32)),
        in_specs=[HBM] * (ns + nl) + [ANY] * nd,
        out_specs=(*[SEM] * (2 * ng), *[HBM] * (ns + nl), pl.BlockSpec(memory_space=pltpu.VMEM)),
        input_output_aliases={i: 2 * ng + i for i in range(ns + nl)},
        compiler_params=pltpu.CompilerParams(has_side_effects=EFFECT),
    )(*[pltpu.with_memory_space_constraint(a, pltpu.HBM) for a in buffers], *deps)
    sems = [(outs[2 * gi], outs[2 * gi + 1]) for gi in range(ng)]
    thru = outs[2 * ng:2 * ng + ns + nl]
    return sems, dict(zip(skeys, thru[:ns])), dict(zip(lkeys, thru[ns:])), outs[-1]


def _split_wait(name, sems, srcs, lands, plan, after):
    skeys, lkeys = list(srcs), list(lands)
    ns, nl = len(skeys), len(lkeys)

    def body(*refs):
        src = dict(zip(skeys, refs[:ns]))
        land = dict(zip(lkeys, refs[ns:ns + nl]))
        ssem, rsem = refs[ns + nl], refs[ns + nl + 1]
        place = _place()
        for k, cp in enumerate(plan):
            dst = land[cp.land] if cp.land in land else src[cp.land]
            pltpu.make_async_remote_copy(
                src_ref=cp.src_view(src[cp.src], place), dst_ref=cp.dst_view(dst, place),
                send_sem=ssem.at[k], recv_sem=rsem.at[k],
                device_id=cp.target(place), device_id_type=MESH).wait_send()
            got = cp.recv_view(dst, place)
            pltpu.make_async_remote_copy(
                src_ref=got, dst_ref=got, send_sem=ssem.at[k], recv_sem=rsem.at[k],
                device_id=cp.target(place), device_id_type=MESH).wait_recv()

    buffers = [srcs[k] for k in skeys] + [lands[k] for k in lkeys]
    outs = pl.pallas_call(
        body, name=name, out_shape=tuple(pltpu.HBM(a.shape, a.dtype) for a in buffers),
        in_specs=(*[HBM] * (ns + nl), SEM, SEM, ANY), out_specs=tuple([HBM] * (ns + nl)),
        input_output_aliases={i: i for i in range(ns + nl)},
        compiler_params=pltpu.CompilerParams(has_side_effects=EFFECT),
    )(*buffers, sems[0], sems[1], after)
    return dict(zip(skeys, outs[:ns])), dict(zip(lkeys, outs[ns:]))


def _chip_of(place):
    x, y, c, chips = place
    return 2 * x + y


GATHER_FIRST = 2


class _WeightGather:
    def __init__(self, blocks):
        self.plans, shapes = {}, {}
        for key, a in blocks.items():
            shapes[key] = jax.ShapeDtypeStruct((N_CHIPS,) + a.shape, a.dtype)
            slot = lambda ref, place: ref.at[_chip_of(place)]
            plan = [_Copy(key, _whole, key, slot,
                          lambda ref, place, k=k: ref.at[2 * place[3][k][0] + place[3][k][1]],
                          lambda place, k=k: (place[3][k][0], place[3][k][1], place[2])) for k in range(3)]
            plan.append(_Copy(key, _whole, key, slot, slot, lambda place: (place[0], place[1], 1 - place[2])))
            self.plans[key] = plan
        keys = list(blocks)
        first, a = keys[0], blocks[keys[0]]
        hr = a.shape[0] // 2
        mine = lambda ref, place, q: ref.at[q, pl.ds(pl.multiple_of(place[2] * hr, 16), hr)]
        theirs = lambda ref, place, q: ref.at[q, pl.ds(pl.multiple_of((1 - place[2]) * hr, 16), hr)]
        chip = lambda place, k: 2 * place[3][k][0] + place[3][k][1]
        sibling = lambda place: (place[0], place[1], 1 - place[2])
        self.plans[first] = [
            _Copy(first, lambda ref, place: ref.at[pl.ds(pl.multiple_of(place[2] * hr, 16), hr)], first,
                  lambda ref, place: mine(ref, place, _chip_of(place)),
                  lambda ref, place, k=k: mine(ref, place, chip(place, k)),
                  lambda place, k=k: (place[3][k][0], place[3][k][1], place[2])) for k in range(3)]
        self.plans[first].append(_Copy(first, _whole, first, lambda ref, place: ref.at[_chip_of(place)],
                                       lambda ref, place: ref.at[_chip_of(place)], sibling))
        self.forward = [_Copy(first, lambda ref, place, k=k: mine(ref, place, chip(place, k)), first,
                              lambda ref, place, k=k: mine(ref, place, chip(place, k)),
                              lambda ref, place, k=k: theirs(ref, place, chip(place, k)), sibling) for k in range(3)]
        self.blocks, self.shapes = blocks, shapes
        self.sems, self.srcs, self.lands = {}, {}, {}
        self._start("gather_start_first", keys[:GATHER_FIRST], ())
        self.rest = keys[GATHER_FIRST:]

    def _start(self, name, part, deps):
        sems, srcs, lands, self.token = _split_start(name, {k: self.blocks[k] for k in part},
                                                     {k: self.shapes[k] for k in part}, [self.plans[k] for k in part], deps)
        self.sems.update(zip(part, sems))
        self.srcs.update(srcs)
        self.lands.update(lands)

    def get(self, l, name, after):
        key = (l, name)
        _, lands = _split_wait(f"gather_wait_{name}{l}", self.sems[key], {key: self.srcs[key]},
                               {key: self.lands[key]}, self.plans[key], after)
        if self.rest:
            sems, bufs, _, _ = _split_start("gather_forward", {key: lands[key]}, {}, [self.forward])
            lands, _ = _split_wait("gather_forward_wait", sems[0], bufs, {}, self.forward, after)
            self._start("gather_start", self.rest, [lands[key]])
            self.rest = []
        return lands[key][:, None]


class _GradReduce:
    def __init__(self, place):
        self.place = place
        self.jobs = []
        self.done = {}
        self.n = 0

    def submit(self, grads):
        views = {k: a.reshape(N_CHIPS, 2, a.shape[1] // 2, a.shape[2]) for k, a in grads.items()}
        shapes = {k: jax.ShapeDtypeStruct((N_DEVICES,) + a.shape[2:], a.dtype) for k, a in views.items()}

        def peer(place, k):
            x, y, c, _ = place
            return (1 - x if k & 4 else x, 1 - y if k & 2 else y, 1 - c if k & 1 else c)

        def index(dev):
            return 4 * dev[0] + 2 * dev[1] + dev[2]

        plan = []
        for key in views:
            for k in range(1, N_DEVICES):
                plan.append(_Copy(
                    key, lambda ref, place, k=k: ref.at[2 * peer(place, k)[0] + peer(place, k)[1], peer(place, k)[2]],
                    key, lambda ref, place: ref.at[index(place[:3])],
                    lambda ref, place, k=k: ref.at[index(peer(place, k))],
                    lambda place, k=k: peer(place, k)))
        sems, srcs, lands, token = _split_start(f"grad_start{self.n}", views, shapes, [plan])
        self.jobs.append(dict(id=self.n, sems=sems[0], srcs=srcs, lands=lands, plan=plan))
        self.n += 1
        return token

    def pump(self, after):
        return []

    def finish(self, after):
        for job in self.jobs:
            srcs, lands = _split_wait(f"grad_wait{job['id']}", job["sems"], job["srcs"], job["lands"], job["plan"],
                                      after)
            for i, k in enumerate(srcs):
                self.done[k] = _device_add(f"grad_add{job['id']}_{i}", srcs[k], lands[k], self.place)
        self.jobs = []
        return self.done


class _PairShare:
    def __init__(self, halves, types):
        sibling = lambda place: (place[0], place[1], 1 - place[2])
        mine = lambda ref, place: ref.at[place[2]]
        theirs = lambda ref, place: ref.at[1 - place[2]]
        self.plans = {t: [_Copy(k, mine, k, mine, theirs, sibling) for k in halves if k[0] == t] for t in types}
        sems, self.bufs, _, self.token = _split_start("share_start", halves, {}, list(self.plans.values()))
        self.sems = dict(zip(self.plans, sems))

    def get(self, t, after):
        keys = [cp.src for cp in self.plans[t]]
        bufs, _ = _split_wait(f"share_wait_{t}", self.sems[t], {k: self.bufs[k] for k in keys}, {}, self.plans[t], after)
        return bufs


def _small_allreduce(part):
    R, C = part.shape
    N_DEV = 8

    def body(in_ref, out_ref, slots, ssem, rsem):
        x, y, c, _ = _place()
        me = 4 * x + 2 * y + c
        sends = []
        for k in range(1, N_DEV):
            kx, ky, kc = (k >> 2) & 1, (k >> 1) & 1, k & 1
            peer = (1 - x if kx else x, 1 - y if ky else y, 1 - c if kc else c)
            cp = pltpu.make_async_remote_copy(
                src_ref=in_ref, dst_ref=slots.at[me], send_sem=ssem.at[k], recv_sem=rsem.at[k],
                device_id=peer, device_id_type=MESH)
            cp.start()
            sends.append(cp)
        slots[me] = in_ref[...]
        for k in range(1, N_DEV):
            kx, ky, kc = (k >> 2) & 1, (k >> 1) & 1, k & 1
            peer = (1 - x if kx else x, 1 - y if ky else y, 1 - c if kc else c)
            slot = slots.at[4 * peer[0] + 2 * peer[1] + peer[2]]
            pltpu.make_async_remote_copy(
                src_ref=slot, dst_ref=slot, send_sem=ssem.at[k], recv_sem=rsem.at[k],
                device_id=peer, device_id_type=MESH).wait_recv()
        acc = slots[0]
        for d in range(1, N_DEV):
            acc = acc + slots[d]
        out_ref[...] = acc
        for cp in sends:
            cp.wait_send()

    vm = pl.BlockSpec(memory_space=pltpu.VMEM)
    return pl.pallas_call(
        body, name="small_allreduce", in_specs=[vm], out_specs=vm,
        out_shape=jax.ShapeDtypeStruct((R, C), F32),
        scratch_shapes=[pltpu.VMEM((N_DEV, R, C), F32), pltpu.SemaphoreType.DMA((N_DEV,)),
                        pltpu.SemaphoreType.DMA((N_DEV,))])(part)


def _local_step(x, target, norm_mix, norm_mlp, norm_kv, norm_final, weights, sink, n_a, n_heads):
    B, S, D = x.shape
    T = B * S
    C = n_heads * HEAD_DIM
    depth = norm_mix.shape[0]
    slopes = 2.0 ** (-ALIBI_MAX_BIAS * jnp.arange(1, n_heads + 1, dtype=F32) / n_heads)
    tm = min(512, T)
    row = lambda v: v.reshape(1, -1)

    h = x.reshape(T, D)
    saved, Wl = [], []
    kv = nkv = h_kv = cwg = None
    for l in range(depth):
        s = {"h_in": h}
        w = {}
        Wl.append(w)
        if l < n_a:
            n_in = _rms_only("a_in_norm0", h, row(norm_mix[l]), tm) if l == 0 else h
            w["w_a_in"] = weights.get(l, "w_a_in", n_in)
            first = [weights.token] if l == 0 else []
            s["n1"], bcu = _norm_mm(f"a_in_fwd{l}", n_in, row(norm_mix[l]), w["w_a_in"], 0, 3, BF16, tm, first, l == 0)
            s["bcu"] = bcu.reshape(3, B, S, D)
            if l == 0:
                cwg = weights.get(0, "conv", bcu)[:, 0, :n_a * 3].reshape(N_CHIPS, n_a, 3, -1)
            s["z"] = _conv_fwd(f"conv_fwd{l}", s["bcu"], cwg, l, CONV_COLS).reshape(T, D)
            w["w_a_out"] = weights.get(l, "w_a_out", s["z"])
            h = _mm_res_rows(f"a_out_fwd{l}", s["z"], w["w_a_out"], 0, h, _to_bf16, tm)
        else:
            i = l - n_a
            if i == 0:
                h_kv = h
                w["w_kv"] = weights.get(l, "w_kv", h)
                nkv, kv = _norm_mm("kv_fwd", h, row(norm_kv), w["w_kv"], 0, 1, F32, tm)
                kv = kv.reshape(B, S, 2 * 3 * C)
            w["w_q"] = weights.get(l, "w_q", h)
            s["n1"], q = _norm_mm(f"q_fwd{i}", h, row(norm_mix[l]), w["w_q"], 0, 1, F32, tm)
            s["q"] = q.reshape(B, S, 3 * C)
            o, lse = _attn_fwd(f"attn_fwd{i}", s["q"], kv, slopes, n_heads)
            s["o"], s["lse"] = o.reshape(T, C), lse.reshape(T, C)
            w["w_o"] = weights.get(l, "w_o", o)
            h = _mm_res_cols(f"o_fwd{i}", s["o"], w["w_o"], 0, h, tm)
        s["h_mid"] = h
        w["w_up"] = weights.get(l, "w_up", h)
        if l < n_a:
            s["n2"], a = _norm_mm(f"up_fwd{l}", h, row(norm_mlp[l]), w["w_up"], 0, 1, BF16, tm)
            s["a"] = a[0]
            w["w_down"] = weights.get(l, "w_down", a)
            h = _mm_res_rows(f"down_fwd{l}", s["a"], w["w_down"], 0, h, _relu2_bf16, tm)
        else:
            w["w_down"] = weights.get(l, "w_down", h)
            s["n2"], s["a"], h = _mlp_fwd(f"mlp_fwd{l}", h, row(norm_mlp[l]), w["w_up"], w["w_down"], tm)
        F = s["a"].shape[1]
        saved.append(s)

    loss, dh, dh16, dg_final = _final_loss("loss_head", h, row(norm_final), target.reshape(T, D), tm)

    g_mix, g_mlp = [None] * depth, [None] * depth
    g_conv = [None] * n_a
    dkv = None
    tt = min(512, T)
    deps = []
    for l in reversed(range(depth)):
        s, w = saved[l], Wl[l]
        g_down = _tn(f"down_wgrad{l}", s["a"], _relu2_bf16, [_seg2d(dh16, tt, 2)], None, False,
                     min(2048, F), tt, deps, BF16).reshape(N_CHIPS, F // N_CHIPS, D)
        da, dh, dh16, g_mlp[l] = _mlp_bwd(f"mlp_bwd{l}", dh, dh16, s["a"], w["w_down"], w["w_up"], s["h_mid"],
                                          row(norm_mlp[l]), tm)
        g_up = _tn(f"up_wgrad{l}", s["n2"], _to_bf16, [_seg2d(da, tt, 2)], F // N_CHIPS, True, D, tt, (), BF16)
        deps = sink.pump(dh) + [sink.submit({("w_up", l): g_up, ("w_down", l): g_down})]
        if l < n_a:
            g_out = _tn(f"a_out_wgrad{l}", s["z"], _to_bf16, [_seg2d(dh16, tt, 2)], None, False,
                        D, tt, deps, BF16).reshape(N_CHIPS, D // N_CHIPS, D)
            dz = _nt_rows(f"a_out_bwd{l}", dh16, w["w_a_out"], 0, None, F32, tm)
            deps = sink.pump(dz) + [sink.submit({("w_a_out", l): g_out})]
            dbcu, g_conv[l] = _conv_bwd(f"conv_bwd{l}", s["bcu"], dz.reshape(B, S, D), cwg, l, CONV_COLS)
            dbcu = dbcu.reshape(3, T, D)
            g_in = _tn(f"a_in_wgrad{l}", s["n1"], _to_bf16, [_seg_plane(dbcu, p, tt, 2) for p in range(3)],
                       3 * D // N_CHIPS, True, D, tt, deps, BF16)
            deps = [sink.submit({("w_a_in", l): g_in})]
            dh, dh16, g_mix[l] = _nt_cols(f"a_in_bwd{l}", [_seg_plane(dbcu, p, tm, 1) for p in range(3)],
                                          w["w_a_in"], 0, tm, (s["h_in"], row(norm_mix[l]), dh), deps)
        else:
            i = l - n_a
            g_o = _tn(f"o_wgrad{i}", s["o"], _to_bf16, [_seg2d(dh16, tt, 2)], D // N_CHIPS, True, C, tt, deps,
                      BF16)
            do = _nt_cols(f"o_bwd{i}", [_seg2d(dh16, tm, 1)], w["w_o"], 0, tm, None)
            deps = sink.pump(do) + [sink.submit({("w_o", i): g_o})]
            dq, dk, dv = _attn_bwd(f"attn_bwd{i}", s["q"], kv, slopes, s["o"].reshape(B, S, C),
                                   s["lse"].reshape(B, S, C), do.reshape(B, S, C), n_heads, dkv)
            dkv = (dk, dv)
            dq = dq.reshape(T, 3 * C)
            g_q = _tn(f"q_wgrad{i}", s["n1"], _to_bf16, [_seg2d(dq, tt, 2)], 3 * C // N_CHIPS, True, D, tt, deps,
                      BF16)
            mixer = {("w_q", i): g_q}
            if i == 0:
                dk2, dv2 = (t.reshape(T, 3 * C) for t in dkv)
                mixer[("w_kv", 0)] = _tn("kv_wgrad", nkv, _to_bf16, _kv_segments(dk2, dv2, C, tt, 2),
                                         6 * C // N_CHIPS, True, D, tt, (), BF16)
            deps = [sink.submit(mixer)]
            dh, dh16, g_mix[l] = _nt_cols(f"q_bwd{i}", [_seg2d(dq, tm, 1)], w["w_q"], 0, tm,
                                          (s["h_in"], row(norm_mix[l]), dh), deps)
            if i == 0:
                dh, dh16, g_kv = _nt_cols("kv_bwd", _kv_segments(dk2, dv2, C, tm, 1), w["w_kv"], 0, tm,
                                          (h_kv, row(norm_kv), dh))
        deps = sink.pump(dh)
    small = dict(norm_mix=jnp.concatenate(g_mix, axis=0), norm_mlp=jnp.concatenate(g_mlp, axis=0),
                 norm_kv=g_kv, norm_final=dg_final, conv_w=jnp.stack(g_conv))
    return loss, dh.reshape(B, S, D), small


BIG = ("w_a_in", "w_a_out", "w_kv", "w_q", "w_o", "w_up", "w_down")
CONV_PAD_ROWS = 16


def kernel(x, norm_mix, norm_mlp, w_a_in, conv_w, w_a_out, norm_kv, w_kv, w_q, w_o, w_up, w_down, norm_final, loss_target, m_norm_mix, m_norm_mlp, m_w_a_in, m_conv_w, m_w_a_out, m_norm_kv, m_w_kv, m_w_q, m_w_o, m_w_up, m_w_down, m_norm_final, v_norm_mix, v_norm_mlp, v_w_a_in, v_conv_w, v_w_a_out, v_norm_kv, v_w_kv, v_w_q, v_w_o, v_w_up, v_w_down, v_norm_final):
    D = x.shape[-1]
    w = dict(norm_mix=norm_mix, norm_mlp=norm_mlp, w_a_in=w_a_in, conv_w=conv_w, w_a_out=w_a_out, norm_kv=norm_kv,
             w_kv=w_kv[None], w_q=w_q, w_o=w_o, w_up=w_up, w_down=w_down, norm_final=norm_final)
    m = dict(norm_mix=m_norm_mix, norm_mlp=m_norm_mlp, w_a_in=m_w_a_in, conv_w=m_conv_w, w_a_out=m_w_a_out,
             norm_kv=m_norm_kv, w_kv=m_w_kv[None], w_q=m_w_q, w_o=m_w_o, w_up=m_w_up, w_down=m_w_down,
             norm_final=m_norm_final)
    v = dict(norm_mix=v_norm_mix, norm_mlp=v_norm_mlp, w_a_in=v_w_a_in, conv_w=v_conv_w, w_a_out=v_w_a_out,
             norm_kv=v_norm_kv, w_kv=v_w_kv[None], w_q=v_w_q, w_o=v_w_o, w_up=v_w_up, w_down=v_w_down,
             norm_final=v_norm_final)
    depth = norm_mix.shape[0]
    n_a, taps, cwc = conv_w.shape
    n_heads = w_o.shape[1] // HEAD_DIM

    conv_rows = jnp.zeros((CONV_PAD_ROWS, cwc), F32).at[:n_a * taps].set(conv_w.reshape(n_a * taps, cwc))
    blocks = {}
    for l in range(depth):
        if l < n_a:
            blocks[(l, "w_a_in")] = w_a_in[l].astype(BF16)
            if l == 0:
                blocks[(0, "conv")] = conv_rows
            blocks[(l, "w_a_out")] = w_a_out[l].astype(BF16)
        else:
            if l == n_a:
                blocks[(l, "w_kv")] = w_kv.astype(BF16)
            blocks[(l, "w_q")] = w_q[l - n_a].astype(BF16)
            blocks[(l, "w_o")] = w_o[l - n_a].astype(BF16)
        blocks[(l, "w_up")] = w_up[l].astype(BF16)
        blocks[(l, "w_down")] = w_down[l].astype(BF16)
    weights = _WeightGather(blocks)
    place = jnp.stack([2 * lax.axis_index("x") + lax.axis_index("y"), lax.axis_index("c")]).astype(jnp.int32)
    sink = _GradReduce(place)

    loss, grad_x, small = _local_step(x, loss_target, norm_mix, norm_mlp, norm_kv, norm_final, weights, sink,
                                      n_a, n_heads)
    loss = lax.psum(loss[0, 0], ("x", "y", "c"))

    share = _PairShare(sink.finish(grad_x), BIG)
    grads = {}

    packed = jnp.concatenate([small["norm_mix"], small["norm_mlp"], small["norm_kv"], small["norm_final"],
                              small["conv_w"].reshape(n_a * taps, D)], axis=0)
    pad = (-packed.shape[0]) % 8
    packed = jnp.pad(packed, ((0, pad), (0, 0)))
    total = _small_allreduce(packed)
    grads["norm_mix"] = total[:depth]
    grads["norm_mlp"] = total[depth:2 * depth]
    grads["norm_kv"] = total[2 * depth]
    grads["norm_final"] = total[2 * depth + 1]
    chip = 2 * lax.axis_index("x") + lax.axis_index("y")
    conv_full = total[2 * depth + 2:2 * depth + 2 + n_a * taps].reshape(n_a, taps, N_CHIPS, cwc)
    grads["conv_w"] = lax.dynamic_index_in_dim(conv_full, chip, axis=2, keepdims=False)

    order = ("norm_mix", "norm_mlp", "w_a_in", "conv_w", "w_a_out", "norm_kv", "w_kv", "w_q", "w_o", "w_up",
             "w_down", "norm_final")
    delta, new_m, new_v = {}, {}, {}
    vec_names = ("norm_mix", "norm_mlp", "norm_kv", "norm_final")
    rows_of = lambda a: a.reshape(-1, D)
    vw, vg, vm_, vv = (jnp.concatenate([rows_of(t[k]) for k in vec_names], axis=0) for t in (w, grads, m, v))
    vpad = (-vw.shape[0]) % 8
    padrows = lambda a: jnp.pad(a, ((0, vpad), (0, 0)))
    vd, vnm, vnv = _adamw("adamw_norms", padrows(vw), padrows(vg), padrows(vm_), padrows(vv))
    off = 0
    for k in vec_names:
        r = rows_of(w[k]).shape[0]
        delta[k] = vd[off:off + r].reshape(w[k].shape)
        new_m[k] = vnm[off:off + r].reshape(w[k].shape)
        new_v[k] = vnv[off:off + r].reshape(w[k].shape)
        off += r
    cpad = (-n_a * taps) % 8
    two_d = lambda a: jnp.pad(a.reshape(-1, cwc), ((0, cpad), (0, 0)))
    cd, cnm, cnv = _adamw("adamw_conv_w", two_d(w["conv_w"]), two_d(grads["conv_w"]), two_d(m["conv_w"]),
                          two_d(v["conv_w"]))
    delta["conv_w"], new_m["conv_w"], new_v["conv_w"] = (t[:n_a * taps].reshape(conv_w.shape) for t in (cd, cnm, cnv))
    after = cd
    for k in sorted(BIG, key=lambda k: w[k].size):
        shared = share.get(k, after)
        per_layer = [shared[(k, l)].reshape(w[k].shape[1:]) for l in range(w[k].shape[0])]
        grads[k], delta[k], new_m[k], new_v[k] = _adamw_layers(f"adamw_{k}", w[k], per_layer, m[k], v[k])
        after = delta[k]
    fix = lambda k, a: a[0] if k == "w_kv" else a
    return (loss, grad_x, *[fix(k, grads[k]) for k in order], *[fix(k, delta[k]) for k in order],
            *[fix(k, new_m[k]) for k in order], *[fix(k, new_v[k]) for k in order])
```

```python
import jax
import jax.numpy as jnp
from jax import lax
from jax.experimental import pallas as pl
from jax.experimental.pallas import tpu as pltpu

F32 = jnp.float32
BF16 = jnp.bfloat16
MESH = pl.DeviceIdType.MESH

EPS = 1e-5
PATTERNS = ((128, 1), (512, 4), (2048, 16))
HEAD_DIM = 64
ALIBI_MAX_BIAS = 8.0
NEG_INF = -1e30
ATT_BLK = 128
BWD_UNROLL = 32
N_CHIPS = 4
LANES = 128
VMEM_LIMIT = 56 * 1024 * 1024

ADAM_LR = 0.001
ADAM_B1 = 0.9
ADAM_B2 = 0.999
ADAM_EPS = 1e-08
ADAM_WD = 0.01
ADAM_STEP = 10


ANY = pl.BlockSpec(memory_space=pl.ANY)


def _params(n_grid_axes):
    return pltpu.CompilerParams(dimension_semantics=("arbitrary",) * n_grid_axes, vmem_limit_bytes=VMEM_LIMIT)


def _dot(a, b):
    return jnp.dot(a, b, preferred_element_type=F32)


def _dot_nt(a, b):
    return lax.dot_general(a, b, (((1,), (1,)), ((), ())), preferred_element_type=F32)


def _dot_tn(a, b):
    return lax.dot_general(a, b, (((0,), (0,)), ((), ())), preferred_element_type=F32)


def _relu2(a):
    return jnp.square(jnp.maximum(a, 0.0))


def _rms(hf, g):
    y = hf * lax.rsqrt(jnp.mean(hf * hf, axis=-1, keepdims=True) + EPS)
    return y * g


def _rms_bwd(hf, g, dn):
    rstd = lax.rsqrt(jnp.mean(hf * hf, axis=-1, keepdims=True) + EPS)
    xhat = hf * rstd
    dg = jnp.sum(dn * xhat, axis=0, keepdims=True)
    dx = dn * g
    dh = rstd * (dx - xhat * jnp.mean(dx * xhat, axis=-1, keepdims=True))
    return dh, dg


def _pieces(seg_widths, chunk_width, max_width):
    total = sum(seg_widths)
    cuts = {0, total}
    acc = 0
    for w in seg_widths:
        cuts.add(acc)
        acc += w
    cuts.update(range(0, total, chunk_width))
    cuts = sorted(cuts)
    fine = []
    for lo, hi in zip(cuts[:-1], cuts[1:]):
        while hi - lo > max_width:
            fine.append((lo, lo + max_width))
            lo += max_width
        fine.append((lo, hi))
    out = []
    for lo, hi in fine:
        acc = 0
        for s, w in enumerate(seg_widths):
            if lo < acc + w:
                break
            acc += w
        out.append((s, lo - acc, lo // chunk_width, lo % chunk_width, hi - lo))
    return out


def _relu2_bf16(a):
    return _relu2(a.astype(F32)).astype(BF16)


def _to_bf16(a):
    return a.astype(BF16)


def _rms_only(name, h, g, tm):
    T, D = h.shape

    def body(h_ref, g_ref, n_ref):
        n_ref[...] = _rms(h_ref[...], g_ref[...]).astype(BF16)

    row = pl.BlockSpec((tm, D), lambda i: (i, 0))
    return pl.pallas_call(
        body, name=name, grid=(T // tm,), in_specs=[row, pl.BlockSpec((1, D), lambda i: (0, 0))], out_specs=row,
        out_shape=jax.ShapeDtypeStruct((T, D), BF16), compiler_params=_params(1))(h, g)


def _norm_mm(name, h, g, wg, layer, planes, out_dtype, tm, deps=(), normed=False):
    T, D = h.shape
    cw = wg.shape[3]
    N = N_CHIPS * cw
    pw = N // planes
    pieces = _pieces([pw] * planes, cw, 512)

    def body(h_ref, g_ref, w_ref, *rest):
        n_ref, o_ref = rest[len(deps):]
        n = h_ref[...] if normed else _rms(h_ref[...], g_ref[...]).astype(BF16)
        n_ref[...] = n
        for s, a0, ch, b0, wd in pieces:
            o_ref[s, :, a0:a0 + wd] = _dot(n, w_ref[ch, :, b0:b0 + wd]).astype(out_dtype)

    return pl.pallas_call(
        body, name=name, grid=(T // tm,),
        in_specs=[pl.BlockSpec((tm, D), lambda i: (i, 0)),
                  pl.BlockSpec((1, D), lambda i: (0, 0)),
                  pl.BlockSpec((N_CHIPS, None, D, cw), lambda i: (0, layer, 0, 0))] + [ANY] * len(deps),
        out_specs=[pl.BlockSpec((tm, D), lambda i: (i, 0)),
                   pl.BlockSpec((planes, tm, pw), lambda i: (0, i, 0))],
        out_shape=[jax.ShapeDtypeStruct((T, D), BF16), jax.ShapeDtypeStruct((planes, T, pw), out_dtype)],
        compiler_params=_params(1))(h, g, wg, *deps)


def _resident(shape, index_map):
    return pl.BlockSpec(shape, index_map, pipeline_mode=pl.Buffered(1))


def _mm_res_rows(name, a, wg, layer, h, act, tm):
    T = a.shape[0]
    rk, D = wg.shape[2], wg.shape[3]

    def body(a_ref, w_ref, h_ref, o_ref):
        acc = h_ref[...]
        for k in range(N_CHIPS):
            acc = acc + _dot(act(a_ref[:, k * rk:(k + 1) * rk]), w_ref[k])
        o_ref[...] = acc

    return pl.pallas_call(
        body, name=name, grid=(T // tm,),
        in_specs=[pl.BlockSpec((tm, N_CHIPS * rk), lambda i: (i, 0)),
                  pl.BlockSpec((N_CHIPS, None, rk, D), lambda i: (0, layer, 0, 0)),
                  pl.BlockSpec((tm, D), lambda i: (i, 0))],
        out_specs=pl.BlockSpec((tm, D), lambda i: (i, 0)),
        out_shape=jax.ShapeDtypeStruct((T, D), F32),
        compiler_params=_params(1))(a, wg, h)


def _mm_res_cols(name, a, wg, layer, h, tm):
    T, K = a.shape
    cw = wg.shape[3]
    D = N_CHIPS * cw

    def body(a_ref, w_ref, h_ref, o_ref):
        a16 = a_ref[...].astype(BF16)
        for j in range(N_CHIPS):
            o_ref[:, j * cw:(j + 1) * cw] = h_ref[:, j * cw:(j + 1) * cw] + _dot(a16, w_ref[j])

    return pl.pallas_call(
        body, name=name, grid=(T // tm,),
        in_specs=[pl.BlockSpec((tm, K), lambda i: (i, 0)),
                  pl.BlockSpec((N_CHIPS, None, K, cw), lambda i: (0, layer, 0, 0)),
                  pl.BlockSpec((tm, D), lambda i: (i, 0))],
        out_specs=pl.BlockSpec((tm, D), lambda i: (i, 0)),
        out_shape=jax.ShapeDtypeStruct((T, D), F32),
        compiler_params=_params(1))(a, wg, h)


def _mlp_fwd(name, h, g, wup, wdown, tm):
    T, D = h.shape
    cw = wup.shape[3]

    def body(h_ref, g_ref, wu_ref, wd_ref, n_ref, a_ref, o_ref):
        hf = h_ref[...]
        n = _rms(hf, g_ref[...]).astype(BF16)
        n_ref[...] = n
        acc = hf
        for ch in range(N_CHIPS):
            a16 = _dot(n, wu_ref[ch]).astype(BF16)
            a_ref[:, ch * cw:(ch + 1) * cw] = a16
            acc = acc + _dot(_relu2_bf16(a16), wd_ref[ch])
        o_ref[...] = acc

    row = pl.BlockSpec((tm, D), lambda i: (i, 0))
    return pl.pallas_call(
        body, name=name, grid=(T // tm,),
        in_specs=[row, pl.BlockSpec((1, D), lambda i: (0, 0)),
                  _resident((N_CHIPS, None, D, cw), lambda i: (0, 0, 0, 0)),
                  _resident((N_CHIPS, None, cw, D), lambda i: (0, 0, 0, 0))],
        out_specs=[row, pl.BlockSpec((tm, N_CHIPS * cw), lambda i: (i, 0)), row],
        out_shape=[jax.ShapeDtypeStruct((T, D), BF16), jax.ShapeDtypeStruct((T, N_CHIPS * cw), BF16),
                   jax.ShapeDtypeStruct((T, D), F32)],
        compiler_params=_params(1))(h, g, wup, wdown)


def _mlp_bwd(name, dh, dh16, a, wdown, wup, h_mid, g, tm, deps=()):
    T, D = dh.shape
    cw = wup.shape[3]
    F = N_CHIPS * cw

    def body(dh_ref, dh16_ref, a_ref, wd_ref, wu_ref, h_ref, g_ref, *rest):
        da_ref, out_ref, out16_ref, dg_ref = rest[len(deps):]
        d16 = dh16_ref[...]
        acc = None
        for ch in range(N_CHIPS):
            cols = slice(ch * cw, (ch + 1) * cw)
            da = (_dot_nt(d16, wd_ref[ch]) * (2.0 * jnp.maximum(a_ref[:, cols].astype(F32), 0.0))).astype(BF16)
            da_ref[:, cols] = da
            d = _dot_nt(da, wu_ref[ch])
            acc = d if acc is None else acc + d
        dh_c, dg = _rms_bwd(h_ref[...], g_ref[...], acc)
        out = dh_ref[...] + dh_c
        out_ref[...] = out
        out16_ref[...] = out.astype(BF16)

        @pl.when(pl.program_id(0) == 0)
        def _():
            dg_ref[...] = dg

        @pl.when(pl.program_id(0) > 0)
        def _():
            dg_ref[...] += dg

    row = pl.BlockSpec((tm, D), lambda i: (i, 0))
    wide = pl.BlockSpec((tm, F), lambda i: (i, 0))
    vec = pl.BlockSpec((1, D), lambda i: (0, 0))
    return pl.pallas_call(
        body, name=name, grid=(T // tm,),
        in_specs=[row, row, wide, _resident((N_CHIPS, None, cw, D), lambda i: (0, 0, 0, 0)),
                  _resident((N_CHIPS, None, D, cw), lambda i: (0, 0, 0, 0)), row, vec] + [ANY] * len(deps),
        out_specs=[wide, row, row, vec],
        out_shape=[jax.ShapeDtypeStruct((T, F), BF16), jax.ShapeDtypeStruct((T, D), F32),
                   jax.ShapeDtypeStruct((T, D), BF16), jax.ShapeDtypeStruct((1, D), F32)],
        compiler_params=_params(1))(dh, dh16, a, wdown, wup, h_mid, g, *deps)


CONV_ROWS = 256
CONV_HALO = 16
CONV_COLS = 2 * LANES


def _conv_shifted(ext, k, r0, rows, at_start):
    rolled = pltpu.roll(ext, k, 0)[CONV_HALO:]
    if not at_start:
        return rolled
    t = r0 + lax.broadcasted_iota(jnp.int32, rolled.shape, 0)
    return jnp.where(t >= k, rolled, 0.0)


def _conv_ahead(ext, k, r0, rows, S, at_end):
    rolled = pltpu.roll(ext, rows + CONV_HALO - k, 0)[:rows]
    if not at_end:
        return rolled
    t = r0 + lax.broadcasted_iota(jnp.int32, rolled.shape, 0)
    return jnp.where(t + k < S, rolled, 0.0)


def _conv_chunks(step, n, carry):
    carry = step(0, carry, True, n == 1)
    if n > 2:
        carry = lax.fori_loop(1, n - 1, lambda i, c: step(i, c, False, False), carry)
    if n > 1:
        carry = step(n - 1, carry, False, True)
    return carry


def _conv_fwd(name, bcu, cwg, layer, tc):
    _, B, S, D = bcu.shape
    cwc = cwg.shape[3]
    per_chunk = cwc // tc
    R = min(CONV_ROWS, S)

    def body(x_ref, w_ref, z_ref):
        w = [w_ref[k:k + 1, :] for k in range(3)]

        def step(i, carry, at_start, at_end):
            r0 = pl.multiple_of(i * R, R)
            h0 = pl.multiple_of(jnp.maximum(r0 - CONV_HALO, 0), CONV_HALO)
            ld = lambda p, start, rows: x_ref[p, pl.ds(start, rows), :].astype(F32)
            cu = jnp.concatenate([ld(1, h0, CONV_HALO) * ld(2, h0, CONV_HALO), ld(1, r0, R) * ld(2, r0, R)], axis=0)
            conv = w[0] * cu[CONV_HALO:]
            conv = conv + w[1] * _conv_shifted(cu, 1, r0, R, at_start)
            conv = conv + w[2] * _conv_shifted(cu, 2, r0, R, at_start)
            z_ref[pl.ds(r0, R), :] = (ld(0, r0, R) * conv).astype(BF16)
            return carry

        _conv_chunks(step, S // R, 0)

    return pl.pallas_call(
        body, name=name, grid=(B, D // tc),
        in_specs=[pl.BlockSpec((3, None, S, tc), lambda b, j: (0, b, 0, j)),
                  pl.BlockSpec((None, None, 3, tc), lambda b, j: (j // per_chunk, layer, 0, j % per_chunk))],
        out_specs=pl.BlockSpec((None, S, tc), lambda b, j: (b, 0, j)),
        out_shape=jax.ShapeDtypeStruct((B, S, D), BF16),
        compiler_params=_params(2))(bcu, cwg)


def _conv_bwd(name, bcu, dz, cwg, layer, tc):
    _, B, S, D = bcu.shape
    cwc = cwg.shape[3]
    per_chunk = cwc // tc
    R = min(CONV_ROWS, S)

    def body(x_ref, dz_ref, w_ref, d_ref, dw_ref):
        w = [w_ref[k:k + 1, :] for k in range(3)]

        @pl.when(pl.program_id(1) == 0)
        def _():
            dw_ref[...] = jnp.zeros_like(dw_ref)

        def step(i, carry, at_start, at_end):
            r0 = pl.multiple_of(i * R, R)
            h0 = pl.multiple_of(jnp.maximum(r0 - CONV_HALO, 0), CONV_HALO)
            a0 = pl.multiple_of(jnp.minimum(r0 + R, S - CONV_HALO), CONV_HALO)
            ld = lambda p, start, rows: x_ref[p, pl.ds(start, rows), :].astype(F32)
            b, c, u = ld(0, r0, R), ld(1, r0, R), ld(2, r0, R)
            dz = dz_ref[pl.ds(r0, R), :]
            cu = jnp.concatenate([ld(1, h0, CONV_HALO) * ld(2, h0, CONV_HALO), c * u], axis=0)
            cu1 = _conv_shifted(cu, 1, r0, R, at_start)
            cu2 = _conv_shifted(cu, 2, r0, R, at_start)
            conv = w[0] * (c * u) + w[1] * cu1 + w[2] * cu2
            dconv = dz * b
            dca = jnp.concatenate([dconv, dz_ref[pl.ds(a0, CONV_HALO), :] * ld(0, a0, CONV_HALO)], axis=0)
            dcu = (w[0] * dconv + w[1] * _conv_ahead(dca, 1, r0, R, S, at_end)
                   + w[2] * _conv_ahead(dca, 2, r0, R, S, at_end))
            d_ref[0, pl.ds(r0, R), :] = (dz * conv).astype(BF16)
            d_ref[1, pl.ds(r0, R), :] = (dcu * u).astype(BF16)
            d_ref[2, pl.ds(r0, R), :] = (dcu * c).astype(BF16)
            return (carry[0] + jnp.sum(dconv * (c * u), axis=0, keepdims=True),
                    carry[1] + jnp.sum(dconv * cu1, axis=0, keepdims=True),
                    carry[2] + jnp.sum(dconv * cu2, axis=0, keepdims=True))

        zero = jnp.zeros((1, tc), F32)
        s0, s1, s2 = _conv_chunks(step, S // R, (zero, zero, zero))
        for k, sk in enumerate((s0, s1, s2)):
            dw_ref[k:k + 1, :] += sk

    return pl.pallas_call(
        body, name=name, grid=(D // tc, B),
        in_specs=[pl.BlockSpec((3, None, S, tc), lambda j, b: (0, b, 0, j)),
                  pl.BlockSpec((None, S, tc), lambda j, b: (b, 0, j)),
                  pl.BlockSpec((None, None, 3, tc), lambda j, b: (j // per_chunk, layer, 0, j % per_chunk))],
        out_specs=[pl.BlockSpec((3, None, S, tc), lambda j, b: (0, b, 0, j)),
                   pl.BlockSpec((3, tc), lambda j, b: (0, j))],
        out_shape=[jax.ShapeDtypeStruct((3, B, S, D), BF16), jax.ShapeDtypeStruct((3, D), F32)],
        compiler_params=_params(2))(bcu, dz, cwg)


def _att_rows(dil, idx, nb):
    r, n = idx // nb, idx % nb
    if dil == 1:
        cur = pl.ds(pl.multiple_of(n * ATT_BLK, ATT_BLK), ATT_BLK)
        prev = pl.ds(pl.multiple_of(jnp.maximum(n - 1, 0) * ATT_BLK, ATT_BLK), ATT_BLK)
    else:
        cur = pl.ds(n * (ATT_BLK * dil) + r, ATT_BLK, stride=dil)
        prev = pl.ds(jnp.maximum(n - 1, 0) * (ATT_BLK * dil) + r, ATT_BLK, stride=dil)
    return n, cur, prev


def _att_bias(bias_ref, dil, sl_ref, hp):
    row = lax.broadcasted_iota(jnp.int32, (2 * ATT_BLK, 2 * ATT_BLK), 0)
    ci = lax.broadcasted_iota(jnp.int32, (2 * ATT_BLK, 2 * ATT_BLK), 1)
    j = ATT_BLK + (row & (ATT_BLK - 1)) - ci
    slope = jnp.where(row < ATT_BLK, sl_ref[2 * hp], sl_ref[2 * hp + 1])
    rest = jnp.where((j >= 0) & (j <= ATT_BLK), -slope * (dil * j).astype(F32), NEG_INF)
    bias_ref[1] = rest
    bias_ref[0] = jnp.where(ci >= ATT_BLK, rest, NEG_INF)


def _stack_heads(x16, lane):
    first = lane < HEAD_DIM
    return jnp.concatenate([jnp.where(first, x16, jnp.zeros_like(x16)),
                            jnp.where(first, jnp.zeros_like(x16), x16)], axis=0)


def _per_head(col, lane):
    return jnp.where(lane < HEAD_DIM, col[:ATT_BLK], col[ATT_BLK:])


def _attn_fwd(name, q, kv, slopes, n_heads):
    B, S, CQ = q.shape
    HP = n_heads * HEAD_DIM // LANES
    scale = HEAD_DIM ** -0.5
    n_groups = len(PATTERNS)
    CH = 256

    def body(sl_ref, q_ref, k_ref, v_ref, o_ref, lse_ref, bias_ref, *parts):
        og, lg = parts[:n_groups], parts[n_groups:]
        hp, g = pl.program_id(1), pl.program_id(2)
        lane = lax.broadcasted_iota(jnp.int32, (1, LANES), 1)

        for gi, (window, dil) in enumerate(PATTERNS):
            nb = S // dil // ATT_BLK

            @pl.when(g == gi)
            def _(gi=gi, dil=dil, nb=nb):
                _att_bias(bias_ref, dil, sl_ref, hp)

                def step(idx, carry):
                    n, cur, prev = _att_rows(dil, idx, nb)
                    qs = _stack_heads((q_ref[cur, :] * scale).astype(BF16), lane)
                    kc = jnp.concatenate([k_ref[prev, :], k_ref[cur, :]], axis=0).astype(BF16)
                    vc = jnp.concatenate([v_ref[prev, :], v_ref[cur, :]], axis=0).astype(BF16)
                    s = _dot_nt(qs, kc) + bias_ref[jnp.minimum(n, 1)]
                    m = jnp.max(s, axis=-1, keepdims=True)
                    p = jnp.exp(s - m)
                    l = jnp.sum(p, axis=-1, keepdims=True)
                    p16 = p.astype(BF16)
                    o_un = _dot(jnp.concatenate([p16[:ATT_BLK], p16[ATT_BLK:]], axis=1), _stack_heads_rows(vc, lane))
                    og[gi][cur, :] = o_un / _per_head(l, lane)
                    lg[gi][cur, :] = _per_head(m + jnp.log(l), lane)
                    return carry

                lax.fori_loop(0, S // ATT_BLK, step, 0, unroll=S // ATT_BLK)

        @pl.when(g == n_groups - 1)
        def _():
            def comb(i, carry):
                rows = pl.ds(pl.multiple_of(i * CH, CH), CH)
                a, b, c = lg[0][rows, :], lg[1][rows, :], lg[2][rows, :]
                m = jnp.maximum(jnp.maximum(a, b), c)
                ea, eb, ec = jnp.exp(a - m), jnp.exp(b - m), jnp.exp(c - m)
                z = ea + eb + ec
                o_ref[rows, :] = (ea / z) * og[0][rows, :] + (eb / z) * og[1][rows, :] + (ec / z) * og[2][rows, :]
                lse_ref[rows, :] = m + jnp.log(z)
                return carry

            lax.fori_loop(0, S // CH, comb, 0)

    blk = (None, S, LANES)
    out = pl.BlockSpec(blk, lambda b, hp, g: (b, 0, hp))
    return pl.pallas_call(
        body, name=name, grid=(B, HP, n_groups),
        in_specs=[pl.BlockSpec(memory_space=pltpu.SMEM),
                  pl.BlockSpec(blk, lambda b, hp, g: (b, 0, g * HP + hp)),
                  pl.BlockSpec(blk, lambda b, hp, g: (b, 0, g * 2 * HP + hp)),
                  pl.BlockSpec(blk, lambda b, hp, g: (b, 0, g * 2 * HP + HP + hp))],
        out_specs=[out, out],
        out_shape=[jax.ShapeDtypeStruct((B, S, HP * LANES), F32)] * 2,
        scratch_shapes=[pltpu.VMEM((2, 2 * ATT_BLK, 2 * ATT_BLK), F32)] + [pltpu.VMEM((S, LANES), F32)] * (2 * n_groups),
        compiler_params=_params(3))(slopes, q, kv, kv)


def _stack_heads_rows(x16, lane):
    first = lane < HEAD_DIM
    return jnp.concatenate([jnp.where(first, x16, jnp.zeros_like(x16)),
                            jnp.where(first, jnp.zeros_like(x16), x16)], axis=0)


def _attn_bwd(name, q, kv, slopes, o, lse, do, n_heads, dkv_prev):
    B, S, CQ = q.shape
    HP = n_heads * HEAD_DIM // LANES
    scale = HEAD_DIM ** -0.5
    n_groups = len(PATTERNS)
    n_prev = 0 if dkv_prev is None else 2

    def body(sl_ref, q_ref, k_ref, v_ref, o_ref, lse_ref, do_ref, *rest):
        dq_ref, dk_ref, dv_ref, bias_ref = rest[n_prev:]
        hp, g = pl.program_id(1), pl.program_id(2)
        lane = lax.broadcasted_iota(jnp.int32, (1, LANES), 1)
        first = lane < HEAD_DIM

        def flush(rows, dk, dv):
            if n_prev:
                dk = dk + rest[0][rows, :]
                dv = dv + rest[1][rows, :]
            dk_ref[rows, :] = dk
            dv_ref[rows, :] = dv

        for gi, (window, dil) in enumerate(PATTERNS):
            nb = S // dil // ATT_BLK
            n_blocks = S // ATT_BLK

            @pl.when(g == gi)
            def _(dil=dil, nb=nb, n_blocks=n_blocks):
                _att_bias(bias_ref, dil, sl_ref, hp)

                def block(idx, carry, first_of_all):
                    n, cur, prev = _att_rows(dil, idx, nb)
                    qs = _stack_heads((q_ref[cur, :] * scale).astype(BF16), lane)
                    kc = jnp.concatenate([k_ref[prev, :], k_ref[cur, :]], axis=0).astype(BF16)
                    vc = jnp.concatenate([v_ref[prev, :], v_ref[cur, :]], axis=0).astype(BF16)
                    dob = do_ref[cur, :]
                    prod = dob * o_ref[cur, :]
                    lseb = lse_ref[cur, :]
                    dos = _stack_heads(dob.astype(BF16), lane)
                    delta = jnp.concatenate(
                        [jnp.sum(jnp.where(first, prod, 0.0), axis=-1, keepdims=True),
                         jnp.sum(jnp.where(first, 0.0, prod), axis=-1, keepdims=True)], axis=0)
                    lse_col = jnp.concatenate(
                        [jnp.max(jnp.where(first, lseb, -jnp.inf), axis=-1, keepdims=True),
                         jnp.max(jnp.where(first, -jnp.inf, lseb), axis=-1, keepdims=True)], axis=0)
                    s = _dot_nt(qs, kc) + bias_ref[jnp.minimum(n, 1)]
                    p = jnp.exp(s - lse_col)
                    ds = p * (_dot_nt(dos, vc) - delta)
                    ds16 = ds.astype(BF16)
                    dq = _dot(jnp.concatenate([ds16[:ATT_BLK], ds16[ATT_BLK:]], axis=1), _stack_heads_rows(kc, lane))
                    dq_ref[cur, :] = dq * scale
                    dk = _dot_tn(ds16, qs)
                    dv = _dot_tn(p.astype(BF16), dos)

                    def flush_before():
                        _, before, _ = _att_rows(dil, idx - 1, nb)
                        flush(before, carry[0] + dk[:ATT_BLK], carry[1] + dv[:ATT_BLK])

                    if first_of_all:
                        pl.when(idx > 0)(flush_before)
                    else:
                        flush_before()
                    return dk[ATT_BLK:], dv[ATT_BLK:]

                def step(i, carry):
                    for u in range(BWD_UNROLL):
                        carry = block(i * BWD_UNROLL + u, carry, u == 0)
                    return carry

                zero = jnp.zeros((ATT_BLK, LANES), F32)
                dk_last, dv_last = lax.fori_loop(0, n_blocks // BWD_UNROLL, step, (zero, zero))
                _, last, _ = _att_rows(dil, n_blocks - 1, nb)
                flush(last, dk_last, dv_last)

    blk = (None, S, LANES)
    shared = pl.BlockSpec(blk, lambda b, hp, g: (b, 0, hp))
    grouped = pl.BlockSpec(blk, lambda b, hp, g: (b, 0, g * HP + hp))
    prev = [] if dkv_prev is None else list(dkv_prev)
    gshape = jax.ShapeDtypeStruct((B, S, n_groups * HP * LANES), F32)
    return pl.pallas_call(
        body, name=name, grid=(B, HP, n_groups),
        in_specs=[pl.BlockSpec(memory_space=pltpu.SMEM), grouped,
                  pl.BlockSpec(blk, lambda b, hp, g: (b, 0, g * 2 * HP + hp)),
                  pl.BlockSpec(blk, lambda b, hp, g: (b, 0, g * 2 * HP + HP + hp)),
                  shared, shared, shared] + [grouped] * n_prev,
        out_specs=[grouped] * 3, out_shape=[gshape] * 3,
        scratch_shapes=[pltpu.VMEM((2, 2 * ATT_BLK, 2 * ATT_BLK), F32)],
        compiler_params=_params(3))(slopes, q, kv, kv, o, lse, do, *prev)


def _final_loss(name, h, g, target, tm):
    T, D = h.shape

    def body(h_ref, g_ref, t_ref, loss_ref, dh_ref, dh16_ref, dg_ref):
        hf = h_ref[...]
        gv = g_ref[...]
        rstd = lax.rsqrt(jnp.mean(hf * hf, axis=-1, keepdims=True) + EPS)
        xhat = hf * rstd
        err = xhat * gv - t_ref[...]
        part = 0.5 * jnp.sum(jnp.mean(err * err, axis=-1, keepdims=True), axis=0, keepdims=True)
        dy = err * (1.0 / D)
        dg = jnp.sum(dy * xhat, axis=0, keepdims=True)
        dx = dy * gv
        dh = rstd * (dx - xhat * jnp.mean(dx * xhat, axis=-1, keepdims=True))
        dh_ref[...] = dh
        dh16_ref[...] = dh.astype(BF16)

        @pl.when(pl.program_id(0) == 0)
        def _():
            loss_ref[...] = part
            dg_ref[...] = dg

        @pl.when(pl.program_id(0) > 0)
        def _():
            loss_ref[...] += part
            dg_ref[...] += dg

    return pl.pallas_call(
        body, name=name, grid=(T // tm,),
        in_specs=[pl.BlockSpec((tm, D), lambda i: (i, 0)), pl.BlockSpec((1, D), lambda i: (0, 0)),
                  pl.BlockSpec((tm, D), lambda i: (i, 0))],
        out_specs=[pl.BlockSpec((1, 1), lambda i: (0, 0)), pl.BlockSpec((tm, D), lambda i: (i, 0)),
                   pl.BlockSpec((tm, D), lambda i: (i, 0)), pl.BlockSpec((1, D), lambda i: (0, 0))],
        out_shape=[jax.ShapeDtypeStruct((1, 1), F32), jax.ShapeDtypeStruct((T, D), F32),
                   jax.ShapeDtypeStruct((T, D), BF16), jax.ShapeDtypeStruct((1, D), F32)],
        compiler_params=_params(1))(h, g, target)


def _nt_rows(name, dh, wg, layer, a_mul, out_dtype, tm, deps=()):
    T, D = dh.shape
    rk = wg.shape[2]
    N = N_CHIPS * rk
    with_a = a_mul is not None

    def body(dh_ref, w_ref, *rest):
        o_ref = rest[-1]
        d16 = dh_ref[...]
        for ch in range(N_CHIPS):
            r = _dot_nt(d16, w_ref[ch])
            if with_a:
                r = r * (2.0 * jnp.maximum(rest[0][:, ch * rk:(ch + 1) * rk].astype(F32), 0.0))
            o_ref[:, ch * rk:(ch + 1) * rk] = r.astype(out_dtype)

    in_specs = [pl.BlockSpec((tm, D), lambda i: (i, 0)),
                pl.BlockSpec((N_CHIPS, None, rk, D), lambda i: (0, layer, 0, 0))]
    args = [dh, wg]
    if with_a:
        in_specs.append(pl.BlockSpec((tm, N), lambda i: (i, 0)))
        args.append(a_mul)
    in_specs += [ANY] * len(deps)
    args += list(deps)
    return pl.pallas_call(
        body, name=name, grid=(T // tm,), in_specs=in_specs,
        out_specs=pl.BlockSpec((tm, N), lambda i: (i, 0)),
        out_shape=jax.ShapeDtypeStruct((T, N), out_dtype),
        compiler_params=_params(1))(*args)


def _nt_cols(name, ysegs, wg, layer, tm, norm, deps=()):
    Nw, cw = wg.shape[2], wg.shape[3]
    widths = [bs[-1] for _, bs, _ in ysegs]
    pieces = _pieces(widths, cw, 1024)
    ns = len(ysegs)
    T = norm[0].shape[0] if norm is not None else ysegs[0][0].shape[-2]

    def body(*refs):
        y_refs = refs[:ns]
        w_ref = refs[ns]
        acc = refs[-1]
        for n, (s, a0, ch, b0, wd) in enumerate(pieces):
            d = _dot_nt(y_refs[s][:, a0:a0 + wd].astype(BF16), w_ref[ch, :, b0:b0 + wd])
            if n == 0:
                acc[...] = d
            else:
                acc[...] += d
        if norm is None:
            refs[ns + 1 + len(deps)][...] = acc[...]
        else:
            h_ref, g_ref, dhin_ref = refs[ns + 1:ns + 4]
            out_ref, out16_ref, dg_ref = refs[ns + 4 + len(deps):ns + 7 + len(deps)]
            dh_c, dg = _rms_bwd(h_ref[...], g_ref[...], acc[...])
            dh = dhin_ref[...] + dh_c
            out_ref[...] = dh
            out16_ref[...] = dh.astype(BF16)

            @pl.when(pl.program_id(0) == 0)
            def _():
                dg_ref[...] = dg

            @pl.when(pl.program_id(0) > 0)
            def _():
                dg_ref[...] += dg

    in_specs = [pl.BlockSpec(bs, im) for _, bs, im in ysegs]
    in_specs.append(pl.BlockSpec((N_CHIPS, None, Nw, cw), lambda i: (0, layer, 0, 0)))
    args = [a for a, _, _ in ysegs] + [wg]
    row = pl.BlockSpec((tm, Nw), lambda i: (i, 0))
    vec = pl.BlockSpec((1, Nw), lambda i: (0, 0))
    if norm is None:
        out_specs = row
        out_shape = jax.ShapeDtypeStruct((T, Nw), F32)
    else:
        in_specs += [row, vec, row]
        args += list(norm)
    in_specs += [ANY] * len(deps)
    args += list(deps)
    if norm is not None:
        out_specs = [row, row, vec]
        out_shape = [jax.ShapeDtypeStruct((T, Nw), F32), jax.ShapeDtypeStruct((T, Nw), BF16),
                     jax.ShapeDtypeStruct((1, Nw), F32)]
    return pl.pallas_call(
        body, name=name, grid=(T // tm,), in_specs=in_specs, out_specs=out_specs, out_shape=out_shape,
        scratch_shapes=[pltpu.VMEM((tm, Nw), F32)], compiler_params=_params(1))(*args)


def _tn(name, x, x_act, ysegs, cw, cols_layout, tmm, tt, deps=(), out_dtype=F32):
    T, M = x.shape
    widths = [bs[-1] for _, bs, _ in ysegs]
    N = sum(widths)
    pieces = _pieces(widths, cw if cols_layout else N, 1024)
    ns = len(ysegs)
    n_t = T // tt
    block = (N_CHIPS, tmm, cw) if cols_layout else (tmm, N)
    narrow = out_dtype != F32

    def body(x_ref, *refs):
        y_refs = refs[:ns]
        o_ref = refs[ns + len(deps)]
        acc = refs[-1] if narrow else o_ref

        @pl.when(pl.program_id(1) == 0)
        def _():
            acc[...] = jnp.zeros_like(acc)

        xt = x_act(x_ref[...])
        for s, a0, ch, b0, wd in pieces:
            d = _dot_tn(xt, y_refs[s][:, a0:a0 + wd].astype(BF16))
            if cols_layout:
                acc[ch, :, b0:b0 + wd] += d
            else:
                acc[:, b0:b0 + wd] += d
        if narrow:
            @pl.when(pl.program_id(1) == n_t - 1)
            def _():
                o_ref[...] = acc[...].astype(out_dtype)

    in_specs = [pl.BlockSpec((tt, tmm), lambda m, t: (t, m))] + [pl.BlockSpec(bs, im) for _, bs, im in ysegs]
    in_specs += [ANY] * len(deps)
    if cols_layout:
        out_specs = pl.BlockSpec(block, lambda m, t: (0, m, 0))
        out_shape = jax.ShapeDtypeStruct((N_CHIPS, M, cw), out_dtype)
    else:
        out_specs = pl.BlockSpec(block, lambda m, t: (m, 0))
        out_shape = jax.ShapeDtypeStruct((M, N), out_dtype)
    return pl.pallas_call(
        body, name=name, grid=(M // tmm, n_t), in_specs=in_specs, out_specs=out_specs, out_shape=out_shape,
        scratch_shapes=[pltpu.VMEM(block, F32)] if narrow else [],
        compiler_params=_params(2))(x, *[a for a, _, _ in ysegs], *deps)


def _seg2d(a, t_rows, grid_rank):
    w = a.shape[1]
    if grid_rank == 1:
        return (a, (t_rows, w), lambda i: (i, 0))
    return (a, (t_rows, w), lambda m, t: (t, 0))


def _kv_segments(dk, dv, C, t_rows, grid_rank):
    segs = []
    for g in range(len(PATTERNS)):
        for a in (dk, dv):
            if grid_rank == 1:
                segs.append((a, (t_rows, C), lambda i, g=g: (i, g)))
            else:
                segs.append((a, (t_rows, C), lambda m, t, g=g: (t, g)))
    return segs


def _seg_plane(a, plane, t_rows, grid_rank):
    w = a.shape[2]
    if grid_rank == 1:
        return (a, (None, t_rows, w), lambda i: (plane, i, 0))
    return (a, (None, t_rows, w), lambda m, t: (plane, t, 0))


def _row_tile(rows, row_bytes, budget_bytes=2 * 1024 * 1024):
    t = rows
    while t * row_bytes > budget_bytes and t % 32 == 0:
        t //= 2
    return t


N_DEVICES = 8


ADD_STEPS = 2


def _device_add(name, owns, slots, place):
    keys = list(owns)

    def body(place_ref, *refs):
        ins, outs = refs[:len(keys) * N_DEVICES], refs[len(keys) * N_DEVICES:]
        for t in range(len(keys)):
            group = ins[t * N_DEVICES:(t + 1) * N_DEVICES]
            acc = group[0][...].astype(F32)
            for r in group[1:]:
                acc = acc + r[...].astype(F32)
            outs[t][...] = acc

    in_specs, args, out_specs, out_shape = [], [], [], []
    for k in keys:
        _, _, hr, c = owns[k].shape
        tr = hr // ADD_STEPS
        in_specs.append(pl.BlockSpec((None, None, tr, c), lambda i, pr: (pr[0], pr[1], i, 0)))
        in_specs += [pl.BlockSpec((None, tr, c), lambda i, pr, j=j: ((2 * pr[0] + pr[1] + j) % N_DEVICES, i, 0))
                     for j in range(1, N_DEVICES)]
        args += [owns[k]] + [slots[k]] * (N_DEVICES - 1)
        out_specs.append(pl.BlockSpec((None, tr, c), lambda i, pr: (pr[1], i, 0)))
        out_shape.append(jax.ShapeDtypeStruct((2, hr, c), F32))
    grid_spec = pltpu.PrefetchScalarGridSpec(num_scalar_prefetch=1, grid=(ADD_STEPS,), in_specs=in_specs,
                                             out_specs=out_specs)
    outs = pl.pallas_call(body, name=name, grid_spec=grid_spec, out_shape=out_shape,
                          compiler_params=_params(1))(place, *args)
    return dict(zip(keys, outs))


def _adamw(name, w, g, m, v):
    rows, cols = w.shape
    tr = _row_tile(rows, cols * 4, 1024 * 1024)

    def body(w_ref, g_ref, m_ref, v_ref, d_ref, nm_ref, nv_ref):
        d_ref[...], nm_ref[...], nv_ref[...] = _adamw_math(w_ref[...], g_ref[...], m_ref[...], v_ref[...])

    spec = pl.BlockSpec((tr, cols), lambda i: (i, 0))
    return pl.pallas_call(
        body, name=name, grid=(rows // tr,), in_specs=[spec] * 4, out_specs=[spec] * 3,
        out_shape=[jax.ShapeDtypeStruct((rows, cols), F32)] * 3, compiler_params=_params(1))(w, g, m, v)


def _adamw_math(w, g, m, v):
    nm = ADAM_B1 * m + (1.0 - ADAM_B1) * g
    nv = ADAM_B2 * v + (1.0 - ADAM_B2) * jnp.square(g)
    m_hat = nm / (1.0 - ADAM_B1 ** ADAM_STEP)
    v_hat = nv / (1.0 - ADAM_B2 ** ADAM_STEP)
    return -ADAM_LR * (m_hat / (jnp.sqrt(v_hat) + ADAM_EPS) + ADAM_WD * w), nm, nv


def _adamw_layers(name, w, grads, m, v):
    L, r, c = w.shape
    tr = _row_tile(r, L * c * 4, 1024 * 1024)

    def body(*refs):
        w_ref, m_ref, v_ref = refs[:3]
        g_refs = refs[3:3 + L]
        go_ref, d_ref, nm_ref, nv_ref = refs[3 + L:]
        for l in range(L):
            g = g_refs[l][...]
            go_ref[l] = g
            d_ref[l], nm_ref[l], nv_ref[l] = _adamw_math(w_ref[l], g, m_ref[l], v_ref[l])

    stacked = pl.BlockSpec((L, tr, c), lambda i: (0, i, 0))
    return pl.pallas_call(
        body, name=name, grid=(r // tr,),
        in_specs=[stacked] * 3 + [pl.BlockSpec((tr, c), lambda i: (i, 0))] * L, out_specs=[stacked] * 4,
        out_shape=[jax.ShapeDtypeStruct((L, r, c), F32)] * 4, compiler_params=_params(1))(w, m, v, *grads)


def _place():
    x, y, c = lax.axis_index("x"), lax.axis_index("y"), lax.axis_index("c")
    chips = [(1 - x, y), (x, 1 - y), (1 - x, 1 - y)]
    return x, y, c, chips


HBM = pl.BlockSpec(memory_space=pltpu.HBM)
SEM = pl.BlockSpec(memory_space=pltpu.SEMAPHORE)
EFFECT = pltpu.SideEffectType.DATAFLOW_SIDE_EFFECTING


class _Copy:
    def __init__(self, src, src_view, land, dst_view, recv_view, target):
        self.src, self.src_view, self.land, self.dst_view, self.recv_view, self.target = (
            src, src_view, land, dst_view, recv_view, target)


def _whole(ref, place):
    return ref


def _split_start(name, srcs, land_shapes, plans, deps=()):
    skeys, lkeys = list(srcs), list(land_shapes)
    ns, nl, ng, nd = len(skeys), len(lkeys), len(plans), len(deps)

    def body(*refs):
        src = dict(zip(skeys, refs[:ns]))
        land = dict(zip(lkeys, refs[ns:ns + nl]))
        sems = refs[ns + nl + nd:ns + nl + nd + 2 * ng]
        token = refs[-1]
        place = _place()
        for gi, plan in enumerate(plans):
            for k, cp in enumerate(plan):
                dst = land[cp.land] if cp.land in land else src[cp.land]
                pltpu.make_async_remote_copy(
                    src_ref=cp.src_view(src[cp.src], place), dst_ref=cp.dst_view(dst, place),
                    send_sem=sems[2 * gi].at[k], recv_sem=sems[2 * gi + 1].at[k],
                    device_id=cp.target(place), device_id_type=MESH).start()
        token[...] = jnp.zeros_like(token)

    sem_shapes = []
    for plan in plans:
        sem_shapes += [pltpu.SemaphoreType.DMA((len(plan),))] * 2
    buffers = [srcs[k] for k in skeys] + [lax.empty(land_shapes[k].shape, land_shapes[k].dtype) for k in lkeys]
    outs = pl.pallas_call(
        body, name=name,
        out_shape=(*sem_shapes, *[pltpu.HBM(a.shape, a.dtype) for a in buffers], jax.ShapeDtypeStruct((8, LANES), F32)),
        in_specs=[HBM] * (ns + nl) + [ANY] * nd,
        out_specs=(*[SEM] * (2 * ng), *[HBM] * (ns + nl), pl.BlockSpec(memory_space=pltpu.VMEM)),
        input_output_aliases={i: 2 * ng + i for i in range(ns + nl)},
        compiler_params=pltpu.CompilerParams(has_side_effects=EFFECT),
    )(*[pltpu.with_memory_space_constraint(a, pltpu.HBM) for a in buffers], *deps)
    sems = [(outs[2 * gi], outs[2 * gi + 1]) for gi in range(ng)]
    thru = outs[2 * ng:2 * ng + ns + nl]
    return sems, dict(zip(skeys, thru[:ns])), dict(zip(lkeys, thru[ns:])), outs[-1]


def _split_wait(name, sems, srcs, lands, plan, after):
    skeys, lkeys = list(srcs), list(lands)
    ns, nl = len(skeys), len(lkeys)

    def body(*refs):
        src = dict(zip(skeys, refs[:ns]))
        land = dict(zip(lkeys, refs[ns:ns + nl]))
        ssem, rsem = refs[ns + nl], refs[ns + nl + 1]
        place = _place()
        for k, cp in enumerate(plan):
            dst = land[cp.land] if cp.land in land else src[cp.land]
            pltpu.make_async_remote_copy(
                src_ref=cp.src_view(src[cp.src], place), dst_ref=cp.dst_view(dst, place),
                send_sem=ssem.at[k], recv_sem=rsem.at[k],
                device_id=cp.target(place), device_id_type=MESH).wait_send()
            got = cp.recv_view(dst, place)
            pltpu.make_async_remote_copy(
                src_ref=got, dst_ref=got, send_sem=ssem.at[k], recv_sem=rsem.at[k],
                device_id=cp.target(place), device_id_type=MESH).wait_recv()

    buffers = [srcs[k] for k in skeys] + [lands[k] for k in lkeys]
    outs = pl.pallas_call(
        body, name=name, out_shape=tuple(pltpu.HBM(a.shape, a.dtype) for a in buffers),
        in_specs=(*[HBM] * (ns + nl), SEM, SEM, ANY), out_specs=tuple([HBM] * (ns + nl)),
        input_output_aliases={i: i for i in range(ns + nl)},
        compiler_params=pltpu.CompilerParams(has_side_effects=EFFECT),
    )(*buffers, sems[0], sems[1], after)
    return dict(zip(skeys, outs[:ns])), dict(zip(lkeys, outs[ns:]))


def _chip_of(place):
    x, y, c, chips = place
    return 2 * x + y


GATHER_FIRST = 2


class _WeightGather:
    def __init__(self, blocks):
        self.plans, shapes = {}, {}
        for key, a in blocks.items():
            shapes[key] = jax.ShapeDtypeStruct((N_CHIPS,) + a.shape, a.dtype)
            slot = lambda ref, place: ref.at[_chip_of(place)]
            plan = [_Copy(key, _whole, key, slot,
                          lambda ref, place, k=k: ref.at[2 * place[3][k][0] + place[3][k][1]],
                          lambda place, k=k: (place[3][k][0], place[3][k][1], place[2])) for k in range(3)]
            plan.append(_Copy(key, _whole, key, slot, slot, lambda place: (place[0], place[1], 1 - place[2])))
            self.plans[key] = plan
        keys = list(blocks)
        first, a = keys[0], blocks[keys[0]]
        hr = a.shape[0] // 2
        mine = lambda ref, place, q: ref.at[q, pl.ds(pl.multiple_of(place[2] * hr, 16), hr)]
        theirs = lambda ref, place, q: ref.at[q, pl.ds(pl.multiple_of((1 - place[2]) * hr, 16), hr)]
        chip = lambda place, k: 2 * place[3][k][0] + place[3][k][1]
        sibling = lambda place: (place[0], place[1], 1 - place[2])
        self.plans[first] = [
            _Copy(first, lambda ref, place: ref.at[pl.ds(pl.multiple_of(place[2] * hr, 16), hr)], first,
                  lambda ref, place: mine(ref, place, _chip_of(place)),
                  lambda ref, place, k=k: mine(ref, place, chip(place, k)),
                  lambda place, k=k: (place[3][k][0], place[3][k][1], place[2])) for k in range(3)]
        self.plans[first].append(_Copy(first, _whole, first, lambda ref, place: ref.at[_chip_of(place)],
                                       lambda ref, place: ref.at[_chip_of(place)], sibling))
        self.forward = [_Copy(first, lambda ref, place, k=k: mine(ref, place, chip(place, k)), first,
                              lambda ref, place, k=k: mine(ref, place, chip(place, k)),
                              lambda ref, place, k=k: theirs(ref, place, chip(place, k)), sibling) for k in range(3)]
        self.blocks, self.shapes = blocks, shapes
        self.sems, self.srcs, self.lands = {}, {}, {}
        self._start("gather_start_first", keys[:GATHER_FIRST], ())
        self.rest = keys[GATHER_FIRST:]

    def _start(self, name, part, deps):
        sems, srcs, lands, self.token = _split_start(name, {k: self.blocks[k] for k in part},
                                                     {k: self.shapes[k] for k in part}, [self.plans[k] for k in part], deps)
        self.sems.update(zip(part, sems))
        self.srcs.update(srcs)
        self.lands.update(lands)

    def get(self, l, name, after):
        key = (l, name)
        _, lands = _split_wait(f"gather_wait_{name}{l}", self.sems[key], {key: self.srcs[key]},
                               {key: self.lands[key]}, self.plans[key], after)
        if self.rest:
            sems, bufs, _, _ = _split_start("gather_forward", {key: lands[key]}, {}, [self.forward])
            lands, _ = _split_wait("gather_forward_wait", sems[0], bufs, {}, self.forward, after)
            self._start("gather_start", self.rest, [lands[key]])
            self.rest = []
        return lands[key][:, None]


class _GradReduce:
    def __init__(self, place):
        self.place = place
        self.jobs = []
        self.done = {}
        self.n = 0

    def submit(self, grads):
        views = {k: a.reshape(N_CHIPS, 2, a.shape[1] // 2, a.shape[2]) for k, a in grads.items()}
        shapes = {k: jax.ShapeDtypeStruct((N_DEVICES,) + a.shape[2:], a.dtype) for k, a in views.items()}

        def peer(place, k):
            x, y, c, _ = place
            return (1 - x if k & 4 else x, 1 - y if k & 2 else y, 1 - c if k & 1 else c)

        def index(dev):
            return 4 * dev[0] + 2 * dev[1] + dev[2]

        plan = []
        for key in views:
            for k in range(1, N_DEVICES):
                plan.append(_Copy(
                    key, lambda ref, place, k=k: ref.at[2 * peer(place, k)[0] + peer(place, k)[1], peer(place, k)[2]],
                    key, lambda ref, place: ref.at[index(place[:3])],
                    lambda ref, place, k=k: ref.at[index(peer(place, k))],
                    lambda place, k=k: peer(place, k)))
        sems, srcs, lands, token = _split_start(f"grad_start{self.n}", views, shapes, [plan])
        self.jobs.append(dict(id=self.n, sems=sems[0], srcs=srcs, lands=lands, plan=plan))
        self.n += 1
        return token

    def pump(self, after):
        return []

    def finish(self, after):
        for job in self.jobs:
            srcs, lands = _split_wait(f"grad_wait{job['id']}", job["sems"], job["srcs"], job["lands"], job["plan"],
                                      after)
            self.done.update(_device_add(f"grad_add{job['id']}", srcs, lands, self.place))
        self.jobs = []
        return self.done


class _PairShare:
    def __init__(self, halves, types):
        sibling = lambda place: (place[0], place[1], 1 - place[2])
        mine = lambda ref, place: ref.at[place[2]]
        theirs = lambda ref, place: ref.at[1 - place[2]]
        self.plans = {t: [_Copy(k, mine, k, mine, theirs, sibling) for k in halves if k[0] == t] for t in types}
        sems, self.bufs, _, self.token = _split_start("share_start", halves, {}, list(self.plans.values()))
        self.sems = dict(zip(self.plans, sems))

    def get(self, t, after):
        keys = [cp.src for cp in self.plans[t]]
        bufs, _ = _split_wait(f"share_wait_{t}", self.sems[t], {k: self.bufs[k] for k in keys}, {}, self.plans[t], after)
        return bufs


def _small_allreduce(part):
    R, C = part.shape
    N_DEV = 8

    def body(in_ref, out_ref, slots, ssem, rsem):
        x, y, c, _ = _place()
        me = 4 * x + 2 * y + c
        sends = []
        for k in range(1, N_DEV):
            kx, ky, kc = (k >> 2) & 1, (k >> 1) & 1, k & 1
            peer = (1 - x if kx else x, 1 - y if ky else y, 1 - c if kc else c)
            cp = pltpu.make_async_remote_copy(
                src_ref=in_ref, dst_ref=slots.at[me], send_sem=ssem.at[k], recv_sem=rsem.at[k],
                device_id=peer, device_id_type=MESH)
            cp.start()
            sends.append(cp)
        slots[me] = in_ref[...]
        for k in range(1, N_DEV):
            kx, ky, kc = (k >> 2) & 1, (k >> 1) & 1, k & 1
            peer = (1 - x if kx else x, 1 - y if ky else y, 1 - c if kc else c)
            slot = slots.at[4 * peer[0] + 2 * peer[1] + peer[2]]
            pltpu.make_async_remote_copy(
                src_ref=slot, dst_ref=slot, send_sem=ssem.at[k], recv_sem=rsem.at[k],
                device_id=peer, device_id_type=MESH).wait_recv()
        acc = slots[0]
        for d in range(1, N_DEV):
            acc = acc + slots[d]
        out_ref[...] = acc
        for cp in sends:
            cp.wait_send()

    vm = pl.BlockSpec(memory_space=pltpu.VMEM)
    return pl.pallas_call(
        body, name="small_allreduce", in_specs=[vm], out_specs=vm,
        out_shape=jax.ShapeDtypeStruct((R, C), F32),
        scratch_shapes=[pltpu.VMEM((N_DEV, R, C), F32), pltpu.SemaphoreType.DMA((N_DEV,)),
                        pltpu.SemaphoreType.DMA((N_DEV,))])(part)


def _local_step(x, target, norm_mix, norm_mlp, norm_kv, norm_final, weights, sink, n_a, n_heads):
    B, S, D = x.shape
    T = B * S
    C = n_heads * HEAD_DIM
    depth = norm_mix.shape[0]
    slopes = 2.0 ** (-ALIBI_MAX_BIAS * jnp.arange(1, n_heads + 1, dtype=F32) / n_heads)
    tm = min(512, T)
    row = lambda v: v.reshape(1, -1)

    h = x.reshape(T, D)
    saved, Wl = [], []
    kv = nkv = h_kv = cwg = None
    for l in range(depth):
        s = {"h_in": h}
        w = {}
        Wl.append(w)
        if l < n_a:
            n_in = _rms_only("a_in_norm0", h, row(norm_mix[l]), tm) if l == 0 else h
            w["w_a_in"] = weights.get(l, "w_a_in", n_in)
            first = [weights.token] if l == 0 else []
            s["n1"], bcu = _norm_mm(f"a_in_fwd{l}", n_in, row(norm_mix[l]), w["w_a_in"], 0, 3, BF16, tm, first, l == 0)
            s["bcu"] = bcu.reshape(3, B, S, D)
            if l == 0:
                cwg = weights.get(0, "conv", bcu)[:, 0, :n_a * 3].reshape(N_CHIPS, n_a, 3, -1)
            s["z"] = _conv_fwd(f"conv_fwd{l}", s["bcu"], cwg, l, CONV_COLS).reshape(T, D)
            w["w_a_out"] = weights.get(l, "w_a_out", s["z"])
            h = _mm_res_rows(f"a_out_fwd{l}", s["z"], w["w_a_out"], 0, h, _to_bf16, tm)
        else:
            i = l - n_a
            if i == 0:
                h_kv = h
                w["w_kv"] = weights.get(l, "w_kv", h)
                nkv, kv = _norm_mm("kv_fwd", h, row(norm_kv), w["w_kv"], 0, 1, F32, tm)
                kv = kv.reshape(B, S, 2 * 3 * C)
            w["w_q"] = weights.get(l, "w_q", h)
            s["n1"], q = _norm_mm(f"q_fwd{i}", h, row(norm_mix[l]), w["w_q"], 0, 1, F32, tm)
            s["q"] = q.reshape(B, S, 3 * C)
            o, lse = _attn_fwd(f"attn_fwd{i}", s["q"], kv, slopes, n_heads)
            s["o"], s["lse"] = o.reshape(T, C), lse.reshape(T, C)
            w["w_o"] = weights.get(l, "w_o", o)
            h = _mm_res_cols(f"o_fwd{i}", s["o"], w["w_o"], 0, h, tm)
        s["h_mid"] = h
        w["w_up"] = weights.get(l, "w_up", h)
        if l < n_a:
            s["n2"], a = _norm_mm(f"up_fwd{l}", h, row(norm_mlp[l]), w["w_up"], 0, 1, BF16, tm)
            s["a"] = a[0]
            w["w_down"] = weights.get(l, "w_down", a)
            h = _mm_res_rows(f"down_fwd{l}", s["a"], w["w_down"], 0, h, _relu2_bf16, tm)
        else:
            w["w_down"] = weights.get(l, "w_down", h)
            s["n2"], s["a"], h = _mlp_fwd(f"mlp_fwd{l}", h, row(norm_mlp[l]), w["w_up"], w["w_down"], tm)
        F = s["a"].shape[1]
        saved.append(s)

    loss, dh, dh16, dg_final = _final_loss("loss_head", h, row(norm_final), target.reshape(T, D), tm)

    g_mix, g_mlp = [None] * depth, [None] * depth
    g_conv = [None] * n_a
    dkv = None
    tt = min(512, T)
    deps = []
    for l in reversed(range(depth)):
        s, w = saved[l], Wl[l]
        g_down = _tn(f"down_wgrad{l}", s["a"], _relu2_bf16, [_seg2d(dh16, tt, 2)], None, False,
                     min(2048, F), tt, deps, BF16).reshape(N_CHIPS, F // N_CHIPS, D)
        da, dh, dh16, g_mlp[l] = _mlp_bwd(f"mlp_bwd{l}", dh, dh16, s["a"], w["w_down"], w["w_up"], s["h_mid"],
                                          row(norm_mlp[l]), tm)
        g_up = _tn(f"up_wgrad{l}", s["n2"], _to_bf16, [_seg2d(da, tt, 2)], F // N_CHIPS, True, D, tt, (), BF16)
        deps = sink.pump(dh) + [sink.submit({("w_up", l): g_up, ("w_down", l): g_down})]
        if l < n_a:
            g_out = _tn(f"a_out_wgrad{l}", s["z"], _to_bf16, [_seg2d(dh16, tt, 2)], None, False,
                        D, tt, deps, BF16).reshape(N_CHIPS, D // N_CHIPS, D)
            dz = _nt_rows(f"a_out_bwd{l}", dh16, w["w_a_out"], 0, None, F32, tm)
            deps = sink.pump(dz) + [sink.submit({("w_a_out", l): g_out})]
            dbcu, g_conv[l] = _conv_bwd(f"conv_bwd{l}", s["bcu"], dz.reshape(B, S, D), cwg, l, CONV_COLS)
            dbcu = dbcu.reshape(3, T, D)
            g_in = _tn(f"a_in_wgrad{l}", s["n1"], _to_bf16, [_seg_plane(dbcu, p, tt, 2) for p in range(3)],
                       3 * D // N_CHIPS, True, D, tt, deps, BF16)
            deps = [sink.submit({("w_a_in", l): g_in})]
            dh, dh16, g_mix[l] = _nt_cols(f"a_in_bwd{l}", [_seg_plane(dbcu, p, tm, 1) for p in range(3)],
                                          w["w_a_in"], 0, tm, (s["h_in"], row(norm_mix[l]), dh), deps)
        else:
            i = l - n_a
            g_o = _tn(f"o_wgrad{i}", s["o"], _to_bf16, [_seg2d(dh16, tt, 2)], D // N_CHIPS, True, C, tt, deps,
                      BF16)
            do = _nt_cols(f"o_bwd{i}", [_seg2d(dh16, tm, 1)], w["w_o"], 0, tm, None)
            deps = sink.pump(do) + [sink.submit({("w_o", i): g_o})]
            dq, dk, dv = _attn_bwd(f"attn_bwd{i}", s["q"], kv, slopes, s["o"].reshape(B, S, C),
                                   s["lse"].reshape(B, S, C), do.reshape(B, S, C), n_heads, dkv)
            dkv = (dk, dv)
            dq = dq.reshape(T, 3 * C)
            g_q = _tn(f"q_wgrad{i}", s["n1"], _to_bf16, [_seg2d(dq, tt, 2)], 3 * C // N_CHIPS, True, D, tt, deps,
                      BF16)
            mixer = {("w_q", i): g_q}
            if i == 0:
                dk2, dv2 = (t.reshape(T, 3 * C) for t in dkv)
                mixer[("w_kv", 0)] = _tn("kv_wgrad", nkv, _to_bf16, _kv_segments(dk2, dv2, C, tt, 2),
                                         6 * C // N_CHIPS, True, D, tt, (), BF16)
            deps = [sink.submit(mixer)]
            dh, dh16, g_mix[l] = _nt_cols(f"q_bwd{i}", [_seg2d(dq, tm, 1)], w["w_q"], 0, tm,
                                          (s["h_in"], row(norm_mix[l]), dh), deps)
            if i == 0:
                dh, dh16, g_kv = _nt_cols("kv_bwd", _kv_segments(dk2, dv2, C, tm, 1), w["w_kv"], 0, tm,
                                          (h_kv, row(norm_kv), dh))
        deps = sink.pump(dh)
    small = dict(norm_mix=jnp.concatenate(g_mix, axis=0), norm_mlp=jnp.concatenate(g_mlp, axis=0),
                 norm_kv=g_kv, norm_final=dg_final, conv_w=jnp.stack(g_conv))
    return loss, dh.reshape(B, S, D), small


BIG = ("w_a_in", "w_a_out", "w_kv", "w_q", "w_o", "w_up", "w_down")
CONV_PAD_ROWS = 16


def kernel(x, norm_mix, norm_mlp, w_a_in, conv_w, w_a_out, norm_kv, w_kv, w_q, w_o, w_up, w_down, norm_final, loss_target, m_norm_mix, m_norm_mlp, m_w_a_in, m_conv_w, m_w_a_out, m_norm_kv, m_w_kv, m_w_q, m_w_o, m_w_up, m_w_down, m_norm_final, v_norm_mix, v_norm_mlp, v_w_a_in, v_conv_w, v_w_a_out, v_norm_kv, v_w_kv, v_w_q, v_w_o, v_w_up, v_w_down, v_norm_final):
    D = x.shape[-1]
    w = dict(norm_mix=norm_mix, norm_mlp=norm_mlp, w_a_in=w_a_in, conv_w=conv_w, w_a_out=w_a_out, norm_kv=norm_kv,
             w_kv=w_kv[None], w_q=w_q, w_o=w_o, w_up=w_up, w_down=w_down, norm_final=norm_final)
    m = dict(norm_mix=m_norm_mix, norm_mlp=m_norm_mlp, w_a_in=m_w_a_in, conv_w=m_conv_w, w_a_out=m_w_a_out,
             norm_kv=m_norm_kv, w_kv=m_w_kv[None], w_q=m_w_q, w_o=m_w_o, w_up=m_w_up, w_down=m_w_down,
             norm_final=m_norm_final)
    v = dict(norm_mix=v_norm_mix, norm_mlp=v_norm_mlp, w_a_in=v_w_a_in, conv_w=v_conv_w, w_a_out=v_w_a_out,
             norm_kv=v_norm_kv, w_kv=v_w_kv[None], w_q=v_w_q, w_o=v_w_o, w_up=v_w_up, w_down=v_w_down,
             norm_final=v_norm_final)
    depth = norm_mix.shape[0]
    n_a, taps, cwc = conv_w.shape
    n_heads = w_o.shape[1] // HEAD_DIM

    conv_rows = jnp.zeros((CONV_PAD_ROWS, cwc), F32).at[:n_a * taps].set(conv_w.reshape(n_a * taps, cwc))
    blocks = {}
    for l in range(depth):
        if l < n_a:
            blocks[(l, "w_a_in")] = w_a_in[l].astype(BF16)
            if l == 0:
                blocks[(0, "conv")] = conv_rows
            blocks[(l, "w_a_out")] = w_a_out[l].astype(BF16)
        else:
            if l == n_a:
                blocks[(l, "w_kv")] = w_kv.astype(BF16)
            blocks[(l, "w_q")] = w_q[l - n_a].astype(BF16)
            blocks[(l, "w_o")] = w_o[l - n_a].astype(BF16)
        blocks[(l, "w_up")] = w_up[l].astype(BF16)
        blocks[(l, "w_down")] = w_down[l].astype(BF16)
    weights = _WeightGather(blocks)
    place = jnp.stack([2 * lax.axis_index("x") + lax.axis_index("y"), lax.axis_index("c")]).astype(jnp.int32)
    sink = _GradReduce(place)

    loss, grad_x, small = _local_step(x, loss_target, norm_mix, norm_mlp, norm_kv, norm_final, weights, sink,
                                      n_a, n_heads)
    loss = lax.psum(loss[0, 0], ("x", "y", "c"))

    share = _PairShare(sink.finish(grad_x), BIG)
    grads = {}

    packed = jnp.concatenate([small["norm_mix"], small["norm_mlp"], small["norm_kv"], small["norm_final"],
                              small["conv_w"].reshape(n_a * taps, D)], axis=0)
    pad = (-packed.shape[0]) % 8
    packed = jnp.pad(packed, ((0, pad), (0, 0)))
    total = _small_allreduce(packed)
    grads["norm_mix"] = total[:depth]
    grads["norm_mlp"] = total[depth:2 * depth]
    grads["norm_kv"] = total[2 * depth]
    grads["norm_final"] = total[2 * depth + 1]
    chip = 2 * lax.axis_index("x") + lax.axis_index("y")
    conv_full = total[2 * depth + 2:2 * depth + 2 + n_a * taps].reshape(n_a, taps, N_CHIPS, cwc)
    grads["conv_w"] = lax.dynamic_index_in_dim(conv_full, chip, axis=2, keepdims=False)

    order = ("norm_mix", "norm_mlp", "w_a_in", "conv_w", "w_a_out", "norm_kv", "w_kv", "w_q", "w_o", "w_up",
             "w_down", "norm_final")
    delta, new_m, new_v = {}, {}, {}
    vec_names = ("norm_mix", "norm_mlp", "norm_kv", "norm_final")
    rows_of = lambda a: a.reshape(-1, D)
    vw, vg, vm_, vv = (jnp.concatenate([rows_of(t[k]) for k in vec_names], axis=0) for t in (w, grads, m, v))
    vpad = (-vw.shape[0]) % 8
    padrows = lambda a: jnp.pad(a, ((0, vpad), (0, 0)))
    vd, vnm, vnv = _adamw("adamw_norms", padrows(vw), padrows(vg), padrows(vm_), padrows(vv))
    off = 0
    for k in vec_names:
        r = rows_of(w[k]).shape[0]
        delta[k] = vd[off:off + r].reshape(w[k].shape)
        new_m[k] = vnm[off:off + r].reshape(w[k].shape)
        new_v[k] = vnv[off:off + r].reshape(w[k].shape)
        off += r
    cpad = (-n_a * taps) % 8
    two_d = lambda a: jnp.pad(a.reshape(-1, cwc), ((0, cpad), (0, 0)))
    cd, cnm, cnv = _adamw("adamw_conv_w", two_d(w["conv_w"]), two_d(grads["conv_w"]), two_d(m["conv_w"]),
                          two_d(v["conv_w"]))
    delta["conv_w"], new_m["conv_w"], new_v["conv_w"] = (t[:n_a * taps].reshape(conv_w.shape) for t in (cd, cnm, cnv))
    after = cd
    for k in sorted(BIG, key=lambda k: w[k].size):
        shared = share.get(k, after)
        per_layer = [shared[(k, l)].reshape(w[k].shape[1:]) for l in range(w[k].shape[0])]
        grads[k], delta[k], new_m[k], new_v[k] = _adamw_layers(f"adamw_{k}", w[k], per_layer, m[k], v[k])
        after = delta[k]
    fix = lambda k, a: a[0] if k == "w_kv" else a
    return (loss, grad_x, *[fix(k, grads[k]) for k in order], *[fix(k, delta[k]) for k in order],
            *[fix(k, new_m[k]) for k in order], *[fix(k, new_v[k]) for k in order])
```
